```python
import math
import jax, jax.numpy as jnp
from jax import lax
import numpy as np

D_MODEL = 1024
BATCH = 8
SEQ = 4096
DEPTH = 1
DEC_BATCH = 32
DEC_SEQ = 64
PAST_LEN = 1024

CHUNK = 64
D_MIX = D_MODEL
POOL_WIDTH = D_MIX // 2
POOL_WINDOWS = (2, 4, 8, 16)
POOL_GROUPS = len(POOL_WINDOWS)
POOL_GROUP = POOL_WIDTH // POOL_GROUPS
POOL_HIST = max(POOL_WINDOWS) - 1
GLA_HEADS = 4
GLA_DV = (D_MIX - POOL_WIDTH) // GLA_HEADS
GLA_DK = GLA_DV // 2
GATE_RANK = 16
GATE_NORMALIZER = 16.0
N_IN = POOL_WIDTH + 2 * GLA_HEADS * GLA_DK + 2 * GLA_HEADS * GLA_DV + GATE_RANK
SPLITS = (POOL_WIDTH,
          POOL_WIDTH + GLA_HEADS * GLA_DK,
          POOL_WIDTH + 2 * GLA_HEADS * GLA_DK,
          POOL_WIDTH + 2 * GLA_HEADS * GLA_DK + GLA_HEADS * GLA_DV,
          POOL_WIDTH + 2 * GLA_HEADS * GLA_DK + 2 * GLA_HEADS * GLA_DV)
NUM_EXPERTS = 32
TOP_K = 4
EXPERT_FF = D_MODEL
SWIGLU_LIMIT = 7.0
SWIGLU_ALPHA = 1.702
MOE_BLOCK = 128
LN_EPS = 1e-5
RMS_EPS = 1e-6

kernel_name = "hybrid_pool_gla_moe_stream_step"


def _layer_norm(x, g, b):
    x32 = x.astype(jnp.float32)
    mu = jnp.mean(x32, axis=-1, keepdims=True)
    var = jnp.mean(jnp.square(x32 - mu), axis=-1, keepdims=True)
    return ((x32 - mu) * lax.rsqrt(var + LN_EPS) * g + b).astype(x.dtype)


def _pool_mix(u, hist, pos0, w_pool, pool_scale):
    B, L, _ = u.shape
    ext = jnp.concatenate([hist.astype(u.dtype), u], axis=1)
    ext32 = ext.astype(jnp.float32)
    cs = jnp.concatenate([jnp.zeros((B, 1, POOL_WIDTH), jnp.float32),
                          jnp.cumsum(ext32, axis=1)], axis=1)
    pos = pos0 + jnp.arange(L)
    s0 = POOL_HIST + 1
    means = []
    for gi, w in enumerate(POOL_WINDOWS):
        sl = slice(gi * POOL_GROUP, (gi + 1) * POOL_GROUP)
        s = cs[:, s0:s0 + L, sl] - cs[:, s0 - w:s0 - w + L, sl]
        cnt = jnp.minimum(pos + 1, w).astype(jnp.float32)[None, :, None]
        means.append(s / cnt)
    pooled = jnp.concatenate(means, axis=-1) - u.astype(jnp.float32)
    z = jnp.einsum('blgc,gcd->blgd', pooled.reshape(B, L, POOL_GROUPS, POOL_GROUP),
                   w_pool.astype(jnp.float32)).reshape(B, L, POOL_WIDTH)
    out = (z * pool_scale.astype(jnp.float32)).astype(u.dtype)
    return out, ext[:, -POOL_HIST:].astype(hist.dtype)


def _gla(q, k, v, g, s0):
    B, L, H, dk = q.shape
    dv = v.shape[-1]
    C = CHUNK if L % CHUNK == 0 else L
    n = L // C

    def to_chunks(a):
        return a.astype(jnp.float32).reshape(B, n, C, H, a.shape[-1]).transpose(1, 0, 3, 2, 4)

    mask = jnp.tril(jnp.ones((C, C), dtype=bool))[:, :, None]

    def step(S, inp):
        qc, kc, vc, gc = inp
        b = jnp.cumsum(gc, axis=2)
        diff = b[:, :, :, None, :] - b[:, :, None, :, :]
        decay = jnp.where(mask, jnp.exp(jnp.minimum(diff, 0.0)), 0.0)
        att = jnp.einsum('bhid,bhjd,bhijd->bhij', qc, kc, decay)
        o = jnp.einsum('bhij,bhje->bhie', att, vc) + \
            jnp.einsum('bhid,bhde->bhie', qc * jnp.exp(b), S)
        bl = b[:, :, -1]
        S_new = jnp.exp(bl)[..., None] * S + \
            jnp.einsum('bhjd,bhje->bhde', kc * jnp.exp(bl[:, :, None, :] - b), vc)
        return S_new, o

    S_fin, o = lax.scan(step, s0.astype(jnp.float32),
                        (to_chunks(q), to_chunks(k), to_chunks(v), to_chunks(g)))
    o = o.transpose(1, 0, 3, 2, 4).reshape(B, L, H, dv)
    return o, S_fin.astype(s0.dtype)


def _moe(x2, w_router, b_router, w_gu, b_gu, w_down, b_down):
    N, D = x2.shape
    logits = jnp.dot(x2, w_router).astype(jnp.float32) + b_router.astype(jnp.float32)
    top_val, top_idx = lax.top_k(logits, TOP_K)
    gates = jax.nn.softmax(top_val, axis=-1)
    NK = N * TOP_K
    flat_e = top_idx.reshape(-1)
    flat_tok = jnp.repeat(jnp.arange(N, dtype=jnp.int32), TOP_K)
    order = jnp.argsort(flat_e)
    sorted_e = flat_e[order]
    sorted_tok = flat_tok[order]
    sorted_gate = gates.reshape(-1)[order]
    counts = jnp.bincount(flat_e, length=NUM_EXPERTS)
    padded = ((counts + MOE_BLOCK - 1) // MOE_BLOCK) * MOE_BLOCK
    pad_end = jnp.cumsum(padded)
    pad_start = pad_end - padded
    start = jnp.cumsum(counts) - counts
    dest = pad_start[sorted_e] + (jnp.arange(NK) - start[sorted_e])
    n_blocks = -(-NK // MOE_BLOCK) + NUM_EXPERTS
    m_pad = n_blocks * MOE_BLOCK
    row_tok = jnp.zeros((m_pad,), jnp.int32).at[dest].set(sorted_tok)
    block_start = jnp.arange(n_blocks) * MOE_BLOCK
    block_expert = jnp.minimum(jnp.searchsorted(pad_end, block_start, side='right'),
                               NUM_EXPERTS - 1)

    def block_fn(args):
        e, rows = args
        xb = x2[rows]
        gu = jnp.dot(xb, w_gu[e]) + b_gu[e]
        gate, up = gu[:, :EXPERT_FF], gu[:, EXPERT_FF:]
        gate = jnp.minimum(gate, SWIGLU_LIMIT)
        up = jnp.clip(up, -SWIGLU_LIMIT, SWIGLU_LIMIT)
        hmid = gate * jax.nn.sigmoid(SWIGLU_ALPHA * gate) * (up + 1.0)
        return jnp.dot(hmid, w_down[e]) + b_down[e]

    out_blocks = lax.map(block_fn, (block_expert, row_tok.reshape(n_blocks, MOE_BLOCK)))
    y_rows = out_blocks.reshape(m_pad, D)[dest]
    y = jax.ops.segment_sum(y_rows.astype(jnp.float32) * sorted_gate[:, None],
                            sorted_tok, num_segments=N)
    return y.astype(x2.dtype)


def _layer(x, hist, s0, pos0, w_in, w_pool, pool_scale, w_gate_up, b_gate, gla_norm_w,
           w_out, ln1_g, ln1_b, w_router, b_router, w_gu, b_gu, w_down, b_down, ln2_g, ln2_b):
    B, L, _ = x.shape
    alpha = (2.0 * DEPTH) ** 0.25
    proj = jnp.einsum('bld,de->ble', x, w_in)
    u, q, k, v, r, glr = jnp.split(proj, SPLITS, axis=-1)
    pool_out, new_hist = _pool_mix(u, hist, pos0, w_pool, pool_scale)
    qh = q.reshape(B, L, GLA_HEADS, GLA_DK) * (GLA_DK ** -0.5)
    kh = k.reshape(B, L, GLA_HEADS, GLA_DK)
    vh = v.reshape(B, L, GLA_HEADS, GLA_DV)
    gk = (jnp.einsum('blr,re->ble', glr, w_gate_up) + b_gate).astype(jnp.float32)
    g = (jax.nn.log_sigmoid(gk) / GATE_NORMALIZER).reshape(B, L, GLA_HEADS, GLA_DK)
    o, new_s = _gla(qh, kh, vh, g, s0)
    o = o * lax.rsqrt(jnp.mean(jnp.square(o), axis=-1, keepdims=True) + RMS_EPS) * gla_norm_w
    o = (o.reshape(B, L, GLA_HEADS * GLA_DV) * jax.nn.silu(r.astype(jnp.float32))).astype(x.dtype)
    mix = jnp.einsum('ble,ed->bld', jnp.concatenate([pool_out, o], axis=-1), w_out)
    h = _layer_norm(alpha * x + mix, ln1_g, ln1_b)
    moe_out = _moe(h.reshape(B * L, D_MODEL), w_router, b_router, w_gu, b_gu,
                   w_down, b_down).reshape(B, L, D_MODEL)
    y = _layer_norm(alpha * h + moe_out, ln2_g, ln2_b)
    return y, new_hist, new_s


def setup_inputs(seed: int = 0) -> dict:
    key = jax.random.key(seed)
    ks = jax.random.split(key, 24)
    beta = (8.0 * DEPTH) ** -0.25
    nrm = lambda kk, shape, s: jax.random.normal(kk, shape, jnp.float32) * s
    return {
        "x_prompt": nrm(ks[0], (BATCH, SEQ, D_MODEL), 1.0),
        "x_sample": nrm(ks[1], (DEC_BATCH, DEC_SEQ, D_MODEL), 1.0),
        "state_pool": nrm(ks[2], (DEPTH, DEC_BATCH, POOL_HIST, POOL_WIDTH), 1.0),
        "state_gla": nrm(ks[3], (DEPTH, DEC_BATCH, GLA_HEADS, GLA_DK, GLA_DV), 1.0),
        "w_in": nrm(ks[4], (DEPTH, D_MODEL, N_IN), D_MODEL ** -0.5),
        "w_pool": nrm(ks[5], (DEPTH, POOL_GROUPS, POOL_GROUP, POOL_GROUP), POOL_GROUP ** -0.5),
        "pool_scale": 1.0 + nrm(ks[6], (DEPTH, POOL_WIDTH), 0.02),
        "w_gate_up": nrm(ks[7], (DEPTH, GATE_RANK, GLA_HEADS * GLA_DK), GATE_RANK ** -0.5),
        "b_gate": nrm(ks[8], (DEPTH, GLA_HEADS * GLA_DK), 0.01),
        "gla_norm_w": 1.0 + nrm(ks[9], (DEPTH, GLA_DV), 0.02),
        "w_out": nrm(ks[10], (DEPTH, D_MIX, D_MODEL), beta * D_MIX ** -0.5),
        "ln1_g": 1.0 + nrm(ks[11], (DEPTH, D_MODEL), 0.02),
        "ln1_b": nrm(ks[12], (DEPTH, D_MODEL), 0.01),
        "w_router": nrm(ks[13], (DEPTH, D_MODEL, NUM_EXPERTS), D_MODEL ** -0.5),
        "b_router": nrm(ks[14], (DEPTH, NUM_EXPERTS), 0.01),
        "w_gu": nrm(ks[15], (DEPTH, NUM_EXPERTS, D_MODEL, 2 * EXPERT_FF), D_MODEL ** -0.5),
        "b_gu": nrm(ks[16], (DEPTH, NUM_EXPERTS, 2 * EXPERT_FF), 0.01),
        "w_down": nrm(ks[17], (DEPTH, NUM_EXPERTS, EXPERT_FF, D_MODEL), beta * EXPERT_FF ** -0.5),
        "b_down": nrm(ks[18], (DEPTH, NUM_EXPERTS, D_MODEL), 0.01),
        "ln2_g": 1.0 + nrm(ks[19], (DEPTH, D_MODEL), 0.02),
        "ln2_b": nrm(ks[20], (DEPTH, D_MODEL), 0.01),
    }


def reference(x_prompt, x_sample, state_pool, state_gla, w_in, w_pool, pool_scale, w_gate_up,
              b_gate, gla_norm_w, w_out, ln1_g, ln1_b, w_router, b_router, w_gu, b_gu,
              w_down, b_down, ln2_g, ln2_b):
    xp, xs = x_prompt, x_sample
    bp = x_prompt.shape[0]
    pool_p, gla_p, pool_s, gla_s = [], [], [], []
    for l in range(DEPTH):
        prm = (w_in[l], w_pool[l], pool_scale[l], w_gate_up[l], b_gate[l], gla_norm_w[l],
               w_out[l], ln1_g[l], ln1_b[l], w_router[l], b_router[l], w_gu[l], b_gu[l],
               w_down[l], b_down[l], ln2_g[l], ln2_b[l])
        hist0 = jnp.zeros((bp, POOL_HIST, POOL_WIDTH), state_pool.dtype)
        s0 = jnp.zeros((bp, GLA_HEADS, GLA_DK, GLA_DV), state_gla.dtype)
        xp, hp, sp = _layer(xp, hist0, s0, 0, *prm)
        xs, hs, ss = _layer(xs, state_pool[l], state_gla[l], PAST_LEN, *prm)
        pool_p.append(hp)
        gla_p.append(sp)
        pool_s.append(hs)
        gla_s.append(ss)
    y_prompt, y_sample = xp, xs
    new_state_pool_prompt = jnp.stack(pool_p)
    new_state_gla_prompt = jnp.stack(gla_p)
    new_state_pool_sample = jnp.stack(pool_s)
    new_state_gla_sample = jnp.stack(gla_s)
    return (y_prompt, y_sample, new_state_pool_prompt, new_state_gla_prompt,
            new_state_pool_sample, new_state_gla_sample)
```

```python
import functools

import jax
import jax.numpy as jnp
from jax import lax
from jax.experimental import pallas as pl
from jax.experimental.pallas import tpu as pltpu

F32 = jnp.float32
BF16 = jnp.bfloat16

D_MODEL = 1024
CHUNK = 64
POOL_WIDTH = 512
POOL_WINDOWS = (2, 4, 8, 16)
POOL_GROUP = 128
POOL_HIST = 15
GLA_HEADS = 4
GLA_DK = 64
GLA_DV = 128
GATE_RANK = 16
GATE_NORMALIZER = 16.0
NUM_EXPERTS = 32
TOP_K = 4
EXPERT_FF = 1024
SWIGLU_LIMIT = 7.0
SWIGLU_ALPHA = 1.702
LN_EPS = 1e-5
RMS_EPS = 1e-6
ALPHA = 2.0 ** 0.25

Q0 = POOL_WIDTH
K0 = Q0 + GLA_HEADS * GLA_DK
V0 = K0 + GLA_HEADS * GLA_DK
R0 = V0 + GLA_HEADS * GLA_DV
N_MAIN = R0 + GLA_HEADS * GLA_DV

LANES = 128
TILE_TOKENS = 512
CHUNKS_PER_TILE = TILE_TOKENS // CHUNK
HIST_PAD = 16
MOE_ROWS = 256
COMBINE_TOKENS = 256
VMEM_LIMIT = 56 * 1024 * 1024


def _dot(a, b):
    return jnp.dot(a, b, preferred_element_type=F32)


def _dot_nt(a, b):
    return lax.dot_general(a, b, (((1,), (1,)), ((), ())), preferred_element_type=F32)


def _dot_tn(a, b):
    return lax.dot_general(a, b, (((0,), (0,)), ((), ())), preferred_element_type=F32)


def _layer_norm(v, g, b):
    mu = jnp.mean(v, axis=-1, keepdims=True)
    c = v - mu
    var = jnp.mean(c * c, axis=-1, keepdims=True)
    return c * lax.rsqrt(var + LN_EPS) * g + b


def _mixer_kernel(per_chunk_state, pos0, *refs):
    if per_chunk_state:
        (x_ref, hist_in_ref, s_in_ref, *rest) = refs
    else:
        (x_ref, *rest) = refs
        hist_in_ref = s_in_ref = None
    (w_main_ref, w_glr_ref, w_gate_ref, b_gate_ref, w_pool_ref, pscale_ref, gnorm_ref,
     w_out_ref, ln1g_ref, ln1b_ref, w_router_ref, b_router_ref, *rest) = rest
    if per_chunk_state:
        rest = rest[2:]
    (h_ref, logits_ref, hist_out_ref, s_out_ref,
     proj_scr, g_scr, o_scr, ext_scr, st_scr) = rest

    t = pl.program_id(1) if not per_chunk_state else None
    x = x_ref[...].reshape(TILE_TOKENS, D_MODEL)
    xb = x.astype(BF16)

    proj_scr[...] = _dot(xb, w_main_ref[...])
    glr = _dot(xb, w_glr_ref[...])
    gk = _dot(glr.astype(BF16), w_gate_ref[...]) + b_gate_ref[...]
    log_sig = jnp.minimum(gk, 0.0) - jnp.log1p(jnp.exp(-jnp.abs(gk)))
    g_scr[...] = log_sig / GATE_NORMALIZER

    if per_chunk_state:
        seg_len, seg_stride, n_seg = CHUNK, CHUNK + HIST_PAD, CHUNKS_PER_TILE
        for c in range(n_seg):
            base = c * seg_stride
            ext_scr[base:base + HIST_PAD, :] = jnp.zeros((HIST_PAD, POOL_WIDTH), F32)
            ext_scr[base + 1:base + HIST_PAD, :] = hist_in_ref[c]
            ext_scr[base + HIST_PAD:base + seg_stride, :] = proj_scr[c * CHUNK:(c + 1) * CHUNK, 0:POOL_WIDTH]
        row_pos = pos0 + lax.broadcasted_iota(jnp.int32, (seg_len, POOL_GROUP), 0)
    else:
        seg_len, seg_stride, n_seg = TILE_TOKENS, TILE_TOKENS + HIST_PAD, 1

        @pl.when(t == 0)
        def _():
            ext_scr[0:HIST_PAD, :] = jnp.zeros((HIST_PAD, POOL_WIDTH), F32)

        ext_scr[HIST_PAD:seg_stride, :] = proj_scr[:, 0:POOL_WIDTH]
        row_pos = pos0 + t * TILE_TOKENS + lax.broadcasted_iota(jnp.int32, (seg_len, POOL_GROUP), 0)

    pool_cols = []
    for gi, w in enumerate(POOL_WINDOWS):
        gs = slice(gi * POOL_GROUP, (gi + 1) * POOL_GROUP)
        cnt = jnp.minimum(row_pos + 1, w).astype(F32)
        segs = []
        for s in range(n_seg):
            base = s * seg_stride + HIST_PAD
            acc = ext_scr[base:base + seg_len, gs]
            cur = acc
            for k in range(1, w):
                acc = acc + ext_scr[base - k:base - k + seg_len, gs]
            segs.append(acc / cnt - cur)
        pooled = segs[0] if n_seg == 1 else jnp.concatenate(segs, axis=0)
        z = _dot(pooled.astype(BF16), w_pool_ref[gi])
        pool_cols.append(z * pscale_ref[:, gs])
    pool_out = jnp.concatenate(pool_cols, axis=1)

    if per_chunk_state:
        for c in range(n_seg):
            end = (c + 1) * seg_stride
            hist_out_ref[c] = ext_scr[end - POOL_HIST:end, :]
    else:
        @pl.when(t == pl.num_programs(1) - 1)
        def _():
            hist_out_ref[...] = ext_scr[seg_stride - POOL_HIST:seg_stride, :]

        ext_scr[0:HIST_PAD, :] = ext_scr[TILE_TOKENS:seg_stride, :]

    if not per_chunk_state:
        @pl.when(t == 0)
        def _():
            st_scr[...] = jnp.zeros_like(st_scr)

    ii = lax.broadcasted_iota(jnp.int32, (CHUNK, CHUNK), 0)
    jj = lax.broadcasted_iota(jnp.int32, (CHUNK, CHUNK), 1)
    causal = ii >= jj
    tri = jnp.where(causal, 1.0, 0.0).astype(BF16)

    def chunk_body(c, carry):
        r0 = pl.multiple_of(c * CHUNK, CHUNK)
        rows = pl.ds(r0, CHUNK)
        gc = g_scr[rows, :]
        g1 = gc.astype(BF16)
        e1 = gc - g1.astype(F32)
        g2 = e1.astype(BF16)
        g3 = (e1 - g2.astype(F32)).astype(BF16)
        b = _dot(tri, g1) + _dot(tri, g2) + _dot(tri, g3)
        b_last = b[CHUNK - 1:CHUNK, :]
        q = proj_scr[rows, Q0:K0] * (GLA_DK ** -0.5)
        k = proj_scr[rows, K0:V0]
        qt = (q * jnp.exp(b)).astype(BF16)
        kt = (k * jnp.exp(-b)).astype(BF16)
        kl = (k * jnp.exp(b_last - b)).astype(BF16)
        e_last = jnp.exp(b_last)
        v = proj_scr[rows, V0:R0].astype(BF16)
        if per_chunk_state:
            st = jnp.transpose(s_in_ref[c].reshape(GLA_HEADS * GLA_DK, GLA_DV))
        else:
            st = st_scr[...]
        new_cols = []
        for h in range(GLA_HEADS):
            ks = slice(h * GLA_DK, (h + 1) * GLA_DK)
            vs = slice(h * GLA_DV, (h + 1) * GLA_DV)
            att = jnp.where(causal, _dot_nt(qt[:, ks], kt[:, ks]), 0.0)
            st_h = st[:, ks]
            o = _dot(att.astype(BF16), v[:, vs]) + _dot_nt(qt[:, ks], st_h.astype(BF16))
            o_scr[rows, vs] = o
            new_cols.append(st_h * e_last[:, ks] + _dot_tn(v[:, vs], kl[:, ks]))
        st_new = jnp.concatenate(new_cols, axis=1)
        if per_chunk_state:
            s_out_ref[c] = jnp.transpose(st_new).reshape(GLA_HEADS, GLA_DK, GLA_DV)
        else:
            st_scr[...] = st_new
        return carry

    lax.fori_loop(0, CHUNKS_PER_TILE, chunk_body, 0)

    if not per_chunk_state:
        @pl.when(t == pl.num_programs(1) - 1)
        def _():
            s_out_ref[...] = jnp.transpose(st_scr[...]).reshape(GLA_HEADS, GLA_DK, GLA_DV)

    r = proj_scr[:, R0:N_MAIN]
    silu_r = r * (1.0 / (1.0 + jnp.exp(-r)))
    gated = []
    for h in range(GLA_HEADS):
        vs = slice(h * GLA_DV, (h + 1) * GLA_DV)
        oh = o_scr[:, vs]
        ms = jnp.mean(oh * oh, axis=-1, keepdims=True)
        gated.append(oh * lax.rsqrt(ms + RMS_EPS) * gnorm_ref[...] * silu_r[:, vs])
    mix_in = jnp.concatenate([pool_out] + gated, axis=1).astype(BF16)
    resid = ALPHA * x + _dot(mix_in, w_out_ref[...])
    h_val = _layer_norm(resid, ln1g_ref[...], ln1b_ref[...])
    h_ref[...] = h_val
    logits_ref[...] = _dot(h_val.astype(BF16), w_router_ref[...]) + b_router_ref[...]


def _const_spec(shape):
    nd = len(shape)
    return pl.BlockSpec(shape, lambda *_: (0,) * nd)


def _mixer_weight_specs():
    return [
        _const_spec((D_MODEL, N_MAIN)),
        _const_spec((D_MODEL, LANES)),
        _const_spec((LANES, GLA_HEADS * GLA_DK)),
        _const_spec((1, GLA_HEADS * GLA_DK)),
        _const_spec((len(POOL_WINDOWS), POOL_GROUP, POOL_GROUP)),
        _const_spec((1, POOL_WIDTH)),
        _const_spec((1, GLA_DV)),
        _const_spec((D_MODEL, D_MODEL)),
        _const_spec((1, D_MODEL)),
        _const_spec((1, D_MODEL)),
        _const_spec((D_MODEL, LANES)),
        _const_spec((1, LANES)),
    ]


def _mixer_scratch(per_chunk_state):
    ext_rows = (CHUNKS_PER_TILE * (CHUNK + HIST_PAD)) if per_chunk_state else (TILE_TOKENS + HIST_PAD)
    return [
        pltpu.VMEM((TILE_TOKENS, N_MAIN), F32),
        pltpu.VMEM((TILE_TOKENS, GLA_HEADS * GLA_DK), F32),
        pltpu.VMEM((TILE_TOKENS, GLA_HEADS * GLA_DV), F32),
        pltpu.VMEM((ext_rows, POOL_WIDTH), F32),
        pltpu.VMEM((GLA_DV, GLA_HEADS * GLA_DK), F32),
    ]


def _mixer_prompt(x, weights, n_total):
    bsz, seq, _ = x.shape
    tiles = seq // TILE_TOKENS
    out_shape = (
        jax.ShapeDtypeStruct((n_total, D_MODEL), F32),
        jax.ShapeDtypeStruct((n_total, LANES), F32),
        jax.ShapeDtypeStruct((bsz, POOL_HIST, POOL_WIDTH), F32),
        jax.ShapeDtypeStruct((bsz, GLA_HEADS, GLA_DK, GLA_DV), F32),
    )
    return pl.pallas_call(
        functools.partial(_mixer_kernel, False, 0),
        grid=(bsz, tiles),
        in_specs=[pl.BlockSpec((None, TILE_TOKENS, D_MODEL), lambda b, t: (b, t, 0))] + _mixer_weight_specs(),
        out_specs=(
            pl.BlockSpec((TILE_TOKENS, D_MODEL), lambda b, t: (b * tiles + t, 0)),
            pl.BlockSpec((TILE_TOKENS, LANES), lambda b, t: (b * tiles + t, 0)),
            pl.BlockSpec((None, POOL_HIST, POOL_WIDTH), lambda b, t: (b, 0, 0)),
            pl.BlockSpec((None, GLA_HEADS, GLA_DK, GLA_DV), lambda b, t: (b, 0, 0, 0)),
        ),
        out_shape=out_shape,
        scratch_shapes=_mixer_scratch(False),
        compiler_params=pltpu.CompilerParams(
            dimension_semantics=("arbitrary", "arbitrary"), vmem_limit_bytes=VMEM_LIMIT),
        name="mixer_prompt",
    )(x, *weights)


def _mixer_sample(x, hist, state, weights, h_all, logits_all, past_len, row_offset):
    bsz = x.shape[0]
    tiles = bsz // CHUNKS_PER_TILE
    tile0 = row_offset // TILE_TOKENS
    n_total = h_all.shape[0]
    out_shape = (
        jax.ShapeDtypeStruct((n_total, D_MODEL), F32),
        jax.ShapeDtypeStruct((n_total, LANES), F32),
        jax.ShapeDtypeStruct((bsz, POOL_HIST, POOL_WIDTH), F32),
        jax.ShapeDtypeStruct((bsz, GLA_HEADS, GLA_DK, GLA_DV), F32),
    )
    n_in = 3 + len(weights)
    return pl.pallas_call(
        functools.partial(_mixer_kernel, True, past_len),
        grid=(tiles,),
        in_specs=[
            pl.BlockSpec((CHUNKS_PER_TILE, CHUNK, D_MODEL), lambda i: (i, 0, 0)),
            pl.BlockSpec((CHUNKS_PER_TILE, POOL_HIST, POOL_WIDTH), lambda i: (i, 0, 0)),
            pl.BlockSpec((CHUNKS_PER_TILE, GLA_HEADS, GLA_DK, GLA_DV), lambda i: (i, 0, 0, 0)),
        ] + _mixer_weight_specs() + [
            pl.BlockSpec(memory_space=pl.ANY),
            pl.BlockSpec(memory_space=pl.ANY),
        ],
        out_specs=(
            pl.BlockSpec((TILE_TOKENS, D_MODEL), lambda i: (tile0 + i, 0)),
            pl.BlockSpec((TILE_TOKENS, LANES), lambda i: (tile0 + i, 0)),
            pl.BlockSpec((CHUNKS_PER_TILE, POOL_HIST, POOL_WIDTH), lambda i: (i, 0, 0)),
            pl.BlockSpec((CHUNKS_PER_TILE, GLA_HEADS, GLA_DK, GLA_DV), lambda i: (i, 0, 0, 0)),
        ),
        out_shape=out_shape,
        input_output_aliases={n_in: 0, n_in + 1: 1},
        scratch_shapes=_mixer_scratch(True),
        compiler_params=pltpu.CompilerParams(
            dimension_semantics=("arbitrary",), vmem_limit_bytes=VMEM_LIMIT),
        name="mixer_sample",
    )(x, hist, state, *weights, h_all, logits_all)


def _moe_kernel(be_ref, nused_ref, x_ref, wgu_ref, bgu_ref, wd_ref, bd_ref, y_ref):
    del be_ref

    @pl.when(pl.program_id(0) < nused_ref[0])
    def _():
        gu = _dot(x_ref[...], wgu_ref[...]) + bgu_ref[...]
        gate = jnp.minimum(gu[:, :EXPERT_FF], SWIGLU_LIMIT)
        up = jnp.clip(gu[:, EXPERT_FF:], -SWIGLU_LIMIT, SWIGLU_LIMIT)
        hmid = gate * (1.0 / (1.0 + jnp.exp(-SWIGLU_ALPHA * gate))) * (up + 1.0)
        y_ref[...] = _dot(hmid.astype(BF16), wd_ref[...]) + bd_ref[...]


def _moe_experts(block_expert, n_used, x_sorted, w_gu, b_gu, w_down, b_down):
    m_pad = x_sorted.shape[0]
    n_blocks = m_pad // MOE_ROWS

    def blk(i, be, nu):
        return jnp.minimum(i, nu[0] - 1)

    grid_spec = pltpu.PrefetchScalarGridSpec(
        num_scalar_prefetch=2,
        grid=(n_blocks,),
        in_specs=[
            pl.BlockSpec((MOE_ROWS, D_MODEL), lambda i, be, nu: (blk(i, be, nu), 0)),
            pl.BlockSpec((None, D_MODEL, 2 * EXPERT_FF), lambda i, be, nu: (be[blk(i, be, nu)], 0, 0)),
            pl.BlockSpec((None, 1, 2 * EXPERT_FF), lambda i, be, nu: (be[blk(i, be, nu)], 0, 0)),
            pl.BlockSpec((None, EXPERT_FF, D_MODEL), lambda i, be, nu: (be[blk(i, be, nu)], 0, 0)),
            pl.BlockSpec((None, 1, D_MODEL), lambda i, be, nu: (be[blk(i, be, nu)], 0, 0)),
        ],
        out_specs=pl.BlockSpec((MOE_ROWS, D_MODEL), lambda i, be, nu: (blk(i, be, nu), 0)),
    )
    return pl.pallas_call(
        _moe_kernel,
        grid_spec=grid_spec,
        out_shape=jax.ShapeDtypeStruct((m_pad, D_MODEL), F32),
        compiler_params=pltpu.CompilerParams(
            dimension_semantics=("arbitrary",), vmem_limit_bytes=VMEM_LIMIT),
        name="moe_experts",
    )(block_expert, n_used, x_sorted, w_gu, b_gu, w_down, b_down)


def _combine_kernel(yk_ref, gates_ref, h_ref, g_ref, b_ref, out_ref):
    gates = gates_ref[...]
    acc = ALPHA * h_ref[...]
    for k in range(TOP_K):
        acc = acc + yk_ref[k] * gates[:, k:k + 1]
    out_ref[...] = _layer_norm(acc, g_ref[...], b_ref[...])


def _combine(yk, gates, h_all, ln_g, ln_b):
    n = h_all.shape[0]
    return pl.pallas_call(
        _combine_kernel,
        grid=(n // COMBINE_TOKENS,),
        in_specs=[
            pl.BlockSpec((TOP_K, COMBINE_TOKENS, D_MODEL), lambda i: (0, i, 0)),
            pl.BlockSpec((COMBINE_TOKENS, TOP_K), lambda i: (i, 0)),
            pl.BlockSpec((COMBINE_TOKENS, D_MODEL), lambda i: (i, 0)),
            _const_spec((1, D_MODEL)),
            _const_spec((1, D_MODEL)),
        ],
        out_specs=pl.BlockSpec((COMBINE_TOKENS, D_MODEL), lambda i: (i, 0)),
        out_shape=jax.ShapeDtypeStruct((n, D_MODEL), F32),
        compiler_params=pltpu.CompilerParams(
            dimension_semantics=("arbitrary",), vmem_limit_bytes=VMEM_LIMIT),
        name="moe_combine",
    )(yk, gates, h_all, ln_g, ln_b)


def _route(logits):
    n = logits.shape[0]
    nk = n * TOP_K
    top_val, top_idx = lax.top_k(logits, TOP_K)
    gates = jax.nn.softmax(top_val, axis=-1)
    flat_e = top_idx.reshape(-1)
    onehot = (flat_e[:, None] == jnp.arange(NUM_EXPERTS, dtype=flat_e.dtype)[None, :]).astype(jnp.int32)
    csum = jnp.cumsum(onehot, axis=0)
    rank = jnp.take_along_axis(csum, flat_e[:, None], axis=1)[:, 0] - 1
    counts = csum[-1]
    padded = ((counts + MOE_ROWS - 1) // MOE_ROWS) * MOE_ROWS
    pad_end = jnp.cumsum(padded)
    pad_start = pad_end - padded
    dest = pad_start[flat_e] + rank
    n_blocks = -(-nk // MOE_ROWS) + NUM_EXPERTS
    m_pad = n_blocks * MOE_ROWS
    row_tok = jnp.zeros((m_pad,), jnp.int32).at[dest].set(jnp.arange(nk, dtype=jnp.int32) // TOP_K)
    block_start = jnp.arange(n_blocks, dtype=jnp.int32) * MOE_ROWS
    block_expert = jnp.minimum(jnp.searchsorted(pad_end, block_start, side='right'),
                               NUM_EXPERTS - 1).astype(jnp.int32)
    n_used = (pad_end[-1] // MOE_ROWS).astype(jnp.int32).reshape(1)
    return gates, dest.reshape(n, TOP_K), row_tok, block_expert, n_used


def kernel(x_prompt, x_sample, state_pool, state_gla, w_in, w_pool, pool_scale, w_gate_up, b_gate,
           gla_norm_w, w_out, ln1_g, ln1_b, w_router, b_router, w_gu, b_gu, w_down, b_down,
           ln2_g, ln2_b):
    assert w_in.shape[0] == 1, "single-layer kernel"
    bp, seq, _ = x_prompt.shape
    bs, dec_seq, _ = x_sample.shape
    assert dec_seq == CHUNK and seq % TILE_TOKENS == 0 and bs % CHUNKS_PER_TILE == 0
    n_prompt = bp * seq
    n_total = n_prompt + bs * dec_seq
    past_len = 1024

    w_glr = jnp.zeros((D_MODEL, LANES), BF16).at[:, :GATE_RANK].set(w_in[0, :, N_MAIN:].astype(BF16))
    w_gate = jnp.zeros((LANES, GLA_HEADS * GLA_DK), BF16).at[:GATE_RANK].set(w_gate_up[0].astype(BF16))
    w_rt = jnp.zeros((D_MODEL, LANES), BF16).at[:, :NUM_EXPERTS].set(w_router[0].astype(BF16))
    b_rt = jnp.zeros((1, LANES), F32).at[0, :NUM_EXPERTS].set(b_router[0])
    weights = (
        w_in[0, :, :N_MAIN].astype(BF16), w_glr, w_gate, b_gate[0][None, :],
        w_pool[0].astype(BF16), pool_scale[0][None, :], gla_norm_w[0][None, :],
        w_out[0].astype(BF16), ln1_g[0][None, :], ln1_b[0][None, :], w_rt, b_rt,
    )

    h_all, logits_all, hist_p, s_p = _mixer_prompt(x_prompt, weights, n_total)
    h_all, logits_all, hist_s, s_s = _mixer_sample(
        x_sample, state_pool[0], state_gla[0], weights, h_all, logits_all, past_len, n_prompt)

    gates, dest, row_tok, block_expert, n_used = _route(logits_all[:, :NUM_EXPERTS])
    x_sorted = h_all.astype(BF16)[row_tok]
    y_sorted = _moe_experts(block_expert, n_used, x_sorted, w_gu[0].astype(BF16),
                            b_gu[0][:, None, :], w_down[0].astype(BF16), b_down[0][:, None, :])
    yk = y_sorted[dest.T]
    y_all = _combine(yk, gates, h_all, ln2_g[0][None, :], ln2_b[0][None, :])

    y_prompt = y_all[:n_prompt].reshape(bp, seq, D_MODEL)
    y_sample = y_all[n_prompt:].reshape(bs, dec_seq, D_MODEL)
    return (y_prompt, y_sample, hist_p[None], s_p[None], hist_s[None], s_s[None])
```

```python
import functools

import jax
import jax.numpy as jnp
from jax import lax
from jax.experimental import pallas as pl
from jax.experimental.pallas import tpu as pltpu

F32 = jnp.float32
BF16 = jnp.bfloat16

D_MODEL = 1024
CHUNK = 64
PAST_LEN = 1024
POOL_WIDTH = 512
POOL_WINDOWS = (2, 4, 8, 16)
POOL_GROUP = 128
POOL_HIST = 15
GLA_HEADS = 4
GLA_DK = 64
GLA_DV = 128
GATE_RANK = 16
GATE_NORMALIZER = 16.0
NUM_EXPERTS = 32
TOP_K = 4
EXPERT_FF = 1024
SWIGLU_LIMIT = 7.0
SWIGLU_ALPHA = 1.702
LN_EPS = 1e-5
RMS_EPS = 1e-6
ALPHA = 2.0 ** 0.25

Q0 = POOL_WIDTH
K0 = Q0 + GLA_HEADS * GLA_DK
V0 = K0 + GLA_HEADS * GLA_DK
R0 = V0 + GLA_HEADS * GLA_DV
N_MAIN = R0 + GLA_HEADS * GLA_DV

LANES = 128
TILE_TOKENS = 512
CHUNKS_PER_TILE = TILE_TOKENS // CHUNK
HIST_PAD = 16
MOE_ROWS = 256
COMBINE_TOKENS = 256
VMEM_LIMIT = 56 * 1024 * 1024


def _dot(a, b):
    return jnp.dot(a, b, preferred_element_type=F32)


def _dot_nt(a, b):
    return lax.dot_general(a, b, (((1,), (1,)), ((), ())), preferred_element_type=F32)


def _dot_tn(a, b):
    return lax.dot_general(a, b, (((0,), (0,)), ((), ())), preferred_element_type=F32)


def _layer_norm(v, g, b):
    mu = jnp.mean(v, axis=-1, keepdims=True)
    c = v - mu
    var = jnp.mean(c * c, axis=-1, keepdims=True)
    return c * lax.rsqrt(var + LN_EPS) * g + b


N_MIXER_WEIGHTS = 12
N_MIXER_SHARED_OUT = 3


def _mixer_kernel(per_chunk_state, pos0, *refs):
    if per_chunk_state:
        (x_ref, hist_in_ref, s_in_ref, *rest) = refs
    else:
        (x_ref, *rest) = refs
        hist_in_ref = s_in_ref = None
    (w_main_ref, w_glr_ref, w_gate_ref, b_gate_ref, w_pool_ref, pscale_ref, gnorm_ref,
     w_out_ref, ln1g_ref, ln1b_ref, w_router_ref, b_router_ref, *rest) = rest
    if per_chunk_state:
        rest = rest[N_MIXER_SHARED_OUT:]
    (h_ref, hb_ref, logits_ref, hist_out_ref, s_out_ref,
     proj_scr, g_scr, o_scr, ext_scr, st_scr) = rest

    t = pl.program_id(1) if not per_chunk_state else None
    x = x_ref[...].reshape(TILE_TOKENS, D_MODEL)
    xb = x.astype(BF16)

    proj_scr[...] = _dot(xb, w_main_ref[...])
    glr = _dot(xb, w_glr_ref[...])
    gk = _dot(glr.astype(BF16), w_gate_ref[...]) + b_gate_ref[...]
    log_sig = jnp.minimum(gk, 0.0) - jnp.log1p(jnp.exp(-jnp.abs(gk)))
    g_scr[...] = log_sig / GATE_NORMALIZER

    if per_chunk_state:
        seg_len, seg_stride, n_seg = CHUNK, CHUNK + HIST_PAD, CHUNKS_PER_TILE
        for c in range(n_seg):
            base = c * seg_stride
            ext_scr[base:base + HIST_PAD, :] = jnp.zeros((HIST_PAD, POOL_WIDTH), F32)
            ext_scr[base + 1:base + HIST_PAD, :] = hist_in_ref[c]
            ext_scr[base + HIST_PAD:base + seg_stride, :] = proj_scr[c * CHUNK:(c + 1) * CHUNK, 0:POOL_WIDTH]
        row_pos = pos0 + lax.broadcasted_iota(jnp.int32, (seg_len, POOL_GROUP), 0)
    else:
        seg_len, seg_stride, n_seg = TILE_TOKENS, TILE_TOKENS + HIST_PAD, 1

        @pl.when(t == 0)
        def _():
            ext_scr[0:HIST_PAD, :] = jnp.zeros((HIST_PAD, POOL_WIDTH), F32)

        ext_scr[HIST_PAD:seg_stride, :] = proj_scr[:, 0:POOL_WIDTH]
        row_pos = pos0 + t * TILE_TOKENS + lax.broadcasted_iota(jnp.int32, (seg_len, POOL_GROUP), 0)

    pool_cols = []
    for gi, w in enumerate(POOL_WINDOWS):
        gs = slice(gi * POOL_GROUP, (gi + 1) * POOL_GROUP)
        cnt = jnp.minimum(row_pos + 1, w).astype(F32)
        segs = []
        for s in range(n_seg):
            base = s * seg_stride + HIST_PAD
            acc = ext_scr[base:base + seg_len, gs]
            cur = acc
            for k in range(1, w):
                acc = acc + ext_scr[base - k:base - k + seg_len, gs]
            segs.append(acc / cnt - cur)
        pooled = segs[0] if n_seg == 1 else jnp.concatenate(segs, axis=0)
        z = _dot(pooled.astype(BF16), w_pool_ref[gi])
        pool_cols.append(z * pscale_ref[:, gs])
    pool_out = jnp.concatenate(pool_cols, axis=1)

    if per_chunk_state:
        for c in range(n_seg):
            end = (c + 1) * seg_stride
            hist_out_ref[c] = ext_scr[end - POOL_HIST:end, :]
    else:
        @pl.when(t == pl.num_programs(1) - 1)
        def _():
            hist_out_ref[...] = ext_scr[seg_stride - POOL_HIST:seg_stride, :]

        ext_scr[0:HIST_PAD, :] = ext_scr[TILE_TOKENS:seg_stride, :]

    if not per_chunk_state:
        @pl.when(t == 0)
        def _():
            st_scr[...] = jnp.zeros_like(st_scr)

    ii = lax.broadcasted_iota(jnp.int32, (CHUNK, CHUNK), 0)
    jj = lax.broadcasted_iota(jnp.int32, (CHUNK, CHUNK), 1)
    causal = ii >= jj
    tri = jnp.where(causal, 1.0, 0.0).astype(BF16)

    def chunk_body(c, carry):
        r0 = pl.multiple_of(c * CHUNK, CHUNK)
        rows = pl.ds(r0, CHUNK)
        gc = g_scr[rows, :]
        g1 = gc.astype(BF16)
        e1 = gc - g1.astype(F32)
        g2 = e1.astype(BF16)
        g3 = (e1 - g2.astype(F32)).astype(BF16)
        b = _dot(tri, g1) + _dot(tri, g2) + _dot(tri, g3)
        b_last = b[CHUNK - 1:CHUNK, :]
        q = proj_scr[rows, Q0:K0] * (GLA_DK ** -0.5)
        k = proj_scr[rows, K0:V0]
        qt = (q * jnp.exp(b)).astype(BF16)
        kt = (k * jnp.exp(-b)).astype(BF16)
        kl = (k * jnp.exp(b_last - b)).astype(BF16)
        e_last = jnp.exp(b_last)
        v = proj_scr[rows, V0:R0].astype(BF16)
        if per_chunk_state:
            st = jnp.transpose(s_in_ref[c].reshape(GLA_HEADS * GLA_DK, GLA_DV))
        else:
            st = st_scr[...]
        new_cols = []
        for h in range(GLA_HEADS):
            ks = slice(h * GLA_DK, (h + 1) * GLA_DK)
            vs = slice(h * GLA_DV, (h + 1) * GLA_DV)
            att = jnp.where(causal, _dot_nt(qt[:, ks], kt[:, ks]), 0.0)
            st_h = st[:, ks]
            o = _dot(att.astype(BF16), v[:, vs]) + _dot_nt(qt[:, ks], st_h.astype(BF16))
            o_scr[rows, vs] = o
            new_cols.append(st_h * e_last[:, ks] + _dot_tn(v[:, vs], kl[:, ks]))
        st_new = jnp.concatenate(new_cols, axis=1)
        if per_chunk_state:
            s_out_ref[c] = jnp.transpose(st_new).reshape(GLA_HEADS, GLA_DK, GLA_DV)
        else:
            st_scr[...] = st_new
        return carry

    lax.fori_loop(0, CHUNKS_PER_TILE, chunk_body, 0)

    if not per_chunk_state:
        @pl.when(t == pl.num_programs(1) - 1)
        def _():
            s_out_ref[...] = jnp.transpose(st_scr[...]).reshape(GLA_HEADS, GLA_DK, GLA_DV)

    r = proj_scr[:, R0:N_MAIN]
    silu_r = r * (1.0 / (1.0 + jnp.exp(-r)))
    gated = []
    for h in range(GLA_HEADS):
        vs = slice(h * GLA_DV, (h + 1) * GLA_DV)
        oh = o_scr[:, vs]
        ms = jnp.mean(oh * oh, axis=-1, keepdims=True)
        gated.append(oh * lax.rsqrt(ms + RMS_EPS) * gnorm_ref[...] * silu_r[:, vs])
    mix_in = jnp.concatenate([pool_out] + gated, axis=1).astype(BF16)
    resid = ALPHA * x + _dot(mix_in, w_out_ref[...])
    h_val = _layer_norm(resid, ln1g_ref[...], ln1b_ref[...])
    h_ref[...] = h_val
    hb = h_val.astype(BF16)
    hb_ref[...] = hb
    logits_ref[...] = _dot_nt(w_router_ref[...], hb) + b_router_ref[:, 0:1]


def _const_spec(shape):
    nd = len(shape)
    return pl.BlockSpec(shape, lambda *_: (0,) * nd)


def _mixer_weight_specs():
    return [
        _const_spec((D_MODEL, N_MAIN)),
        _const_spec((D_MODEL, LANES)),
        _const_spec((LANES, GLA_HEADS * GLA_DK)),
        _const_spec((1, GLA_HEADS * GLA_DK)),
        _const_spec((len(POOL_WINDOWS), POOL_GROUP, POOL_GROUP)),
        _const_spec((1, POOL_WIDTH)),
        _const_spec((1, GLA_DV)),
        _const_spec((D_MODEL, D_MODEL)),
        _const_spec((1, D_MODEL)),
        _const_spec((1, D_MODEL)),
        _const_spec((NUM_EXPERTS, D_MODEL)),
        _const_spec((NUM_EXPERTS, LANES)),
    ]


def _mixer_scratch(per_chunk_state):
    ext_rows = (CHUNKS_PER_TILE * (CHUNK + HIST_PAD)) if per_chunk_state else (TILE_TOKENS + HIST_PAD)
    return [
        pltpu.VMEM((TILE_TOKENS, N_MAIN), F32),
        pltpu.VMEM((TILE_TOKENS, GLA_HEADS * GLA_DK), F32),
        pltpu.VMEM((TILE_TOKENS, GLA_HEADS * GLA_DV), F32),
        pltpu.VMEM((ext_rows, POOL_WIDTH), F32),
        pltpu.VMEM((GLA_DV, GLA_HEADS * GLA_DK), F32),
    ]


def _mixer_out_shapes(n_total, bsz):
    return (
        jax.ShapeDtypeStruct((n_total, D_MODEL), F32),
        jax.ShapeDtypeStruct((n_total, D_MODEL), BF16),
        jax.ShapeDtypeStruct((NUM_EXPERTS, n_total), F32),
        jax.ShapeDtypeStruct((bsz, POOL_HIST, POOL_WIDTH), F32),
        jax.ShapeDtypeStruct((bsz, GLA_HEADS, GLA_DK, GLA_DV), F32),
    )


def _mixer_prompt(x, weights, n_total):
    bsz, seq, _ = x.shape
    tiles = seq // TILE_TOKENS
    return pl.pallas_call(
        functools.partial(_mixer_kernel, False, 0),
        grid=(bsz, tiles),
        in_specs=[pl.BlockSpec((None, TILE_TOKENS, D_MODEL), lambda b, t: (b, t, 0))] + _mixer_weight_specs(),
        out_specs=(
            pl.BlockSpec((TILE_TOKENS, D_MODEL), lambda b, t: (b * tiles + t, 0)),
            pl.BlockSpec((TILE_TOKENS, D_MODEL), lambda b, t: (b * tiles + t, 0)),
            pl.BlockSpec((NUM_EXPERTS, TILE_TOKENS), lambda b, t: (0, b * tiles + t)),
            pl.BlockSpec((None, POOL_HIST, POOL_WIDTH), lambda b, t: (b, 0, 0)),
            pl.BlockSpec((None, GLA_HEADS, GLA_DK, GLA_DV), lambda b, t: (b, 0, 0, 0)),
        ),
        out_shape=_mixer_out_shapes(n_total, bsz),
        scratch_shapes=_mixer_scratch(False),
        compiler_params=pltpu.CompilerParams(
            dimension_semantics=("arbitrary", "arbitrary"), vmem_limit_bytes=VMEM_LIMIT),
        name="mixer_prompt",
    )(x, *weights)


def _mixer_sample(x, hist, state, weights, shared, row_offset):
    bsz = x.shape[0]
    tiles = bsz // CHUNKS_PER_TILE
    tile0 = row_offset // TILE_TOKENS
    n_total = shared[0].shape[0]
    n_in = 3 + N_MIXER_WEIGHTS
    return pl.pallas_call(
        functools.partial(_mixer_kernel, True, PAST_LEN),
        grid=(tiles,),
        in_specs=[
            pl.BlockSpec((CHUNKS_PER_TILE, CHUNK, D_MODEL), lambda i: (i, 0, 0)),
            pl.BlockSpec((CHUNKS_PER_TILE, POOL_HIST, POOL_WIDTH), lambda i: (i, 0, 0)),
            pl.BlockSpec((CHUNKS_PER_TILE, GLA_HEADS, GLA_DK, GLA_DV), lambda i: (i, 0, 0, 0)),
        ] + _mixer_weight_specs() + [pl.BlockSpec(memory_space=pl.ANY)] * N_MIXER_SHARED_OUT,
        out_specs=(
            pl.BlockSpec((TILE_TOKENS, D_MODEL), lambda i: (tile0 + i, 0)),
            pl.BlockSpec((TILE_TOKENS, D_MODEL), lambda i: (tile0 + i, 0)),
            pl.BlockSpec((NUM_EXPERTS, TILE_TOKENS), lambda i: (0, tile0 + i)),
            pl.BlockSpec((CHUNKS_PER_TILE, POOL_HIST, POOL_WIDTH), lambda i: (i, 0, 0)),
            pl.BlockSpec((CHUNKS_PER_TILE, GLA_HEADS, GLA_DK, GLA_DV), lambda i: (i, 0, 0, 0)),
        ),
        out_shape=_mixer_out_shapes(n_total, bsz),
        input_output_aliases={n_in + j: j for j in range(N_MIXER_SHARED_OUT)},
        scratch_shapes=_mixer_scratch(True),
        compiler_params=pltpu.CompilerParams(
            dimension_semantics=("arbitrary",), vmem_limit_bytes=VMEM_LIMIT),
        name="mixer_sample",
    )(x, hist, state, *weights, *shared)


def _router_kernel(lt_ref, gates_ref, dest_ref, padend_ref, cnt_scr, base_scr, pstart_scr):
    phase = pl.program_id(0)
    i = pl.program_id(1)
    shape = (NUM_EXPERTS, TILE_TOKENS)
    logits = lt_ref[...]
    row = lax.broadcasted_iota(jnp.int32, shape, 0)
    sel, vals = [], []
    for _ in range(TOP_K):
        m = jnp.max(logits, axis=0, keepdims=True)
        idx = jnp.min(jnp.where(logits == m, row, NUM_EXPERTS), axis=0, keepdims=True)
        hit = row == idx
        sel.append(hit)
        vals.append(m)
        logits = jnp.where(hit, -jnp.inf, logits)
    chosen = sum(jnp.where(hit, 1.0, 0.0) for hit in sel)
    tile_counts = jnp.broadcast_to(jnp.sum(chosen, axis=1, keepdims=True), (NUM_EXPERTS, LANES))

    @pl.when(phase == 0)
    def _():
        @pl.when(i == 0)
        def _():
            cnt_scr[...] = jnp.zeros_like(cnt_scr)

        cnt_scr[...] += tile_counts

    @pl.when(phase == 1)
    def _():
        @pl.when(i == 0)
        def _():
            blocks = jnp.floor((cnt_scr[...] + (MOE_ROWS - 1)) * (1.0 / MOE_ROWS))
            erow = lax.broadcasted_iota(jnp.int32, (NUM_EXPERTS, LANES), 0)
            cum = blocks
            shift = 1
            while shift < NUM_EXPERTS:
                cum = cum + jnp.where(erow >= shift, pltpu.roll(cum, shift, 0), 0.0)
                shift *= 2
            padend_ref[...] = cum * MOE_ROWS
            pstart_scr[...] = (cum - blocks) * MOE_ROWS
            base_scr[...] = jnp.zeros_like(base_scr)

        ti = lax.broadcasted_iota(jnp.int32, (TILE_TOKENS, TILE_TOKENS), 0)
        tj = lax.broadcasted_iota(jnp.int32, (TILE_TOKENS, TILE_TOKENS), 1)
        before = jnp.where(ti < tj, 1.0, 0.0).astype(BF16)
        earlier = _dot(chosen.astype(BF16), before)
        pos = pstart_scr[:, 0:1] + base_scr[:, 0:1] + earlier
        dest = [jnp.sum(jnp.where(hit, pos, 0.0), axis=0, keepdims=True) for hit in sel]
        dest_ref[...] = jnp.concatenate(dest, axis=0).astype(jnp.int32)
        ex = [jnp.exp(v - vals[0]) for v in vals]
        denom = ex[0] + ex[1] + ex[2] + ex[3]
        gates_ref[...] = jnp.concatenate([e / denom for e in ex], axis=0)
        base_scr[...] += tile_counts


def _router(logits_t):
    n = logits_t.shape[1]
    tiles = n // TILE_TOKENS
    return pl.pallas_call(
        _router_kernel,
        grid=(2, tiles),
        in_specs=[pl.BlockSpec((NUM_EXPERTS, TILE_TOKENS), lambda p, i: (0, i))],
        out_specs=(
            pl.BlockSpec((TOP_K, TILE_TOKENS), lambda p, i: (0, i * p)),
            pl.BlockSpec((TOP_K, TILE_TOKENS), lambda p, i: (0, i * p)),
            pl.BlockSpec((NUM_EXPERTS, LANES), lambda p, i: (0, 0)),
        ),
        out_shape=(
            jax.ShapeDtypeStruct((TOP_K, n), F32),
            jax.ShapeDtypeStruct((TOP_K, n), jnp.int32),
            jax.ShapeDtypeStruct((NUM_EXPERTS, LANES), F32),
        ),
        scratch_shapes=[pltpu.VMEM((NUM_EXPERTS, LANES), F32)] * 3,
        compiler_params=pltpu.CompilerParams(
            dimension_semantics=("arbitrary", "arbitrary"), vmem_limit_bytes=VMEM_LIMIT),
        name="router",
    )(logits_t)


def _moe_kernel(be_ref, nused_ref, x_ref, wgu_ref, bgu_ref, wd_ref, bd_ref, y_ref, wgu_bf, wd_bf):
    i = pl.program_id(0)

    @pl.when(i < nused_ref[0])
    def _():
        @pl.when((i == 0) | (be_ref[i] != be_ref[jnp.maximum(i - 1, 0)]))
        def _():
            wgu_bf[...] = wgu_ref[...].astype(BF16)
            wd_bf[...] = wd_ref[...].astype(BF16)

        gu = _dot(x_ref[...], wgu_bf[...]) + bgu_ref[...]
        gate = jnp.minimum(gu[:, :EXPERT_FF], SWIGLU_LIMIT)
        up = jnp.clip(gu[:, EXPERT_FF:], -SWIGLU_LIMIT, SWIGLU_LIMIT)
        hmid = gate * (1.0 / (1.0 + jnp.exp(-SWIGLU_ALPHA * gate))) * (up + 1.0)
        y_ref[...] = _dot(hmid.astype(BF16), wd_bf[...]) + bd_ref[...]


def _moe_experts(block_expert, n_used, x_sorted, w_gu, b_gu, w_down, b_down):
    m_pad = x_sorted.shape[0]
    n_blocks = m_pad // MOE_ROWS

    def blk(i, be, nu):
        return jnp.minimum(i, nu[0] - 1)

    def expert(i, be, nu):
        return be[blk(i, be, nu)]

    grid_spec = pltpu.PrefetchScalarGridSpec(
        num_scalar_prefetch=2,
        grid=(n_blocks,),
        in_specs=[
            pl.BlockSpec((MOE_ROWS, D_MODEL), lambda i, be, nu: (blk(i, be, nu), 0)),
            pl.BlockSpec((None, D_MODEL, 2 * EXPERT_FF), lambda i, be, nu: (expert(i, be, nu), 0, 0)),
            pl.BlockSpec((None, 1, 2 * EXPERT_FF), lambda i, be, nu: (expert(i, be, nu), 0, 0)),
            pl.BlockSpec((None, EXPERT_FF, D_MODEL), lambda i, be, nu: (expert(i, be, nu), 0, 0)),
            pl.BlockSpec((None, 1, D_MODEL), lambda i, be, nu: (expert(i, be, nu), 0, 0)),
        ],
        out_specs=pl.BlockSpec((MOE_ROWS, D_MODEL), lambda i, be, nu: (blk(i, be, nu), 0)),
        scratch_shapes=[
            pltpu.VMEM((D_MODEL, 2 * EXPERT_FF), BF16),
            pltpu.VMEM((EXPERT_FF, D_MODEL), BF16),
        ],
    )
    return pl.pallas_call(
        _moe_kernel,
        grid_spec=grid_spec,
        out_shape=jax.ShapeDtypeStruct((m_pad, D_MODEL), F32),
        compiler_params=pltpu.CompilerParams(
            dimension_semantics=("arbitrary",), vmem_limit_bytes=VMEM_LIMIT),
        name="moe_experts",
    )(block_expert, n_used, x_sorted, w_gu, b_gu, w_down, b_down)


def _combine_kernel(yk_ref, gates_ref, h_ref, g_ref, b_ref, out_ref):
    gates = gates_ref[...]
    acc = ALPHA * h_ref[...]
    for k in range(TOP_K):
        acc = acc + yk_ref[k] * gates[:, k:k + 1]
    out_ref[...] = _layer_norm(acc, g_ref[...], b_ref[...])


def _combine(yk, gates, h_all, ln_g, ln_b, row_offset, n_rows):
    tile0 = row_offset // COMBINE_TOKENS
    return pl.pallas_call(
        _combine_kernel,
        grid=(n_rows // COMBINE_TOKENS,),
        in_specs=[
            pl.BlockSpec((TOP_K, COMBINE_TOKENS, D_MODEL), lambda i: (0, tile0 + i, 0)),
            pl.BlockSpec((COMBINE_TOKENS, TOP_K), lambda i: (tile0 + i, 0)),
            pl.BlockSpec((COMBINE_TOKENS, D_MODEL), lambda i: (tile0 + i, 0)),
            _const_spec((1, D_MODEL)),
            _const_spec((1, D_MODEL)),
        ],
        out_specs=pl.BlockSpec((COMBINE_TOKENS, D_MODEL), lambda i: (i, 0)),
        out_shape=jax.ShapeDtypeStruct((n_rows, D_MODEL), F32),
        compiler_params=pltpu.CompilerParams(
            dimension_semantics=("arbitrary",), vmem_limit_bytes=VMEM_LIMIT),
        name="moe_combine",
    )(yk, gates, h_all, ln_g, ln_b)


def kernel(x_prompt, x_sample, state_pool, state_gla, w_in, w_pool, pool_scale, w_gate_up, b_gate,
           gla_norm_w, w_out, ln1_g, ln1_b, w_router, b_router, w_gu, b_gu, w_down, b_down,
           ln2_g, ln2_b):
    assert w_in.shape[0] == 1, "single-layer kernel"
    bp, seq, _ = x_prompt.shape
    bs, dec_seq, _ = x_sample.shape
    assert dec_seq == CHUNK and seq % TILE_TOKENS == 0 and bs % CHUNKS_PER_TILE == 0
    n_prompt = bp * seq
    n_sample = bs * dec_seq
    n_total = n_prompt + n_sample
    nk = n_total * TOP_K
    n_blocks = -(-nk // MOE_ROWS) + NUM_EXPERTS
    m_pad = n_blocks * MOE_ROWS

    w_glr = jnp.zeros((D_MODEL, LANES), BF16).at[:, :GATE_RANK].set(w_in[0, :, N_MAIN:].astype(BF16))
    w_gate = jnp.zeros((LANES, GLA_HEADS * GLA_DK), BF16).at[:GATE_RANK].set(w_gate_up[0].astype(BF16))
    weights = (
        w_in[0, :, :N_MAIN].astype(BF16), w_glr, w_gate, b_gate[0][None, :],
        w_pool[0].astype(BF16), pool_scale[0][None, :], gla_norm_w[0][None, :],
        w_out[0].astype(BF16), ln1_g[0][None, :], ln1_b[0][None, :],
        w_router[0].T.astype(BF16), jnp.broadcast_to(b_router[0][:, None], (NUM_EXPERTS, LANES)),
    )
    assert len(weights) == N_MIXER_WEIGHTS

    *shared, hist_p, s_p = _mixer_prompt(x_prompt, weights, n_total)
    h_all, hb_all, logits_t, hist_s, s_s = _mixer_sample(
        x_sample, state_pool[0], state_gla[0], weights, shared, n_prompt)

    gates_t, dest_t, pad_end = _router(logits_t)
    pad_end = pad_end[:, 0].astype(jnp.int32)
    block_start = jnp.arange(n_blocks, dtype=jnp.int32) * MOE_ROWS
    block_expert = jnp.minimum(jnp.sum((block_start[:, None] >= pad_end[None, :]).astype(jnp.int32), axis=1),
                               NUM_EXPERTS - 1)
    n_used = (pad_end[-1:] // MOE_ROWS).astype(jnp.int32)

    tok = jnp.broadcast_to(jnp.arange(n_total, dtype=jnp.int32)[None, :], (TOP_K, n_total))
    row_tok = jnp.zeros((m_pad,), jnp.int32).at[dest_t.reshape(-1)].set(tok.reshape(-1))
    x_sorted = hb_all[row_tok]
    y_sorted = _moe_experts(block_expert, n_used, x_sorted, w_gu[0], b_gu[0][:, None, :],
                            w_down[0], b_down[0][:, None, :])
    yk = y_sorted[dest_t]
    gates = gates_t.T
    ln_g, ln_b = ln2_g[0][None, :], ln2_b[0][None, :]
    y_prompt = _combine(yk, gates, h_all, ln_g, ln_b, 0, n_prompt).reshape(bp, seq, D_MODEL)
    y_sample = _combine(yk, gates, h_all, ln_g, ln_b, n_prompt, n_sample).reshape(bs, dec_seq, D_MODEL)
    return (y_prompt, y_sample, hist_p[None], s_p[None], hist_s[None], s_s[None])
```

```python
import functools

import jax
import jax.numpy as jnp
from jax import lax
from jax.experimental import pallas as pl
from jax.experimental.pallas import tpu as pltpu
from jax.experimental.pallas import tpu_sc as plsc

F32 = jnp.float32
BF16 = jnp.bfloat16

D_MODEL = 1024
CHUNK = 64
PAST_LEN = 1024
POOL_WIDTH = 512
POOL_WINDOWS = (2, 4, 8, 16)
POOL_GROUP = 128
POOL_HIST = 15
GLA_HEADS = 4
GLA_DK = 64
GLA_DV = 128
GATE_RANK = 16
GATE_NORMALIZER = 16.0
NUM_EXPERTS = 32
TOP_K = 4
EXPERT_FF = 1024
SWIGLU_LIMIT = 7.0
SWIGLU_ALPHA = 1.702
LN_EPS = 1e-5
RMS_EPS = 1e-6
ALPHA = 2.0 ** 0.25

Q0 = POOL_WIDTH
K0 = Q0 + GLA_HEADS * GLA_DK
V0 = K0 + GLA_HEADS * GLA_DK
R0 = V0 + GLA_HEADS * GLA_DV
N_MAIN = R0 + GLA_HEADS * GLA_DV

LANES = 128
TILE_TOKENS = 512
CHUNKS_PER_TILE = TILE_TOKENS // CHUNK
HIST_PAD = 16
MOE_ROWS = 256
COMBINE_TOKENS = 256
VMEM_LIMIT = 56 * 1024 * 1024


def _dot(a, b):
    return jnp.dot(a, b, preferred_element_type=F32)


def _dot_nt(a, b):
    return lax.dot_general(a, b, (((1,), (1,)), ((), ())), preferred_element_type=F32)


def _dot_tn(a, b):
    return lax.dot_general(a, b, (((0,), (0,)), ((), ())), preferred_element_type=F32)


HALF = D_MODEL // 2
HI_MASK = 0xFFFF0000


def _pack_bf16_pairs(xb):
    lo = lax.bitcast_convert_type(xb[:, :HALF].astype(F32), jnp.uint32) >> 16
    hi = lax.bitcast_convert_type(xb[:, HALF:].astype(F32), jnp.uint32) & jnp.uint32(HI_MASK)
    return lax.bitcast_convert_type(hi | lo, jnp.int32)


def _unpack_bf16_pairs(p):
    u = lax.bitcast_convert_type(p, jnp.uint32)
    lo = lax.bitcast_convert_type(u << 16, F32)
    hi = lax.bitcast_convert_type(u & jnp.uint32(HI_MASK), F32)
    return jnp.concatenate([lo, hi], axis=1).astype(BF16)


def _layer_norm(v, g, b):
    mu = jnp.mean(v, axis=-1, keepdims=True)
    c = v - mu
    var = jnp.mean(c * c, axis=-1, keepdims=True)
    return c * lax.rsqrt(var + LN_EPS) * g + b


N_MIXER_WEIGHTS = 12
N_MIXER_SHARED_OUT = 3


def _mixer_kernel(per_chunk_state, pos0, *refs):
    if per_chunk_state:
        (x_ref, hist_in_ref, s_in_ref, *rest) = refs
    else:
        (x_ref, *rest) = refs
        hist_in_ref = s_in_ref = None
    (w_main_ref, w_glr_ref, w_gate_ref, b_gate_ref, w_pool_ref, pscale_ref, gnorm_ref,
     w_out_ref, ln1g_ref, ln1b_ref, w_router_ref, b_router_ref, *rest) = rest
    if per_chunk_state:
        rest = rest[N_MIXER_SHARED_OUT:]
    (h_ref, hb_ref, logits_ref, hist_out_ref, s_out_ref,
     proj_scr, g_scr, o_scr, ext_scr, st_scr) = rest

    t = pl.program_id(1) if not per_chunk_state else None
    x = x_ref[...].reshape(TILE_TOKENS, D_MODEL)
    xb = x.astype(BF16)

    proj_scr[...] = _dot(xb, w_main_ref[...])
    glr = _dot(xb, w_glr_ref[...])
    gk = _dot(glr.astype(BF16), w_gate_ref[...]) + b_gate_ref[...]
    log_sig = jnp.minimum(gk, 0.0) - jnp.log1p(jnp.exp(-jnp.abs(gk)))
    g_scr[...] = log_sig / GATE_NORMALIZER

    if per_chunk_state:
        seg_len, seg_stride, n_seg = CHUNK, CHUNK + HIST_PAD, CHUNKS_PER_TILE
        for c in range(n_seg):
            base = c * seg_stride
            ext_scr[base:base + HIST_PAD, :] = jnp.zeros((HIST_PAD, POOL_WIDTH), F32)
            ext_scr[base + 1:base + HIST_PAD, :] = hist_in_ref[c]
            ext_scr[base + HIST_PAD:base + seg_stride, :] = proj_scr[c * CHUNK:(c + 1) * CHUNK, 0:POOL_WIDTH]
        row_pos = pos0 + lax.broadcasted_iota(jnp.int32, (seg_len, POOL_GROUP), 0)
    else:
        seg_len, seg_stride, n_seg = TILE_TOKENS, TILE_TOKENS + HIST_PAD, 1

        @pl.when(t == 0)
        def _():
            ext_scr[0:HIST_PAD, :] = jnp.zeros((HIST_PAD, POOL_WIDTH), F32)

        ext_scr[HIST_PAD:seg_stride, :] = proj_scr[:, 0:POOL_WIDTH]
        row_pos = pos0 + t * TILE_TOKENS + lax.broadcasted_iota(jnp.int32, (seg_len, POOL_GROUP), 0)

    pool_cols = []
    for gi, w in enumerate(POOL_WINDOWS):
        gs = slice(gi * POOL_GROUP, (gi + 1) * POOL_GROUP)
        cnt = jnp.minimum(row_pos + 1, w).astype(F32)
        segs = []
        for s in range(n_seg):
            base = s * seg_stride + HIST_PAD
            acc = ext_scr[base:base + seg_len, gs]
            cur = acc
            for k in range(1, w):
                acc = acc + ext_scr[base - k:base - k + seg_len, gs]
            segs.append(acc / cnt - cur)
        pooled = segs[0] if n_seg == 1 else jnp.concatenate(segs, axis=0)
        z = _dot(pooled.astype(BF16), w_pool_ref[gi])
        pool_cols.append(z * pscale_ref[:, gs])
    pool_out = jnp.concatenate(pool_cols, axis=1)

    if per_chunk_state:
        for c in range(n_seg):
            end = (c + 1) * seg_stride
            hist_out_ref[c] = ext_scr[end - POOL_HIST:end, :]
    else:
        @pl.when(t == pl.num_programs(1) - 1)
        def _():
            hist_out_ref[...] = ext_scr[seg_stride - POOL_HIST:seg_stride, :]

        ext_scr[0:HIST_PAD, :] = ext_scr[TILE_TOKENS:seg_stride, :]

    if not per_chunk_state:
        @pl.when(t == 0)
        def _():
            st_scr[...] = jnp.zeros_like(st_scr)

    ii = lax.broadcasted_iota(jnp.int32, (CHUNK, CHUNK), 0)
    jj = lax.broadcasted_iota(jnp.int32, (CHUNK, CHUNK), 1)
    causal = ii >= jj
    tri = jnp.where(causal, 1.0, 0.0).astype(BF16)

    def chunk_body(c, carry):
        r0 = pl.multiple_of(c * CHUNK, CHUNK)
        rows = pl.ds(r0, CHUNK)
        gc = g_scr[rows, :]
        g1 = gc.astype(BF16)
        e1 = gc - g1.astype(F32)
        g2 = e1.astype(BF16)
        g3 = (e1 - g2.astype(F32)).astype(BF16)
        b = _dot(tri, g1) + _dot(tri, g2) + _dot(tri, g3)
        b_last = b[CHUNK - 1:CHUNK, :]
        q = proj_scr[rows, Q0:K0] * (GLA_DK ** -0.5)
        k = proj_scr[rows, K0:V0]
        qt = (q * jnp.exp(b)).astype(BF16)
        kt = (k * jnp.exp(-b)).astype(BF16)
        kl = (k * jnp.exp(b_last - b)).astype(BF16)
        e_last = jnp.exp(b_last)
        v = proj_scr[rows, V0:R0].astype(BF16)
        if per_chunk_state:
            st = jnp.transpose(s_in_ref[c].reshape(GLA_HEADS * GLA_DK, GLA_DV))
        else:
            st = st_scr[...]
        new_cols = []
        for h in range(GLA_HEADS):
            ks = slice(h * GLA_DK, (h + 1) * GLA_DK)
            vs = slice(h * GLA_DV, (h + 1) * GLA_DV)
            att = jnp.where(causal, _dot_nt(qt[:, ks], kt[:, ks]), 0.0)
            st_h = st[:, ks]
            o = _dot(att.astype(BF16), v[:, vs]) + _dot_nt(qt[:, ks], st_h.astype(BF16))
            o_scr[rows, vs] = o
            new_cols.append(st_h * e_last[:, ks] + _dot_tn(v[:, vs], kl[:, ks]))
        st_new = jnp.concatenate(new_cols, axis=1)
        if per_chunk_state:
            s_out_ref[c] = jnp.transpose(st_new).reshape(GLA_HEADS, GLA_DK, GLA_DV)
        else:
            st_scr[...] = st_new
        return carry

    lax.fori_loop(0, CHUNKS_PER_TILE, chunk_body, 0)

    if not per_chunk_state:
        @pl.when(t == pl.num_programs(1) - 1)
        def _():
            s_out_ref[...] = jnp.transpose(st_scr[...]).reshape(GLA_HEADS, GLA_DK, GLA_DV)

    r = proj_scr[:, R0:N_MAIN]
    silu_r = r * (1.0 / (1.0 + jnp.exp(-r)))
    gated = []
    for h in range(GLA_HEADS):
        vs = slice(h * GLA_DV, (h + 1) * GLA_DV)
        oh = o_scr[:, vs]
        ms = jnp.mean(oh * oh, axis=-1, keepdims=True)
        gated.append(oh * lax.rsqrt(ms + RMS_EPS) * gnorm_ref[...] * silu_r[:, vs])
    mix_in = jnp.concatenate([pool_out] + gated, axis=1).astype(BF16)
    resid = ALPHA * x + _dot(mix_in, w_out_ref[...])
    h_val = _layer_norm(resid, ln1g_ref[...], ln1b_ref[...])
    h_ref[...] = h_val
    hb = h_val.astype(BF16)
    hb_ref[...] = _pack_bf16_pairs(hb)
    logits_ref[...] = _dot_nt(w_router_ref[...], hb) + b_router_ref[:, 0:1]


def _const_spec(shape):
    nd = len(shape)
    return pl.BlockSpec(shape, lambda *_: (0,) * nd)


def _mixer_weight_specs():
    return [
        _const_spec((D_MODEL, N_MAIN)),
        _const_spec((D_MODEL, LANES)),
        _const_spec((LANES, GLA_HEADS * GLA_DK)),
        _const_spec((1, GLA_HEADS * GLA_DK)),
        _const_spec((len(POOL_WINDOWS), POOL_GROUP, POOL_GROUP)),
        _const_spec((1, POOL_WIDTH)),
        _const_spec((1, GLA_DV)),
        _const_spec((D_MODEL, D_MODEL)),
        _const_spec((1, D_MODEL)),
        _const_spec((1, D_MODEL)),
        _const_spec((NUM_EXPERTS, D_MODEL)),
        _const_spec((NUM_EXPERTS, LANES)),
    ]


def _mixer_scratch(per_chunk_state):
    ext_rows = (CHUNKS_PER_TILE * (CHUNK + HIST_PAD)) if per_chunk_state else (TILE_TOKENS + HIST_PAD)
    return [
        pltpu.VMEM((TILE_TOKENS, N_MAIN), F32),
        pltpu.VMEM((TILE_TOKENS, GLA_HEADS * GLA_DK), F32),
        pltpu.VMEM((TILE_TOKENS, GLA_HEADS * GLA_DV), F32),
        pltpu.VMEM((ext_rows, POOL_WIDTH), F32),
        pltpu.VMEM((GLA_DV, GLA_HEADS * GLA_DK), F32),
    ]


def _mixer_out_shapes(n_total, bsz):
    return (
        jax.ShapeDtypeStruct((n_total, D_MODEL), F32),
        jax.ShapeDtypeStruct((n_total, HALF), jnp.int32),
        jax.ShapeDtypeStruct((NUM_EXPERTS, n_total), F32),
        jax.ShapeDtypeStruct((bsz, POOL_HIST, POOL_WIDTH), F32),
        jax.ShapeDtypeStruct((bsz, GLA_HEADS, GLA_DK, GLA_DV), F32),
    )


def _mixer_prompt(x, weights, n_total):
    bsz, seq, _ = x.shape
    tiles = seq // TILE_TOKENS
    return pl.pallas_call(
        functools.partial(_mixer_kernel, False, 0),
        grid=(bsz, tiles),
        in_specs=[pl.BlockSpec((None, TILE_TOKENS, D_MODEL), lambda b, t: (b, t, 0))] + _mixer_weight_specs(),
        out_specs=(
            pl.BlockSpec((TILE_TOKENS, D_MODEL), lambda b, t: (b * tiles + t, 0)),
            pl.BlockSpec((TILE_TOKENS, HALF), lambda b, t: (b * tiles + t, 0)),
            pl.BlockSpec((NUM_EXPERTS, TILE_TOKENS), lambda b, t: (0, b * tiles + t)),
            pl.BlockSpec((None, POOL_HIST, POOL_WIDTH), lambda b, t: (b, 0, 0)),
            pl.BlockSpec((None, GLA_HEADS, GLA_DK, GLA_DV), lambda b, t: (b, 0, 0, 0)),
        ),
        out_shape=_mixer_out_shapes(n_total, bsz),
        scratch_shapes=_mixer_scratch(False),
        compiler_params=pltpu.CompilerParams(
            dimension_semantics=("arbitrary", "arbitrary"), vmem_limit_bytes=VMEM_LIMIT),
        name="mixer_prompt",
    )(x, *weights)


def _mixer_sample(x, hist, state, weights, shared, row_offset):
    bsz = x.shape[0]
    tiles = bsz // CHUNKS_PER_TILE
    tile0 = row_offset // TILE_TOKENS
    n_total = shared[0].shape[0]
    n_in = 3 + N_MIXER_WEIGHTS
    return pl.pallas_call(
        functools.partial(_mixer_kernel, True, PAST_LEN),
        grid=(tiles,),
        in_specs=[
            pl.BlockSpec((CHUNKS_PER_TILE, CHUNK, D_MODEL), lambda i: (i, 0, 0)),
            pl.BlockSpec((CHUNKS_PER_TILE, POOL_HIST, POOL_WIDTH), lambda i: (i, 0, 0)),
            pl.BlockSpec((CHUNKS_PER_TILE, GLA_HEADS, GLA_DK, GLA_DV), lambda i: (i, 0, 0, 0)),
        ] + _mixer_weight_specs() + [pl.BlockSpec(memory_space=pl.ANY)] * N_MIXER_SHARED_OUT,
        out_specs=(
            pl.BlockSpec((TILE_TOKENS, D_MODEL), lambda i: (tile0 + i, 0)),
            pl.BlockSpec((TILE_TOKENS, HALF), lambda i: (tile0 + i, 0)),
            pl.BlockSpec((NUM_EXPERTS, TILE_TOKENS), lambda i: (0, tile0 + i)),
            pl.BlockSpec((CHUNKS_PER_TILE, POOL_HIST, POOL_WIDTH), lambda i: (i, 0, 0)),
            pl.BlockSpec((CHUNKS_PER_TILE, GLA_HEADS, GLA_DK, GLA_DV), lambda i: (i, 0, 0, 0)),
        ),
        out_shape=_mixer_out_shapes(n_total, bsz),
        input_output_aliases={n_in + j: j for j in range(N_MIXER_SHARED_OUT)},
        scratch_shapes=_mixer_scratch(True),
        compiler_params=pltpu.CompilerParams(
            dimension_semantics=("arbitrary",), vmem_limit_bytes=VMEM_LIMIT),
        name="mixer_sample",
    )(x, hist, state, *weights, *shared)


def _router_kernel(lt_ref, gates_ref, dest_ref, padend_ref, cnt_scr, base_scr, pstart_scr):
    phase = pl.program_id(0)
    i = pl.program_id(1)
    shape = (NUM_EXPERTS, TILE_TOKENS)
    logits = lt_ref[...]
    row = lax.broadcasted_iota(jnp.int32, shape, 0)
    sel, vals = [], []
    for _ in range(TOP_K):
        m = jnp.max(logits, axis=0, keepdims=True)
        idx = jnp.min(jnp.where(logits == m, row, NUM_EXPERTS), axis=0, keepdims=True)
        hit = row == idx
        sel.append(hit)
        vals.append(m)
        logits = jnp.where(hit, -jnp.inf, logits)
    chosen = sum(jnp.where(hit, 1.0, 0.0) for hit in sel)
    tile_counts = jnp.broadcast_to(jnp.sum(chosen, axis=1, keepdims=True), (NUM_EXPERTS, LANES))

    @pl.when(phase == 0)
    def _():
        @pl.when(i == 0)
        def _():
            cnt_scr[...] = jnp.zeros_like(cnt_scr)

        cnt_scr[...] += tile_counts

    @pl.when(phase == 1)
    def _():
        @pl.when(i == 0)
        def _():
            blocks = jnp.floor((cnt_scr[...] + (MOE_ROWS - 1)) * (1.0 / MOE_ROWS))
            erow = lax.broadcasted_iota(jnp.int32, (NUM_EXPERTS, LANES), 0)
            cum = blocks
            shift = 1
            while shift < NUM_EXPERTS:
                cum = cum + jnp.where(erow >= shift, pltpu.roll(cum, shift, 0), 0.0)
                shift *= 2
            padend_ref[...] = cum * MOE_ROWS
            pstart_scr[...] = (cum - blocks) * MOE_ROWS
            base_scr[...] = jnp.zeros_like(base_scr)

        ti = lax.broadcasted_iota(jnp.int32, (TILE_TOKENS, TILE_TOKENS), 0)
        tj = lax.broadcasted_iota(jnp.int32, (TILE_TOKENS, TILE_TOKENS), 1)
        before = jnp.where(ti < tj, 1.0, 0.0).astype(BF16)
        earlier = _dot(chosen.astype(BF16), before)
        pos = pstart_scr[:, 0:1] + base_scr[:, 0:1] + earlier
        dest = [jnp.sum(jnp.where(hit, pos, 0.0), axis=0, keepdims=True) for hit in sel]
        dest_ref[...] = jnp.concatenate(dest, axis=0).astype(jnp.int32)
        ex = [jnp.exp(v - vals[0]) for v in vals]
        denom = ex[0] + ex[1] + ex[2] + ex[3]
        gates_ref[...] = jnp.concatenate([e / denom for e in ex], axis=0)
        base_scr[...] += tile_counts


def _router(logits_t):
    n = logits_t.shape[1]
    tiles = n // TILE_TOKENS
    return pl.pallas_call(
        _router_kernel,
        grid=(2, tiles),
        in_specs=[pl.BlockSpec((NUM_EXPERTS, TILE_TOKENS), lambda p, i: (0, i))],
        out_specs=(
            pl.BlockSpec((TOP_K, TILE_TOKENS), lambda p, i: (0, i * p)),
            pl.BlockSpec((TOP_K, TILE_TOKENS), lambda p, i: (0, i * p)),
            pl.BlockSpec((NUM_EXPERTS, LANES), lambda p, i: (0, 0)),
        ),
        out_shape=(
            jax.ShapeDtypeStruct((TOP_K, n), F32),
            jax.ShapeDtypeStruct((TOP_K, n), jnp.int32),
            jax.ShapeDtypeStruct((NUM_EXPERTS, LANES), F32),
        ),
        scratch_shapes=[pltpu.VMEM((NUM_EXPERTS, LANES), F32)] * 3,
        compiler_params=pltpu.CompilerParams(
            dimension_semantics=("arbitrary", "arbitrary"), vmem_limit_bytes=VMEM_LIMIT),
        name="router",
    )(logits_t)


SC_CORES = 2
SC_SUBCORES = 16
SC_WORKERS = SC_CORES * SC_SUBCORES
DISPATCH_ROWS = 64


def _dispatch(h_packed, dest_chunks, m_pad):
    n = h_packed.shape[0]
    n_chunks = n // DISPATCH_ROWS
    assert n_chunks % SC_WORKERS == 0
    per_worker = n_chunks // SC_WORKERS
    mesh = plsc.VectorSubcoreMesh(core_axis_name="c", subcore_axis_name="s")

    @functools.partial(
        pl.kernel, mesh=mesh,
        out_type=jax.ShapeDtypeStruct((m_pad, HALF), jnp.int32),
        scratch_types=[
            pltpu.VMEM((TOP_K, DISPATCH_ROWS), jnp.int32),
            pltpu.VMEM((DISPATCH_ROWS, HALF), jnp.int32),
            pltpu.SemaphoreType.DMA,
        ],
        compiler_params=pltpu.CompilerParams(use_tc_tiling_on_sc=True),
        name="dispatch",
    )
    def dispatch_kernel(h_hbm, dest_hbm, out_hbm, idx_v, rows_v, sem):
        wid = lax.axis_index("s") * SC_CORES + lax.axis_index("c")

        @pl.loop(0, per_worker)
        def _(j):
            chunk = wid * per_worker + j
            pltpu.sync_copy(dest_hbm.at[chunk], idx_v)
            pltpu.sync_copy(h_hbm.at[pl.ds(chunk * DISPATCH_ROWS, DISPATCH_ROWS)], rows_v)
            for k in range(TOP_K):
                pltpu.async_copy(rows_v, out_hbm.at[idx_v.at[k]], sem).wait()

    return dispatch_kernel(h_packed, dest_chunks)


def _moe_kernel(be_ref, nused_ref, x_ref, wgu_ref, bgu_ref, wd_ref, bd_ref, y_ref, wgu_bf, wd_bf):
    i = pl.program_id(0)

    @pl.when(i < nused_ref[0])
    def _():
        @pl.when((i == 0) | (be_ref[i] != be_ref[jnp.maximum(i - 1, 0)]))
        def _():
            wgu_bf[...] = wgu_ref[...].astype(BF16)
            wd_bf[...] = wd_ref[...].astype(BF16)

        gu = _dot(_unpack_bf16_pairs(x_ref[...]), wgu_bf[...]) + bgu_ref[...]
        gate = jnp.minimum(gu[:, :EXPERT_FF], SWIGLU_LIMIT)
        up = jnp.clip(gu[:, EXPERT_FF:], -SWIGLU_LIMIT, SWIGLU_LIMIT)
        hmid = gate * (1.0 / (1.0 + jnp.exp(-SWIGLU_ALPHA * gate))) * (up + 1.0)
        y_ref[...] = _dot(hmid.astype(BF16), wd_bf[...]) + bd_ref[...]


def _moe_experts(block_expert, n_used, x_sorted, w_gu, b_gu, w_down, b_down):
    m_pad = x_sorted.shape[0]
    n_blocks = m_pad // MOE_ROWS

    def blk(i, be, nu):
        return jnp.minimum(i, nu[0] - 1)

    def expert(i, be, nu):
        return be[blk(i, be, nu)]

    grid_spec = pltpu.PrefetchScalarGridSpec(
        num_scalar_prefetch=2,
        grid=(n_blocks,),
        in_specs=[
            pl.BlockSpec((MOE_ROWS, HALF), lambda i, be, nu: (blk(i, be, nu), 0)),
            pl.BlockSpec((None, D_MODEL, 2 * EXPERT_FF), lambda i, be, nu: (expert(i, be, nu), 0, 0)),
            pl.BlockSpec((None, 1, 2 * EXPERT_FF), lambda i, be, nu: (expert(i, be, nu), 0, 0)),
            pl.BlockSpec((None, EXPERT_FF, D_MODEL), lambda i, be, nu: (expert(i, be, nu), 0, 0)),
            pl.BlockSpec((None, 1, D_MODEL), lambda i, be, nu: (expert(i, be, nu), 0, 0)),
        ],
        out_specs=pl.BlockSpec((MOE_ROWS, D_MODEL), lambda i, be, nu: (blk(i, be, nu), 0)),
        scratch_shapes=[
            pltpu.VMEM((D_MODEL, 2 * EXPERT_FF), BF16),
            pltpu.VMEM((EXPERT_FF, D_MODEL), BF16),
        ],
    )
    return pl.pallas_call(
        _moe_kernel,
        grid_spec=grid_spec,
        out_shape=jax.ShapeDtypeStruct((m_pad, D_MODEL), F32),
        compiler_params=pltpu.CompilerParams(
            dimension_semantics=("arbitrary",), vmem_limit_bytes=VMEM_LIMIT),
        name="moe_experts",
    )(block_expert, n_used, x_sorted, w_gu, b_gu, w_down, b_down)


def _combine_kernel(yk_ref, gates_ref, h_ref, g_ref, b_ref, out_ref):
    gates = gates_ref[...]
    acc = ALPHA * h_ref[...]
    for k in range(TOP_K):
        acc = acc + yk_ref[k] * gates[:, k:k + 1]
    out_ref[...] = _layer_norm(acc, g_ref[...], b_ref[...])


def _combine(yk, gates, h_all, ln_g, ln_b, row_offset, n_rows):
    tile0 = row_offset // COMBINE_TOKENS
    return pl.pallas_call(
        _combine_kernel,
        grid=(n_rows // COMBINE_TOKENS,),
        in_specs=[
            pl.BlockSpec((TOP_K, COMBINE_TOKENS, D_MODEL), lambda i: (0, tile0 + i, 0)),
            pl.BlockSpec((COMBINE_TOKENS, TOP_K), lambda i: (tile0 + i, 0)),
            pl.BlockSpec((COMBINE_TOKENS, D_MODEL), lambda i: (tile0 + i, 0)),
            _const_spec((1, D_MODEL)),
            _const_spec((1, D_MODEL)),
        ],
        out_specs=pl.BlockSpec((COMBINE_TOKENS, D_MODEL), lambda i: (i, 0)),
        out_shape=jax.ShapeDtypeStruct((n_rows, D_MODEL), F32),
        compiler_params=pltpu.CompilerParams(
            dimension_semantics=("arbitrary",), vmem_limit_bytes=VMEM_LIMIT),
        name="moe_combine",
    )(yk, gates, h_all, ln_g, ln_b)


def kernel(x_prompt, x_sample, state_pool, state_gla, w_in, w_pool, pool_scale, w_gate_up, b_gate,
           gla_norm_w, w_out, ln1_g, ln1_b, w_router, b_router, w_gu, b_gu, w_down, b_down,
           ln2_g, ln2_b):
    assert w_in.shape[0] == 1, "single-layer kernel"
    bp, seq, _ = x_prompt.shape
    bs, dec_seq, _ = x_sample.shape
    assert dec_seq == CHUNK and seq % TILE_TOKENS == 0 and bs % CHUNKS_PER_TILE == 0
    n_prompt = bp * seq
    n_sample = bs * dec_seq
    n_total = n_prompt + n_sample
    nk = n_total * TOP_K
    n_blocks = -(-nk // MOE_ROWS) + NUM_EXPERTS
    m_pad = n_blocks * MOE_ROWS

    w_glr = jnp.zeros((D_MODEL, LANES), BF16).at[:, :GATE_RANK].set(w_in[0, :, N_MAIN:].astype(BF16))
    w_gate = jnp.zeros((LANES, GLA_HEADS * GLA_DK), BF16).at[:GATE_RANK].set(w_gate_up[0].astype(BF16))
    weights = (
        w_in[0, :, :N_MAIN].astype(BF16), w_glr, w_gate, b_gate[0][None, :],
        w_pool[0].astype(BF16), pool_scale[0][None, :], gla_norm_w[0][None, :],
        w_out[0].astype(BF16), ln1_g[0][None, :], ln1_b[0][None, :],
        w_router[0].T.astype(BF16), jnp.broadcast_to(b_router[0][:, None], (NUM_EXPERTS, LANES)),
    )
    assert len(weights) == N_MIXER_WEIGHTS

    *shared, hist_p, s_p = _mixer_prompt(x_prompt, weights, n_total)
    h_all, hb_all, logits_t, hist_s, s_s = _mixer_sample(
        x_sample, state_pool[0], state_gla[0], weights, shared, n_prompt)

    gates_t, dest_t, pad_end = _router(logits_t)
    pad_end = pad_end[:, 0].astype(jnp.int32)
    block_start = jnp.arange(n_blocks, dtype=jnp.int32) * MOE_ROWS
    block_expert = jnp.minimum(jnp.sum((block_start[:, None] >= pad_end[None, :]).astype(jnp.int32), axis=1),
                               NUM_EXPERTS - 1)
    n_used = (pad_end[-1:] // MOE_ROWS).astype(jnp.int32)

    dest_chunks = dest_t.reshape(TOP_K, n_total // DISPATCH_ROWS, DISPATCH_ROWS).transpose(1, 0, 2)
    x_sorted = _dispatch(hb_all, dest_chunks, m_pad)
    y_sorted = _moe_experts(block_expert, n_used, x_sorted, w_gu[0], b_gu[0][:, None, :],
                            w_down[0], b_down[0][:, None, :])
    yk = y_sorted[dest_t]
    gates = gates_t.T
    ln_g, ln_b = ln2_g[0][None, :], ln2_b[0][None, :]
    y_prompt = _combine(yk, gates, h_all, ln_g, ln_b, 0, n_prompt).reshape(bp, seq, D_MODEL)
    y_sample = _combine(yk, gates, h_all, ln_g, ln_b, n_prompt, n_sample).reshape(bs, dec_seq, D_MODEL)
    return (y_prompt, y_sample, hist_p[None], s_p[None], hist_s[None], s_s[None])
```

```python
import functools

import jax
import jax.numpy as jnp
from jax import lax
from jax.experimental import pallas as pl
from jax.experimental.pallas import tpu as pltpu
from jax.experimental.pallas import tpu_sc as plsc

F32 = jnp.float32
BF16 = jnp.bfloat16

D_MODEL = 1024
CHUNK = 64
PAST_LEN = 1024
POOL_WIDTH = 512
POOL_WINDOWS = (2, 4, 8, 16)
POOL_GROUP = 128
POOL_HIST = 15
GLA_HEADS = 4
GLA_DK = 64
GLA_DV = 128
GATE_RANK = 16
GATE_NORMALIZER = 16.0
NUM_EXPERTS = 32
TOP_K = 4
EXPERT_FF = 1024
SWIGLU_LIMIT = 7.0
SWIGLU_ALPHA = 1.702
LN_EPS = 1e-5
RMS_EPS = 1e-6
ALPHA = 2.0 ** 0.25

Q0 = POOL_WIDTH
K0 = Q0 + GLA_HEADS * GLA_DK
V0 = K0 + GLA_HEADS * GLA_DK
R0 = V0 + GLA_HEADS * GLA_DV
N_MAIN = R0 + GLA_HEADS * GLA_DV

LANES = 128
TILE_TOKENS = 512
CHUNKS_PER_TILE = TILE_TOKENS // CHUNK
HIST_PAD = 16
MOE_ROWS = 512
COMBINE_TOKENS = 256
VMEM_LIMIT = 56 * 1024 * 1024


def _dot(a, b):
    return jnp.dot(a, b, preferred_element_type=F32)


def _dot_nt(a, b):
    return lax.dot_general(a, b, (((1,), (1,)), ((), ())), preferred_element_type=F32)


def _dot_tn(a, b):
    return lax.dot_general(a, b, (((0,), (0,)), ((), ())), preferred_element_type=F32)


HALF = D_MODEL // 2
HI_MASK = 0xFFFF0000


def _pack_bf16_pairs(xb):
    lo = lax.bitcast_convert_type(xb[:, :HALF].astype(F32), jnp.uint32) >> 16
    hi = lax.bitcast_convert_type(xb[:, HALF:].astype(F32), jnp.uint32) & jnp.uint32(HI_MASK)
    return lax.bitcast_convert_type(hi | lo, jnp.int32)


def _unpack_bf16_pairs(p):
    u = lax.bitcast_convert_type(p, jnp.uint32)
    lo = lax.bitcast_convert_type(u << 16, F32)
    hi = lax.bitcast_convert_type(u & jnp.uint32(HI_MASK), F32)
    return jnp.concatenate([lo, hi], axis=1).astype(BF16)


def _layer_norm(v, g, b):
    mu = jnp.mean(v, axis=-1, keepdims=True)
    c = v - mu
    var = jnp.mean(c * c, axis=-1, keepdims=True)
    return c * lax.rsqrt(var + LN_EPS) * g + b


N_MIXER_WEIGHTS = 12
N_MIXER_SHARED_OUT = 3


def _mixer_kernel(per_chunk_state, pos0, *refs):
    if per_chunk_state:
        (x_ref, hist_in_ref, s_in_ref, *rest) = refs
    else:
        (x_ref, *rest) = refs
        hist_in_ref = s_in_ref = None
    (w_main_ref, w_glr_ref, w_gate_ref, b_gate_ref, w_pool_ref, pscale_ref, gnorm_ref,
     w_out_ref, ln1g_ref, ln1b_ref, w_router_ref, b_router_ref, *rest) = rest
    if per_chunk_state:
        rest = rest[N_MIXER_SHARED_OUT:]
    (h_ref, hb_ref, logits_ref, hist_out_ref, s_out_ref,
     proj_scr, g_scr, o_scr, ext_scr, st_scr) = rest

    t = pl.program_id(1) if not per_chunk_state else None
    x = x_ref[...].reshape(TILE_TOKENS, D_MODEL)
    xb = x.astype(BF16)

    proj_scr[...] = _dot(xb, w_main_ref[...])
    glr = _dot(xb, w_glr_ref[...])
    gk = _dot(glr.astype(BF16), w_gate_ref[...]) + b_gate_ref[...]
    log_sig = jnp.minimum(gk, 0.0) - jnp.log1p(jnp.exp(-jnp.abs(gk)))
    g_scr[...] = log_sig / GATE_NORMALIZER

    if per_chunk_state:
        seg_len, seg_stride, n_seg = CHUNK, CHUNK + HIST_PAD, CHUNKS_PER_TILE
        for c in range(n_seg):
            base = c * seg_stride
            ext_scr[base:base + HIST_PAD, :] = jnp.zeros((HIST_PAD, POOL_WIDTH), F32)
            ext_scr[base + 1:base + HIST_PAD, :] = hist_in_ref[c]
            ext_scr[base + HIST_PAD:base + seg_stride, :] = proj_scr[c * CHUNK:(c + 1) * CHUNK, 0:POOL_WIDTH]
        row_pos = pos0 + lax.broadcasted_iota(jnp.int32, (seg_len, POOL_GROUP), 0)
    else:
        seg_len, seg_stride, n_seg = TILE_TOKENS, TILE_TOKENS + HIST_PAD, 1

        @pl.when(t == 0)
        def _():
            ext_scr[0:HIST_PAD, :] = jnp.zeros((HIST_PAD, POOL_WIDTH), F32)

        ext_scr[HIST_PAD:seg_stride, :] = proj_scr[:, 0:POOL_WIDTH]
        row_pos = pos0 + t * TILE_TOKENS + lax.broadcasted_iota(jnp.int32, (seg_len, POOL_GROUP), 0)

    pool_cols = []
    for gi, w in enumerate(POOL_WINDOWS):
        gs = slice(gi * POOL_GROUP, (gi + 1) * POOL_GROUP)
        cnt = jnp.minimum(row_pos + 1, w).astype(F32)
        segs = []
        for s in range(n_seg):
            base = s * seg_stride + HIST_PAD
            acc = ext_scr[base:base + seg_len, gs]
            cur = acc
            for k in range(1, w):
                acc = acc + ext_scr[base - k:base - k + seg_len, gs]
            segs.append(acc / cnt - cur)
        pooled = segs[0] if n_seg == 1 else jnp.concatenate(segs, axis=0)
        z = _dot(pooled.astype(BF16), w_pool_ref[gi])
        pool_cols.append(z * pscale_ref[:, gs])
    pool_out = jnp.concatenate(pool_cols, axis=1)

    if per_chunk_state:
        for c in range(n_seg):
            end = (c + 1) * seg_stride
            hist_out_ref[c] = ext_scr[end - POOL_HIST:end, :]
    else:
        @pl.when(t == pl.num_programs(1) - 1)
        def _():
            hist_out_ref[...] = ext_scr[seg_stride - POOL_HIST:seg_stride, :]

        ext_scr[0:HIST_PAD, :] = ext_scr[TILE_TOKENS:seg_stride, :]

    if not per_chunk_state:
        @pl.when(t == 0)
        def _():
            st_scr[...] = jnp.zeros_like(st_scr)

    ii = lax.broadcasted_iota(jnp.int32, (CHUNK, CHUNK), 0)
    jj = lax.broadcasted_iota(jnp.int32, (CHUNK, CHUNK), 1)
    causal = ii >= jj
    tri = jnp.where(causal, 1.0, 0.0).astype(BF16)

    def chunk_body(c, carry):
        r0 = pl.multiple_of(c * CHUNK, CHUNK)
        rows = pl.ds(r0, CHUNK)
        gc = g_scr[rows, :]
        g1 = gc.astype(BF16)
        e1 = gc - g1.astype(F32)
        g2 = e1.astype(BF16)
        g3 = (e1 - g2.astype(F32)).astype(BF16)
        b = _dot(tri, g1) + _dot(tri, g2) + _dot(tri, g3)
        b_last = b[CHUNK - 1:CHUNK, :]
        q = proj_scr[rows, Q0:K0] * (GLA_DK ** -0.5)
        k = proj_scr[rows, K0:V0]
        qt = (q * jnp.exp(b)).astype(BF16)
        kt = (k * jnp.exp(-b)).astype(BF16)
        kl = (k * jnp.exp(b_last - b)).astype(BF16)
        e_last = jnp.exp(b_last)
        v = proj_scr[rows, V0:R0].astype(BF16)
        if per_chunk_state:
            st = jnp.transpose(s_in_ref[c].reshape(GLA_HEADS * GLA_DK, GLA_DV))
        else:
            st = st_scr[...]
        new_cols = []
        for h in range(GLA_HEADS):
            ks = slice(h * GLA_DK, (h + 1) * GLA_DK)
            vs = slice(h * GLA_DV, (h + 1) * GLA_DV)
            att = jnp.where(causal, _dot_nt(qt[:, ks], kt[:, ks]), 0.0)
            st_h = st[:, ks]
            o = _dot(att.astype(BF16), v[:, vs]) + _dot_nt(qt[:, ks], st_h.astype(BF16))
            o_scr[rows, vs] = o
            new_cols.append(st_h * e_last[:, ks] + _dot_tn(v[:, vs], kl[:, ks]))
        st_new = jnp.concatenate(new_cols, axis=1)
        if per_chunk_state:
            s_out_ref[c] = jnp.transpose(st_new).reshape(GLA_HEADS, GLA_DK, GLA_DV)
        else:
            st_scr[...] = st_new
        return carry

    lax.fori_loop(0, CHUNKS_PER_TILE, chunk_body, 0, unroll=True)

    if not per_chunk_state:
        @pl.when(t == pl.num_programs(1) - 1)
        def _():
            s_out_ref[...] = jnp.transpose(st_scr[...]).reshape(GLA_HEADS, GLA_DK, GLA_DV)

    r = proj_scr[:, R0:N_MAIN]
    silu_r = r * (1.0 / (1.0 + jnp.exp(-r)))
    gated = []
    for h in range(GLA_HEADS):
        vs = slice(h * GLA_DV, (h + 1) * GLA_DV)
        oh = o_scr[:, vs]
        ms = jnp.mean(oh * oh, axis=-1, keepdims=True)
        gated.append(oh * lax.rsqrt(ms + RMS_EPS) * gnorm_ref[...] * silu_r[:, vs])
    mix_in = jnp.concatenate([pool_out] + gated, axis=1).astype(BF16)
    resid = ALPHA * x + _dot(mix_in, w_out_ref[...])
    h_val = _layer_norm(resid, ln1g_ref[...], ln1b_ref[...])
    h_ref[...] = h_val
    hb = h_val.astype(BF16)
    hb_ref[...] = _pack_bf16_pairs(hb)
    logits_ref[...] = _dot_nt(w_router_ref[...], hb) + b_router_ref[:, 0:1]


def _const_spec(shape):
    nd = len(shape)
    return pl.BlockSpec(shape, lambda *_: (0,) * nd)


def _mixer_weight_specs():
    return [
        _const_spec((D_MODEL, N_MAIN)),
        _const_spec((D_MODEL, LANES)),
        _const_spec((LANES, GLA_HEADS * GLA_DK)),
        _const_spec((1, GLA_HEADS * GLA_DK)),
        _const_spec((len(POOL_WINDOWS), POOL_GROUP, POOL_GROUP)),
        _const_spec((1, POOL_WIDTH)),
        _const_spec((1, GLA_DV)),
        _const_spec((D_MODEL, D_MODEL)),
        _const_spec((1, D_MODEL)),
        _const_spec((1, D_MODEL)),
        _const_spec((NUM_EXPERTS, D_MODEL)),
        _const_spec((NUM_EXPERTS, LANES)),
    ]


def _mixer_scratch(per_chunk_state):
    ext_rows = (CHUNKS_PER_TILE * (CHUNK + HIST_PAD)) if per_chunk_state else (TILE_TOKENS + HIST_PAD)
    return [
        pltpu.VMEM((TILE_TOKENS, N_MAIN), F32),
        pltpu.VMEM((TILE_TOKENS, GLA_HEADS * GLA_DK), F32),
        pltpu.VMEM((TILE_TOKENS, GLA_HEADS * GLA_DV), F32),
        pltpu.VMEM((ext_rows, POOL_WIDTH), F32),
        pltpu.VMEM((GLA_DV, GLA_HEADS * GLA_DK), F32),
    ]


def _mixer_out_shapes(n_total, bsz):
    return (
        jax.ShapeDtypeStruct((n_total, D_MODEL), F32),
        jax.ShapeDtypeStruct((n_total, HALF), jnp.int32),
        jax.ShapeDtypeStruct((NUM_EXPERTS, n_total), F32),
        jax.ShapeDtypeStruct((bsz, POOL_HIST, POOL_WIDTH), F32),
        jax.ShapeDtypeStruct((bsz, GLA_HEADS, GLA_DK, GLA_DV), F32),
    )


def _mixer_prompt(x, weights, n_total):
    bsz, seq, _ = x.shape
    tiles = seq // TILE_TOKENS
    return pl.pallas_call(
        functools.partial(_mixer_kernel, False, 0),
        grid=(bsz, tiles),
        in_specs=[pl.BlockSpec((None, TILE_TOKENS, D_MODEL), lambda b, t: (b, t, 0))] + _mixer_weight_specs(),
        out_specs=(
            pl.BlockSpec((TILE_TOKENS, D_MODEL), lambda b, t: (b * tiles + t, 0)),
            pl.BlockSpec((TILE_TOKENS, HALF), lambda b, t: (b * tiles + t, 0)),
            pl.BlockSpec((NUM_EXPERTS, TILE_TOKENS), lambda b, t: (0, b * tiles + t)),
            pl.BlockSpec((None, POOL_HIST, POOL_WIDTH), lambda b, t: (b, 0, 0)),
            pl.BlockSpec((None, GLA_HEADS, GLA_DK, GLA_DV), lambda b, t: (b, 0, 0, 0)),
        ),
        out_shape=_mixer_out_shapes(n_total, bsz),
        scratch_shapes=_mixer_scratch(False),
        compiler_params=pltpu.CompilerParams(
            dimension_semantics=("arbitrary", "arbitrary"), vmem_limit_bytes=VMEM_LIMIT),
        name="mixer_prompt",
    )(x, *weights)


def _mixer_sample(x, hist, state, weights, shared, row_offset):
    bsz = x.shape[0]
    tiles = bsz // CHUNKS_PER_TILE
    tile0 = row_offset // TILE_TOKENS
    n_total = shared[0].shape[0]
    n_in = 3 + N_MIXER_WEIGHTS
    return pl.pallas_call(
        functools.partial(_mixer_kernel, True, PAST_LEN),
        grid=(tiles,),
        in_specs=[
            pl.BlockSpec((CHUNKS_PER_TILE, CHUNK, D_MODEL), lambda i: (i, 0, 0)),
            pl.BlockSpec((CHUNKS_PER_TILE, POOL_HIST, POOL_WIDTH), lambda i: (i, 0, 0)),
            pl.BlockSpec((CHUNKS_PER_TILE, GLA_HEADS, GLA_DK, GLA_DV), lambda i: (i, 0, 0, 0)),
        ] + _mixer_weight_specs() + [pl.BlockSpec(memory_space=pl.ANY)] * N_MIXER_SHARED_OUT,
        out_specs=(
            pl.BlockSpec((TILE_TOKENS, D_MODEL), lambda i: (tile0 + i, 0)),
            pl.BlockSpec((TILE_TOKENS, HALF), lambda i: (tile0 + i, 0)),
            pl.BlockSpec((NUM_EXPERTS, TILE_TOKENS), lambda i: (0, tile0 + i)),
            pl.BlockSpec((CHUNKS_PER_TILE, POOL_HIST, POOL_WIDTH), lambda i: (i, 0, 0)),
            pl.BlockSpec((CHUNKS_PER_TILE, GLA_HEADS, GLA_DK, GLA_DV), lambda i: (i, 0, 0, 0)),
        ),
        out_shape=_mixer_out_shapes(n_total, bsz),
        input_output_aliases={n_in + j: j for j in range(N_MIXER_SHARED_OUT)},
        scratch_shapes=_mixer_scratch(True),
        compiler_params=pltpu.CompilerParams(
            dimension_semantics=("arbitrary",), vmem_limit_bytes=VMEM_LIMIT),
        name="mixer_sample",
    )(x, hist, state, *weights, *shared)


def _router_kernel(lt_ref, gates_ref, dest_ref, padend_ref, cnt_scr, base_scr, pstart_scr):
    phase = pl.program_id(0)
    i = pl.program_id(1)
    shape = (NUM_EXPERTS, TILE_TOKENS)
    logits = lt_ref[...]
    row = lax.broadcasted_iota(jnp.int32, shape, 0)
    sel, vals = [], []
    for _ in range(TOP_K):
        m = jnp.max(logits, axis=0, keepdims=True)
        idx = jnp.min(jnp.where(logits == m, row, NUM_EXPERTS), axis=0, keepdims=True)
        hit = row == idx
        sel.append(hit)
        vals.append(m)
        logits = jnp.where(hit, -jnp.inf, logits)
    chosen = sum(jnp.where(hit, 1.0, 0.0) for hit in sel)
    tile_counts = jnp.broadcast_to(jnp.sum(chosen, axis=1, keepdims=True), (NUM_EXPERTS, LANES))

    @pl.when(phase == 0)
    def _():
        @pl.when(i == 0)
        def _():
            cnt_scr[...] = jnp.zeros_like(cnt_scr)

        cnt_scr[...] += tile_counts

    @pl.when(phase == 1)
    def _():
        @pl.when(i == 0)
        def _():
            blocks = jnp.floor((cnt_scr[...] + (MOE_ROWS - 1)) * (1.0 / MOE_ROWS))
            erow = lax.broadcasted_iota(jnp.int32, (NUM_EXPERTS, LANES), 0)
            cum = blocks
            shift = 1
            while shift < NUM_EXPERTS:
                cum = cum + jnp.where(erow >= shift, pltpu.roll(cum, shift, 0), 0.0)
                shift *= 2
            padend_ref[...] = cum * MOE_ROWS
            pstart_scr[...] = (cum - blocks) * MOE_ROWS
            base_scr[...] = jnp.zeros_like(base_scr)

        ti = lax.broadcasted_iota(jnp.int32, (TILE_TOKENS, TILE_TOKENS), 0)
        tj = lax.broadcasted_iota(jnp.int32, (TILE_TOKENS, TILE_TOKENS), 1)
        before = jnp.where(ti < tj, 1.0, 0.0).astype(BF16)
        earlier = _dot(chosen.astype(BF16), before)
        pos = pstart_scr[:, 0:1] + base_scr[:, 0:1] + earlier
        dest = [jnp.sum(jnp.where(hit, pos, 0.0), axis=0, keepdims=True) for hit in sel]
        dest_ref[...] = jnp.concatenate(dest, axis=0).astype(jnp.int32)
        ex = [jnp.exp(v - vals[0]) for v in vals]
        denom = ex[0] + ex[1] + ex[2] + ex[3]
        gates_ref[...] = jnp.concatenate([e / denom for e in ex], axis=0)
        base_scr[...] += tile_counts


def _router(logits_t):
    n = logits_t.shape[1]
    tiles = n // TILE_TOKENS
    return pl.pallas_call(
        _router_kernel,
        grid=(2, tiles),
        in_specs=[pl.BlockSpec((NUM_EXPERTS, TILE_TOKENS), lambda p, i: (0, i))],
        out_specs=(
            pl.BlockSpec((TOP_K, TILE_TOKENS), lambda p, i: (0, i * p)),
            pl.BlockSpec((TOP_K, TILE_TOKENS), lambda p, i: (0, i * p)),
            pl.BlockSpec((NUM_EXPERTS, LANES), lambda p, i: (0, 0)),
        ),
        out_shape=(
            jax.ShapeDtypeStruct((TOP_K, n), F32),
            jax.ShapeDtypeStruct((TOP_K, n), jnp.int32),
            jax.ShapeDtypeStruct((NUM_EXPERTS, LANES), F32),
        ),
        scratch_shapes=[pltpu.VMEM((NUM_EXPERTS, LANES), F32)] * 3,
        compiler_params=pltpu.CompilerParams(
            dimension_semantics=("arbitrary", "arbitrary"), vmem_limit_bytes=VMEM_LIMIT),
        name="router",
    )(logits_t)


SC_CORES = 2
SC_SUBCORES = 16
SC_WORKERS = SC_CORES * SC_SUBCORES
DISPATCH_ROWS = 64


def _dispatch(h_packed, dest_chunks, m_pad):
    n = h_packed.shape[0]
    n_chunks = n // DISPATCH_ROWS
    assert n_chunks % SC_WORKERS == 0
    per_worker = n_chunks // SC_WORKERS
    mesh = plsc.VectorSubcoreMesh(core_axis_name="c", subcore_axis_name="s")

    @functools.partial(
        pl.kernel, mesh=mesh,
        out_type=jax.ShapeDtypeStruct((m_pad, HALF), jnp.int32),
        scratch_types=[
            pltpu.VMEM((TOP_K, DISPATCH_ROWS), jnp.int32),
            pltpu.VMEM((DISPATCH_ROWS, HALF), jnp.int32),
            pltpu.SemaphoreType.DMA,
        ],
        compiler_params=pltpu.CompilerParams(use_tc_tiling_on_sc=True),
        name="dispatch",
    )
    def dispatch_kernel(h_hbm, dest_hbm, out_hbm, idx_v, rows_v, sem):
        wid = lax.axis_index("s") * SC_CORES + lax.axis_index("c")

        @pl.loop(0, per_worker)
        def _(j):
            chunk = wid * per_worker + j
            pltpu.sync_copy(dest_hbm.at[chunk], idx_v)
            pltpu.sync_copy(h_hbm.at[pl.ds(chunk * DISPATCH_ROWS, DISPATCH_ROWS)], rows_v)
            for k in range(TOP_K):
                pltpu.async_copy(rows_v, out_hbm.at[idx_v.at[k]], sem).wait()

    return dispatch_kernel(h_packed, dest_chunks)


def _gather_expert_rows(y_sorted, dest_chunks, n):
    n_chunks = n // DISPATCH_ROWS
    assert n_chunks % SC_WORKERS == 0
    per_worker = n_chunks // SC_WORKERS
    mesh = plsc.VectorSubcoreMesh(core_axis_name="c", subcore_axis_name="s")

    @functools.partial(
        pl.kernel, mesh=mesh,
        out_type=jax.ShapeDtypeStruct((TOP_K, n, HALF), jnp.int32),
        scratch_types=[
            pltpu.VMEM((TOP_K, DISPATCH_ROWS), jnp.int32),
            pltpu.VMEM((DISPATCH_ROWS, HALF), jnp.int32),
            pltpu.SemaphoreType.DMA,
        ],
        compiler_params=pltpu.CompilerParams(use_tc_tiling_on_sc=True),
        name="gather_expert_rows",
    )
    def gather_kernel(y_hbm, dest_hbm, out_hbm, idx_v, rows_v, sem):
        wid = lax.axis_index("s") * SC_CORES + lax.axis_index("c")

        @pl.loop(0, per_worker)
        def _(j):
            chunk = wid * per_worker + j
            pltpu.sync_copy(dest_hbm.at[chunk], idx_v)
            for k in range(TOP_K):
                pltpu.async_copy(y_hbm.at[idx_v.at[k]], rows_v, sem).wait()
                pltpu.sync_copy(rows_v, out_hbm.at[k, pl.ds(chunk * DISPATCH_ROWS, DISPATCH_ROWS)])

    return gather_kernel(y_sorted, dest_chunks)


def _moe_kernel(be_ref, nused_ref, x_ref, wgu_ref, bgu_ref, wd_ref, bd_ref, y_ref, wgu_bf, wd_bf):
    i = pl.program_id(0)

    @pl.when(i < nused_ref[0])
    def _():
        @pl.when((i == 0) | (be_ref[i] != be_ref[jnp.maximum(i - 1, 0)]))
        def _():
            wgu_bf[...] = wgu_ref[...].astype(BF16)
            wd_bf[...] = wd_ref[...].astype(BF16)

        gu = _dot(_unpack_bf16_pairs(x_ref[...]), wgu_bf[...]) + bgu_ref[...]
        gate = jnp.minimum(gu[:, :EXPERT_FF], SWIGLU_LIMIT)
        up = jnp.clip(gu[:, EXPERT_FF:], -SWIGLU_LIMIT, SWIGLU_LIMIT)
        hmid = gate * (1.0 / (1.0 + jnp.exp(-SWIGLU_ALPHA * gate))) * (up + 1.0)
        y = _dot(hmid.astype(BF16), wd_bf[...]) + bd_ref[...]
        y_ref[...] = _pack_bf16_pairs(y.astype(BF16))


def _moe_experts(block_expert, n_used, x_sorted, w_gu, b_gu, w_down, b_down):
    m_pad = x_sorted.shape[0]
    n_blocks = m_pad // MOE_ROWS

    def blk(i, be, nu):
        return jnp.minimum(i, nu[0] - 1)

    def expert(i, be, nu):
        return be[blk(i, be, nu)]

    grid_spec = pltpu.PrefetchScalarGridSpec(
        num_scalar_prefetch=2,
        grid=(n_blocks,),
        in_specs=[
            pl.BlockSpec((MOE_ROWS, HALF), lambda i, be, nu: (blk(i, be, nu), 0)),
            pl.BlockSpec((None, D_MODEL, 2 * EXPERT_FF), lambda i, be, nu: (expert(i, be, nu), 0, 0)),
            pl.BlockSpec((None, 1, 2 * EXPERT_FF), lambda i, be, nu: (expert(i, be, nu), 0, 0)),
            pl.BlockSpec((None, EXPERT_FF, D_MODEL), lambda i, be, nu: (expert(i, be, nu), 0, 0)),
            pl.BlockSpec((None, 1, D_MODEL), lambda i, be, nu: (expert(i, be, nu), 0, 0)),
        ],
        out_specs=pl.BlockSpec((MOE_ROWS, HALF), lambda i, be, nu: (blk(i, be, nu), 0)),
        scratch_shapes=[
            pltpu.VMEM((D_MODEL, 2 * EXPERT_FF), BF16),
            pltpu.VMEM((EXPERT_FF, D_MODEL), BF16),
        ],
    )
    return pl.pallas_call(
        _moe_kernel,
        grid_spec=grid_spec,
        out_shape=jax.ShapeDtypeStruct((m_pad, HALF), jnp.int32),
        compiler_params=pltpu.CompilerParams(
            dimension_semantics=("arbitrary",), vmem_limit_bytes=VMEM_LIMIT),
        name="moe_experts",
    )(block_expert, n_used, x_sorted, w_gu, b_gu, w_down, b_down)


def _combine_kernel(yk_ref, gates_ref, h_ref, g_ref, b_ref, out_ref):
    gates = gates_ref[...]
    lo = hi = None
    for k in range(TOP_K):
        u = lax.bitcast_convert_type(yk_ref[k], jnp.uint32)
        gk = gates[:, k:k + 1]
        lo_k = lax.bitcast_convert_type(u << 16, F32) * gk
        hi_k = lax.bitcast_convert_type(u & jnp.uint32(HI_MASK), F32) * gk
        lo = lo_k if lo is None else lo + lo_k
        hi = hi_k if hi is None else hi + hi_k
    acc = ALPHA * h_ref[...] + jnp.concatenate([lo, hi], axis=1)
    out_ref[...] = _layer_norm(acc, g_ref[...], b_ref[...])


def _combine(yk, gates, h_all, ln_g, ln_b, row_offset, n_rows):
    tile0 = row_offset // COMBINE_TOKENS
    return pl.pallas_call(
        _combine_kernel,
        grid=(n_rows // COMBINE_TOKENS,),
        in_specs=[
            pl.BlockSpec((TOP_K, COMBINE_TOKENS, HALF), lambda i: (0, tile0 + i, 0)),
            pl.BlockSpec((COMBINE_TOKENS, TOP_K), lambda i: (tile0 + i, 0)),
            pl.BlockSpec((COMBINE_TOKENS, D_MODEL), lambda i: (tile0 + i, 0)),
            _const_spec((1, D_MODEL)),
            _const_spec((1, D_MODEL)),
        ],
        out_specs=pl.BlockSpec((COMBINE_TOKENS, D_MODEL), lambda i: (i, 0)),
        out_shape=jax.ShapeDtypeStruct((n_rows, D_MODEL), F32),
        compiler_params=pltpu.CompilerParams(
            dimension_semantics=("arbitrary",), vmem_limit_bytes=VMEM_LIMIT),
        name="moe_combine",
    )(yk, gates, h_all, ln_g, ln_b)


def kernel(x_prompt, x_sample, state_pool, state_gla, w_in, w_pool, pool_scale, w_gate_up, b_gate,
           gla_norm_w, w_out, ln1_g, ln1_b, w_router, b_router, w_gu, b_gu, w_down, b_down,
           ln2_g, ln2_b):
    assert w_in.shape[0] == 1, "single-layer kernel"
    bp, seq, _ = x_prompt.shape
    bs, dec_seq, _ = x_sample.shape
    assert dec_seq == CHUNK and seq % TILE_TOKENS == 0 and bs % CHUNKS_PER_TILE == 0
    n_prompt = bp * seq
    n_sample = bs * dec_seq
    n_total = n_prompt + n_sample
    nk = n_total * TOP_K
    n_blocks = -(-nk // MOE_ROWS) + NUM_EXPERTS
    m_pad = n_blocks * MOE_ROWS

    w_glr = jnp.zeros((D_MODEL, LANES), BF16).at[:, :GATE_RANK].set(w_in[0, :, N_MAIN:].astype(BF16))
    w_gate = jnp.zeros((LANES, GLA_HEADS * GLA_DK), BF16).at[:GATE_RANK].set(w_gate_up[0].astype(BF16))
    weights = (
        w_in[0, :, :N_MAIN].astype(BF16), w_glr, w_gate, b_gate[0][None, :],
        w_pool[0].astype(BF16), pool_scale[0][None, :], gla_norm_w[0][None, :],
        w_out[0].astype(BF16), ln1_g[0][None, :], ln1_b[0][None, :],
        w_router[0].T.astype(BF16), jnp.broadcast_to(b_router[0][:, None], (NUM_EXPERTS, LANES)),
    )
    assert len(weights) == N_MIXER_WEIGHTS

    *shared, hist_p, s_p = _mixer_prompt(x_prompt, weights, n_total)
    h_all, hb_all, logits_t, hist_s, s_s = _mixer_sample(
        x_sample, state_pool[0], state_gla[0], weights, shared, n_prompt)

    gates_t, dest_t, pad_end = _router(logits_t)
    pad_end = pad_end[:, 0].astype(jnp.int32)
    block_start = jnp.arange(n_blocks, dtype=jnp.int32) * MOE_ROWS
    block_expert = jnp.minimum(jnp.sum((block_start[:, None] >= pad_end[None, :]).astype(jnp.int32), axis=1),
                               NUM_EXPERTS - 1)
    n_used = (pad_end[-1:] // MOE_ROWS).astype(jnp.int32)

    dest_chunks = dest_t.reshape(TOP_K, n_total // DISPATCH_ROWS, DISPATCH_ROWS).transpose(1, 0, 2)
    x_sorted = _dispatch(hb_all, dest_chunks, m_pad)
    y_sorted = _moe_experts(block_expert, n_used, x_sorted, w_gu[0], b_gu[0][:, None, :],
                            w_down[0], b_down[0][:, None, :])
    yk = _gather_expert_rows(y_sorted, dest_chunks, n_total)
    gates = gates_t.T
    ln_g, ln_b = ln2_g[0][None, :], ln2_b[0][None, :]
    y_prompt = _combine(yk, gates, h_all, ln_g, ln_b, 0, n_prompt).reshape(bp, seq, D_MODEL)
    y_sample = _combine(yk, gates, h_all, ln_g, ln_b, n_prompt, n_sample).reshape(bs, dec_seq, D_MODEL)
    return (y_prompt, y_sample, hist_p[None], s_p[None], hist_s[None], s_s[None])
```

```python
import functools

import jax
import jax.numpy as jnp
from jax import lax
from jax.experimental import pallas as pl
from jax.experimental.pallas import tpu as pltpu
from jax.experimental.pallas import tpu_sc as plsc

F32 = jnp.float32
BF16 = jnp.bfloat16

D_MODEL = 1024
CHUNK = 64
PAST_LEN = 1024
POOL_WIDTH = 512
POOL_WINDOWS = (2, 4, 8, 16)
POOL_GROUP = 128
POOL_HIST = 15
GLA_HEADS = 4
GLA_DK = 64
GLA_DV = 128
GATE_RANK = 16
GATE_NORMALIZER = 16.0
NUM_EXPERTS = 32
TOP_K = 4
EXPERT_FF = 1024
SWIGLU_LIMIT = 7.0
SWIGLU_ALPHA = 1.702
LN_EPS = 1e-5
RMS_EPS = 1e-6
ALPHA = 2.0 ** 0.25

Q0 = POOL_WIDTH
K0 = Q0 + GLA_HEADS * GLA_DK
V0 = K0 + GLA_HEADS * GLA_DK
R0 = V0 + GLA_HEADS * GLA_DV
N_MAIN = R0 + GLA_HEADS * GLA_DV

LANES = 128
TILE_TOKENS = 512
CHUNKS_PER_TILE = TILE_TOKENS // CHUNK
HIST_PAD = 16
MOE_ROWS = 512
COMBINE_TOKENS = 256
VMEM_LIMIT = 56 * 1024 * 1024


def _dot(a, b):
    return jnp.dot(a, b, preferred_element_type=F32)


def _dot_nt(a, b):
    return lax.dot_general(a, b, (((1,), (1,)), ((), ())), preferred_element_type=F32)


def _dot_tn(a, b):
    return lax.dot_general(a, b, (((0,), (0,)), ((), ())), preferred_element_type=F32)


HALF = D_MODEL // 2
HI_MASK = 0xFFFF0000


def _pack_bf16_pairs(xb):
    lo = lax.bitcast_convert_type(xb[:, :HALF].astype(F32), jnp.uint32) >> 16
    hi = lax.bitcast_convert_type(xb[:, HALF:].astype(F32), jnp.uint32) & jnp.uint32(HI_MASK)
    return lax.bitcast_convert_type(hi | lo, jnp.int32)


def _unpack_bf16_pairs(p):
    u = lax.bitcast_convert_type(p, jnp.uint32)
    lo = lax.bitcast_convert_type(u << 16, F32)
    hi = lax.bitcast_convert_type(u & jnp.uint32(HI_MASK), F32)
    return jnp.concatenate([lo, hi], axis=1).astype(BF16)


def _layer_norm(v, g, b):
    mu = jnp.mean(v, axis=-1, keepdims=True)
    c = v - mu
    var = jnp.mean(c * c, axis=-1, keepdims=True)
    return c * lax.rsqrt(var + LN_EPS) * g + b


N_MIXER_WEIGHTS = 12
N_MIXER_SHARED_OUT = 3


def _mixer_kernel(per_chunk_state, pos0, *refs):
    if per_chunk_state:
        (x_ref, hist_in_ref, s_in_ref, *rest) = refs
    else:
        (x_ref, *rest) = refs
        hist_in_ref = s_in_ref = None
    (w_main_ref, w_glr_ref, w_gate_ref, b_gate_ref, w_pool_ref, pscale_ref, gnorm_ref,
     w_out_ref, ln1g_ref, ln1b_ref, w_router_ref, b_router_ref, *rest) = rest
    if per_chunk_state:
        rest = rest[N_MIXER_SHARED_OUT:]
    (h_ref, hb_ref, logits_ref, hist_out_ref, s_out_ref,
     proj_scr, b_scr, o_scr, ext_scr, st_scr, tri_scr) = rest

    if per_chunk_state:
        t = None
        first_step = pl.program_id(0) == 0
    else:
        t = pl.program_id(1)
        first_step = (pl.program_id(0) == 0) & (t == 0)
    x = x_ref[...].reshape(TILE_TOKENS, D_MODEL)
    xb = x.astype(BF16)

    @pl.when(first_step)
    def _():
        ti = lax.broadcasted_iota(jnp.int32, (TILE_TOKENS, TILE_TOKENS), 0)
        tj = lax.broadcasted_iota(jnp.int32, (TILE_TOKENS, TILE_TOKENS), 1)
        same_chunk = (ti // CHUNK) == (tj // CHUNK)
        tri_scr[...] = jnp.where(same_chunk & (ti >= tj), 1.0, 0.0).astype(BF16)

    proj_scr[...] = _dot(xb, w_main_ref[...])
    glr = _dot(xb, w_glr_ref[...])
    gk = _dot(glr.astype(BF16), w_gate_ref[...]) + b_gate_ref[...]
    log_sig = jnp.minimum(gk, 0.0) - jnp.log1p(jnp.exp(-jnp.abs(gk)))
    g = log_sig / GATE_NORMALIZER
    g_hi = g.astype(BF16)
    g_lo = (g - g_hi.astype(F32)).astype(BF16)
    b_scr[...] = _dot(tri_scr[...], g_hi) + _dot(tri_scr[...], g_lo)

    if per_chunk_state:
        seg_len, seg_stride, n_seg = CHUNK, CHUNK + HIST_PAD, CHUNKS_PER_TILE
        for c in range(n_seg):
            base = c * seg_stride
            ext_scr[base:base + HIST_PAD, :] = jnp.zeros((HIST_PAD, POOL_WIDTH), F32)
            ext_scr[base + 1:base + HIST_PAD, :] = hist_in_ref[c]
            ext_scr[base + HIST_PAD:base + seg_stride, :] = proj_scr[c * CHUNK:(c + 1) * CHUNK, 0:POOL_WIDTH]
        row_pos = pos0 + lax.broadcasted_iota(jnp.int32, (seg_len, POOL_GROUP), 0)
    else:
        seg_len, seg_stride, n_seg = TILE_TOKENS, TILE_TOKENS + HIST_PAD, 1

        @pl.when(t == 0)
        def _():
            ext_scr[0:HIST_PAD, :] = jnp.zeros((HIST_PAD, POOL_WIDTH), F32)

        ext_scr[HIST_PAD:seg_stride, :] = proj_scr[:, 0:POOL_WIDTH]
        row_pos = pos0 + t * TILE_TOKENS + lax.broadcasted_iota(jnp.int32, (seg_len, POOL_GROUP), 0)

    pool_cols = []
    for gi, w in enumerate(POOL_WINDOWS):
        gs = slice(gi * POOL_GROUP, (gi + 1) * POOL_GROUP)
        cnt = jnp.minimum(row_pos + 1, w).astype(F32)
        ext = ext_scr[:, gs]
        win = ext
        shift = 1
        while shift < w:
            win = win + pltpu.roll(win, shift, 0)
            shift *= 2
        segs = []
        for s in range(n_seg):
            base = s * seg_stride + HIST_PAD
            segs.append(win[base:base + seg_len] / cnt - ext[base:base + seg_len])
        pooled = segs[0] if n_seg == 1 else jnp.concatenate(segs, axis=0)
        z = _dot(pooled.astype(BF16), w_pool_ref[gi])
        pool_cols.append(z * pscale_ref[:, gs])
    pool_out = jnp.concatenate(pool_cols, axis=1)

    if per_chunk_state:
        for c in range(n_seg):
            end = (c + 1) * seg_stride
            hist_out_ref[c] = ext_scr[end - POOL_HIST:end, :]
    else:
        @pl.when(t == pl.num_programs(1) - 1)
        def _():
            hist_out_ref[...] = ext_scr[seg_stride - POOL_HIST:seg_stride, :]

        ext_scr[0:HIST_PAD, :] = ext_scr[TILE_TOKENS:seg_stride, :]

    if not per_chunk_state:
        @pl.when(t == 0)
        def _():
            st_scr[...] = jnp.zeros_like(st_scr)

    hk = GLA_HEADS * GLA_DK
    hv = GLA_HEADS * GLA_DV

    def head_of(shape, dim, width):
        return lax.broadcasted_iota(jnp.int32, shape, dim) // width

    same_head_k = head_of((hk, hk), 0, CHUNK) == head_of((hk, hk), 1, GLA_DK)
    same_head_v = head_of((hk, hv), 0, CHUNK) == head_of((hk, hv), 1, GLA_DV)
    same_head_s = head_of((hv, hk), 0, GLA_DV) == head_of((hv, hk), 1, GLA_DK)
    col_head = head_of((GLA_DV, hk), 1, GLA_DK)
    causal = (lax.broadcasted_iota(jnp.int32, (CHUNK, hk), 0)
              >= lax.broadcasted_iota(jnp.int32, (CHUNK, hk), 1) % CHUNK)

    def stack_heads(a, keep):
        return jnp.where(keep, jnp.concatenate([a] * GLA_HEADS, axis=0), 0.0).astype(BF16)

    st = None if per_chunk_state else st_scr[...]
    for c in range(CHUNKS_PER_TILE):
        rows = slice(c * CHUNK, (c + 1) * CHUNK)
        b = b_scr[rows, :]
        b_last = b_scr[(c + 1) * CHUNK - 1:(c + 1) * CHUNK, :]
        q = proj_scr[rows, Q0:K0] * (GLA_DK ** -0.5)
        k = proj_scr[rows, K0:V0]
        v = proj_scr[rows, V0:R0]
        qt = (q * jnp.exp(b)).astype(BF16)
        kl = (k * jnp.exp(b_last - b)).astype(BF16)
        if per_chunk_state:
            st = jnp.transpose(s_in_ref[c].reshape(hk, GLA_DV))
        att = jnp.where(causal, _dot_nt(qt, stack_heads(k * jnp.exp(-b), same_head_k)), 0.0)
        o_scr[rows, :] = (_dot(att.astype(BF16), stack_heads(v, same_head_v))
                          + _dot_nt(qt, stack_heads(st, same_head_s)))
        kv = _dot_tn(v.astype(BF16), kl)
        upd = jnp.zeros((GLA_DV, hk), F32)
        for h in range(GLA_HEADS):
            upd = jnp.where(col_head == h, kv[h * GLA_DV:(h + 1) * GLA_DV, :], upd)
        st = st * jnp.exp(b_last) + upd
        if per_chunk_state:
            s_out_ref[c] = jnp.transpose(st).reshape(GLA_HEADS, GLA_DK, GLA_DV)

    if not per_chunk_state:
        st_scr[...] = st

        @pl.when(t == pl.num_programs(1) - 1)
        def _():
            s_out_ref[...] = jnp.transpose(st).reshape(GLA_HEADS, GLA_DK, GLA_DV)

    r = proj_scr[:, R0:N_MAIN]
    silu_r = r * (1.0 / (1.0 + jnp.exp(-r)))
    gated = []
    for h in range(GLA_HEADS):
        vs = slice(h * GLA_DV, (h + 1) * GLA_DV)
        oh = o_scr[:, vs]
        ms = jnp.mean(oh * oh, axis=-1, keepdims=True)
        gated.append(oh * lax.rsqrt(ms + RMS_EPS) * gnorm_ref[...] * silu_r[:, vs])
    mix_in = jnp.concatenate([pool_out] + gated, axis=1).astype(BF16)
    resid = ALPHA * x + _dot(mix_in, w_out_ref[...])
    h_val = _layer_norm(resid, ln1g_ref[...], ln1b_ref[...])
    h_ref[...] = h_val
    hb = h_val.astype(BF16)
    hb_ref[...] = _pack_bf16_pairs(hb)
    logits_ref[...] = _dot_nt(w_router_ref[...], hb) + b_router_ref[:, 0:1]


def _const_spec(shape):
    nd = len(shape)
    return pl.BlockSpec(shape, lambda *_: (0,) * nd)


def _mixer_weight_specs():
    return [
        _const_spec((D_MODEL, N_MAIN)),
        _const_spec((D_MODEL, LANES)),
        _const_spec((LANES, GLA_HEADS * GLA_DK)),
        _const_spec((1, GLA_HEADS * GLA_DK)),
        _const_spec((len(POOL_WINDOWS), POOL_GROUP, POOL_GROUP)),
        _const_spec((1, POOL_WIDTH)),
        _const_spec((1, GLA_DV)),
        _const_spec((D_MODEL, D_MODEL)),
        _const_spec((1, D_MODEL)),
        _const_spec((1, D_MODEL)),
        _const_spec((NUM_EXPERTS, D_MODEL)),
        _const_spec((NUM_EXPERTS, LANES)),
    ]


def _mixer_scratch(per_chunk_state):
    ext_rows = (CHUNKS_PER_TILE * (CHUNK + HIST_PAD)) if per_chunk_state else (TILE_TOKENS + HIST_PAD)
    return [
        pltpu.VMEM((TILE_TOKENS, N_MAIN), F32),
        pltpu.VMEM((TILE_TOKENS, GLA_HEADS * GLA_DK), F32),
        pltpu.VMEM((TILE_TOKENS, GLA_HEADS * GLA_DV), F32),
        pltpu.VMEM((ext_rows, POOL_WIDTH), F32),
        pltpu.VMEM((GLA_DV, GLA_HEADS * GLA_DK), F32),
        pltpu.VMEM((TILE_TOKENS, TILE_TOKENS), BF16),
    ]


def _mixer_out_shapes(n_total, bsz):
    return (
        jax.ShapeDtypeStruct((n_total, D_MODEL), F32),
        jax.ShapeDtypeStruct((n_total, HALF), jnp.int32),
        jax.ShapeDtypeStruct((NUM_EXPERTS, n_total), F32),
        jax.ShapeDtypeStruct((bsz, POOL_HIST, POOL_WIDTH), F32),
        jax.ShapeDtypeStruct((bsz, GLA_HEADS, GLA_DK, GLA_DV), F32),
    )


def _mixer_prompt(x, weights, n_total):
    bsz, seq, _ = x.shape
    tiles = seq // TILE_TOKENS
    return pl.pallas_call(
        functools.partial(_mixer_kernel, False, 0),
        grid=(bsz, tiles),
        in_specs=[pl.BlockSpec((None, TILE_TOKENS, D_MODEL), lambda b, t: (b, t, 0))] + _mixer_weight_specs(),
        out_specs=(
            pl.BlockSpec((TILE_TOKENS, D_MODEL), lambda b, t: (b * tiles + t, 0)),
            pl.BlockSpec((TILE_TOKENS, HALF), lambda b, t: (b * tiles + t, 0)),
            pl.BlockSpec((NUM_EXPERTS, TILE_TOKENS), lambda b, t: (0, b * tiles + t)),
            pl.BlockSpec((None, POOL_HIST, POOL_WIDTH), lambda b, t: (b, 0, 0)),
            pl.BlockSpec((None, GLA_HEADS, GLA_DK, GLA_DV), lambda b, t: (b, 0, 0, 0)),
        ),
        out_shape=_mixer_out_shapes(n_total, bsz),
        scratch_shapes=_mixer_scratch(False),
        compiler_params=pltpu.CompilerParams(
            dimension_semantics=("arbitrary", "arbitrary"), vmem_limit_bytes=VMEM_LIMIT),
        name="mixer_prompt",
    )(x, *weights)


def _mixer_sample(x, hist, state, weights, shared, row_offset):
    bsz = x.shape[0]
    tiles = bsz // CHUNKS_PER_TILE
    tile0 = row_offset // TILE_TOKENS
    n_total = shared[0].shape[0]
    n_in = 3 + N_MIXER_WEIGHTS
    return pl.pallas_call(
        functools.partial(_mixer_kernel, True, PAST_LEN),
        grid=(tiles,),
        in_specs=[
            pl.BlockSpec((CHUNKS_PER_TILE, CHUNK, D_MODEL), lambda i: (i, 0, 0)),
            pl.BlockSpec((CHUNKS_PER_TILE, POOL_HIST, POOL_WIDTH), lambda i: (i, 0, 0)),
            pl.BlockSpec((CHUNKS_PER_TILE, GLA_HEADS, GLA_DK, GLA_DV), lambda i: (i, 0, 0, 0)),
        ] + _mixer_weight_specs() + [pl.BlockSpec(memory_space=pl.ANY)] * N_MIXER_SHARED_OUT,
        out_specs=(
            pl.BlockSpec((TILE_TOKENS, D_MODEL), lambda i: (tile0 + i, 0)),
            pl.BlockSpec((TILE_TOKENS, HALF), lambda i: (tile0 + i, 0)),
            pl.BlockSpec((NUM_EXPERTS, TILE_TOKENS), lambda i: (0, tile0 + i)),
            pl.BlockSpec((CHUNKS_PER_TILE, POOL_HIST, POOL_WIDTH), lambda i: (i, 0, 0)),
            pl.BlockSpec((CHUNKS_PER_TILE, GLA_HEADS, GLA_DK, GLA_DV), lambda i: (i, 0, 0, 0)),
        ),
        out_shape=_mixer_out_shapes(n_total, bsz),
        input_output_aliases={n_in + j: j for j in range(N_MIXER_SHARED_OUT)},
        scratch_shapes=_mixer_scratch(True),
        compiler_params=pltpu.CompilerParams(
            dimension_semantics=("arbitrary",), vmem_limit_bytes=VMEM_LIMIT),
        name="mixer_sample",
    )(x, hist, state, *weights, *shared)


def _router_kernel(lt_ref, gates_ref, dest_ref, padend_ref, cnt_scr, base_scr, pstart_scr):
    phase = pl.program_id(0)
    i = pl.program_id(1)
    shape = (NUM_EXPERTS, TILE_TOKENS)
    logits = lt_ref[...]
    row = lax.broadcasted_iota(jnp.int32, shape, 0)
    sel, vals = [], []
    for _ in range(TOP_K):
        m = jnp.max(logits, axis=0, keepdims=True)
        idx = jnp.min(jnp.where(logits == m, row, NUM_EXPERTS), axis=0, keepdims=True)
        hit = row == idx
        sel.append(hit)
        vals.append(m)
        logits = jnp.where(hit, -jnp.inf, logits)
    chosen = sum(jnp.where(hit, 1.0, 0.0) for hit in sel)
    tile_counts = jnp.broadcast_to(jnp.sum(chosen, axis=1, keepdims=True), (NUM_EXPERTS, LANES))

    @pl.when(phase == 0)
    def _():
        @pl.when(i == 0)
        def _():
            cnt_scr[...] = jnp.zeros_like(cnt_scr)

        cnt_scr[...] += tile_counts

    @pl.when(phase == 1)
    def _():
        @pl.when(i == 0)
        def _():
            blocks = jnp.floor((cnt_scr[...] + (MOE_ROWS - 1)) * (1.0 / MOE_ROWS))
            erow = lax.broadcasted_iota(jnp.int32, (NUM_EXPERTS, LANES), 0)
            cum = blocks
            shift = 1
            while shift < NUM_EXPERTS:
                cum = cum + jnp.where(erow >= shift, pltpu.roll(cum, shift, 0), 0.0)
                shift *= 2
            padend_ref[...] = cum * MOE_ROWS
            pstart_scr[...] = (cum - blocks) * MOE_ROWS
            base_scr[...] = jnp.zeros_like(base_scr)

        ti = lax.broadcasted_iota(jnp.int32, (TILE_TOKENS, TILE_TOKENS), 0)
        tj = lax.broadcasted_iota(jnp.int32, (TILE_TOKENS, TILE_TOKENS), 1)
        before = jnp.where(ti < tj, 1.0, 0.0).astype(BF16)
        earlier = _dot(chosen.astype(BF16), before)
        pos = pstart_scr[:, 0:1] + base_scr[:, 0:1] + earlier
        dest = [jnp.sum(jnp.where(hit, pos, 0.0), axis=0, keepdims=True) for hit in sel]
        dest_ref[...] = jnp.concatenate(dest, axis=0).astype(jnp.int32)
        ex = [jnp.exp(v - vals[0]) for v in vals]
        denom = ex[0] + ex[1] + ex[2] + ex[3]
        gates_ref[...] = jnp.concatenate([e / denom for e in ex], axis=0)
        base_scr[...] += tile_counts


def _router(logits_t):
    n = logits_t.shape[1]
    tiles = n // TILE_TOKENS
    return pl.pallas_call(
        _router_kernel,
        grid=(2, tiles),
        in_specs=[pl.BlockSpec((NUM_EXPERTS, TILE_TOKENS), lambda p, i: (0, i))],
        out_specs=(
            pl.BlockSpec((TOP_K, TILE_TOKENS), lambda p, i: (0, i * p)),
            pl.BlockSpec((TOP_K, TILE_TOKENS), lambda p, i: (0, i * p)),
            pl.BlockSpec((NUM_EXPERTS, LANES), lambda p, i: (0, 0)),
        ),
        out_shape=(
            jax.ShapeDtypeStruct((TOP_K, n), F32),
            jax.ShapeDtypeStruct((TOP_K, n), jnp.int32),
            jax.ShapeDtypeStruct((NUM_EXPERTS, LANES), F32),
        ),
        scratch_shapes=[pltpu.VMEM((NUM_EXPERTS, LANES), F32)] * 3,
        compiler_params=pltpu.CompilerParams(
            dimension_semantics=("arbitrary", "arbitrary"), vmem_limit_bytes=VMEM_LIMIT),
        name="router",
    )(logits_t)


SC_CORES = 2
SC_SUBCORES = 16
SC_WORKERS = SC_CORES * SC_SUBCORES
DISPATCH_ROWS = 64


def _dispatch(h_packed, dest_chunks, m_pad):
    n = h_packed.shape[0]
    n_chunks = n // DISPATCH_ROWS
    assert n_chunks % SC_WORKERS == 0
    per_worker = n_chunks // SC_WORKERS
    mesh = plsc.VectorSubcoreMesh(core_axis_name="c", subcore_axis_name="s")

    @functools.partial(
        pl.kernel, mesh=mesh,
        out_type=jax.ShapeDtypeStruct((m_pad, HALF), jnp.int32),
        scratch_types=[
            pltpu.VMEM((TOP_K, DISPATCH_ROWS), jnp.int32),
            pltpu.VMEM((DISPATCH_ROWS, HALF), jnp.int32),
            pltpu.SemaphoreType.DMA,
        ],
        compiler_params=pltpu.CompilerParams(use_tc_tiling_on_sc=True),
        name="dispatch",
    )
    def dispatch_kernel(h_hbm, dest_hbm, out_hbm, idx_v, rows_v, sem):
        wid = lax.axis_index("s") * SC_CORES + lax.axis_index("c")

        @pl.loop(0, per_worker)
        def _(j):
            chunk = wid * per_worker + j
            pltpu.sync_copy(dest_hbm.at[chunk], idx_v)
            pltpu.sync_copy(h_hbm.at[pl.ds(chunk * DISPATCH_ROWS, DISPATCH_ROWS)], rows_v)
            for k in range(TOP_K):
                pltpu.async_copy(rows_v, out_hbm.at[idx_v.at[k]], sem).wait()

    return dispatch_kernel(h_packed, dest_chunks)


def _gather_expert_rows(y_sorted, dest_chunks, n):
    n_chunks = n // DISPATCH_ROWS
    assert n_chunks % SC_WORKERS == 0
    per_worker = n_chunks // SC_WORKERS
    mesh = plsc.VectorSubcoreMesh(core_axis_name="c", subcore_axis_name="s")

    @functools.partial(
        pl.kernel, mesh=mesh,
        out_type=jax.ShapeDtypeStruct((TOP_K, n, HALF), jnp.int32),
        scratch_types=[
            pltpu.VMEM((TOP_K, DISPATCH_ROWS), jnp.int32),
            pltpu.VMEM((DISPATCH_ROWS, HALF), jnp.int32),
            pltpu.SemaphoreType.DMA,
        ],
        compiler_params=pltpu.CompilerParams(use_tc_tiling_on_sc=True),
        name="gather_expert_rows",
    )
    def gather_kernel(y_hbm, dest_hbm, out_hbm, idx_v, rows_v, sem):
        wid = lax.axis_index("s") * SC_CORES + lax.axis_index("c")

        @pl.loop(0, per_worker)
        def _(j):
            chunk = wid * per_worker + j
            pltpu.sync_copy(dest_hbm.at[chunk], idx_v)
            for k in range(TOP_K):
                pltpu.async_copy(y_hbm.at[idx_v.at[k]], rows_v, sem).wait()
                pltpu.sync_copy(rows_v, out_hbm.at[k, pl.ds(chunk * DISPATCH_ROWS, DISPATCH_ROWS)])

    return gather_kernel(y_sorted, dest_chunks)


def _moe_kernel(be_ref, nused_ref, x_ref, wgu_ref, bgu_ref, wd_ref, bd_ref, y_ref, wgu_bf, wd_bf):
    i = pl.program_id(0)

    @pl.when(i < nused_ref[0])
    def _():
        @pl.when((i == 0) | (be_ref[i] != be_ref[jnp.maximum(i - 1, 0)]))
        def _():
            wgu_bf[...] = wgu_ref[...].astype(BF16)
            wd_bf[...] = wd_ref[...].astype(BF16)

        gu = _dot(_unpack_bf16_pairs(x_ref[...]), wgu_bf[...]) + bgu_ref[...]
        gate = jnp.minimum(gu[:, :EXPERT_FF], SWIGLU_LIMIT)
        up = jnp.clip(gu[:, EXPERT_FF:], -SWIGLU_LIMIT, SWIGLU_LIMIT)
        hmid = gate * (1.0 / (1.0 + jnp.exp(-SWIGLU_ALPHA * gate))) * (up + 1.0)
        y = _dot(hmid.astype(BF16), wd_bf[...]) + bd_ref[...]
        y_ref[...] = _pack_bf16_pairs(y.astype(BF16))


def _moe_experts(block_expert, n_used, x_sorted, w_gu, b_gu, w_down, b_down):
    m_pad = x_sorted.shape[0]
    n_blocks = m_pad // MOE_ROWS

    def blk(i, be, nu):
        return jnp.minimum(i, nu[0] - 1)

    def expert(i, be, nu):
        return be[blk(i, be, nu)]

    grid_spec = pltpu.PrefetchScalarGridSpec(
        num_scalar_prefetch=2,
        grid=(n_blocks,),
        in_specs=[
            pl.BlockSpec((MOE_ROWS, HALF), lambda i, be, nu: (blk(i, be, nu), 0)),
            pl.BlockSpec((None, D_MODEL, 2 * EXPERT_FF), lambda i, be, nu: (expert(i, be, nu), 0, 0)),
            pl.BlockSpec((None, 1, 2 * EXPERT_FF), lambda i, be, nu: (expert(i, be, nu), 0, 0)),
            pl.BlockSpec((None, EXPERT_FF, D_MODEL), lambda i, be, nu: (expert(i, be, nu), 0, 0)),
            pl.BlockSpec((None, 1, D_MODEL), lambda i, be, nu: (expert(i, be, nu), 0, 0)),
        ],
        out_specs=pl.BlockSpec((MOE_ROWS, HALF), lambda i, be, nu: (blk(i, be, nu), 0)),
        scratch_shapes=[
            pltpu.VMEM((D_MODEL, 2 * EXPERT_FF), BF16),
            pltpu.VMEM((EXPERT_FF, D_MODEL), BF16),
        ],
    )
    return pl.pallas_call(
        _moe_kernel,
        grid_spec=grid_spec,
        out_shape=jax.ShapeDtypeStruct((m_pad, HALF), jnp.int32),
        compiler_params=pltpu.CompilerParams(
            dimension_semantics=("arbitrary",), vmem_limit_bytes=VMEM_LIMIT),
        name="moe_experts",
    )(block_expert, n_used, x_sorted, w_gu, b_gu, w_down, b_down)


def _combine_kernel(yk_ref, gates_ref, h_ref, g_ref, b_ref, out_ref):
    gates = gates_ref[...]
    lo = hi = None
    for k in range(TOP_K):
        u = lax.bitcast_convert_type(yk_ref[k], jnp.uint32)
        gk = gates[:, k:k + 1]
        lo_k = lax.bitcast_convert_type(u << 16, F32) * gk
        hi_k = lax.bitcast_convert_type(u & jnp.uint32(HI_MASK), F32) * gk
        lo = lo_k if lo is None else lo + lo_k
        hi = hi_k if hi is None else hi + hi_k
    acc = ALPHA * h_ref[...] + jnp.concatenate([lo, hi], axis=1)
    out_ref[...] = _layer_norm(acc, g_ref[...], b_ref[...])


def _combine(yk, gates, h_all, ln_g, ln_b, row_offset, n_rows):
    tile0 = row_offset // COMBINE_TOKENS
    return pl.pallas_call(
        _combine_kernel,
        grid=(n_rows // COMBINE_TOKENS,),
        in_specs=[
            pl.BlockSpec((TOP_K, COMBINE_TOKENS, HALF), lambda i: (0, tile0 + i, 0)),
            pl.BlockSpec((COMBINE_TOKENS, TOP_K), lambda i: (tile0 + i, 0)),
            pl.BlockSpec((COMBINE_TOKENS, D_MODEL), lambda i: (tile0 + i, 0)),
            _const_spec((1, D_MODEL)),
            _const_spec((1, D_MODEL)),
        ],
        out_specs=pl.BlockSpec((COMBINE_TOKENS, D_MODEL), lambda i: (i, 0)),
        out_shape=jax.ShapeDtypeStruct((n_rows, D_MODEL), F32),
        compiler_params=pltpu.CompilerParams(
            dimension_semantics=("arbitrary",), vmem_limit_bytes=VMEM_LIMIT),
        name="moe_combine",
    )(yk, gates, h_all, ln_g, ln_b)


def kernel(x_prompt, x_sample, state_pool, state_gla, w_in, w_pool, pool_scale, w_gate_up, b_gate,
           gla_norm_w, w_out, ln1_g, ln1_b, w_router, b_router, w_gu, b_gu, w_down, b_down,
           ln2_g, ln2_b):
    assert w_in.shape[0] == 1, "single-layer kernel"
    bp, seq, _ = x_prompt.shape
    bs, dec_seq, _ = x_sample.shape
    assert dec_seq == CHUNK and seq % TILE_TOKENS == 0 and bs % CHUNKS_PER_TILE == 0
    n_prompt = bp * seq
    n_sample = bs * dec_seq
    n_total = n_prompt + n_sample
    nk = n_total * TOP_K
    n_blocks = -(-nk // MOE_ROWS) + NUM_EXPERTS
    m_pad = n_blocks * MOE_ROWS

    w_glr = jnp.zeros((D_MODEL, LANES), BF16).at[:, :GATE_RANK].set(w_in[0, :, N_MAIN:].astype(BF16))
    w_gate = jnp.zeros((LANES, GLA_HEADS * GLA_DK), BF16).at[:GATE_RANK].set(w_gate_up[0].astype(BF16))
    weights = (
        w_in[0, :, :N_MAIN].astype(BF16), w_glr, w_gate, b_gate[0][None, :],
        w_pool[0].astype(BF16), pool_scale[0][None, :], gla_norm_w[0][None, :],
        w_out[0].astype(BF16), ln1_g[0][None, :], ln1_b[0][None, :],
        w_router[0].T.astype(BF16), jnp.broadcast_to(b_router[0][:, None], (NUM_EXPERTS, LANES)),
    )
    assert len(weights) == N_MIXER_WEIGHTS

    *shared, hist_p, s_p = _mixer_prompt(x_prompt, weights, n_total)
    h_all, hb_all, logits_t, hist_s, s_s = _mixer_sample(
        x_sample, state_pool[0], state_gla[0], weights, shared, n_prompt)

    gates_t, dest_t, pad_end = _router(logits_t)
    pad_end = pad_end[:, 0].astype(jnp.int32)
    block_start = jnp.arange(n_blocks, dtype=jnp.int32) * MOE_ROWS
    block_expert = jnp.minimum(jnp.sum((block_start[:, None] >= pad_end[None, :]).astype(jnp.int32), axis=1),
                               NUM_EXPERTS - 1)
    n_used = (pad_end[-1:] // MOE_ROWS).astype(jnp.int32)

    dest_chunks = dest_t.reshape(TOP_K, n_total // DISPATCH_ROWS, DISPATCH_ROWS).transpose(1, 0, 2)
    x_sorted = _dispatch(hb_all, dest_chunks, m_pad)
    y_sorted = _moe_experts(block_expert, n_used, x_sorted, w_gu[0], b_gu[0][:, None, :],
                            w_down[0], b_down[0][:, None, :])
    yk = _gather_expert_rows(y_sorted, dest_chunks, n_total)
    gates = gates_t.T
    ln_g, ln_b = ln2_g[0][None, :], ln2_b[0][None, :]
    y_prompt = _combine(yk, gates, h_all, ln_g, ln_b, 0, n_prompt).reshape(bp, seq, D_MODEL)
    y_sample = _combine(yk, gates, h_all, ln_g, ln_b, n_prompt, n_sample).reshape(bs, dec_seq, D_MODEL)
    return (y_prompt, y_sample, hist_p[None], s_p[None], hist_s[None], s_s[None])
```

```python
import functools

import jax
import jax.numpy as jnp
from jax import lax
from jax.experimental import pallas as pl
from jax.experimental.pallas import tpu as pltpu
from jax.experimental.pallas import tpu_sc as plsc

F32 = jnp.float32
BF16 = jnp.bfloat16

D_MODEL = 1024
CHUNK = 64
PAST_LEN = 1024
POOL_WIDTH = 512
POOL_WINDOWS = (2, 4, 8, 16)
POOL_GROUP = 128
POOL_HIST = 15
GLA_HEADS = 4
GLA_DK = 64
GLA_DV = 128
GATE_RANK = 16
GATE_NORMALIZER = 16.0
NUM_EXPERTS = 32
TOP_K = 4
EXPERT_FF = 1024
SWIGLU_LIMIT = 7.0
SWIGLU_ALPHA = 1.702
LN_EPS = 1e-5
RMS_EPS = 1e-6
ALPHA = 2.0 ** 0.25

Q0 = POOL_WIDTH
K0 = Q0 + GLA_HEADS * GLA_DK
V0 = K0 + GLA_HEADS * GLA_DK
R0 = V0 + GLA_HEADS * GLA_DV
N_MAIN = R0 + GLA_HEADS * GLA_DV

LANES = 128
TILE_TOKENS = 512
CHUNKS_PER_TILE = TILE_TOKENS // CHUNK
HIST_PAD = 16
MOE_ROWS = 512
ROUTE_TOKENS = 1024
COMBINE_TOKENS = 256
COMBINE_SEGMENTS = 4
VMEM_LIMIT = 56 * 1024 * 1024


def _dot(a, b):
    return jnp.dot(a, b, preferred_element_type=F32)


def _dot_nt(a, b):
    return lax.dot_general(a, b, (((1,), (1,)), ((), ())), preferred_element_type=F32)


def _dot_tn(a, b):
    return lax.dot_general(a, b, (((0,), (0,)), ((), ())), preferred_element_type=F32)


HALF = D_MODEL // 2
HI_MASK = 0xFFFF0000


def _pack_bf16_pairs(xb):
    lo = lax.bitcast_convert_type(xb[:, :HALF].astype(F32), jnp.uint32) >> 16
    hi = lax.bitcast_convert_type(xb[:, HALF:].astype(F32), jnp.uint32) & jnp.uint32(HI_MASK)
    return lax.bitcast_convert_type(hi | lo, jnp.int32)


def _unpack_bf16_pairs(p):
    u = lax.bitcast_convert_type(p, jnp.uint32)
    lo = lax.bitcast_convert_type(u << 16, F32)
    hi = lax.bitcast_convert_type(u & jnp.uint32(HI_MASK), F32)
    return jnp.concatenate([lo, hi], axis=1).astype(BF16)


def _layer_norm(v, g, b):
    mu = jnp.mean(v, axis=-1, keepdims=True)
    c = v - mu
    var = jnp.mean(c * c, axis=-1, keepdims=True)
    return c * lax.rsqrt(var + LN_EPS) * g + b


N_MIXER_WEIGHTS = 12
N_MIXER_SHARED_OUT = 3


def _mixer_kernel(per_chunk_state, pos0, *refs):
    if per_chunk_state:
        (x_ref, hist_in_ref, s_in_ref, *rest) = refs
    else:
        (x_ref, *rest) = refs
        hist_in_ref = s_in_ref = None
    (w_main_ref, w_glr_ref, w_gate_ref, b_gate_ref, w_pool_ref, pscale_ref, gnorm_ref,
     w_out_ref, ln1g_ref, ln1b_ref, w_router_ref, b_router_ref, *rest) = rest
    if per_chunk_state:
        rest = rest[N_MIXER_SHARED_OUT:]
    (h_ref, hb_ref, logits_ref, hist_out_ref, s_out_ref,
     proj_scr, b_scr, o_scr, ext_scr, st_scr, tri_scr) = rest

    if per_chunk_state:
        t = None
        first_step = pl.program_id(0) == 0
    else:
        t = pl.program_id(1)
        first_step = (pl.program_id(0) == 0) & (t == 0)
    x = x_ref[...].reshape(TILE_TOKENS, D_MODEL)
    xb = x.astype(BF16)

    @pl.when(first_step)
    def _():
        ti = lax.broadcasted_iota(jnp.int32, (TILE_TOKENS, TILE_TOKENS), 0)
        tj = lax.broadcasted_iota(jnp.int32, (TILE_TOKENS, TILE_TOKENS), 1)
        same_chunk = (ti // CHUNK) == (tj // CHUNK)
        tri_scr[...] = jnp.where(same_chunk & (ti >= tj), 1.0, 0.0).astype(BF16)

    proj_scr[...] = _dot(xb, w_main_ref[...])
    glr = _dot(xb, w_glr_ref[...])
    gk = _dot(glr.astype(BF16), w_gate_ref[...]) + b_gate_ref[...]
    log_sig = jnp.minimum(gk, 0.0) - jnp.log1p(jnp.exp(-jnp.abs(gk)))
    g = log_sig / GATE_NORMALIZER
    g_hi = g.astype(BF16)
    g_lo = (g - g_hi.astype(F32)).astype(BF16)
    b_scr[...] = _dot(tri_scr[...], g_hi) + _dot(tri_scr[...], g_lo)

    if per_chunk_state:
        seg_len, seg_stride, n_seg = CHUNK, CHUNK + HIST_PAD, CHUNKS_PER_TILE
        for c in range(n_seg):
            base = c * seg_stride
            ext_scr[base:base + HIST_PAD, :] = jnp.zeros((HIST_PAD, POOL_WIDTH), F32)
            ext_scr[base + 1:base + HIST_PAD, :] = hist_in_ref[c]
            ext_scr[base + HIST_PAD:base + seg_stride, :] = proj_scr[c * CHUNK:(c + 1) * CHUNK, 0:POOL_WIDTH]
        row_pos = pos0 + lax.broadcasted_iota(jnp.int32, (seg_len, POOL_GROUP), 0)
    else:
        seg_len, seg_stride, n_seg = TILE_TOKENS, TILE_TOKENS + HIST_PAD, 1

        @pl.when(t == 0)
        def _():
            ext_scr[0:HIST_PAD, :] = jnp.zeros((HIST_PAD, POOL_WIDTH), F32)

        ext_scr[HIST_PAD:seg_stride, :] = proj_scr[:, 0:POOL_WIDTH]
        row_pos = pos0 + t * TILE_TOKENS + lax.broadcasted_iota(jnp.int32, (seg_len, POOL_GROUP), 0)

    pool_cols = []
    for gi, w in enumerate(POOL_WINDOWS):
        gs = slice(gi * POOL_GROUP, (gi + 1) * POOL_GROUP)
        cnt = jnp.minimum(row_pos + 1, w).astype(F32)
        ext = ext_scr[:, gs]
        win = ext
        shift = 1
        while shift < w:
            win = win + pltpu.roll(win, shift, 0)
            shift *= 2
        segs = []
        for s in range(n_seg):
            base = s * seg_stride + HIST_PAD
            segs.append(win[base:base + seg_len] / cnt - ext[base:base + seg_len])
        pooled = segs[0] if n_seg == 1 else jnp.concatenate(segs, axis=0)
        z = _dot(pooled.astype(BF16), w_pool_ref[gi])
        pool_cols.append(z * pscale_ref[:, gs])
    pool_out = jnp.concatenate(pool_cols, axis=1)

    if per_chunk_state:
        for c in range(n_seg):
            end = (c + 1) * seg_stride
            hist_out_ref[c] = ext_scr[end - POOL_HIST:end, :]
    else:
        @pl.when(t == pl.num_programs(1) - 1)
        def _():
            hist_out_ref[...] = ext_scr[seg_stride - POOL_HIST:seg_stride, :]

        ext_scr[0:HIST_PAD, :] = ext_scr[TILE_TOKENS:seg_stride, :]

    if not per_chunk_state:
        @pl.when(t == 0)
        def _():
            st_scr[...] = jnp.zeros_like(st_scr)

    hk = GLA_HEADS * GLA_DK
    hv = GLA_HEADS * GLA_DV

    def head_of(shape, dim, width):
        return lax.broadcasted_iota(jnp.int32, shape, dim) // width

    same_head_k = head_of((hk, hk), 0, CHUNK) == head_of((hk, hk), 1, GLA_DK)
    same_head_v = head_of((hk, hv), 0, CHUNK) == head_of((hk, hv), 1, GLA_DV)
    same_head_s = head_of((hv, hk), 0, GLA_DV) == head_of((hv, hk), 1, GLA_DK)
    col_head = head_of((GLA_DV, hk), 1, GLA_DK)
    causal = (lax.broadcasted_iota(jnp.int32, (CHUNK, hk), 0)
              >= lax.broadcasted_iota(jnp.int32, (CHUNK, hk), 1) % CHUNK)

    def stack_heads(a, keep):
        return jnp.where(keep, jnp.concatenate([a] * GLA_HEADS, axis=0), 0.0).astype(BF16)

    st = None if per_chunk_state else st_scr[...]
    for c in range(CHUNKS_PER_TILE):
        rows = slice(c * CHUNK, (c + 1) * CHUNK)
        b = b_scr[rows, :]
        b_last = b_scr[(c + 1) * CHUNK - 1:(c + 1) * CHUNK, :]
        q = proj_scr[rows, Q0:K0] * (GLA_DK ** -0.5)
        k = proj_scr[rows, K0:V0]
        v = proj_scr[rows, V0:R0]
        qt = (q * jnp.exp(b)).astype(BF16)
        kl = (k * jnp.exp(b_last - b)).astype(BF16)
        if per_chunk_state:
            st = jnp.transpose(s_in_ref[c].reshape(hk, GLA_DV))
        att = jnp.where(causal, _dot_nt(qt, stack_heads(k * jnp.exp(-b), same_head_k)), 0.0)
        o_scr[rows, :] = (_dot(att.astype(BF16), stack_heads(v, same_head_v))
                          + _dot_nt(qt, stack_heads(st, same_head_s)))
        kv = _dot_tn(v.astype(BF16), kl)
        upd = jnp.zeros((GLA_DV, hk), F32)
        for h in range(GLA_HEADS):
            upd = jnp.where(col_head == h, kv[h * GLA_DV:(h + 1) * GLA_DV, :], upd)
        st = st * jnp.exp(b_last) + upd
        if per_chunk_state:
            s_out_ref[c] = jnp.transpose(st).reshape(GLA_HEADS, GLA_DK, GLA_DV)

    if not per_chunk_state:
        st_scr[...] = st

        @pl.when(t == pl.num_programs(1) - 1)
        def _():
            s_out_ref[...] = jnp.transpose(st).reshape(GLA_HEADS, GLA_DK, GLA_DV)

    r = proj_scr[:, R0:N_MAIN]
    silu_r = r * (1.0 / (1.0 + jnp.exp(-r)))
    gated = []
    for h in range(GLA_HEADS):
        vs = slice(h * GLA_DV, (h + 1) * GLA_DV)
        oh = o_scr[:, vs]
        ms = jnp.mean(oh * oh, axis=-1, keepdims=True)
        gated.append(oh * lax.rsqrt(ms + RMS_EPS) * gnorm_ref[...] * silu_r[:, vs])
    mix_in = jnp.concatenate([pool_out] + gated, axis=1).astype(BF16)
    resid = ALPHA * x + _dot(mix_in, w_out_ref[...])
    h_val = _layer_norm(resid, ln1g_ref[...], ln1b_ref[...])
    h_ref[...] = h_val
    hb = h_val.astype(BF16)
    hb_ref[...] = _pack_bf16_pairs(hb)
    logits_ref[...] = _dot_nt(w_router_ref[...], hb) + b_router_ref[:, 0:1]


def _const_spec(shape):
    nd = len(shape)
    return pl.BlockSpec(shape, lambda *_: (0,) * nd)


def _mixer_weight_specs():
    return [
        _const_spec((D_MODEL, N_MAIN)),
        _const_spec((D_MODEL, LANES)),
        _const_spec((LANES, GLA_HEADS * GLA_DK)),
        _const_spec((1, GLA_HEADS * GLA_DK)),
        _const_spec((len(POOL_WINDOWS), POOL_GROUP, POOL_GROUP)),
        _const_spec((1, POOL_WIDTH)),
        _const_spec((1, GLA_DV)),
        _const_spec((D_MODEL, D_MODEL)),
        _const_spec((1, D_MODEL)),
        _const_spec((1, D_MODEL)),
        _const_spec((NUM_EXPERTS, D_MODEL)),
        _const_spec((NUM_EXPERTS, LANES)),
    ]


def _mixer_scratch(per_chunk_state):
    ext_rows = (CHUNKS_PER_TILE * (CHUNK + HIST_PAD)) if per_chunk_state else (TILE_TOKENS + HIST_PAD)
    return [
        pltpu.VMEM((TILE_TOKENS, N_MAIN), F32),
        pltpu.VMEM((TILE_TOKENS, GLA_HEADS * GLA_DK), F32),
        pltpu.VMEM((TILE_TOKENS, GLA_HEADS * GLA_DV), F32),
        pltpu.VMEM((ext_rows, POOL_WIDTH), F32),
        pltpu.VMEM((GLA_DV, GLA_HEADS * GLA_DK), F32),
        pltpu.VMEM((TILE_TOKENS, TILE_TOKENS), BF16),
    ]


def _mixer_out_shapes(n_total, bsz):
    return (
        jax.ShapeDtypeStruct((n_total, D_MODEL), F32),
        jax.ShapeDtypeStruct((n_total, HALF), jnp.int32),
        jax.ShapeDtypeStruct((NUM_EXPERTS, n_total), F32),
        jax.ShapeDtypeStruct((bsz, POOL_HIST, POOL_WIDTH), F32),
        jax.ShapeDtypeStruct((bsz, GLA_HEADS, GLA_DK, GLA_DV), F32),
    )


def _mixer_prompt(x, weights, n_total):
    bsz, seq, _ = x.shape
    tiles = seq // TILE_TOKENS
    return pl.pallas_call(
        functools.partial(_mixer_kernel, False, 0),
        grid=(bsz, tiles),
        in_specs=[pl.BlockSpec((None, TILE_TOKENS, D_MODEL), lambda b, t: (b, t, 0))] + _mixer_weight_specs(),
        out_specs=(
            pl.BlockSpec((TILE_TOKENS, D_MODEL), lambda b, t: (b * tiles + t, 0)),
            pl.BlockSpec((TILE_TOKENS, HALF), lambda b, t: (b * tiles + t, 0)),
            pl.BlockSpec((NUM_EXPERTS, TILE_TOKENS), lambda b, t: (0, b * tiles + t)),
            pl.BlockSpec((None, POOL_HIST, POOL_WIDTH), lambda b, t: (b, 0, 0)),
            pl.BlockSpec((None, GLA_HEADS, GLA_DK, GLA_DV), lambda b, t: (b, 0, 0, 0)),
        ),
        out_shape=_mixer_out_shapes(n_total, bsz),
        scratch_shapes=_mixer_scratch(False),
        compiler_params=pltpu.CompilerParams(
            dimension_semantics=("arbitrary", "arbitrary"), vmem_limit_bytes=VMEM_LIMIT),
        name="mixer_prompt",
    )(x, *weights)


def _mixer_sample(x, hist, state, weights, shared, row_offset):
    bsz = x.shape[0]
    tiles = bsz // CHUNKS_PER_TILE
    tile0 = row_offset // TILE_TOKENS
    n_total = shared[0].shape[0]
    n_in = 3 + N_MIXER_WEIGHTS
    return pl.pallas_call(
        functools.partial(_mixer_kernel, True, PAST_LEN),
        grid=(tiles,),
        in_specs=[
            pl.BlockSpec((CHUNKS_PER_TILE, CHUNK, D_MODEL), lambda i: (i, 0, 0)),
            pl.BlockSpec((CHUNKS_PER_TILE, POOL_HIST, POOL_WIDTH), lambda i: (i, 0, 0)),
            pl.BlockSpec((CHUNKS_PER_TILE, GLA_HEADS, GLA_DK, GLA_DV), lambda i: (i, 0, 0, 0)),
        ] + _mixer_weight_specs() + [pl.BlockSpec(memory_space=pl.ANY)] * N_MIXER_SHARED_OUT,
        out_specs=(
            pl.BlockSpec((TILE_TOKENS, D_MODEL), lambda i: (tile0 + i, 0)),
            pl.BlockSpec((TILE_TOKENS, HALF), lambda i: (tile0 + i, 0)),
            pl.BlockSpec((NUM_EXPERTS, TILE_TOKENS), lambda i: (0, tile0 + i)),
            pl.BlockSpec((CHUNKS_PER_TILE, POOL_HIST, POOL_WIDTH), lambda i: (i, 0, 0)),
            pl.BlockSpec((CHUNKS_PER_TILE, GLA_HEADS, GLA_DK, GLA_DV), lambda i: (i, 0, 0, 0)),
        ),
        out_shape=_mixer_out_shapes(n_total, bsz),
        input_output_aliases={n_in + j: j for j in range(N_MIXER_SHARED_OUT)},
        scratch_shapes=_mixer_scratch(True),
        compiler_params=pltpu.CompilerParams(
            dimension_semantics=("arbitrary",), vmem_limit_bytes=VMEM_LIMIT),
        name="mixer_sample",
    )(x, hist, state, *weights, *shared)


def _router_kernel(lt_ref, gates_ref, dest_ref, padend_ref, cnt_scr, base_scr, pstart_scr, before_scr):
    phase = pl.program_id(0)
    i = pl.program_id(1)
    shape = (NUM_EXPERTS, ROUTE_TOKENS)
    logits = lt_ref[...]
    row = lax.broadcasted_iota(jnp.int32, shape, 0)
    sel, vals = [], []
    for _ in range(TOP_K):
        m = jnp.max(logits, axis=0, keepdims=True)
        idx = jnp.min(jnp.where(logits == m, row, NUM_EXPERTS), axis=0, keepdims=True)
        hit = row == idx
        sel.append(hit)
        vals.append(m)
        logits = jnp.where(hit, -jnp.inf, logits)
    chosen = sum(jnp.where(hit, 1.0, 0.0) for hit in sel)
    tile_counts = jnp.broadcast_to(jnp.sum(chosen, axis=1, keepdims=True), (NUM_EXPERTS, LANES))

    @pl.when(phase == 0)
    def _():
        @pl.when(i == 0)
        def _():
            cnt_scr[...] = jnp.zeros_like(cnt_scr)

        cnt_scr[...] += tile_counts

    @pl.when(phase == 1)
    def _():
        @pl.when(i == 0)
        def _():
            blocks = jnp.floor((cnt_scr[...] + (MOE_ROWS - 1)) * (1.0 / MOE_ROWS))
            erow = lax.broadcasted_iota(jnp.int32, (NUM_EXPERTS, LANES), 0)
            cum = blocks
            shift = 1
            while shift < NUM_EXPERTS:
                cum = cum + jnp.where(erow >= shift, pltpu.roll(cum, shift, 0), 0.0)
                shift *= 2
            padend_ref[...] = cum * MOE_ROWS
            pstart_scr[...] = (cum - blocks) * MOE_ROWS
            base_scr[...] = jnp.zeros_like(base_scr)
            ti = lax.broadcasted_iota(jnp.int32, (ROUTE_TOKENS, ROUTE_TOKENS), 0)
            tj = lax.broadcasted_iota(jnp.int32, (ROUTE_TOKENS, ROUTE_TOKENS), 1)
            before_scr[...] = jnp.where(ti < tj, 1.0, 0.0).astype(BF16)

        earlier = _dot(chosen.astype(BF16), before_scr[...])
        pos = pstart_scr[:, 0:1] + base_scr[:, 0:1] + earlier
        dest = [jnp.sum(jnp.where(hit, pos, 0.0), axis=0, keepdims=True) for hit in sel]
        dest_ref[...] = jnp.concatenate(dest, axis=0).astype(jnp.int32)
        ex = [jnp.exp(v - vals[0]) for v in vals]
        denom = ex[0] + ex[1] + ex[2] + ex[3]
        gates_ref[...] = jnp.concatenate([e / denom for e in ex], axis=0)
        base_scr[...] += tile_counts


def _router(logits_t):
    n = logits_t.shape[1]
    assert n % ROUTE_TOKENS == 0
    tiles = n // ROUTE_TOKENS
    return pl.pallas_call(
        _router_kernel,
        grid=(2, tiles),
        in_specs=[pl.BlockSpec((NUM_EXPERTS, ROUTE_TOKENS), lambda p, i: (0, i))],
        out_specs=(
            pl.BlockSpec((TOP_K, ROUTE_TOKENS), lambda p, i: (0, i * p)),
            pl.BlockSpec((TOP_K, ROUTE_TOKENS), lambda p, i: (0, i * p)),
            pl.BlockSpec((NUM_EXPERTS, LANES), lambda p, i: (0, 0)),
        ),
        out_shape=(
            jax.ShapeDtypeStruct((TOP_K, n), F32),
            jax.ShapeDtypeStruct((TOP_K, n), jnp.int32),
            jax.ShapeDtypeStruct((NUM_EXPERTS, LANES), F32),
        ),
        scratch_shapes=[pltpu.VMEM((NUM_EXPERTS, LANES), F32)] * 3
        + [pltpu.VMEM((ROUTE_TOKENS, ROUTE_TOKENS), BF16)],
        compiler_params=pltpu.CompilerParams(
            dimension_semantics=("arbitrary", "arbitrary"), vmem_limit_bytes=VMEM_LIMIT),
        name="router",
    )(logits_t)


SC_CORES = 2
SC_SUBCORES = 16
SC_WORKERS = SC_CORES * SC_SUBCORES
DISPATCH_ROWS = 64


def _dispatch(h_packed, dest_chunks, m_pad):
    n = h_packed.shape[0]
    n_chunks = n // DISPATCH_ROWS
    assert n_chunks % SC_WORKERS == 0
    per_worker = n_chunks // SC_WORKERS
    mesh = plsc.VectorSubcoreMesh(core_axis_name="c", subcore_axis_name="s")

    @functools.partial(
        pl.kernel, mesh=mesh,
        out_type=jax.ShapeDtypeStruct((m_pad, HALF), jnp.int32),
        scratch_types=[
            pltpu.VMEM((TOP_K, DISPATCH_ROWS), jnp.int32),
            pltpu.VMEM((DISPATCH_ROWS, HALF), jnp.int32),
            pltpu.SemaphoreType.DMA,
        ],
        compiler_params=pltpu.CompilerParams(use_tc_tiling_on_sc=True),
        name="dispatch",
    )
    def dispatch_kernel(h_hbm, dest_hbm, out_hbm, idx_v, rows_v, sem):
        wid = lax.axis_index("s") * SC_CORES + lax.axis_index("c")

        @pl.loop(0, per_worker)
        def _(j):
            chunk = wid * per_worker + j
            pltpu.sync_copy(dest_hbm.at[chunk], idx_v)
            pltpu.sync_copy(h_hbm.at[pl.ds(chunk * DISPATCH_ROWS, DISPATCH_ROWS)], rows_v)
            for k in range(TOP_K):
                pltpu.async_copy(rows_v, out_hbm.at[idx_v.at[k]], sem).wait()

    return dispatch_kernel(h_packed, dest_chunks)


def _gather_expert_rows(y_sorted, dest_chunks, row_offset, n):
    n_chunks = n // DISPATCH_ROWS
    assert n_chunks % SC_WORKERS == 0 and row_offset % DISPATCH_ROWS == 0
    per_worker = n_chunks // SC_WORKERS
    chunk0 = row_offset // DISPATCH_ROWS
    mesh = plsc.VectorSubcoreMesh(core_axis_name="c", subcore_axis_name="s")

    @functools.partial(
        pl.kernel, mesh=mesh,
        out_type=jax.ShapeDtypeStruct((TOP_K, n, HALF), jnp.int32),
        scratch_types=[
            pltpu.VMEM((TOP_K, DISPATCH_ROWS), jnp.int32),
            pltpu.VMEM((2, DISPATCH_ROWS, HALF), jnp.int32),
            pltpu.SemaphoreType.DMA((2,)),
        ],
        compiler_params=pltpu.CompilerParams(use_tc_tiling_on_sc=True),
        name="gather_expert_rows",
    )
    def gather_kernel(y_hbm, dest_hbm, out_hbm, idx_v, rows_v, sems):
        wid = lax.axis_index("s") * SC_CORES + lax.axis_index("c")

        def gather(k):
            return pltpu.make_async_copy(y_hbm.at[idx_v.at[k]], rows_v.at[k % 2], sems.at[k % 2])

        @pl.loop(0, per_worker)
        def _(j):
            local = wid * per_worker + j
            pltpu.sync_copy(dest_hbm.at[chunk0 + local], idx_v)
            gather(0).start()
            for k in range(TOP_K):
                if k + 1 < TOP_K:
                    gather(k + 1).start()
                gather(k).wait()
                pltpu.sync_copy(rows_v.at[k % 2],
                                out_hbm.at[k, pl.ds(local * DISPATCH_ROWS, DISPATCH_ROWS)])

    return gather_kernel(y_sorted, dest_chunks)


def _moe_kernel(be_ref, nused_ref, x_ref, wgu_ref, bgu_ref, wd_ref, bd_ref, y_ref, wgu_bf, wd_bf):
    i = pl.program_id(0)

    @pl.when(i < nused_ref[0])
    def _():
        @pl.when((i == 0) | (be_ref[i] != be_ref[jnp.maximum(i - 1, 0)]))
        def _():
            wgu_bf[...] = wgu_ref[...].astype(BF16)
            wd_bf[...] = wd_ref[...].astype(BF16)

        gu = _dot(_unpack_bf16_pairs(x_ref[...]), wgu_bf[...]) + bgu_ref[...]
        gate = jnp.minimum(gu[:, :EXPERT_FF], SWIGLU_LIMIT)
        up = jnp.clip(gu[:, EXPERT_FF:], -SWIGLU_LIMIT, SWIGLU_LIMIT)
        hmid = gate * (1.0 / (1.0 + jnp.exp(-SWIGLU_ALPHA * gate))) * (up + 1.0)
        y = _dot(hmid.astype(BF16), wd_bf[...]) + bd_ref[...]
        y_ref[...] = _pack_bf16_pairs(y.astype(BF16))


def _moe_experts(block_expert, n_used, x_sorted, w_gu, b_gu, w_down, b_down):
    m_pad = x_sorted.shape[0]
    n_blocks = m_pad // MOE_ROWS

    def blk(i, be, nu):
        return jnp.minimum(i, nu[0] - 1)

    def expert(i, be, nu):
        return be[blk(i, be, nu)]

    grid_spec = pltpu.PrefetchScalarGridSpec(
        num_scalar_prefetch=2,
        grid=(n_blocks,),
        in_specs=[
            pl.BlockSpec((MOE_ROWS, HALF), lambda i, be, nu: (blk(i, be, nu), 0)),
            pl.BlockSpec((None, D_MODEL, 2 * EXPERT_FF), lambda i, be, nu: (expert(i, be, nu), 0, 0)),
            pl.BlockSpec((None, 1, 2 * EXPERT_FF), lambda i, be, nu: (expert(i, be, nu), 0, 0)),
            pl.BlockSpec((None, EXPERT_FF, D_MODEL), lambda i, be, nu: (expert(i, be, nu), 0, 0)),
            pl.BlockSpec((None, 1, D_MODEL), lambda i, be, nu: (expert(i, be, nu), 0, 0)),
        ],
        out_specs=pl.BlockSpec((MOE_ROWS, HALF), lambda i, be, nu: (blk(i, be, nu), 0)),
        scratch_shapes=[
            pltpu.VMEM((D_MODEL, 2 * EXPERT_FF), BF16),
            pltpu.VMEM((EXPERT_FF, D_MODEL), BF16),
        ],
    )
    return pl.pallas_call(
        _moe_kernel,
        grid_spec=grid_spec,
        out_shape=jax.ShapeDtypeStruct((m_pad, HALF), jnp.int32),
        compiler_params=pltpu.CompilerParams(
            dimension_semantics=("arbitrary",), vmem_limit_bytes=VMEM_LIMIT),
        name="moe_experts",
    )(block_expert, n_used, x_sorted, w_gu, b_gu, w_down, b_down)


def _combine_kernel(yk_ref, gates_ref, h_ref, g_ref, b_ref, out_ref):
    gates = gates_ref[...]
    lo = hi = None
    for k in range(TOP_K):
        u = lax.bitcast_convert_type(yk_ref[k], jnp.uint32)
        gk = gates[:, k:k + 1]
        lo_k = lax.bitcast_convert_type(u << 16, F32) * gk
        hi_k = lax.bitcast_convert_type(u & jnp.uint32(HI_MASK), F32) * gk
        lo = lo_k if lo is None else lo + lo_k
        hi = hi_k if hi is None else hi + hi_k
    acc = ALPHA * h_ref[...] + jnp.concatenate([lo, hi], axis=1)
    out_ref[...] = _layer_norm(acc, g_ref[...], b_ref[...])


def _combine_kernel_aliased(yk_ref, gates_ref, h_ref, g_ref, b_ref, prev_ref, out_ref):
    del prev_ref
    _combine_kernel(yk_ref, gates_ref, h_ref, g_ref, b_ref, out_ref)


def _combine(yk, gates, h_all, ln_g, ln_b, row_offset, out_rows, out_offset, out_prev):
    n_seg = yk.shape[1]
    tile0 = row_offset // COMBINE_TOKENS
    out_tile0 = out_offset // COMBINE_TOKENS
    in_specs = [
        pl.BlockSpec((TOP_K, COMBINE_TOKENS, HALF), lambda i: (0, i, 0)),
        pl.BlockSpec((COMBINE_TOKENS, TOP_K), lambda i: (tile0 + i, 0)),
        pl.BlockSpec((COMBINE_TOKENS, D_MODEL), lambda i: (tile0 + i, 0)),
        _const_spec((1, D_MODEL)),
        _const_spec((1, D_MODEL)),
    ]
    args = [yk, gates, h_all, ln_g, ln_b]
    aliases = {}
    kern = _combine_kernel
    if out_prev is not None:
        in_specs.append(pl.BlockSpec(memory_space=pl.ANY))
        aliases = {len(args): 0}
        args.append(out_prev)
        kern = _combine_kernel_aliased
    return pl.pallas_call(
        kern,
        grid=(n_seg // COMBINE_TOKENS,),
        in_specs=in_specs,
        out_specs=pl.BlockSpec((COMBINE_TOKENS, D_MODEL), lambda i: (out_tile0 + i, 0)),
        out_shape=jax.ShapeDtypeStruct((out_rows, D_MODEL), F32),
        input_output_aliases=aliases,
        compiler_params=pltpu.CompilerParams(
            dimension_semantics=("arbitrary",), vmem_limit_bytes=VMEM_LIMIT),
        name="moe_combine",
    )(*args)


def kernel(x_prompt, x_sample, state_pool, state_gla, w_in, w_pool, pool_scale, w_gate_up, b_gate,
           gla_norm_w, w_out, ln1_g, ln1_b, w_router, b_router, w_gu, b_gu, w_down, b_down,
           ln2_g, ln2_b):
    assert w_in.shape[0] == 1, "single-layer kernel"
    bp, seq, _ = x_prompt.shape
    bs, dec_seq, _ = x_sample.shape
    assert dec_seq == CHUNK and seq % TILE_TOKENS == 0 and bs % CHUNKS_PER_TILE == 0
    n_prompt = bp * seq
    n_sample = bs * dec_seq
    n_total = n_prompt + n_sample
    nk = n_total * TOP_K
    n_blocks = -(-nk // MOE_ROWS) + NUM_EXPERTS
    m_pad = n_blocks * MOE_ROWS

    w_glr = jnp.zeros((D_MODEL, LANES), BF16).at[:, :GATE_RANK].set(w_in[0, :, N_MAIN:].astype(BF16))
    w_gate = jnp.zeros((LANES, GLA_HEADS * GLA_DK), BF16).at[:GATE_RANK].set(w_gate_up[0].astype(BF16))
    weights = (
        w_in[0, :, :N_MAIN].astype(BF16), w_glr, w_gate, b_gate[0][None, :],
        w_pool[0].astype(BF16), pool_scale[0][None, :], gla_norm_w[0][None, :],
        w_out[0].astype(BF16), ln1_g[0][None, :], ln1_b[0][None, :],
        w_router[0].T.astype(BF16), jnp.broadcast_to(b_router[0][:, None], (NUM_EXPERTS, LANES)),
    )
    assert len(weights) == N_MIXER_WEIGHTS

    *shared, hist_p, s_p = _mixer_prompt(x_prompt, weights, n_total)
    h_all, hb_all, logits_t, hist_s, s_s = _mixer_sample(
        x_sample, state_pool[0], state_gla[0], weights, shared, n_prompt)

    gates_t, dest_t, pad_end = _router(logits_t)
    pad_end = pad_end[:, 0].astype(jnp.int32)
    block_start = jnp.arange(n_blocks, dtype=jnp.int32) * MOE_ROWS
    block_expert = jnp.minimum(jnp.sum((block_start[:, None] >= pad_end[None, :]).astype(jnp.int32), axis=1),
                               NUM_EXPERTS - 1)
    n_used = (pad_end[-1:] // MOE_ROWS).astype(jnp.int32)

    dest_chunks = dest_t.reshape(TOP_K, n_total // DISPATCH_ROWS, DISPATCH_ROWS).transpose(1, 0, 2)
    x_sorted = _dispatch(hb_all, dest_chunks, m_pad)
    y_sorted = _moe_experts(block_expert, n_used, x_sorted, w_gu[0], b_gu[0][:, None, :],
                            w_down[0], b_down[0][:, None, :])
    gates = gates_t.T
    ln_g, ln_b = ln2_g[0][None, :], ln2_b[0][None, :]
    seg = n_prompt // COMBINE_SEGMENTS
    y_prompt = None
    for s in range(COMBINE_SEGMENTS):
        yk = _gather_expert_rows(y_sorted, dest_chunks, s * seg, seg)
        y_prompt = _combine(yk, gates, h_all, ln_g, ln_b, s * seg, n_prompt, s * seg, y_prompt)
    yk = _gather_expert_rows(y_sorted, dest_chunks, n_prompt, n_sample)
    y_sample = _combine(yk, gates, h_all, ln_g, ln_b, n_prompt, n_sample, 0, None)
    y_prompt = y_prompt.reshape(bp, seq, D_MODEL)
    y_sample = y_sample.reshape(bs, dec_seq, D_MODEL)
    return (y_prompt, y_sample, hist_p[None], s_p[None], hist_s[None], s_s[None])
```

```python
import functools

import jax
import jax.numpy as jnp
from jax import lax
from jax.experimental import pallas as pl
from jax.experimental.pallas import tpu as pltpu
from jax.experimental.pallas import tpu_sc as plsc

F32 = jnp.float32
BF16 = jnp.bfloat16

D_MODEL = 1024
CHUNK = 64
PAST_LEN = 1024
POOL_WIDTH = 512
POOL_WINDOWS = (2, 4, 8, 16)
POOL_GROUP = 128
POOL_HIST = 15
GLA_HEADS = 4
GLA_DK = 64
GLA_DV = 128
GATE_RANK = 16
GATE_NORMALIZER = 16.0
NUM_EXPERTS = 32
TOP_K = 4
EXPERT_FF = 1024
SWIGLU_LIMIT = 7.0
SWIGLU_ALPHA = 1.702
LN_EPS = 1e-5
RMS_EPS = 1e-6
ALPHA = 2.0 ** 0.25

Q0 = POOL_WIDTH
K0 = Q0 + GLA_HEADS * GLA_DK
V0 = K0 + GLA_HEADS * GLA_DK
R0 = V0 + GLA_HEADS * GLA_DV
N_MAIN = R0 + GLA_HEADS * GLA_DV

LANES = 128
TILE_TOKENS = 512
CHUNKS_PER_TILE = TILE_TOKENS // CHUNK
HIST_PAD = 16
OUT_CHUNKS = 4
MOE_ROWS = 512
ROUTE_TOKENS = 1024
COMBINE_TOKENS = 256
COMBINE_SEGMENTS = 4
VMEM_LIMIT = 56 * 1024 * 1024


def _dot(a, b):
    return jnp.dot(a, b, preferred_element_type=F32)


def _dot_nt(a, b):
    return lax.dot_general(a, b, (((1,), (1,)), ((), ())), preferred_element_type=F32)


def _dot_tn(a, b):
    return lax.dot_general(a, b, (((0,), (0,)), ((), ())), preferred_element_type=F32)


HALF = D_MODEL // 2
HI_MASK = 0xFFFF0000


def _pack_bf16_pairs(xb):
    lo = lax.bitcast_convert_type(xb[:, :HALF].astype(F32), jnp.uint32) >> 16
    hi = lax.bitcast_convert_type(xb[:, HALF:].astype(F32), jnp.uint32) & jnp.uint32(HI_MASK)
    return lax.bitcast_convert_type(hi | lo, jnp.int32)


def _unpack_bf16_pairs(p):
    u = lax.bitcast_convert_type(p, jnp.uint32)
    lo = lax.bitcast_convert_type(u << 16, F32)
    hi = lax.bitcast_convert_type(u & jnp.uint32(HI_MASK), F32)
    return jnp.concatenate([lo, hi], axis=1).astype(BF16)


def _layer_norm(v, g, b):
    mu = jnp.mean(v, axis=-1, keepdims=True)
    c = v - mu
    var = jnp.mean(c * c, axis=-1, keepdims=True)
    return c * lax.rsqrt(var + LN_EPS) * g + b


N_MIXER_WEIGHTS = 12
N_MIXER_SHARED_OUT = 3


def _mixer_kernel(per_chunk_state, pos0, *refs):
    if per_chunk_state:
        (x_ref, hist_in_ref, s_in_ref, *rest) = refs
    else:
        (x_ref, *rest) = refs
        hist_in_ref = s_in_ref = None
    (w_main_ref, w_glr_ref, w_gate_ref, b_gate_ref, w_pool_ref, pscale_ref, gnorm_ref,
     w_out_ref, ln1g_ref, ln1b_ref, w_router_ref, b_router_ref, *rest) = rest
    if per_chunk_state:
        rest = rest[N_MIXER_SHARED_OUT:]
    (h_ref, hb_ref, logits_ref, hist_out_ref, s_out_ref,
     proj_scr, b_scr, o_scr, ext_scr, st_scr, tri_scr) = rest

    if per_chunk_state:
        t = None
        first_step = pl.program_id(0) == 0
    else:
        t = pl.program_id(1)
        first_step = (pl.program_id(0) == 0) & (t == 0)
    x = x_ref[...].reshape(TILE_TOKENS, D_MODEL)
    xb = x.astype(BF16)

    @pl.when(first_step)
    def _():
        ti = lax.broadcasted_iota(jnp.int32, (TILE_TOKENS, TILE_TOKENS), 0)
        tj = lax.broadcasted_iota(jnp.int32, (TILE_TOKENS, TILE_TOKENS), 1)
        same_chunk = (ti // CHUNK) == (tj // CHUNK)
        tri_scr[...] = jnp.where(same_chunk & (ti >= tj), 1.0, 0.0).astype(BF16)
        if not per_chunk_state:
            st_scr[...] = jnp.zeros_like(st_scr)
            ext_scr[0:HIST_PAD, :] = jnp.zeros((HIST_PAD, POOL_WIDTH), F32)

    glr = _dot(xb, w_glr_ref[...])
    proj_scr[:, 0:V0] = _dot(xb, w_main_ref[:, 0:V0])
    gk = _dot(glr.astype(BF16), w_gate_ref[...]) + b_gate_ref[...]
    log_sig = jnp.minimum(gk, 0.0) - jnp.log1p(jnp.exp(-jnp.abs(gk)))
    g = log_sig / GATE_NORMALIZER
    g_hi = g.astype(BF16)
    g_lo = (g - g_hi.astype(F32)).astype(BF16)
    proj_scr[:, V0:N_MAIN] = _dot(xb, w_main_ref[:, V0:N_MAIN])
    b_scr[...] = _dot(tri_scr[...], g_hi) + _dot(tri_scr[...], g_lo)

    if per_chunk_state:
        seg_len, seg_stride, n_seg = CHUNK, CHUNK + HIST_PAD, CHUNKS_PER_TILE
        for c in range(n_seg):
            base = c * seg_stride
            ext_scr[base:base + HIST_PAD, :] = jnp.zeros((HIST_PAD, POOL_WIDTH), F32)
            ext_scr[base + 1:base + HIST_PAD, :] = hist_in_ref[c]
            ext_scr[base + HIST_PAD:base + seg_stride, :] = proj_scr[c * CHUNK:(c + 1) * CHUNK, 0:POOL_WIDTH]
        row_pos = pos0 + lax.broadcasted_iota(jnp.int32, (seg_len, POOL_GROUP), 0)
    else:
        seg_len, seg_stride, n_seg = TILE_TOKENS, TILE_TOKENS + HIST_PAD, 1
        ext_scr[0:HIST_PAD, :] = jnp.where(t == 0, 0.0, ext_scr[0:HIST_PAD, :])
        ext_scr[HIST_PAD:seg_stride, :] = proj_scr[:, 0:POOL_WIDTH]
        row_pos = pos0 + t * TILE_TOKENS + lax.broadcasted_iota(jnp.int32, (seg_len, POOL_GROUP), 0)

    pool_cols = []
    for gi, w in enumerate(POOL_WINDOWS):
        gs = slice(gi * POOL_GROUP, (gi + 1) * POOL_GROUP)
        cnt = jnp.minimum(row_pos + 1, w).astype(F32)
        ext = ext_scr[:, gs]
        win = ext
        shift = 1
        while shift < w:
            win = win + pltpu.roll(win, shift, 0)
            shift *= 2
        segs = []
        for s in range(n_seg):
            base = s * seg_stride + HIST_PAD
            segs.append(win[base:base + seg_len] / cnt - ext[base:base + seg_len])
        pooled = segs[0] if n_seg == 1 else jnp.concatenate(segs, axis=0)
        z = _dot(pooled.astype(BF16), w_pool_ref[gi])
        pool_cols.append(z * pscale_ref[:, gs])
    pool_out = jnp.concatenate(pool_cols, axis=1)

    if per_chunk_state:
        for c in range(n_seg):
            end = (c + 1) * seg_stride
            hist_out_ref[c] = ext_scr[end - POOL_HIST:end, :]
    else:
        hist_out_ref[...] = ext_scr[seg_stride - POOL_HIST:seg_stride, :]
        ext_scr[0:HIST_PAD, :] = ext_scr[TILE_TOKENS:seg_stride, :]

    hk = GLA_HEADS * GLA_DK
    hv = GLA_HEADS * GLA_DV
    pair_rows = 2 * CHUNK
    decay_cols = LANES // CHUNKS_PER_TILE

    def head_of(shape, dim, width):
        return lax.broadcasted_iota(jnp.int32, shape, dim) // width

    same_head_k = head_of((hk, hk), 0, CHUNK) == head_of((hk, hk), 1, GLA_DK)
    same_head_v = head_of((hk, hv), 0, CHUNK) == head_of((hk, hv), 1, GLA_DV)
    row_head = head_of((hk, GLA_DV), 0, GLA_DK)
    pair_half = head_of((pair_rows, hv), 0, CHUNK)
    causal = (lax.broadcasted_iota(jnp.int32, (CHUNK, hk), 0)
              >= lax.broadcasted_iota(jnp.int32, (CHUNK, hk), 1) % CHUNK)

    b_all = b_scr[...]
    b_last = [b_scr[(c + 1) * CHUNK - 1:(c + 1) * CHUNK, :] for c in range(CHUNKS_PER_TILE)]
    b_last_rows = jnp.concatenate([jnp.broadcast_to(bl, (CHUNK, hk)) for bl in b_last], axis=0)
    k_all = proj_scr[:, K0:V0]
    qt_all = (proj_scr[:, Q0:K0] * (GLA_DK ** -0.5) * jnp.exp(b_all)).astype(BF16)
    kt_all = k_all * jnp.exp(-b_all)
    kl_t = jnp.transpose(k_all * jnp.exp(b_last_rows - b_all)).astype(BF16)
    decay_t = jnp.transpose(jnp.exp(jnp.concatenate(
        [jnp.broadcast_to(bl, (decay_cols, hk)) for bl in b_last], axis=0)))

    def finish_rows(rs):
        r = proj_scr[rs, R0:N_MAIN]
        silu_r = r * (1.0 / (1.0 + jnp.exp(-r)))
        gated = []
        for h in range(GLA_HEADS):
            vs = slice(h * GLA_DV, (h + 1) * GLA_DV)
            oh = o_scr[rs, vs]
            ms = jnp.mean(oh * oh, axis=-1, keepdims=True)
            gated.append(oh * lax.rsqrt(ms + RMS_EPS) * gnorm_ref[...] * silu_r[:, vs])
        mix_in = jnp.concatenate([pool_out[rs]] + gated, axis=1).astype(BF16)
        resid = ALPHA * x[rs] + _dot(mix_in, w_out_ref[...])
        h_val = _layer_norm(resid, ln1g_ref[...], ln1b_ref[...])
        h_ref[rs, :] = h_val
        hb = h_val.astype(BF16)
        hb_ref[rs, :] = _pack_bf16_pairs(hb)
        logits_ref[:, rs] = _dot_nt(w_router_ref[...], hb) + b_router_ref[:, 0:1]

    st = None if per_chunk_state else jnp.where(t == 0, 0.0, st_scr[...])
    for c in range(CHUNKS_PER_TILE):
        rows = slice(c * CHUNK, (c + 1) * CHUNK)
        pair = slice((c // 2) * pair_rows, (c // 2 + 1) * pair_rows)
        if per_chunk_state:
            st = s_in_ref[c].reshape(hk, GLA_DV)
        qt = qt_all[rows]
        k_stack = jnp.where(same_head_k, jnp.concatenate([kt_all[rows]] * GLA_HEADS, axis=0), 0.0)
        v_stack = jnp.where(same_head_v, jnp.concatenate([proj_scr[rows, V0:R0]] * GLA_HEADS, axis=0), 0.0)
        s_stack = jnp.where(same_head_v, jnp.concatenate([st] * GLA_HEADS, axis=1), 0.0)
        att = jnp.where(causal, _dot_nt(qt, k_stack.astype(BF16)), 0.0)
        o_scr[rows, :] = (_dot(att.astype(BF16), v_stack.astype(BF16))
                          + _dot(qt, s_stack.astype(BF16)))
        v_chunk = jnp.where(pair_half == c % 2, proj_scr[pair, V0:R0], 0.0).astype(BF16)
        kv = _dot(kl_t[:, pair], v_chunk)
        upd = jnp.zeros((hk, GLA_DV), F32)
        for h in range(GLA_HEADS):
            upd = jnp.where(row_head == h, kv[:, h * GLA_DV:(h + 1) * GLA_DV], upd)
        st = st * decay_t[:, c * decay_cols:c * decay_cols + 1] + upd
        if per_chunk_state:
            s_out_ref[c] = st.reshape(GLA_HEADS, GLA_DK, GLA_DV)
        if (c + 1) % OUT_CHUNKS == 0:
            finish_rows(slice((c + 1 - OUT_CHUNKS) * CHUNK, (c + 1) * CHUNK))

    if not per_chunk_state:
        st_scr[...] = st
        s_out_ref[...] = st.reshape(GLA_HEADS, GLA_DK, GLA_DV)


def _const_spec(shape):
    nd = len(shape)
    return pl.BlockSpec(shape, lambda *_: (0,) * nd)


def _mixer_weight_specs():
    return [
        _const_spec((D_MODEL, N_MAIN)),
        _const_spec((D_MODEL, LANES)),
        _const_spec((LANES, GLA_HEADS * GLA_DK)),
        _const_spec((1, GLA_HEADS * GLA_DK)),
        _const_spec((len(POOL_WINDOWS), POOL_GROUP, POOL_GROUP)),
        _const_spec((1, POOL_WIDTH)),
        _const_spec((1, GLA_DV)),
        _const_spec((D_MODEL, D_MODEL)),
        _const_spec((1, D_MODEL)),
        _const_spec((1, D_MODEL)),
        _const_spec((NUM_EXPERTS, D_MODEL)),
        _const_spec((NUM_EXPERTS, LANES)),
    ]


def _mixer_weights(w_in, w_pool, pool_scale, w_gate_up, b_gate, gla_norm_w, w_out, ln1_g, ln1_b,
                   w_router, b_router):
    w_glr = jnp.zeros((D_MODEL, LANES), BF16).at[:, :GATE_RANK].set(w_in[0, :, N_MAIN:].astype(BF16))
    w_gate = jnp.zeros((LANES, GLA_HEADS * GLA_DK), BF16).at[:GATE_RANK].set(w_gate_up[0].astype(BF16))
    weights = (
        w_in[0, :, :N_MAIN].astype(BF16), w_glr, w_gate, b_gate[0][None, :],
        w_pool[0].astype(BF16), pool_scale[0][None, :], gla_norm_w[0][None, :],
        w_out[0].astype(BF16), ln1_g[0][None, :], ln1_b[0][None, :],
        w_router[0].T.astype(BF16), jnp.broadcast_to(b_router[0][:, None], (NUM_EXPERTS, LANES)),
    )
    assert len(weights) == N_MIXER_WEIGHTS
    return weights


def _mixer_scratch(per_chunk_state):
    ext_rows = (CHUNKS_PER_TILE * (CHUNK + HIST_PAD)) if per_chunk_state else (TILE_TOKENS + HIST_PAD)
    return [
        pltpu.VMEM((TILE_TOKENS, N_MAIN), F32),
        pltpu.VMEM((TILE_TOKENS, GLA_HEADS * GLA_DK), F32),
        pltpu.VMEM((TILE_TOKENS, GLA_HEADS * GLA_DV), F32),
        pltpu.VMEM((ext_rows, POOL_WIDTH), F32),
        pltpu.VMEM((GLA_HEADS * GLA_DK, GLA_DV), F32),
        pltpu.VMEM((TILE_TOKENS, TILE_TOKENS), BF16),
    ]


def _mixer_out_shapes(n_total, bsz):
    return (
        jax.ShapeDtypeStruct((n_total, D_MODEL), F32),
        jax.ShapeDtypeStruct((n_total, HALF), jnp.int32),
        jax.ShapeDtypeStruct((NUM_EXPERTS, n_total), F32),
        jax.ShapeDtypeStruct((bsz, POOL_HIST, POOL_WIDTH), F32),
        jax.ShapeDtypeStruct((bsz, GLA_HEADS, GLA_DK, GLA_DV), F32),
    )


def _mixer_prompt(x, weights, n_total):
    bsz, seq, _ = x.shape
    tiles = seq // TILE_TOKENS
    return pl.pallas_call(
        functools.partial(_mixer_kernel, False, 0),
        grid=(bsz, tiles),
        in_specs=[pl.BlockSpec((None, TILE_TOKENS, D_MODEL), lambda b, t: (b, t, 0))] + _mixer_weight_specs(),
        out_specs=(
            pl.BlockSpec((TILE_TOKENS, D_MODEL), lambda b, t: (b * tiles + t, 0)),
            pl.BlockSpec((TILE_TOKENS, HALF), lambda b, t: (b * tiles + t, 0)),
            pl.BlockSpec((NUM_EXPERTS, TILE_TOKENS), lambda b, t: (0, b * tiles + t)),
            pl.BlockSpec((None, POOL_HIST, POOL_WIDTH), lambda b, t: (b, 0, 0)),
            pl.BlockSpec((None, GLA_HEADS, GLA_DK, GLA_DV), lambda b, t: (b, 0, 0, 0)),
        ),
        out_shape=_mixer_out_shapes(n_total, bsz),
        scratch_shapes=_mixer_scratch(False),
        compiler_params=pltpu.CompilerParams(
            dimension_semantics=("arbitrary", "arbitrary"), vmem_limit_bytes=VMEM_LIMIT),
        name="mixer_prompt",
    )(x, *weights)


def _mixer_sample(x, hist, state, weights, shared, row_offset):
    bsz = x.shape[0]
    tiles = bsz // CHUNKS_PER_TILE
    tile0 = row_offset // TILE_TOKENS
    n_total = shared[0].shape[0]
    n_in = 3 + N_MIXER_WEIGHTS
    return pl.pallas_call(
        functools.partial(_mixer_kernel, True, PAST_LEN),
        grid=(tiles,),
        in_specs=[
            pl.BlockSpec((CHUNKS_PER_TILE, CHUNK, D_MODEL), lambda i: (i, 0, 0)),
            pl.BlockSpec((CHUNKS_PER_TILE, POOL_HIST, POOL_WIDTH), lambda i: (i, 0, 0)),
            pl.BlockSpec((CHUNKS_PER_TILE, GLA_HEADS, GLA_DK, GLA_DV), lambda i: (i, 0, 0, 0)),
        ] + _mixer_weight_specs() + [pl.BlockSpec(memory_space=pl.ANY)] * N_MIXER_SHARED_OUT,
        out_specs=(
            pl.BlockSpec((TILE_TOKENS, D_MODEL), lambda i: (tile0 + i, 0)),
            pl.BlockSpec((TILE_TOKENS, HALF), lambda i: (tile0 + i, 0)),
            pl.BlockSpec((NUM_EXPERTS, TILE_TOKENS), lambda i: (0, tile0 + i)),
            pl.BlockSpec((CHUNKS_PER_TILE, POOL_HIST, POOL_WIDTH), lambda i: (i, 0, 0)),
            pl.BlockSpec((CHUNKS_PER_TILE, GLA_HEADS, GLA_DK, GLA_DV), lambda i: (i, 0, 0, 0)),
        ),
        out_shape=_mixer_out_shapes(n_total, bsz),
        input_output_aliases={n_in + j: j for j in range(N_MIXER_SHARED_OUT)},
        scratch_shapes=_mixer_scratch(True),
        compiler_params=pltpu.CompilerParams(
            dimension_semantics=("arbitrary",), vmem_limit_bytes=VMEM_LIMIT),
        name="mixer_sample",
    )(x, hist, state, *weights, *shared)


def _router_kernel(lt_ref, gates_ref, dest_ref, padend_ref, cnt_scr, base_scr, pstart_scr, before_scr):
    phase = pl.program_id(0)
    i = pl.program_id(1)
    shape = (NUM_EXPERTS, ROUTE_TOKENS)
    logits = lt_ref[...]
    row = lax.broadcasted_iota(jnp.int32, shape, 0)
    sel, vals = [], []
    for _ in range(TOP_K):
        m = jnp.max(logits, axis=0, keepdims=True)
        idx = jnp.min(jnp.where(logits == m, row, NUM_EXPERTS), axis=0, keepdims=True)
        hit = row == idx
        sel.append(hit)
        vals.append(m)
        logits = jnp.where(hit, -jnp.inf, logits)
    chosen = sum(jnp.where(hit, 1.0, 0.0) for hit in sel)
    tile_counts = jnp.broadcast_to(jnp.sum(chosen, axis=1, keepdims=True), (NUM_EXPERTS, LANES))

    @pl.when(phase == 0)
    def _():
        @pl.when(i == 0)
        def _():
            cnt_scr[...] = jnp.zeros_like(cnt_scr)

        cnt_scr[...] += tile_counts

    @pl.when(phase == 1)
    def _():
        @pl.when(i == 0)
        def _():
            blocks = jnp.floor((cnt_scr[...] + (MOE_ROWS - 1)) * (1.0 / MOE_ROWS))
            erow = lax.broadcasted_iota(jnp.int32, (NUM_EXPERTS, LANES), 0)
            cum = blocks
            shift = 1
            while shift < NUM_EXPERTS:
                cum = cum + jnp.where(erow >= shift, pltpu.roll(cum, shift, 0), 0.0)
                shift *= 2
            padend_ref[...] = cum * MOE_ROWS
            pstart_scr[...] = (cum - blocks) * MOE_ROWS
            base_scr[...] = jnp.zeros_like(base_scr)
            ti = lax.broadcasted_iota(jnp.int32, (ROUTE_TOKENS, ROUTE_TOKENS), 0)
            tj = lax.broadcasted_iota(jnp.int32, (ROUTE_TOKENS, ROUTE_TOKENS), 1)
            before_scr[...] = jnp.where(ti < tj, 1.0, 0.0).astype(BF16)

        earlier = _dot(chosen.astype(BF16), before_scr[...])
        pos = pstart_scr[:, 0:1] + base_scr[:, 0:1] + earlier
        dest = [jnp.sum(jnp.where(hit, pos, 0.0), axis=0, keepdims=True) for hit in sel]
        dest_ref[...] = jnp.concatenate(dest, axis=0).astype(jnp.int32)
        ex = [jnp.exp(v - vals[0]) for v in vals]
        denom = ex[0] + ex[1] + ex[2] + ex[3]
        gates_ref[...] = jnp.concatenate([e / denom for e in ex], axis=0)
        base_scr[...] += tile_counts


def _router(logits_t):
    n = logits_t.shape[1]
    assert n % ROUTE_TOKENS == 0
    tiles = n // ROUTE_TOKENS
    return pl.pallas_call(
        _router_kernel,
        grid=(2, tiles),
        in_specs=[pl.BlockSpec((NUM_EXPERTS, ROUTE_TOKENS), lambda p, i: (0, i))],
        out_specs=(
            pl.BlockSpec((TOP_K, ROUTE_TOKENS), lambda p, i: (0, i * p)),
            pl.BlockSpec((TOP_K, ROUTE_TOKENS), lambda p, i: (0, i * p)),
            pl.BlockSpec((NUM_EXPERTS, LANES), lambda p, i: (0, 0)),
        ),
        out_shape=(
            jax.ShapeDtypeStruct((TOP_K, n), F32),
            jax.ShapeDtypeStruct((TOP_K, n), jnp.int32),
            jax.ShapeDtypeStruct((NUM_EXPERTS, LANES), F32),
        ),
        scratch_shapes=[pltpu.VMEM((NUM_EXPERTS, LANES), F32)] * 3
        + [pltpu.VMEM((ROUTE_TOKENS, ROUTE_TOKENS), BF16)],
        compiler_params=pltpu.CompilerParams(
            dimension_semantics=("arbitrary", "arbitrary"), vmem_limit_bytes=VMEM_LIMIT),
        name="router",
    )(logits_t)


SC_CORES = 2
SC_SUBCORES = 16
SC_WORKERS = SC_CORES * SC_SUBCORES
DISPATCH_ROWS = 64


def _dispatch(h_packed, dest_chunks, m_pad):
    n = h_packed.shape[0]
    n_chunks = n // DISPATCH_ROWS
    assert n_chunks % SC_WORKERS == 0
    per_worker = n_chunks // SC_WORKERS
    mesh = plsc.VectorSubcoreMesh(core_axis_name="c", subcore_axis_name="s")

    @functools.partial(
        pl.kernel, mesh=mesh,
        out_type=jax.ShapeDtypeStruct((m_pad, HALF), jnp.int32),
        scratch_types=[
            pltpu.VMEM((TOP_K, DISPATCH_ROWS), jnp.int32),
            pltpu.VMEM((DISPATCH_ROWS, HALF), jnp.int32),
            pltpu.SemaphoreType.DMA,
        ],
        compiler_params=pltpu.CompilerParams(use_tc_tiling_on_sc=True),
        name="dispatch",
    )
    def dispatch_kernel(h_hbm, dest_hbm, out_hbm, idx_v, rows_v, sem):
        wid = lax.axis_index("s") * SC_CORES + lax.axis_index("c")

        @pl.loop(0, per_worker)
        def _(j):
            chunk = wid * per_worker + j
            pltpu.sync_copy(dest_hbm.at[chunk], idx_v)
            pltpu.sync_copy(h_hbm.at[pl.ds(chunk * DISPATCH_ROWS, DISPATCH_ROWS)], rows_v)
            for k in range(TOP_K):
                pltpu.async_copy(rows_v, out_hbm.at[idx_v.at[k]], sem).wait()

    return dispatch_kernel(h_packed, dest_chunks)


def _gather_expert_rows(y_sorted, dest_chunks, row_offset, n):
    n_chunks = n // DISPATCH_ROWS
    assert n_chunks % SC_WORKERS == 0 and row_offset % DISPATCH_ROWS == 0
    per_worker = n_chunks // SC_WORKERS
    chunk0 = row_offset // DISPATCH_ROWS
    mesh = plsc.VectorSubcoreMesh(core_axis_name="c", subcore_axis_name="s")

    @functools.partial(
        pl.kernel, mesh=mesh,
        out_type=jax.ShapeDtypeStruct((TOP_K, n, HALF), jnp.int32),
        scratch_types=[
            pltpu.VMEM((TOP_K, DISPATCH_ROWS), jnp.int32),
            pltpu.VMEM((2, DISPATCH_ROWS, HALF), jnp.int32),
            pltpu.SemaphoreType.DMA((2,)),
        ],
        compiler_params=pltpu.CompilerParams(use_tc_tiling_on_sc=True),
        name="gather_expert_rows",
    )
    def gather_kernel(y_hbm, dest_hbm, out_hbm, idx_v, rows_v, sems):
        wid = lax.axis_index("s") * SC_CORES + lax.axis_index("c")

        def gather(k):
            return pltpu.make_async_copy(y_hbm.at[idx_v.at[k]], rows_v.at[k % 2], sems.at[k % 2])

        @pl.loop(0, per_worker)
        def _(j):
            local = wid * per_worker + j
            pltpu.sync_copy(dest_hbm.at[chunk0 + local], idx_v)
            gather(0).start()
            for k in range(TOP_K):
                if k + 1 < TOP_K:
                    gather(k + 1).start()
                gather(k).wait()
                pltpu.sync_copy(rows_v.at[k % 2],
                                out_hbm.at[k, pl.ds(local * DISPATCH_ROWS, DISPATCH_ROWS)])

    return gather_kernel(y_sorted, dest_chunks)


def _moe_kernel(be_ref, nused_ref, x_ref, wgu_ref, bgu_ref, wd_ref, bd_ref, y_ref, wgu_bf, wd_bf):
    i = pl.program_id(0)

    @pl.when(i < nused_ref[0])
    def _():
        @pl.when((i == 0) | (be_ref[i] != be_ref[jnp.maximum(i - 1, 0)]))
        def _():
            wgu_bf[...] = wgu_ref[...].astype(BF16)
            wd_bf[...] = wd_ref[...].astype(BF16)

        gu = _dot(_unpack_bf16_pairs(x_ref[...]), wgu_bf[...]) + bgu_ref[...]
        gate = jnp.minimum(gu[:, :EXPERT_FF], SWIGLU_LIMIT)
        up = jnp.clip(gu[:, EXPERT_FF:], -SWIGLU_LIMIT, SWIGLU_LIMIT)
        hmid = gate * (1.0 / (1.0 + jnp.exp(-SWIGLU_ALPHA * gate))) * (up + 1.0)
        y = _dot(hmid.astype(BF16), wd_bf[...]) + bd_ref[...]
        y_ref[...] = _pack_bf16_pairs(y.astype(BF16))


def _moe_experts(block_expert, n_used, x_sorted, w_gu, b_gu, w_down, b_down):
    m_pad = x_sorted.shape[0]
    n_blocks = m_pad // MOE_ROWS

    def blk(i, be, nu):
        return jnp.minimum(i, nu[0] - 1)

    def expert(i, be, nu):
        return be[blk(i, be, nu)]

    grid_spec = pltpu.PrefetchScalarGridSpec(
        num_scalar_prefetch=2,
        grid=(n_blocks,),
        in_specs=[
            pl.BlockSpec((MOE_ROWS, HALF), lambda i, be, nu: (blk(i, be, nu), 0)),
            pl.BlockSpec((None, D_MODEL, 2 * EXPERT_FF), lambda i, be, nu: (expert(i, be, nu), 0, 0)),
            pl.BlockSpec((None, 1, 2 * EXPERT_FF), lambda i, be, nu: (expert(i, be, nu), 0, 0)),
            pl.BlockSpec((None, EXPERT_FF, D_MODEL), lambda i, be, nu: (expert(i, be, nu), 0, 0)),
            pl.BlockSpec((None, 1, D_MODEL), lambda i, be, nu: (expert(i, be, nu), 0, 0)),
        ],
        out_specs=pl.BlockSpec((MOE_ROWS, HALF), lambda i, be, nu: (blk(i, be, nu), 0)),
        scratch_shapes=[
            pltpu.VMEM((D_MODEL, 2 * EXPERT_FF), BF16),
            pltpu.VMEM((EXPERT_FF, D_MODEL), BF16),
        ],
    )
    return pl.pallas_call(
        _moe_kernel,
        grid_spec=grid_spec,
        out_shape=jax.ShapeDtypeStruct((m_pad, HALF), jnp.int32),
        compiler_params=pltpu.CompilerParams(
            dimension_semantics=("arbitrary",), vmem_limit_bytes=VMEM_LIMIT),
        name="moe_experts",
    )(block_expert, n_used, x_sorted, w_gu, b_gu, w_down, b_down)


def _combine_kernel(yk_ref, gates_ref, h_ref, g_ref, b_ref, out_ref):
    gates = gates_ref[...]
    lo = hi = None
    for k in range(TOP_K):
        u = lax.bitcast_convert_type(yk_ref[k], jnp.uint32)
        gk = gates[:, k:k + 1]
        lo_k = lax.bitcast_convert_type(u << 16, F32) * gk
        hi_k = lax.bitcast_convert_type(u & jnp.uint32(HI_MASK), F32) * gk
        lo = lo_k if lo is None else lo + lo_k
        hi = hi_k if hi is None else hi + hi_k
    acc = ALPHA * h_ref[...] + jnp.concatenate([lo, hi], axis=1)
    out_ref[...] = _layer_norm(acc, g_ref[...], b_ref[...])


def _combine_kernel_aliased(yk_ref, gates_ref, h_ref, g_ref, b_ref, prev_ref, out_ref):
    del prev_ref
    _combine_kernel(yk_ref, gates_ref, h_ref, g_ref, b_ref, out_ref)


def _combine(yk, gates, h_all, ln_g, ln_b, row_offset, out_rows, out_offset, out_prev):
    n_seg = yk.shape[1]
    tile0 = row_offset // COMBINE_TOKENS
    out_tile0 = out_offset // COMBINE_TOKENS
    in_specs = [
        pl.BlockSpec((TOP_K, COMBINE_TOKENS, HALF), lambda i: (0, i, 0)),
        pl.BlockSpec((COMBINE_TOKENS, TOP_K), lambda i: (tile0 + i, 0)),
        pl.BlockSpec((COMBINE_TOKENS, D_MODEL), lambda i: (tile0 + i, 0)),
        _const_spec((1, D_MODEL)),
        _const_spec((1, D_MODEL)),
    ]
    args = [yk, gates, h_all, ln_g, ln_b]
    aliases = {}
    kern = _combine_kernel
    if out_prev is not None:
        in_specs.append(pl.BlockSpec(memory_space=pl.ANY))
        aliases = {len(args): 0}
        args.append(out_prev)
        kern = _combine_kernel_aliased
    return pl.pallas_call(
        kern,
        grid=(n_seg // COMBINE_TOKENS,),
        in_specs=in_specs,
        out_specs=pl.BlockSpec((COMBINE_TOKENS, D_MODEL), lambda i: (out_tile0 + i, 0)),
        out_shape=jax.ShapeDtypeStruct((out_rows, D_MODEL), F32),
        input_output_aliases=aliases,
        compiler_params=pltpu.CompilerParams(
            dimension_semantics=("arbitrary",), vmem_limit_bytes=VMEM_LIMIT),
        name="moe_combine",
    )(*args)


def kernel(x_prompt, x_sample, state_pool, state_gla, w_in, w_pool, pool_scale, w_gate_up, b_gate,
           gla_norm_w, w_out, ln1_g, ln1_b, w_router, b_router, w_gu, b_gu, w_down, b_down,
           ln2_g, ln2_b):
    assert w_in.shape[0] == 1, "single-layer kernel"
    bp, seq, _ = x_prompt.shape
    bs, dec_seq, _ = x_sample.shape
    assert dec_seq == CHUNK and seq % TILE_TOKENS == 0 and bs % CHUNKS_PER_TILE == 0
    n_prompt = bp * seq
    n_sample = bs * dec_seq
    n_total = n_prompt + n_sample
    nk = n_total * TOP_K
    n_blocks = -(-nk // MOE_ROWS) + NUM_EXPERTS
    m_pad = n_blocks * MOE_ROWS

    weights = _mixer_weights(w_in, w_pool, pool_scale, w_gate_up, b_gate, gla_norm_w, w_out,
                             ln1_g, ln1_b, w_router, b_router)

    *shared, hist_p, s_p = _mixer_prompt(x_prompt, weights, n_total)
    h_all, hb_all, logits_t, hist_s, s_s = _mixer_sample(
        x_sample, state_pool[0], state_gla[0], weights, shared, n_prompt)

    gates_t, dest_t, pad_end = _router(logits_t)
    pad_end = pad_end[:, 0].astype(jnp.int32)
    block_start = jnp.arange(n_blocks, dtype=jnp.int32) * MOE_ROWS
    block_expert = jnp.minimum(jnp.sum((block_start[:, None] >= pad_end[None, :]).astype(jnp.int32), axis=1),
                               NUM_EXPERTS - 1)
    n_used = (pad_end[-1:] // MOE_ROWS).astype(jnp.int32)

    dest_chunks = dest_t.reshape(TOP_K, n_total // DISPATCH_ROWS, DISPATCH_ROWS).transpose(1, 0, 2)
    x_sorted = _dispatch(hb_all, dest_chunks, m_pad)
    y_sorted = _moe_experts(block_expert, n_used, x_sorted, w_gu[0], b_gu[0][:, None, :],
                            w_down[0], b_down[0][:, None, :])
    gates = gates_t.T
    ln_g, ln_b = ln2_g[0][None, :], ln2_b[0][None, :]
    seg = n_prompt // COMBINE_SEGMENTS
    y_prompt = None
    for s in range(COMBINE_SEGMENTS):
        yk = _gather_expert_rows(y_sorted, dest_chunks, s * seg, seg)
        y_prompt = _combine(yk, gates, h_all, ln_g, ln_b, s * seg, n_prompt, s * seg, y_prompt)
    yk = _gather_expert_rows(y_sorted, dest_chunks, n_prompt, n_sample)
    y_sample = _combine(yk, gates, h_all, ln_g, ln_b, n_prompt, n_sample, 0, None)
    y_prompt = y_prompt.reshape(bp, seq, D_MODEL)
    y_sample = y_sample.reshape(bs, dec_seq, D_MODEL)
    return (y_prompt, y_sample, hist_p[None], s_p[None], hist_s[None], s_s[None])
```

```python
import functools

import jax
import jax.numpy as jnp
from jax import lax
from jax.experimental import pallas as pl
from jax.experimental.pallas import tpu as pltpu
from jax.experimental.pallas import tpu_sc as plsc

F32 = jnp.float32
BF16 = jnp.bfloat16

D_MODEL = 1024
CHUNK = 64
PAST_LEN = 1024
POOL_WIDTH = 512
POOL_WINDOWS = (2, 4, 8, 16)
POOL_GROUP = 128
POOL_HIST = 15
GLA_HEADS = 4
GLA_DK = 64
GLA_DV = 128
GATE_RANK = 16
GATE_NORMALIZER = 16.0
NUM_EXPERTS = 32
TOP_K = 4
EXPERT_FF = 1024
SWIGLU_LIMIT = 7.0
SWIGLU_ALPHA = 1.702
LN_EPS = 1e-5
RMS_EPS = 1e-6
ALPHA = 2.0 ** 0.25

Q0 = POOL_WIDTH
K0 = Q0 + GLA_HEADS * GLA_DK
V0 = K0 + GLA_HEADS * GLA_DK
R0 = V0 + GLA_HEADS * GLA_DV
N_MAIN = R0 + GLA_HEADS * GLA_DV

LANES = 128
TILE_TOKENS = 512
CHUNKS_PER_TILE = TILE_TOKENS // CHUNK
HIST_PAD = 16
OUT_CHUNKS = 4
MOE_ROWS = 512
ROUTE_TOKENS = 1024
COMBINE_TOKENS = 256
COMBINE_SEGMENTS = 4
VMEM_LIMIT = 56 * 1024 * 1024


def _dot(a, b):
    return jnp.dot(a, b, preferred_element_type=F32)


def _dot_nt(a, b):
    return lax.dot_general(a, b, (((1,), (1,)), ((), ())), preferred_element_type=F32)


def _dot_tn(a, b):
    return lax.dot_general(a, b, (((0,), (0,)), ((), ())), preferred_element_type=F32)


HALF = D_MODEL // 2
HI_MASK = 0xFFFF0000


def _pack_bf16_pairs(xb):
    lo = lax.bitcast_convert_type(xb[:, :HALF].astype(F32), jnp.uint32) >> 16
    hi = lax.bitcast_convert_type(xb[:, HALF:].astype(F32), jnp.uint32) & jnp.uint32(HI_MASK)
    return lax.bitcast_convert_type(hi | lo, jnp.int32)


def _unpack_bf16_pairs(p):
    u = lax.bitcast_convert_type(p, jnp.uint32)
    lo = lax.bitcast_convert_type(u << 16, F32)
    hi = lax.bitcast_convert_type(u & jnp.uint32(HI_MASK), F32)
    return jnp.concatenate([lo, hi], axis=1).astype(BF16)


def _layer_norm(v, g, b):
    mu = jnp.mean(v, axis=-1, keepdims=True)
    c = v - mu
    var = jnp.mean(c * c, axis=-1, keepdims=True)
    return c * lax.rsqrt(var + LN_EPS) * g + b


N_MIXER_WEIGHTS = 12
N_MIXER_SHARED_OUT = 3


def _mixer_kernel(per_chunk_state, pos0, *refs):
    if per_chunk_state:
        (x_ref, hist_in_ref, s_in_ref, *rest) = refs
    else:
        (x_ref, *rest) = refs
        hist_in_ref = s_in_ref = None
    (w_main_ref, w_glr_ref, w_gate_ref, b_gate_ref, w_pool_ref, pscale_ref, gnorm_ref,
     w_out_ref, ln1g_ref, ln1b_ref, w_router_ref, b_router_ref, *rest) = rest
    if per_chunk_state:
        rest = rest[N_MIXER_SHARED_OUT:]
    (h_ref, hb_ref, logits_ref, hist_out_ref, s_out_ref,
     proj_scr, b_scr, o_scr, ext_scr, st_scr, tri_scr) = rest

    if per_chunk_state:
        t = None
        first_step = pl.program_id(0) == 0
    else:
        t = pl.program_id(1)
        first_step = (pl.program_id(0) == 0) & (t == 0)
    x = x_ref[...].reshape(TILE_TOKENS, D_MODEL)
    xb = x.astype(BF16)

    @pl.when(first_step)
    def _():
        ti = lax.broadcasted_iota(jnp.int32, (TILE_TOKENS, TILE_TOKENS), 0)
        tj = lax.broadcasted_iota(jnp.int32, (TILE_TOKENS, TILE_TOKENS), 1)
        same_chunk = (ti // CHUNK) == (tj // CHUNK)
        tri_scr[...] = jnp.where(same_chunk & (ti >= tj), 1.0, 0.0).astype(BF16)
        if not per_chunk_state:
            st_scr[...] = jnp.zeros_like(st_scr)
            ext_scr[0:HIST_PAD, :] = jnp.zeros((HIST_PAD, POOL_WIDTH), F32)

    glr = _dot(xb, w_glr_ref[...])
    proj_scr[:, 0:V0] = _dot(xb, w_main_ref[:, 0:V0])
    gk = _dot(glr.astype(BF16), w_gate_ref[...]) + b_gate_ref[...]
    log_sig = jnp.minimum(gk, 0.0) - jnp.log1p(jnp.exp(-jnp.abs(gk)))
    g = log_sig / GATE_NORMALIZER
    g_hi = g.astype(BF16)
    g_lo = (g - g_hi.astype(F32)).astype(BF16)
    proj_scr[:, V0:N_MAIN] = _dot(xb, w_main_ref[:, V0:N_MAIN])
    b_scr[...] = _dot(tri_scr[...], g_hi) + _dot(tri_scr[...], g_lo)

    if per_chunk_state:
        seg_len, seg_stride, n_seg = CHUNK, CHUNK + HIST_PAD, CHUNKS_PER_TILE
        for c in range(n_seg):
            base = c * seg_stride
            ext_scr[base:base + HIST_PAD, :] = jnp.zeros((HIST_PAD, POOL_WIDTH), F32)
            ext_scr[base + 1:base + HIST_PAD, :] = hist_in_ref[c]
            ext_scr[base + HIST_PAD:base + seg_stride, :] = proj_scr[c * CHUNK:(c + 1) * CHUNK, 0:POOL_WIDTH]
        row_pos = pos0 + lax.broadcasted_iota(jnp.int32, (seg_len, POOL_GROUP), 0)
    else:
        seg_len, seg_stride, n_seg = TILE_TOKENS, TILE_TOKENS + HIST_PAD, 1
        ext_scr[0:HIST_PAD, :] = jnp.where(t == 0, 0.0, ext_scr[0:HIST_PAD, :])
        ext_scr[HIST_PAD:seg_stride, :] = proj_scr[:, 0:POOL_WIDTH]
        row_pos = pos0 + t * TILE_TOKENS + lax.broadcasted_iota(jnp.int32, (seg_len, POOL_GROUP), 0)

    pool_cols = []
    for gi, w in enumerate(POOL_WINDOWS):
        gs = slice(gi * POOL_GROUP, (gi + 1) * POOL_GROUP)
        cnt = jnp.minimum(row_pos + 1, w).astype(F32)
        ext = ext_scr[:, gs]
        win = ext
        shift = 1
        while shift < w:
            win = win + pltpu.roll(win, shift, 0)
            shift *= 2
        segs = []
        for s in range(n_seg):
            base = s * seg_stride + HIST_PAD
            segs.append(win[base:base + seg_len] / cnt - ext[base:base + seg_len])
        pooled = segs[0] if n_seg == 1 else jnp.concatenate(segs, axis=0)
        z = _dot(pooled.astype(BF16), w_pool_ref[gi])
        pool_cols.append(z * pscale_ref[:, gs])
    pool_out = jnp.concatenate(pool_cols, axis=1)

    if per_chunk_state:
        for c in range(n_seg):
            end = (c + 1) * seg_stride
            hist_out_ref[c] = ext_scr[end - POOL_HIST:end, :]
    else:
        hist_out_ref[...] = ext_scr[seg_stride - POOL_HIST:seg_stride, :]
        ext_scr[0:HIST_PAD, :] = ext_scr[TILE_TOKENS:seg_stride, :]

    hk = GLA_HEADS * GLA_DK
    hv = GLA_HEADS * GLA_DV
    pair_rows = 2 * CHUNK
    decay_cols = LANES // CHUNKS_PER_TILE

    def head_of(shape, dim, width):
        return lax.broadcasted_iota(jnp.int32, shape, dim) // width

    same_head_k = head_of((hk, hk), 0, CHUNK) == head_of((hk, hk), 1, GLA_DK)
    same_head_v = head_of((hk, hv), 0, CHUNK) == head_of((hk, hv), 1, GLA_DV)
    row_head = head_of((hk, GLA_DV), 0, GLA_DK)
    pair_half = head_of((pair_rows, hv), 0, CHUNK)
    causal = (lax.broadcasted_iota(jnp.int32, (CHUNK, hk), 0)
              >= lax.broadcasted_iota(jnp.int32, (CHUNK, hk), 1) % CHUNK)

    b_all = b_scr[...]
    b_last = [b_scr[(c + 1) * CHUNK - 1:(c + 1) * CHUNK, :] for c in range(CHUNKS_PER_TILE)]
    b_last_rows = jnp.concatenate([jnp.broadcast_to(bl, (CHUNK, hk)) for bl in b_last], axis=0)
    k_all = proj_scr[:, K0:V0]
    qt_all = (proj_scr[:, Q0:K0] * (GLA_DK ** -0.5) * jnp.exp(b_all)).astype(BF16)
    kt_all = k_all * jnp.exp(-b_all)
    kl_t = jnp.transpose(k_all * jnp.exp(b_last_rows - b_all)).astype(BF16)
    decay_t = jnp.transpose(jnp.exp(jnp.concatenate(
        [jnp.broadcast_to(bl, (decay_cols, hk)) for bl in b_last], axis=0)))

    def finish_rows(rs):
        r = proj_scr[rs, R0:N_MAIN]
        silu_r = r * (1.0 / (1.0 + jnp.exp(-r)))
        gated = []
        for h in range(GLA_HEADS):
            vs = slice(h * GLA_DV, (h + 1) * GLA_DV)
            oh = o_scr[rs, vs]
            ms = jnp.mean(oh * oh, axis=-1, keepdims=True)
            gated.append(oh * lax.rsqrt(ms + RMS_EPS) * gnorm_ref[...] * silu_r[:, vs])
        mix_in = jnp.concatenate([pool_out[rs]] + gated, axis=1).astype(BF16)
        resid = ALPHA * x[rs] + _dot(mix_in, w_out_ref[...])
        h_val = _layer_norm(resid, ln1g_ref[...], ln1b_ref[...])
        h_ref[rs, :] = h_val
        hb = h_val.astype(BF16)
        hb_ref[rs, :] = _pack_bf16_pairs(hb)
        logits_ref[:, rs] = _dot_nt(w_router_ref[...], hb) + b_router_ref[:, 0:1]

    st = None if per_chunk_state else jnp.where(t == 0, 0.0, st_scr[...])
    for c in range(CHUNKS_PER_TILE):
        rows = slice(c * CHUNK, (c + 1) * CHUNK)
        pair = slice((c // 2) * pair_rows, (c // 2 + 1) * pair_rows)
        if per_chunk_state:
            st = s_in_ref[c].reshape(hk, GLA_DV)
        qt = qt_all[rows]
        k_stack = jnp.where(same_head_k, jnp.concatenate([kt_all[rows]] * GLA_HEADS, axis=0), 0.0)
        v_stack = jnp.where(same_head_v, jnp.concatenate([proj_scr[rows, V0:R0]] * GLA_HEADS, axis=0), 0.0)
        s_stack = jnp.where(same_head_v, jnp.concatenate([st] * GLA_HEADS, axis=1), 0.0)
        att = jnp.where(causal, _dot_nt(qt, k_stack.astype(BF16)), 0.0)
        o_scr[rows, :] = (_dot(att.astype(BF16), v_stack.astype(BF16))
                          + _dot(qt, s_stack.astype(BF16)))
        v_chunk = jnp.where(pair_half == c % 2, proj_scr[pair, V0:R0], 0.0).astype(BF16)
        kv = _dot(kl_t[:, pair], v_chunk)
        upd = jnp.zeros((hk, GLA_DV), F32)
        for h in range(GLA_HEADS):
            upd = jnp.where(row_head == h, kv[:, h * GLA_DV:(h + 1) * GLA_DV], upd)
        st = st * decay_t[:, c * decay_cols:c * decay_cols + 1] + upd
        if per_chunk_state:
            s_out_ref[c] = st.reshape(GLA_HEADS, GLA_DK, GLA_DV)
        if (c + 1) % OUT_CHUNKS == 0:
            finish_rows(slice((c + 1 - OUT_CHUNKS) * CHUNK, (c + 1) * CHUNK))

    if not per_chunk_state:
        st_scr[...] = st
        s_out_ref[...] = st.reshape(GLA_HEADS, GLA_DK, GLA_DV)


def _const_spec(shape):
    nd = len(shape)
    return pl.BlockSpec(shape, lambda *_: (0,) * nd)


def _mixer_weight_specs():
    return [
        _const_spec((D_MODEL, N_MAIN)),
        _const_spec((D_MODEL, LANES)),
        _const_spec((LANES, GLA_HEADS * GLA_DK)),
        _const_spec((1, GLA_HEADS * GLA_DK)),
        _const_spec((len(POOL_WINDOWS), POOL_GROUP, POOL_GROUP)),
        _const_spec((1, POOL_WIDTH)),
        _const_spec((1, GLA_DV)),
        _const_spec((D_MODEL, D_MODEL)),
        _const_spec((1, D_MODEL)),
        _const_spec((1, D_MODEL)),
        _const_spec((NUM_EXPERTS, D_MODEL)),
        _const_spec((NUM_EXPERTS, LANES)),
    ]


def _mixer_weights(w_in, w_pool, pool_scale, w_gate_up, b_gate, gla_norm_w, w_out, ln1_g, ln1_b,
                   w_router, b_router):
    w_glr = jnp.zeros((D_MODEL, LANES), BF16).at[:, :GATE_RANK].set(w_in[0, :, N_MAIN:].astype(BF16))
    w_gate = jnp.zeros((LANES, GLA_HEADS * GLA_DK), BF16).at[:GATE_RANK].set(w_gate_up[0].astype(BF16))
    weights = (
        w_in[0, :, :N_MAIN].astype(BF16), w_glr, w_gate, b_gate[0][None, :],
        w_pool[0].astype(BF16), pool_scale[0][None, :], gla_norm_w[0][None, :],
        w_out[0].astype(BF16), ln1_g[0][None, :], ln1_b[0][None, :],
        w_router[0].T.astype(BF16), jnp.broadcast_to(b_router[0][:, None], (NUM_EXPERTS, LANES)),
    )
    assert len(weights) == N_MIXER_WEIGHTS
    return weights


def _mixer_scratch(per_chunk_state):
    ext_rows = (CHUNKS_PER_TILE * (CHUNK + HIST_PAD)) if per_chunk_state else (TILE_TOKENS + HIST_PAD)
    return [
        pltpu.VMEM((TILE_TOKENS, N_MAIN), F32),
        pltpu.VMEM((TILE_TOKENS, GLA_HEADS * GLA_DK), F32),
        pltpu.VMEM((TILE_TOKENS, GLA_HEADS * GLA_DV), F32),
        pltpu.VMEM((ext_rows, POOL_WIDTH), F32),
        pltpu.VMEM((GLA_HEADS * GLA_DK, GLA_DV), F32),
        pltpu.VMEM((TILE_TOKENS, TILE_TOKENS), BF16),
    ]


def _mixer_out_shapes(n_total, bsz):
    return (
        jax.ShapeDtypeStruct((n_total, D_MODEL), F32),
        jax.ShapeDtypeStruct((n_total, HALF), jnp.int32),
        jax.ShapeDtypeStruct((NUM_EXPERTS, n_total), F32),
        jax.ShapeDtypeStruct((bsz, POOL_HIST, POOL_WIDTH), F32),
        jax.ShapeDtypeStruct((bsz, GLA_HEADS, GLA_DK, GLA_DV), F32),
    )


def _mixer_prompt(x, weights, n_total):
    bsz, seq, _ = x.shape
    tiles = seq // TILE_TOKENS
    return pl.pallas_call(
        functools.partial(_mixer_kernel, False, 0),
        grid=(bsz, tiles),
        in_specs=[pl.BlockSpec((None, TILE_TOKENS, D_MODEL), lambda b, t: (b, t, 0))] + _mixer_weight_specs(),
        out_specs=(
            pl.BlockSpec((TILE_TOKENS, D_MODEL), lambda b, t: (b * tiles + t, 0)),
            pl.BlockSpec((TILE_TOKENS, HALF), lambda b, t: (b * tiles + t, 0)),
            pl.BlockSpec((NUM_EXPERTS, TILE_TOKENS), lambda b, t: (0, b * tiles + t)),
            pl.BlockSpec((None, POOL_HIST, POOL_WIDTH), lambda b, t: (b, 0, 0)),
            pl.BlockSpec((None, GLA_HEADS, GLA_DK, GLA_DV), lambda b, t: (b, 0, 0, 0)),
        ),
        out_shape=_mixer_out_shapes(n_total, bsz),
        scratch_shapes=_mixer_scratch(False),
        compiler_params=pltpu.CompilerParams(
            dimension_semantics=("arbitrary", "arbitrary"), vmem_limit_bytes=VMEM_LIMIT),
        name="mixer_prompt",
    )(x, *weights)


def _mixer_sample(x, hist, state, weights, shared, row_offset):
    bsz = x.shape[0]
    tiles = bsz // CHUNKS_PER_TILE
    tile0 = row_offset // TILE_TOKENS
    n_total = shared[0].shape[0]
    n_in = 3 + N_MIXER_WEIGHTS
    return pl.pallas_call(
        functools.partial(_mixer_kernel, True, PAST_LEN),
        grid=(tiles,),
        in_specs=[
            pl.BlockSpec((CHUNKS_PER_TILE, CHUNK, D_MODEL), lambda i: (i, 0, 0)),
            pl.BlockSpec((CHUNKS_PER_TILE, POOL_HIST, POOL_WIDTH), lambda i: (i, 0, 0)),
            pl.BlockSpec((CHUNKS_PER_TILE, GLA_HEADS, GLA_DK, GLA_DV), lambda i: (i, 0, 0, 0)),
        ] + _mixer_weight_specs() + [pl.BlockSpec(memory_space=pl.ANY)] * N_MIXER_SHARED_OUT,
        out_specs=(
            pl.BlockSpec((TILE_TOKENS, D_MODEL), lambda i: (tile0 + i, 0)),
            pl.BlockSpec((TILE_TOKENS, HALF), lambda i: (tile0 + i, 0)),
            pl.BlockSpec((NUM_EXPERTS, TILE_TOKENS), lambda i: (0, tile0 + i)),
            pl.BlockSpec((CHUNKS_PER_TILE, POOL_HIST, POOL_WIDTH), lambda i: (i, 0, 0)),
            pl.BlockSpec((CHUNKS_PER_TILE, GLA_HEADS, GLA_DK, GLA_DV), lambda i: (i, 0, 0, 0)),
        ),
        out_shape=_mixer_out_shapes(n_total, bsz),
        input_output_aliases={n_in + j: j for j in range(N_MIXER_SHARED_OUT)},
        scratch_shapes=_mixer_scratch(True),
        compiler_params=pltpu.CompilerParams(
            dimension_semantics=("arbitrary",), vmem_limit_bytes=VMEM_LIMIT),
        name="mixer_sample",
    )(x, hist, state, *weights, *shared)


def _router_kernel(lt_ref, gates_ref, dest_ref, padend_ref, cnt_scr, base_scr, pstart_scr, before_scr):
    phase = pl.program_id(0)
    i = pl.program_id(1)
    shape = (NUM_EXPERTS, ROUTE_TOKENS)
    logits = lt_ref[...]
    row = lax.broadcasted_iota(jnp.int32, shape, 0)
    sel, vals = [], []
    for _ in range(TOP_K):
        m = jnp.max(logits, axis=0, keepdims=True)
        idx = jnp.min(jnp.where(logits == m, row, NUM_EXPERTS), axis=0, keepdims=True)
        hit = row == idx
        sel.append(hit)
        vals.append(m)
        logits = jnp.where(hit, -jnp.inf, logits)
    chosen = sum(jnp.where(hit, 1.0, 0.0) for hit in sel)
    tile_counts = jnp.broadcast_to(jnp.sum(chosen, axis=1, keepdims=True), (NUM_EXPERTS, LANES))

    @pl.when(phase == 0)
    def _():
        @pl.when(i == 0)
        def _():
            cnt_scr[...] = jnp.zeros_like(cnt_scr)

        cnt_scr[...] += tile_counts

    @pl.when(phase == 1)
    def _():
        @pl.when(i == 0)
        def _():
            blocks = jnp.floor((cnt_scr[...] + (MOE_ROWS - 1)) * (1.0 / MOE_ROWS))
            erow = lax.broadcasted_iota(jnp.int32, (NUM_EXPERTS, LANES), 0)
            cum = blocks
            shift = 1
            while shift < NUM_EXPERTS:
                cum = cum + jnp.where(erow >= shift, pltpu.roll(cum, shift, 0), 0.0)
                shift *= 2
            padend_ref[...] = cum * MOE_ROWS
            pstart_scr[...] = (cum - blocks) * MOE_ROWS
            base_scr[...] = jnp.zeros_like(base_scr)
            ti = lax.broadcasted_iota(jnp.int32, (ROUTE_TOKENS, ROUTE_TOKENS), 0)
            tj = lax.broadcasted_iota(jnp.int32, (ROUTE_TOKENS, ROUTE_TOKENS), 1)
            before_scr[...] = jnp.where(ti < tj, 1.0, 0.0).astype(BF16)

        earlier = _dot(chosen.astype(BF16), before_scr[...])
        pos = pstart_scr[:, 0:1] + base_scr[:, 0:1] + earlier
        dest = [jnp.sum(jnp.where(hit, pos, 0.0), axis=0, keepdims=True) for hit in sel]
        dest_ref[...] = jnp.concatenate(dest, axis=0).astype(jnp.int32)
        ex = [jnp.exp(v - vals[0]) for v in vals]
        denom = ex[0] + ex[1] + ex[2] + ex[3]
        gates_ref[...] = jnp.concatenate([e / denom for e in ex], axis=0)
        base_scr[...] += tile_counts


def _router(logits_t):
    n = logits_t.shape[1]
    assert n % ROUTE_TOKENS == 0
    tiles = n // ROUTE_TOKENS
    return pl.pallas_call(
        _router_kernel,
        grid=(2, tiles),
        in_specs=[pl.BlockSpec((NUM_EXPERTS, ROUTE_TOKENS), lambda p, i: (0, i))],
        out_specs=(
            pl.BlockSpec((TOP_K, ROUTE_TOKENS), lambda p, i: (0, i * p)),
            pl.BlockSpec((TOP_K, ROUTE_TOKENS), lambda p, i: (0, i * p)),
            pl.BlockSpec((NUM_EXPERTS, LANES), lambda p, i: (0, 0)),
        ),
        out_shape=(
            jax.ShapeDtypeStruct((TOP_K, n), F32),
            jax.ShapeDtypeStruct((TOP_K, n), jnp.int32),
            jax.ShapeDtypeStruct((NUM_EXPERTS, LANES), F32),
        ),
        scratch_shapes=[pltpu.VMEM((NUM_EXPERTS, LANES), F32)] * 3
        + [pltpu.VMEM((ROUTE_TOKENS, ROUTE_TOKENS), BF16)],
        compiler_params=pltpu.CompilerParams(
            dimension_semantics=("arbitrary", "arbitrary"), vmem_limit_bytes=VMEM_LIMIT),
        name="router",
    )(logits_t)


SC_CORES = 2
SC_SUBCORES = 16
SC_WORKERS = SC_CORES * SC_SUBCORES
DISPATCH_ROWS = 64


def _dispatch(h_packed, dest_chunks, m_pad):
    n = h_packed.shape[0]
    n_chunks = n // DISPATCH_ROWS
    assert n_chunks % SC_WORKERS == 0
    per_worker = n_chunks // SC_WORKERS
    mesh = plsc.VectorSubcoreMesh(core_axis_name="c", subcore_axis_name="s")

    @functools.partial(
        pl.kernel, mesh=mesh,
        out_type=jax.ShapeDtypeStruct((m_pad, HALF), jnp.int32),
        scratch_types=[
            pltpu.VMEM((2, TOP_K, DISPATCH_ROWS), jnp.int32),
            pltpu.VMEM((2, DISPATCH_ROWS, HALF), jnp.int32),
            pltpu.SemaphoreType.DMA((2,)),
            pltpu.SemaphoreType.DMA((2,)),
        ],
        compiler_params=pltpu.CompilerParams(use_tc_tiling_on_sc=True),
        name="dispatch",
    )
    def dispatch_kernel(h_hbm, dest_hbm, out_hbm, idx_v, rows_v, load_sems, scatter_sems):
        wid = lax.axis_index("s") * SC_CORES + lax.axis_index("c")

        def loads(j):
            chunk = wid * per_worker + j
            slot = j % 2
            return (
                pltpu.make_async_copy(dest_hbm.at[chunk], idx_v.at[slot], load_sems.at[slot]),
                pltpu.make_async_copy(h_hbm.at[pl.ds(chunk * DISPATCH_ROWS, DISPATCH_ROWS)],
                                      rows_v.at[slot], load_sems.at[slot]),
            )

        def scatters(j):
            slot = j % 2
            return [pltpu.make_async_copy(rows_v.at[slot], out_hbm.at[idx_v.at[slot, k]],
                                          scatter_sems.at[slot]) for k in range(TOP_K)]

        for cp in loads(0):
            cp.start()
        for j in range(per_worker):
            for cp in loads(j):
                cp.wait()
            if j >= 1:
                for cp in scatters(j - 1):
                    cp.wait()
            if j + 1 < per_worker:
                for cp in loads(j + 1):
                    cp.start()
            for cp in scatters(j):
                cp.start()
        for cp in scatters(per_worker - 1):
            cp.wait()

    return dispatch_kernel(h_packed, dest_chunks)


def _gather_expert_rows(y_sorted, dest_chunks, row_offset, n):
    n_chunks = n // DISPATCH_ROWS
    assert n_chunks % SC_WORKERS == 0 and row_offset % DISPATCH_ROWS == 0
    per_worker = n_chunks // SC_WORKERS
    chunk0 = row_offset // DISPATCH_ROWS
    mesh = plsc.VectorSubcoreMesh(core_axis_name="c", subcore_axis_name="s")

    @functools.partial(
        pl.kernel, mesh=mesh,
        out_type=jax.ShapeDtypeStruct((TOP_K, n, HALF), jnp.int32),
        scratch_types=[
            pltpu.VMEM((TOP_K, DISPATCH_ROWS), jnp.int32),
            pltpu.VMEM((2, DISPATCH_ROWS, HALF), jnp.int32),
            pltpu.SemaphoreType.DMA((2,)),
        ],
        compiler_params=pltpu.CompilerParams(use_tc_tiling_on_sc=True),
        name="gather_expert_rows",
    )
    def gather_kernel(y_hbm, dest_hbm, out_hbm, idx_v, rows_v, sems):
        wid = lax.axis_index("s") * SC_CORES + lax.axis_index("c")

        def gather(k):
            return pltpu.make_async_copy(y_hbm.at[idx_v.at[k]], rows_v.at[k % 2], sems.at[k % 2])

        @pl.loop(0, per_worker)
        def _(j):
            local = wid * per_worker + j
            pltpu.sync_copy(dest_hbm.at[chunk0 + local], idx_v)
            gather(0).start()
            for k in range(TOP_K):
                if k + 1 < TOP_K:
                    gather(k + 1).start()
                gather(k).wait()
                pltpu.sync_copy(rows_v.at[k % 2],
                                out_hbm.at[k, pl.ds(local * DISPATCH_ROWS, DISPATCH_ROWS)])

    return gather_kernel(y_sorted, dest_chunks)


def _moe_kernel(be_ref, nused_ref, x_ref, wgu_ref, bgu_ref, wd_ref, bd_ref, y_ref, wgu_bf, wd_bf):
    i = pl.program_id(0)

    @pl.when(i < nused_ref[0])
    def _():
        @pl.when((i == 0) | (be_ref[i] != be_ref[jnp.maximum(i - 1, 0)]))
        def _():
            wgu_bf[...] = wgu_ref[...].astype(BF16)
            wd_bf[...] = wd_ref[...].astype(BF16)

        gu = _dot(_unpack_bf16_pairs(x_ref[...]), wgu_bf[...]) + bgu_ref[...]
        gate = jnp.minimum(gu[:, :EXPERT_FF], SWIGLU_LIMIT)
        up = jnp.clip(gu[:, EXPERT_FF:], -SWIGLU_LIMIT, SWIGLU_LIMIT)
        hmid = gate * (1.0 / (1.0 + jnp.exp(-SWIGLU_ALPHA * gate))) * (up + 1.0)
        y = _dot(hmid.astype(BF16), wd_bf[...]) + bd_ref[...]
        y_ref[...] = _pack_bf16_pairs(y.astype(BF16))


def _moe_experts(block_expert, n_used, x_sorted, w_gu, b_gu, w_down, b_down):
    m_pad = x_sorted.shape[0]
    n_blocks = m_pad // MOE_ROWS

    def blk(i, be, nu):
        return jnp.minimum(i, nu[0] - 1)

    def expert(i, be, nu):
        return be[blk(i, be, nu)]

    grid_spec = pltpu.PrefetchScalarGridSpec(
        num_scalar_prefetch=2,
        grid=(n_blocks,),
        in_specs=[
            pl.BlockSpec((MOE_ROWS, HALF), lambda i, be, nu: (blk(i, be, nu), 0)),
            pl.BlockSpec((None, D_MODEL, 2 * EXPERT_FF), lambda i, be, nu: (expert(i, be, nu), 0, 0)),
            pl.BlockSpec((None, 1, 2 * EXPERT_FF), lambda i, be, nu: (expert(i, be, nu), 0, 0)),
            pl.BlockSpec((None, EXPERT_FF, D_MODEL), lambda i, be, nu: (expert(i, be, nu), 0, 0)),
            pl.BlockSpec((None, 1, D_MODEL), lambda i, be, nu: (expert(i, be, nu), 0, 0)),
        ],
        out_specs=pl.BlockSpec((MOE_ROWS, HALF), lambda i, be, nu: (blk(i, be, nu), 0)),
        scratch_shapes=[
            pltpu.VMEM((D_MODEL, 2 * EXPERT_FF), BF16),
            pltpu.VMEM((EXPERT_FF, D_MODEL), BF16),
        ],
    )
    return pl.pallas_call(
        _moe_kernel,
        grid_spec=grid_spec,
        out_shape=jax.ShapeDtypeStruct((m_pad, HALF), jnp.int32),
        compiler_params=pltpu.CompilerParams(
            dimension_semantics=("arbitrary",), vmem_limit_bytes=VMEM_LIMIT),
        name="moe_experts",
    )(block_expert, n_used, x_sorted, w_gu, b_gu, w_down, b_down)


def _combine_kernel(yk_ref, gates_ref, h_ref, g_ref, b_ref, out_ref):
    gates = gates_ref[...]
    lo = hi = None
    for k in range(TOP_K):
        u = lax.bitcast_convert_type(yk_ref[k], jnp.uint32)
        gk = gates[:, k:k + 1]
        lo_k = lax.bitcast_convert_type(u << 16, F32) * gk
        hi_k = lax.bitcast_convert_type(u & jnp.uint32(HI_MASK), F32) * gk
        lo = lo_k if lo is None else lo + lo_k
        hi = hi_k if hi is None else hi + hi_k
    acc = ALPHA * h_ref[...] + jnp.concatenate([lo, hi], axis=1)
    out_ref[...] = _layer_norm(acc, g_ref[...], b_ref[...])


def _combine_kernel_aliased(yk_ref, gates_ref, h_ref, g_ref, b_ref, prev_ref, out_ref):
    del prev_ref
    _combine_kernel(yk_ref, gates_ref, h_ref, g_ref, b_ref, out_ref)


def _combine(yk, gates, h_all, ln_g, ln_b, row_offset, out_rows, out_offset, out_prev):
    n_seg = yk.shape[1]
    tile0 = row_offset // COMBINE_TOKENS
    out_tile0 = out_offset // COMBINE_TOKENS
    in_specs = [
        pl.BlockSpec((TOP_K, COMBINE_TOKENS, HALF), lambda i: (0, i, 0)),
        pl.BlockSpec((COMBINE_TOKENS, TOP_K), lambda i: (tile0 + i, 0)),
        pl.BlockSpec((COMBINE_TOKENS, D_MODEL), lambda i: (tile0 + i, 0)),
        _const_spec((1, D_MODEL)),
        _const_spec((1, D_MODEL)),
    ]
    args = [yk, gates, h_all, ln_g, ln_b]
    aliases = {}
    kern = _combine_kernel
    if out_prev is not None:
        in_specs.append(pl.BlockSpec(memory_space=pl.ANY))
        aliases = {len(args): 0}
        args.append(out_prev)
        kern = _combine_kernel_aliased
    return pl.pallas_call(
        kern,
        grid=(n_seg // COMBINE_TOKENS,),
        in_specs=in_specs,
        out_specs=pl.BlockSpec((COMBINE_TOKENS, D_MODEL), lambda i: (out_tile0 + i, 0)),
        out_shape=jax.ShapeDtypeStruct((out_rows, D_MODEL), F32),
        input_output_aliases=aliases,
        compiler_params=pltpu.CompilerParams(
            dimension_semantics=("arbitrary",), vmem_limit_bytes=VMEM_LIMIT),
        name="moe_combine",
    )(*args)


def kernel(x_prompt, x_sample, state_pool, state_gla, w_in, w_pool, pool_scale, w_gate_up, b_gate,
           gla_norm_w, w_out, ln1_g, ln1_b, w_router, b_router, w_gu, b_gu, w_down, b_down,
           ln2_g, ln2_b):
    assert w_in.shape[0] == 1, "single-layer kernel"
    bp, seq, _ = x_prompt.shape
    bs, dec_seq, _ = x_sample.shape
    assert dec_seq == CHUNK and seq % TILE_TOKENS == 0 and bs % CHUNKS_PER_TILE == 0
    n_prompt = bp * seq
    n_sample = bs * dec_seq
    n_total = n_prompt + n_sample
    nk = n_total * TOP_K
    n_blocks = -(-nk // MOE_ROWS) + NUM_EXPERTS
    m_pad = n_blocks * MOE_ROWS

    weights = _mixer_weights(w_in, w_pool, pool_scale, w_gate_up, b_gate, gla_norm_w, w_out,
                             ln1_g, ln1_b, w_router, b_router)

    *shared, hist_p, s_p = _mixer_prompt(x_prompt, weights, n_total)
    h_all, hb_all, logits_t, hist_s, s_s = _mixer_sample(
        x_sample, state_pool[0], state_gla[0], weights, shared, n_prompt)

    gates_t, dest_t, pad_end = _router(logits_t)
    pad_end = pad_end[:, 0].astype(jnp.int32)
    block_start = jnp.arange(n_blocks, dtype=jnp.int32) * MOE_ROWS
    block_expert = jnp.minimum(jnp.sum((block_start[:, None] >= pad_end[None, :]).astype(jnp.int32), axis=1),
                               NUM_EXPERTS - 1)
    n_used = (pad_end[-1:] // MOE_ROWS).astype(jnp.int32)

    dest_chunks = dest_t.reshape(TOP_K, n_total // DISPATCH_ROWS, DISPATCH_ROWS).transpose(1, 0, 2)
    x_sorted = _dispatch(hb_all, dest_chunks, m_pad)
    y_sorted = _moe_experts(block_expert, n_used, x_sorted, w_gu[0], b_gu[0][:, None, :],
                            w_down[0], b_down[0][:, None, :])
    gates = gates_t.T
    ln_g, ln_b = ln2_g[0][None, :], ln2_b[0][None, :]
    seg = n_prompt // COMBINE_SEGMENTS
    y_prompt = None
    for s in range(COMBINE_SEGMENTS):
        yk = _gather_expert_rows(y_sorted, dest_chunks, s * seg, seg)
        y_prompt = _combine(yk, gates, h_all, ln_g, ln_b, s * seg, n_prompt, s * seg, y_prompt)
    yk = _gather_expert_rows(y_sorted, dest_chunks, n_prompt, n_sample)
    y_sample = _combine(yk, gates, h_all, ln_g, ln_b, n_prompt, n_sample, 0, None)
    y_prompt = y_prompt.reshape(bp, seq, D_MODEL)
    y_sample = y_sample.reshape(bs, dec_seq, D_MODEL)
    return (y_prompt, y_sample, hist_p[None], s_p[None], hist_s[None], s_s[None])
```

```python
import functools

import jax
import jax.numpy as jnp
from jax import lax
from jax.experimental import pallas as pl
from jax.experimental.pallas import tpu as pltpu
from jax.experimental.pallas import tpu_sc as plsc

F32 = jnp.float32
BF16 = jnp.bfloat16

D_MODEL = 1024
CHUNK = 64
PAST_LEN = 1024
POOL_WIDTH = 512
POOL_WINDOWS = (2, 4, 8, 16)
POOL_GROUP = 128
POOL_HIST = 15
GLA_HEADS = 4
GLA_DK = 64
GLA_DV = 128
GATE_RANK = 16
GATE_NORMALIZER = 16.0
NUM_EXPERTS = 32
TOP_K = 4
EXPERT_FF = 1024
SWIGLU_LIMIT = 7.0
SWIGLU_ALPHA = 1.702
LN_EPS = 1e-5
RMS_EPS = 1e-6
ALPHA = 2.0 ** 0.25

Q0 = POOL_WIDTH
K0 = Q0 + GLA_HEADS * GLA_DK
V0 = K0 + GLA_HEADS * GLA_DK
R0 = V0 + GLA_HEADS * GLA_DV
N_MAIN = R0 + GLA_HEADS * GLA_DV

LANES = 128
TILE_TOKENS = 512
CHUNKS_PER_TILE = TILE_TOKENS // CHUNK
HIST_PAD = 16
OUT_CHUNKS = 4
SCAN_ROWS = 256
MOE_ROWS = 512
ROUTE_TOKENS = 1024
COMBINE_TOKENS = 512
COMBINE_SEGMENTS = 4
VMEM_LIMIT = 56 * 1024 * 1024


def _dot(a, b):
    return jnp.dot(a, b, preferred_element_type=F32)


def _dot_nt(a, b):
    return lax.dot_general(a, b, (((1,), (1,)), ((), ())), preferred_element_type=F32)


def _dot_tn(a, b):
    return lax.dot_general(a, b, (((0,), (0,)), ((), ())), preferred_element_type=F32)


HALF = D_MODEL // 2
HI_MASK = 0xFFFF0000


def _pack_bf16_pairs(xb):
    lo = lax.bitcast_convert_type(xb[:, :HALF].astype(F32), jnp.uint32) >> 16
    hi = lax.bitcast_convert_type(xb[:, HALF:].astype(F32), jnp.uint32) & jnp.uint32(HI_MASK)
    return lax.bitcast_convert_type(hi | lo, jnp.int32)


def _unpack_bf16_pairs(p):
    u = lax.bitcast_convert_type(p, jnp.uint32)
    lo = lax.bitcast_convert_type(u << 16, F32)
    hi = lax.bitcast_convert_type(u & jnp.uint32(HI_MASK), F32)
    return jnp.concatenate([lo, hi], axis=1).astype(BF16)


def _layer_norm(v, g, b):
    mu = jnp.mean(v, axis=-1, keepdims=True)
    c = v - mu
    var = jnp.mean(c * c, axis=-1, keepdims=True)
    return c * lax.rsqrt(var + LN_EPS) * g + b


N_MIXER_WEIGHTS = 12
N_MIXER_SHARED_OUT = 3


def _mixer_kernel(per_chunk_state, pos0, *refs):
    if per_chunk_state:
        (x_ref, hist_in_ref, s_in_ref, *rest) = refs
    else:
        (x_ref, *rest) = refs
        hist_in_ref = s_in_ref = None
    (w_main_ref, w_glr_ref, w_gate_ref, b_gate_ref, w_pool_ref, pscale_ref, gnorm_ref,
     w_out_ref, ln1g_ref, ln1b_ref, w_router_ref, b_router_ref, *rest) = rest
    if per_chunk_state:
        rest = rest[N_MIXER_SHARED_OUT:]
    (h_ref, hb_ref, logits_ref, hist_out_ref, s_out_ref,
     proj_scr, b_scr, o_scr, ext_scr, st_scr, tri_scr) = rest

    if per_chunk_state:
        t = None
        first_step = pl.program_id(0) == 0
    else:
        t = pl.program_id(1)
        first_step = (pl.program_id(0) == 0) & (t == 0)
    x = x_ref[...].reshape(TILE_TOKENS, D_MODEL)
    xb = x.astype(BF16)

    @pl.when(first_step)
    def _():
        ti = lax.broadcasted_iota(jnp.int32, (SCAN_ROWS, SCAN_ROWS), 0)
        tj = lax.broadcasted_iota(jnp.int32, (SCAN_ROWS, SCAN_ROWS), 1)
        same_chunk = (ti // CHUNK) == (tj // CHUNK)
        tri_scr[...] = jnp.where(same_chunk & (ti >= tj), 1.0, 0.0).astype(BF16)
        if not per_chunk_state:
            st_scr[...] = jnp.zeros_like(st_scr)
            ext_scr[0:HIST_PAD, :] = jnp.zeros((HIST_PAD, POOL_WIDTH), F32)

    glr = _dot(xb, w_glr_ref[...])
    proj_scr[:, 0:V0] = _dot(xb, w_main_ref[:, 0:V0])
    gk = _dot(glr.astype(BF16), w_gate_ref[...]) + b_gate_ref[...]
    log_sig = jnp.minimum(gk, 0.0) - jnp.log1p(jnp.exp(-jnp.abs(gk)))
    g = log_sig / GATE_NORMALIZER
    g_hi = g.astype(BF16)
    g_lo = (g - g_hi.astype(F32)).astype(BF16)
    proj_scr[:, V0:N_MAIN] = _dot(xb, w_main_ref[:, V0:N_MAIN])
    for s in range(TILE_TOKENS // SCAN_ROWS):
        rs = slice(s * SCAN_ROWS, (s + 1) * SCAN_ROWS)
        b_scr[rs, :] = _dot(tri_scr[...], g_hi[rs]) + _dot(tri_scr[...], g_lo[rs])

    if per_chunk_state:
        seg_len, seg_stride, n_seg = CHUNK, CHUNK + HIST_PAD, CHUNKS_PER_TILE
        for c in range(n_seg):
            base = c * seg_stride
            ext_scr[base:base + HIST_PAD, :] = jnp.zeros((HIST_PAD, POOL_WIDTH), F32)
            ext_scr[base + 1:base + HIST_PAD, :] = hist_in_ref[c]
            ext_scr[base + HIST_PAD:base + seg_stride, :] = proj_scr[c * CHUNK:(c + 1) * CHUNK, 0:POOL_WIDTH]
        row_pos = pos0 + lax.broadcasted_iota(jnp.int32, (seg_len, POOL_GROUP), 0)
    else:
        seg_len, seg_stride, n_seg = TILE_TOKENS, TILE_TOKENS + HIST_PAD, 1
        ext_scr[0:HIST_PAD, :] = jnp.where(t == 0, 0.0, ext_scr[0:HIST_PAD, :])
        ext_scr[HIST_PAD:seg_stride, :] = proj_scr[:, 0:POOL_WIDTH]
        row_pos = pos0 + t * TILE_TOKENS + lax.broadcasted_iota(jnp.int32, (seg_len, POOL_GROUP), 0)

    pooled_groups = []
    for gi, w in enumerate(POOL_WINDOWS):
        gs = slice(gi * POOL_GROUP, (gi + 1) * POOL_GROUP)
        cnt = jnp.minimum(row_pos + 1, w).astype(F32)
        ext = ext_scr[:, gs]
        win = ext
        shift = 1
        while shift < w:
            win = win + pltpu.roll(win, shift, 0)
            shift *= 2
        segs = []
        for s in range(n_seg):
            base = s * seg_stride + HIST_PAD
            segs.append(win[base:base + seg_len] / cnt - ext[base:base + seg_len])
        pooled = segs[0] if n_seg == 1 else jnp.concatenate(segs, axis=0)
        pooled_groups.append(pooled.astype(BF16))
    pool_cols = []
    for p in range(len(POOL_WINDOWS) // 2):
        both = jnp.concatenate(pooled_groups[2 * p:2 * p + 2], axis=1)
        pool_cols.append(_dot(both, w_pool_ref[p]))
    pool_out = jnp.concatenate(pool_cols, axis=1) * pscale_ref[...]

    if per_chunk_state:
        for c in range(n_seg):
            end = (c + 1) * seg_stride
            hist_out_ref[c] = ext_scr[end - POOL_HIST:end, :]
    else:
        hist_out_ref[...] = ext_scr[seg_stride - POOL_HIST:seg_stride, :]
        ext_scr[0:HIST_PAD, :] = ext_scr[TILE_TOKENS:seg_stride, :]

    hk = GLA_HEADS * GLA_DK
    hv = GLA_HEADS * GLA_DV
    pair_rows = 2 * CHUNK
    decay_cols = LANES // CHUNKS_PER_TILE

    def head_of(shape, dim, width):
        return lax.broadcasted_iota(jnp.int32, shape, dim) // width

    same_head_k = head_of((hk, hk), 0, CHUNK) == head_of((hk, hk), 1, GLA_DK)
    same_head_v = head_of((hk, hv), 0, CHUNK) == head_of((hk, hv), 1, GLA_DV)
    pair_half = head_of((pair_rows, hv), 0, CHUNK)
    causal = (lax.broadcasted_iota(jnp.int32, (CHUNK, hk), 0)
              >= lax.broadcasted_iota(jnp.int32, (CHUNK, hk), 1) % CHUNK)

    b_all = b_scr[...]
    b_last = [b_scr[(c + 1) * CHUNK - 1:(c + 1) * CHUNK, :] for c in range(CHUNKS_PER_TILE)]
    b_last_rows = jnp.concatenate([jnp.broadcast_to(bl, (CHUNK, hk)) for bl in b_last], axis=0)
    k_all = proj_scr[:, K0:V0]
    qt_all = (proj_scr[:, Q0:K0] * (GLA_DK ** -0.5) * jnp.exp(b_all)).astype(BF16)
    kt_all = k_all * jnp.exp(-b_all)
    kl_t = jnp.transpose(k_all * jnp.exp(b_last_rows - b_all)).astype(BF16)
    decay_t = jnp.transpose(jnp.exp(jnp.concatenate(
        [jnp.broadcast_to(bl, (decay_cols, hk)) for bl in b_last], axis=0)))

    def finish_rows(rs):
        r = proj_scr[rs, R0:N_MAIN]
        silu_r = r * (1.0 / (1.0 + jnp.exp(-r)))
        gated = []
        for h in range(GLA_HEADS):
            vs = slice(h * GLA_DV, (h + 1) * GLA_DV)
            oh = o_scr[rs, vs]
            ms = jnp.mean(oh * oh, axis=-1, keepdims=True)
            gated.append(oh * lax.rsqrt(ms + RMS_EPS) * gnorm_ref[...] * silu_r[:, vs])
        mix_in = jnp.concatenate([pool_out[rs]] + gated, axis=1).astype(BF16)
        resid = ALPHA * x[rs] + _dot(mix_in, w_out_ref[...])
        h_val = _layer_norm(resid, ln1g_ref[...], ln1b_ref[...])
        h_ref[rs, :] = h_val
        hb = h_val.astype(BF16)
        hb_ref[rs, :] = _pack_bf16_pairs(hb)
        logits_ref[:, rs] = _dot_nt(w_router_ref[...], hb) + b_router_ref[:, 0:1]

    st = None if per_chunk_state else jnp.where(t == 0, 0.0, st_scr[...])
    for c in range(CHUNKS_PER_TILE):
        rows = slice(c * CHUNK, (c + 1) * CHUNK)
        pair = slice((c // 2) * pair_rows, (c // 2 + 1) * pair_rows)
        if per_chunk_state:
            st = s_in_ref[c].reshape(hk, GLA_DV)
        qt = qt_all[rows]
        k_stack = jnp.where(same_head_k, jnp.concatenate([kt_all[rows]] * GLA_HEADS, axis=0), 0.0)
        v_stack = jnp.where(same_head_v, jnp.concatenate([proj_scr[rows, V0:R0]] * GLA_HEADS, axis=0), 0.0)
        s_stack = jnp.where(same_head_v, jnp.concatenate([st] * GLA_HEADS, axis=1), 0.0)
        att = jnp.where(causal, _dot_nt(qt, k_stack.astype(BF16)), 0.0)
        o_scr[rows, :] = (_dot(att.astype(BF16), v_stack.astype(BF16))
                          + _dot(qt, s_stack.astype(BF16)))
        v_chunk = jnp.where(pair_half == c % 2, proj_scr[pair, V0:R0], 0.0).astype(BF16)
        upd = jnp.concatenate(
            [_dot(kl_t[h * GLA_DK:(h + 1) * GLA_DK, pair], v_chunk[:, h * GLA_DV:(h + 1) * GLA_DV])
             for h in range(GLA_HEADS)], axis=0)
        st = st * decay_t[:, c * decay_cols:c * decay_cols + 1] + upd
        if per_chunk_state:
            s_out_ref[c] = st.reshape(GLA_HEADS, GLA_DK, GLA_DV)
        if (c + 1) % OUT_CHUNKS == 0:
            finish_rows(slice((c + 1 - OUT_CHUNKS) * CHUNK, (c + 1) * CHUNK))

    if not per_chunk_state:
        st_scr[...] = st
        s_out_ref[...] = st.reshape(GLA_HEADS, GLA_DK, GLA_DV)


def _const_spec(shape):
    nd = len(shape)
    return pl.BlockSpec(shape, lambda *_: (0,) * nd)


def _mixer_weight_specs():
    return [
        _const_spec((D_MODEL, N_MAIN)),
        _const_spec((D_MODEL, LANES)),
        _const_spec((LANES, GLA_HEADS * GLA_DK)),
        _const_spec((1, GLA_HEADS * GLA_DK)),
        _const_spec((len(POOL_WINDOWS) // 2, 2 * POOL_GROUP, 2 * POOL_GROUP)),
        _const_spec((1, POOL_WIDTH)),
        _const_spec((1, GLA_DV)),
        _const_spec((D_MODEL, D_MODEL)),
        _const_spec((1, D_MODEL)),
        _const_spec((1, D_MODEL)),
        _const_spec((NUM_EXPERTS, D_MODEL)),
        _const_spec((NUM_EXPERTS, LANES)),
    ]


def _mixer_weights(w_in, w_pool, pool_scale, w_gate_up, b_gate, gla_norm_w, w_out, ln1_g, ln1_b,
                   w_router, b_router):
    w_glr = jnp.zeros((D_MODEL, LANES), BF16).at[:, :GATE_RANK].set(w_in[0, :, N_MAIN:].astype(BF16))
    w_gate = jnp.zeros((LANES, GLA_HEADS * GLA_DK), BF16).at[:GATE_RANK].set(w_gate_up[0].astype(BF16))
    wp = w_pool[0].astype(BF16)
    zero = jnp.zeros((POOL_GROUP, POOL_GROUP), BF16)
    w_pool_pairs = jnp.stack([jnp.block([[wp[2 * p], zero], [zero, wp[2 * p + 1]]])
                              for p in range(len(POOL_WINDOWS) // 2)])
    weights = (
        w_in[0, :, :N_MAIN].astype(BF16), w_glr, w_gate, b_gate[0][None, :],
        w_pool_pairs, pool_scale[0][None, :], gla_norm_w[0][None, :],
        w_out[0].astype(BF16), ln1_g[0][None, :], ln1_b[0][None, :],
        w_router[0].T.astype(BF16), jnp.broadcast_to(b_router[0][:, None], (NUM_EXPERTS, LANES)),
    )
    assert len(weights) == N_MIXER_WEIGHTS
    return weights


def _mixer_scratch(per_chunk_state):
    ext_rows = (CHUNKS_PER_TILE * (CHUNK + HIST_PAD)) if per_chunk_state else (TILE_TOKENS + HIST_PAD)
    return [
        pltpu.VMEM((TILE_TOKENS, N_MAIN), F32),
        pltpu.VMEM((TILE_TOKENS, GLA_HEADS * GLA_DK), F32),
        pltpu.VMEM((TILE_TOKENS, GLA_HEADS * GLA_DV), F32),
        pltpu.VMEM((ext_rows, POOL_WIDTH), F32),
        pltpu.VMEM((GLA_HEADS * GLA_DK, GLA_DV), F32),
        pltpu.VMEM((SCAN_ROWS, SCAN_ROWS), BF16),
    ]


def _mixer_out_shapes(n_total, bsz):
    return (
        jax.ShapeDtypeStruct((n_total, D_MODEL), F32),
        jax.ShapeDtypeStruct((n_total, HALF), jnp.int32),
        jax.ShapeDtypeStruct((NUM_EXPERTS, n_total), F32),
        jax.ShapeDtypeStruct((bsz, POOL_HIST, POOL_WIDTH), F32),
        jax.ShapeDtypeStruct((bsz, GLA_HEADS, GLA_DK, GLA_DV), F32),
    )


def _mixer_prompt(x, weights, n_total):
    bsz, seq, _ = x.shape
    tiles = seq // TILE_TOKENS
    return pl.pallas_call(
        functools.partial(_mixer_kernel, False, 0),
        grid=(bsz, tiles),
        in_specs=[pl.BlockSpec((None, TILE_TOKENS, D_MODEL), lambda b, t: (b, t, 0))] + _mixer_weight_specs(),
        out_specs=(
            pl.BlockSpec((TILE_TOKENS, D_MODEL), lambda b, t: (b * tiles + t, 0)),
            pl.BlockSpec((TILE_TOKENS, HALF), lambda b, t: (b * tiles + t, 0)),
            pl.BlockSpec((NUM_EXPERTS, TILE_TOKENS), lambda b, t: (0, b * tiles + t)),
            pl.BlockSpec((None, POOL_HIST, POOL_WIDTH), lambda b, t: (b, 0, 0)),
            pl.BlockSpec((None, GLA_HEADS, GLA_DK, GLA_DV), lambda b, t: (b, 0, 0, 0)),
        ),
        out_shape=_mixer_out_shapes(n_total, bsz),
        scratch_shapes=_mixer_scratch(False),
        compiler_params=pltpu.CompilerParams(
            dimension_semantics=("arbitrary", "arbitrary"), vmem_limit_bytes=VMEM_LIMIT),
        name="mixer_prompt",
    )(x, *weights)


def _mixer_sample(x, hist, state, weights, shared, row_offset):
    bsz = x.shape[0]
    tiles = bsz // CHUNKS_PER_TILE
    tile0 = row_offset // TILE_TOKENS
    n_total = shared[0].shape[0]
    n_in = 3 + N_MIXER_WEIGHTS
    return pl.pallas_call(
        functools.partial(_mixer_kernel, True, PAST_LEN),
        grid=(tiles,),
        in_specs=[
            pl.BlockSpec((CHUNKS_PER_TILE, CHUNK, D_MODEL), lambda i: (i, 0, 0)),
            pl.BlockSpec((CHUNKS_PER_TILE, POOL_HIST, POOL_WIDTH), lambda i: (i, 0, 0)),
            pl.BlockSpec((CHUNKS_PER_TILE, GLA_HEADS, GLA_DK, GLA_DV), lambda i: (i, 0, 0, 0)),
        ] + _mixer_weight_specs() + [pl.BlockSpec(memory_space=pl.ANY)] * N_MIXER_SHARED_OUT,
        out_specs=(
            pl.BlockSpec((TILE_TOKENS, D_MODEL), lambda i: (tile0 + i, 0)),
            pl.BlockSpec((TILE_TOKENS, HALF), lambda i: (tile0 + i, 0)),
            pl.BlockSpec((NUM_EXPERTS, TILE_TOKENS), lambda i: (0, tile0 + i)),
            pl.BlockSpec((CHUNKS_PER_TILE, POOL_HIST, POOL_WIDTH), lambda i: (i, 0, 0)),
            pl.BlockSpec((CHUNKS_PER_TILE, GLA_HEADS, GLA_DK, GLA_DV), lambda i: (i, 0, 0, 0)),
        ),
        out_shape=_mixer_out_shapes(n_total, bsz),
        input_output_aliases={n_in + j: j for j in range(N_MIXER_SHARED_OUT)},
        scratch_shapes=_mixer_scratch(True),
        compiler_params=pltpu.CompilerParams(
            dimension_semantics=("arbitrary",), vmem_limit_bytes=VMEM_LIMIT),
        name="mixer_sample",
    )(x, hist, state, *weights, *shared)


def _router_kernel(lt_ref, gates_ref, dest_ref, padend_ref, cnt_scr, base_scr, pstart_scr, before_scr):
    phase = pl.program_id(0)
    i = pl.program_id(1)
    shape = (NUM_EXPERTS, ROUTE_TOKENS)
    logits = lt_ref[...]
    row = lax.broadcasted_iota(jnp.int32, shape, 0)
    sel, vals = [], []
    for _ in range(TOP_K):
        m = jnp.max(logits, axis=0, keepdims=True)
        idx = jnp.min(jnp.where(logits == m, row, NUM_EXPERTS), axis=0, keepdims=True)
        hit = row == idx
        sel.append(hit)
        vals.append(m)
        logits = jnp.where(hit, -jnp.inf, logits)
    chosen = sum(jnp.where(hit, 1.0, 0.0) for hit in sel)
    tile_counts = jnp.broadcast_to(jnp.sum(chosen, axis=1, keepdims=True), (NUM_EXPERTS, LANES))

    @pl.when(phase == 0)
    def _():
        @pl.when(i == 0)
        def _():
            cnt_scr[...] = jnp.zeros_like(cnt_scr)

        cnt_scr[...] += tile_counts

    @pl.when(phase == 1)
    def _():
        @pl.when(i == 0)
        def _():
            blocks = jnp.floor((cnt_scr[...] + (MOE_ROWS - 1)) * (1.0 / MOE_ROWS))
            erow = lax.broadcasted_iota(jnp.int32, (NUM_EXPERTS, LANES), 0)
            cum = blocks
            shift = 1
            while shift < NUM_EXPERTS:
                cum = cum + jnp.where(erow >= shift, pltpu.roll(cum, shift, 0), 0.0)
                shift *= 2
            padend_ref[...] = cum * MOE_ROWS
            pstart_scr[...] = (cum - blocks) * MOE_ROWS
            base_scr[...] = jnp.zeros_like(base_scr)
            ti = lax.broadcasted_iota(jnp.int32, (ROUTE_TOKENS, ROUTE_TOKENS), 0)
            tj = lax.broadcasted_iota(jnp.int32, (ROUTE_TOKENS, ROUTE_TOKENS), 1)
            before_scr[...] = jnp.where(ti < tj, 1.0, 0.0).astype(BF16)

        earlier = _dot(chosen.astype(BF16), before_scr[...])
        pos = pstart_scr[:, 0:1] + base_scr[:, 0:1] + earlier
        dest = [jnp.sum(jnp.where(hit, pos, 0.0), axis=0, keepdims=True) for hit in sel]
        dest_ref[...] = jnp.concatenate(dest, axis=0).astype(jnp.int32)
        ex = [jnp.exp(v - vals[0]) for v in vals]
        denom = ex[0] + ex[1] + ex[2] + ex[3]
        gates_ref[...] = jnp.concatenate([e / denom for e in ex], axis=0)
        base_scr[...] += tile_counts


def _router(logits_t):
    n = logits_t.shape[1]
    assert n % ROUTE_TOKENS == 0
    tiles = n // ROUTE_TOKENS
    return pl.pallas_call(
        _router_kernel,
        grid=(2, tiles),
        in_specs=[pl.BlockSpec((NUM_EXPERTS, ROUTE_TOKENS), lambda p, i: (0, i))],
        out_specs=(
            pl.BlockSpec((TOP_K, ROUTE_TOKENS), lambda p, i: (0, i * p)),
            pl.BlockSpec((TOP_K, ROUTE_TOKENS), lambda p, i: (0, i * p)),
            pl.BlockSpec((NUM_EXPERTS, LANES), lambda p, i: (0, 0)),
        ),
        out_shape=(
            jax.ShapeDtypeStruct((TOP_K, n), F32),
            jax.ShapeDtypeStruct((TOP_K, n), jnp.int32),
            jax.ShapeDtypeStruct((NUM_EXPERTS, LANES), F32),
        ),
        scratch_shapes=[pltpu.VMEM((NUM_EXPERTS, LANES), F32)] * 3
        + [pltpu.VMEM((ROUTE_TOKENS, ROUTE_TOKENS), BF16)],
        compiler_params=pltpu.CompilerParams(
            dimension_semantics=("arbitrary", "arbitrary"), vmem_limit_bytes=VMEM_LIMIT),
        name="router",
    )(logits_t)


SC_CORES = 2
SC_SUBCORES = 16
SC_WORKERS = SC_CORES * SC_SUBCORES
DISPATCH_ROWS = 64


def _dispatch(h_packed, dest_chunks, m_pad):
    n = h_packed.shape[0]
    n_chunks = n // DISPATCH_ROWS
    assert n_chunks % SC_WORKERS == 0
    per_worker = n_chunks // SC_WORKERS
    mesh = plsc.VectorSubcoreMesh(core_axis_name="c", subcore_axis_name="s")

    @functools.partial(
        pl.kernel, mesh=mesh,
        out_type=jax.ShapeDtypeStruct((m_pad, HALF), jnp.int32),
        scratch_types=[
            pltpu.VMEM((2, TOP_K, DISPATCH_ROWS), jnp.int32),
            pltpu.VMEM((2, DISPATCH_ROWS, HALF), jnp.int32),
            pltpu.SemaphoreType.DMA((2,)),
            pltpu.SemaphoreType.DMA((2,)),
        ],
        compiler_params=pltpu.CompilerParams(use_tc_tiling_on_sc=True),
        name="dispatch",
    )
    def dispatch_kernel(h_hbm, dest_hbm, out_hbm, idx_v, rows_v, load_sems, scatter_sems):
        wid = lax.axis_index("s") * SC_CORES + lax.axis_index("c")

        def loads(j):
            chunk = wid * per_worker + j
            slot = j % 2
            return (
                pltpu.make_async_copy(dest_hbm.at[chunk], idx_v.at[slot], load_sems.at[slot]),
                pltpu.make_async_copy(h_hbm.at[pl.ds(chunk * DISPATCH_ROWS, DISPATCH_ROWS)],
                                      rows_v.at[slot], load_sems.at[slot]),
            )

        def scatters(j):
            slot = j % 2
            return [pltpu.make_async_copy(rows_v.at[slot], out_hbm.at[idx_v.at[slot, k]],
                                          scatter_sems.at[slot]) for k in range(TOP_K)]

        for cp in loads(0):
            cp.start()
        for j in range(per_worker):
            for cp in loads(j):
                cp.wait()
            if j >= 1:
                for cp in scatters(j - 1):
                    cp.wait()
            if j + 1 < per_worker:
                for cp in loads(j + 1):
                    cp.start()
            for cp in scatters(j):
                cp.start()
        for cp in scatters(per_worker - 1):
            cp.wait()

    return dispatch_kernel(h_packed, dest_chunks)


def _gather_expert_rows(y_sorted, dest_chunks, row_offset, n):
    n_chunks = n // DISPATCH_ROWS
    assert n_chunks % SC_WORKERS == 0 and row_offset % DISPATCH_ROWS == 0
    per_worker = n_chunks // SC_WORKERS
    chunk0 = row_offset // DISPATCH_ROWS
    mesh = plsc.VectorSubcoreMesh(core_axis_name="c", subcore_axis_name="s")

    @functools.partial(
        pl.kernel, mesh=mesh,
        out_type=jax.ShapeDtypeStruct((TOP_K, n, HALF), jnp.int32),
        scratch_types=[
            pltpu.VMEM((TOP_K, DISPATCH_ROWS), jnp.int32),
            pltpu.VMEM((2, DISPATCH_ROWS, HALF), jnp.int32),
            pltpu.SemaphoreType.DMA((2,)),
        ],
        compiler_params=pltpu.CompilerParams(use_tc_tiling_on_sc=True),
        name="gather_expert_rows",
    )
    def gather_kernel(y_hbm, dest_hbm, out_hbm, idx_v, rows_v, sems):
        wid = lax.axis_index("s") * SC_CORES + lax.axis_index("c")

        def gather(k):
            return pltpu.make_async_copy(y_hbm.at[idx_v.at[k]], rows_v.at[k % 2], sems.at[k % 2])

        @pl.loop(0, per_worker)
        def _(j):
            local = wid * per_worker + j
            pltpu.sync_copy(dest_hbm.at[chunk0 + local], idx_v)
            gather(0).start()
            for k in range(TOP_K):
                if k + 1 < TOP_K:
                    gather(k + 1).start()
                gather(k).wait()
                pltpu.sync_copy(rows_v.at[k % 2],
                                out_hbm.at[k, pl.ds(local * DISPATCH_ROWS, DISPATCH_ROWS)])

    return gather_kernel(y_sorted, dest_chunks)


def _moe_kernel(be_ref, nused_ref, x_ref, wgu_ref, bgu_ref, wd_ref, bd_ref, y_ref, wgu_bf, wd_bf):
    i = pl.program_id(0)

    @pl.when(i < nused_ref[0])
    def _():
        @pl.when((i == 0) | (be_ref[i] != be_ref[jnp.maximum(i - 1, 0)]))
        def _():
            wgu_bf[...] = wgu_ref[...].astype(BF16)
            wd_bf[...] = wd_ref[...].astype(BF16)

        gu = _dot(_unpack_bf16_pairs(x_ref[...]), wgu_bf[...]) + bgu_ref[...]
        gate = jnp.minimum(gu[:, :EXPERT_FF], SWIGLU_LIMIT)
        up = jnp.clip(gu[:, EXPERT_FF:], -SWIGLU_LIMIT, SWIGLU_LIMIT)
        hmid = gate * (1.0 / (1.0 + jnp.exp(-SWIGLU_ALPHA * gate))) * (up + 1.0)
        y = _dot(hmid.astype(BF16), wd_bf[...]) + bd_ref[...]
        y_ref[...] = _pack_bf16_pairs(y.astype(BF16))


def _moe_experts(block_expert, n_used, x_sorted, w_gu, b_gu, w_down, b_down):
    m_pad = x_sorted.shape[0]
    n_blocks = m_pad // MOE_ROWS

    def blk(i, be, nu):
        return jnp.minimum(i, nu[0] - 1)

    def expert(i, be, nu):
        return be[blk(i, be, nu)]

    grid_spec = pltpu.PrefetchScalarGridSpec(
        num_scalar_prefetch=2,
        grid=(n_blocks,),
        in_specs=[
            pl.BlockSpec((MOE_ROWS, HALF), lambda i, be, nu: (blk(i, be, nu), 0)),
            pl.BlockSpec((None, D_MODEL, 2 * EXPERT_FF), lambda i, be, nu: (expert(i, be, nu), 0, 0)),
            pl.BlockSpec((None, 1, 2 * EXPERT_FF), lambda i, be, nu: (expert(i, be, nu), 0, 0)),
            pl.BlockSpec((None, EXPERT_FF, D_MODEL), lambda i, be, nu: (expert(i, be, nu), 0, 0)),
            pl.BlockSpec((None, 1, D_MODEL), lambda i, be, nu: (expert(i, be, nu), 0, 0)),
        ],
        out_specs=pl.BlockSpec((MOE_ROWS, HALF), lambda i, be, nu: (blk(i, be, nu), 0)),
        scratch_shapes=[
            pltpu.VMEM((D_MODEL, 2 * EXPERT_FF), BF16),
            pltpu.VMEM((EXPERT_FF, D_MODEL), BF16),
        ],
    )
    return pl.pallas_call(
        _moe_kernel,
        grid_spec=grid_spec,
        out_shape=jax.ShapeDtypeStruct((m_pad, HALF), jnp.int32),
        compiler_params=pltpu.CompilerParams(
            dimension_semantics=("arbitrary",), vmem_limit_bytes=VMEM_LIMIT),
        name="moe_experts",
    )(block_expert, n_used, x_sorted, w_gu, b_gu, w_down, b_down)


def _combine_kernel(yk_ref, gates_ref, h_ref, g_ref, b_ref, out_ref):
    gates = gates_ref[...]
    lo = hi = None
    for k in range(TOP_K):
        u = lax.bitcast_convert_type(yk_ref[k], jnp.uint32)
        gk = gates[:, k:k + 1]
        lo_k = lax.bitcast_convert_type(u << 16, F32) * gk
        hi_k = lax.bitcast_convert_type(u & jnp.uint32(HI_MASK), F32) * gk
        lo = lo_k if lo is None else lo + lo_k
        hi = hi_k if hi is None else hi + hi_k
    acc = ALPHA * h_ref[...] + jnp.concatenate([lo, hi], axis=1)
    out_ref[...] = _layer_norm(acc, g_ref[...], b_ref[...])


def _combine_kernel_aliased(yk_ref, gates_ref, h_ref, g_ref, b_ref, prev_ref, out_ref):
    del prev_ref
    _combine_kernel(yk_ref, gates_ref, h_ref, g_ref, b_ref, out_ref)


def _combine(yk, gates, h_all, ln_g, ln_b, row_offset, out_rows, out_offset, out_prev):
    n_seg = yk.shape[1]
    tile0 = row_offset // COMBINE_TOKENS
    out_tile0 = out_offset // COMBINE_TOKENS
    in_specs = [
        pl.BlockSpec((TOP_K, COMBINE_TOKENS, HALF), lambda i: (0, i, 0)),
        pl.BlockSpec((COMBINE_TOKENS, TOP_K), lambda i: (tile0 + i, 0)),
        pl.BlockSpec((COMBINE_TOKENS, D_MODEL), lambda i: (tile0 + i, 0)),
        _const_spec((1, D_MODEL)),
        _const_spec((1, D_MODEL)),
    ]
    args = [yk, gates, h_all, ln_g, ln_b]
    aliases = {}
    kern = _combine_kernel
    if out_prev is not None:
        in_specs.append(pl.BlockSpec(memory_space=pl.ANY))
        aliases = {len(args): 0}
        args.append(out_prev)
        kern = _combine_kernel_aliased
    return pl.pallas_call(
        kern,
        grid=(n_seg // COMBINE_TOKENS,),
        in_specs=in_specs,
        out_specs=pl.BlockSpec((COMBINE_TOKENS, D_MODEL), lambda i: (out_tile0 + i, 0)),
        out_shape=jax.ShapeDtypeStruct((out_rows, D_MODEL), F32),
        input_output_aliases=aliases,
        compiler_params=pltpu.CompilerParams(
            dimension_semantics=("arbitrary",), vmem_limit_bytes=VMEM_LIMIT),
        name="moe_combine",
    )(*args)


def kernel(x_prompt, x_sample, state_pool, state_gla, w_in, w_pool, pool_scale, w_gate_up, b_gate,
           gla_norm_w, w_out, ln1_g, ln1_b, w_router, b_router, w_gu, b_gu, w_down, b_down,
           ln2_g, ln2_b):
    assert w_in.shape[0] == 1, "single-layer kernel"
    bp, seq, _ = x_prompt.shape
    bs, dec_seq, _ = x_sample.shape
    assert dec_seq == CHUNK and seq % TILE_TOKENS == 0 and bs % CHUNKS_PER_TILE == 0
    n_prompt = bp * seq
    n_sample = bs * dec_seq
    n_total = n_prompt + n_sample
    nk = n_total * TOP_K
    n_blocks = -(-nk // MOE_ROWS) + NUM_EXPERTS
    m_pad = n_blocks * MOE_ROWS

    weights = _mixer_weights(w_in, w_pool, pool_scale, w_gate_up, b_gate, gla_norm_w, w_out,
                             ln1_g, ln1_b, w_router, b_router)

    *shared, hist_p, s_p = _mixer_prompt(x_prompt, weights, n_total)
    h_all, hb_all, logits_t, hist_s, s_s = _mixer_sample(
        x_sample, state_pool[0], state_gla[0], weights, shared, n_prompt)

    gates_t, dest_t, pad_end = _router(logits_t)
    pad_end = pad_end[:, 0].astype(jnp.int32)
    block_start = jnp.arange(n_blocks, dtype=jnp.int32) * MOE_ROWS
    block_expert = jnp.minimum(jnp.sum((block_start[:, None] >= pad_end[None, :]).astype(jnp.int32), axis=1),
                               NUM_EXPERTS - 1)
    n_used = (pad_end[-1:] // MOE_ROWS).astype(jnp.int32)

    dest_chunks = dest_t.reshape(TOP_K, n_total // DISPATCH_ROWS, DISPATCH_ROWS).transpose(1, 0, 2)
    x_sorted = _dispatch(hb_all, dest_chunks, m_pad)
    y_sorted = _moe_experts(block_expert, n_used, x_sorted, w_gu[0], b_gu[0][:, None, :],
                            w_down[0], b_down[0][:, None, :])
    gates = gates_t.T
    ln_g, ln_b = ln2_g[0][None, :], ln2_b[0][None, :]
    seg = n_prompt // COMBINE_SEGMENTS
    y_prompt = None
    for s in range(COMBINE_SEGMENTS):
        yk = _gather_expert_rows(y_sorted, dest_chunks, s * seg, seg)
        y_prompt = _combine(yk, gates, h_all, ln_g, ln_b, s * seg, n_prompt, s * seg, y_prompt)
    yk = _gather_expert_rows(y_sorted, dest_chunks, n_prompt, n_sample)
    y_sample = _combine(yk, gates, h_all, ln_g, ln_b, n_prompt, n_sample, 0, None)
    y_prompt = y_prompt.reshape(bp, seq, D_MODEL)
    y_sample = y_sample.reshape(bs, dec_seq, D_MODEL)
    return (y_prompt, y_sample, hist_p[None], s_p[None], hist_s[None], s_s[None])
```

```python
import functools

import jax
import jax.numpy as jnp
from jax import lax
from jax.experimental import pallas as pl
from jax.experimental.pallas import tpu as pltpu
from jax.experimental.pallas import tpu_sc as plsc

F32 = jnp.float32
BF16 = jnp.bfloat16

D_MODEL = 1024
CHUNK = 64
PAST_LEN = 1024
POOL_WIDTH = 512
POOL_WINDOWS = (2, 4, 8, 16)
POOL_GROUP = 128
POOL_HIST = 15
GLA_HEADS = 4
GLA_DK = 64
GLA_DV = 128
GATE_RANK = 16
GATE_NORMALIZER = 16.0
NUM_EXPERTS = 32
TOP_K = 4
EXPERT_FF = 1024
SWIGLU_LIMIT = 7.0
SWIGLU_ALPHA = 1.702
LN_EPS = 1e-5
RMS_EPS = 1e-6
ALPHA = 2.0 ** 0.25

Q0 = POOL_WIDTH
K0 = Q0 + GLA_HEADS * GLA_DK
V0 = K0 + GLA_HEADS * GLA_DK
R0 = V0 + GLA_HEADS * GLA_DV
N_MAIN = R0 + GLA_HEADS * GLA_DV

LANES = 128
TILE_TOKENS = 512
CHUNKS_PER_TILE = TILE_TOKENS // CHUNK
HIST_PAD = 16
OUT_CHUNKS = 4
SCAN_ROWS = 256
MOE_ROWS = 512
MOE_STEP_ROWS = 1024
ROUTE_TOKENS = 1024
COMBINE_TOKENS = 512
COMBINE_SEGMENTS = 4
VMEM_LIMIT = 56 * 1024 * 1024


def _dot(a, b):
    return jnp.dot(a, b, preferred_element_type=F32)


def _dot_nt(a, b):
    return lax.dot_general(a, b, (((1,), (1,)), ((), ())), preferred_element_type=F32)


def _dot_tn(a, b):
    return lax.dot_general(a, b, (((0,), (0,)), ((), ())), preferred_element_type=F32)


HALF = D_MODEL // 2
HI_MASK = 0xFFFF0000


def _pack_bf16_pairs(xb):
    lo = lax.bitcast_convert_type(xb[:, :HALF].astype(F32), jnp.uint32) >> 16
    hi = lax.bitcast_convert_type(xb[:, HALF:].astype(F32), jnp.uint32) & jnp.uint32(HI_MASK)
    return lax.bitcast_convert_type(hi | lo, jnp.int32)


def _unpack_bf16_pairs(p):
    u = lax.bitcast_convert_type(p, jnp.uint32)
    lo = lax.bitcast_convert_type(u << 16, F32)
    hi = lax.bitcast_convert_type(u & jnp.uint32(HI_MASK), F32)
    return jnp.concatenate([lo, hi], axis=1).astype(BF16)


def _layer_norm(v, g, b):
    mu = jnp.mean(v, axis=-1, keepdims=True)
    c = v - mu
    var = jnp.mean(c * c, axis=-1, keepdims=True)
    return c * lax.rsqrt(var + LN_EPS) * g + b


N_MIXER_WEIGHTS = 12
N_MIXER_SHARED_OUT = 3


def _mixer_kernel(per_chunk_state, pos0, *refs):
    if per_chunk_state:
        (x_ref, hist_in_ref, s_in_ref, *rest) = refs
    else:
        (x_ref, *rest) = refs
        hist_in_ref = s_in_ref = None
    (w_main_ref, w_glr_ref, w_gate_ref, b_gate_ref, w_pool_ref, pscale_ref, gnorm_ref,
     w_out_ref, ln1g_ref, ln1b_ref, w_router_ref, b_router_ref, *rest) = rest
    if per_chunk_state:
        rest = rest[N_MIXER_SHARED_OUT:]
    (h_ref, hb_ref, logits_ref, hist_out_ref, s_out_ref,
     proj_scr, b_scr, o_scr, ext_scr, st_scr, tri_scr) = rest

    if per_chunk_state:
        t = None
        first_step = pl.program_id(0) == 0
    else:
        t = pl.program_id(1)
        first_step = (pl.program_id(0) == 0) & (t == 0)
    x = x_ref[...].reshape(TILE_TOKENS, D_MODEL)
    xb = x.astype(BF16)

    @pl.when(first_step)
    def _():
        ti = lax.broadcasted_iota(jnp.int32, (SCAN_ROWS, SCAN_ROWS), 0)
        tj = lax.broadcasted_iota(jnp.int32, (SCAN_ROWS, SCAN_ROWS), 1)
        same_chunk = (ti // CHUNK) == (tj // CHUNK)
        tri_scr[...] = jnp.where(same_chunk & (ti >= tj), 1.0, 0.0).astype(BF16)
        if not per_chunk_state:
            st_scr[...] = jnp.zeros_like(st_scr)
            ext_scr[0:HIST_PAD, :] = jnp.zeros((HIST_PAD, POOL_WIDTH), F32)

    glr = _dot(xb, w_glr_ref[...])
    proj_scr[:, 0:V0] = _dot(xb, w_main_ref[:, 0:V0])
    gk = _dot(glr.astype(BF16), w_gate_ref[...]) + b_gate_ref[...]
    log_sig = jnp.minimum(gk, 0.0) - jnp.log1p(jnp.exp(-jnp.abs(gk)))
    g = log_sig / GATE_NORMALIZER
    g_hi = g.astype(BF16)
    g_lo = (g - g_hi.astype(F32)).astype(BF16)
    proj_scr[:, V0:N_MAIN] = _dot(xb, w_main_ref[:, V0:N_MAIN])
    for s in range(TILE_TOKENS // SCAN_ROWS):
        rs = slice(s * SCAN_ROWS, (s + 1) * SCAN_ROWS)
        b_scr[rs, :] = _dot(tri_scr[...], g_hi[rs]) + _dot(tri_scr[...], g_lo[rs])

    if per_chunk_state:
        seg_len, seg_stride, n_seg = CHUNK, CHUNK + HIST_PAD, CHUNKS_PER_TILE
        for c in range(n_seg):
            base = c * seg_stride
            ext_scr[base:base + HIST_PAD, :] = jnp.zeros((HIST_PAD, POOL_WIDTH), F32)
            ext_scr[base + 1:base + HIST_PAD, :] = hist_in_ref[c]
            ext_scr[base + HIST_PAD:base + seg_stride, :] = proj_scr[c * CHUNK:(c + 1) * CHUNK, 0:POOL_WIDTH]
        row_pos = pos0 + lax.broadcasted_iota(jnp.int32, (seg_len, POOL_GROUP), 0)
    else:
        seg_len, seg_stride, n_seg = TILE_TOKENS, TILE_TOKENS + HIST_PAD, 1
        ext_scr[0:HIST_PAD, :] = jnp.where(t == 0, 0.0, ext_scr[0:HIST_PAD, :])
        ext_scr[HIST_PAD:seg_stride, :] = proj_scr[:, 0:POOL_WIDTH]
        row_pos = pos0 + t * TILE_TOKENS + lax.broadcasted_iota(jnp.int32, (seg_len, POOL_GROUP), 0)

    pooled_groups = []
    for gi, w in enumerate(POOL_WINDOWS):
        gs = slice(gi * POOL_GROUP, (gi + 1) * POOL_GROUP)
        cnt = jnp.minimum(row_pos + 1, w).astype(F32)
        ext = ext_scr[:, gs]
        win = ext
        shift = 1
        while shift < w:
            win = win + pltpu.roll(win, shift, 0)
            shift *= 2
        segs = []
        for s in range(n_seg):
            base = s * seg_stride + HIST_PAD
            segs.append(win[base:base + seg_len] / cnt - ext[base:base + seg_len])
        pooled = segs[0] if n_seg == 1 else jnp.concatenate(segs, axis=0)
        pooled_groups.append(pooled.astype(BF16))
    pool_cols = []
    for p in range(len(POOL_WINDOWS) // 2):
        both = jnp.concatenate(pooled_groups[2 * p:2 * p + 2], axis=1)
        pool_cols.append(_dot(both, w_pool_ref[p]))
    pool_out = jnp.concatenate(pool_cols, axis=1) * pscale_ref[...]

    if per_chunk_state:
        for c in range(n_seg):
            end = (c + 1) * seg_stride
            hist_out_ref[c] = ext_scr[end - POOL_HIST:end, :]
    else:
        hist_out_ref[...] = ext_scr[seg_stride - POOL_HIST:seg_stride, :]
        ext_scr[0:HIST_PAD, :] = ext_scr[TILE_TOKENS:seg_stride, :]

    hk = GLA_HEADS * GLA_DK
    hv = GLA_HEADS * GLA_DV
    pair_rows = 2 * CHUNK
    decay_cols = LANES // CHUNKS_PER_TILE

    def head_of(shape, dim, width):
        return lax.broadcasted_iota(jnp.int32, shape, dim) // width

    same_head_k = head_of((hk, hk), 0, CHUNK) == head_of((hk, hk), 1, GLA_DK)
    same_head_v = head_of((hk, hv), 0, CHUNK) == head_of((hk, hv), 1, GLA_DV)
    pair_half = head_of((pair_rows, hv), 0, CHUNK)
    causal = (lax.broadcasted_iota(jnp.int32, (CHUNK, hk), 0)
              >= lax.broadcasted_iota(jnp.int32, (CHUNK, hk), 1) % CHUNK)

    b_all = b_scr[...]
    b_last = [b_scr[(c + 1) * CHUNK - 1:(c + 1) * CHUNK, :] for c in range(CHUNKS_PER_TILE)]
    b_last_rows = jnp.concatenate([jnp.broadcast_to(bl, (CHUNK, hk)) for bl in b_last], axis=0)
    k_all = proj_scr[:, K0:V0]
    qt_all = (proj_scr[:, Q0:K0] * (GLA_DK ** -0.5) * jnp.exp(b_all)).astype(BF16)
    kt_all = k_all * jnp.exp(-b_all)
    kl_t = jnp.transpose(k_all * jnp.exp(b_last_rows - b_all)).astype(BF16)
    decay_t = jnp.transpose(jnp.exp(jnp.concatenate(
        [jnp.broadcast_to(bl, (decay_cols, hk)) for bl in b_last], axis=0)))

    def finish_rows(rs):
        r = proj_scr[rs, R0:N_MAIN]
        silu_r = r * (1.0 / (1.0 + jnp.exp(-r)))
        gated = []
        for h in range(GLA_HEADS):
            vs = slice(h * GLA_DV, (h + 1) * GLA_DV)
            oh = o_scr[rs, vs]
            ms = jnp.mean(oh * oh, axis=-1, keepdims=True)
            gated.append(oh * lax.rsqrt(ms + RMS_EPS) * gnorm_ref[...] * silu_r[:, vs])
        mix_in = jnp.concatenate([pool_out[rs]] + gated, axis=1).astype(BF16)
        resid = ALPHA * x[rs] + _dot(mix_in, w_out_ref[...])
        h_val = _layer_norm(resid, ln1g_ref[...], ln1b_ref[...])
        h_ref[rs, :] = h_val
        hb = h_val.astype(BF16)
        hb_ref[rs, :] = _pack_bf16_pairs(hb)
        logits_ref[:, rs] = _dot_nt(w_router_ref[...], hb) + b_router_ref[:, 0:1]

    st = None if per_chunk_state else jnp.where(t == 0, 0.0, st_scr[...])
    for c in range(CHUNKS_PER_TILE):
        rows = slice(c * CHUNK, (c + 1) * CHUNK)
        pair = slice((c // 2) * pair_rows, (c // 2 + 1) * pair_rows)
        if per_chunk_state:
            st = s_in_ref[c].reshape(hk, GLA_DV)
        qt = qt_all[rows]
        k_stack = jnp.where(same_head_k, jnp.concatenate([kt_all[rows]] * GLA_HEADS, axis=0), 0.0)
        v_stack = jnp.where(same_head_v, jnp.concatenate([proj_scr[rows, V0:R0]] * GLA_HEADS, axis=0), 0.0)
        s_stack = jnp.where(same_head_v, jnp.concatenate([st] * GLA_HEADS, axis=1), 0.0)
        att = jnp.where(causal, _dot_nt(qt, k_stack.astype(BF16)), 0.0)
        o_scr[rows, :] = (_dot(att.astype(BF16), v_stack.astype(BF16))
                          + _dot(qt, s_stack.astype(BF16)))
        v_chunk = jnp.where(pair_half == c % 2, proj_scr[pair, V0:R0], 0.0).astype(BF16)
        upd = jnp.concatenate(
            [_dot(kl_t[h * GLA_DK:(h + 1) * GLA_DK, pair], v_chunk[:, h * GLA_DV:(h + 1) * GLA_DV])
             for h in range(GLA_HEADS)], axis=0)
        st = st * decay_t[:, c * decay_cols:c * decay_cols + 1] + upd
        if per_chunk_state:
            s_out_ref[c] = st.reshape(GLA_HEADS, GLA_DK, GLA_DV)
        if (c + 1) % OUT_CHUNKS == 0:
            finish_rows(slice((c + 1 - OUT_CHUNKS) * CHUNK, (c + 1) * CHUNK))

    if not per_chunk_state:
        st_scr[...] = st
        s_out_ref[...] = st.reshape(GLA_HEADS, GLA_DK, GLA_DV)


def _const_spec(shape):
    nd = len(shape)
    return pl.BlockSpec(shape, lambda *_: (0,) * nd)


def _mixer_weight_specs():
    return [
        _const_spec((D_MODEL, N_MAIN)),
        _const_spec((D_MODEL, LANES)),
        _const_spec((LANES, GLA_HEADS * GLA_DK)),
        _const_spec((1, GLA_HEADS * GLA_DK)),
        _const_spec((len(POOL_WINDOWS) // 2, 2 * POOL_GROUP, 2 * POOL_GROUP)),
        _const_spec((1, POOL_WIDTH)),
        _const_spec((1, GLA_DV)),
        _const_spec((D_MODEL, D_MODEL)),
        _const_spec((1, D_MODEL)),
        _const_spec((1, D_MODEL)),
        _const_spec((NUM_EXPERTS, D_MODEL)),
        _const_spec((NUM_EXPERTS, LANES)),
    ]


def _mixer_weights(w_in, w_pool, pool_scale, w_gate_up, b_gate, gla_norm_w, w_out, ln1_g, ln1_b,
                   w_router, b_router):
    w_glr = jnp.zeros((D_MODEL, LANES), BF16).at[:, :GATE_RANK].set(w_in[0, :, N_MAIN:].astype(BF16))
    w_gate = jnp.zeros((LANES, GLA_HEADS * GLA_DK), BF16).at[:GATE_RANK].set(w_gate_up[0].astype(BF16))
    wp = w_pool[0].astype(BF16)
    zero = jnp.zeros((POOL_GROUP, POOL_GROUP), BF16)
    w_pool_pairs = jnp.stack([jnp.block([[wp[2 * p], zero], [zero, wp[2 * p + 1]]])
                              for p in range(len(POOL_WINDOWS) // 2)])
    weights = (
        w_in[0, :, :N_MAIN].astype(BF16), w_glr, w_gate, b_gate[0][None, :],
        w_pool_pairs, pool_scale[0][None, :], gla_norm_w[0][None, :],
        w_out[0].astype(BF16), ln1_g[0][None, :], ln1_b[0][None, :],
        w_router[0].T.astype(BF16), jnp.broadcast_to(b_router[0][:, None], (NUM_EXPERTS, LANES)),
    )
    assert len(weights) == N_MIXER_WEIGHTS
    return weights


def _mixer_scratch(per_chunk_state):
    ext_rows = (CHUNKS_PER_TILE * (CHUNK + HIST_PAD)) if per_chunk_state else (TILE_TOKENS + HIST_PAD)
    return [
        pltpu.VMEM((TILE_TOKENS, N_MAIN), F32),
        pltpu.VMEM((TILE_TOKENS, GLA_HEADS * GLA_DK), F32),
        pltpu.VMEM((TILE_TOKENS, GLA_HEADS * GLA_DV), F32),
        pltpu.VMEM((ext_rows, POOL_WIDTH), F32),
        pltpu.VMEM((GLA_HEADS * GLA_DK, GLA_DV), F32),
        pltpu.VMEM((SCAN_ROWS, SCAN_ROWS), BF16),
    ]


def _mixer_out_shapes(n_total, bsz):
    return (
        jax.ShapeDtypeStruct((n_total, D_MODEL), F32),
        jax.ShapeDtypeStruct((n_total, HALF), jnp.int32),
        jax.ShapeDtypeStruct((NUM_EXPERTS, n_total), F32),
        jax.ShapeDtypeStruct((bsz, POOL_HIST, POOL_WIDTH), F32),
        jax.ShapeDtypeStruct((bsz, GLA_HEADS, GLA_DK, GLA_DV), F32),
    )


def _mixer_prompt(x, weights, n_total):
    bsz, seq, _ = x.shape
    tiles = seq // TILE_TOKENS
    return pl.pallas_call(
        functools.partial(_mixer_kernel, False, 0),
        grid=(bsz, tiles),
        in_specs=[pl.BlockSpec((None, TILE_TOKENS, D_MODEL), lambda b, t: (b, t, 0))] + _mixer_weight_specs(),
        out_specs=(
            pl.BlockSpec((TILE_TOKENS, D_MODEL), lambda b, t: (b * tiles + t, 0)),
            pl.BlockSpec((TILE_TOKENS, HALF), lambda b, t: (b * tiles + t, 0)),
            pl.BlockSpec((NUM_EXPERTS, TILE_TOKENS), lambda b, t: (0, b * tiles + t)),
            pl.BlockSpec((None, POOL_HIST, POOL_WIDTH), lambda b, t: (b, 0, 0)),
            pl.BlockSpec((None, GLA_HEADS, GLA_DK, GLA_DV), lambda b, t: (b, 0, 0, 0)),
        ),
        out_shape=_mixer_out_shapes(n_total, bsz),
        scratch_shapes=_mixer_scratch(False),
        compiler_params=pltpu.CompilerParams(
            dimension_semantics=("arbitrary", "arbitrary"), vmem_limit_bytes=VMEM_LIMIT),
        name="mixer_prompt",
    )(x, *weights)


def _mixer_sample(x, hist, state, weights, shared, row_offset):
    bsz = x.shape[0]
    tiles = bsz // CHUNKS_PER_TILE
    tile0 = row_offset // TILE_TOKENS
    n_total = shared[0].shape[0]
    n_in = 3 + N_MIXER_WEIGHTS
    return pl.pallas_call(
        functools.partial(_mixer_kernel, True, PAST_LEN),
        grid=(tiles,),
        in_specs=[
            pl.BlockSpec((CHUNKS_PER_TILE, CHUNK, D_MODEL), lambda i: (i, 0, 0)),
            pl.BlockSpec((CHUNKS_PER_TILE, POOL_HIST, POOL_WIDTH), lambda i: (i, 0, 0)),
            pl.BlockSpec((CHUNKS_PER_TILE, GLA_HEADS, GLA_DK, GLA_DV), lambda i: (i, 0, 0, 0)),
        ] + _mixer_weight_specs() + [pl.BlockSpec(memory_space=pl.ANY)] * N_MIXER_SHARED_OUT,
        out_specs=(
            pl.BlockSpec((TILE_TOKENS, D_MODEL), lambda i: (tile0 + i, 0)),
            pl.BlockSpec((TILE_TOKENS, HALF), lambda i: (tile0 + i, 0)),
            pl.BlockSpec((NUM_EXPERTS, TILE_TOKENS), lambda i: (0, tile0 + i)),
            pl.BlockSpec((CHUNKS_PER_TILE, POOL_HIST, POOL_WIDTH), lambda i: (i, 0, 0)),
            pl.BlockSpec((CHUNKS_PER_TILE, GLA_HEADS, GLA_DK, GLA_DV), lambda i: (i, 0, 0, 0)),
        ),
        out_shape=_mixer_out_shapes(n_total, bsz),
        input_output_aliases={n_in + j: j for j in range(N_MIXER_SHARED_OUT)},
        scratch_shapes=_mixer_scratch(True),
        compiler_params=pltpu.CompilerParams(
            dimension_semantics=("arbitrary",), vmem_limit_bytes=VMEM_LIMIT),
        name="mixer_sample",
    )(x, hist, state, *weights, *shared)


def _router_kernel(lt_ref, gates_ref, dest_ref, padend_ref, cnt_scr, base_scr, pstart_scr, before_scr):
    phase = pl.program_id(0)
    i = pl.program_id(1)
    shape = (NUM_EXPERTS, ROUTE_TOKENS)
    logits = lt_ref[...]
    row = lax.broadcasted_iota(jnp.int32, shape, 0)
    sel, vals = [], []
    for _ in range(TOP_K):
        m = jnp.max(logits, axis=0, keepdims=True)
        idx = jnp.min(jnp.where(logits == m, row, NUM_EXPERTS), axis=0, keepdims=True)
        hit = row == idx
        sel.append(hit)
        vals.append(m)
        logits = jnp.where(hit, -jnp.inf, logits)
    chosen = sum(jnp.where(hit, 1.0, 0.0) for hit in sel)
    tile_counts = jnp.broadcast_to(jnp.sum(chosen, axis=1, keepdims=True), (NUM_EXPERTS, LANES))

    @pl.when(phase == 0)
    def _():
        @pl.when(i == 0)
        def _():
            cnt_scr[...] = jnp.zeros_like(cnt_scr)

        cnt_scr[...] += tile_counts

    @pl.when(phase == 1)
    def _():
        @pl.when(i == 0)
        def _():
            blocks = jnp.floor((cnt_scr[...] + (MOE_STEP_ROWS - 1)) * (1.0 / MOE_STEP_ROWS))
            erow = lax.broadcasted_iota(jnp.int32, (NUM_EXPERTS, LANES), 0)
            lane = lax.broadcasted_iota(jnp.int32, (NUM_EXPERTS, LANES), 1)
            cum = blocks
            shift = 1
            while shift < NUM_EXPERTS:
                cum = cum + jnp.where(erow >= shift, pltpu.roll(cum, shift, 0), 0.0)
                shift *= 2
            padend_ref[...] = jnp.where(lane == 1, cnt_scr[...], cum * MOE_STEP_ROWS)
            pstart_scr[...] = (cum - blocks) * MOE_STEP_ROWS
            base_scr[...] = jnp.zeros_like(base_scr)
            ti = lax.broadcasted_iota(jnp.int32, (ROUTE_TOKENS, ROUTE_TOKENS), 0)
            tj = lax.broadcasted_iota(jnp.int32, (ROUTE_TOKENS, ROUTE_TOKENS), 1)
            before_scr[...] = jnp.where(ti < tj, 1.0, 0.0).astype(BF16)

        earlier = _dot(chosen.astype(BF16), before_scr[...])
        pos = pstart_scr[:, 0:1] + base_scr[:, 0:1] + earlier
        dest = [jnp.sum(jnp.where(hit, pos, 0.0), axis=0, keepdims=True) for hit in sel]
        dest_ref[...] = jnp.concatenate(dest, axis=0).astype(jnp.int32)
        ex = [jnp.exp(v - vals[0]) for v in vals]
        denom = ex[0] + ex[1] + ex[2] + ex[3]
        gates_ref[...] = jnp.concatenate([e / denom for e in ex], axis=0)
        base_scr[...] += tile_counts


def _router(logits_t):
    n = logits_t.shape[1]
    assert n % ROUTE_TOKENS == 0
    tiles = n // ROUTE_TOKENS
    return pl.pallas_call(
        _router_kernel,
        grid=(2, tiles),
        in_specs=[pl.BlockSpec((NUM_EXPERTS, ROUTE_TOKENS), lambda p, i: (0, i))],
        out_specs=(
            pl.BlockSpec((TOP_K, ROUTE_TOKENS), lambda p, i: (0, i * p)),
            pl.BlockSpec((TOP_K, ROUTE_TOKENS), lambda p, i: (0, i * p)),
            pl.BlockSpec((NUM_EXPERTS, LANES), lambda p, i: (0, 0)),
        ),
        out_shape=(
            jax.ShapeDtypeStruct((TOP_K, n), F32),
            jax.ShapeDtypeStruct((TOP_K, n), jnp.int32),
            jax.ShapeDtypeStruct((NUM_EXPERTS, LANES), F32),
        ),
        scratch_shapes=[pltpu.VMEM((NUM_EXPERTS, LANES), F32)] * 3
        + [pltpu.VMEM((ROUTE_TOKENS, ROUTE_TOKENS), BF16)],
        compiler_params=pltpu.CompilerParams(
            dimension_semantics=("arbitrary", "arbitrary"), vmem_limit_bytes=VMEM_LIMIT),
        name="router",
    )(logits_t)


SC_CORES = 2
SC_SUBCORES = 16
SC_WORKERS = SC_CORES * SC_SUBCORES
DISPATCH_ROWS = 64


def _dispatch(h_packed, dest_chunks, m_pad):
    n = h_packed.shape[0]
    n_chunks = n // DISPATCH_ROWS
    assert n_chunks % SC_WORKERS == 0
    per_worker = n_chunks // SC_WORKERS
    mesh = plsc.VectorSubcoreMesh(core_axis_name="c", subcore_axis_name="s")

    @functools.partial(
        pl.kernel, mesh=mesh,
        out_type=jax.ShapeDtypeStruct((m_pad, HALF), jnp.int32),
        scratch_types=[
            pltpu.VMEM((2, TOP_K, DISPATCH_ROWS), jnp.int32),
            pltpu.VMEM((2, DISPATCH_ROWS, HALF), jnp.int32),
            pltpu.SemaphoreType.DMA((2,)),
            pltpu.SemaphoreType.DMA((2,)),
        ],
        compiler_params=pltpu.CompilerParams(use_tc_tiling_on_sc=True),
        name="dispatch",
    )
    def dispatch_kernel(h_hbm, dest_hbm, out_hbm, idx_v, rows_v, load_sems, scatter_sems):
        wid = lax.axis_index("s") * SC_CORES + lax.axis_index("c")

        def loads(j):
            chunk = wid * per_worker + j
            slot = j % 2
            return (
                pltpu.make_async_copy(dest_hbm.at[chunk], idx_v.at[slot], load_sems.at[slot]),
                pltpu.make_async_copy(h_hbm.at[pl.ds(chunk * DISPATCH_ROWS, DISPATCH_ROWS)],
                                      rows_v.at[slot], load_sems.at[slot]),
            )

        def scatters(j):
            slot = j % 2
            return [pltpu.make_async_copy(rows_v.at[slot], out_hbm.at[idx_v.at[slot, k]],
                                          scatter_sems.at[slot]) for k in range(TOP_K)]

        for cp in loads(0):
            cp.start()
        for j in range(per_worker):
            for cp in loads(j):
                cp.wait()
            if j >= 1:
                for cp in scatters(j - 1):
                    cp.wait()
            if j + 1 < per_worker:
                for cp in loads(j + 1):
                    cp.start()
            for cp in scatters(j):
                cp.start()
        for cp in scatters(per_worker - 1):
            cp.wait()

    return dispatch_kernel(h_packed, dest_chunks)


def _gather_expert_rows(y_sorted, dest_chunks, row_offset, n):
    n_chunks = n // DISPATCH_ROWS
    assert n_chunks % SC_WORKERS == 0 and row_offset % DISPATCH_ROWS == 0
    per_worker = n_chunks // SC_WORKERS
    chunk0 = row_offset // DISPATCH_ROWS
    mesh = plsc.VectorSubcoreMesh(core_axis_name="c", subcore_axis_name="s")

    @functools.partial(
        pl.kernel, mesh=mesh,
        out_type=jax.ShapeDtypeStruct((TOP_K, n, HALF), jnp.int32),
        scratch_types=[
            pltpu.VMEM((TOP_K, DISPATCH_ROWS), jnp.int32),
            pltpu.VMEM((2, DISPATCH_ROWS, HALF), jnp.int32),
            pltpu.SemaphoreType.DMA((2,)),
        ],
        compiler_params=pltpu.CompilerParams(use_tc_tiling_on_sc=True),
        name="gather_expert_rows",
    )
    def gather_kernel(y_hbm, dest_hbm, out_hbm, idx_v, rows_v, sems):
        wid = lax.axis_index("s") * SC_CORES + lax.axis_index("c")

        def gather(k):
            return pltpu.make_async_copy(y_hbm.at[idx_v.at[k]], rows_v.at[k % 2], sems.at[k % 2])

        @pl.loop(0, per_worker)
        def _(j):
            local = wid * per_worker + j
            pltpu.sync_copy(dest_hbm.at[chunk0 + local], idx_v)
            gather(0).start()
            for k in range(TOP_K):
                if k + 1 < TOP_K:
                    gather(k + 1).start()
                gather(k).wait()
                pltpu.sync_copy(rows_v.at[k % 2],
                                out_hbm.at[k, pl.ds(local * DISPATCH_ROWS, DISPATCH_ROWS)])

    return gather_kernel(y_sorted, dest_chunks)


def _moe_kernel(be_ref, nused_ref, nsub_ref, x_ref, wgu_ref, bgu_ref, wd_ref, bd_ref, y_ref,
                wgu_bf, wd_bf):
    i = pl.program_id(0)

    def ffn(rows):
        gu = _dot(_unpack_bf16_pairs(x_ref[rows, :]), wgu_bf[...]) + bgu_ref[...]
        gate = jnp.minimum(gu[:, :EXPERT_FF], SWIGLU_LIMIT)
        up = jnp.clip(gu[:, EXPERT_FF:], -SWIGLU_LIMIT, SWIGLU_LIMIT)
        hmid = gate * (1.0 / (1.0 + jnp.exp(-SWIGLU_ALPHA * gate))) * (up + 1.0)
        y = _dot(hmid.astype(BF16), wd_bf[...]) + bd_ref[...]
        y_ref[rows, :] = _pack_bf16_pairs(y.astype(BF16))

    @pl.when(i < nused_ref[0])
    def _():
        @pl.when((i == 0) | (be_ref[i] != be_ref[jnp.maximum(i - 1, 0)]))
        def _():
            wgu_bf[...] = wgu_ref[...].astype(BF16)
            wd_bf[...] = wd_ref[...].astype(BF16)

        ffn(slice(0, MOE_ROWS))
        for s in range(1, MOE_STEP_ROWS // MOE_ROWS):
            @pl.when(nsub_ref[i] > s)
            def _():
                ffn(slice(s * MOE_ROWS, (s + 1) * MOE_ROWS))


def _moe_experts(step_expert, n_used, n_sub, x_sorted, w_gu, b_gu, w_down, b_down):
    m_pad = x_sorted.shape[0]
    n_steps = m_pad // MOE_STEP_ROWS

    def blk(i, be, nu, ns):
        return jnp.minimum(i, nu[0] - 1)

    def expert(i, be, nu, ns):
        return be[blk(i, be, nu, ns)]

    grid_spec = pltpu.PrefetchScalarGridSpec(
        num_scalar_prefetch=3,
        grid=(n_steps,),
        in_specs=[
            pl.BlockSpec((MOE_STEP_ROWS, HALF), lambda *a: (blk(*a), 0)),
            pl.BlockSpec((None, D_MODEL, 2 * EXPERT_FF), lambda *a: (expert(*a), 0, 0)),
            pl.BlockSpec((None, 1, 2 * EXPERT_FF), lambda *a: (expert(*a), 0, 0)),
            pl.BlockSpec((None, EXPERT_FF, D_MODEL), lambda *a: (expert(*a), 0, 0)),
            pl.BlockSpec((None, 1, D_MODEL), lambda *a: (expert(*a), 0, 0)),
        ],
        out_specs=pl.BlockSpec((MOE_STEP_ROWS, HALF), lambda *a: (blk(*a), 0)),
        scratch_shapes=[
            pltpu.VMEM((D_MODEL, 2 * EXPERT_FF), BF16),
            pltpu.VMEM((EXPERT_FF, D_MODEL), BF16),
        ],
    )
    return pl.pallas_call(
        _moe_kernel,
        grid_spec=grid_spec,
        out_shape=jax.ShapeDtypeStruct((m_pad, HALF), jnp.int32),
        compiler_params=pltpu.CompilerParams(
            dimension_semantics=("arbitrary",), vmem_limit_bytes=VMEM_LIMIT),
        name="moe_experts",
    )(step_expert, n_used, n_sub, x_sorted, w_gu, b_gu, w_down, b_down)


def _combine_kernel(yk_ref, gates_ref, h_ref, g_ref, b_ref, out_ref):
    gates = gates_ref[...]
    lo = hi = None
    for k in range(TOP_K):
        u = lax.bitcast_convert_type(yk_ref[k], jnp.uint32)
        gk = gates[:, k:k + 1]
        lo_k = lax.bitcast_convert_type(u << 16, F32) * gk
        hi_k = lax.bitcast_convert_type(u & jnp.uint32(HI_MASK), F32) * gk
        lo = lo_k if lo is None else lo + lo_k
        hi = hi_k if hi is None else hi + hi_k
    acc = ALPHA * h_ref[...] + jnp.concatenate([lo, hi], axis=1)
    out_ref[...] = _layer_norm(acc, g_ref[...], b_ref[...])


def _combine_kernel_aliased(yk_ref, gates_ref, h_ref, g_ref, b_ref, prev_ref, out_ref):
    del prev_ref
    _combine_kernel(yk_ref, gates_ref, h_ref, g_ref, b_ref, out_ref)


def _combine(yk, gates, h_all, ln_g, ln_b, row_offset, out_rows, out_offset, out_prev):
    n_seg = yk.shape[1]
    tile0 = row_offset // COMBINE_TOKENS
    out_tile0 = out_offset // COMBINE_TOKENS
    in_specs = [
        pl.BlockSpec((TOP_K, COMBINE_TOKENS, HALF), lambda i: (0, i, 0)),
        pl.BlockSpec((COMBINE_TOKENS, TOP_K), lambda i: (tile0 + i, 0)),
        pl.BlockSpec((COMBINE_TOKENS, D_MODEL), lambda i: (tile0 + i, 0)),
        _const_spec((1, D_MODEL)),
        _const_spec((1, D_MODEL)),
    ]
    args = [yk, gates, h_all, ln_g, ln_b]
    aliases = {}
    kern = _combine_kernel
    if out_prev is not None:
        in_specs.append(pl.BlockSpec(memory_space=pl.ANY))
        aliases = {len(args): 0}
        args.append(out_prev)
        kern = _combine_kernel_aliased
    return pl.pallas_call(
        kern,
        grid=(n_seg // COMBINE_TOKENS,),
        in_specs=in_specs,
        out_specs=pl.BlockSpec((COMBINE_TOKENS, D_MODEL), lambda i: (out_tile0 + i, 0)),
        out_shape=jax.ShapeDtypeStruct((out_rows, D_MODEL), F32),
        input_output_aliases=aliases,
        compiler_params=pltpu.CompilerParams(
            dimension_semantics=("arbitrary",), vmem_limit_bytes=VMEM_LIMIT),
        name="moe_combine",
    )(*args)


def kernel(x_prompt, x_sample, state_pool, state_gla, w_in, w_pool, pool_scale, w_gate_up, b_gate,
           gla_norm_w, w_out, ln1_g, ln1_b, w_router, b_router, w_gu, b_gu, w_down, b_down,
           ln2_g, ln2_b):
    assert w_in.shape[0] == 1, "single-layer kernel"
    bp, seq, _ = x_prompt.shape
    bs, dec_seq, _ = x_sample.shape
    assert dec_seq == CHUNK and seq % TILE_TOKENS == 0 and bs % CHUNKS_PER_TILE == 0
    n_prompt = bp * seq
    n_sample = bs * dec_seq
    n_total = n_prompt + n_sample
    nk = n_total * TOP_K
    n_steps = -(-nk // MOE_STEP_ROWS) + NUM_EXPERTS
    m_pad = n_steps * MOE_STEP_ROWS

    weights = _mixer_weights(w_in, w_pool, pool_scale, w_gate_up, b_gate, gla_norm_w, w_out,
                             ln1_g, ln1_b, w_router, b_router)

    *shared, hist_p, s_p = _mixer_prompt(x_prompt, weights, n_total)
    h_all, hb_all, logits_t, hist_s, s_s = _mixer_sample(
        x_sample, state_pool[0], state_gla[0], weights, shared, n_prompt)

    gates_t, dest_t, layout = _router(logits_t)
    pad_end = layout[:, 0].astype(jnp.int32)
    counts = layout[:, 1].astype(jnp.int32)
    pad_start = jnp.concatenate([jnp.zeros((1,), jnp.int32), pad_end[:-1]])
    step_start = jnp.arange(n_steps, dtype=jnp.int32) * MOE_STEP_ROWS
    step_expert = jnp.minimum(jnp.sum((step_start[:, None] >= pad_end[None, :]).astype(jnp.int32), axis=1),
                              NUM_EXPERTS - 1)
    rows_left = counts[step_expert] - (step_start - pad_start[step_expert])
    n_sub = jnp.clip((rows_left + MOE_ROWS - 1) // MOE_ROWS, 0, MOE_STEP_ROWS // MOE_ROWS).astype(jnp.int32)
    n_used = (pad_end[-1:] // MOE_STEP_ROWS).astype(jnp.int32)

    dest_chunks = dest_t.reshape(TOP_K, n_total // DISPATCH_ROWS, DISPATCH_ROWS).transpose(1, 0, 2)
    x_sorted = _dispatch(hb_all, dest_chunks, m_pad)
    y_sorted = _moe_experts(step_expert, n_used, n_sub, x_sorted, w_gu[0], b_gu[0][:, None, :],
                            w_down[0], b_down[0][:, None, :])
    gates = gates_t.T
    ln_g, ln_b = ln2_g[0][None, :], ln2_b[0][None, :]
    seg = n_prompt // COMBINE_SEGMENTS
    y_prompt = None
    for s in range(COMBINE_SEGMENTS):
        yk = _gather_expert_rows(y_sorted, dest_chunks, s * seg, seg)
        y_prompt = _combine(yk, gates, h_all, ln_g, ln_b, s * seg, n_prompt, s * seg, y_prompt)
    yk = _gather_expert_rows(y_sorted, dest_chunks, n_prompt, n_sample)
    y_sample = _combine(yk, gates, h_all, ln_g, ln_b, n_prompt, n_sample, 0, None)
    y_prompt = y_prompt.reshape(bp, seq, D_MODEL)
    y_sample = y_sample.reshape(bs, dec_seq, D_MODEL)
    return (y_prompt, y_sample, hist_p[None], s_p[None], hist_s[None], s_s[None])
```

```python
import functools

import jax
import jax.numpy as jnp
from jax import lax
from jax.experimental import pallas as pl
from jax.experimental.pallas import tpu as pltpu
from jax.experimental.pallas import tpu_sc as plsc

F32 = jnp.float32
BF16 = jnp.bfloat16

D_MODEL = 1024
CHUNK = 64
PAST_LEN = 1024
POOL_WIDTH = 512
POOL_WINDOWS = (2, 4, 8, 16)
POOL_GROUP = 128
POOL_HIST = 15
GLA_HEADS = 4
GLA_DK = 64
GLA_DV = 128
GATE_RANK = 16
GATE_NORMALIZER = 16.0
NUM_EXPERTS = 32
TOP_K = 4
EXPERT_FF = 1024
SWIGLU_LIMIT = 7.0
SWIGLU_ALPHA = 1.702
LN_EPS = 1e-5
RMS_EPS = 1e-6
ALPHA = 2.0 ** 0.25

Q0 = POOL_WIDTH
K0 = Q0 + GLA_HEADS * GLA_DK
V0 = K0 + GLA_HEADS * GLA_DK
R0 = V0 + GLA_HEADS * GLA_DV
N_MAIN = R0 + GLA_HEADS * GLA_DV

LANES = 128
TILE_TOKENS = 512
CHUNKS_PER_TILE = TILE_TOKENS // CHUNK
HIST_PAD = 16
OUT_CHUNKS = 4
SCAN_ROWS = 256
MOE_ROWS = 512
ROUTE_TOKENS = 1024
COMBINE_TOKENS = 512
COMBINE_SEGMENTS = 4
VMEM_LIMIT = 56 * 1024 * 1024


def _dot(a, b):
    return jnp.dot(a, b, preferred_element_type=F32)


def _dot_nt(a, b):
    return lax.dot_general(a, b, (((1,), (1,)), ((), ())), preferred_element_type=F32)


def _dot_tn(a, b):
    return lax.dot_general(a, b, (((0,), (0,)), ((), ())), preferred_element_type=F32)


HALF = D_MODEL // 2
HI_MASK = 0xFFFF0000


def _pack_bf16_pairs(xb):
    lo = lax.bitcast_convert_type(xb[:, :HALF].astype(F32), jnp.uint32) >> 16
    hi = lax.bitcast_convert_type(xb[:, HALF:].astype(F32), jnp.uint32) & jnp.uint32(HI_MASK)
    return lax.bitcast_convert_type(hi | lo, jnp.int32)


def _unpack_bf16_pairs(p):
    u = lax.bitcast_convert_type(p, jnp.uint32)
    lo = lax.bitcast_convert_type(u << 16, F32)
    hi = lax.bitcast_convert_type(u & jnp.uint32(HI_MASK), F32)
    return jnp.concatenate([lo, hi], axis=1).astype(BF16)


def _layer_norm(v, g, b):
    mu = jnp.mean(v, axis=-1, keepdims=True)
    c = v - mu
    var = jnp.mean(c * c, axis=-1, keepdims=True)
    return c * lax.rsqrt(var + LN_EPS) * g + b


N_MIXER_WEIGHTS = 12
N_MIXER_SHARED_OUT = 3


def _mixer_kernel(per_chunk_state, pos0, *refs):
    if per_chunk_state:
        (x_ref, hist_in_ref, s_in_ref, *rest) = refs
    else:
        (x_ref, *rest) = refs
        hist_in_ref = s_in_ref = None
    (w_main_ref, w_glr_ref, w_gate_ref, b_gate_ref, w_pool_ref, pscale_ref, gnorm_ref,
     w_out_ref, ln1g_ref, ln1b_ref, w_router_ref, b_router_ref, *rest) = rest
    if per_chunk_state:
        rest = rest[N_MIXER_SHARED_OUT:]
    (h_ref, hb_ref, logits_ref, hist_out_ref, s_out_ref,
     proj_scr, b_scr, o_scr, ext_scr, st_scr, tri_scr) = rest

    if per_chunk_state:
        t = None
        first_step = pl.program_id(0) == 0
    else:
        t = pl.program_id(1)
        first_step = (pl.program_id(0) == 0) & (t == 0)
    x = x_ref[...].reshape(TILE_TOKENS, D_MODEL)
    xb = x.astype(BF16)

    @pl.when(first_step)
    def _():
        ti = lax.broadcasted_iota(jnp.int32, (SCAN_ROWS, SCAN_ROWS), 0)
        tj = lax.broadcasted_iota(jnp.int32, (SCAN_ROWS, SCAN_ROWS), 1)
        same_chunk = (ti // CHUNK) == (tj // CHUNK)
        tri_scr[...] = jnp.where(same_chunk & (ti >= tj), 1.0, 0.0).astype(BF16)
        if not per_chunk_state:
            st_scr[...] = jnp.zeros_like(st_scr)
            ext_scr[0:HIST_PAD, :] = jnp.zeros((HIST_PAD, POOL_WIDTH), F32)

    glr = _dot(xb, w_glr_ref[...])
    proj_scr[:, 0:V0] = _dot(xb, w_main_ref[:, 0:V0])
    gk = _dot(glr.astype(BF16), w_gate_ref[...]) + b_gate_ref[...]
    log_sig = jnp.minimum(gk, 0.0) - jnp.log1p(jnp.exp(-jnp.abs(gk)))
    g = log_sig / GATE_NORMALIZER
    g_hi = g.astype(BF16)
    g_lo = (g - g_hi.astype(F32)).astype(BF16)
    proj_scr[:, V0:N_MAIN] = _dot(xb, w_main_ref[:, V0:N_MAIN])
    for s in range(TILE_TOKENS // SCAN_ROWS):
        rs = slice(s * SCAN_ROWS, (s + 1) * SCAN_ROWS)
        b_scr[rs, :] = _dot(tri_scr[...], g_hi[rs]) + _dot(tri_scr[...], g_lo[rs])

    if per_chunk_state:
        seg_len, seg_stride, n_seg = CHUNK, CHUNK + HIST_PAD, CHUNKS_PER_TILE
        for c in range(n_seg):
            base = c * seg_stride
            ext_scr[base:base + HIST_PAD, :] = jnp.zeros((HIST_PAD, POOL_WIDTH), F32)
            ext_scr[base + 1:base + HIST_PAD, :] = hist_in_ref[c]
            ext_scr[base + HIST_PAD:base + seg_stride, :] = proj_scr[c * CHUNK:(c + 1) * CHUNK, 0:POOL_WIDTH]
        row_pos = pos0 + lax.broadcasted_iota(jnp.int32, (seg_len, POOL_GROUP), 0)
    else:
        seg_len, seg_stride, n_seg = TILE_TOKENS, TILE_TOKENS + HIST_PAD, 1
        ext_scr[0:HIST_PAD, :] = jnp.where(t == 0, 0.0, ext_scr[0:HIST_PAD, :])
        ext_scr[HIST_PAD:seg_stride, :] = proj_scr[:, 0:POOL_WIDTH]
        row_pos = pos0 + t * TILE_TOKENS + lax.broadcasted_iota(jnp.int32, (seg_len, POOL_GROUP), 0)

    pooled_groups = []
    for gi, w in enumerate(POOL_WINDOWS):
        gs = slice(gi * POOL_GROUP, (gi + 1) * POOL_GROUP)
        cnt = jnp.minimum(row_pos + 1, w).astype(F32)
        ext = ext_scr[:, gs]
        win = ext
        shift = 1
        while shift < w:
            win = win + pltpu.roll(win, shift, 0)
            shift *= 2
        segs = []
        for s in range(n_seg):
            base = s * seg_stride + HIST_PAD
            segs.append(win[base:base + seg_len] / cnt - ext[base:base + seg_len])
        pooled = segs[0] if n_seg == 1 else jnp.concatenate(segs, axis=0)
        pooled_groups.append(pooled.astype(BF16))
    pool_cols = []
    for p in range(len(POOL_WINDOWS) // 2):
        both = jnp.concatenate(pooled_groups[2 * p:2 * p + 2], axis=1)
        pool_cols.append(_dot(both, w_pool_ref[p]))
    pool_out = jnp.concatenate(pool_cols, axis=1) * pscale_ref[...]

    if per_chunk_state:
        for c in range(n_seg):
            end = (c + 1) * seg_stride
            hist_out_ref[c] = ext_scr[end - POOL_HIST:end, :]
    else:
        hist_out_ref[...] = ext_scr[seg_stride - POOL_HIST:seg_stride, :]
        ext_scr[0:HIST_PAD, :] = ext_scr[TILE_TOKENS:seg_stride, :]

    hk = GLA_HEADS * GLA_DK
    hv = GLA_HEADS * GLA_DV
    pair_rows = 2 * CHUNK
    decay_cols = LANES // CHUNKS_PER_TILE

    def head_of(shape, dim, width):
        return lax.broadcasted_iota(jnp.int32, shape, dim) // width

    same_head_k = head_of((hk, hk), 0, CHUNK) == head_of((hk, hk), 1, GLA_DK)
    same_head_v = head_of((hk, hv), 0, CHUNK) == head_of((hk, hv), 1, GLA_DV)
    pair_half = head_of((pair_rows, hv), 0, CHUNK)
    causal = (lax.broadcasted_iota(jnp.int32, (CHUNK, hk), 0)
              >= lax.broadcasted_iota(jnp.int32, (CHUNK, hk), 1) % CHUNK)

    b_all = b_scr[...]
    b_last = [b_scr[(c + 1) * CHUNK - 1:(c + 1) * CHUNK, :] for c in range(CHUNKS_PER_TILE)]
    b_last_rows = jnp.concatenate([jnp.broadcast_to(bl, (CHUNK, hk)) for bl in b_last], axis=0)
    k_all = proj_scr[:, K0:V0]
    qt_all = (proj_scr[:, Q0:K0] * (GLA_DK ** -0.5) * jnp.exp(b_all)).astype(BF16)
    kt_all = k_all * jnp.exp(-b_all)
    kl_t = jnp.transpose(k_all * jnp.exp(b_last_rows - b_all)).astype(BF16)
    decay_t = jnp.transpose(jnp.exp(jnp.concatenate(
        [jnp.broadcast_to(bl, (decay_cols, hk)) for bl in b_last], axis=0)))

    def finish_rows(rs):
        r = proj_scr[rs, R0:N_MAIN]
        silu_r = r * (1.0 / (1.0 + jnp.exp(-r)))
        gated = []
        for h in range(GLA_HEADS):
            vs = slice(h * GLA_DV, (h + 1) * GLA_DV)
            oh = o_scr[rs, vs]
            ms = jnp.mean(oh * oh, axis=-1, keepdims=True)
            gated.append(oh * lax.rsqrt(ms + RMS_EPS) * gnorm_ref[...] * silu_r[:, vs])
        mix_in = jnp.concatenate([pool_out[rs]] + gated, axis=1).astype(BF16)
        resid = ALPHA * x[rs] + _dot(mix_in, w_out_ref[...])
        h_val = _layer_norm(resid, ln1g_ref[...], ln1b_ref[...])
        h_ref[rs, :] = h_val
        hb = h_val.astype(BF16)
        hb_ref[rs, :] = _pack_bf16_pairs(hb)
        logits_ref[:, rs] = _dot_nt(w_router_ref[...], hb) + b_router_ref[:, 0:1]

    st = None if per_chunk_state else jnp.where(t == 0, 0.0, st_scr[...])
    for c in range(CHUNKS_PER_TILE):
        rows = slice(c * CHUNK, (c + 1) * CHUNK)
        pair = slice((c // 2) * pair_rows, (c // 2 + 1) * pair_rows)
        if per_chunk_state:
            st = s_in_ref[c].reshape(hk, GLA_DV)
        qt = qt_all[rows]
        k_stack = jnp.where(same_head_k, jnp.concatenate([kt_all[rows]] * GLA_HEADS, axis=0), 0.0)
        v_stack = jnp.where(same_head_v, jnp.concatenate([proj_scr[rows, V0:R0]] * GLA_HEADS, axis=0), 0.0)
        s_stack = jnp.where(same_head_v, jnp.concatenate([st] * GLA_HEADS, axis=1), 0.0)
        att = jnp.where(causal, _dot_nt(qt, k_stack.astype(BF16)), 0.0)
        o_scr[rows, :] = (_dot(att.astype(BF16), v_stack.astype(BF16))
                          + _dot(qt, s_stack.astype(BF16)))
        v_chunk = jnp.where(pair_half == c % 2, proj_scr[pair, V0:R0], 0.0).astype(BF16)
        upd = jnp.concatenate(
            [_dot(kl_t[h * GLA_DK:(h + 1) * GLA_DK, pair], v_chunk[:, h * GLA_DV:(h + 1) * GLA_DV])
             for h in range(GLA_HEADS)], axis=0)
        st = st * decay_t[:, c * decay_cols:c * decay_cols + 1] + upd
        if per_chunk_state:
            s_out_ref[c] = st.reshape(GLA_HEADS, GLA_DK, GLA_DV)
        if (c + 1) % OUT_CHUNKS == 0:
            finish_rows(slice((c + 1 - OUT_CHUNKS) * CHUNK, (c + 1) * CHUNK))

    if not per_chunk_state:
        st_scr[...] = st
        s_out_ref[...] = st.reshape(GLA_HEADS, GLA_DK, GLA_DV)


def _const_spec(shape):
    nd = len(shape)
    return pl.BlockSpec(shape, lambda *_: (0,) * nd)


def _mixer_weight_specs():
    return [
        _const_spec((D_MODEL, N_MAIN)),
        _const_spec((D_MODEL, LANES)),
        _const_spec((LANES, GLA_HEADS * GLA_DK)),
        _const_spec((1, GLA_HEADS * GLA_DK)),
        _const_spec((len(POOL_WINDOWS) // 2, 2 * POOL_GROUP, 2 * POOL_GROUP)),
        _const_spec((1, POOL_WIDTH)),
        _const_spec((1, GLA_DV)),
        _const_spec((D_MODEL, D_MODEL)),
        _const_spec((1, D_MODEL)),
        _const_spec((1, D_MODEL)),
        _const_spec((NUM_EXPERTS, D_MODEL)),
        _const_spec((NUM_EXPERTS, LANES)),
    ]


def _mixer_weights(w_in, w_pool, pool_scale, w_gate_up, b_gate, gla_norm_w, w_out, ln1_g, ln1_b,
                   w_router, b_router):
    w_glr = jnp.zeros((D_MODEL, LANES), BF16).at[:, :GATE_RANK].set(w_in[0, :, N_MAIN:].astype(BF16))
    w_gate = jnp.zeros((LANES, GLA_HEADS * GLA_DK), BF16).at[:GATE_RANK].set(w_gate_up[0].astype(BF16))
    wp = w_pool[0].astype(BF16)
    zero = jnp.zeros((POOL_GROUP, POOL_GROUP), BF16)
    w_pool_pairs = jnp.stack([jnp.block([[wp[2 * p], zero], [zero, wp[2 * p + 1]]])
                              for p in range(len(POOL_WINDOWS) // 2)])
    weights = (
        w_in[0, :, :N_MAIN].astype(BF16), w_glr, w_gate, b_gate[0][None, :],
        w_pool_pairs, pool_scale[0][None, :], gla_norm_w[0][None, :],
        w_out[0].astype(BF16), ln1_g[0][None, :], ln1_b[0][None, :],
        w_router[0].T.astype(BF16), jnp.broadcast_to(b_router[0][:, None], (NUM_EXPERTS, LANES)),
    )
    assert len(weights) == N_MIXER_WEIGHTS
    return weights


def _mixer_scratch(per_chunk_state):
    ext_rows = (CHUNKS_PER_TILE * (CHUNK + HIST_PAD)) if per_chunk_state else (TILE_TOKENS + HIST_PAD)
    return [
        pltpu.VMEM((TILE_TOKENS, N_MAIN), F32),
        pltpu.VMEM((TILE_TOKENS, GLA_HEADS * GLA_DK), F32),
        pltpu.VMEM((TILE_TOKENS, GLA_HEADS * GLA_DV), F32),
        pltpu.VMEM((ext_rows, POOL_WIDTH), F32),
        pltpu.VMEM((GLA_HEADS * GLA_DK, GLA_DV), F32),
        pltpu.VMEM((SCAN_ROWS, SCAN_ROWS), BF16),
    ]


def _mixer_out_shapes(n_total, bsz):
    return (
        jax.ShapeDtypeStruct((n_total, D_MODEL), F32),
        jax.ShapeDtypeStruct((n_total, HALF), jnp.int32),
        jax.ShapeDtypeStruct((NUM_EXPERTS, n_total), F32),
        jax.ShapeDtypeStruct((bsz, POOL_HIST, POOL_WIDTH), F32),
        jax.ShapeDtypeStruct((bsz, GLA_HEADS, GLA_DK, GLA_DV), F32),
    )


def _mixer_prompt(x, weights, n_total):
    bsz, seq, _ = x.shape
    tiles = seq // TILE_TOKENS
    return pl.pallas_call(
        functools.partial(_mixer_kernel, False, 0),
        grid=(bsz, tiles),
        in_specs=[pl.BlockSpec((None, TILE_TOKENS, D_MODEL), lambda b, t: (b, t, 0))] + _mixer_weight_specs(),
        out_specs=(
            pl.BlockSpec((TILE_TOKENS, D_MODEL), lambda b, t: (b * tiles + t, 0)),
            pl.BlockSpec((TILE_TOKENS, HALF), lambda b, t: (b * tiles + t, 0)),
            pl.BlockSpec((NUM_EXPERTS, TILE_TOKENS), lambda b, t: (0, b * tiles + t)),
            pl.BlockSpec((None, POOL_HIST, POOL_WIDTH), lambda b, t: (b, 0, 0)),
            pl.BlockSpec((None, GLA_HEADS, GLA_DK, GLA_DV), lambda b, t: (b, 0, 0, 0)),
        ),
        out_shape=_mixer_out_shapes(n_total, bsz),
        scratch_shapes=_mixer_scratch(False),
        compiler_params=pltpu.CompilerParams(
            dimension_semantics=("arbitrary", "arbitrary"), vmem_limit_bytes=VMEM_LIMIT),
        name="mixer_prompt",
    )(x, *weights)


def _mixer_sample(x, hist, state, weights, shared, row_offset):
    bsz = x.shape[0]
    tiles = bsz // CHUNKS_PER_TILE
    tile0 = row_offset // TILE_TOKENS
    n_total = shared[0].shape[0]
    n_in = 3 + N_MIXER_WEIGHTS
    return pl.pallas_call(
        functools.partial(_mixer_kernel, True, PAST_LEN),
        grid=(tiles,),
        in_specs=[
            pl.BlockSpec((CHUNKS_PER_TILE, CHUNK, D_MODEL), lambda i: (i, 0, 0)),
            pl.BlockSpec((CHUNKS_PER_TILE, POOL_HIST, POOL_WIDTH), lambda i: (i, 0, 0)),
            pl.BlockSpec((CHUNKS_PER_TILE, GLA_HEADS, GLA_DK, GLA_DV), lambda i: (i, 0, 0, 0)),
        ] + _mixer_weight_specs() + [pl.BlockSpec(memory_space=pl.ANY)] * N_MIXER_SHARED_OUT,
        out_specs=(
            pl.BlockSpec((TILE_TOKENS, D_MODEL), lambda i: (tile0 + i, 0)),
            pl.BlockSpec((TILE_TOKENS, HALF), lambda i: (tile0 + i, 0)),
            pl.BlockSpec((NUM_EXPERTS, TILE_TOKENS), lambda i: (0, tile0 + i)),
            pl.BlockSpec((CHUNKS_PER_TILE, POOL_HIST, POOL_WIDTH), lambda i: (i, 0, 0)),
            pl.BlockSpec((CHUNKS_PER_TILE, GLA_HEADS, GLA_DK, GLA_DV), lambda i: (i, 0, 0, 0)),
        ),
        out_shape=_mixer_out_shapes(n_total, bsz),
        input_output_aliases={n_in + j: j for j in range(N_MIXER_SHARED_OUT)},
        scratch_shapes=_mixer_scratch(True),
        compiler_params=pltpu.CompilerParams(
            dimension_semantics=("arbitrary",), vmem_limit_bytes=VMEM_LIMIT),
        name="mixer_sample",
    )(x, hist, state, *weights, *shared)


def _router_kernel(lt_ref, gates_ref, dest_ref, padend_ref, cnt_scr, base_scr, pstart_scr, before_scr):
    phase = pl.program_id(0)
    i = pl.program_id(1)
    shape = (NUM_EXPERTS, ROUTE_TOKENS)
    logits = lt_ref[...]
    row = lax.broadcasted_iota(jnp.int32, shape, 0)
    sel, vals = [], []
    for _ in range(TOP_K):
        m = jnp.max(logits, axis=0, keepdims=True)
        idx = jnp.min(jnp.where(logits == m, row, NUM_EXPERTS), axis=0, keepdims=True)
        hit = row == idx
        sel.append(hit)
        vals.append(m)
        logits = jnp.where(hit, -jnp.inf, logits)
    chosen = sum(jnp.where(hit, 1.0, 0.0) for hit in sel)
    tile_counts = jnp.broadcast_to(jnp.sum(chosen, axis=1, keepdims=True), (NUM_EXPERTS, LANES))

    @pl.when(phase == 0)
    def _():
        @pl.when(i == 0)
        def _():
            cnt_scr[...] = jnp.zeros_like(cnt_scr)

        cnt_scr[...] += tile_counts

    @pl.when(phase == 1)
    def _():
        @pl.when(i == 0)
        def _():
            blocks = jnp.floor((cnt_scr[...] + (MOE_ROWS - 1)) * (1.0 / MOE_ROWS))
            erow = lax.broadcasted_iota(jnp.int32, (NUM_EXPERTS, LANES), 0)
            cum = blocks
            shift = 1
            while shift < NUM_EXPERTS:
                cum = cum + jnp.where(erow >= shift, pltpu.roll(cum, shift, 0), 0.0)
                shift *= 2
            padend_ref[...] = cum * MOE_ROWS
            pstart_scr[...] = (cum - blocks) * MOE_ROWS
            base_scr[...] = jnp.zeros_like(base_scr)
            ti = lax.broadcasted_iota(jnp.int32, (ROUTE_TOKENS, ROUTE_TOKENS), 0)
            tj = lax.broadcasted_iota(jnp.int32, (ROUTE_TOKENS, ROUTE_TOKENS), 1)
            before_scr[...] = jnp.where(ti < tj, 1.0, 0.0).astype(BF16)

        earlier = _dot(chosen.astype(BF16), before_scr[...])
        pos = pstart_scr[:, 0:1] + base_scr[:, 0:1] + earlier
        dest = [jnp.sum(jnp.where(hit, pos, 0.0), axis=0, keepdims=True) for hit in sel]
        dest_ref[...] = jnp.concatenate(dest, axis=0).astype(jnp.int32)
        ex = [jnp.exp(v - vals[0]) for v in vals]
        denom = ex[0] + ex[1] + ex[2] + ex[3]
        gates_ref[...] = jnp.concatenate([e / denom for e in ex], axis=0)
        base_scr[...] += tile_counts


def _router(logits_t):
    n = logits_t.shape[1]
    assert n % ROUTE_TOKENS == 0
    tiles = n // ROUTE_TOKENS
    return pl.pallas_call(
        _router_kernel,
        grid=(2, tiles),
        in_specs=[pl.BlockSpec((NUM_EXPERTS, ROUTE_TOKENS), lambda p, i: (0, i))],
        out_specs=(
            pl.BlockSpec((TOP_K, ROUTE_TOKENS), lambda p, i: (0, i * p)),
            pl.BlockSpec((TOP_K, ROUTE_TOKENS), lambda p, i: (0, i * p)),
            pl.BlockSpec((NUM_EXPERTS, LANES), lambda p, i: (0, 0)),
        ),
        out_shape=(
            jax.ShapeDtypeStruct((TOP_K, n), F32),
            jax.ShapeDtypeStruct((TOP_K, n), jnp.int32),
            jax.ShapeDtypeStruct((NUM_EXPERTS, LANES), F32),
        ),
        scratch_shapes=[pltpu.VMEM((NUM_EXPERTS, LANES), F32)] * 3
        + [pltpu.VMEM((ROUTE_TOKENS, ROUTE_TOKENS), BF16)],
        compiler_params=pltpu.CompilerParams(
            dimension_semantics=("arbitrary", "arbitrary"), vmem_limit_bytes=VMEM_LIMIT),
        name="router",
    )(logits_t)


SC_CORES = 2
SC_SUBCORES = 16
SC_WORKERS = SC_CORES * SC_SUBCORES
DISPATCH_ROWS = 64


def _dispatch(h_packed, dest_chunks, m_pad):
    n = h_packed.shape[0]
    n_chunks = n // DISPATCH_ROWS
    assert n_chunks % SC_WORKERS == 0
    per_worker = n_chunks // SC_WORKERS
    mesh = plsc.VectorSubcoreMesh(core_axis_name="c", subcore_axis_name="s")

    @functools.partial(
        pl.kernel, mesh=mesh,
        out_type=jax.ShapeDtypeStruct((m_pad, HALF), jnp.int32),
        scratch_types=[
            pltpu.VMEM((2, TOP_K, DISPATCH_ROWS), jnp.int32),
            pltpu.VMEM((2, DISPATCH_ROWS, HALF), jnp.int32),
            pltpu.SemaphoreType.DMA((2,)),
            pltpu.SemaphoreType.DMA((2,)),
        ],
        compiler_params=pltpu.CompilerParams(use_tc_tiling_on_sc=True),
        name="dispatch",
    )
    def dispatch_kernel(h_hbm, dest_hbm, out_hbm, idx_v, rows_v, load_sems, scatter_sems):
        wid = lax.axis_index("s") * SC_CORES + lax.axis_index("c")

        def loads(j):
            chunk = wid * per_worker + j
            slot = j % 2
            return (
                pltpu.make_async_copy(dest_hbm.at[chunk], idx_v.at[slot], load_sems.at[slot]),
                pltpu.make_async_copy(h_hbm.at[pl.ds(chunk * DISPATCH_ROWS, DISPATCH_ROWS)],
                                      rows_v.at[slot], load_sems.at[slot]),
            )

        def scatters(j):
            slot = j % 2
            return [pltpu.make_async_copy(rows_v.at[slot], out_hbm.at[idx_v.at[slot, k]],
                                          scatter_sems.at[slot]) for k in range(TOP_K)]

        for cp in loads(0):
            cp.start()
        for j in range(per_worker):
            for cp in loads(j):
                cp.wait()
            if j >= 1:
                for cp in scatters(j - 1):
                    cp.wait()
            if j + 1 < per_worker:
                for cp in loads(j + 1):
                    cp.start()
            for cp in scatters(j):
                cp.start()
        for cp in scatters(per_worker - 1):
            cp.wait()

    return dispatch_kernel(h_packed, dest_chunks)


def _gather_expert_rows(y_sorted, dest_chunks, row_offset, n):
    n_chunks = n // DISPATCH_ROWS
    assert n_chunks % SC_WORKERS == 0 and row_offset % DISPATCH_ROWS == 0
    per_worker = n_chunks // SC_WORKERS
    chunk0 = row_offset // DISPATCH_ROWS
    mesh = plsc.VectorSubcoreMesh(core_axis_name="c", subcore_axis_name="s")

    @functools.partial(
        pl.kernel, mesh=mesh,
        out_type=jax.ShapeDtypeStruct((TOP_K, n, HALF), jnp.int32),
        scratch_types=[
            pltpu.VMEM((TOP_K, DISPATCH_ROWS), jnp.int32),
            pltpu.VMEM((2, DISPATCH_ROWS, HALF), jnp.int32),
            pltpu.SemaphoreType.DMA((2,)),
        ],
        compiler_params=pltpu.CompilerParams(use_tc_tiling_on_sc=True),
        name="gather_expert_rows",
    )
    def gather_kernel(y_hbm, dest_hbm, out_hbm, idx_v, rows_v, sems):
        wid = lax.axis_index("s") * SC_CORES + lax.axis_index("c")

        def gather(k):
            return pltpu.make_async_copy(y_hbm.at[idx_v.at[k]], rows_v.at[k % 2], sems.at[k % 2])

        @pl.loop(0, per_worker)
        def _(j):
            local = wid * per_worker + j
            pltpu.sync_copy(dest_hbm.at[chunk0 + local], idx_v)
            gather(0).start()
            for k in range(TOP_K):
                if k + 1 < TOP_K:
                    gather(k + 1).start()
                gather(k).wait()
                pltpu.sync_copy(rows_v.at[k % 2],
                                out_hbm.at[k, pl.ds(local * DISPATCH_ROWS, DISPATCH_ROWS)])

    return gather_kernel(y_sorted, dest_chunks)


def _moe_kernel(be_ref, wsel_ref, nused_ref, x_ref, wgu_ref, bgu_ref, wd_ref, bd_ref, y_ref,
                wgu_bf, wd_bf):
    del wsel_ref
    i = pl.program_id(0)

    @pl.when(i < nused_ref[0])
    def _():
        @pl.when((i == 0) | (be_ref[i] != be_ref[jnp.maximum(i - 1, 0)]))
        def _():
            wgu_bf[...] = wgu_ref[...].astype(BF16)
            wd_bf[...] = wd_ref[...].astype(BF16)

        gu = _dot(_unpack_bf16_pairs(x_ref[...]), wgu_bf[...]) + bgu_ref[...]
        gate = jnp.minimum(gu[:, :EXPERT_FF], SWIGLU_LIMIT)
        up = jnp.clip(gu[:, EXPERT_FF:], -SWIGLU_LIMIT, SWIGLU_LIMIT)
        hmid = gate * (1.0 / (1.0 + jnp.exp(-SWIGLU_ALPHA * gate))) * (up + 1.0)
        y = _dot(hmid.astype(BF16), wd_bf[...]) + bd_ref[...]
        y_ref[...] = _pack_bf16_pairs(y.astype(BF16))


def _moe_experts(block_expert, weight_expert, n_used, x_sorted, w_gu, b_gu, w_down, b_down):
    m_pad = x_sorted.shape[0]
    n_blocks = m_pad // MOE_ROWS

    def blk(i, be, ws, nu):
        return jnp.minimum(i, nu[0] - 1)

    def expert(i, be, ws, nu):
        return be[blk(i, be, ws, nu)]

    def held(i, be, ws, nu):
        return ws[blk(i, be, ws, nu)]

    grid_spec = pltpu.PrefetchScalarGridSpec(
        num_scalar_prefetch=3,
        grid=(n_blocks,),
        in_specs=[
            pl.BlockSpec((MOE_ROWS, HALF), lambda *a: (blk(*a), 0)),
            pl.BlockSpec((None, D_MODEL, 2 * EXPERT_FF), lambda *a: (held(*a), 0, 0)),
            pl.BlockSpec((None, 1, 2 * EXPERT_FF), lambda *a: (expert(*a), 0, 0)),
            pl.BlockSpec((None, EXPERT_FF, D_MODEL), lambda *a: (held(*a), 0, 0)),
            pl.BlockSpec((None, 1, D_MODEL), lambda *a: (expert(*a), 0, 0)),
        ],
        out_specs=pl.BlockSpec((MOE_ROWS, HALF), lambda *a: (blk(*a), 0)),
        scratch_shapes=[
            pltpu.VMEM((D_MODEL, 2 * EXPERT_FF), BF16),
            pltpu.VMEM((EXPERT_FF, D_MODEL), BF16),
        ],
    )
    return pl.pallas_call(
        _moe_kernel,
        grid_spec=grid_spec,
        out_shape=jax.ShapeDtypeStruct((m_pad, HALF), jnp.int32),
        compiler_params=pltpu.CompilerParams(
            dimension_semantics=("arbitrary",), vmem_limit_bytes=VMEM_LIMIT),
        name="moe_experts",
    )(block_expert, weight_expert, n_used, x_sorted, w_gu, b_gu, w_down, b_down)


def _combine_kernel(yk_ref, gates_ref, h_ref, g_ref, b_ref, out_ref):
    gates = gates_ref[...]
    lo = hi = None
    for k in range(TOP_K):
        u = lax.bitcast_convert_type(yk_ref[k], jnp.uint32)
        gk = gates[:, k:k + 1]
        lo_k = lax.bitcast_convert_type(u << 16, F32) * gk
        hi_k = lax.bitcast_convert_type(u & jnp.uint32(HI_MASK), F32) * gk
        lo = lo_k if lo is None else lo + lo_k
        hi = hi_k if hi is None else hi + hi_k
    acc = ALPHA * h_ref[...] + jnp.concatenate([lo, hi], axis=1)
    out_ref[...] = _layer_norm(acc, g_ref[...], b_ref[...])


def _combine_kernel_aliased(yk_ref, gates_ref, h_ref, g_ref, b_ref, prev_ref, out_ref):
    del prev_ref
    _combine_kernel(yk_ref, gates_ref, h_ref, g_ref, b_ref, out_ref)


def _combine(yk, gates, h_all, ln_g, ln_b, row_offset, out_rows, out_offset, out_prev):
    n_seg = yk.shape[1]
    tile0 = row_offset // COMBINE_TOKENS
    out_tile0 = out_offset // COMBINE_TOKENS
    in_specs = [
        pl.BlockSpec((TOP_K, COMBINE_TOKENS, HALF), lambda i: (0, i, 0)),
        pl.BlockSpec((COMBINE_TOKENS, TOP_K), lambda i: (tile0 + i, 0)),
        pl.BlockSpec((COMBINE_TOKENS, D_MODEL), lambda i: (tile0 + i, 0)),
        _const_spec((1, D_MODEL)),
        _const_spec((1, D_MODEL)),
    ]
    args = [yk, gates, h_all, ln_g, ln_b]
    aliases = {}
    kern = _combine_kernel
    if out_prev is not None:
        in_specs.append(pl.BlockSpec(memory_space=pl.ANY))
        aliases = {len(args): 0}
        args.append(out_prev)
        kern = _combine_kernel_aliased
    return pl.pallas_call(
        kern,
        grid=(n_seg // COMBINE_TOKENS,),
        in_specs=in_specs,
        out_specs=pl.BlockSpec((COMBINE_TOKENS, D_MODEL), lambda i: (out_tile0 + i, 0)),
        out_shape=jax.ShapeDtypeStruct((out_rows, D_MODEL), F32),
        input_output_aliases=aliases,
        compiler_params=pltpu.CompilerParams(
            dimension_semantics=("arbitrary",), vmem_limit_bytes=VMEM_LIMIT),
        name="moe_combine",
    )(*args)


def kernel(x_prompt, x_sample, state_pool, state_gla, w_in, w_pool, pool_scale, w_gate_up, b_gate,
           gla_norm_w, w_out, ln1_g, ln1_b, w_router, b_router, w_gu, b_gu, w_down, b_down,
           ln2_g, ln2_b):
    assert w_in.shape[0] == 1, "single-layer kernel"
    bp, seq, _ = x_prompt.shape
    bs, dec_seq, _ = x_sample.shape
    assert dec_seq == CHUNK and seq % TILE_TOKENS == 0 and bs % CHUNKS_PER_TILE == 0
    n_prompt = bp * seq
    n_sample = bs * dec_seq
    n_total = n_prompt + n_sample
    nk = n_total * TOP_K
    n_blocks = -(-nk // MOE_ROWS) + NUM_EXPERTS
    m_pad = n_blocks * MOE_ROWS

    weights = _mixer_weights(w_in, w_pool, pool_scale, w_gate_up, b_gate, gla_norm_w, w_out,
                             ln1_g, ln1_b, w_router, b_router)

    *shared, hist_p, s_p = _mixer_prompt(x_prompt, weights, n_total)
    h_all, hb_all, logits_t, hist_s, s_s = _mixer_sample(
        x_sample, state_pool[0], state_gla[0], weights, shared, n_prompt)

    gates_t, dest_t, pad_end = _router(logits_t)
    pad_end = pad_end[:, 0].astype(jnp.int32)
    block_start = jnp.arange(n_blocks, dtype=jnp.int32) * MOE_ROWS
    block_expert = jnp.minimum(jnp.sum((block_start[:, None] >= pad_end[None, :]).astype(jnp.int32), axis=1),
                               NUM_EXPERTS - 1)
    n_used = (pad_end[-1:] // MOE_ROWS).astype(jnp.int32)
    is_first = jnp.concatenate([jnp.ones((1,), bool), block_expert[1:] != block_expert[:-1]])
    blocks = jnp.arange(n_blocks, dtype=jnp.int32)
    later_other = ((blocks[None, :] > blocks[:, None]) & (blocks[None, :] < n_used[0])
                   & (block_expert[None, :] != block_expert[:, None]))
    next_expert = jnp.min(jnp.where(later_other, block_expert[None, :], NUM_EXPERTS), axis=1)
    next_expert = jnp.where(next_expert == NUM_EXPERTS, block_expert, next_expert)
    weight_expert = jnp.where(is_first, block_expert, next_expert).astype(jnp.int32)

    dest_chunks = dest_t.reshape(TOP_K, n_total // DISPATCH_ROWS, DISPATCH_ROWS).transpose(1, 0, 2)
    x_sorted = _dispatch(hb_all, dest_chunks, m_pad)
    y_sorted = _moe_experts(block_expert, weight_expert, n_used, x_sorted, w_gu[0], b_gu[0][:, None, :],
                            w_down[0], b_down[0][:, None, :])
    gates = gates_t.T
    ln_g, ln_b = ln2_g[0][None, :], ln2_b[0][None, :]
    seg = n_prompt // COMBINE_SEGMENTS
    y_prompt = None
    for s in range(COMBINE_SEGMENTS):
        yk = _gather_expert_rows(y_sorted, dest_chunks, s * seg, seg)
        y_prompt = _combine(yk, gates, h_all, ln_g, ln_b, s * seg, n_prompt, s * seg, y_prompt)
    yk = _gather_expert_rows(y_sorted, dest_chunks, n_prompt, n_sample)
    y_sample = _combine(yk, gates, h_all, ln_g, ln_b, n_prompt, n_sample, 0, None)
    y_prompt = y_prompt.reshape(bp, seq, D_MODEL)
    y_sample = y_sample.reshape(bs, dec_seq, D_MODEL)
    return (y_prompt, y_sample, hist_p[None], s_p[None], hist_s[None], s_s[None])
```

```python
import functools

import jax
import jax.numpy as jnp
from jax import lax
from jax.experimental import pallas as pl
from jax.experimental.pallas import tpu as pltpu
from jax.experimental.pallas import tpu_sc as plsc

F32 = jnp.float32
BF16 = jnp.bfloat16

D_MODEL = 1024
CHUNK = 64
PAST_LEN = 1024
POOL_WIDTH = 512
POOL_WINDOWS = (2, 4, 8, 16)
POOL_GROUP = 128
POOL_HIST = 15
GLA_HEADS = 4
GLA_DK = 64
GLA_DV = 128
GATE_RANK = 16
GATE_NORMALIZER = 16.0
NUM_EXPERTS = 32
TOP_K = 4
EXPERT_FF = 1024
SWIGLU_LIMIT = 7.0
SWIGLU_ALPHA = 1.702
LN_EPS = 1e-5
RMS_EPS = 1e-6
ALPHA = 2.0 ** 0.25

Q0 = POOL_WIDTH
K0 = Q0 + GLA_HEADS * GLA_DK
V0 = K0 + GLA_HEADS * GLA_DK
R0 = V0 + GLA_HEADS * GLA_DV
N_MAIN = R0 + GLA_HEADS * GLA_DV
N_IN = N_MAIN + GATE_RANK

LANES = 128
TILE_TOKENS = 512
CHUNKS_PER_TILE = TILE_TOKENS // CHUNK
HIST_PAD = 16
OUT_CHUNKS = 4
SCAN_ROWS = 256
MOE_ROWS = 512
ROUTE_TOKENS = 1024
COMBINE_TOKENS = 512
COMBINE_SEGMENTS = 4
VMEM_LIMIT = 56 * 1024 * 1024


def _dot(a, b):
    return jnp.dot(a, b, preferred_element_type=F32)


def _dot_nt(a, b):
    return lax.dot_general(a, b, (((1,), (1,)), ((), ())), preferred_element_type=F32)


def _dot_tn(a, b):
    return lax.dot_general(a, b, (((0,), (0,)), ((), ())), preferred_element_type=F32)


HALF = D_MODEL // 2
HI_MASK = 0xFFFF0000


def _pack_bf16_pairs(xb):
    lo = lax.bitcast_convert_type(xb[:, :HALF].astype(F32), jnp.uint32) >> 16
    hi = lax.bitcast_convert_type(xb[:, HALF:].astype(F32), jnp.uint32) & jnp.uint32(HI_MASK)
    return lax.bitcast_convert_type(hi | lo, jnp.int32)


def _unpack_bf16_pairs(p):
    u = lax.bitcast_convert_type(p, jnp.uint32)
    lo = lax.bitcast_convert_type(u << 16, F32)
    hi = lax.bitcast_convert_type(u & jnp.uint32(HI_MASK), F32)
    return jnp.concatenate([lo, hi], axis=1).astype(BF16)


def _layer_norm(v, g, b):
    mu = jnp.mean(v, axis=-1, keepdims=True)
    c = v - mu
    var = jnp.mean(c * c, axis=-1, keepdims=True)
    return c * lax.rsqrt(var + LN_EPS) * g + b


N_MIXER_WEIGHTS = 12
N_MIXER_SHARED_OUT = 3


def _mixer_kernel(per_chunk_state, pos0, *refs):
    if per_chunk_state:
        (x_ref, hist_in_ref, s_in_ref, *rest) = refs
    else:
        (x_ref, *rest) = refs
        hist_in_ref = s_in_ref = None
    (w_main_ref, w_glr_ref, w_gate_ref, b_gate_ref, w_pool_ref, pscale_ref, gnorm_ref,
     w_out_ref, ln1g_ref, ln1b_ref, w_router_ref, b_router_ref, *rest) = rest
    if per_chunk_state:
        rest = rest[N_MIXER_SHARED_OUT:]
    (h_ref, hb_ref, logits_ref, hist_out_ref, s_out_ref,
     proj_scr, b_scr, o_scr, ext_scr, st_scr, tri_scr, w_main_bf, w_out_bf) = rest

    if per_chunk_state:
        t = None
        first_step = pl.program_id(0) == 0
    else:
        t = pl.program_id(1)
        first_step = (pl.program_id(0) == 0) & (t == 0)
    x = x_ref[...].reshape(TILE_TOKENS, D_MODEL)
    xb = x.astype(BF16)

    @pl.when(first_step)
    def _():
        ti = lax.broadcasted_iota(jnp.int32, (SCAN_ROWS, SCAN_ROWS), 0)
        tj = lax.broadcasted_iota(jnp.int32, (SCAN_ROWS, SCAN_ROWS), 1)
        same_chunk = (ti // CHUNK) == (tj // CHUNK)
        tri_scr[...] = jnp.where(same_chunk & (ti >= tj), 1.0, 0.0).astype(BF16)
        w_main_bf[...] = w_main_ref[:, 0:N_MAIN].astype(BF16)
        w_out_bf[...] = w_out_ref[...].astype(BF16)
        if not per_chunk_state:
            st_scr[...] = jnp.zeros_like(st_scr)
            ext_scr[0:HIST_PAD, :] = jnp.zeros((HIST_PAD, POOL_WIDTH), F32)

    glr = _dot(xb, w_glr_ref[...])
    proj_scr[:, 0:V0] = _dot(xb, w_main_bf[:, 0:V0])
    gk = _dot(glr.astype(BF16), w_gate_ref[...]) + b_gate_ref[...]
    log_sig = jnp.minimum(gk, 0.0) - jnp.log1p(jnp.exp(-jnp.abs(gk)))
    g = log_sig / GATE_NORMALIZER
    g_hi = g.astype(BF16)
    g_lo = (g - g_hi.astype(F32)).astype(BF16)
    proj_scr[:, V0:N_MAIN] = _dot(xb, w_main_bf[:, V0:N_MAIN])
    for s in range(TILE_TOKENS // SCAN_ROWS):
        rs = slice(s * SCAN_ROWS, (s + 1) * SCAN_ROWS)
        b_scr[rs, :] = _dot(tri_scr[...], g_hi[rs]) + _dot(tri_scr[...], g_lo[rs])

    if per_chunk_state:
        seg_len, seg_stride, n_seg = CHUNK, CHUNK + HIST_PAD, CHUNKS_PER_TILE
        for c in range(n_seg):
            base = c * seg_stride
            ext_scr[base:base + HIST_PAD, :] = jnp.zeros((HIST_PAD, POOL_WIDTH), F32)
            ext_scr[base + 1:base + HIST_PAD, :] = hist_in_ref[c]
            ext_scr[base + HIST_PAD:base + seg_stride, :] = proj_scr[c * CHUNK:(c + 1) * CHUNK, 0:POOL_WIDTH]
        row_pos = pos0 + lax.broadcasted_iota(jnp.int32, (seg_len, POOL_GROUP), 0)
    else:
        seg_len, seg_stride, n_seg = TILE_TOKENS, TILE_TOKENS + HIST_PAD, 1
        ext_scr[0:HIST_PAD, :] = jnp.where(t == 0, 0.0, ext_scr[0:HIST_PAD, :])
        ext_scr[HIST_PAD:seg_stride, :] = proj_scr[:, 0:POOL_WIDTH]
        row_pos = pos0 + t * TILE_TOKENS + lax.broadcasted_iota(jnp.int32, (seg_len, POOL_GROUP), 0)

    pooled_groups = []
    for gi, w in enumerate(POOL_WINDOWS):
        gs = slice(gi * POOL_GROUP, (gi + 1) * POOL_GROUP)
        cnt = jnp.minimum(row_pos + 1, w).astype(F32)
        ext = ext_scr[:, gs]
        win = ext
        shift = 1
        while shift < w:
            win = win + pltpu.roll(win, shift, 0)
            shift *= 2
        segs = []
        for s in range(n_seg):
            base = s * seg_stride + HIST_PAD
            segs.append(win[base:base + seg_len] / cnt - ext[base:base + seg_len])
        pooled = segs[0] if n_seg == 1 else jnp.concatenate(segs, axis=0)
        pooled_groups.append(pooled.astype(BF16))
    pool_cols = []
    for p in range(len(POOL_WINDOWS) // 2):
        both = jnp.concatenate(pooled_groups[2 * p:2 * p + 2], axis=1)
        pool_cols.append(_dot(both, w_pool_ref[p]))
    pool_out = jnp.concatenate(pool_cols, axis=1) * pscale_ref[...]

    if per_chunk_state:
        for c in range(n_seg):
            end = (c + 1) * seg_stride
            hist_out_ref[c] = ext_scr[end - POOL_HIST:end, :]
    else:
        hist_out_ref[...] = ext_scr[seg_stride - POOL_HIST:seg_stride, :]
        ext_scr[0:HIST_PAD, :] = ext_scr[TILE_TOKENS:seg_stride, :]

    hk = GLA_HEADS * GLA_DK
    hv = GLA_HEADS * GLA_DV
    pair_rows = 2 * CHUNK
    decay_cols = LANES // CHUNKS_PER_TILE

    def head_of(shape, dim, width):
        return lax.broadcasted_iota(jnp.int32, shape, dim) // width

    same_head_k = head_of((hk, hk), 0, CHUNK) == head_of((hk, hk), 1, GLA_DK)
    same_head_v = head_of((hk, hv), 0, CHUNK) == head_of((hk, hv), 1, GLA_DV)
    pair_half = head_of((pair_rows, hv), 0, CHUNK)
    causal = (lax.broadcasted_iota(jnp.int32, (CHUNK, hk), 0)
              >= lax.broadcasted_iota(jnp.int32, (CHUNK, hk), 1) % CHUNK)

    b_all = b_scr[...]
    b_last = [b_scr[(c + 1) * CHUNK - 1:(c + 1) * CHUNK, :] for c in range(CHUNKS_PER_TILE)]
    b_last_rows = jnp.concatenate([jnp.broadcast_to(bl, (CHUNK, hk)) for bl in b_last], axis=0)
    k_all = proj_scr[:, K0:V0]
    qt_all = (proj_scr[:, Q0:K0] * (GLA_DK ** -0.5) * jnp.exp(b_all)).astype(BF16)
    kt_all = k_all * jnp.exp(-b_all)
    kl_t = jnp.transpose(k_all * jnp.exp(b_last_rows - b_all)).astype(BF16)
    decay_t = jnp.transpose(jnp.exp(jnp.concatenate(
        [jnp.broadcast_to(bl, (decay_cols, hk)) for bl in b_last], axis=0)))

    def finish_rows(rs):
        r = proj_scr[rs, R0:N_MAIN]
        silu_r = r * (1.0 / (1.0 + jnp.exp(-r)))
        gated = []
        for h in range(GLA_HEADS):
            vs = slice(h * GLA_DV, (h + 1) * GLA_DV)
            oh = o_scr[rs, vs]
            ms = jnp.mean(oh * oh, axis=-1, keepdims=True)
            gated.append(oh * lax.rsqrt(ms + RMS_EPS) * gnorm_ref[...] * silu_r[:, vs])
        mix_in = jnp.concatenate([pool_out[rs]] + gated, axis=1).astype(BF16)
        resid = ALPHA * x[rs] + _dot(mix_in, w_out_bf[...])
        h_val = _layer_norm(resid, ln1g_ref[...], ln1b_ref[...])
        h_ref[rs, :] = h_val
        hb = h_val.astype(BF16)
        hb_ref[rs, :] = _pack_bf16_pairs(hb)
        logits_ref[:, rs] = _dot_nt(w_router_ref[...], hb) + b_router_ref[:, 0:1]

    st = None if per_chunk_state else jnp.where(t == 0, 0.0, st_scr[...])
    for c in range(CHUNKS_PER_TILE):
        rows = slice(c * CHUNK, (c + 1) * CHUNK)
        pair = slice((c // 2) * pair_rows, (c // 2 + 1) * pair_rows)
        if per_chunk_state:
            st = s_in_ref[c].reshape(hk, GLA_DV)
        qt = qt_all[rows]
        zero = jnp.zeros((), BF16)
        k_stack = jnp.where(same_head_k, jnp.concatenate([kt_all[rows].astype(BF16)] * GLA_HEADS, axis=0), zero)
        v_stack = jnp.where(same_head_v, jnp.concatenate(
            [proj_scr[rows, V0:R0].astype(BF16)] * GLA_HEADS, axis=0), zero)
        s_stack = jnp.where(same_head_v, jnp.concatenate([st.astype(BF16)] * GLA_HEADS, axis=1), zero)
        att = jnp.where(causal, _dot_nt(qt, k_stack), 0.0)
        o_scr[rows, :] = _dot(att.astype(BF16), v_stack) + _dot(qt, s_stack)
        v_chunk = jnp.where(pair_half == c % 2, proj_scr[pair, V0:R0], 0.0).astype(BF16)
        upd = jnp.concatenate(
            [_dot(kl_t[h * GLA_DK:(h + 1) * GLA_DK, pair], v_chunk[:, h * GLA_DV:(h + 1) * GLA_DV])
             for h in range(GLA_HEADS)], axis=0)
        st = st * decay_t[:, c * decay_cols:c * decay_cols + 1] + upd
        if per_chunk_state:
            s_out_ref[c] = st.reshape(GLA_HEADS, GLA_DK, GLA_DV)
        if (c + 1) % OUT_CHUNKS == 0:
            finish_rows(slice((c + 1 - OUT_CHUNKS) * CHUNK, (c + 1) * CHUNK))

    if not per_chunk_state:
        st_scr[...] = st
        s_out_ref[...] = st.reshape(GLA_HEADS, GLA_DK, GLA_DV)


def _const_spec(shape, single_buffer=False):
    nd = len(shape)
    if single_buffer:
        return pl.BlockSpec(shape, lambda *_: (0,) * nd, pipeline_mode=pl.Buffered(1))
    return pl.BlockSpec(shape, lambda *_: (0,) * nd)


def _mixer_weight_specs():
    return [
        _const_spec((None, D_MODEL, N_IN), single_buffer=True),
        _const_spec((D_MODEL, LANES)),
        _const_spec((LANES, GLA_HEADS * GLA_DK)),
        _const_spec((1, GLA_HEADS * GLA_DK)),
        _const_spec((len(POOL_WINDOWS) // 2, 2 * POOL_GROUP, 2 * POOL_GROUP)),
        _const_spec((1, POOL_WIDTH)),
        _const_spec((1, GLA_DV)),
        _const_spec((None, D_MODEL, D_MODEL), single_buffer=True),
        _const_spec((1, D_MODEL)),
        _const_spec((1, D_MODEL)),
        _const_spec((NUM_EXPERTS, D_MODEL)),
        _const_spec((NUM_EXPERTS, LANES)),
    ]


def _mixer_weights(w_in, w_pool, pool_scale, w_gate_up, b_gate, gla_norm_w, w_out, ln1_g, ln1_b,
                   w_router, b_router):
    w_glr = jnp.zeros((D_MODEL, LANES), BF16).at[:, :GATE_RANK].set(w_in[0, :, N_MAIN:].astype(BF16))
    w_gate = jnp.zeros((LANES, GLA_HEADS * GLA_DK), BF16).at[:GATE_RANK].set(w_gate_up[0].astype(BF16))
    wp = w_pool[0].astype(BF16)
    zero = jnp.zeros((POOL_GROUP, POOL_GROUP), BF16)
    w_pool_pairs = jnp.stack([jnp.block([[wp[2 * p], zero], [zero, wp[2 * p + 1]]])
                              for p in range(len(POOL_WINDOWS) // 2)])
    weights = (
        w_in, w_glr, w_gate, b_gate[0][None, :],
        w_pool_pairs, pool_scale[0][None, :], gla_norm_w[0][None, :],
        w_out, ln1_g[0][None, :], ln1_b[0][None, :],
        w_router[0].T.astype(BF16), jnp.broadcast_to(b_router[0][:, None], (NUM_EXPERTS, LANES)),
    )
    assert len(weights) == N_MIXER_WEIGHTS
    return weights


def _mixer_scratch(per_chunk_state):
    ext_rows = (CHUNKS_PER_TILE * (CHUNK + HIST_PAD)) if per_chunk_state else (TILE_TOKENS + HIST_PAD)
    return [
        pltpu.VMEM((TILE_TOKENS, N_MAIN), F32),
        pltpu.VMEM((TILE_TOKENS, GLA_HEADS * GLA_DK), F32),
        pltpu.VMEM((TILE_TOKENS, GLA_HEADS * GLA_DV), F32),
        pltpu.VMEM((ext_rows, POOL_WIDTH), F32),
        pltpu.VMEM((GLA_HEADS * GLA_DK, GLA_DV), F32),
        pltpu.VMEM((SCAN_ROWS, SCAN_ROWS), BF16),
        pltpu.VMEM((D_MODEL, N_MAIN), BF16),
        pltpu.VMEM((D_MODEL, D_MODEL), BF16),
    ]


def _mixer_out_shapes(n_total, bsz):
    return (
        jax.ShapeDtypeStruct((n_total, D_MODEL), F32),
        jax.ShapeDtypeStruct((n_total, HALF), jnp.int32),
        jax.ShapeDtypeStruct((NUM_EXPERTS, n_total), F32),
        jax.ShapeDtypeStruct((bsz, POOL_HIST, POOL_WIDTH), F32),
        jax.ShapeDtypeStruct((bsz, GLA_HEADS, GLA_DK, GLA_DV), F32),
    )


def _mixer_prompt(x, weights, n_total):
    bsz, seq, _ = x.shape
    tiles = seq // TILE_TOKENS
    return pl.pallas_call(
        functools.partial(_mixer_kernel, False, 0),
        grid=(bsz, tiles),
        in_specs=[pl.BlockSpec((None, TILE_TOKENS, D_MODEL), lambda b, t: (b, t, 0))] + _mixer_weight_specs(),
        out_specs=(
            pl.BlockSpec((TILE_TOKENS, D_MODEL), lambda b, t: (b * tiles + t, 0)),
            pl.BlockSpec((TILE_TOKENS, HALF), lambda b, t: (b * tiles + t, 0)),
            pl.BlockSpec((NUM_EXPERTS, TILE_TOKENS), lambda b, t: (0, b * tiles + t)),
            pl.BlockSpec((None, POOL_HIST, POOL_WIDTH), lambda b, t: (b, 0, 0)),
            pl.BlockSpec((None, GLA_HEADS, GLA_DK, GLA_DV), lambda b, t: (b, 0, 0, 0)),
        ),
        out_shape=_mixer_out_shapes(n_total, bsz),
        scratch_shapes=_mixer_scratch(False),
        compiler_params=pltpu.CompilerParams(
            dimension_semantics=("arbitrary", "arbitrary"), vmem_limit_bytes=VMEM_LIMIT),
        name="mixer_prompt",
    )(x, *weights)


def _mixer_sample(x, hist, state, weights, shared, row_offset):
    bsz = x.shape[0]
    tiles = bsz // CHUNKS_PER_TILE
    tile0 = row_offset // TILE_TOKENS
    n_total = shared[0].shape[0]
    n_in = 3 + N_MIXER_WEIGHTS
    return pl.pallas_call(
        functools.partial(_mixer_kernel, True, PAST_LEN),
        grid=(tiles,),
        in_specs=[
            pl.BlockSpec((CHUNKS_PER_TILE, CHUNK, D_MODEL), lambda i: (i, 0, 0)),
            pl.BlockSpec((CHUNKS_PER_TILE, POOL_HIST, POOL_WIDTH), lambda i: (i, 0, 0)),
            pl.BlockSpec((CHUNKS_PER_TILE, GLA_HEADS, GLA_DK, GLA_DV), lambda i: (i, 0, 0, 0)),
        ] + _mixer_weight_specs() + [pl.BlockSpec(memory_space=pl.ANY)] * N_MIXER_SHARED_OUT,
        out_specs=(
            pl.BlockSpec((TILE_TOKENS, D_MODEL), lambda i: (tile0 + i, 0)),
            pl.BlockSpec((TILE_TOKENS, HALF), lambda i: (tile0 + i, 0)),
            pl.BlockSpec((NUM_EXPERTS, TILE_TOKENS), lambda i: (0, tile0 + i)),
            pl.BlockSpec((CHUNKS_PER_TILE, POOL_HIST, POOL_WIDTH), lambda i: (i, 0, 0)),
            pl.BlockSpec((CHUNKS_PER_TILE, GLA_HEADS, GLA_DK, GLA_DV), lambda i: (i, 0, 0, 0)),
        ),
        out_shape=_mixer_out_shapes(n_total, bsz),
        input_output_aliases={n_in + j: j for j in range(N_MIXER_SHARED_OUT)},
        scratch_shapes=_mixer_scratch(True),
        compiler_params=pltpu.CompilerParams(
            dimension_semantics=("arbitrary",), vmem_limit_bytes=VMEM_LIMIT),
        name="mixer_sample",
    )(x, hist, state, *weights, *shared)


def _router_kernel(lt_ref, gates_ref, dest_ref, padend_ref, cnt_scr, base_scr, pstart_scr, before_scr,
                   topk_scr):
    phase = pl.program_id(0)
    i = pl.program_id(1)
    shape = (NUM_EXPERTS, ROUTE_TOKENS)
    row = lax.broadcasted_iota(jnp.int32, shape, 0)

    def tile_counts_of(chosen):
        return jnp.broadcast_to(jnp.sum(chosen, axis=1, keepdims=True), (NUM_EXPERTS, LANES))

    @pl.when(phase == 0)
    def _():
        @pl.when(i == 0)
        def _():
            cnt_scr[...] = jnp.zeros_like(cnt_scr)

        logits = lt_ref[...]
        idxs, vals = [], []
        chosen = jnp.zeros(shape, F32)
        for _ in range(TOP_K):
            m = jnp.max(logits, axis=0, keepdims=True)
            idx = jnp.min(jnp.where(logits == m, row, NUM_EXPERTS), axis=0, keepdims=True)
            hit = row == idx
            idxs.append(idx.astype(F32))
            vals.append(m)
            chosen = chosen + jnp.where(hit, 1.0, 0.0)
            logits = jnp.where(hit, -jnp.inf, logits)
        topk_scr[i] = jnp.concatenate(idxs + vals, axis=0)
        cnt_scr[...] += tile_counts_of(chosen)

    @pl.when(phase == 1)
    def _():
        @pl.when(i == 0)
        def _():
            blocks = jnp.floor((cnt_scr[...] + (MOE_ROWS - 1)) * (1.0 / MOE_ROWS))
            erow = lax.broadcasted_iota(jnp.int32, (NUM_EXPERTS, LANES), 0)
            lane = lax.broadcasted_iota(jnp.int32, (NUM_EXPERTS, LANES), 1)
            cum = blocks
            shift = 1
            while shift < NUM_EXPERTS:
                cum = cum + jnp.where(erow >= shift, pltpu.roll(cum, shift, 0), 0.0)
                shift *= 2
            padend_ref[...] = jnp.where(lane == 1, cnt_scr[...], cum * MOE_ROWS)
            pstart_scr[...] = (cum - blocks) * MOE_ROWS
            base_scr[...] = jnp.zeros_like(base_scr)
            ti = lax.broadcasted_iota(jnp.int32, (ROUTE_TOKENS, ROUTE_TOKENS), 0)
            tj = lax.broadcasted_iota(jnp.int32, (ROUTE_TOKENS, ROUTE_TOKENS), 1)
            before_scr[...] = jnp.where(ti < tj, 1.0, 0.0).astype(BF16)

        topk = topk_scr[i]
        sel = [row == topk[k:k + 1, :].astype(jnp.int32) for k in range(TOP_K)]
        vals = [topk[TOP_K + k:TOP_K + k + 1, :] for k in range(TOP_K)]
        chosen = sum(jnp.where(hit, 1.0, 0.0) for hit in sel)
        earlier = _dot(chosen.astype(BF16), before_scr[...])
        pos = pstart_scr[:, 0:1] + base_scr[:, 0:1] + earlier
        dest = [jnp.sum(jnp.where(hit, pos, 0.0), axis=0, keepdims=True) for hit in sel]
        dest_ref[...] = jnp.concatenate(dest, axis=0).astype(jnp.int32)
        ex = [jnp.exp(v - vals[0]) for v in vals]
        denom = ex[0] + ex[1] + ex[2] + ex[3]
        gates_ref[...] = jnp.concatenate([e / denom for e in ex], axis=0)
        base_scr[...] += tile_counts_of(chosen)


def _router(logits_t):
    n = logits_t.shape[1]
    assert n % ROUTE_TOKENS == 0
    tiles = n // ROUTE_TOKENS
    return pl.pallas_call(
        _router_kernel,
        grid=(2, tiles),
        in_specs=[pl.BlockSpec((NUM_EXPERTS, ROUTE_TOKENS), lambda p, i: (0, i))],
        out_specs=(
            pl.BlockSpec((TOP_K, ROUTE_TOKENS), lambda p, i: (0, i * p)),
            pl.BlockSpec((TOP_K, ROUTE_TOKENS), lambda p, i: (0, i * p)),
            pl.BlockSpec((NUM_EXPERTS, LANES), lambda p, i: (0, 0)),
        ),
        out_shape=(
            jax.ShapeDtypeStruct((TOP_K, n), F32),
            jax.ShapeDtypeStruct((TOP_K, n), jnp.int32),
            jax.ShapeDtypeStruct((NUM_EXPERTS, LANES), F32),
        ),
        scratch_shapes=[pltpu.VMEM((NUM_EXPERTS, LANES), F32)] * 3
        + [pltpu.VMEM((ROUTE_TOKENS, ROUTE_TOKENS), BF16),
           pltpu.VMEM((tiles, 2 * TOP_K, ROUTE_TOKENS), F32)],
        compiler_params=pltpu.CompilerParams(
            dimension_semantics=("arbitrary", "arbitrary"), vmem_limit_bytes=VMEM_LIMIT),
        name="router",
    )(logits_t)


SC_CORES = 2
SC_SUBCORES = 16
SC_WORKERS = SC_CORES * SC_SUBCORES
DISPATCH_ROWS = 64


def _dispatch(h_packed, dest_chunks, m_pad):
    n = h_packed.shape[0]
    n_chunks = n // DISPATCH_ROWS
    assert n_chunks % SC_WORKERS == 0
    per_worker = n_chunks // SC_WORKERS
    mesh = plsc.VectorSubcoreMesh(core_axis_name="c", subcore_axis_name="s")

    @functools.partial(
        pl.kernel, mesh=mesh,
        out_type=jax.ShapeDtypeStruct((m_pad, HALF), jnp.int32),
        scratch_types=[
            pltpu.VMEM((2, TOP_K, DISPATCH_ROWS), jnp.int32),
            pltpu.VMEM((2, DISPATCH_ROWS, HALF), jnp.int32),
            pltpu.SemaphoreType.DMA((2,)),
            pltpu.SemaphoreType.DMA((2,)),
        ],
        compiler_params=pltpu.CompilerParams(use_tc_tiling_on_sc=True),
        name="dispatch",
    )
    def dispatch_kernel(h_hbm, dest_hbm, out_hbm, idx_v, rows_v, load_sems, scatter_sems):
        wid = lax.axis_index("s") * SC_CORES + lax.axis_index("c")

        def loads(j):
            chunk = wid * per_worker + j
            slot = j % 2
            return (
                pltpu.make_async_copy(dest_hbm.at[chunk], idx_v.at[slot], load_sems.at[slot]),
                pltpu.make_async_copy(h_hbm.at[pl.ds(chunk * DISPATCH_ROWS, DISPATCH_ROWS)],
                                      rows_v.at[slot], load_sems.at[slot]),
            )

        def scatters(j):
            slot = j % 2
            return [pltpu.make_async_copy(rows_v.at[slot], out_hbm.at[idx_v.at[slot, k]],
                                          scatter_sems.at[slot]) for k in range(TOP_K)]

        for cp in loads(0):
            cp.start()
        for j in range(per_worker):
            for cp in loads(j):
                cp.wait()
            if j >= 1:
                for cp in scatters(j - 1):
                    cp.wait()
            if j + 1 < per_worker:
                for cp in loads(j + 1):
                    cp.start()
            for cp in scatters(j):
                cp.start()
        for cp in scatters(per_worker - 1):
            cp.wait()

    return dispatch_kernel(h_packed, dest_chunks)


def _gather_expert_rows(y_sorted, dest_chunks, row_offset, n):
    n_chunks = n // DISPATCH_ROWS
    assert n_chunks % SC_WORKERS == 0 and row_offset % DISPATCH_ROWS == 0
    per_worker = n_chunks // SC_WORKERS
    chunk0 = row_offset // DISPATCH_ROWS
    mesh = plsc.VectorSubcoreMesh(core_axis_name="c", subcore_axis_name="s")

    @functools.partial(
        pl.kernel, mesh=mesh,
        out_type=jax.ShapeDtypeStruct((TOP_K, n, HALF), jnp.int32),
        scratch_types=[
            pltpu.VMEM((TOP_K, DISPATCH_ROWS), jnp.int32),
            pltpu.VMEM((2, DISPATCH_ROWS, HALF), jnp.int32),
            pltpu.SemaphoreType.DMA((2,)),
        ],
        compiler_params=pltpu.CompilerParams(use_tc_tiling_on_sc=True),
        name="gather_expert_rows",
    )
    def gather_kernel(y_hbm, dest_hbm, out_hbm, idx_v, rows_v, sems):
        wid = lax.axis_index("s") * SC_CORES + lax.axis_index("c")

        def gather(k):
            return pltpu.make_async_copy(y_hbm.at[idx_v.at[k]], rows_v.at[k % 2], sems.at[k % 2])

        @pl.loop(0, per_worker)
        def _(j):
            local = wid * per_worker + j
            pltpu.sync_copy(dest_hbm.at[chunk0 + local], idx_v)
            gather(0).start()
            for k in range(TOP_K):
                if k + 1 < TOP_K:
                    gather(k + 1).start()
                gather(k).wait()
                pltpu.sync_copy(rows_v.at[k % 2],
                                out_hbm.at[k, pl.ds(local * DISPATCH_ROWS, DISPATCH_ROWS)])

    return gather_kernel(y_sorted, dest_chunks)


def _moe_kernel(be_ref, wsel_ref, half_ref, nused_ref, x_ref, wgu_ref, bgu_ref, wd_ref, bd_ref, y_ref,
                wgu_bf, wd_bf):
    del wsel_ref
    i = pl.program_id(0)

    def ffn(rows):
        gu = _dot(_unpack_bf16_pairs(x_ref[rows, :]), wgu_bf[...]) + bgu_ref[...]
        gate = jnp.minimum(gu[:, :EXPERT_FF], SWIGLU_LIMIT)
        up = jnp.clip(gu[:, EXPERT_FF:], -SWIGLU_LIMIT, SWIGLU_LIMIT)
        hmid = gate * (1.0 / (1.0 + jnp.exp(-SWIGLU_ALPHA * gate))) * (up + 1.0)
        y = _dot(hmid.astype(BF16), wd_bf[...]) + bd_ref[...]
        y_ref[rows, :] = _pack_bf16_pairs(y.astype(BF16))

    @pl.when(i < nused_ref[0])
    def _():
        @pl.when((i == 0) | (be_ref[i] != be_ref[jnp.maximum(i - 1, 0)]))
        def _():
            wgu_bf[...] = wgu_ref[...].astype(BF16)
            wd_bf[...] = wd_ref[...].astype(BF16)

        @pl.when(half_ref[i] == 0)
        def _():
            ffn(slice(0, MOE_ROWS))

        @pl.when(half_ref[i] != 0)
        def _():
            ffn(slice(0, MOE_ROWS // 2))


def _moe_experts(block_expert, weight_expert, half_block, n_used, x_sorted, w_gu, b_gu, w_down, b_down):
    m_pad = x_sorted.shape[0]
    n_blocks = m_pad // MOE_ROWS

    def blk(i, be, ws, hb, nu):
        return jnp.minimum(i, nu[0] - 1)

    def expert(i, be, ws, hb, nu):
        return be[blk(i, be, ws, hb, nu)]

    def held(i, be, ws, hb, nu):
        return ws[blk(i, be, ws, hb, nu)]

    grid_spec = pltpu.PrefetchScalarGridSpec(
        num_scalar_prefetch=4,
        grid=(n_blocks,),
        in_specs=[
            pl.BlockSpec((MOE_ROWS, HALF), lambda *a: (blk(*a), 0)),
            pl.BlockSpec((None, D_MODEL, 2 * EXPERT_FF), lambda *a: (held(*a), 0, 0)),
            pl.BlockSpec((None, 1, 2 * EXPERT_FF), lambda *a: (expert(*a), 0, 0)),
            pl.BlockSpec((None, EXPERT_FF, D_MODEL), lambda *a: (held(*a), 0, 0)),
            pl.BlockSpec((None, 1, D_MODEL), lambda *a: (expert(*a), 0, 0)),
        ],
        out_specs=pl.BlockSpec((MOE_ROWS, HALF), lambda *a: (blk(*a), 0)),
        scratch_shapes=[
            pltpu.VMEM((D_MODEL, 2 * EXPERT_FF), BF16),
            pltpu.VMEM((EXPERT_FF, D_MODEL), BF16),
        ],
    )
    return pl.pallas_call(
        _moe_kernel,
        grid_spec=grid_spec,
        out_shape=jax.ShapeDtypeStruct((m_pad, HALF), jnp.int32),
        compiler_params=pltpu.CompilerParams(
            dimension_semantics=("arbitrary",), vmem_limit_bytes=VMEM_LIMIT),
        name="moe_experts",
    )(block_expert, weight_expert, half_block, n_used, x_sorted, w_gu, b_gu, w_down, b_down)


def _combine_kernel(yk_ref, gates_ref, h_ref, g_ref, b_ref, out_ref):
    pad = jnp.zeros((LANES - TOP_K, COMBINE_TOKENS), F32)
    gates = jnp.transpose(jnp.concatenate([gates_ref[...], pad], axis=0))
    lo = hi = None
    for k in range(TOP_K):
        u = lax.bitcast_convert_type(yk_ref[k], jnp.uint32)
        gk = gates[:, k:k + 1]
        lo_k = lax.bitcast_convert_type(u << 16, F32) * gk
        hi_k = lax.bitcast_convert_type(u & jnp.uint32(HI_MASK), F32) * gk
        lo = lo_k if lo is None else lo + lo_k
        hi = hi_k if hi is None else hi + hi_k
    acc = ALPHA * h_ref[...] + jnp.concatenate([lo, hi], axis=1)
    out_ref[...] = _layer_norm(acc, g_ref[...], b_ref[...])


def _combine_kernel_aliased(yk_ref, gates_ref, h_ref, g_ref, b_ref, prev_ref, out_ref):
    del prev_ref
    _combine_kernel(yk_ref, gates_ref, h_ref, g_ref, b_ref, out_ref)


def _combine(yk, gates, h_all, ln_g, ln_b, row_offset, out_rows, out_offset, out_prev):
    n_seg = yk.shape[1]
    tile0 = row_offset // COMBINE_TOKENS
    out_tile0 = out_offset // COMBINE_TOKENS
    in_specs = [
        pl.BlockSpec((TOP_K, COMBINE_TOKENS, HALF), lambda i: (0, i, 0)),
        pl.BlockSpec((TOP_K, COMBINE_TOKENS), lambda i: (0, tile0 + i)),
        pl.BlockSpec((COMBINE_TOKENS, D_MODEL), lambda i: (tile0 + i, 0)),
        _const_spec((1, D_MODEL)),
        _const_spec((1, D_MODEL)),
    ]
    args = [yk, gates, h_all, ln_g, ln_b]
    aliases = {}
    kern = _combine_kernel
    if out_prev is not None:
        in_specs.append(pl.BlockSpec(memory_space=pl.ANY))
        aliases = {len(args): 0}
        args.append(out_prev)
        kern = _combine_kernel_aliased
    return pl.pallas_call(
        kern,
        grid=(n_seg // COMBINE_TOKENS,),
        in_specs=in_specs,
        out_specs=pl.BlockSpec((COMBINE_TOKENS, D_MODEL), lambda i: (out_tile0 + i, 0)),
        out_shape=jax.ShapeDtypeStruct((out_rows, D_MODEL), F32),
        input_output_aliases=aliases,
        compiler_params=pltpu.CompilerParams(
            dimension_semantics=("arbitrary",), vmem_limit_bytes=VMEM_LIMIT),
        name="moe_combine",
    )(*args)


def kernel(x_prompt, x_sample, state_pool, state_gla, w_in, w_pool, pool_scale, w_gate_up, b_gate,
           gla_norm_w, w_out, ln1_g, ln1_b, w_router, b_router, w_gu, b_gu, w_down, b_down,
           ln2_g, ln2_b):
    assert w_in.shape[0] == 1, "single-layer kernel"
    bp, seq, _ = x_prompt.shape
    bs, dec_seq, _ = x_sample.shape
    assert dec_seq == CHUNK and seq % TILE_TOKENS == 0 and bs % CHUNKS_PER_TILE == 0
    n_prompt = bp * seq
    n_sample = bs * dec_seq
    n_total = n_prompt + n_sample
    nk = n_total * TOP_K
    n_blocks = -(-nk // MOE_ROWS) + NUM_EXPERTS
    m_pad = n_blocks * MOE_ROWS

    weights = _mixer_weights(w_in, w_pool, pool_scale, w_gate_up, b_gate, gla_norm_w, w_out,
                             ln1_g, ln1_b, w_router, b_router)

    *shared, hist_p, s_p = _mixer_prompt(x_prompt, weights, n_total)
    h_all, hb_all, logits_t, hist_s, s_s = _mixer_sample(
        x_sample, state_pool[0], state_gla[0], weights, shared, n_prompt)

    gates_t, dest_t, layout = _router(logits_t)
    pad_end = layout[:, 0].astype(jnp.int32)
    counts = layout[:, 1].astype(jnp.int32)
    block_start = jnp.arange(n_blocks, dtype=jnp.int32) * MOE_ROWS
    block_expert = jnp.minimum(jnp.sum((block_start[:, None] >= pad_end[None, :]).astype(jnp.int32), axis=1),
                               NUM_EXPERTS - 1)
    n_used = (pad_end[-1:] // MOE_ROWS).astype(jnp.int32)
    is_first = jnp.concatenate([jnp.ones((1,), bool), block_expert[1:] != block_expert[:-1]])
    blocks = jnp.arange(n_blocks, dtype=jnp.int32)
    later_other = ((blocks[None, :] > blocks[:, None]) & (blocks[None, :] < n_used[0])
                   & (block_expert[None, :] != block_expert[:, None]))
    next_expert = jnp.min(jnp.where(later_other, block_expert[None, :], NUM_EXPERTS), axis=1)
    next_expert = jnp.where(next_expert == NUM_EXPERTS, block_expert, next_expert)
    weight_expert = jnp.where(is_first, block_expert, next_expert).astype(jnp.int32)
    of_expert = block_expert[:, None] == jnp.arange(NUM_EXPERTS, dtype=jnp.int32)[None, :]
    seg_end = jnp.sum(jnp.where(of_expert, (pad_end - (-counts % MOE_ROWS))[None, :], 0), axis=1)
    half_block = ((seg_end - block_start) <= MOE_ROWS // 2).astype(jnp.int32)

    dest_chunks = dest_t.reshape(TOP_K, n_total // DISPATCH_ROWS, DISPATCH_ROWS).transpose(1, 0, 2)
    x_sorted = _dispatch(hb_all, dest_chunks, m_pad)
    y_sorted = _moe_experts(block_expert, weight_expert, half_block, n_used, x_sorted, w_gu[0], b_gu[0][:, None, :],
                            w_down[0], b_down[0][:, None, :])
    gates = gates_t
    ln_g, ln_b = ln2_g[0][None, :], ln2_b[0][None, :]
    yk = _gather_expert_rows(y_sorted, dest_chunks, n_prompt, n_sample)
    y_sample = _combine(yk, gates, h_all, ln_g, ln_b, n_prompt, n_sample, 0, None)
    seg = n_prompt // COMBINE_SEGMENTS
    y_prompt = None
    for s in range(COMBINE_SEGMENTS):
        yk = _gather_expert_rows(y_sorted, dest_chunks, s * seg, seg)
        y_prompt = _combine(yk, gates, h_all, ln_g, ln_b, s * seg, n_prompt, s * seg, y_prompt)
    y_prompt = y_prompt.reshape(bp, seq, D_MODEL)
    y_sample = y_sample.reshape(bs, dec_seq, D_MODEL)
    return (y_prompt, y_sample, hist_p[None], s_p[None], hist_s[None], s_s[None])
```

```python
import functools

import jax
import jax.numpy as jnp
from jax import lax
from jax.experimental import pallas as pl
from jax.experimental.pallas import tpu as pltpu
from jax.experimental.pallas import tpu_sc as plsc

F32 = jnp.float32
BF16 = jnp.bfloat16

D_MODEL = 1024
CHUNK = 64
PAST_LEN = 1024
POOL_WIDTH = 512
POOL_WINDOWS = (2, 4, 8, 16)
POOL_GROUP = 128
POOL_HIST = 15
GLA_HEADS = 4
GLA_DK = 64
GLA_DV = 128
GATE_RANK = 16
GATE_NORMALIZER = 16.0
NUM_EXPERTS = 32
TOP_K = 4
EXPERT_FF = 1024
SWIGLU_LIMIT = 7.0
SWIGLU_ALPHA = 1.702
LN_EPS = 1e-5
RMS_EPS = 1e-6
ALPHA = 2.0 ** 0.25

Q0 = POOL_WIDTH
K0 = Q0 + GLA_HEADS * GLA_DK
V0 = K0 + GLA_HEADS * GLA_DK
R0 = V0 + GLA_HEADS * GLA_DV
N_MAIN = R0 + GLA_HEADS * GLA_DV
N_IN = N_MAIN + GATE_RANK

LANES = 128
TILE_TOKENS = 512
CHUNKS_PER_TILE = TILE_TOKENS // CHUNK
HIST_PAD = 16
OUT_CHUNKS = 4
SCAN_ROWS = 256
MOE_ROWS = 512
ROUTE_TOKENS = 1024
COMBINE_TOKENS = 512
VMEM_LIMIT = 56 * 1024 * 1024


def _dot(a, b):
    return jnp.dot(a, b, preferred_element_type=F32)


def _dot_nt(a, b):
    return lax.dot_general(a, b, (((1,), (1,)), ((), ())), preferred_element_type=F32)


def _dot_tn(a, b):
    return lax.dot_general(a, b, (((0,), (0,)), ((), ())), preferred_element_type=F32)


HALF = D_MODEL // 2
HI_MASK = 0xFFFF0000


def _pack_bf16_pairs(xb):
    lo = lax.bitcast_convert_type(xb[:, :HALF].astype(F32), jnp.uint32) >> 16
    hi = lax.bitcast_convert_type(xb[:, HALF:].astype(F32), jnp.uint32) & jnp.uint32(HI_MASK)
    return lax.bitcast_convert_type(hi | lo, jnp.int32)


def _unpack_bf16_pairs(p):
    u = lax.bitcast_convert_type(p, jnp.uint32)
    lo = lax.bitcast_convert_type(u << 16, F32)
    hi = lax.bitcast_convert_type(u & jnp.uint32(HI_MASK), F32)
    return jnp.concatenate([lo, hi], axis=1).astype(BF16)


def _layer_norm(v, g, b):
    mu = jnp.mean(v, axis=-1, keepdims=True)
    c = v - mu
    var = jnp.mean(c * c, axis=-1, keepdims=True)
    return c * lax.rsqrt(var + LN_EPS) * g + b


N_MIXER_WEIGHTS = 12
N_MIXER_SHARED_OUT = 3


def _mixer_kernel(per_chunk_state, pos0, *refs):
    if per_chunk_state:
        (x_ref, hist_in_ref, s_in_ref, *rest) = refs
    else:
        (x_ref, *rest) = refs
        hist_in_ref = s_in_ref = None
    (w_main_ref, w_glr_ref, w_gate_ref, b_gate_ref, w_pool_ref, pscale_ref, gnorm_ref,
     w_out_ref, ln1g_ref, ln1b_ref, w_router_ref, b_router_ref, *rest) = rest
    if per_chunk_state:
        rest = rest[N_MIXER_SHARED_OUT:]
    (h_ref, hb_ref, logits_ref, hist_out_ref, s_out_ref,
     proj_scr, b_scr, o_scr, ext_scr, st_scr, tri_scr, w_main_bf, w_out_bf) = rest

    if per_chunk_state:
        t = None
        first_step = pl.program_id(0) == 0
    else:
        t = pl.program_id(1)
        first_step = (pl.program_id(0) == 0) & (t == 0)
    x = x_ref[...].reshape(TILE_TOKENS, D_MODEL)
    xb = x.astype(BF16)

    @pl.when(first_step)
    def _():
        ti = lax.broadcasted_iota(jnp.int32, (SCAN_ROWS, SCAN_ROWS), 0)
        tj = lax.broadcasted_iota(jnp.int32, (SCAN_ROWS, SCAN_ROWS), 1)
        same_chunk = (ti // CHUNK) == (tj // CHUNK)
        tri_scr[...] = jnp.where(same_chunk & (ti >= tj), 1.0, 0.0).astype(BF16)
        w_main_bf[...] = w_main_ref[:, 0:N_MAIN].astype(BF16)
        w_out_bf[...] = w_out_ref[...].astype(BF16)
        if not per_chunk_state:
            st_scr[...] = jnp.zeros_like(st_scr)
            ext_scr[0:HIST_PAD, :] = jnp.zeros((HIST_PAD, POOL_WIDTH), F32)

    glr = _dot(xb, w_glr_ref[...])
    proj_scr[:, 0:V0] = _dot(xb, w_main_bf[:, 0:V0])
    gk = _dot(glr.astype(BF16), w_gate_ref[...]) + b_gate_ref[...]
    log_sig = jnp.minimum(gk, 0.0) - jnp.log1p(jnp.exp(-jnp.abs(gk)))
    g = log_sig / GATE_NORMALIZER
    g_hi = g.astype(BF16)
    g_lo = (g - g_hi.astype(F32)).astype(BF16)
    proj_scr[:, V0:N_MAIN] = _dot(xb, w_main_bf[:, V0:N_MAIN])
    for s in range(TILE_TOKENS // SCAN_ROWS):
        rs = slice(s * SCAN_ROWS, (s + 1) * SCAN_ROWS)
        b_scr[rs, :] = _dot(tri_scr[...], g_hi[rs]) + _dot(tri_scr[...], g_lo[rs])

    if per_chunk_state:
        seg_len, seg_stride, n_seg = CHUNK, CHUNK + HIST_PAD, CHUNKS_PER_TILE
        for c in range(n_seg):
            base = c * seg_stride
            ext_scr[base:base + HIST_PAD, :] = jnp.zeros((HIST_PAD, POOL_WIDTH), F32)
            ext_scr[base + 1:base + HIST_PAD, :] = hist_in_ref[c]
            ext_scr[base + HIST_PAD:base + seg_stride, :] = proj_scr[c * CHUNK:(c + 1) * CHUNK, 0:POOL_WIDTH]
        row_pos = pos0 + lax.broadcasted_iota(jnp.int32, (seg_len, POOL_GROUP), 0)
    else:
        seg_len, seg_stride, n_seg = TILE_TOKENS, TILE_TOKENS + HIST_PAD, 1
        ext_scr[0:HIST_PAD, :] = jnp.where(t == 0, 0.0, ext_scr[0:HIST_PAD, :])
        ext_scr[HIST_PAD:seg_stride, :] = proj_scr[:, 0:POOL_WIDTH]
        row_pos = pos0 + t * TILE_TOKENS + lax.broadcasted_iota(jnp.int32, (seg_len, POOL_GROUP), 0)

    pooled_groups = []
    for gi, w in enumerate(POOL_WINDOWS):
        gs = slice(gi * POOL_GROUP, (gi + 1) * POOL_GROUP)
        cnt = jnp.minimum(row_pos + 1, w).astype(F32)
        ext = ext_scr[:, gs]
        win = ext
        shift = 1
        while shift < w:
            win = win + pltpu.roll(win, shift, 0)
            shift *= 2
        segs = []
        for s in range(n_seg):
            base = s * seg_stride + HIST_PAD
            segs.append(win[base:base + seg_len] / cnt - ext[base:base + seg_len])
        pooled = segs[0] if n_seg == 1 else jnp.concatenate(segs, axis=0)
        pooled_groups.append(pooled.astype(BF16))
    pool_cols = []
    for p in range(len(POOL_WINDOWS) // 2):
        both = jnp.concatenate(pooled_groups[2 * p:2 * p + 2], axis=1)
        pool_cols.append(_dot(both, w_pool_ref[p]))
    pool_out = jnp.concatenate(pool_cols, axis=1) * pscale_ref[...]

    if per_chunk_state:
        for c in range(n_seg):
            end = (c + 1) * seg_stride
            hist_out_ref[c] = ext_scr[end - POOL_HIST:end, :]
    else:
        hist_out_ref[...] = ext_scr[seg_stride - POOL_HIST:seg_stride, :]
        ext_scr[0:HIST_PAD, :] = ext_scr[TILE_TOKENS:seg_stride, :]

    hk = GLA_HEADS * GLA_DK
    hv = GLA_HEADS * GLA_DV
    pair_rows = 2 * CHUNK
    decay_cols = LANES // CHUNKS_PER_TILE

    def head_of(shape, dim, width):
        return lax.broadcasted_iota(jnp.int32, shape, dim) // width

    same_head_k = head_of((hk, hk), 0, CHUNK) == head_of((hk, hk), 1, GLA_DK)
    same_head_v = head_of((hk, hv), 0, CHUNK) == head_of((hk, hv), 1, GLA_DV)
    pair_half = head_of((pair_rows, hv), 0, CHUNK)
    causal = (lax.broadcasted_iota(jnp.int32, (CHUNK, hk), 0)
              >= lax.broadcasted_iota(jnp.int32, (CHUNK, hk), 1) % CHUNK)

    b_all = b_scr[...]
    b_last = [b_scr[(c + 1) * CHUNK - 1:(c + 1) * CHUNK, :] for c in range(CHUNKS_PER_TILE)]
    b_last_rows = jnp.concatenate([jnp.broadcast_to(bl, (CHUNK, hk)) for bl in b_last], axis=0)
    k_all = proj_scr[:, K0:V0]
    qt_all = (proj_scr[:, Q0:K0] * (GLA_DK ** -0.5) * jnp.exp(b_all)).astype(BF16)
    kt_all = k_all * jnp.exp(-b_all)
    kl_t = jnp.transpose(k_all * jnp.exp(b_last_rows - b_all)).astype(BF16)
    decay_t = jnp.transpose(jnp.exp(jnp.concatenate(
        [jnp.broadcast_to(bl, (decay_cols, hk)) for bl in b_last], axis=0)))

    def finish_rows(rs):
        r = proj_scr[rs, R0:N_MAIN]
        silu_r = r * (1.0 / (1.0 + jnp.exp(-r)))
        gated = []
        for h in range(GLA_HEADS):
            vs = slice(h * GLA_DV, (h + 1) * GLA_DV)
            oh = o_scr[rs, vs]
            ms = jnp.mean(oh * oh, axis=-1, keepdims=True)
            gated.append(oh * lax.rsqrt(ms + RMS_EPS) * gnorm_ref[...] * silu_r[:, vs])
        mix_in = jnp.concatenate([pool_out[rs]] + gated, axis=1).astype(BF16)
        resid = ALPHA * x[rs] + _dot(mix_in, w_out_bf[...])
        h_val = _layer_norm(resid, ln1g_ref[...], ln1b_ref[...])
        h_ref[rs, :] = h_val
        hb = h_val.astype(BF16)
        hb_ref[rs, :] = _pack_bf16_pairs(hb)
        logits_ref[:, rs] = _dot_nt(w_router_ref[...], hb) + b_router_ref[:, 0:1]

    st = None if per_chunk_state else jnp.where(t == 0, 0.0, st_scr[...])
    for c in range(CHUNKS_PER_TILE):
        rows = slice(c * CHUNK, (c + 1) * CHUNK)
        pair = slice((c // 2) * pair_rows, (c // 2 + 1) * pair_rows)
        if per_chunk_state:
            st = s_in_ref[c].reshape(hk, GLA_DV)
        qt = qt_all[rows]
        zero = jnp.zeros((), BF16)
        k_stack = jnp.where(same_head_k, jnp.concatenate([kt_all[rows].astype(BF16)] * GLA_HEADS, axis=0), zero)
        v_stack = jnp.where(same_head_v, jnp.concatenate(
            [proj_scr[rows, V0:R0].astype(BF16)] * GLA_HEADS, axis=0), zero)
        s_stack = jnp.where(same_head_v, jnp.concatenate([st.astype(BF16)] * GLA_HEADS, axis=1), zero)
        att = jnp.where(causal, _dot_nt(qt, k_stack), 0.0)
        o_scr[rows, :] = _dot(att.astype(BF16), v_stack) + _dot(qt, s_stack)
        v_chunk = jnp.where(pair_half == c % 2, proj_scr[pair, V0:R0], 0.0).astype(BF16)
        upd = jnp.concatenate(
            [_dot(kl_t[h * GLA_DK:(h + 1) * GLA_DK, pair], v_chunk[:, h * GLA_DV:(h + 1) * GLA_DV])
             for h in range(GLA_HEADS)], axis=0)
        st = st * decay_t[:, c * decay_cols:c * decay_cols + 1] + upd
        if per_chunk_state:
            s_out_ref[c] = st.reshape(GLA_HEADS, GLA_DK, GLA_DV)
        if (c + 1) % OUT_CHUNKS == 0:
            finish_rows(slice((c + 1 - OUT_CHUNKS) * CHUNK, (c + 1) * CHUNK))

    if not per_chunk_state:
        st_scr[...] = st
        s_out_ref[...] = st.reshape(GLA_HEADS, GLA_DK, GLA_DV)


def _const_spec(shape, single_buffer=False):
    nd = len(shape)
    if single_buffer:
        return pl.BlockSpec(shape, lambda *_: (0,) * nd, pipeline_mode=pl.Buffered(1))
    return pl.BlockSpec(shape, lambda *_: (0,) * nd)


def _mixer_weight_specs():
    return [
        _const_spec((None, D_MODEL, N_IN), single_buffer=True),
        _const_spec((D_MODEL, LANES)),
        _const_spec((LANES, GLA_HEADS * GLA_DK)),
        _const_spec((1, GLA_HEADS * GLA_DK)),
        _const_spec((len(POOL_WINDOWS) // 2, 2 * POOL_GROUP, 2 * POOL_GROUP)),
        _const_spec((1, POOL_WIDTH)),
        _const_spec((1, GLA_DV)),
        _const_spec((None, D_MODEL, D_MODEL), single_buffer=True),
        _const_spec((1, D_MODEL)),
        _const_spec((1, D_MODEL)),
        _const_spec((NUM_EXPERTS, D_MODEL)),
        _const_spec((NUM_EXPERTS, LANES)),
    ]


def _mixer_weights(w_in, w_pool, pool_scale, w_gate_up, b_gate, gla_norm_w, w_out, ln1_g, ln1_b,
                   w_router, b_router):
    w_glr = jnp.zeros((D_MODEL, LANES), BF16).at[:, :GATE_RANK].set(w_in[0, :, N_MAIN:].astype(BF16))
    w_gate = jnp.zeros((LANES, GLA_HEADS * GLA_DK), BF16).at[:GATE_RANK].set(w_gate_up[0].astype(BF16))
    wp = w_pool[0].astype(BF16)
    zero = jnp.zeros((POOL_GROUP, POOL_GROUP), BF16)
    w_pool_pairs = jnp.stack([jnp.block([[wp[2 * p], zero], [zero, wp[2 * p + 1]]])
                              for p in range(len(POOL_WINDOWS) // 2)])
    weights = (
        w_in, w_glr, w_gate, b_gate[0][None, :],
        w_pool_pairs, pool_scale[0][None, :], gla_norm_w[0][None, :],
        w_out, ln1_g[0][None, :], ln1_b[0][None, :],
        w_router[0].T.astype(BF16), jnp.broadcast_to(b_router[0][:, None], (NUM_EXPERTS, LANES)),
    )
    assert len(weights) == N_MIXER_WEIGHTS
    return weights


def _mixer_scratch(per_chunk_state):
    ext_rows = (CHUNKS_PER_TILE * (CHUNK + HIST_PAD)) if per_chunk_state else (TILE_TOKENS + HIST_PAD)
    return [
        pltpu.VMEM((TILE_TOKENS, N_MAIN), F32),
        pltpu.VMEM((TILE_TOKENS, GLA_HEADS * GLA_DK), F32),
        pltpu.VMEM((TILE_TOKENS, GLA_HEADS * GLA_DV), F32),
        pltpu.VMEM((ext_rows, POOL_WIDTH), F32),
        pltpu.VMEM((GLA_HEADS * GLA_DK, GLA_DV), F32),
        pltpu.VMEM((SCAN_ROWS, SCAN_ROWS), BF16),
        pltpu.VMEM((D_MODEL, N_MAIN), BF16),
        pltpu.VMEM((D_MODEL, D_MODEL), BF16),
    ]


def _mixer_out_shapes(n_total, bsz):
    return (
        jax.ShapeDtypeStruct((n_total, D_MODEL), F32),
        jax.ShapeDtypeStruct((n_total, HALF), jnp.int32),
        jax.ShapeDtypeStruct((NUM_EXPERTS, n_total), F32),
        jax.ShapeDtypeStruct((bsz, POOL_HIST, POOL_WIDTH), F32),
        jax.ShapeDtypeStruct((bsz, GLA_HEADS, GLA_DK, GLA_DV), F32),
    )


def _mixer_prompt(x, weights, n_total):
    bsz, seq, _ = x.shape
    tiles = seq // TILE_TOKENS
    return pl.pallas_call(
        functools.partial(_mixer_kernel, False, 0),
        grid=(bsz, tiles),
        in_specs=[pl.BlockSpec((None, TILE_TOKENS, D_MODEL), lambda b, t: (b, t, 0))] + _mixer_weight_specs(),
        out_specs=(
            pl.BlockSpec((TILE_TOKENS, D_MODEL), lambda b, t: (b * tiles + t, 0)),
            pl.BlockSpec((TILE_TOKENS, HALF), lambda b, t: (b * tiles + t, 0)),
            pl.BlockSpec((NUM_EXPERTS, TILE_TOKENS), lambda b, t: (0, b * tiles + t)),
            pl.BlockSpec((None, POOL_HIST, POOL_WIDTH), lambda b, t: (b, 0, 0)),
            pl.BlockSpec((None, GLA_HEADS, GLA_DK, GLA_DV), lambda b, t: (b, 0, 0, 0)),
        ),
        out_shape=_mixer_out_shapes(n_total, bsz),
        scratch_shapes=_mixer_scratch(False),
        compiler_params=pltpu.CompilerParams(
            dimension_semantics=("arbitrary", "arbitrary"), vmem_limit_bytes=VMEM_LIMIT),
        name="mixer_prompt",
    )(x, *weights)


def _mixer_sample(x, hist, state, weights, shared, row_offset):
    bsz = x.shape[0]
    tiles = bsz // CHUNKS_PER_TILE
    tile0 = row_offset // TILE_TOKENS
    n_total = shared[0].shape[0]
    n_in = 3 + N_MIXER_WEIGHTS
    return pl.pallas_call(
        functools.partial(_mixer_kernel, True, PAST_LEN),
        grid=(tiles,),
        in_specs=[
            pl.BlockSpec((CHUNKS_PER_TILE, CHUNK, D_MODEL), lambda i: (i, 0, 0)),
            pl.BlockSpec((CHUNKS_PER_TILE, POOL_HIST, POOL_WIDTH), lambda i: (i, 0, 0)),
            pl.BlockSpec((CHUNKS_PER_TILE, GLA_HEADS, GLA_DK, GLA_DV), lambda i: (i, 0, 0, 0)),
        ] + _mixer_weight_specs() + [pl.BlockSpec(memory_space=pl.ANY)] * N_MIXER_SHARED_OUT,
        out_specs=(
            pl.BlockSpec((TILE_TOKENS, D_MODEL), lambda i: (tile0 + i, 0)),
            pl.BlockSpec((TILE_TOKENS, HALF), lambda i: (tile0 + i, 0)),
            pl.BlockSpec((NUM_EXPERTS, TILE_TOKENS), lambda i: (0, tile0 + i)),
            pl.BlockSpec((CHUNKS_PER_TILE, POOL_HIST, POOL_WIDTH), lambda i: (i, 0, 0)),
            pl.BlockSpec((CHUNKS_PER_TILE, GLA_HEADS, GLA_DK, GLA_DV), lambda i: (i, 0, 0, 0)),
        ),
        out_shape=_mixer_out_shapes(n_total, bsz),
        input_output_aliases={n_in + j: j for j in range(N_MIXER_SHARED_OUT)},
        scratch_shapes=_mixer_scratch(True),
        compiler_params=pltpu.CompilerParams(
            dimension_semantics=("arbitrary",), vmem_limit_bytes=VMEM_LIMIT),
        name="mixer_sample",
    )(x, hist, state, *weights, *shared)


def _router_kernel(lt_ref, gates_ref, dest_ref, padend_ref, cnt_scr, base_scr, pstart_scr, before_scr,
                   topk_scr):
    phase = pl.program_id(0)
    i = pl.program_id(1)
    shape = (NUM_EXPERTS, ROUTE_TOKENS)
    row = lax.broadcasted_iota(jnp.int32, shape, 0)

    def tile_counts_of(chosen):
        return jnp.broadcast_to(jnp.sum(chosen, axis=1, keepdims=True), (NUM_EXPERTS, LANES))

    @pl.when(phase == 0)
    def _():
        @pl.when(i == 0)
        def _():
            cnt_scr[...] = jnp.zeros_like(cnt_scr)

        logits = lt_ref[...]
        idxs, vals = [], []
        chosen = jnp.zeros(shape, F32)
        for _ in range(TOP_K):
            m = jnp.max(logits, axis=0, keepdims=True)
            idx = jnp.min(jnp.where(logits == m, row, NUM_EXPERTS), axis=0, keepdims=True)
            hit = row == idx
            idxs.append(idx.astype(F32))
            vals.append(m)
            chosen = chosen + jnp.where(hit, 1.0, 0.0)
            logits = jnp.where(hit, -jnp.inf, logits)
        topk_scr[i] = jnp.concatenate(idxs + vals, axis=0)
        cnt_scr[...] += tile_counts_of(chosen)

    @pl.when(phase == 1)
    def _():
        @pl.when(i == 0)
        def _():
            blocks = jnp.floor((cnt_scr[...] + (MOE_ROWS - 1)) * (1.0 / MOE_ROWS))
            erow = lax.broadcasted_iota(jnp.int32, (NUM_EXPERTS, LANES), 0)
            lane = lax.broadcasted_iota(jnp.int32, (NUM_EXPERTS, LANES), 1)
            cum = blocks
            shift = 1
            while shift < NUM_EXPERTS:
                cum = cum + jnp.where(erow >= shift, pltpu.roll(cum, shift, 0), 0.0)
                shift *= 2
            padend_ref[...] = jnp.where(lane == 1, cnt_scr[...], cum * MOE_ROWS)
            pstart_scr[...] = (cum - blocks) * MOE_ROWS
            base_scr[...] = jnp.zeros_like(base_scr)
            ti = lax.broadcasted_iota(jnp.int32, (ROUTE_TOKENS, ROUTE_TOKENS), 0)
            tj = lax.broadcasted_iota(jnp.int32, (ROUTE_TOKENS, ROUTE_TOKENS), 1)
            before_scr[...] = jnp.where(ti < tj, 1.0, 0.0).astype(BF16)

        topk = topk_scr[i]
        sel = [row == topk[k:k + 1, :].astype(jnp.int32) for k in range(TOP_K)]
        vals = [topk[TOP_K + k:TOP_K + k + 1, :] for k in range(TOP_K)]
        chosen = sum(jnp.where(hit, 1.0, 0.0) for hit in sel)
        earlier = _dot(chosen.astype(BF16), before_scr[...])
        pos = pstart_scr[:, 0:1] + base_scr[:, 0:1] + earlier
        dest = [jnp.sum(jnp.where(hit, pos, 0.0), axis=0, keepdims=True) for hit in sel]
        dest_ref[...] = jnp.concatenate(dest, axis=0).astype(jnp.int32)
        ex = [jnp.exp(v - vals[0]) for v in vals]
        denom = ex[0] + ex[1] + ex[2] + ex[3]
        gates_ref[...] = jnp.concatenate([e / denom for e in ex], axis=0)
        base_scr[...] += tile_counts_of(chosen)


def _router(logits_t):
    n = logits_t.shape[1]
    assert n % ROUTE_TOKENS == 0
    tiles = n // ROUTE_TOKENS
    return pl.pallas_call(
        _router_kernel,
        grid=(2, tiles),
        in_specs=[pl.BlockSpec((NUM_EXPERTS, ROUTE_TOKENS), lambda p, i: (0, i))],
        out_specs=(
            pl.BlockSpec((TOP_K, ROUTE_TOKENS), lambda p, i: (0, i * p)),
            pl.BlockSpec((TOP_K, ROUTE_TOKENS), lambda p, i: (0, i * p)),
            pl.BlockSpec((NUM_EXPERTS, LANES), lambda p, i: (0, 0)),
        ),
        out_shape=(
            jax.ShapeDtypeStruct((TOP_K, n), F32),
            jax.ShapeDtypeStruct((TOP_K, n), jnp.int32),
            jax.ShapeDtypeStruct((NUM_EXPERTS, LANES), F32),
        ),
        scratch_shapes=[pltpu.VMEM((NUM_EXPERTS, LANES), F32)] * 3
        + [pltpu.VMEM((ROUTE_TOKENS, ROUTE_TOKENS), BF16),
           pltpu.VMEM((tiles, 2 * TOP_K, ROUTE_TOKENS), F32)],
        compiler_params=pltpu.CompilerParams(
            dimension_semantics=("arbitrary", "arbitrary"), vmem_limit_bytes=VMEM_LIMIT),
        name="router",
    )(logits_t)


SC_CORES = 2
SC_SUBCORES = 16
SC_WORKERS = SC_CORES * SC_SUBCORES
DISPATCH_ROWS = 64


def _dispatch(h_packed, dest_chunks, m_pad):
    n = h_packed.shape[0]
    n_chunks = n // DISPATCH_ROWS
    assert n_chunks % SC_WORKERS == 0
    per_worker = n_chunks // SC_WORKERS
    mesh = plsc.VectorSubcoreMesh(core_axis_name="c", subcore_axis_name="s")

    @functools.partial(
        pl.kernel, mesh=mesh,
        out_type=jax.ShapeDtypeStruct((m_pad, HALF), jnp.int32),
        scratch_types=[
            pltpu.VMEM((2, TOP_K, DISPATCH_ROWS), jnp.int32),
            pltpu.VMEM((2, DISPATCH_ROWS, HALF), jnp.int32),
            pltpu.SemaphoreType.DMA((2,)),
            pltpu.SemaphoreType.DMA((2,)),
        ],
        compiler_params=pltpu.CompilerParams(use_tc_tiling_on_sc=True),
        name="dispatch",
    )
    def dispatch_kernel(h_hbm, dest_hbm, out_hbm, idx_v, rows_v, load_sems, scatter_sems):
        wid = lax.axis_index("s") * SC_CORES + lax.axis_index("c")

        def loads(j):
            chunk = wid * per_worker + j
            slot = j % 2
            return (
                pltpu.make_async_copy(dest_hbm.at[chunk], idx_v.at[slot], load_sems.at[slot]),
                pltpu.make_async_copy(h_hbm.at[pl.ds(chunk * DISPATCH_ROWS, DISPATCH_ROWS)],
                                      rows_v.at[slot], load_sems.at[slot]),
            )

        def scatters(j):
            slot = j % 2
            return [pltpu.make_async_copy(rows_v.at[slot], out_hbm.at[idx_v.at[slot, k]],
                                          scatter_sems.at[slot]) for k in range(TOP_K)]

        for cp in loads(0):
            cp.start()
        for j in range(per_worker):
            for cp in loads(j):
                cp.wait()
            if j >= 1:
                for cp in scatters(j - 1):
                    cp.wait()
            if j + 1 < per_worker:
                for cp in loads(j + 1):
                    cp.start()
            for cp in scatters(j):
                cp.start()
        for cp in scatters(per_worker - 1):
            cp.wait()

    return dispatch_kernel(h_packed, dest_chunks)


def _gather_expert_rows(y_sorted, dest_chunks, row_offset, n):
    n_chunks = n // DISPATCH_ROWS
    assert n_chunks % SC_WORKERS == 0 and row_offset % DISPATCH_ROWS == 0
    per_worker = n_chunks // SC_WORKERS
    chunk0 = row_offset // DISPATCH_ROWS
    mesh = plsc.VectorSubcoreMesh(core_axis_name="c", subcore_axis_name="s")

    @functools.partial(
        pl.kernel, mesh=mesh,
        out_type=jax.ShapeDtypeStruct((TOP_K, n, HALF), jnp.int32),
        scratch_types=[
            pltpu.VMEM((TOP_K, DISPATCH_ROWS), jnp.int32),
            pltpu.VMEM((2, DISPATCH_ROWS, HALF), jnp.int32),
            pltpu.SemaphoreType.DMA((2,)),
        ],
        compiler_params=pltpu.CompilerParams(use_tc_tiling_on_sc=True),
        name="gather_expert_rows",
    )
    def gather_kernel(y_hbm, dest_hbm, out_hbm, idx_v, rows_v, sems):
        wid = lax.axis_index("s") * SC_CORES + lax.axis_index("c")

        def gather(k):
            return pltpu.make_async_copy(y_hbm.at[idx_v.at[k]], rows_v.at[k % 2], sems.at[k % 2])

        @pl.loop(0, per_worker)
        def _(j):
            local = wid * per_worker + j
            pltpu.sync_copy(dest_hbm.at[chunk0 + local], idx_v)
            gather(0).start()
            for k in range(TOP_K):
                if k + 1 < TOP_K:
                    gather(k + 1).start()
                gather(k).wait()
                pltpu.sync_copy(rows_v.at[k % 2],
                                out_hbm.at[k, pl.ds(local * DISPATCH_ROWS, DISPATCH_ROWS)])

    return gather_kernel(y_sorted, dest_chunks)


def _moe_kernel(be_ref, wsel_ref, half_ref, nused_ref, x_ref, wgu_ref, bgu_ref, wd_ref, bd_ref, y_ref,
                wgu_bf, wd_bf):
    del wsel_ref
    i = pl.program_id(0)

    def ffn(rows):
        gu = _dot(_unpack_bf16_pairs(x_ref[rows, :]), wgu_bf[...]) + bgu_ref[...]
        gate = jnp.minimum(gu[:, :EXPERT_FF], SWIGLU_LIMIT)
        up = jnp.clip(gu[:, EXPERT_FF:], -SWIGLU_LIMIT, SWIGLU_LIMIT)
        hmid = gate * (1.0 / (1.0 + jnp.exp(-SWIGLU_ALPHA * gate))) * (up + 1.0)
        y = _dot(hmid.astype(BF16), wd_bf[...]) + bd_ref[...]
        y_ref[rows, :] = _pack_bf16_pairs(y.astype(BF16))

    @pl.when(i < nused_ref[0])
    def _():
        @pl.when((i == 0) | (be_ref[i] != be_ref[jnp.maximum(i - 1, 0)]))
        def _():
            wgu_bf[...] = wgu_ref[...].astype(BF16)
            wd_bf[...] = wd_ref[...].astype(BF16)

        @pl.when(half_ref[i] == 0)
        def _():
            ffn(slice(0, MOE_ROWS))

        @pl.when(half_ref[i] != 0)
        def _():
            ffn(slice(0, MOE_ROWS // 2))


def _moe_experts(block_expert, weight_expert, half_block, n_used, x_sorted, w_gu, b_gu, w_down, b_down):
    m_pad = x_sorted.shape[0]
    n_blocks = m_pad // MOE_ROWS

    def blk(i, be, ws, hb, nu):
        return jnp.minimum(i, nu[0] - 1)

    def expert(i, be, ws, hb, nu):
        return be[blk(i, be, ws, hb, nu)]

    def held(i, be, ws, hb, nu):
        return ws[blk(i, be, ws, hb, nu)]

    grid_spec = pltpu.PrefetchScalarGridSpec(
        num_scalar_prefetch=4,
        grid=(n_blocks,),
        in_specs=[
            pl.BlockSpec((MOE_ROWS, HALF), lambda *a: (blk(*a), 0)),
            pl.BlockSpec((None, D_MODEL, 2 * EXPERT_FF), lambda *a: (held(*a), 0, 0)),
            pl.BlockSpec((None, 1, 2 * EXPERT_FF), lambda *a: (expert(*a), 0, 0)),
            pl.BlockSpec((None, EXPERT_FF, D_MODEL), lambda *a: (held(*a), 0, 0)),
            pl.BlockSpec((None, 1, D_MODEL), lambda *a: (expert(*a), 0, 0)),
        ],
        out_specs=pl.BlockSpec((MOE_ROWS, HALF), lambda *a: (blk(*a), 0)),
        scratch_shapes=[
            pltpu.VMEM((D_MODEL, 2 * EXPERT_FF), BF16),
            pltpu.VMEM((EXPERT_FF, D_MODEL), BF16),
        ],
    )
    return pl.pallas_call(
        _moe_kernel,
        grid_spec=grid_spec,
        out_shape=jax.ShapeDtypeStruct((m_pad, HALF), jnp.int32),
        compiler_params=pltpu.CompilerParams(
            dimension_semantics=("arbitrary",), vmem_limit_bytes=VMEM_LIMIT),
        name="moe_experts",
    )(block_expert, weight_expert, half_block, n_used, x_sorted, w_gu, b_gu, w_down, b_down)


def _combine_kernel(yk_ref, gates_ref, h_ref, g_ref, b_ref, out_ref):
    pad = jnp.zeros((LANES - TOP_K, COMBINE_TOKENS), F32)
    gates = jnp.transpose(jnp.concatenate([gates_ref[...], pad], axis=0))
    lo = hi = None
    for k in range(TOP_K):
        u = lax.bitcast_convert_type(yk_ref[k], jnp.uint32)
        gk = gates[:, k:k + 1]
        lo_k = lax.bitcast_convert_type(u << 16, F32) * gk
        hi_k = lax.bitcast_convert_type(u & jnp.uint32(HI_MASK), F32) * gk
        lo = lo_k if lo is None else lo + lo_k
        hi = hi_k if hi is None else hi + hi_k
    acc = ALPHA * h_ref[...] + jnp.concatenate([lo, hi], axis=1)
    out_ref[...] = _layer_norm(acc, g_ref[...], b_ref[...])


def _combine_kernel_aliased(yk_ref, gates_ref, h_ref, g_ref, b_ref, prev_ref, out_ref):
    del prev_ref
    _combine_kernel(yk_ref, gates_ref, h_ref, g_ref, b_ref, out_ref)


def _combine(yk, gates, h_all, ln_g, ln_b, row_offset, out_rows, out_offset, out_prev):
    n_seg = yk.shape[1]
    tile0 = row_offset // COMBINE_TOKENS
    out_tile0 = out_offset // COMBINE_TOKENS
    in_specs = [
        pl.BlockSpec((TOP_K, COMBINE_TOKENS, HALF), lambda i: (0, i, 0)),
        pl.BlockSpec((TOP_K, COMBINE_TOKENS), lambda i: (0, tile0 + i)),
        pl.BlockSpec((COMBINE_TOKENS, D_MODEL), lambda i: (tile0 + i, 0)),
        _const_spec((1, D_MODEL)),
        _const_spec((1, D_MODEL)),
    ]
    args = [yk, gates, h_all, ln_g, ln_b]
    aliases = {}
    kern = _combine_kernel
    if out_prev is not None:
        in_specs.append(pl.BlockSpec(memory_space=pl.ANY))
        aliases = {len(args): 0}
        args.append(out_prev)
        kern = _combine_kernel_aliased
    return pl.pallas_call(
        kern,
        grid=(n_seg // COMBINE_TOKENS,),
        in_specs=in_specs,
        out_specs=pl.BlockSpec((COMBINE_TOKENS, D_MODEL), lambda i: (out_tile0 + i, 0)),
        out_shape=jax.ShapeDtypeStruct((out_rows, D_MODEL), F32),
        input_output_aliases=aliases,
        compiler_params=pltpu.CompilerParams(
            dimension_semantics=("arbitrary",), vmem_limit_bytes=VMEM_LIMIT),
        name="moe_combine",
    )(*args)


def kernel(x_prompt, x_sample, state_pool, state_gla, w_in, w_pool, pool_scale, w_gate_up, b_gate,
           gla_norm_w, w_out, ln1_g, ln1_b, w_router, b_router, w_gu, b_gu, w_down, b_down,
           ln2_g, ln2_b):
    assert w_in.shape[0] == 1, "single-layer kernel"
    bp, seq, _ = x_prompt.shape
    bs, dec_seq, _ = x_sample.shape
    assert dec_seq == CHUNK and seq % TILE_TOKENS == 0 and bs % CHUNKS_PER_TILE == 0
    n_prompt = bp * seq
    n_sample = bs * dec_seq
    n_total = n_prompt + n_sample
    nk = n_total * TOP_K
    n_blocks = -(-nk // MOE_ROWS) + NUM_EXPERTS
    m_pad = n_blocks * MOE_ROWS

    weights = _mixer_weights(w_in, w_pool, pool_scale, w_gate_up, b_gate, gla_norm_w, w_out,
                             ln1_g, ln1_b, w_router, b_router)

    *shared, hist_p, s_p = _mixer_prompt(x_prompt, weights, n_total)
    h_all, hb_all, logits_t, hist_s, s_s = _mixer_sample(
        x_sample, state_pool[0], state_gla[0], weights, shared, n_prompt)

    gates_t, dest_t, layout = _router(logits_t)
    pad_end = layout[:, 0].astype(jnp.int32)
    counts = layout[:, 1].astype(jnp.int32)
    block_start = jnp.arange(n_blocks, dtype=jnp.int32) * MOE_ROWS
    block_expert = jnp.minimum(jnp.sum((block_start[:, None] >= pad_end[None, :]).astype(jnp.int32), axis=1),
                               NUM_EXPERTS - 1)
    n_used = (pad_end[-1:] // MOE_ROWS).astype(jnp.int32)
    is_first = jnp.concatenate([jnp.ones((1,), bool), block_expert[1:] != block_expert[:-1]])
    blocks = jnp.arange(n_blocks, dtype=jnp.int32)
    later_other = ((blocks[None, :] > blocks[:, None]) & (blocks[None, :] < n_used[0])
                   & (block_expert[None, :] != block_expert[:, None]))
    next_expert = jnp.min(jnp.where(later_other, block_expert[None, :], NUM_EXPERTS), axis=1)
    next_expert = jnp.where(next_expert == NUM_EXPERTS, block_expert, next_expert)
    weight_expert = jnp.where(is_first, block_expert, next_expert).astype(jnp.int32)
    of_expert = block_expert[:, None] == jnp.arange(NUM_EXPERTS, dtype=jnp.int32)[None, :]
    seg_end = jnp.sum(jnp.where(of_expert, (pad_end - (-counts % MOE_ROWS))[None, :], 0), axis=1)
    half_block = ((seg_end - block_start) <= MOE_ROWS // 2).astype(jnp.int32)

    dest_chunks = dest_t.reshape(TOP_K, n_total // DISPATCH_ROWS, DISPATCH_ROWS).transpose(1, 0, 2)
    x_sorted = _dispatch(hb_all, dest_chunks, m_pad)
    y_sorted = _moe_experts(block_expert, weight_expert, half_block, n_used, x_sorted, w_gu[0], b_gu[0][:, None, :],
                            w_down[0], b_down[0][:, None, :])
    gates = gates_t
    ln_g, ln_b = ln2_g[0][None, :], ln2_b[0][None, :]
    yk = _gather_expert_rows(y_sorted, dest_chunks, n_prompt, n_sample)
    y_sample = _combine(yk, gates, h_all, ln_g, ln_b, n_prompt, n_sample, 0, None)
    unit = SC_WORKERS * DISPATCH_ROWS
    assert n_prompt % unit == 0
    sizes, left, size = [], n_prompt // unit, 1
    while left > 0:
        size = min(left, size)
        sizes.append(size * unit)
        left -= size
        size = max(size + 1, (3 * size) // 2)
    y_prompt, start = None, 0
    for seg in sizes:
        yk = _gather_expert_rows(y_sorted, dest_chunks, start, seg)
        y_prompt = _combine(yk, gates, h_all, ln_g, ln_b, start, n_prompt, start, y_prompt)
        start += seg
    y_prompt = y_prompt.reshape(bp, seq, D_MODEL)
    y_sample = y_sample.reshape(bs, dec_seq, D_MODEL)
    return (y_prompt, y_sample, hist_p[None], s_p[None], hist_s[None], s_s[None])
```

```python
import functools

import jax
import jax.numpy as jnp
from jax import lax
from jax.experimental import pallas as pl
from jax.experimental.pallas import tpu as pltpu
from jax.experimental.pallas import tpu_sc as plsc

F32 = jnp.float32
BF16 = jnp.bfloat16

D_MODEL = 1024
CHUNK = 64
PAST_LEN = 1024
POOL_WIDTH = 512
POOL_WINDOWS = (2, 4, 8, 16)
POOL_GROUP = 128
POOL_HIST = 15
GLA_HEADS = 4
GLA_DK = 64
GLA_DV = 128
GATE_RANK = 16
GATE_NORMALIZER = 16.0
NUM_EXPERTS = 32
TOP_K = 4
EXPERT_FF = 1024
SWIGLU_LIMIT = 7.0
SWIGLU_ALPHA = 1.702
LN_EPS = 1e-5
RMS_EPS = 1e-6
ALPHA = 2.0 ** 0.25

Q0 = POOL_WIDTH
K0 = Q0 + GLA_HEADS * GLA_DK
V0 = K0 + GLA_HEADS * GLA_DK
R0 = V0 + GLA_HEADS * GLA_DV
N_MAIN = R0 + GLA_HEADS * GLA_DV
N_IN = N_MAIN + GATE_RANK

LANES = 128
TILE_TOKENS = 512
CHUNKS_PER_TILE = TILE_TOKENS // CHUNK
HIST_PAD = 16
OUT_CHUNKS = 4
SCAN_ROWS = 256
MOE_ROWS = 512
ROUTE_TOKENS = 1024
COMBINE_TOKENS = 512
VMEM_LIMIT = 56 * 1024 * 1024


def _dot(a, b):
    return jnp.dot(a, b, preferred_element_type=F32)


def _dot_nt(a, b):
    return lax.dot_general(a, b, (((1,), (1,)), ((), ())), preferred_element_type=F32)


def _dot_tn(a, b):
    return lax.dot_general(a, b, (((0,), (0,)), ((), ())), preferred_element_type=F32)


HALF = D_MODEL // 2
HI_MASK = 0xFFFF0000


def _pack_bf16_pairs(xb):
    lo = lax.bitcast_convert_type(xb[:, :HALF].astype(F32), jnp.uint32) >> 16
    hi = lax.bitcast_convert_type(xb[:, HALF:].astype(F32), jnp.uint32) & jnp.uint32(HI_MASK)
    return lax.bitcast_convert_type(hi | lo, jnp.int32)


def _unpack_bf16_pairs(p):
    u = lax.bitcast_convert_type(p, jnp.uint32)
    lo = lax.bitcast_convert_type(u << 16, F32)
    hi = lax.bitcast_convert_type(u & jnp.uint32(HI_MASK), F32)
    return jnp.concatenate([lo, hi], axis=1).astype(BF16)


def _layer_norm(v, g, b):
    mu = jnp.mean(v, axis=-1, keepdims=True)
    c = v - mu
    var = jnp.mean(c * c, axis=-1, keepdims=True)
    return c * lax.rsqrt(var + LN_EPS) * g + b


N_MIXER_WEIGHTS = 12


def _mixer_kernel(per_chunk_state, pos0, *refs):
    if per_chunk_state:
        (x_ref, hist_in_ref, s_in_ref, *rest) = refs
    else:
        (x_ref, *rest) = refs
        hist_in_ref = s_in_ref = None
    (w_main_ref, w_glr_ref, w_gate_ref, b_gate_ref, w_pool_ref, pscale_ref, gnorm_ref,
     w_out_ref, ln1g_ref, ln1b_ref, w_router_ref, b_router_ref, *rest) = rest
    (h_ref, hb_ref, logits_ref, hist_out_ref, s_out_ref,
     proj_scr, b_scr, o_scr, ext_scr, st_scr, tri_scr, w_main_bf, w_out_bf) = rest

    if per_chunk_state:
        t = None
        first_step = pl.program_id(0) == 0
    else:
        t = pl.program_id(1)
        first_step = (pl.program_id(0) == 0) & (t == 0)
    x = x_ref[...].reshape(TILE_TOKENS, D_MODEL)
    xb = x.astype(BF16)

    @pl.when(first_step)
    def _():
        ti = lax.broadcasted_iota(jnp.int32, (SCAN_ROWS, SCAN_ROWS), 0)
        tj = lax.broadcasted_iota(jnp.int32, (SCAN_ROWS, SCAN_ROWS), 1)
        same_chunk = (ti // CHUNK) == (tj // CHUNK)
        tri_scr[...] = jnp.where(same_chunk & (ti >= tj), 1.0, 0.0).astype(BF16)
        w_main_bf[...] = w_main_ref[:, 0:N_MAIN].astype(BF16)
        w_out_bf[...] = w_out_ref[...].astype(BF16)
        if not per_chunk_state:
            st_scr[...] = jnp.zeros_like(st_scr)
            ext_scr[0:HIST_PAD, :] = jnp.zeros((HIST_PAD, POOL_WIDTH), F32)

    glr = _dot(xb, w_glr_ref[...])
    proj_scr[:, 0:V0] = _dot(xb, w_main_bf[:, 0:V0])
    gk = _dot(glr.astype(BF16), w_gate_ref[...]) + b_gate_ref[...]
    log_sig = jnp.minimum(gk, 0.0) - jnp.log1p(jnp.exp(-jnp.abs(gk)))
    g = log_sig / GATE_NORMALIZER
    g_hi = g.astype(BF16)
    g_lo = (g - g_hi.astype(F32)).astype(BF16)
    proj_scr[:, V0:N_MAIN] = _dot(xb, w_main_bf[:, V0:N_MAIN])
    for s in range(TILE_TOKENS // SCAN_ROWS):
        rs = slice(s * SCAN_ROWS, (s + 1) * SCAN_ROWS)
        b_scr[rs, :] = _dot(tri_scr[...], g_hi[rs]) + _dot(tri_scr[...], g_lo[rs])

    if per_chunk_state:
        seg_len, seg_stride, n_seg = CHUNK, CHUNK + HIST_PAD, CHUNKS_PER_TILE
        for c in range(n_seg):
            base = c * seg_stride
            ext_scr[base:base + HIST_PAD, :] = jnp.zeros((HIST_PAD, POOL_WIDTH), F32)
            ext_scr[base + 1:base + HIST_PAD, :] = hist_in_ref[c]
            ext_scr[base + HIST_PAD:base + seg_stride, :] = proj_scr[c * CHUNK:(c + 1) * CHUNK, 0:POOL_WIDTH]
        row_pos = pos0 + lax.broadcasted_iota(jnp.int32, (seg_len, POOL_GROUP), 0)
    else:
        seg_len, seg_stride, n_seg = TILE_TOKENS, TILE_TOKENS + HIST_PAD, 1
        ext_scr[0:HIST_PAD, :] = jnp.where(t == 0, 0.0, ext_scr[0:HIST_PAD, :])
        ext_scr[HIST_PAD:seg_stride, :] = proj_scr[:, 0:POOL_WIDTH]
        row_pos = pos0 + t * TILE_TOKENS + lax.broadcasted_iota(jnp.int32, (seg_len, POOL_GROUP), 0)

    pooled_groups = []
    for gi, w in enumerate(POOL_WINDOWS):
        gs = slice(gi * POOL_GROUP, (gi + 1) * POOL_GROUP)
        cnt = jnp.minimum(row_pos + 1, w).astype(F32)
        ext = ext_scr[:, gs]
        win = ext
        shift = 1
        while shift < w:
            win = win + pltpu.roll(win, shift, 0)
            shift *= 2
        segs = []
        for s in range(n_seg):
            base = s * seg_stride + HIST_PAD
            segs.append(win[base:base + seg_len] / cnt - ext[base:base + seg_len])
        pooled = segs[0] if n_seg == 1 else jnp.concatenate(segs, axis=0)
        pooled_groups.append(pooled.astype(BF16))
    pool_cols = []
    for p in range(len(POOL_WINDOWS) // 2):
        both = jnp.concatenate(pooled_groups[2 * p:2 * p + 2], axis=1)
        pool_cols.append(_dot(both, w_pool_ref[p]))
    pool_out = jnp.concatenate(pool_cols, axis=1) * pscale_ref[...]

    if per_chunk_state:
        for c in range(n_seg):
            end = (c + 1) * seg_stride
            hist_out_ref[c] = ext_scr[end - POOL_HIST:end, :]
    else:
        hist_out_ref[...] = ext_scr[seg_stride - POOL_HIST:seg_stride, :]
        ext_scr[0:HIST_PAD, :] = ext_scr[TILE_TOKENS:seg_stride, :]

    hk = GLA_HEADS * GLA_DK
    hv = GLA_HEADS * GLA_DV
    pair_rows = 2 * CHUNK
    decay_cols = LANES // CHUNKS_PER_TILE

    def head_of(shape, dim, width):
        return lax.broadcasted_iota(jnp.int32, shape, dim) // width

    same_head_k = head_of((hk, hk), 0, CHUNK) == head_of((hk, hk), 1, GLA_DK)
    same_head_v = head_of((hk, hv), 0, CHUNK) == head_of((hk, hv), 1, GLA_DV)
    pair_half = head_of((pair_rows, hv), 0, CHUNK)
    causal = (lax.broadcasted_iota(jnp.int32, (CHUNK, hk), 0)
              >= lax.broadcasted_iota(jnp.int32, (CHUNK, hk), 1) % CHUNK)

    b_all = b_scr[...]
    b_last = [b_scr[(c + 1) * CHUNK - 1:(c + 1) * CHUNK, :] for c in range(CHUNKS_PER_TILE)]
    b_last_rows = jnp.concatenate([jnp.broadcast_to(bl, (CHUNK, hk)) for bl in b_last], axis=0)
    k_all = proj_scr[:, K0:V0]
    qt_all = (proj_scr[:, Q0:K0] * (GLA_DK ** -0.5) * jnp.exp(b_all)).astype(BF16)
    kt_all = k_all * jnp.exp(-b_all)
    kl_t = jnp.transpose(k_all * jnp.exp(b_last_rows - b_all)).astype(BF16)
    decay_t = jnp.transpose(jnp.exp(jnp.concatenate(
        [jnp.broadcast_to(bl, (decay_cols, hk)) for bl in b_last], axis=0)))

    def finish_rows(rs):
        r = proj_scr[rs, R0:N_MAIN]
        silu_r = r * (1.0 / (1.0 + jnp.exp(-r)))
        gated = []
        for h in range(GLA_HEADS):
            vs = slice(h * GLA_DV, (h + 1) * GLA_DV)
            oh = o_scr[rs, vs]
            ms = jnp.mean(oh * oh, axis=-1, keepdims=True)
            gated.append(oh * lax.rsqrt(ms + RMS_EPS) * gnorm_ref[...] * silu_r[:, vs])
        mix_in = jnp.concatenate([pool_out[rs]] + gated, axis=1).astype(BF16)
        resid = ALPHA * x[rs] + _dot(mix_in, w_out_bf[...])
        h_val = _layer_norm(resid, ln1g_ref[...], ln1b_ref[...])
        h_ref[rs, :] = h_val
        hb = h_val.astype(BF16)
        hb_ref[rs, :] = _pack_bf16_pairs(hb)
        logits_ref[:, rs] = _dot_nt(w_router_ref[...], hb) + b_router_ref[:, 0:1]

    st = None if per_chunk_state else jnp.where(t == 0, 0.0, st_scr[...])
    for c in range(CHUNKS_PER_TILE):
        rows = slice(c * CHUNK, (c + 1) * CHUNK)
        pair = slice((c // 2) * pair_rows, (c // 2 + 1) * pair_rows)
        if per_chunk_state:
            st = s_in_ref[c].reshape(hk, GLA_DV)
        qt = qt_all[rows]
        zero = jnp.zeros((), BF16)
        k_stack = jnp.where(same_head_k, jnp.concatenate([kt_all[rows].astype(BF16)] * GLA_HEADS, axis=0), zero)
        v_stack = jnp.where(same_head_v, jnp.concatenate(
            [proj_scr[rows, V0:R0].astype(BF16)] * GLA_HEADS, axis=0), zero)
        s_stack = jnp.where(same_head_v, jnp.concatenate([st.astype(BF16)] * GLA_HEADS, axis=1), zero)
        att = jnp.where(causal, _dot_nt(qt, k_stack), 0.0)
        o_scr[rows, :] = _dot(att.astype(BF16), v_stack) + _dot(qt, s_stack)
        v_chunk = jnp.where(pair_half == c % 2, proj_scr[pair, V0:R0], 0.0).astype(BF16)
        upd = jnp.concatenate(
            [_dot(kl_t[h * GLA_DK:(h + 1) * GLA_DK, pair], v_chunk[:, h * GLA_DV:(h + 1) * GLA_DV])
             for h in range(GLA_HEADS)], axis=0)
        st = st * decay_t[:, c * decay_cols:c * decay_cols + 1] + upd
        if per_chunk_state:
            s_out_ref[c] = st.reshape(GLA_HEADS, GLA_DK, GLA_DV)
        if (c + 1) % OUT_CHUNKS == 0:
            finish_rows(slice((c + 1 - OUT_CHUNKS) * CHUNK, (c + 1) * CHUNK))

    if not per_chunk_state:
        st_scr[...] = st
        s_out_ref[...] = st.reshape(GLA_HEADS, GLA_DK, GLA_DV)


def _const_spec(shape, single_buffer=False):
    nd = len(shape)
    if single_buffer:
        return pl.BlockSpec(shape, lambda *_: (0,) * nd, pipeline_mode=pl.Buffered(1))
    return pl.BlockSpec(shape, lambda *_: (0,) * nd)


def _mixer_weight_specs():
    return [
        _const_spec((None, D_MODEL, N_IN), single_buffer=True),
        _const_spec((D_MODEL, LANES)),
        _const_spec((LANES, GLA_HEADS * GLA_DK)),
        _const_spec((1, GLA_HEADS * GLA_DK)),
        _const_spec((len(POOL_WINDOWS) // 2, 2 * POOL_GROUP, 2 * POOL_GROUP)),
        _const_spec((1, POOL_WIDTH)),
        _const_spec((1, GLA_DV)),
        _const_spec((None, D_MODEL, D_MODEL), single_buffer=True),
        _const_spec((1, D_MODEL)),
        _const_spec((1, D_MODEL)),
        _const_spec((NUM_EXPERTS, D_MODEL)),
        _const_spec((NUM_EXPERTS, LANES)),
    ]


def _mixer_weights(w_in, w_pool, pool_scale, w_gate_up, b_gate, gla_norm_w, w_out, ln1_g, ln1_b,
                   w_router, b_router):
    w_glr = jnp.zeros((D_MODEL, LANES), BF16).at[:, :GATE_RANK].set(w_in[0, :, N_MAIN:].astype(BF16))
    w_gate = jnp.zeros((LANES, GLA_HEADS * GLA_DK), BF16).at[:GATE_RANK].set(w_gate_up[0].astype(BF16))
    wp = w_pool[0].astype(BF16)
    zero = jnp.zeros((POOL_GROUP, POOL_GROUP), BF16)
    w_pool_pairs = jnp.stack([jnp.block([[wp[2 * p], zero], [zero, wp[2 * p + 1]]])
                              for p in range(len(POOL_WINDOWS) // 2)])
    weights = (
        w_in, w_glr, w_gate, b_gate[0][None, :],
        w_pool_pairs, pool_scale[0][None, :], gla_norm_w[0][None, :],
        w_out, ln1_g[0][None, :], ln1_b[0][None, :],
        w_router[0].T.astype(BF16), jnp.broadcast_to(b_router[0][:, None], (NUM_EXPERTS, LANES)),
    )
    assert len(weights) == N_MIXER_WEIGHTS
    return weights


def _mixer_scratch(per_chunk_state):
    ext_rows = (CHUNKS_PER_TILE * (CHUNK + HIST_PAD)) if per_chunk_state else (TILE_TOKENS + HIST_PAD)
    return [
        pltpu.VMEM((TILE_TOKENS, N_MAIN), F32),
        pltpu.VMEM((TILE_TOKENS, GLA_HEADS * GLA_DK), F32),
        pltpu.VMEM((TILE_TOKENS, GLA_HEADS * GLA_DV), F32),
        pltpu.VMEM((ext_rows, POOL_WIDTH), F32),
        pltpu.VMEM((GLA_HEADS * GLA_DK, GLA_DV), F32),
        pltpu.VMEM((SCAN_ROWS, SCAN_ROWS), BF16),
        pltpu.VMEM((D_MODEL, N_MAIN), BF16),
        pltpu.VMEM((D_MODEL, D_MODEL), BF16),
    ]


def _mixer_out_shapes(n, bsz):
    return (
        jax.ShapeDtypeStruct((n, D_MODEL), F32),
        jax.ShapeDtypeStruct((n, HALF), jnp.int32),
        jax.ShapeDtypeStruct((NUM_EXPERTS, n), F32),
        jax.ShapeDtypeStruct((bsz, POOL_HIST, POOL_WIDTH), F32),
        jax.ShapeDtypeStruct((bsz, GLA_HEADS, GLA_DK, GLA_DV), F32),
    )


def _mixer_prompt(x, weights):
    bsz, seq, _ = x.shape
    tiles = seq // TILE_TOKENS
    n_total = bsz * seq
    return pl.pallas_call(
        functools.partial(_mixer_kernel, False, 0),
        grid=(bsz, tiles),
        in_specs=[pl.BlockSpec((None, TILE_TOKENS, D_MODEL), lambda b, t: (b, t, 0))] + _mixer_weight_specs(),
        out_specs=(
            pl.BlockSpec((TILE_TOKENS, D_MODEL), lambda b, t: (b * tiles + t, 0)),
            pl.BlockSpec((TILE_TOKENS, HALF), lambda b, t: (b * tiles + t, 0)),
            pl.BlockSpec((NUM_EXPERTS, TILE_TOKENS), lambda b, t: (0, b * tiles + t)),
            pl.BlockSpec((None, POOL_HIST, POOL_WIDTH), lambda b, t: (b, 0, 0)),
            pl.BlockSpec((None, GLA_HEADS, GLA_DK, GLA_DV), lambda b, t: (b, 0, 0, 0)),
        ),
        out_shape=_mixer_out_shapes(n_total, bsz),
        scratch_shapes=_mixer_scratch(False),
        compiler_params=pltpu.CompilerParams(
            dimension_semantics=("arbitrary", "arbitrary"), vmem_limit_bytes=VMEM_LIMIT),
        name="mixer_prompt",
    )(x, *weights)


def _mixer_sample(x, hist, state, weights):
    bsz = x.shape[0]
    tiles = bsz // CHUNKS_PER_TILE
    return pl.pallas_call(
        functools.partial(_mixer_kernel, True, PAST_LEN),
        grid=(tiles,),
        in_specs=[
            pl.BlockSpec((CHUNKS_PER_TILE, CHUNK, D_MODEL), lambda i: (i, 0, 0)),
            pl.BlockSpec((CHUNKS_PER_TILE, POOL_HIST, POOL_WIDTH), lambda i: (i, 0, 0)),
            pl.BlockSpec((CHUNKS_PER_TILE, GLA_HEADS, GLA_DK, GLA_DV), lambda i: (i, 0, 0, 0)),
        ] + _mixer_weight_specs(),
        out_specs=(
            pl.BlockSpec((TILE_TOKENS, D_MODEL), lambda i: (i, 0)),
            pl.BlockSpec((TILE_TOKENS, HALF), lambda i: (i, 0)),
            pl.BlockSpec((NUM_EXPERTS, TILE_TOKENS), lambda i: (0, i)),
            pl.BlockSpec((CHUNKS_PER_TILE, POOL_HIST, POOL_WIDTH), lambda i: (i, 0, 0)),
            pl.BlockSpec((CHUNKS_PER_TILE, GLA_HEADS, GLA_DK, GLA_DV), lambda i: (i, 0, 0, 0)),
        ),
        out_shape=_mixer_out_shapes(bsz * CHUNK, bsz),
        scratch_shapes=_mixer_scratch(True),
        compiler_params=pltpu.CompilerParams(
            dimension_semantics=("arbitrary",), vmem_limit_bytes=VMEM_LIMIT),
        name="mixer_sample",
    )(x, hist, state, *weights)


def _router_kernel(lt_ref, gates_ref, dest_ref, padend_ref, cnt_scr, base_scr, pstart_scr, before_scr,
                   topk_scr):
    phase = pl.program_id(0)
    i = pl.program_id(1)
    shape = (NUM_EXPERTS, ROUTE_TOKENS)
    row = lax.broadcasted_iota(jnp.int32, shape, 0)

    def tile_counts_of(chosen):
        return jnp.broadcast_to(jnp.sum(chosen, axis=1, keepdims=True), (NUM_EXPERTS, LANES))

    @pl.when(phase == 0)
    def _():
        @pl.when(i == 0)
        def _():
            cnt_scr[...] = jnp.zeros_like(cnt_scr)

        logits = lt_ref[...]
        idxs, vals = [], []
        chosen = jnp.zeros(shape, F32)
        for _ in range(TOP_K):
            m = jnp.max(logits, axis=0, keepdims=True)
            idx = jnp.min(jnp.where(logits == m, row, NUM_EXPERTS), axis=0, keepdims=True)
            hit = row == idx
            idxs.append(idx.astype(F32))
            vals.append(m)
            chosen = chosen + jnp.where(hit, 1.0, 0.0)
            logits = jnp.where(hit, -jnp.inf, logits)
        topk_scr[i] = jnp.concatenate(idxs + vals, axis=0)
        cnt_scr[...] += tile_counts_of(chosen)

    @pl.when(phase == 1)
    def _():
        @pl.when(i == 0)
        def _():
            blocks = jnp.floor((cnt_scr[...] + (MOE_ROWS - 1)) * (1.0 / MOE_ROWS))
            erow = lax.broadcasted_iota(jnp.int32, (NUM_EXPERTS, LANES), 0)
            lane = lax.broadcasted_iota(jnp.int32, (NUM_EXPERTS, LANES), 1)
            cum = blocks
            shift = 1
            while shift < NUM_EXPERTS:
                cum = cum + jnp.where(erow >= shift, pltpu.roll(cum, shift, 0), 0.0)
                shift *= 2
            padend_ref[...] = jnp.where(lane == 1, cnt_scr[...], cum * MOE_ROWS)
            pstart_scr[...] = (cum - blocks) * MOE_ROWS
            base_scr[...] = jnp.zeros_like(base_scr)
            ti = lax.broadcasted_iota(jnp.int32, (ROUTE_TOKENS, ROUTE_TOKENS), 0)
            tj = lax.broadcasted_iota(jnp.int32, (ROUTE_TOKENS, ROUTE_TOKENS), 1)
            before_scr[...] = jnp.where(ti < tj, 1.0, 0.0).astype(BF16)

        topk = topk_scr[i]
        sel = [row == topk[k:k + 1, :].astype(jnp.int32) for k in range(TOP_K)]
        vals = [topk[TOP_K + k:TOP_K + k + 1, :] for k in range(TOP_K)]
        chosen = sum(jnp.where(hit, 1.0, 0.0) for hit in sel)
        earlier = _dot(chosen.astype(BF16), before_scr[...])
        pos = pstart_scr[:, 0:1] + base_scr[:, 0:1] + earlier
        dest = [jnp.sum(jnp.where(hit, pos, 0.0), axis=0, keepdims=True) for hit in sel]
        dest_ref[...] = jnp.concatenate(dest, axis=0).astype(jnp.int32)
        ex = [jnp.exp(v - vals[0]) for v in vals]
        denom = ex[0] + ex[1] + ex[2] + ex[3]
        gates_ref[...] = jnp.concatenate([e / denom for e in ex], axis=0)
        base_scr[...] += tile_counts_of(chosen)


def _router(logits_t):
    n = logits_t.shape[1]
    assert n % ROUTE_TOKENS == 0
    tiles = n // ROUTE_TOKENS
    return pl.pallas_call(
        _router_kernel,
        grid=(2, tiles),
        in_specs=[pl.BlockSpec((NUM_EXPERTS, ROUTE_TOKENS), lambda p, i: (0, i))],
        out_specs=(
            pl.BlockSpec((TOP_K, ROUTE_TOKENS), lambda p, i: (0, i * p)),
            pl.BlockSpec((TOP_K, ROUTE_TOKENS), lambda p, i: (0, i * p)),
            pl.BlockSpec((NUM_EXPERTS, LANES), lambda p, i: (0, 0)),
        ),
        out_shape=(
            jax.ShapeDtypeStruct((TOP_K, n), F32),
            jax.ShapeDtypeStruct((TOP_K, n), jnp.int32),
            jax.ShapeDtypeStruct((NUM_EXPERTS, LANES), F32),
        ),
        scratch_shapes=[pltpu.VMEM((NUM_EXPERTS, LANES), F32)] * 3
        + [pltpu.VMEM((ROUTE_TOKENS, ROUTE_TOKENS), BF16),
           pltpu.VMEM((tiles, 2 * TOP_K, ROUTE_TOKENS), F32)],
        compiler_params=pltpu.CompilerParams(
            dimension_semantics=("arbitrary", "arbitrary"), vmem_limit_bytes=VMEM_LIMIT),
        name="router",
    )(logits_t)


SC_CORES = 2
SC_SUBCORES = 16
SC_WORKERS = SC_CORES * SC_SUBCORES
DISPATCH_ROWS = 64


def _dispatch(h_sources, dest_chunks, m_pad):
    shares, first_chunk = [], 0
    for src in h_sources:
        n_chunks = src.shape[0] // DISPATCH_ROWS
        assert src.shape[0] % DISPATCH_ROWS == 0 and n_chunks % SC_WORKERS == 0
        shares.append((first_chunk, n_chunks // SC_WORKERS))
        first_chunk += n_chunks
    plan = [(s, first, per, j) for s, (first, per) in enumerate(shares) for j in range(per)]
    per_worker = len(plan)
    mesh = plsc.VectorSubcoreMesh(core_axis_name="c", subcore_axis_name="s")

    @functools.partial(
        pl.kernel, mesh=mesh,
        out_type=jax.ShapeDtypeStruct((m_pad, HALF), jnp.int32),
        scratch_types=[
            pltpu.VMEM((2, TOP_K, DISPATCH_ROWS), jnp.int32),
            pltpu.VMEM((2, DISPATCH_ROWS, HALF), jnp.int32),
            pltpu.SemaphoreType.DMA((2,)),
            pltpu.SemaphoreType.DMA((2,)),
        ],
        compiler_params=pltpu.CompilerParams(use_tc_tiling_on_sc=True),
        name="dispatch",
    )
    def dispatch_kernel(*refs):
        h_hbms = refs[:len(h_sources)]
        dest_hbm, out_hbm, idx_v, rows_v, load_sems, scatter_sems = refs[len(h_sources):]
        wid = lax.axis_index("s") * SC_CORES + lax.axis_index("c")

        def loads(j):
            src, first, per, k = plan[j]
            local = wid * per + k
            slot = j % 2
            return (
                pltpu.make_async_copy(dest_hbm.at[first + local], idx_v.at[slot], load_sems.at[slot]),
                pltpu.make_async_copy(h_hbms[src].at[pl.ds(local * DISPATCH_ROWS, DISPATCH_ROWS)],
                                      rows_v.at[slot], load_sems.at[slot]),
            )

        def scatters(j):
            slot = j % 2
            return [pltpu.make_async_copy(rows_v.at[slot], out_hbm.at[idx_v.at[slot, k]],
                                          scatter_sems.at[slot]) for k in range(TOP_K)]

        for cp in loads(0):
            cp.start()
        for j in range(per_worker):
            for cp in loads(j):
                cp.wait()
            if j >= 1:
                for cp in scatters(j - 1):
                    cp.wait()
            if j + 1 < per_worker:
                for cp in loads(j + 1):
                    cp.start()
            for cp in scatters(j):
                cp.start()
        for cp in scatters(per_worker - 1):
            cp.wait()

    return dispatch_kernel(*h_sources, dest_chunks)


def _gather_expert_rows(y_sorted, dest_chunks, row_offset, n):
    n_chunks = n // DISPATCH_ROWS
    assert n_chunks % SC_WORKERS == 0 and row_offset % DISPATCH_ROWS == 0
    per_worker = n_chunks // SC_WORKERS
    chunk0 = row_offset // DISPATCH_ROWS
    mesh = plsc.VectorSubcoreMesh(core_axis_name="c", subcore_axis_name="s")

    @functools.partial(
        pl.kernel, mesh=mesh,
        out_type=jax.ShapeDtypeStruct((TOP_K, n, HALF), jnp.int32),
        scratch_types=[
            pltpu.VMEM((TOP_K, DISPATCH_ROWS), jnp.int32),
            pltpu.VMEM((2, DISPATCH_ROWS, HALF), jnp.int32),
            pltpu.SemaphoreType.DMA((2,)),
        ],
        compiler_params=pltpu.CompilerParams(use_tc_tiling_on_sc=True),
        name="gather_expert_rows",
    )
    def gather_kernel(y_hbm, dest_hbm, out_hbm, idx_v, rows_v, sems):
        wid = lax.axis_index("s") * SC_CORES + lax.axis_index("c")

        def gather(k):
            return pltpu.make_async_copy(y_hbm.at[idx_v.at[k]], rows_v.at[k % 2], sems.at[k % 2])

        @pl.loop(0, per_worker)
        def _(j):
            local = wid * per_worker + j
            pltpu.sync_copy(dest_hbm.at[chunk0 + local], idx_v)
            gather(0).start()
            for k in range(TOP_K):
                if k + 1 < TOP_K:
                    gather(k + 1).start()
                gather(k).wait()
                pltpu.sync_copy(rows_v.at[k % 2],
                                out_hbm.at[k, pl.ds(local * DISPATCH_ROWS, DISPATCH_ROWS)])

    return gather_kernel(y_sorted, dest_chunks)


def _moe_kernel(be_ref, wsel_ref, half_ref, nused_ref, x_ref, wgu_ref, bgu_ref, wd_ref, bd_ref, y_ref,
                wgu_bf, wd_bf):
    del wsel_ref
    i = pl.program_id(0)

    def ffn(rows):
        gu = _dot(_unpack_bf16_pairs(x_ref[rows, :]), wgu_bf[...]) + bgu_ref[...]
        gate = jnp.minimum(gu[:, :EXPERT_FF], SWIGLU_LIMIT)
        up = jnp.clip(gu[:, EXPERT_FF:], -SWIGLU_LIMIT, SWIGLU_LIMIT)
        hmid = gate * (1.0 / (1.0 + jnp.exp(-SWIGLU_ALPHA * gate))) * (up + 1.0)
        y = _dot(hmid.astype(BF16), wd_bf[...]) + bd_ref[...]
        y_ref[rows, :] = _pack_bf16_pairs(y.astype(BF16))

    @pl.when(i < nused_ref[0])
    def _():
        @pl.when((i == 0) | (be_ref[i] != be_ref[jnp.maximum(i - 1, 0)]))
        def _():
            wgu_bf[...] = wgu_ref[...].astype(BF16)
            wd_bf[...] = wd_ref[...].astype(BF16)

        @pl.when(half_ref[i] == 0)
        def _():
            ffn(slice(0, MOE_ROWS))

        @pl.when(half_ref[i] != 0)
        def _():
            ffn(slice(0, MOE_ROWS // 2))


def _moe_experts(block_expert, weight_expert, half_block, n_used, x_sorted, w_gu, b_gu, w_down, b_down):
    m_pad = x_sorted.shape[0]
    n_blocks = m_pad // MOE_ROWS

    def blk(i, be, ws, hb, nu):
        return jnp.minimum(i, nu[0] - 1)

    def expert(i, be, ws, hb, nu):
        return be[blk(i, be, ws, hb, nu)]

    def held(i, be, ws, hb, nu):
        return ws[blk(i, be, ws, hb, nu)]

    grid_spec = pltpu.PrefetchScalarGridSpec(
        num_scalar_prefetch=4,
        grid=(n_blocks,),
        in_specs=[
            pl.BlockSpec((MOE_ROWS, HALF), lambda *a: (blk(*a), 0)),
            pl.BlockSpec((None, D_MODEL, 2 * EXPERT_FF), lambda *a: (held(*a), 0, 0)),
            pl.BlockSpec((None, 1, 2 * EXPERT_FF), lambda *a: (expert(*a), 0, 0)),
            pl.BlockSpec((None, EXPERT_FF, D_MODEL), lambda *a: (held(*a), 0, 0)),
            pl.BlockSpec((None, 1, D_MODEL), lambda *a: (expert(*a), 0, 0)),
        ],
        out_specs=pl.BlockSpec((MOE_ROWS, HALF), lambda *a: (blk(*a), 0)),
        scratch_shapes=[
            pltpu.VMEM((D_MODEL, 2 * EXPERT_FF), BF16),
            pltpu.VMEM((EXPERT_FF, D_MODEL), BF16),
        ],
    )
    return pl.pallas_call(
        _moe_kernel,
        grid_spec=grid_spec,
        out_shape=jax.ShapeDtypeStruct((m_pad, HALF), jnp.int32),
        compiler_params=pltpu.CompilerParams(
            dimension_semantics=("arbitrary",), vmem_limit_bytes=VMEM_LIMIT),
        name="moe_experts",
    )(block_expert, weight_expert, half_block, n_used, x_sorted, w_gu, b_gu, w_down, b_down)


def _combine_kernel(yk_ref, gates_ref, h_ref, g_ref, b_ref, out_ref):
    pad = jnp.zeros((LANES - TOP_K, COMBINE_TOKENS), F32)
    gates = jnp.transpose(jnp.concatenate([gates_ref[...], pad], axis=0))
    lo = hi = None
    for k in range(TOP_K):
        u = lax.bitcast_convert_type(yk_ref[k], jnp.uint32)
        gk = gates[:, k:k + 1]
        lo_k = lax.bitcast_convert_type(u << 16, F32) * gk
        hi_k = lax.bitcast_convert_type(u & jnp.uint32(HI_MASK), F32) * gk
        lo = lo_k if lo is None else lo + lo_k
        hi = hi_k if hi is None else hi + hi_k
    acc = ALPHA * h_ref[...] + jnp.concatenate([lo, hi], axis=1)
    out_ref[...] = _layer_norm(acc, g_ref[...], b_ref[...])


def _combine_kernel_aliased(yk_ref, gates_ref, h_ref, g_ref, b_ref, prev_ref, out_ref):
    del prev_ref
    _combine_kernel(yk_ref, gates_ref, h_ref, g_ref, b_ref, out_ref)


def _combine(yk, gates, token_offset, h_src, h_offset, ln_g, ln_b, out_prev):
    n_seg = yk.shape[1]
    out_rows = h_src.shape[0]
    tile0 = token_offset // COMBINE_TOKENS
    out_tile0 = h_offset // COMBINE_TOKENS
    in_specs = [
        pl.BlockSpec((TOP_K, COMBINE_TOKENS, HALF), lambda i: (0, i, 0)),
        pl.BlockSpec((TOP_K, COMBINE_TOKENS), lambda i: (0, tile0 + i)),
        pl.BlockSpec((COMBINE_TOKENS, D_MODEL), lambda i: (out_tile0 + i, 0)),
        _const_spec((1, D_MODEL)),
        _const_spec((1, D_MODEL)),
    ]
    args = [yk, gates, h_src, ln_g, ln_b]
    aliases = {}
    kern = _combine_kernel
    if out_prev is not None:
        in_specs.append(pl.BlockSpec(memory_space=pl.ANY))
        aliases = {len(args): 0}
        args.append(out_prev)
        kern = _combine_kernel_aliased
    return pl.pallas_call(
        kern,
        grid=(n_seg // COMBINE_TOKENS,),
        in_specs=in_specs,
        out_specs=pl.BlockSpec((COMBINE_TOKENS, D_MODEL), lambda i: (out_tile0 + i, 0)),
        out_shape=jax.ShapeDtypeStruct((out_rows, D_MODEL), F32),
        input_output_aliases=aliases,
        compiler_params=pltpu.CompilerParams(
            dimension_semantics=("arbitrary",), vmem_limit_bytes=VMEM_LIMIT),
        name="moe_combine",
    )(*args)


def kernel(x_prompt, x_sample, state_pool, state_gla, w_in, w_pool, pool_scale, w_gate_up, b_gate,
           gla_norm_w, w_out, ln1_g, ln1_b, w_router, b_router, w_gu, b_gu, w_down, b_down,
           ln2_g, ln2_b):
    assert w_in.shape[0] == 1, "single-layer kernel"
    bp, seq, _ = x_prompt.shape
    bs, dec_seq, _ = x_sample.shape
    assert dec_seq == CHUNK and seq % TILE_TOKENS == 0 and bs % CHUNKS_PER_TILE == 0
    n_prompt = bp * seq
    n_sample = bs * dec_seq
    n_total = n_prompt + n_sample
    nk = n_total * TOP_K
    n_blocks = -(-nk // MOE_ROWS) + NUM_EXPERTS
    m_pad = n_blocks * MOE_ROWS

    weights = _mixer_weights(w_in, w_pool, pool_scale, w_gate_up, b_gate, gla_norm_w, w_out,
                             ln1_g, ln1_b, w_router, b_router)

    h_p, hb_p, logits_p, hist_p, s_p = _mixer_prompt(x_prompt, weights)
    h_s, hb_s, logits_s, hist_s, s_s = _mixer_sample(x_sample, state_pool[0], state_gla[0], weights)

    gates_t, dest_t, layout = _router(jnp.concatenate([logits_p, logits_s], axis=1))
    pad_end = layout[:, 0].astype(jnp.int32)
    counts = layout[:, 1].astype(jnp.int32)
    block_start = jnp.arange(n_blocks, dtype=jnp.int32) * MOE_ROWS
    block_expert = jnp.minimum(jnp.sum((block_start[:, None] >= pad_end[None, :]).astype(jnp.int32), axis=1),
                               NUM_EXPERTS - 1)
    n_used = (pad_end[-1:] // MOE_ROWS).astype(jnp.int32)
    is_first = jnp.concatenate([jnp.ones((1,), bool), block_expert[1:] != block_expert[:-1]])
    blocks = jnp.arange(n_blocks, dtype=jnp.int32)
    later_other = ((blocks[None, :] > blocks[:, None]) & (blocks[None, :] < n_used[0])
                   & (block_expert[None, :] != block_expert[:, None]))
    next_expert = jnp.min(jnp.where(later_other, block_expert[None, :], NUM_EXPERTS), axis=1)
    next_expert = jnp.where(next_expert == NUM_EXPERTS, block_expert, next_expert)
    weight_expert = jnp.where(is_first, block_expert, next_expert).astype(jnp.int32)
    of_expert = block_expert[:, None] == jnp.arange(NUM_EXPERTS, dtype=jnp.int32)[None, :]
    seg_end = jnp.sum(jnp.where(of_expert, (pad_end - (-counts % MOE_ROWS))[None, :], 0), axis=1)
    half_block = ((seg_end - block_start) <= MOE_ROWS // 2).astype(jnp.int32)

    dest_chunks = dest_t.reshape(TOP_K, n_total // DISPATCH_ROWS, DISPATCH_ROWS).transpose(1, 0, 2)
    x_sorted = _dispatch((hb_p, hb_s), dest_chunks, m_pad)
    y_sorted = _moe_experts(block_expert, weight_expert, half_block, n_used, x_sorted, w_gu[0], b_gu[0][:, None, :],
                            w_down[0], b_down[0][:, None, :])
    gates = gates_t
    ln_g, ln_b = ln2_g[0][None, :], ln2_b[0][None, :]
    yk = _gather_expert_rows(y_sorted, dest_chunks, n_prompt, n_sample)
    y_sample = _combine(yk, gates, n_prompt, h_s, 0, ln_g, ln_b, None)
    unit = SC_WORKERS * DISPATCH_ROWS
    assert n_prompt % unit == 0
    sizes, left, size = [], n_prompt // unit, 1
    while left > 0:
        size = min(left, size)
        sizes.append(size * unit)
        left -= size
        size = max(size + 1, (3 * size) // 2)
    y_prompt, start = None, 0
    for seg in sizes:
        yk = _gather_expert_rows(y_sorted, dest_chunks, start, seg)
        y_prompt = _combine(yk, gates, start, h_p, start, ln_g, ln_b, y_prompt)
        start += seg
    y_prompt = y_prompt.reshape(bp, seq, D_MODEL)
    y_sample = y_sample.reshape(bs, dec_seq, D_MODEL)
    return (y_prompt, y_sample, hist_p[None], s_p[None], hist_s[None], s_s[None])
```

```python
import functools

import jax
import jax.numpy as jnp
from jax import lax
from jax.experimental import pallas as pl
from jax.experimental.pallas import tpu as pltpu
from jax.experimental.pallas import tpu_sc as plsc

F32 = jnp.float32
BF16 = jnp.bfloat16

D_MODEL = 1024
CHUNK = 64
PAST_LEN = 1024
POOL_WIDTH = 512
POOL_WINDOWS = (2, 4, 8, 16)
POOL_GROUP = 128
POOL_HIST = 15
GLA_HEADS = 4
GLA_DK = 64
GLA_DV = 128
GATE_RANK = 16
GATE_NORMALIZER = 16.0
NUM_EXPERTS = 32
TOP_K = 4
EXPERT_FF = 1024
SWIGLU_LIMIT = 7.0
SWIGLU_ALPHA = 1.702
LN_EPS = 1e-5
RMS_EPS = 1e-6
ALPHA = 2.0 ** 0.25

Q0 = POOL_WIDTH
K0 = Q0 + GLA_HEADS * GLA_DK
V0 = K0 + GLA_HEADS * GLA_DK
R0 = V0 + GLA_HEADS * GLA_DV
N_MAIN = R0 + GLA_HEADS * GLA_DV
N_IN = N_MAIN + GATE_RANK

LANES = 128
TILE_TOKENS = 512
CHUNKS_PER_TILE = TILE_TOKENS // CHUNK
HIST_PAD = 16
OUT_CHUNKS = 4
SCAN_ROWS = 256
MOE_ROWS = 768
ROUTE_TOKENS = 1024
COMBINE_TOKENS = 512
VMEM_LIMIT = 56 * 1024 * 1024


def _dot(a, b):
    return jnp.dot(a, b, preferred_element_type=F32)


def _dot_nt(a, b):
    return lax.dot_general(a, b, (((1,), (1,)), ((), ())), preferred_element_type=F32)


def _dot_tn(a, b):
    return lax.dot_general(a, b, (((0,), (0,)), ((), ())), preferred_element_type=F32)


HALF = D_MODEL // 2
HI_MASK = 0xFFFF0000


def _pack_bf16_pairs(xb):
    lo = lax.bitcast_convert_type(xb[:, :HALF].astype(F32), jnp.uint32) >> 16
    hi = lax.bitcast_convert_type(xb[:, HALF:].astype(F32), jnp.uint32) & jnp.uint32(HI_MASK)
    return lax.bitcast_convert_type(hi | lo, jnp.int32)


def _unpack_bf16_pairs(p):
    u = lax.bitcast_convert_type(p, jnp.uint32)
    lo = lax.bitcast_convert_type(u << 16, F32)
    hi = lax.bitcast_convert_type(u & jnp.uint32(HI_MASK), F32)
    return jnp.concatenate([lo, hi], axis=1).astype(BF16)


def _layer_norm(v, g, b):
    mu = jnp.mean(v, axis=-1, keepdims=True)
    c = v - mu
    var = jnp.mean(c * c, axis=-1, keepdims=True)
    return c * lax.rsqrt(var + LN_EPS) * g + b


N_MIXER_WEIGHTS = 12


def _mixer_kernel(per_chunk_state, pos0, *refs):
    if per_chunk_state:
        (x_ref, hist_in_ref, s_in_ref, *rest) = refs
    else:
        (x_ref, *rest) = refs
        hist_in_ref = s_in_ref = None
    (w_main_ref, w_glr_ref, w_gate_ref, b_gate_ref, w_pool_ref, pscale_ref, gnorm_ref,
     w_out_ref, ln1g_ref, ln1b_ref, w_router_ref, b_router_ref, *rest) = rest
    (h_ref, hb_ref, logits_ref, hist_out_ref, s_out_ref,
     proj_scr, b_scr, o_scr, ext_scr, st_scr, tri_scr, w_main_bf, w_out_bf) = rest

    if per_chunk_state:
        t = None
        first_step = pl.program_id(0) == 0
    else:
        t = pl.program_id(1)
        first_step = (pl.program_id(0) == 0) & (t == 0)
    x = x_ref[...].reshape(TILE_TOKENS, D_MODEL)
    xb = x.astype(BF16)

    @pl.when(first_step)
    def _():
        ti = lax.broadcasted_iota(jnp.int32, (SCAN_ROWS, SCAN_ROWS), 0)
        tj = lax.broadcasted_iota(jnp.int32, (SCAN_ROWS, SCAN_ROWS), 1)
        same_chunk = (ti // CHUNK) == (tj // CHUNK)
        tri_scr[...] = jnp.where(same_chunk & (ti >= tj), 1.0, 0.0).astype(BF16)
        w_main_bf[...] = w_main_ref[:, 0:N_MAIN].astype(BF16)
        w_out_bf[...] = w_out_ref[...].astype(BF16)
        if not per_chunk_state:
            st_scr[...] = jnp.zeros_like(st_scr)
            ext_scr[0:HIST_PAD, :] = jnp.zeros((HIST_PAD, POOL_WIDTH), F32)

    glr = _dot(xb, w_glr_ref[...])
    proj_scr[:, 0:V0] = _dot(xb, w_main_bf[:, 0:V0])
    gk = _dot(glr.astype(BF16), w_gate_ref[...]) + b_gate_ref[...]
    log_sig = jnp.minimum(gk, 0.0) - jnp.log1p(jnp.exp(-jnp.abs(gk)))
    g = log_sig / GATE_NORMALIZER
    g_hi = g.astype(BF16)
    g_lo = (g - g_hi.astype(F32)).astype(BF16)
    proj_scr[:, V0:N_MAIN] = _dot(xb, w_main_bf[:, V0:N_MAIN])
    for s in range(TILE_TOKENS // SCAN_ROWS):
        rs = slice(s * SCAN_ROWS, (s + 1) * SCAN_ROWS)
        b_scr[rs, :] = _dot(tri_scr[...], g_hi[rs]) + _dot(tri_scr[...], g_lo[rs])

    if per_chunk_state:
        seg_len, seg_stride, n_seg = CHUNK, CHUNK + HIST_PAD, CHUNKS_PER_TILE
        for c in range(n_seg):
            base = c * seg_stride
            ext_scr[base:base + HIST_PAD, :] = jnp.zeros((HIST_PAD, POOL_WIDTH), F32)
            ext_scr[base + 1:base + HIST_PAD, :] = hist_in_ref[c]
            ext_scr[base + HIST_PAD:base + seg_stride, :] = proj_scr[c * CHUNK:(c + 1) * CHUNK, 0:POOL_WIDTH]
        row_pos = pos0 + lax.broadcasted_iota(jnp.int32, (seg_len, POOL_GROUP), 0)
    else:
        seg_len, seg_stride, n_seg = TILE_TOKENS, TILE_TOKENS + HIST_PAD, 1
        ext_scr[0:HIST_PAD, :] = jnp.where(t == 0, 0.0, ext_scr[0:HIST_PAD, :])
        ext_scr[HIST_PAD:seg_stride, :] = proj_scr[:, 0:POOL_WIDTH]
        row_pos = pos0 + t * TILE_TOKENS + lax.broadcasted_iota(jnp.int32, (seg_len, POOL_GROUP), 0)

    pooled_groups = []
    for gi, w in enumerate(POOL_WINDOWS):
        gs = slice(gi * POOL_GROUP, (gi + 1) * POOL_GROUP)
        cnt = jnp.minimum(row_pos + 1, w).astype(F32)
        ext = ext_scr[:, gs]
        win = ext
        shift = 1
        while shift < w:
            win = win + pltpu.roll(win, shift, 0)
            shift *= 2
        segs = []
        for s in range(n_seg):
            base = s * seg_stride + HIST_PAD
            segs.append(win[base:base + seg_len] / cnt - ext[base:base + seg_len])
        pooled = segs[0] if n_seg == 1 else jnp.concatenate(segs, axis=0)
        pooled_groups.append(pooled.astype(BF16))
    pool_cols = []
    for p in range(len(POOL_WINDOWS) // 2):
        both = jnp.concatenate(pooled_groups[2 * p:2 * p + 2], axis=1)
        pool_cols.append(_dot(both, w_pool_ref[p]))
    pool_out = jnp.concatenate(pool_cols, axis=1) * pscale_ref[...]

    if per_chunk_state:
        for c in range(n_seg):
            end = (c + 1) * seg_stride
            hist_out_ref[c] = ext_scr[end - POOL_HIST:end, :]
    else:
        hist_out_ref[...] = ext_scr[seg_stride - POOL_HIST:seg_stride, :]
        ext_scr[0:HIST_PAD, :] = ext_scr[TILE_TOKENS:seg_stride, :]

    hk = GLA_HEADS * GLA_DK
    hv = GLA_HEADS * GLA_DV
    pair_rows = 2 * CHUNK
    decay_cols = LANES // CHUNKS_PER_TILE

    def head_of(shape, dim, width):
        return lax.broadcasted_iota(jnp.int32, shape, dim) // width

    same_head_k = head_of((hk, hk), 0, CHUNK) == head_of((hk, hk), 1, GLA_DK)
    same_head_v = head_of((hk, hv), 0, CHUNK) == head_of((hk, hv), 1, GLA_DV)
    pair_half = head_of((pair_rows, hv), 0, CHUNK)
    causal = (lax.broadcasted_iota(jnp.int32, (CHUNK, hk), 0)
              >= lax.broadcasted_iota(jnp.int32, (CHUNK, hk), 1) % CHUNK)

    b_all = b_scr[...]
    b_last = [b_scr[(c + 1) * CHUNK - 1:(c + 1) * CHUNK, :] for c in range(CHUNKS_PER_TILE)]
    b_last_rows = jnp.concatenate([jnp.broadcast_to(bl, (CHUNK, hk)) for bl in b_last], axis=0)
    k_all = proj_scr[:, K0:V0]
    qt_all = (proj_scr[:, Q0:K0] * (GLA_DK ** -0.5) * jnp.exp(b_all)).astype(BF16)
    kt_all = k_all * jnp.exp(-b_all)
    kl_t = jnp.transpose(k_all * jnp.exp(b_last_rows - b_all)).astype(BF16)
    decay_t = jnp.transpose(jnp.exp(jnp.concatenate(
        [jnp.broadcast_to(bl, (decay_cols, hk)) for bl in b_last], axis=0)))

    def finish_rows(rs):
        r = proj_scr[rs, R0:N_MAIN]
        silu_r = r * (1.0 / (1.0 + jnp.exp(-r)))
        gated = []
        for h in range(GLA_HEADS):
            vs = slice(h * GLA_DV, (h + 1) * GLA_DV)
            oh = o_scr[rs, vs]
            ms = jnp.mean(oh * oh, axis=-1, keepdims=True)
            gated.append(oh * lax.rsqrt(ms + RMS_EPS) * gnorm_ref[...] * silu_r[:, vs])
        mix_in = jnp.concatenate([pool_out[rs]] + gated, axis=1).astype(BF16)
        resid = ALPHA * x[rs] + _dot(mix_in, w_out_bf[...])
        h_val = _layer_norm(resid, ln1g_ref[...], ln1b_ref[...])
        h_ref[rs, :] = h_val
        hb = h_val.astype(BF16)
        hb_ref[rs, :] = _pack_bf16_pairs(hb)
        logits_ref[:, rs] = _dot_nt(w_router_ref[...], hb) + b_router_ref[:, 0:1]

    st = None if per_chunk_state else jnp.where(t == 0, 0.0, st_scr[...])
    for c in range(CHUNKS_PER_TILE):
        rows = slice(c * CHUNK, (c + 1) * CHUNK)
        pair = slice((c // 2) * pair_rows, (c // 2 + 1) * pair_rows)
        if per_chunk_state:
            st = s_in_ref[c].reshape(hk, GLA_DV)
        qt = qt_all[rows]
        zero = jnp.zeros((), BF16)
        k_stack = jnp.where(same_head_k, jnp.concatenate([kt_all[rows].astype(BF16)] * GLA_HEADS, axis=0), zero)
        v_stack = jnp.where(same_head_v, jnp.concatenate(
            [proj_scr[rows, V0:R0].astype(BF16)] * GLA_HEADS, axis=0), zero)
        s_stack = jnp.where(same_head_v, jnp.concatenate([st.astype(BF16)] * GLA_HEADS, axis=1), zero)
        att = jnp.where(causal, _dot_nt(qt, k_stack), 0.0)
        o_scr[rows, :] = _dot(att.astype(BF16), v_stack) + _dot(qt, s_stack)
        v_chunk = jnp.where(pair_half == c % 2, proj_scr[pair, V0:R0], 0.0).astype(BF16)
        upd = jnp.concatenate(
            [_dot(kl_t[h * GLA_DK:(h + 1) * GLA_DK, pair], v_chunk[:, h * GLA_DV:(h + 1) * GLA_DV])
             for h in range(GLA_HEADS)], axis=0)
        st = st * decay_t[:, c * decay_cols:c * decay_cols + 1] + upd
        if per_chunk_state:
            s_out_ref[c] = st.reshape(GLA_HEADS, GLA_DK, GLA_DV)
        if (c + 1) % OUT_CHUNKS == 0:
            finish_rows(slice((c + 1 - OUT_CHUNKS) * CHUNK, (c + 1) * CHUNK))

    if not per_chunk_state:
        st_scr[...] = st
        s_out_ref[...] = st.reshape(GLA_HEADS, GLA_DK, GLA_DV)


def _const_spec(shape, single_buffer=False):
    nd = len(shape)
    if single_buffer:
        return pl.BlockSpec(shape, lambda *_: (0,) * nd, pipeline_mode=pl.Buffered(1))
    return pl.BlockSpec(shape, lambda *_: (0,) * nd)


def _mixer_weight_specs():
    return [
        _const_spec((None, D_MODEL, N_IN), single_buffer=True),
        _const_spec((D_MODEL, LANES)),
        _const_spec((LANES, GLA_HEADS * GLA_DK)),
        _const_spec((1, GLA_HEADS * GLA_DK)),
        _const_spec((len(POOL_WINDOWS) // 2, 2 * POOL_GROUP, 2 * POOL_GROUP)),
        _const_spec((1, POOL_WIDTH)),
        _const_spec((1, GLA_DV)),
        _const_spec((None, D_MODEL, D_MODEL), single_buffer=True),
        _const_spec((1, D_MODEL)),
        _const_spec((1, D_MODEL)),
        _const_spec((NUM_EXPERTS, D_MODEL)),
        _const_spec((NUM_EXPERTS, LANES)),
    ]


def _mixer_weights(w_in, w_pool, pool_scale, w_gate_up, b_gate, gla_norm_w, w_out, ln1_g, ln1_b,
                   w_router, b_router):
    w_glr = jnp.zeros((D_MODEL, LANES), BF16).at[:, :GATE_RANK].set(w_in[0, :, N_MAIN:].astype(BF16))
    w_gate = jnp.zeros((LANES, GLA_HEADS * GLA_DK), BF16).at[:GATE_RANK].set(w_gate_up[0].astype(BF16))
    wp = w_pool[0].astype(BF16)
    zero = jnp.zeros((POOL_GROUP, POOL_GROUP), BF16)
    w_pool_pairs = jnp.stack([jnp.block([[wp[2 * p], zero], [zero, wp[2 * p + 1]]])
                              for p in range(len(POOL_WINDOWS) // 2)])
    weights = (
        w_in, w_glr, w_gate, b_gate[0][None, :],
        w_pool_pairs, pool_scale[0][None, :], gla_norm_w[0][None, :],
        w_out, ln1_g[0][None, :], ln1_b[0][None, :],
        w_router[0].T.astype(BF16), jnp.broadcast_to(b_router[0][:, None], (NUM_EXPERTS, LANES)),
    )
    assert len(weights) == N_MIXER_WEIGHTS
    return weights


def _mixer_scratch(per_chunk_state):
    ext_rows = (CHUNKS_PER_TILE * (CHUNK + HIST_PAD)) if per_chunk_state else (TILE_TOKENS + HIST_PAD)
    return [
        pltpu.VMEM((TILE_TOKENS, N_MAIN), F32),
        pltpu.VMEM((TILE_TOKENS, GLA_HEADS * GLA_DK), F32),
        pltpu.VMEM((TILE_TOKENS, GLA_HEADS * GLA_DV), F32),
        pltpu.VMEM((ext_rows, POOL_WIDTH), F32),
        pltpu.VMEM((GLA_HEADS * GLA_DK, GLA_DV), F32),
        pltpu.VMEM((SCAN_ROWS, SCAN_ROWS), BF16),
        pltpu.VMEM((D_MODEL, N_MAIN), BF16),
        pltpu.VMEM((D_MODEL, D_MODEL), BF16),
    ]


def _mixer_out_shapes(n, bsz):
    return (
        jax.ShapeDtypeStruct((n, D_MODEL), F32),
        jax.ShapeDtypeStruct((n, HALF), jnp.int32),
        jax.ShapeDtypeStruct((NUM_EXPERTS, n), F32),
        jax.ShapeDtypeStruct((bsz, POOL_HIST, POOL_WIDTH), F32),
        jax.ShapeDtypeStruct((bsz, GLA_HEADS, GLA_DK, GLA_DV), F32),
    )


def _mixer_prompt(x, weights):
    bsz, seq, _ = x.shape
    tiles = seq // TILE_TOKENS
    n_total = bsz * seq
    return pl.pallas_call(
        functools.partial(_mixer_kernel, False, 0),
        grid=(bsz, tiles),
        in_specs=[pl.BlockSpec((None, TILE_TOKENS, D_MODEL), lambda b, t: (b, t, 0))] + _mixer_weight_specs(),
        out_specs=(
            pl.BlockSpec((TILE_TOKENS, D_MODEL), lambda b, t: (b * tiles + t, 0)),
            pl.BlockSpec((TILE_TOKENS, HALF), lambda b, t: (b * tiles + t, 0)),
            pl.BlockSpec((NUM_EXPERTS, TILE_TOKENS), lambda b, t: (0, b * tiles + t)),
            pl.BlockSpec((None, POOL_HIST, POOL_WIDTH), lambda b, t: (b, 0, 0)),
            pl.BlockSpec((None, GLA_HEADS, GLA_DK, GLA_DV), lambda b, t: (b, 0, 0, 0)),
        ),
        out_shape=_mixer_out_shapes(n_total, bsz),
        scratch_shapes=_mixer_scratch(False),
        compiler_params=pltpu.CompilerParams(
            dimension_semantics=("arbitrary", "arbitrary"), vmem_limit_bytes=VMEM_LIMIT),
        name="mixer_prompt",
    )(x, *weights)


def _mixer_sample(x, hist, state, weights):
    bsz = x.shape[0]
    tiles = bsz // CHUNKS_PER_TILE
    return pl.pallas_call(
        functools.partial(_mixer_kernel, True, PAST_LEN),
        grid=(tiles,),
        in_specs=[
            pl.BlockSpec((CHUNKS_PER_TILE, CHUNK, D_MODEL), lambda i: (i, 0, 0)),
            pl.BlockSpec((CHUNKS_PER_TILE, POOL_HIST, POOL_WIDTH), lambda i: (i, 0, 0)),
            pl.BlockSpec((CHUNKS_PER_TILE, GLA_HEADS, GLA_DK, GLA_DV), lambda i: (i, 0, 0, 0)),
        ] + _mixer_weight_specs(),
        out_specs=(
            pl.BlockSpec((TILE_TOKENS, D_MODEL), lambda i: (i, 0)),
            pl.BlockSpec((TILE_TOKENS, HALF), lambda i: (i, 0)),
            pl.BlockSpec((NUM_EXPERTS, TILE_TOKENS), lambda i: (0, i)),
            pl.BlockSpec((CHUNKS_PER_TILE, POOL_HIST, POOL_WIDTH), lambda i: (i, 0, 0)),
            pl.BlockSpec((CHUNKS_PER_TILE, GLA_HEADS, GLA_DK, GLA_DV), lambda i: (i, 0, 0, 0)),
        ),
        out_shape=_mixer_out_shapes(bsz * CHUNK, bsz),
        scratch_shapes=_mixer_scratch(True),
        compiler_params=pltpu.CompilerParams(
            dimension_semantics=("arbitrary",), vmem_limit_bytes=VMEM_LIMIT),
        name="mixer_sample",
    )(x, hist, state, *weights)


def _router_kernel(lt_ref, gates_ref, dest_ref, padend_ref, cnt_scr, base_scr, pstart_scr, before_scr,
                   topk_scr):
    phase = pl.program_id(0)
    i = pl.program_id(1)
    shape = (NUM_EXPERTS, ROUTE_TOKENS)
    row = lax.broadcasted_iota(jnp.int32, shape, 0)

    def tile_counts_of(chosen):
        return jnp.broadcast_to(jnp.sum(chosen, axis=1, keepdims=True), (NUM_EXPERTS, LANES))

    @pl.when(phase == 0)
    def _():
        @pl.when(i == 0)
        def _():
            cnt_scr[...] = jnp.zeros_like(cnt_scr)

        logits = lt_ref[...]
        idxs, vals = [], []
        chosen = jnp.zeros(shape, F32)
        for _ in range(TOP_K):
            m = jnp.max(logits, axis=0, keepdims=True)
            idx = jnp.min(jnp.where(logits == m, row, NUM_EXPERTS), axis=0, keepdims=True)
            hit = row == idx
            idxs.append(idx.astype(F32))
            vals.append(m)
            chosen = chosen + jnp.where(hit, 1.0, 0.0)
            logits = jnp.where(hit, -jnp.inf, logits)
        topk_scr[i] = jnp.concatenate(idxs + vals, axis=0)
        cnt_scr[...] += tile_counts_of(chosen)

    @pl.when(phase == 1)
    def _():
        @pl.when(i == 0)
        def _():
            blocks = jnp.floor((cnt_scr[...] + (MOE_ROWS - 1)) * (1.0 / MOE_ROWS))
            erow = lax.broadcasted_iota(jnp.int32, (NUM_EXPERTS, LANES), 0)
            lane = lax.broadcasted_iota(jnp.int32, (NUM_EXPERTS, LANES), 1)
            cum = blocks
            shift = 1
            while shift < NUM_EXPERTS:
                cum = cum + jnp.where(erow >= shift, pltpu.roll(cum, shift, 0), 0.0)
                shift *= 2
            padend_ref[...] = jnp.where(lane == 1, cnt_scr[...], cum * MOE_ROWS)
            pstart_scr[...] = (cum - blocks) * MOE_ROWS
            base_scr[...] = jnp.zeros_like(base_scr)
            ti = lax.broadcasted_iota(jnp.int32, (ROUTE_TOKENS, ROUTE_TOKENS), 0)
            tj = lax.broadcasted_iota(jnp.int32, (ROUTE_TOKENS, ROUTE_TOKENS), 1)
            before_scr[...] = jnp.where(ti < tj, 1.0, 0.0).astype(BF16)

        topk = topk_scr[i]
        sel = [row == topk[k:k + 1, :].astype(jnp.int32) for k in range(TOP_K)]
        vals = [topk[TOP_K + k:TOP_K + k + 1, :] for k in range(TOP_K)]
        chosen = sum(jnp.where(hit, 1.0, 0.0) for hit in sel)
        earlier = _dot(chosen.astype(BF16), before_scr[...])
        pos = pstart_scr[:, 0:1] + base_scr[:, 0:1] + earlier
        dest = [jnp.sum(jnp.where(hit, pos, 0.0), axis=0, keepdims=True) for hit in sel]
        dest_ref[...] = jnp.concatenate(dest, axis=0).astype(jnp.int32)
        ex = [jnp.exp(v - vals[0]) for v in vals]
        denom = ex[0] + ex[1] + ex[2] + ex[3]
        gates_ref[...] = jnp.concatenate([e / denom for e in ex], axis=0)
        base_scr[...] += tile_counts_of(chosen)


def _router(logits_t):
    n = logits_t.shape[1]
    assert n % ROUTE_TOKENS == 0
    tiles = n // ROUTE_TOKENS
    return pl.pallas_call(
        _router_kernel,
        grid=(2, tiles),
        in_specs=[pl.BlockSpec((NUM_EXPERTS, ROUTE_TOKENS), lambda p, i: (0, i))],
        out_specs=(
            pl.BlockSpec((TOP_K, ROUTE_TOKENS), lambda p, i: (0, i * p)),
            pl.BlockSpec((TOP_K, ROUTE_TOKENS), lambda p, i: (0, i * p)),
            pl.BlockSpec((NUM_EXPERTS, LANES), lambda p, i: (0, 0)),
        ),
        out_shape=(
            jax.ShapeDtypeStruct((TOP_K, n), F32),
            jax.ShapeDtypeStruct((TOP_K, n), jnp.int32),
            jax.ShapeDtypeStruct((NUM_EXPERTS, LANES), F32),
        ),
        scratch_shapes=[pltpu.VMEM((NUM_EXPERTS, LANES), F32)] * 3
        + [pltpu.VMEM((ROUTE_TOKENS, ROUTE_TOKENS), BF16),
           pltpu.VMEM((tiles, 2 * TOP_K, ROUTE_TOKENS), F32)],
        compiler_params=pltpu.CompilerParams(
            dimension_semantics=("arbitrary", "arbitrary"), vmem_limit_bytes=VMEM_LIMIT),
        name="router",
    )(logits_t)


SC_CORES = 2
SC_SUBCORES = 16
SC_WORKERS = SC_CORES * SC_SUBCORES
DISPATCH_ROWS = 64


def _dispatch(h_sources, dest_chunks, m_pad):
    shares, first_chunk = [], 0
    for src in h_sources:
        n_chunks = src.shape[0] // DISPATCH_ROWS
        assert src.shape[0] % DISPATCH_ROWS == 0 and n_chunks % SC_WORKERS == 0
        shares.append((first_chunk, n_chunks // SC_WORKERS))
        first_chunk += n_chunks
    plan = [(s, first, per, j) for s, (first, per) in enumerate(shares) for j in range(per)]
    per_worker = len(plan)
    mesh = plsc.VectorSubcoreMesh(core_axis_name="c", subcore_axis_name="s")

    @functools.partial(
        pl.kernel, mesh=mesh,
        out_type=jax.ShapeDtypeStruct((m_pad, HALF), jnp.int32),
        scratch_types=[
            pltpu.VMEM((2, TOP_K, DISPATCH_ROWS), jnp.int32),
            pltpu.VMEM((2, DISPATCH_ROWS, HALF), jnp.int32),
            pltpu.SemaphoreType.DMA((2,)),
            pltpu.SemaphoreType.DMA((2,)),
        ],
        compiler_params=pltpu.CompilerParams(use_tc_tiling_on_sc=True),
        name="dispatch",
    )
    def dispatch_kernel(*refs):
        h_hbms = refs[:len(h_sources)]
        dest_hbm, out_hbm, idx_v, rows_v, load_sems, scatter_sems = refs[len(h_sources):]
        wid = lax.axis_index("s") * SC_CORES + lax.axis_index("c")

        def loads(j):
            src, first, per, k = plan[j]
            local = wid * per + k
            slot = j % 2
            return (
                pltpu.make_async_copy(dest_hbm.at[first + local], idx_v.at[slot], load_sems.at[slot]),
                pltpu.make_async_copy(h_hbms[src].at[pl.ds(local * DISPATCH_ROWS, DISPATCH_ROWS)],
                                      rows_v.at[slot], load_sems.at[slot]),
            )

        def scatters(j):
            slot = j % 2
            return [pltpu.make_async_copy(rows_v.at[slot], out_hbm.at[idx_v.at[slot, k]],
                                          scatter_sems.at[slot]) for k in range(TOP_K)]

        for cp in loads(0):
            cp.start()
        for j in range(per_worker):
            for cp in loads(j):
                cp.wait()
            if j >= 1:
                for cp in scatters(j - 1):
                    cp.wait()
            if j + 1 < per_worker:
                for cp in loads(j + 1):
                    cp.start()
            for cp in scatters(j):
                cp.start()
        for cp in scatters(per_worker - 1):
            cp.wait()

    return dispatch_kernel(*h_sources, dest_chunks)


def _gather_expert_rows(y_sorted, dest_chunks, row_offset, n):
    n_chunks = n // DISPATCH_ROWS
    assert n_chunks % SC_WORKERS == 0 and row_offset % DISPATCH_ROWS == 0
    per_worker = n_chunks // SC_WORKERS
    chunk0 = row_offset // DISPATCH_ROWS
    mesh = plsc.VectorSubcoreMesh(core_axis_name="c", subcore_axis_name="s")

    @functools.partial(
        pl.kernel, mesh=mesh,
        out_type=jax.ShapeDtypeStruct((TOP_K, n, HALF), jnp.int32),
        scratch_types=[
            pltpu.VMEM((TOP_K, DISPATCH_ROWS), jnp.int32),
            pltpu.VMEM((2, DISPATCH_ROWS, HALF), jnp.int32),
            pltpu.SemaphoreType.DMA((2,)),
        ],
        compiler_params=pltpu.CompilerParams(use_tc_tiling_on_sc=True),
        name="gather_expert_rows",
    )
    def gather_kernel(y_hbm, dest_hbm, out_hbm, idx_v, rows_v, sems):
        wid = lax.axis_index("s") * SC_CORES + lax.axis_index("c")

        def gather(k):
            return pltpu.make_async_copy(y_hbm.at[idx_v.at[k]], rows_v.at[k % 2], sems.at[k % 2])

        @pl.loop(0, per_worker)
        def _(j):
            local = wid * per_worker + j
            pltpu.sync_copy(dest_hbm.at[chunk0 + local], idx_v)
            gather(0).start()
            for k in range(TOP_K):
                if k + 1 < TOP_K:
                    gather(k + 1).start()
                gather(k).wait()
                pltpu.sync_copy(rows_v.at[k % 2],
                                out_hbm.at[k, pl.ds(local * DISPATCH_ROWS, DISPATCH_ROWS)])

    return gather_kernel(y_sorted, dest_chunks)


def _moe_kernel(be_ref, wsel_ref, half_ref, nused_ref, x_ref, wgu_ref, bgu_ref, wd_ref, bd_ref, y_ref,
                wgu_bf, wd_bf):
    del wsel_ref
    i = pl.program_id(0)

    def ffn(rows):
        gu = _dot(_unpack_bf16_pairs(x_ref[rows, :]), wgu_bf[...]) + bgu_ref[...]
        gate = jnp.minimum(gu[:, :EXPERT_FF], SWIGLU_LIMIT)
        up = jnp.clip(gu[:, EXPERT_FF:], -SWIGLU_LIMIT, SWIGLU_LIMIT)
        hmid = gate * (1.0 / (1.0 + jnp.exp(-SWIGLU_ALPHA * gate))) * (up + 1.0)
        y = _dot(hmid.astype(BF16), wd_bf[...]) + bd_ref[...]
        y_ref[rows, :] = _pack_bf16_pairs(y.astype(BF16))

    @pl.when(i < nused_ref[0])
    def _():
        @pl.when((i == 0) | (be_ref[i] != be_ref[jnp.maximum(i - 1, 0)]))
        def _():
            wgu_bf[...] = wgu_ref[...].astype(BF16)
            wd_bf[...] = wd_ref[...].astype(BF16)

        @pl.when(half_ref[i] == 0)
        def _():
            ffn(slice(0, MOE_ROWS))

        @pl.when(half_ref[i] != 0)
        def _():
            ffn(slice(0, MOE_ROWS // 2))


def _moe_experts(block_expert, weight_expert, half_block, n_used, x_sorted, w_gu, b_gu, w_down, b_down):
    m_pad = x_sorted.shape[0]
    n_blocks = m_pad // MOE_ROWS

    def blk(i, be, ws, hb, nu):
        return jnp.minimum(i, nu[0] - 1)

    def expert(i, be, ws, hb, nu):
        return be[blk(i, be, ws, hb, nu)]

    def held(i, be, ws, hb, nu):
        return ws[blk(i, be, ws, hb, nu)]

    grid_spec = pltpu.PrefetchScalarGridSpec(
        num_scalar_prefetch=4,
        grid=(n_blocks,),
        in_specs=[
            pl.BlockSpec((MOE_ROWS, HALF), lambda *a: (blk(*a), 0)),
            pl.BlockSpec((None, D_MODEL, 2 * EXPERT_FF), lambda *a: (held(*a), 0, 0)),
            pl.BlockSpec((None, 1, 2 * EXPERT_FF), lambda *a: (expert(*a), 0, 0)),
            pl.BlockSpec((None, EXPERT_FF, D_MODEL), lambda *a: (held(*a), 0, 0)),
            pl.BlockSpec((None, 1, D_MODEL), lambda *a: (expert(*a), 0, 0)),
        ],
        out_specs=pl.BlockSpec((MOE_ROWS, HALF), lambda *a: (blk(*a), 0)),
        scratch_shapes=[
            pltpu.VMEM((D_MODEL, 2 * EXPERT_FF), BF16),
            pltpu.VMEM((EXPERT_FF, D_MODEL), BF16),
        ],
    )
    return pl.pallas_call(
        _moe_kernel,
        grid_spec=grid_spec,
        out_shape=jax.ShapeDtypeStruct((m_pad, HALF), jnp.int32),
        compiler_params=pltpu.CompilerParams(
            dimension_semantics=("arbitrary",), vmem_limit_bytes=VMEM_LIMIT),
        name="moe_experts",
    )(block_expert, weight_expert, half_block, n_used, x_sorted, w_gu, b_gu, w_down, b_down)


def _combine_kernel(yk_ref, gates_ref, h_ref, g_ref, b_ref, out_ref):
    pad = jnp.zeros((LANES - TOP_K, COMBINE_TOKENS), F32)
    gates = jnp.transpose(jnp.concatenate([gates_ref[...], pad], axis=0))
    lo = hi = None
    for k in range(TOP_K):
        u = lax.bitcast_convert_type(yk_ref[k], jnp.uint32)
        gk = gates[:, k:k + 1]
        lo_k = lax.bitcast_convert_type(u << 16, F32) * gk
        hi_k = lax.bitcast_convert_type(u & jnp.uint32(HI_MASK), F32) * gk
        lo = lo_k if lo is None else lo + lo_k
        hi = hi_k if hi is None else hi + hi_k
    acc = ALPHA * h_ref[...] + jnp.concatenate([lo, hi], axis=1)
    out_ref[...] = _layer_norm(acc, g_ref[...], b_ref[...])


def _combine_kernel_aliased(yk_ref, gates_ref, h_ref, g_ref, b_ref, prev_ref, out_ref):
    del prev_ref
    _combine_kernel(yk_ref, gates_ref, h_ref, g_ref, b_ref, out_ref)


def _combine(yk, gates, token_offset, h_src, h_offset, ln_g, ln_b, out_prev):
    n_seg = yk.shape[1]
    out_rows = h_src.shape[0]
    tile0 = token_offset // COMBINE_TOKENS
    out_tile0 = h_offset // COMBINE_TOKENS
    in_specs = [
        pl.BlockSpec((TOP_K, COMBINE_TOKENS, HALF), lambda i: (0, i, 0)),
        pl.BlockSpec((TOP_K, COMBINE_TOKENS), lambda i: (0, tile0 + i)),
        pl.BlockSpec((COMBINE_TOKENS, D_MODEL), lambda i: (out_tile0 + i, 0)),
        _const_spec((1, D_MODEL)),
        _const_spec((1, D_MODEL)),
    ]
    args = [yk, gates, h_src, ln_g, ln_b]
    aliases = {}
    kern = _combine_kernel
    if out_prev is not None:
        in_specs.append(pl.BlockSpec(memory_space=pl.ANY))
        aliases = {len(args): 0}
        args.append(out_prev)
        kern = _combine_kernel_aliased
    return pl.pallas_call(
        kern,
        grid=(n_seg // COMBINE_TOKENS,),
        in_specs=in_specs,
        out_specs=pl.BlockSpec((COMBINE_TOKENS, D_MODEL), lambda i: (out_tile0 + i, 0)),
        out_shape=jax.ShapeDtypeStruct((out_rows, D_MODEL), F32),
        input_output_aliases=aliases,
        compiler_params=pltpu.CompilerParams(
            dimension_semantics=("arbitrary",), vmem_limit_bytes=VMEM_LIMIT),
        name="moe_combine",
    )(*args)


def kernel(x_prompt, x_sample, state_pool, state_gla, w_in, w_pool, pool_scale, w_gate_up, b_gate,
           gla_norm_w, w_out, ln1_g, ln1_b, w_router, b_router, w_gu, b_gu, w_down, b_down,
           ln2_g, ln2_b):
    assert w_in.shape[0] == 1, "single-layer kernel"
    bp, seq, _ = x_prompt.shape
    bs, dec_seq, _ = x_sample.shape
    assert dec_seq == CHUNK and seq % TILE_TOKENS == 0 and bs % CHUNKS_PER_TILE == 0
    n_prompt = bp * seq
    n_sample = bs * dec_seq
    n_total = n_prompt + n_sample
    nk = n_total * TOP_K
    n_blocks = -(-nk // MOE_ROWS) + NUM_EXPERTS
    m_pad = n_blocks * MOE_ROWS

    weights = _mixer_weights(w_in, w_pool, pool_scale, w_gate_up, b_gate, gla_norm_w, w_out,
                             ln1_g, ln1_b, w_router, b_router)

    h_p, hb_p, logits_p, hist_p, s_p = _mixer_prompt(x_prompt, weights)
    h_s, hb_s, logits_s, hist_s, s_s = _mixer_sample(x_sample, state_pool[0], state_gla[0], weights)

    gates_t, dest_t, layout = _router(jnp.concatenate([logits_p, logits_s], axis=1))
    pad_end = layout[:, 0].astype(jnp.int32)
    counts = layout[:, 1].astype(jnp.int32)
    block_start = jnp.arange(n_blocks, dtype=jnp.int32) * MOE_ROWS
    block_expert = jnp.minimum(jnp.sum((block_start[:, None] >= pad_end[None, :]).astype(jnp.int32), axis=1),
                               NUM_EXPERTS - 1)
    n_used = (pad_end[-1:] // MOE_ROWS).astype(jnp.int32)
    is_first = jnp.concatenate([jnp.ones((1,), bool), block_expert[1:] != block_expert[:-1]])
    blocks = jnp.arange(n_blocks, dtype=jnp.int32)
    later_other = ((blocks[None, :] > blocks[:, None]) & (blocks[None, :] < n_used[0])
                   & (block_expert[None, :] != block_expert[:, None]))
    next_expert = jnp.min(jnp.where(later_other, block_expert[None, :], NUM_EXPERTS), axis=1)
    next_expert = jnp.where(next_expert == NUM_EXPERTS, block_expert, next_expert)
    weight_expert = jnp.where(is_first, block_expert, next_expert).astype(jnp.int32)
    of_expert = block_expert[:, None] == jnp.arange(NUM_EXPERTS, dtype=jnp.int32)[None, :]
    seg_end = jnp.sum(jnp.where(of_expert, (pad_end - (-counts % MOE_ROWS))[None, :], 0), axis=1)
    half_block = ((seg_end - block_start) <= MOE_ROWS // 2).astype(jnp.int32)

    dest_chunks = dest_t.reshape(TOP_K, n_total // DISPATCH_ROWS, DISPATCH_ROWS).transpose(1, 0, 2)
    x_sorted = _dispatch((hb_p, hb_s), dest_chunks, m_pad)
    y_sorted = _moe_experts(block_expert, weight_expert, half_block, n_used, x_sorted, w_gu[0], b_gu[0][:, None, :],
                            w_down[0], b_down[0][:, None, :])
    gates = gates_t
    ln_g, ln_b = ln2_g[0][None, :], ln2_b[0][None, :]
    yk = _gather_expert_rows(y_sorted, dest_chunks, n_prompt, n_sample)
    y_sample = _combine(yk, gates, n_prompt, h_s, 0, ln_g, ln_b, None)
    unit = SC_WORKERS * DISPATCH_ROWS
    assert n_prompt % unit == 0
    sizes, left, size = [], n_prompt // unit, 1
    while left > 0:
        size = min(left, size)
        sizes.append(size * unit)
        left -= size
        size = max(size + 1, (3 * size) // 2)
    y_prompt, start = None, 0
    for seg in sizes:
        yk = _gather_expert_rows(y_sorted, dest_chunks, start, seg)
        y_prompt = _combine(yk, gates, start, h_p, start, ln_g, ln_b, y_prompt)
        start += seg
    y_prompt = y_prompt.reshape(bp, seq, D_MODEL)
    y_sample = y_sample.reshape(bs, dec_seq, D_MODEL)
    return (y_prompt, y_sample, hist_p[None], s_p[None], hist_s[None], s_s[None])
```

```python
import functools

import jax
import jax.numpy as jnp
from jax import lax
from jax.experimental import pallas as pl
from jax.experimental.pallas import tpu as pltpu
from jax.experimental.pallas import tpu_sc as plsc

F32 = jnp.float32
BF16 = jnp.bfloat16

D_MODEL = 1024
CHUNK = 64
PAST_LEN = 1024
POOL_WIDTH = 512
POOL_WINDOWS = (2, 4, 8, 16)
POOL_GROUP = 128
POOL_HIST = 15
GLA_HEADS = 4
GLA_DK = 64
GLA_DV = 128
GATE_RANK = 16
GATE_NORMALIZER = 16.0
NUM_EXPERTS = 32
TOP_K = 4
EXPERT_FF = 1024
SWIGLU_LIMIT = 7.0
SWIGLU_ALPHA = 1.702
LN_EPS = 1e-5
RMS_EPS = 1e-6
ALPHA = 2.0 ** 0.25

Q0 = POOL_WIDTH
K0 = Q0 + GLA_HEADS * GLA_DK
V0 = K0 + GLA_HEADS * GLA_DK
R0 = V0 + GLA_HEADS * GLA_DV
N_MAIN = R0 + GLA_HEADS * GLA_DV
N_IN = N_MAIN + GATE_RANK

LANES = 128
TILE_TOKENS = 512
CHUNKS_PER_TILE = TILE_TOKENS // CHUNK
HIST_PAD = 16
OUT_CHUNKS = 4
SCAN_ROWS = 256
MOE_ROWS = 1024
MOE_TAIL_ROWS = 256
ROUTE_TOKENS = 1024
COMBINE_TOKENS = 512
VMEM_LIMIT = 56 * 1024 * 1024


def _dot(a, b):
    return jnp.dot(a, b, preferred_element_type=F32)


def _dot_nt(a, b):
    return lax.dot_general(a, b, (((1,), (1,)), ((), ())), preferred_element_type=F32)


def _dot_tn(a, b):
    return lax.dot_general(a, b, (((0,), (0,)), ((), ())), preferred_element_type=F32)


HALF = D_MODEL // 2
HI_MASK = 0xFFFF0000


def _pack_bf16_pairs(xb):
    lo = lax.bitcast_convert_type(xb[:, :HALF].astype(F32), jnp.uint32) >> 16
    hi = lax.bitcast_convert_type(xb[:, HALF:].astype(F32), jnp.uint32) & jnp.uint32(HI_MASK)
    return lax.bitcast_convert_type(hi | lo, jnp.int32)


def _unpack_bf16_pairs(p):
    u = lax.bitcast_convert_type(p, jnp.uint32)
    lo = lax.bitcast_convert_type(u << 16, F32)
    hi = lax.bitcast_convert_type(u & jnp.uint32(HI_MASK), F32)
    return jnp.concatenate([lo, hi], axis=1).astype(BF16)


def _layer_norm(v, g, b):
    mu = jnp.mean(v, axis=-1, keepdims=True)
    c = v - mu
    var = jnp.mean(c * c, axis=-1, keepdims=True)
    return c * lax.rsqrt(var + LN_EPS) * g + b


N_MIXER_WEIGHTS = 12


def _mixer_kernel(per_chunk_state, pos0, *refs):
    if per_chunk_state:
        (x_ref, hist_in_ref, s_in_ref, *rest) = refs
    else:
        (x_ref, *rest) = refs
        hist_in_ref = s_in_ref = None
    (w_main_ref, w_glr_ref, w_gate_ref, b_gate_ref, w_pool_ref, pscale_ref, gnorm_ref,
     w_out_ref, ln1g_ref, ln1b_ref, w_router_ref, b_router_ref, *rest) = rest
    (h_ref, hb_ref, logits_ref, hist_out_ref, s_out_ref,
     proj_scr, b_scr, o_scr, ext_scr, st_scr, tri_scr, w_main_bf, w_out_bf) = rest

    if per_chunk_state:
        t = None
        first_step = pl.program_id(0) == 0
    else:
        t = pl.program_id(1)
        first_step = (pl.program_id(0) == 0) & (t == 0)
    x = x_ref[...].reshape(TILE_TOKENS, D_MODEL)
    xb = x.astype(BF16)

    @pl.when(first_step)
    def _():
        ti = lax.broadcasted_iota(jnp.int32, (SCAN_ROWS, SCAN_ROWS), 0)
        tj = lax.broadcasted_iota(jnp.int32, (SCAN_ROWS, SCAN_ROWS), 1)
        same_chunk = (ti // CHUNK) == (tj // CHUNK)
        tri_scr[...] = jnp.where(same_chunk & (ti >= tj), 1.0, 0.0).astype(BF16)
        w_main_bf[...] = w_main_ref[:, 0:N_MAIN].astype(BF16)
        w_out_bf[...] = w_out_ref[...].astype(BF16)
        if not per_chunk_state:
            st_scr[...] = jnp.zeros_like(st_scr)
            ext_scr[0:HIST_PAD, :] = jnp.zeros((HIST_PAD, POOL_WIDTH), F32)

    glr = _dot(xb, w_glr_ref[...])
    proj_scr[:, 0:V0] = _dot(xb, w_main_bf[:, 0:V0])
    gk = _dot(glr.astype(BF16), w_gate_ref[...]) + b_gate_ref[...]
    log_sig = jnp.minimum(gk, 0.0) - jnp.log1p(jnp.exp(-jnp.abs(gk)))
    g = log_sig / GATE_NORMALIZER
    g_hi = g.astype(BF16)
    g_lo = (g - g_hi.astype(F32)).astype(BF16)
    proj_scr[:, V0:N_MAIN] = _dot(xb, w_main_bf[:, V0:N_MAIN])
    for s in range(TILE_TOKENS // SCAN_ROWS):
        rs = slice(s * SCAN_ROWS, (s + 1) * SCAN_ROWS)
        b_scr[rs, :] = _dot(tri_scr[...], g_hi[rs]) + _dot(tri_scr[...], g_lo[rs])

    if per_chunk_state:
        seg_len, seg_stride, n_seg = CHUNK, CHUNK + HIST_PAD, CHUNKS_PER_TILE
        for c in range(n_seg):
            base = c * seg_stride
            ext_scr[base:base + HIST_PAD, :] = jnp.zeros((HIST_PAD, POOL_WIDTH), F32)
            ext_scr[base + 1:base + HIST_PAD, :] = hist_in_ref[c]
            ext_scr[base + HIST_PAD:base + seg_stride, :] = proj_scr[c * CHUNK:(c + 1) * CHUNK, 0:POOL_WIDTH]
        row_pos = pos0 + lax.broadcasted_iota(jnp.int32, (seg_len, POOL_GROUP), 0)
    else:
        seg_len, seg_stride, n_seg = TILE_TOKENS, TILE_TOKENS + HIST_PAD, 1
        ext_scr[0:HIST_PAD, :] = jnp.where(t == 0, 0.0, ext_scr[0:HIST_PAD, :])
        ext_scr[HIST_PAD:seg_stride, :] = proj_scr[:, 0:POOL_WIDTH]
        row_pos = pos0 + t * TILE_TOKENS + lax.broadcasted_iota(jnp.int32, (seg_len, POOL_GROUP), 0)

    pooled_groups = []
    for gi, w in enumerate(POOL_WINDOWS):
        gs = slice(gi * POOL_GROUP, (gi + 1) * POOL_GROUP)
        cnt = jnp.minimum(row_pos + 1, w).astype(F32)
        ext = ext_scr[:, gs]
        win = ext
        shift = 1
        while shift < w:
            win = win + pltpu.roll(win, shift, 0)
            shift *= 2
        segs = []
        for s in range(n_seg):
            base = s * seg_stride + HIST_PAD
            segs.append(win[base:base + seg_len] / cnt - ext[base:base + seg_len])
        pooled = segs[0] if n_seg == 1 else jnp.concatenate(segs, axis=0)
        pooled_groups.append(pooled.astype(BF16))
    pool_cols = []
    for p in range(len(POOL_WINDOWS) // 2):
        both = jnp.concatenate(pooled_groups[2 * p:2 * p + 2], axis=1)
        pool_cols.append(_dot(both, w_pool_ref[p]))
    pool_out = jnp.concatenate(pool_cols, axis=1) * pscale_ref[...]

    if per_chunk_state:
        for c in range(n_seg):
            end = (c + 1) * seg_stride
            hist_out_ref[c] = ext_scr[end - POOL_HIST:end, :]
    else:
        hist_out_ref[...] = ext_scr[seg_stride - POOL_HIST:seg_stride, :]
        ext_scr[0:HIST_PAD, :] = ext_scr[TILE_TOKENS:seg_stride, :]

    hk = GLA_HEADS * GLA_DK
    hv = GLA_HEADS * GLA_DV
    pair_rows = 2 * CHUNK
    decay_cols = LANES // CHUNKS_PER_TILE

    def head_of(shape, dim, width):
        return lax.broadcasted_iota(jnp.int32, shape, dim) // width

    same_head_k = head_of((hk, hk), 0, CHUNK) == head_of((hk, hk), 1, GLA_DK)
    same_head_v = head_of((hk, hv), 0, CHUNK) == head_of((hk, hv), 1, GLA_DV)
    pair_half = head_of((pair_rows, hv), 0, CHUNK)
    causal = (lax.broadcasted_iota(jnp.int32, (CHUNK, hk), 0)
              >= lax.broadcasted_iota(jnp.int32, (CHUNK, hk), 1) % CHUNK)

    b_all = b_scr[...]
    b_last = [b_scr[(c + 1) * CHUNK - 1:(c + 1) * CHUNK, :] for c in range(CHUNKS_PER_TILE)]
    b_last_rows = jnp.concatenate([jnp.broadcast_to(bl, (CHUNK, hk)) for bl in b_last], axis=0)
    k_all = proj_scr[:, K0:V0]
    qt_all = (proj_scr[:, Q0:K0] * (GLA_DK ** -0.5) * jnp.exp(b_all)).astype(BF16)
    kt_all = k_all * jnp.exp(-b_all)
    kl_t = jnp.transpose(k_all * jnp.exp(b_last_rows - b_all)).astype(BF16)
    decay_t = jnp.transpose(jnp.exp(jnp.concatenate(
        [jnp.broadcast_to(bl, (decay_cols, hk)) for bl in b_last], axis=0)))

    def finish_rows(rs):
        r = proj_scr[rs, R0:N_MAIN]
        silu_r = r * (1.0 / (1.0 + jnp.exp(-r)))
        gated = []
        for h in range(GLA_HEADS):
            vs = slice(h * GLA_DV, (h + 1) * GLA_DV)
            oh = o_scr[rs, vs]
            ms = jnp.mean(oh * oh, axis=-1, keepdims=True)
            gated.append(oh * lax.rsqrt(ms + RMS_EPS) * gnorm_ref[...] * silu_r[:, vs])
        mix_in = jnp.concatenate([pool_out[rs]] + gated, axis=1).astype(BF16)
        resid = ALPHA * x[rs] + _dot(mix_in, w_out_bf[...])
        h_val = _layer_norm(resid, ln1g_ref[...], ln1b_ref[...])
        h_ref[rs, :] = h_val
        hb = h_val.astype(BF16)
        hb_ref[rs, :] = _pack_bf16_pairs(hb)
        logits_ref[:, rs] = _dot_nt(w_router_ref[...], hb) + b_router_ref[:, 0:1]

    st = None if per_chunk_state else jnp.where(t == 0, 0.0, st_scr[...])
    for c in range(CHUNKS_PER_TILE):
        rows = slice(c * CHUNK, (c + 1) * CHUNK)
        pair = slice((c // 2) * pair_rows, (c // 2 + 1) * pair_rows)
        if per_chunk_state:
            st = s_in_ref[c].reshape(hk, GLA_DV)
        qt = qt_all[rows]
        zero = jnp.zeros((), BF16)
        k_stack = jnp.where(same_head_k, jnp.concatenate([kt_all[rows].astype(BF16)] * GLA_HEADS, axis=0), zero)
        v_stack = jnp.where(same_head_v, jnp.concatenate(
            [proj_scr[rows, V0:R0].astype(BF16)] * GLA_HEADS, axis=0), zero)
        s_stack = jnp.where(same_head_v, jnp.concatenate([st.astype(BF16)] * GLA_HEADS, axis=1), zero)
        att = jnp.where(causal, _dot_nt(qt, k_stack), 0.0)
        o_scr[rows, :] = _dot(att.astype(BF16), v_stack) + _dot(qt, s_stack)
        v_chunk = jnp.where(pair_half == c % 2, proj_scr[pair, V0:R0], 0.0).astype(BF16)
        upd = jnp.concatenate(
            [_dot(kl_t[h * GLA_DK:(h + 1) * GLA_DK, pair], v_chunk[:, h * GLA_DV:(h + 1) * GLA_DV])
             for h in range(GLA_HEADS)], axis=0)
        st = st * decay_t[:, c * decay_cols:c * decay_cols + 1] + upd
        if per_chunk_state:
            s_out_ref[c] = st.reshape(GLA_HEADS, GLA_DK, GLA_DV)
        if (c + 1) % OUT_CHUNKS == 0:
            finish_rows(slice((c + 1 - OUT_CHUNKS) * CHUNK, (c + 1) * CHUNK))

    if not per_chunk_state:
        st_scr[...] = st
        s_out_ref[...] = st.reshape(GLA_HEADS, GLA_DK, GLA_DV)


def _const_spec(shape, single_buffer=False):
    nd = len(shape)
    if single_buffer:
        return pl.BlockSpec(shape, lambda *_: (0,) * nd, pipeline_mode=pl.Buffered(1))
    return pl.BlockSpec(shape, lambda *_: (0,) * nd)


def _mixer_weight_specs():
    return [
        _const_spec((None, D_MODEL, N_IN), single_buffer=True),
        _const_spec((D_MODEL, LANES)),
        _const_spec((LANES, GLA_HEADS * GLA_DK)),
        _const_spec((1, GLA_HEADS * GLA_DK)),
        _const_spec((len(POOL_WINDOWS) // 2, 2 * POOL_GROUP, 2 * POOL_GROUP)),
        _const_spec((1, POOL_WIDTH)),
        _const_spec((1, GLA_DV)),
        _const_spec((None, D_MODEL, D_MODEL), single_buffer=True),
        _const_spec((1, D_MODEL)),
        _const_spec((1, D_MODEL)),
        _const_spec((NUM_EXPERTS, D_MODEL)),
        _const_spec((NUM_EXPERTS, LANES)),
    ]


def _mixer_weights(w_in, w_pool, pool_scale, w_gate_up, b_gate, gla_norm_w, w_out, ln1_g, ln1_b,
                   w_router, b_router):
    w_glr = jnp.zeros((D_MODEL, LANES), BF16).at[:, :GATE_RANK].set(w_in[0, :, N_MAIN:].astype(BF16))
    w_gate = jnp.zeros((LANES, GLA_HEADS * GLA_DK), BF16).at[:GATE_RANK].set(w_gate_up[0].astype(BF16))
    wp = w_pool[0].astype(BF16)
    zero = jnp.zeros((POOL_GROUP, POOL_GROUP), BF16)
    w_pool_pairs = jnp.stack([jnp.block([[wp[2 * p], zero], [zero, wp[2 * p + 1]]])
                              for p in range(len(POOL_WINDOWS) // 2)])
    weights = (
        w_in, w_glr, w_gate, b_gate[0][None, :],
        w_pool_pairs, pool_scale[0][None, :], gla_norm_w[0][None, :],
        w_out, ln1_g[0][None, :], ln1_b[0][None, :],
        w_router[0].T.astype(BF16), jnp.broadcast_to(b_router[0][:, None], (NUM_EXPERTS, LANES)),
    )
    assert len(weights) == N_MIXER_WEIGHTS
    return weights


def _mixer_scratch(per_chunk_state):
    ext_rows = (CHUNKS_PER_TILE * (CHUNK + HIST_PAD)) if per_chunk_state else (TILE_TOKENS + HIST_PAD)
    return [
        pltpu.VMEM((TILE_TOKENS, N_MAIN), F32),
        pltpu.VMEM((TILE_TOKENS, GLA_HEADS * GLA_DK), F32),
        pltpu.VMEM((TILE_TOKENS, GLA_HEADS * GLA_DV), F32),
        pltpu.VMEM((ext_rows, POOL_WIDTH), F32),
        pltpu.VMEM((GLA_HEADS * GLA_DK, GLA_DV), F32),
        pltpu.VMEM((SCAN_ROWS, SCAN_ROWS), BF16),
        pltpu.VMEM((D_MODEL, N_MAIN), BF16),
        pltpu.VMEM((D_MODEL, D_MODEL), BF16),
    ]


def _mixer_out_shapes(n, bsz):
    return (
        jax.ShapeDtypeStruct((n, D_MODEL), F32),
        jax.ShapeDtypeStruct((n, HALF), jnp.int32),
        jax.ShapeDtypeStruct((NUM_EXPERTS, n), F32),
        jax.ShapeDtypeStruct((bsz, POOL_HIST, POOL_WIDTH), F32),
        jax.ShapeDtypeStruct((bsz, GLA_HEADS, GLA_DK, GLA_DV), F32),
    )


def _mixer_prompt(x, weights):
    bsz, seq, _ = x.shape
    tiles = seq // TILE_TOKENS
    n_total = bsz * seq
    return pl.pallas_call(
        functools.partial(_mixer_kernel, False, 0),
        grid=(bsz, tiles),
        in_specs=[pl.BlockSpec((None, TILE_TOKENS, D_MODEL), lambda b, t: (b, t, 0))] + _mixer_weight_specs(),
        out_specs=(
            pl.BlockSpec((TILE_TOKENS, D_MODEL), lambda b, t: (b * tiles + t, 0)),
            pl.BlockSpec((TILE_TOKENS, HALF), lambda b, t: (b * tiles + t, 0)),
            pl.BlockSpec((NUM_EXPERTS, TILE_TOKENS), lambda b, t: (0, b * tiles + t)),
            pl.BlockSpec((None, POOL_HIST, POOL_WIDTH), lambda b, t: (b, 0, 0)),
            pl.BlockSpec((None, GLA_HEADS, GLA_DK, GLA_DV), lambda b, t: (b, 0, 0, 0)),
        ),
        out_shape=_mixer_out_shapes(n_total, bsz),
        scratch_shapes=_mixer_scratch(False),
        compiler_params=pltpu.CompilerParams(
            dimension_semantics=("arbitrary", "arbitrary"), vmem_limit_bytes=VMEM_LIMIT),
        name="mixer_prompt",
    )(x, *weights)


def _mixer_sample(x, hist, state, weights):
    bsz = x.shape[0]
    tiles = bsz // CHUNKS_PER_TILE
    return pl.pallas_call(
        functools.partial(_mixer_kernel, True, PAST_LEN),
        grid=(tiles,),
        in_specs=[
            pl.BlockSpec((CHUNKS_PER_TILE, CHUNK, D_MODEL), lambda i: (i, 0, 0)),
            pl.BlockSpec((CHUNKS_PER_TILE, POOL_HIST, POOL_WIDTH), lambda i: (i, 0, 0)),
            pl.BlockSpec((CHUNKS_PER_TILE, GLA_HEADS, GLA_DK, GLA_DV), lambda i: (i, 0, 0, 0)),
        ] + _mixer_weight_specs(),
        out_specs=(
            pl.BlockSpec((TILE_TOKENS, D_MODEL), lambda i: (i, 0)),
            pl.BlockSpec((TILE_TOKENS, HALF), lambda i: (i, 0)),
            pl.BlockSpec((NUM_EXPERTS, TILE_TOKENS), lambda i: (0, i)),
            pl.BlockSpec((CHUNKS_PER_TILE, POOL_HIST, POOL_WIDTH), lambda i: (i, 0, 0)),
            pl.BlockSpec((CHUNKS_PER_TILE, GLA_HEADS, GLA_DK, GLA_DV), lambda i: (i, 0, 0, 0)),
        ),
        out_shape=_mixer_out_shapes(bsz * CHUNK, bsz),
        scratch_shapes=_mixer_scratch(True),
        compiler_params=pltpu.CompilerParams(
            dimension_semantics=("arbitrary",), vmem_limit_bytes=VMEM_LIMIT),
        name="mixer_sample",
    )(x, hist, state, *weights)


def _router_kernel(lt_ref, gates_ref, dest_ref, padend_ref, cnt_scr, base_scr, pstart_scr, before_scr,
                   topk_scr):
    phase = pl.program_id(0)
    i = pl.program_id(1)
    shape = (NUM_EXPERTS, ROUTE_TOKENS)
    row = lax.broadcasted_iota(jnp.int32, shape, 0)

    def tile_counts_of(chosen):
        return jnp.broadcast_to(jnp.sum(chosen, axis=1, keepdims=True), (NUM_EXPERTS, LANES))

    @pl.when(phase == 0)
    def _():
        @pl.when(i == 0)
        def _():
            cnt_scr[...] = jnp.zeros_like(cnt_scr)

        logits = lt_ref[...]
        idxs, vals = [], []
        chosen = jnp.zeros(shape, F32)
        for _ in range(TOP_K):
            m = jnp.max(logits, axis=0, keepdims=True)
            idx = jnp.min(jnp.where(logits == m, row, NUM_EXPERTS), axis=0, keepdims=True)
            hit = row == idx
            idxs.append(idx.astype(F32))
            vals.append(m)
            chosen = chosen + jnp.where(hit, 1.0, 0.0)
            logits = jnp.where(hit, -jnp.inf, logits)
        topk_scr[i] = jnp.concatenate(idxs + vals, axis=0)
        cnt_scr[...] += tile_counts_of(chosen)

    @pl.when(phase == 1)
    def _():
        @pl.when(i == 0)
        def _():
            blocks = jnp.floor((cnt_scr[...] + (MOE_ROWS - 1)) * (1.0 / MOE_ROWS))
            erow = lax.broadcasted_iota(jnp.int32, (NUM_EXPERTS, LANES), 0)
            lane = lax.broadcasted_iota(jnp.int32, (NUM_EXPERTS, LANES), 1)
            cum = blocks
            shift = 1
            while shift < NUM_EXPERTS:
                cum = cum + jnp.where(erow >= shift, pltpu.roll(cum, shift, 0), 0.0)
                shift *= 2
            padend_ref[...] = jnp.where(lane == 1, cnt_scr[...], cum * MOE_ROWS)
            pstart_scr[...] = (cum - blocks) * MOE_ROWS
            base_scr[...] = jnp.zeros_like(base_scr)
            ti = lax.broadcasted_iota(jnp.int32, (ROUTE_TOKENS, ROUTE_TOKENS), 0)
            tj = lax.broadcasted_iota(jnp.int32, (ROUTE_TOKENS, ROUTE_TOKENS), 1)
            before_scr[...] = jnp.where(ti < tj, 1.0, 0.0).astype(BF16)

        topk = topk_scr[i]
        sel = [row == topk[k:k + 1, :].astype(jnp.int32) for k in range(TOP_K)]
        vals = [topk[TOP_K + k:TOP_K + k + 1, :] for k in range(TOP_K)]
        chosen = sum(jnp.where(hit, 1.0, 0.0) for hit in sel)
        earlier = _dot(chosen.astype(BF16), before_scr[...])
        pos = pstart_scr[:, 0:1] + base_scr[:, 0:1] + earlier
        dest = [jnp.sum(jnp.where(hit, pos, 0.0), axis=0, keepdims=True) for hit in sel]
        dest_ref[...] = jnp.concatenate(dest, axis=0).astype(jnp.int32)
        ex = [jnp.exp(v - vals[0]) for v in vals]
        denom = ex[0] + ex[1] + ex[2] + ex[3]
        gates_ref[...] = jnp.concatenate([e / denom for e in ex], axis=0)
        base_scr[...] += tile_counts_of(chosen)


def _router(logits_t):
    n = logits_t.shape[1]
    assert n % ROUTE_TOKENS == 0
    tiles = n // ROUTE_TOKENS
    return pl.pallas_call(
        _router_kernel,
        grid=(2, tiles),
        in_specs=[pl.BlockSpec((NUM_EXPERTS, ROUTE_TOKENS), lambda p, i: (0, i))],
        out_specs=(
            pl.BlockSpec((TOP_K, ROUTE_TOKENS), lambda p, i: (0, i * p)),
            pl.BlockSpec((TOP_K, ROUTE_TOKENS), lambda p, i: (0, i * p)),
            pl.BlockSpec((NUM_EXPERTS, LANES), lambda p, i: (0, 0)),
        ),
        out_shape=(
            jax.ShapeDtypeStruct((TOP_K, n), F32),
            jax.ShapeDtypeStruct((TOP_K, n), jnp.int32),
            jax.ShapeDtypeStruct((NUM_EXPERTS, LANES), F32),
        ),
        scratch_shapes=[pltpu.VMEM((NUM_EXPERTS, LANES), F32)] * 3
        + [pltpu.VMEM((ROUTE_TOKENS, ROUTE_TOKENS), BF16),
           pltpu.VMEM((tiles, 2 * TOP_K, ROUTE_TOKENS), F32)],
        compiler_params=pltpu.CompilerParams(
            dimension_semantics=("arbitrary", "arbitrary"), vmem_limit_bytes=VMEM_LIMIT),
        name="router",
    )(logits_t)


SC_CORES = 2
SC_SUBCORES = 16
SC_WORKERS = SC_CORES * SC_SUBCORES
DISPATCH_ROWS = 64


def _dispatch(h_sources, dest_chunks, m_pad):
    shares, first_chunk = [], 0
    for src in h_sources:
        n_chunks = src.shape[0] // DISPATCH_ROWS
        assert src.shape[0] % DISPATCH_ROWS == 0 and n_chunks % SC_WORKERS == 0
        shares.append((first_chunk, n_chunks // SC_WORKERS))
        first_chunk += n_chunks
    plan = [(s, first, per, j) for s, (first, per) in enumerate(shares) for j in range(per)]
    per_worker = len(plan)
    mesh = plsc.VectorSubcoreMesh(core_axis_name="c", subcore_axis_name="s")

    @functools.partial(
        pl.kernel, mesh=mesh,
        out_type=jax.ShapeDtypeStruct((m_pad, HALF), jnp.int32),
        scratch_types=[
            pltpu.VMEM((2, TOP_K, DISPATCH_ROWS), jnp.int32),
            pltpu.VMEM((2, DISPATCH_ROWS, HALF), jnp.int32),
            pltpu.SemaphoreType.DMA((2,)),
            pltpu.SemaphoreType.DMA((2,)),
        ],
        compiler_params=pltpu.CompilerParams(use_tc_tiling_on_sc=True),
        name="dispatch",
    )
    def dispatch_kernel(*refs):
        h_hbms = refs[:len(h_sources)]
        dest_hbm, out_hbm, idx_v, rows_v, load_sems, scatter_sems = refs[len(h_sources):]
        wid = lax.axis_index("s") * SC_CORES + lax.axis_index("c")

        def loads(j):
            src, first, per, k = plan[j]
            local = wid * per + k
            slot = j % 2
            return (
                pltpu.make_async_copy(dest_hbm.at[first + local], idx_v.at[slot], load_sems.at[slot]),
                pltpu.make_async_copy(h_hbms[src].at[pl.ds(local * DISPATCH_ROWS, DISPATCH_ROWS)],
                                      rows_v.at[slot], load_sems.at[slot]),
            )

        def scatters(j):
            slot = j % 2
            return [pltpu.make_async_copy(rows_v.at[slot], out_hbm.at[idx_v.at[slot, k]],
                                          scatter_sems.at[slot]) for k in range(TOP_K)]

        for cp in loads(0):
            cp.start()
        for j in range(per_worker):
            for cp in loads(j):
                cp.wait()
            if j >= 1:
                for cp in scatters(j - 1):
                    cp.wait()
            if j + 1 < per_worker:
                for cp in loads(j + 1):
                    cp.start()
            for cp in scatters(j):
                cp.start()
        for cp in scatters(per_worker - 1):
            cp.wait()

    return dispatch_kernel(*h_sources, dest_chunks)


def _gather_expert_rows(y_sorted, dest_chunks, row_offset, n):
    n_chunks = n // DISPATCH_ROWS
    assert n_chunks % SC_WORKERS == 0 and row_offset % DISPATCH_ROWS == 0
    per_worker = n_chunks // SC_WORKERS
    chunk0 = row_offset // DISPATCH_ROWS
    mesh = plsc.VectorSubcoreMesh(core_axis_name="c", subcore_axis_name="s")

    @functools.partial(
        pl.kernel, mesh=mesh,
        out_type=jax.ShapeDtypeStruct((TOP_K, n, HALF), jnp.int32),
        scratch_types=[
            pltpu.VMEM((TOP_K, DISPATCH_ROWS), jnp.int32),
            pltpu.VMEM((2, DISPATCH_ROWS, HALF), jnp.int32),
            pltpu.SemaphoreType.DMA((2,)),
        ],
        compiler_params=pltpu.CompilerParams(use_tc_tiling_on_sc=True),
        name="gather_expert_rows",
    )
    def gather_kernel(y_hbm, dest_hbm, out_hbm, idx_v, rows_v, sems):
        wid = lax.axis_index("s") * SC_CORES + lax.axis_index("c")

        def gather(k):
            return pltpu.make_async_copy(y_hbm.at[idx_v.at[k]], rows_v.at[k % 2], sems.at[k % 2])

        @pl.loop(0, per_worker)
        def _(j):
            local = wid * per_worker + j
            pltpu.sync_copy(dest_hbm.at[chunk0 + local], idx_v)
            gather(0).start()
            for k in range(TOP_K):
                if k + 1 < TOP_K:
                    gather(k + 1).start()
                gather(k).wait()
                pltpu.sync_copy(rows_v.at[k % 2],
                                out_hbm.at[k, pl.ds(local * DISPATCH_ROWS, DISPATCH_ROWS)])

    return gather_kernel(y_sorted, dest_chunks)


def _moe_kernel(be_ref, wsel_ref, half_ref, nused_ref, x_ref, wgu_ref, bgu_ref, wd_ref, bd_ref, y_ref,
                wgu_bf, wd_bf):
    del wsel_ref
    i = pl.program_id(0)

    def ffn(rows):
        gu = _dot(_unpack_bf16_pairs(x_ref[rows, :]), wgu_bf[...]) + bgu_ref[...]
        gate = jnp.minimum(gu[:, :EXPERT_FF], SWIGLU_LIMIT)
        up = jnp.clip(gu[:, EXPERT_FF:], -SWIGLU_LIMIT, SWIGLU_LIMIT)
        hmid = gate * (1.0 / (1.0 + jnp.exp(-SWIGLU_ALPHA * gate))) * (up + 1.0)
        y = _dot(hmid.astype(BF16), wd_bf[...]) + bd_ref[...]
        y_ref[rows, :] = _pack_bf16_pairs(y.astype(BF16))

    @pl.when(i < nused_ref[0])
    def _():
        @pl.when((i == 0) | (be_ref[i] != be_ref[jnp.maximum(i - 1, 0)]))
        def _():
            wgu_bf[...] = wgu_ref[...].astype(BF16)
            wd_bf[...] = wd_ref[...].astype(BF16)

        for units in range(1, MOE_ROWS // MOE_TAIL_ROWS + 1):
            @pl.when(half_ref[i] == units)
            def _(units=units):
                ffn(slice(0, units * MOE_TAIL_ROWS))


def _moe_experts(block_expert, weight_expert, half_block, n_used, x_sorted, w_gu, b_gu, w_down, b_down):
    m_pad = x_sorted.shape[0]
    n_blocks = m_pad // MOE_ROWS

    def blk(i, be, ws, hb, nu):
        return jnp.minimum(i, nu[0] - 1)

    def expert(i, be, ws, hb, nu):
        return be[blk(i, be, ws, hb, nu)]

    def held(i, be, ws, hb, nu):
        return ws[blk(i, be, ws, hb, nu)]

    grid_spec = pltpu.PrefetchScalarGridSpec(
        num_scalar_prefetch=4,
        grid=(n_blocks,),
        in_specs=[
            pl.BlockSpec((MOE_ROWS, HALF), lambda *a: (blk(*a), 0)),
            pl.BlockSpec((None, D_MODEL, 2 * EXPERT_FF), lambda *a: (held(*a), 0, 0)),
            pl.BlockSpec((None, 1, 2 * EXPERT_FF), lambda *a: (expert(*a), 0, 0)),
            pl.BlockSpec((None, EXPERT_FF, D_MODEL), lambda *a: (held(*a), 0, 0)),
            pl.BlockSpec((None, 1, D_MODEL), lambda *a: (expert(*a), 0, 0)),
        ],
        out_specs=pl.BlockSpec((MOE_ROWS, HALF), lambda *a: (blk(*a), 0)),
        scratch_shapes=[
            pltpu.VMEM((D_MODEL, 2 * EXPERT_FF), BF16),
            pltpu.VMEM((EXPERT_FF, D_MODEL), BF16),
        ],
    )
    return pl.pallas_call(
        _moe_kernel,
        grid_spec=grid_spec,
        out_shape=jax.ShapeDtypeStruct((m_pad, HALF), jnp.int32),
        compiler_params=pltpu.CompilerParams(
            dimension_semantics=("arbitrary",), vmem_limit_bytes=VMEM_LIMIT),
        name="moe_experts",
    )(block_expert, weight_expert, half_block, n_used, x_sorted, w_gu, b_gu, w_down, b_down)


def _combine_kernel(yk_ref, gates_ref, h_ref, g_ref, b_ref, out_ref):
    pad = jnp.zeros((LANES - TOP_K, COMBINE_TOKENS), F32)
    gates = jnp.transpose(jnp.concatenate([gates_ref[...], pad], axis=0))
    lo = hi = None
    for k in range(TOP_K):
        u = lax.bitcast_convert_type(yk_ref[k], jnp.uint32)
        gk = gates[:, k:k + 1]
        lo_k = lax.bitcast_convert_type(u << 16, F32) * gk
        hi_k = lax.bitcast_convert_type(u & jnp.uint32(HI_MASK), F32) * gk
        lo = lo_k if lo is None else lo + lo_k
        hi = hi_k if hi is None else hi + hi_k
    acc = ALPHA * h_ref[...] + jnp.concatenate([lo, hi], axis=1)
    out_ref[...] = _layer_norm(acc, g_ref[...], b_ref[...])


def _combine_kernel_aliased(yk_ref, gates_ref, h_ref, g_ref, b_ref, prev_ref, out_ref):
    del prev_ref
    _combine_kernel(yk_ref, gates_ref, h_ref, g_ref, b_ref, out_ref)


def _combine(yk, gates, token_offset, h_src, h_offset, ln_g, ln_b, out_prev):
    n_seg = yk.shape[1]
    out_rows = h_src.shape[0]
    tile0 = token_offset // COMBINE_TOKENS
    out_tile0 = h_offset // COMBINE_TOKENS
    in_specs = [
        pl.BlockSpec((TOP_K, COMBINE_TOKENS, HALF), lambda i: (0, i, 0)),
        pl.BlockSpec((TOP_K, COMBINE_TOKENS), lambda i: (0, tile0 + i)),
        pl.BlockSpec((COMBINE_TOKENS, D_MODEL), lambda i: (out_tile0 + i, 0)),
        _const_spec((1, D_MODEL)),
        _const_spec((1, D_MODEL)),
    ]
    args = [yk, gates, h_src, ln_g, ln_b]
    aliases = {}
    kern = _combine_kernel
    if out_prev is not None:
        in_specs.append(pl.BlockSpec(memory_space=pl.ANY))
        aliases = {len(args): 0}
        args.append(out_prev)
        kern = _combine_kernel_aliased
    return pl.pallas_call(
        kern,
        grid=(n_seg // COMBINE_TOKENS,),
        in_specs=in_specs,
        out_specs=pl.BlockSpec((COMBINE_TOKENS, D_MODEL), lambda i: (out_tile0 + i, 0)),
        out_shape=jax.ShapeDtypeStruct((out_rows, D_MODEL), F32),
        input_output_aliases=aliases,
        compiler_params=pltpu.CompilerParams(
            dimension_semantics=("arbitrary",), vmem_limit_bytes=VMEM_LIMIT),
        name="moe_combine",
    )(*args)


def kernel(x_prompt, x_sample, state_pool, state_gla, w_in, w_pool, pool_scale, w_gate_up, b_gate,
           gla_norm_w, w_out, ln1_g, ln1_b, w_router, b_router, w_gu, b_gu, w_down, b_down,
           ln2_g, ln2_b):
    assert w_in.shape[0] == 1, "single-layer kernel"
    bp, seq, _ = x_prompt.shape
    bs, dec_seq, _ = x_sample.shape
    assert dec_seq == CHUNK and seq % TILE_TOKENS == 0 and bs % CHUNKS_PER_TILE == 0
    n_prompt = bp * seq
    n_sample = bs * dec_seq
    n_total = n_prompt + n_sample
    nk = n_total * TOP_K
    n_blocks = -(-nk // MOE_ROWS) + NUM_EXPERTS
    m_pad = n_blocks * MOE_ROWS

    weights = _mixer_weights(w_in, w_pool, pool_scale, w_gate_up, b_gate, gla_norm_w, w_out,
                             ln1_g, ln1_b, w_router, b_router)

    h_p, hb_p, logits_p, hist_p, s_p = _mixer_prompt(x_prompt, weights)
    h_s, hb_s, logits_s, hist_s, s_s = _mixer_sample(x_sample, state_pool[0], state_gla[0], weights)

    gates_t, dest_t, layout = _router(jnp.concatenate([logits_p, logits_s], axis=1))
    pad_end = layout[:, 0].astype(jnp.int32)
    counts = layout[:, 1].astype(jnp.int32)
    block_start = jnp.arange(n_blocks, dtype=jnp.int32) * MOE_ROWS
    block_expert = jnp.minimum(jnp.sum((block_start[:, None] >= pad_end[None, :]).astype(jnp.int32), axis=1),
                               NUM_EXPERTS - 1)
    n_used = (pad_end[-1:] // MOE_ROWS).astype(jnp.int32)
    is_first = jnp.concatenate([jnp.ones((1,), bool), block_expert[1:] != block_expert[:-1]])
    blocks = jnp.arange(n_blocks, dtype=jnp.int32)
    later_other = ((blocks[None, :] > blocks[:, None]) & (blocks[None, :] < n_used[0])
                   & (block_expert[None, :] != block_expert[:, None]))
    next_expert = jnp.min(jnp.where(later_other, block_expert[None, :], NUM_EXPERTS), axis=1)
    next_expert = jnp.where(next_expert == NUM_EXPERTS, block_expert, next_expert)
    weight_expert = jnp.where(is_first, block_expert, next_expert).astype(jnp.int32)
    of_expert = block_expert[:, None] == jnp.arange(NUM_EXPERTS, dtype=jnp.int32)[None, :]
    seg_end = jnp.sum(jnp.where(of_expert, (pad_end - (-counts % MOE_ROWS))[None, :], 0), axis=1)
    half_block = jnp.clip((seg_end - block_start + MOE_TAIL_ROWS - 1) // MOE_TAIL_ROWS,
                          1, MOE_ROWS // MOE_TAIL_ROWS).astype(jnp.int32)

    dest_chunks = dest_t.reshape(TOP_K, n_total // DISPATCH_ROWS, DISPATCH_ROWS).transpose(1, 0, 2)
    x_sorted = _dispatch((hb_p, hb_s), dest_chunks, m_pad)
    y_sorted = _moe_experts(block_expert, weight_expert, half_block, n_used, x_sorted, w_gu[0], b_gu[0][:, None, :],
                            w_down[0], b_down[0][:, None, :])
    gates = gates_t
    ln_g, ln_b = ln2_g[0][None, :], ln2_b[0][None, :]
    yk = _gather_expert_rows(y_sorted, dest_chunks, n_prompt, n_sample)
    y_sample = _combine(yk, gates, n_prompt, h_s, 0, ln_g, ln_b, None)
    unit = SC_WORKERS * DISPATCH_ROWS
    assert n_prompt % unit == 0
    sizes, left, size = [], n_prompt // unit, 1
    while left > 0:
        size = min(left, size)
        sizes.append(size * unit)
        left -= size
        size = max(size + 1, (3 * size) // 2)
    y_prompt, start = None, 0
    for seg in sizes:
        yk = _gather_expert_rows(y_sorted, dest_chunks, start, seg)
        y_prompt = _combine(yk, gates, start, h_p, start, ln_g, ln_b, y_prompt)
        start += seg
    y_prompt = y_prompt.reshape(bp, seq, D_MODEL)
    y_sample = y_sample.reshape(bs, dec_seq, D_MODEL)
    return (y_prompt, y_sample, hist_p[None], s_p[None], hist_s[None], s_s[None])
```

```python
import functools

import jax
import jax.numpy as jnp
from jax import lax
from jax.experimental import pallas as pl
from jax.experimental.pallas import tpu as pltpu
from jax.experimental.pallas import tpu_sc as plsc

F32 = jnp.float32
BF16 = jnp.bfloat16

D_MODEL = 1024
CHUNK = 64
PAST_LEN = 1024
POOL_WIDTH = 512
POOL_WINDOWS = (2, 4, 8, 16)
POOL_GROUP = 128
POOL_HIST = 15
GLA_HEADS = 4
GLA_DK = 64
GLA_DV = 128
GATE_RANK = 16
GATE_NORMALIZER = 16.0
NUM_EXPERTS = 32
TOP_K = 4
EXPERT_FF = 1024
SWIGLU_LIMIT = 7.0
SWIGLU_ALPHA = 1.702
LN_EPS = 1e-5
RMS_EPS = 1e-6
ALPHA = 2.0 ** 0.25

Q0 = POOL_WIDTH
K0 = Q0 + GLA_HEADS * GLA_DK
V0 = K0 + GLA_HEADS * GLA_DK
R0 = V0 + GLA_HEADS * GLA_DV
N_MAIN = R0 + GLA_HEADS * GLA_DV
N_IN = N_MAIN + GATE_RANK

LANES = 128
TILE_TOKENS = 512
CHUNKS_PER_TILE = TILE_TOKENS // CHUNK
HIST_PAD = 16
OUT_CHUNKS = 4
SCAN_ROWS = 256
MOE_ROWS = 1024
MOE_TAIL_ROWS = 256
ROUTE_TOKENS = 2048
COMBINE_TOKENS = 1024
VMEM_LIMIT = 56 * 1024 * 1024


def _dot(a, b):
    return jnp.dot(a, b, preferred_element_type=F32)


def _dot_nt(a, b):
    return lax.dot_general(a, b, (((1,), (1,)), ((), ())), preferred_element_type=F32)


def _dot_tn(a, b):
    return lax.dot_general(a, b, (((0,), (0,)), ((), ())), preferred_element_type=F32)


HALF = D_MODEL // 2
HI_MASK = 0xFFFF0000


def _pack_bf16_pairs(xb):
    lo = lax.bitcast_convert_type(xb[:, :HALF].astype(F32), jnp.uint32) >> 16
    hi = lax.bitcast_convert_type(xb[:, HALF:].astype(F32), jnp.uint32) & jnp.uint32(HI_MASK)
    return lax.bitcast_convert_type(hi | lo, jnp.int32)


def _unpack_bf16_pairs(p):
    u = lax.bitcast_convert_type(p, jnp.uint32)
    lo = lax.bitcast_convert_type(u << 16, F32)
    hi = lax.bitcast_convert_type(u & jnp.uint32(HI_MASK), F32)
    return jnp.concatenate([lo, hi], axis=1).astype(BF16)


def _layer_norm(v, g, b):
    mu = jnp.mean(v, axis=-1, keepdims=True)
    c = v - mu
    var = jnp.mean(c * c, axis=-1, keepdims=True)
    return c * lax.rsqrt(var + LN_EPS) * g + b


N_MIXER_WEIGHTS = 12


def _mixer_kernel(per_chunk_state, pos0, *refs):
    if per_chunk_state:
        (x_ref, hist_in_ref, s_in_ref, *rest) = refs
    else:
        (x_ref, *rest) = refs
        hist_in_ref = s_in_ref = None
    (w_main_ref, w_glr_ref, w_gate_ref, b_gate_ref, w_pool_ref, pscale_ref, gnorm_ref,
     w_out_ref, ln1g_ref, ln1b_ref, w_router_ref, b_router_ref, *rest) = rest
    (h_ref, hb_ref, logits_ref, hist_out_ref, s_out_ref,
     proj_scr, b_scr, o_scr, ext_scr, st_scr, tri_scr, w_main_bf, w_out_bf) = rest

    if per_chunk_state:
        t = None
        first_step = pl.program_id(0) == 0
    else:
        t = pl.program_id(1)
        first_step = (pl.program_id(0) == 0) & (t == 0)
    x = x_ref[...].reshape(TILE_TOKENS, D_MODEL)
    xb = x.astype(BF16)

    @pl.when(first_step)
    def _():
        ti = lax.broadcasted_iota(jnp.int32, (SCAN_ROWS, SCAN_ROWS), 0)
        tj = lax.broadcasted_iota(jnp.int32, (SCAN_ROWS, SCAN_ROWS), 1)
        same_chunk = (ti // CHUNK) == (tj // CHUNK)
        tri_scr[...] = jnp.where(same_chunk & (ti >= tj), 1.0, 0.0).astype(BF16)
        w_main_bf[...] = w_main_ref[:, 0:N_MAIN].astype(BF16)
        w_out_bf[...] = w_out_ref[...].astype(BF16)
        if not per_chunk_state:
            st_scr[...] = jnp.zeros_like(st_scr)
            ext_scr[0:HIST_PAD, :] = jnp.zeros((HIST_PAD, POOL_WIDTH), F32)

    glr = _dot(xb, w_glr_ref[...])
    proj_scr[:, 0:V0] = _dot(xb, w_main_bf[:, 0:V0])
    gk = _dot(glr.astype(BF16), w_gate_ref[...]) + b_gate_ref[...]
    log_sig = jnp.minimum(gk, 0.0) - jnp.log1p(jnp.exp(-jnp.abs(gk)))
    g = log_sig / GATE_NORMALIZER
    g_hi = g.astype(BF16)
    g_lo = (g - g_hi.astype(F32)).astype(BF16)
    proj_scr[:, V0:N_MAIN] = _dot(xb, w_main_bf[:, V0:N_MAIN])
    for s in range(TILE_TOKENS // SCAN_ROWS):
        rs = slice(s * SCAN_ROWS, (s + 1) * SCAN_ROWS)
        b_scr[rs, :] = _dot(tri_scr[...], g_hi[rs]) + _dot(tri_scr[...], g_lo[rs])

    if per_chunk_state:
        seg_len, seg_stride, n_seg = CHUNK, CHUNK + HIST_PAD, CHUNKS_PER_TILE
        for c in range(n_seg):
            base = c * seg_stride
            ext_scr[base:base + HIST_PAD, :] = jnp.zeros((HIST_PAD, POOL_WIDTH), F32)
            ext_scr[base + 1:base + HIST_PAD, :] = hist_in_ref[c]
            ext_scr[base + HIST_PAD:base + seg_stride, :] = proj_scr[c * CHUNK:(c + 1) * CHUNK, 0:POOL_WIDTH]
        row_pos = pos0 + lax.broadcasted_iota(jnp.int32, (seg_len, POOL_GROUP), 0)
    else:
        seg_len, seg_stride, n_seg = TILE_TOKENS, TILE_TOKENS + HIST_PAD, 1
        ext_scr[0:HIST_PAD, :] = jnp.where(t == 0, 0.0, ext_scr[0:HIST_PAD, :])
        ext_scr[HIST_PAD:seg_stride, :] = proj_scr[:, 0:POOL_WIDTH]
        row_pos = pos0 + t * TILE_TOKENS + lax.broadcasted_iota(jnp.int32, (seg_len, POOL_GROUP), 0)

    pooled_groups = []
    for gi, w in enumerate(POOL_WINDOWS):
        gs = slice(gi * POOL_GROUP, (gi + 1) * POOL_GROUP)
        cnt = jnp.minimum(row_pos + 1, w).astype(F32)
        ext = ext_scr[:, gs]
        win = ext
        shift = 1
        while shift < w:
            win = win + pltpu.roll(win, shift, 0)
            shift *= 2
        segs = []
        for s in range(n_seg):
            base = s * seg_stride + HIST_PAD
            segs.append(win[base:base + seg_len] / cnt - ext[base:base + seg_len])
        pooled = segs[0] if n_seg == 1 else jnp.concatenate(segs, axis=0)
        pooled_groups.append(pooled.astype(BF16))
    pool_cols = []
    for p in range(len(POOL_WINDOWS) // 2):
        both = jnp.concatenate(pooled_groups[2 * p:2 * p + 2], axis=1)
        pool_cols.append(_dot(both, w_pool_ref[p]))
    pool_out = jnp.concatenate(pool_cols, axis=1) * pscale_ref[...]

    if per_chunk_state:
        for c in range(n_seg):
            end = (c + 1) * seg_stride
            hist_out_ref[c] = ext_scr[end - POOL_HIST:end, :]
    else:
        hist_out_ref[...] = ext_scr[seg_stride - POOL_HIST:seg_stride, :]
        ext_scr[0:HIST_PAD, :] = ext_scr[TILE_TOKENS:seg_stride, :]

    hk = GLA_HEADS * GLA_DK
    hv = GLA_HEADS * GLA_DV
    pair_rows = 2 * CHUNK
    decay_cols = LANES // CHUNKS_PER_TILE

    def head_of(shape, dim, width):
        return lax.broadcasted_iota(jnp.int32, shape, dim) // width

    same_head_k = head_of((hk, hk), 0, CHUNK) == head_of((hk, hk), 1, GLA_DK)
    same_head_v = head_of((hk, hv), 0, CHUNK) == head_of((hk, hv), 1, GLA_DV)
    pair_half = head_of((pair_rows, hv), 0, CHUNK)
    causal = (lax.broadcasted_iota(jnp.int32, (CHUNK, hk), 0)
              >= lax.broadcasted_iota(jnp.int32, (CHUNK, hk), 1) % CHUNK)

    b_all = b_scr[...]
    b_last = [b_scr[(c + 1) * CHUNK - 1:(c + 1) * CHUNK, :] for c in range(CHUNKS_PER_TILE)]
    b_last_rows = jnp.concatenate([jnp.broadcast_to(bl, (CHUNK, hk)) for bl in b_last], axis=0)
    k_all = proj_scr[:, K0:V0]
    qt_all = (proj_scr[:, Q0:K0] * (GLA_DK ** -0.5) * jnp.exp(b_all)).astype(BF16)
    kt_all = k_all * jnp.exp(-b_all)
    kl_t = jnp.transpose(k_all * jnp.exp(b_last_rows - b_all)).astype(BF16)
    decay_t = jnp.transpose(jnp.exp(jnp.concatenate(
        [jnp.broadcast_to(bl, (decay_cols, hk)) for bl in b_last], axis=0)))

    def finish_rows(rs):
        r = proj_scr[rs, R0:N_MAIN]
        silu_r = r * (1.0 / (1.0 + jnp.exp(-r)))
        gated = []
        for h in range(GLA_HEADS):
            vs = slice(h * GLA_DV, (h + 1) * GLA_DV)
            oh = o_scr[rs, vs]
            ms = jnp.mean(oh * oh, axis=-1, keepdims=True)
            gated.append(oh * lax.rsqrt(ms + RMS_EPS) * gnorm_ref[...] * silu_r[:, vs])
        mix_in = jnp.concatenate([pool_out[rs]] + gated, axis=1).astype(BF16)
        resid = ALPHA * x[rs] + _dot(mix_in, w_out_bf[...])
        h_val = _layer_norm(resid, ln1g_ref[...], ln1b_ref[...])
        h_ref[rs, :] = h_val
        hb = h_val.astype(BF16)
        hb_ref[rs, :] = _pack_bf16_pairs(hb)
        logits_ref[:, rs] = _dot_nt(w_router_ref[...], hb) + b_router_ref[:, 0:1]

    st = None if per_chunk_state else jnp.where(t == 0, 0.0, st_scr[...])
    for c in range(CHUNKS_PER_TILE):
        rows = slice(c * CHUNK, (c + 1) * CHUNK)
        pair = slice((c // 2) * pair_rows, (c // 2 + 1) * pair_rows)
        if per_chunk_state:
            st = s_in_ref[c].reshape(hk, GLA_DV)
        qt = qt_all[rows]
        zero = jnp.zeros((), BF16)
        k_stack = jnp.where(same_head_k, jnp.concatenate([kt_all[rows].astype(BF16)] * GLA_HEADS, axis=0), zero)
        v_stack = jnp.where(same_head_v, jnp.concatenate(
            [proj_scr[rows, V0:R0].astype(BF16)] * GLA_HEADS, axis=0), zero)
        s_stack = jnp.where(same_head_v, jnp.concatenate([st.astype(BF16)] * GLA_HEADS, axis=1), zero)
        att = jnp.where(causal, _dot_nt(qt, k_stack), 0.0)
        o_scr[rows, :] = _dot(att.astype(BF16), v_stack) + _dot(qt, s_stack)
        v_chunk = jnp.where(pair_half == c % 2, proj_scr[pair, V0:R0], 0.0).astype(BF16)
        upd = jnp.concatenate(
            [_dot(kl_t[h * GLA_DK:(h + 1) * GLA_DK, pair], v_chunk[:, h * GLA_DV:(h + 1) * GLA_DV])
             for h in range(GLA_HEADS)], axis=0)
        st = st * decay_t[:, c * decay_cols:c * decay_cols + 1] + upd
        if per_chunk_state:
            s_out_ref[c] = st.reshape(GLA_HEADS, GLA_DK, GLA_DV)
        if (c + 1) % OUT_CHUNKS == 0:
            finish_rows(slice((c + 1 - OUT_CHUNKS) * CHUNK, (c + 1) * CHUNK))

    if not per_chunk_state:
        st_scr[...] = st
        s_out_ref[...] = st.reshape(GLA_HEADS, GLA_DK, GLA_DV)


def _const_spec(shape, single_buffer=False):
    nd = len(shape)
    if single_buffer:
        return pl.BlockSpec(shape, lambda *_: (0,) * nd, pipeline_mode=pl.Buffered(1))
    return pl.BlockSpec(shape, lambda *_: (0,) * nd)


def _mixer_weight_specs():
    return [
        _const_spec((None, D_MODEL, N_IN), single_buffer=True),
        _const_spec((D_MODEL, LANES)),
        _const_spec((LANES, GLA_HEADS * GLA_DK)),
        _const_spec((1, GLA_HEADS * GLA_DK)),
        _const_spec((len(POOL_WINDOWS) // 2, 2 * POOL_GROUP, 2 * POOL_GROUP)),
        _const_spec((1, POOL_WIDTH)),
        _const_spec((1, GLA_DV)),
        _const_spec((None, D_MODEL, D_MODEL), single_buffer=True),
        _const_spec((1, D_MODEL)),
        _const_spec((1, D_MODEL)),
        _const_spec((NUM_EXPERTS, D_MODEL)),
        _const_spec((NUM_EXPERTS, LANES)),
    ]


def _mixer_weights(w_in, w_pool, pool_scale, w_gate_up, b_gate, gla_norm_w, w_out, ln1_g, ln1_b,
                   w_router, b_router):
    w_glr = jnp.zeros((D_MODEL, LANES), BF16).at[:, :GATE_RANK].set(w_in[0, :, N_MAIN:].astype(BF16))
    w_gate = jnp.zeros((LANES, GLA_HEADS * GLA_DK), BF16).at[:GATE_RANK].set(w_gate_up[0].astype(BF16))
    wp = w_pool[0].astype(BF16)
    zero = jnp.zeros((POOL_GROUP, POOL_GROUP), BF16)
    w_pool_pairs = jnp.stack([jnp.block([[wp[2 * p], zero], [zero, wp[2 * p + 1]]])
                              for p in range(len(POOL_WINDOWS) // 2)])
    weights = (
        w_in, w_glr, w_gate, b_gate[0][None, :],
        w_pool_pairs, pool_scale[0][None, :], gla_norm_w[0][None, :],
        w_out, ln1_g[0][None, :], ln1_b[0][None, :],
        w_router[0].T.astype(BF16), jnp.broadcast_to(b_router[0][:, None], (NUM_EXPERTS, LANES)),
    )
    assert len(weights) == N_MIXER_WEIGHTS
    return weights


def _mixer_scratch(per_chunk_state):
    ext_rows = (CHUNKS_PER_TILE * (CHUNK + HIST_PAD)) if per_chunk_state else (TILE_TOKENS + HIST_PAD)
    return [
        pltpu.VMEM((TILE_TOKENS, N_MAIN), F32),
        pltpu.VMEM((TILE_TOKENS, GLA_HEADS * GLA_DK), F32),
        pltpu.VMEM((TILE_TOKENS, GLA_HEADS * GLA_DV), F32),
        pltpu.VMEM((ext_rows, POOL_WIDTH), F32),
        pltpu.VMEM((GLA_HEADS * GLA_DK, GLA_DV), F32),
        pltpu.VMEM((SCAN_ROWS, SCAN_ROWS), BF16),
        pltpu.VMEM((D_MODEL, N_MAIN), BF16),
        pltpu.VMEM((D_MODEL, D_MODEL), BF16),
    ]


def _mixer_out_shapes(n, bsz):
    return (
        jax.ShapeDtypeStruct((n, D_MODEL), F32),
        jax.ShapeDtypeStruct((n, HALF), jnp.int32),
        jax.ShapeDtypeStruct((NUM_EXPERTS, n), F32),
        jax.ShapeDtypeStruct((bsz, POOL_HIST, POOL_WIDTH), F32),
        jax.ShapeDtypeStruct((bsz, GLA_HEADS, GLA_DK, GLA_DV), F32),
    )


def _mixer_prompt(x, weights):
    bsz, seq, _ = x.shape
    tiles = seq // TILE_TOKENS
    n_total = bsz * seq
    return pl.pallas_call(
        functools.partial(_mixer_kernel, False, 0),
        grid=(bsz, tiles),
        in_specs=[pl.BlockSpec((None, TILE_TOKENS, D_MODEL), lambda b, t: (b, t, 0))] + _mixer_weight_specs(),
        out_specs=(
            pl.BlockSpec((TILE_TOKENS, D_MODEL), lambda b, t: (b * tiles + t, 0)),
            pl.BlockSpec((TILE_TOKENS, HALF), lambda b, t: (b * tiles + t, 0)),
            pl.BlockSpec((NUM_EXPERTS, TILE_TOKENS), lambda b, t: (0, b * tiles + t)),
            pl.BlockSpec((None, POOL_HIST, POOL_WIDTH), lambda b, t: (b, 0, 0)),
            pl.BlockSpec((None, GLA_HEADS, GLA_DK, GLA_DV), lambda b, t: (b, 0, 0, 0)),
        ),
        out_shape=_mixer_out_shapes(n_total, bsz),
        scratch_shapes=_mixer_scratch(False),
        compiler_params=pltpu.CompilerParams(
            dimension_semantics=("arbitrary", "arbitrary"), vmem_limit_bytes=VMEM_LIMIT),
        name="mixer_prompt",
    )(x, *weights)


def _mixer_sample(x, hist, state, weights):
    bsz = x.shape[0]
    tiles = bsz // CHUNKS_PER_TILE
    return pl.pallas_call(
        functools.partial(_mixer_kernel, True, PAST_LEN),
        grid=(tiles,),
        in_specs=[
            pl.BlockSpec((CHUNKS_PER_TILE, CHUNK, D_MODEL), lambda i: (i, 0, 0)),
            pl.BlockSpec((CHUNKS_PER_TILE, POOL_HIST, POOL_WIDTH), lambda i: (i, 0, 0)),
            pl.BlockSpec((CHUNKS_PER_TILE, GLA_HEADS, GLA_DK, GLA_DV), lambda i: (i, 0, 0, 0)),
        ] + _mixer_weight_specs(),
        out_specs=(
            pl.BlockSpec((TILE_TOKENS, D_MODEL), lambda i: (i, 0)),
            pl.BlockSpec((TILE_TOKENS, HALF), lambda i: (i, 0)),
            pl.BlockSpec((NUM_EXPERTS, TILE_TOKENS), lambda i: (0, i)),
            pl.BlockSpec((CHUNKS_PER_TILE, POOL_HIST, POOL_WIDTH), lambda i: (i, 0, 0)),
            pl.BlockSpec((CHUNKS_PER_TILE, GLA_HEADS, GLA_DK, GLA_DV), lambda i: (i, 0, 0, 0)),
        ),
        out_shape=_mixer_out_shapes(bsz * CHUNK, bsz),
        scratch_shapes=_mixer_scratch(True),
        compiler_params=pltpu.CompilerParams(
            dimension_semantics=("arbitrary",), vmem_limit_bytes=VMEM_LIMIT),
        name="mixer_sample",
    )(x, hist, state, *weights)


def _router_kernel(lt_ref, gates_ref, dest_ref, padend_ref, cnt_scr, base_scr, pstart_scr, before_scr,
                   topk_scr):
    phase = pl.program_id(0)
    i = pl.program_id(1)
    shape = (NUM_EXPERTS, ROUTE_TOKENS)
    row = lax.broadcasted_iota(jnp.int32, shape, 0)

    def tile_counts_of(chosen):
        return jnp.broadcast_to(jnp.sum(chosen, axis=1, keepdims=True), (NUM_EXPERTS, LANES))

    @pl.when(phase == 0)
    def _():
        @pl.when(i == 0)
        def _():
            cnt_scr[...] = jnp.zeros_like(cnt_scr)

        logits = lt_ref[...]
        idxs, vals = [], []
        chosen = jnp.zeros(shape, F32)
        for _ in range(TOP_K):
            m = jnp.max(logits, axis=0, keepdims=True)
            idx = jnp.min(jnp.where(logits == m, row, NUM_EXPERTS), axis=0, keepdims=True)
            hit = row == idx
            idxs.append(idx.astype(F32))
            vals.append(m)
            chosen = chosen + jnp.where(hit, 1.0, 0.0)
            logits = jnp.where(hit, -jnp.inf, logits)
        topk_scr[i] = jnp.concatenate(idxs + vals, axis=0)
        cnt_scr[...] += tile_counts_of(chosen)

    @pl.when(phase == 1)
    def _():
        @pl.when(i == 0)
        def _():
            blocks = jnp.floor((cnt_scr[...] + (MOE_ROWS - 1)) * (1.0 / MOE_ROWS))
            erow = lax.broadcasted_iota(jnp.int32, (NUM_EXPERTS, LANES), 0)
            lane = lax.broadcasted_iota(jnp.int32, (NUM_EXPERTS, LANES), 1)
            cum = blocks
            shift = 1
            while shift < NUM_EXPERTS:
                cum = cum + jnp.where(erow >= shift, pltpu.roll(cum, shift, 0), 0.0)
                shift *= 2
            padend_ref[...] = jnp.where(lane == 1, cnt_scr[...], cum * MOE_ROWS)
            pstart_scr[...] = (cum - blocks) * MOE_ROWS
            base_scr[...] = jnp.zeros_like(base_scr)
            ti = lax.broadcasted_iota(jnp.int32, (ROUTE_TOKENS, ROUTE_TOKENS), 0)
            tj = lax.broadcasted_iota(jnp.int32, (ROUTE_TOKENS, ROUTE_TOKENS), 1)
            before_scr[...] = jnp.where(ti < tj, 1.0, 0.0).astype(BF16)

        topk = topk_scr[i]
        sel = [row == topk[k:k + 1, :].astype(jnp.int32) for k in range(TOP_K)]
        vals = [topk[TOP_K + k:TOP_K + k + 1, :] for k in range(TOP_K)]
        chosen = sum(jnp.where(hit, 1.0, 0.0) for hit in sel)
        earlier = _dot(chosen.astype(BF16), before_scr[...])
        pos = pstart_scr[:, 0:1] + base_scr[:, 0:1] + earlier
        dest = [jnp.sum(jnp.where(hit, pos, 0.0), axis=0, keepdims=True) for hit in sel]
        dest_ref[...] = jnp.concatenate(dest, axis=0).astype(jnp.int32)
        ex = [jnp.exp(v - vals[0]) for v in vals]
        denom = ex[0] + ex[1] + ex[2] + ex[3]
        gates_ref[...] = jnp.concatenate([e / denom for e in ex], axis=0)
        base_scr[...] += tile_counts_of(chosen)


def _router(logits_t):
    n = logits_t.shape[1]
    assert n % ROUTE_TOKENS == 0
    tiles = n // ROUTE_TOKENS
    return pl.pallas_call(
        _router_kernel,
        grid=(2, tiles),
        in_specs=[pl.BlockSpec((NUM_EXPERTS, ROUTE_TOKENS), lambda p, i: (0, i))],
        out_specs=(
            pl.BlockSpec((TOP_K, ROUTE_TOKENS), lambda p, i: (0, i * p)),
            pl.BlockSpec((TOP_K, ROUTE_TOKENS), lambda p, i: (0, i * p)),
            pl.BlockSpec((NUM_EXPERTS, LANES), lambda p, i: (0, 0)),
        ),
        out_shape=(
            jax.ShapeDtypeStruct((TOP_K, n), F32),
            jax.ShapeDtypeStruct((TOP_K, n), jnp.int32),
            jax.ShapeDtypeStruct((NUM_EXPERTS, LANES), F32),
        ),
        scratch_shapes=[pltpu.VMEM((NUM_EXPERTS, LANES), F32)] * 3
        + [pltpu.VMEM((ROUTE_TOKENS, ROUTE_TOKENS), BF16),
           pltpu.VMEM((tiles, 2 * TOP_K, ROUTE_TOKENS), F32)],
        compiler_params=pltpu.CompilerParams(
            dimension_semantics=("arbitrary", "arbitrary"), vmem_limit_bytes=VMEM_LIMIT),
        name="router",
    )(logits_t)


SC_CORES = 2
SC_SUBCORES = 16
SC_WORKERS = SC_CORES * SC_SUBCORES
DISPATCH_ROWS = 64


def _dispatch(h_sources, dest_chunks, m_pad):
    shares, first_chunk = [], 0
    for src in h_sources:
        n_chunks = src.shape[0] // DISPATCH_ROWS
        assert src.shape[0] % DISPATCH_ROWS == 0 and n_chunks % SC_WORKERS == 0
        shares.append((first_chunk, n_chunks // SC_WORKERS))
        first_chunk += n_chunks
    plan = [(s, first, per, j) for s, (first, per) in enumerate(shares) for j in range(per)]
    per_worker = len(plan)
    mesh = plsc.VectorSubcoreMesh(core_axis_name="c", subcore_axis_name="s")

    @functools.partial(
        pl.kernel, mesh=mesh,
        out_type=jax.ShapeDtypeStruct((m_pad, HALF), jnp.int32),
        scratch_types=[
            pltpu.VMEM((2, TOP_K, DISPATCH_ROWS), jnp.int32),
            pltpu.VMEM((2, DISPATCH_ROWS, HALF), jnp.int32),
            pltpu.SemaphoreType.DMA((2,)),
            pltpu.SemaphoreType.DMA((2,)),
        ],
        compiler_params=pltpu.CompilerParams(use_tc_tiling_on_sc=True),
        name="dispatch",
    )
    def dispatch_kernel(*refs):
        h_hbms = refs[:len(h_sources)]
        dest_hbm, out_hbm, idx_v, rows_v, load_sems, scatter_sems = refs[len(h_sources):]
        wid = lax.axis_index("s") * SC_CORES + lax.axis_index("c")

        def loads(j):
            src, first, per, k = plan[j]
            local = wid * per + k
            slot = j % 2
            return (
                pltpu.make_async_copy(dest_hbm.at[first + local], idx_v.at[slot], load_sems.at[slot]),
                pltpu.make_async_copy(h_hbms[src].at[pl.ds(local * DISPATCH_ROWS, DISPATCH_ROWS)],
                                      rows_v.at[slot], load_sems.at[slot]),
            )

        def scatters(j):
            slot = j % 2
            return [pltpu.make_async_copy(rows_v.at[slot], out_hbm.at[idx_v.at[slot, k]],
                                          scatter_sems.at[slot]) for k in range(TOP_K)]

        for cp in loads(0):
            cp.start()
        for j in range(per_worker):
            for cp in loads(j):
                cp.wait()
            if j >= 1:
                for cp in scatters(j - 1):
                    cp.wait()
            if j + 1 < per_worker:
                for cp in loads(j + 1):
                    cp.start()
            for cp in scatters(j):
                cp.start()
        for cp in scatters(per_worker - 1):
            cp.wait()

    return dispatch_kernel(*h_sources, dest_chunks)


def _gather_expert_rows(y_sorted, dest_chunks, row_offset, n):
    n_chunks = n // DISPATCH_ROWS
    assert n_chunks % SC_WORKERS == 0 and row_offset % DISPATCH_ROWS == 0
    per_worker = n_chunks // SC_WORKERS
    chunk0 = row_offset // DISPATCH_ROWS
    mesh = plsc.VectorSubcoreMesh(core_axis_name="c", subcore_axis_name="s")

    @functools.partial(
        pl.kernel, mesh=mesh,
        out_type=jax.ShapeDtypeStruct((TOP_K, n, HALF), jnp.int32),
        scratch_types=[
            pltpu.VMEM((TOP_K, DISPATCH_ROWS), jnp.int32),
            pltpu.VMEM((2, DISPATCH_ROWS, HALF), jnp.int32),
            pltpu.SemaphoreType.DMA((2,)),
        ],
        compiler_params=pltpu.CompilerParams(use_tc_tiling_on_sc=True),
        name="gather_expert_rows",
    )
    def gather_kernel(y_hbm, dest_hbm, out_hbm, idx_v, rows_v, sems):
        wid = lax.axis_index("s") * SC_CORES + lax.axis_index("c")

        def gather(k):
            return pltpu.make_async_copy(y_hbm.at[idx_v.at[k]], rows_v.at[k % 2], sems.at[k % 2])

        @pl.loop(0, per_worker)
        def _(j):
            local = wid * per_worker + j
            pltpu.sync_copy(dest_hbm.at[chunk0 + local], idx_v)
            gather(0).start()
            for k in range(TOP_K):
                if k + 1 < TOP_K:
                    gather(k + 1).start()
                gather(k).wait()
                pltpu.sync_copy(rows_v.at[k % 2],
                                out_hbm.at[k, pl.ds(local * DISPATCH_ROWS, DISPATCH_ROWS)])

    return gather_kernel(y_sorted, dest_chunks)


def _moe_kernel(be_ref, wsel_ref, units_ref, nused_ref, x_ref, wgu_ref, bgu_ref, wd_ref, bd_ref, y_ref,
                wgu_bf, wd_bf):
    del wsel_ref
    i = pl.program_id(0)

    def ffn(rows):
        gu = _dot(_unpack_bf16_pairs(x_ref[rows, :]), wgu_bf[...]) + bgu_ref[...]
        gate = jnp.minimum(gu[:, :EXPERT_FF], SWIGLU_LIMIT)
        up = jnp.clip(gu[:, EXPERT_FF:], -SWIGLU_LIMIT, SWIGLU_LIMIT)
        hmid = gate * (1.0 / (1.0 + jnp.exp(-SWIGLU_ALPHA * gate))) * (up + 1.0)
        y = _dot(hmid.astype(BF16), wd_bf[...]) + bd_ref[...]
        y_ref[rows, :] = _pack_bf16_pairs(y.astype(BF16))

    @pl.when(i < nused_ref[0])
    def _():
        @pl.when((i == 0) | (be_ref[i] != be_ref[jnp.maximum(i - 1, 0)]))
        def _():
            wgu_bf[...] = wgu_ref[...].astype(BF16)
            wd_bf[...] = wd_ref[...].astype(BF16)

        for units in range(1, MOE_ROWS // MOE_TAIL_ROWS + 1):
            @pl.when(units_ref[i] == units)
            def _(units=units):
                ffn(slice(0, units * MOE_TAIL_ROWS))


def _moe_experts(block_expert, weight_expert, block_units, n_used, x_sorted, w_gu, b_gu, w_down, b_down):
    m_pad = x_sorted.shape[0]
    n_blocks = m_pad // MOE_ROWS

    def blk(i, be, ws, hb, nu):
        return jnp.minimum(i, nu[0] - 1)

    def expert(i, be, ws, hb, nu):
        return be[blk(i, be, ws, hb, nu)]

    def held(i, be, ws, hb, nu):
        return ws[blk(i, be, ws, hb, nu)]

    grid_spec = pltpu.PrefetchScalarGridSpec(
        num_scalar_prefetch=4,
        grid=(n_blocks,),
        in_specs=[
            pl.BlockSpec((MOE_ROWS, HALF), lambda *a: (blk(*a), 0)),
            pl.BlockSpec((None, D_MODEL, 2 * EXPERT_FF), lambda *a: (held(*a), 0, 0)),
            pl.BlockSpec((None, 1, 2 * EXPERT_FF), lambda *a: (expert(*a), 0, 0)),
            pl.BlockSpec((None, EXPERT_FF, D_MODEL), lambda *a: (held(*a), 0, 0)),
            pl.BlockSpec((None, 1, D_MODEL), lambda *a: (expert(*a), 0, 0)),
        ],
        out_specs=pl.BlockSpec((MOE_ROWS, HALF), lambda *a: (blk(*a), 0)),
        scratch_shapes=[
            pltpu.VMEM((D_MODEL, 2 * EXPERT_FF), BF16),
            pltpu.VMEM((EXPERT_FF, D_MODEL), BF16),
        ],
    )
    return pl.pallas_call(
        _moe_kernel,
        grid_spec=grid_spec,
        out_shape=jax.ShapeDtypeStruct((m_pad, HALF), jnp.int32),
        compiler_params=pltpu.CompilerParams(
            dimension_semantics=("arbitrary",), vmem_limit_bytes=VMEM_LIMIT),
        name="moe_experts",
    )(block_expert, weight_expert, block_units, n_used, x_sorted, w_gu, b_gu, w_down, b_down)


def _combine_kernel(yk_ref, gates_ref, h_ref, g_ref, b_ref, out_ref):
    pad = jnp.zeros((LANES - TOP_K, COMBINE_TOKENS), F32)
    gates = jnp.transpose(jnp.concatenate([gates_ref[...], pad], axis=0))
    lo = hi = None
    for k in range(TOP_K):
        u = lax.bitcast_convert_type(yk_ref[k], jnp.uint32)
        gk = gates[:, k:k + 1]
        lo_k = lax.bitcast_convert_type(u << 16, F32) * gk
        hi_k = lax.bitcast_convert_type(u & jnp.uint32(HI_MASK), F32) * gk
        lo = lo_k if lo is None else lo + lo_k
        hi = hi_k if hi is None else hi + hi_k
    acc = ALPHA * h_ref[...] + jnp.concatenate([lo, hi], axis=1)
    out_ref[...] = _layer_norm(acc, g_ref[...], b_ref[...])


def _combine_kernel_aliased(yk_ref, gates_ref, h_ref, g_ref, b_ref, prev_ref, out_ref):
    del prev_ref
    _combine_kernel(yk_ref, gates_ref, h_ref, g_ref, b_ref, out_ref)


def _combine(yk, gates, token_offset, h_src, h_offset, ln_g, ln_b, out_prev):
    n_seg = yk.shape[1]
    out_rows = h_src.shape[0]
    tile0 = token_offset // COMBINE_TOKENS
    out_tile0 = h_offset // COMBINE_TOKENS
    in_specs = [
        pl.BlockSpec((TOP_K, COMBINE_TOKENS, HALF), lambda i: (0, i, 0)),
        pl.BlockSpec((TOP_K, COMBINE_TOKENS), lambda i: (0, tile0 + i)),
        pl.BlockSpec((COMBINE_TOKENS, D_MODEL), lambda i: (out_tile0 + i, 0)),
        _const_spec((1, D_MODEL)),
        _const_spec((1, D_MODEL)),
    ]
    args = [yk, gates, h_src, ln_g, ln_b]
    aliases = {}
    kern = _combine_kernel
    if out_prev is not None:
        in_specs.append(pl.BlockSpec(memory_space=pl.ANY))
        aliases = {len(args): 0}
        args.append(out_prev)
        kern = _combine_kernel_aliased
    return pl.pallas_call(
        kern,
        grid=(n_seg // COMBINE_TOKENS,),
        in_specs=in_specs,
        out_specs=pl.BlockSpec((COMBINE_TOKENS, D_MODEL), lambda i: (out_tile0 + i, 0)),
        out_shape=jax.ShapeDtypeStruct((out_rows, D_MODEL), F32),
        input_output_aliases=aliases,
        compiler_params=pltpu.CompilerParams(
            dimension_semantics=("arbitrary",), vmem_limit_bytes=VMEM_LIMIT),
        name="moe_combine",
    )(*args)


def kernel(x_prompt, x_sample, state_pool, state_gla, w_in, w_pool, pool_scale, w_gate_up, b_gate,
           gla_norm_w, w_out, ln1_g, ln1_b, w_router, b_router, w_gu, b_gu, w_down, b_down,
           ln2_g, ln2_b):
    assert w_in.shape[0] == 1, "single-layer kernel"
    bp, seq, _ = x_prompt.shape
    bs, dec_seq, _ = x_sample.shape
    assert dec_seq == CHUNK and seq % TILE_TOKENS == 0 and bs % CHUNKS_PER_TILE == 0
    n_prompt = bp * seq
    n_sample = bs * dec_seq
    n_total = n_prompt + n_sample
    nk = n_total * TOP_K
    n_blocks = -(-nk // MOE_ROWS) + NUM_EXPERTS
    m_pad = n_blocks * MOE_ROWS

    weights = _mixer_weights(w_in, w_pool, pool_scale, w_gate_up, b_gate, gla_norm_w, w_out,
                             ln1_g, ln1_b, w_router, b_router)

    h_p, hb_p, logits_p, hist_p, s_p = _mixer_prompt(x_prompt, weights)
    h_s, hb_s, logits_s, hist_s, s_s = _mixer_sample(x_sample, state_pool[0], state_gla[0], weights)

    gates_t, dest_t, layout = _router(jnp.concatenate([logits_p, logits_s], axis=1))
    pad_end = layout[:, 0].astype(jnp.int32)
    counts = layout[:, 1].astype(jnp.int32)
    block_start = jnp.arange(n_blocks, dtype=jnp.int32) * MOE_ROWS
    block_expert = jnp.minimum(jnp.sum((block_start[:, None] >= pad_end[None, :]).astype(jnp.int32), axis=1),
                               NUM_EXPERTS - 1)
    n_used = (pad_end[-1:] // MOE_ROWS).astype(jnp.int32)
    is_first = jnp.concatenate([jnp.ones((1,), bool), block_expert[1:] != block_expert[:-1]])
    blocks = jnp.arange(n_blocks, dtype=jnp.int32)
    later_other = ((blocks[None, :] > blocks[:, None]) & (blocks[None, :] < n_used[0])
                   & (block_expert[None, :] != block_expert[:, None]))
    next_expert = jnp.min(jnp.where(later_other, block_expert[None, :], NUM_EXPERTS), axis=1)
    next_expert = jnp.where(next_expert == NUM_EXPERTS, block_expert, next_expert)
    weight_expert = jnp.where(is_first, block_expert, next_expert).astype(jnp.int32)
    of_expert = block_expert[:, None] == jnp.arange(NUM_EXPERTS, dtype=jnp.int32)[None, :]
    seg_end = jnp.sum(jnp.where(of_expert, (pad_end - (-counts % MOE_ROWS))[None, :], 0), axis=1)
    block_units = jnp.clip((seg_end - block_start + MOE_TAIL_ROWS - 1) // MOE_TAIL_ROWS,
                          1, MOE_ROWS // MOE_TAIL_ROWS).astype(jnp.int32)

    dest_chunks = dest_t.reshape(TOP_K, n_total // DISPATCH_ROWS, DISPATCH_ROWS).transpose(1, 0, 2)
    x_sorted = _dispatch((hb_p, hb_s), dest_chunks, m_pad)
    y_sorted = _moe_experts(block_expert, weight_expert, block_units, n_used, x_sorted, w_gu[0], b_gu[0][:, None, :],
                            w_down[0], b_down[0][:, None, :])
    gates = gates_t
    ln_g, ln_b = ln2_g[0][None, :], ln2_b[0][None, :]
    yk = _gather_expert_rows(y_sorted, dest_chunks, n_prompt, n_sample)
    y_sample = _combine(yk, gates, n_prompt, h_s, 0, ln_g, ln_b, None)
    unit = SC_WORKERS * DISPATCH_ROWS
    assert n_prompt % unit == 0
    sizes, left, size = [], n_prompt // unit, 1
    while left > 0:
        size = min(left, size)
        sizes.append(size * unit)
        left -= size
        size = max(size + 1, (3 * size) // 2)
    y_prompt, start = None, 0
    for seg in sizes:
        yk = _gather_expert_rows(y_sorted, dest_chunks, start, seg)
        y_prompt = _combine(yk, gates, start, h_p, start, ln_g, ln_b, y_prompt)
        start += seg
    y_prompt = y_prompt.reshape(bp, seq, D_MODEL)
    y_sample = y_sample.reshape(bs, dec_seq, D_MODEL)
    return (y_prompt, y_sample, hist_p[None], s_p[None], hist_s[None], s_s[None])
```

```python
import functools

import jax
import jax.numpy as jnp
from jax import lax
from jax.experimental import pallas as pl
from jax.experimental.pallas import tpu as pltpu
from jax.experimental.pallas import tpu_sc as plsc

F32 = jnp.float32
BF16 = jnp.bfloat16

D_MODEL = 1024
CHUNK = 64
PAST_LEN = 1024
POOL_WIDTH = 512
POOL_WINDOWS = (2, 4, 8, 16)
POOL_GROUP = 128
POOL_HIST = 15
GLA_HEADS = 4
GLA_DK = 64
GLA_DV = 128
GATE_RANK = 16
GATE_NORMALIZER = 16.0
NUM_EXPERTS = 32
TOP_K = 4
EXPERT_FF = 1024
SWIGLU_LIMIT = 7.0
SWIGLU_ALPHA = 1.702
LN_EPS = 1e-5
RMS_EPS = 1e-6
ALPHA = 2.0 ** 0.25

Q0 = POOL_WIDTH
K0 = Q0 + GLA_HEADS * GLA_DK
V0 = K0 + GLA_HEADS * GLA_DK
R0 = V0 + GLA_HEADS * GLA_DV
N_MAIN = R0 + GLA_HEADS * GLA_DV
N_IN = N_MAIN + GATE_RANK

LANES = 128
TILE_TOKENS = 512
CHUNKS_PER_TILE = TILE_TOKENS // CHUNK
HIST_PAD = 16
OUT_CHUNKS = 4
SCAN_ROWS = 256
MOE_ROWS = 1024
MOE_TAIL_ROWS = 256
ROUTE_TOKENS = 2048
COMBINE_TOKENS = 1024
VMEM_LIMIT = 56 * 1024 * 1024


def _dot(a, b):
    return jnp.dot(a, b, preferred_element_type=F32)


def _dot_nt(a, b):
    return lax.dot_general(a, b, (((1,), (1,)), ((), ())), preferred_element_type=F32)


def _dot_tn(a, b):
    return lax.dot_general(a, b, (((0,), (0,)), ((), ())), preferred_element_type=F32)


HALF = D_MODEL // 2
HI_MASK = 0xFFFF0000


def _pack_bf16_pairs(xb):
    lo = lax.bitcast_convert_type(xb[:, :HALF].astype(F32), jnp.uint32) >> 16
    hi = lax.bitcast_convert_type(xb[:, HALF:].astype(F32), jnp.uint32) & jnp.uint32(HI_MASK)
    return lax.bitcast_convert_type(hi | lo, jnp.int32)


def _unpack_bf16_pairs(p):
    u = lax.bitcast_convert_type(p, jnp.uint32)
    lo = lax.bitcast_convert_type(u << 16, F32)
    hi = lax.bitcast_convert_type(u & jnp.uint32(HI_MASK), F32)
    return jnp.concatenate([lo, hi], axis=1).astype(BF16)


def _layer_norm(v, g, b):
    mu = jnp.mean(v, axis=-1, keepdims=True)
    c = v - mu
    var = jnp.mean(c * c, axis=-1, keepdims=True)
    return c * lax.rsqrt(var + LN_EPS) * g + b


N_MIXER_WEIGHTS = 12


def _mixer_kernel(per_chunk_state, pos0, *refs):
    if per_chunk_state:
        (x_ref, hist_in_ref, s_in_ref, *rest) = refs
    else:
        (x_ref, *rest) = refs
        hist_in_ref = s_in_ref = None
    (w_main_ref, w_glr_ref, w_gate_ref, b_gate_ref, w_pool_ref, pscale_ref, gnorm_ref,
     w_out_ref, ln1g_ref, ln1b_ref, w_router_ref, b_router_ref, *rest) = rest
    (h_ref, hb_ref, logits_ref, hist_out_ref, s_out_ref,
     proj_scr, b_scr, o_scr, ext_scr, st_scr, tri_scr, w_main_bf, w_out_bf) = rest

    if per_chunk_state:
        t = None
        first_step = pl.program_id(0) == 0
    else:
        t = pl.program_id(1)
        first_step = (pl.program_id(0) == 0) & (t == 0)
    x = x_ref[...].reshape(TILE_TOKENS, D_MODEL)
    xb = x.astype(BF16)

    @pl.when(first_step)
    def _():
        ti = lax.broadcasted_iota(jnp.int32, (SCAN_ROWS, SCAN_ROWS), 0)
        tj = lax.broadcasted_iota(jnp.int32, (SCAN_ROWS, SCAN_ROWS), 1)
        same_chunk = (ti // CHUNK) == (tj // CHUNK)
        tri_scr[...] = jnp.where(same_chunk & (ti >= tj), 1.0, 0.0).astype(BF16)
        w_main_bf[...] = w_main_ref[:, 0:N_MAIN].astype(BF16)
        w_out_bf[...] = w_out_ref[...].astype(BF16)
        if not per_chunk_state:
            st_scr[...] = jnp.zeros_like(st_scr)
            ext_scr[0:HIST_PAD, :] = jnp.zeros((HIST_PAD, POOL_WIDTH), F32)

    glr = _dot(xb, w_glr_ref[...])
    proj_scr[:, 0:V0] = _dot(xb, w_main_bf[:, 0:V0])
    gk = _dot(glr.astype(BF16), w_gate_ref[...]) + b_gate_ref[...]
    log_sig = jnp.minimum(gk, 0.0) - jnp.log1p(jnp.exp(-jnp.abs(gk)))
    g = log_sig / GATE_NORMALIZER
    g_hi = g.astype(BF16)
    g_lo = (g - g_hi.astype(F32)).astype(BF16)
    proj_scr[:, V0:N_MAIN] = _dot(xb, w_main_bf[:, V0:N_MAIN])
    for s in range(TILE_TOKENS // SCAN_ROWS):
        rs = slice(s * SCAN_ROWS, (s + 1) * SCAN_ROWS)
        b_scr[rs, :] = _dot(tri_scr[...], g_hi[rs]) + _dot(tri_scr[...], g_lo[rs])

    if per_chunk_state:
        seg_len, seg_stride, n_seg = CHUNK, CHUNK + HIST_PAD, CHUNKS_PER_TILE
        for c in range(n_seg):
            base = c * seg_stride
            ext_scr[base:base + HIST_PAD, :] = jnp.zeros((HIST_PAD, POOL_WIDTH), F32)
            ext_scr[base + 1:base + HIST_PAD, :] = hist_in_ref[c]
            ext_scr[base + HIST_PAD:base + seg_stride, :] = proj_scr[c * CHUNK:(c + 1) * CHUNK, 0:POOL_WIDTH]
        row_pos = pos0 + lax.broadcasted_iota(jnp.int32, (seg_len, POOL_GROUP), 0)
    else:
        seg_len, seg_stride, n_seg = TILE_TOKENS, TILE_TOKENS + HIST_PAD, 1
        ext_scr[0:HIST_PAD, :] = jnp.where(t == 0, 0.0, ext_scr[0:HIST_PAD, :])
        ext_scr[HIST_PAD:seg_stride, :] = proj_scr[:, 0:POOL_WIDTH]
        row_pos = pos0 + t * TILE_TOKENS + lax.broadcasted_iota(jnp.int32, (seg_len, POOL_GROUP), 0)

    pooled_groups = []
    for gi, w in enumerate(POOL_WINDOWS):
        gs = slice(gi * POOL_GROUP, (gi + 1) * POOL_GROUP)
        cnt = jnp.minimum(row_pos + 1, w).astype(F32)
        ext = ext_scr[:, gs]
        win = ext
        shift = 1
        while shift < w:
            win = win + pltpu.roll(win, shift, 0)
            shift *= 2
        segs = []
        for s in range(n_seg):
            base = s * seg_stride + HIST_PAD
            segs.append(win[base:base + seg_len] / cnt - ext[base:base + seg_len])
        pooled = segs[0] if n_seg == 1 else jnp.concatenate(segs, axis=0)
        pooled_groups.append(pooled.astype(BF16))
    pool_cols = []
    for p in range(len(POOL_WINDOWS) // 2):
        both = jnp.concatenate(pooled_groups[2 * p:2 * p + 2], axis=1)
        pool_cols.append(_dot(both, w_pool_ref[p]))
    pool_out = jnp.concatenate(pool_cols, axis=1) * pscale_ref[...]

    if per_chunk_state:
        for c in range(n_seg):
            end = (c + 1) * seg_stride
            hist_out_ref[c] = ext_scr[end - POOL_HIST:end, :]
    else:
        hist_out_ref[...] = ext_scr[seg_stride - POOL_HIST:seg_stride, :]
        ext_scr[0:HIST_PAD, :] = ext_scr[TILE_TOKENS:seg_stride, :]

    hk = GLA_HEADS * GLA_DK
    hv = GLA_HEADS * GLA_DV
    pair_rows = 2 * CHUNK
    decay_cols = LANES // CHUNKS_PER_TILE

    def head_of(shape, dim, width):
        return lax.broadcasted_iota(jnp.int32, shape, dim) // width

    same_head_k = head_of((hk, hk), 0, CHUNK) == head_of((hk, hk), 1, GLA_DK)
    same_head_v = head_of((hk, hv), 0, CHUNK) == head_of((hk, hv), 1, GLA_DV)
    pair_half = head_of((pair_rows, hv), 0, CHUNK)
    causal = (lax.broadcasted_iota(jnp.int32, (CHUNK, hk), 0)
              >= lax.broadcasted_iota(jnp.int32, (CHUNK, hk), 1) % CHUNK)

    b_all = b_scr[...]
    b_last = [b_scr[(c + 1) * CHUNK - 1:(c + 1) * CHUNK, :] for c in range(CHUNKS_PER_TILE)]
    b_last_rows = jnp.concatenate([jnp.broadcast_to(bl, (CHUNK, hk)) for bl in b_last], axis=0)
    k_all = proj_scr[:, K0:V0]
    qt_all = (proj_scr[:, Q0:K0] * (GLA_DK ** -0.5) * jnp.exp(b_all)).astype(BF16)
    kt_all = k_all * jnp.exp(-b_all)
    kl_t = jnp.transpose(k_all * jnp.exp(b_last_rows - b_all)).astype(BF16)
    decay_t = jnp.transpose(jnp.exp(jnp.concatenate(
        [jnp.broadcast_to(bl, (decay_cols, hk)) for bl in b_last], axis=0)))

    def finish_rows(rs):
        r = proj_scr[rs, R0:N_MAIN]
        silu_r = r * (1.0 / (1.0 + jnp.exp(-r)))
        gated = []
        for h in range(GLA_HEADS):
            vs = slice(h * GLA_DV, (h + 1) * GLA_DV)
            oh = o_scr[rs, vs]
            ms = jnp.mean(oh * oh, axis=-1, keepdims=True)
            gated.append(oh * lax.rsqrt(ms + RMS_EPS) * gnorm_ref[...] * silu_r[:, vs])
        mix_in = jnp.concatenate([pool_out[rs]] + gated, axis=1).astype(BF16)
        resid = ALPHA * x[rs] + _dot(mix_in, w_out_bf[...])
        h_val = _layer_norm(resid, ln1g_ref[...], ln1b_ref[...])
        h_ref[rs, :] = h_val
        hb = h_val.astype(BF16)
        hb_ref[rs, :] = _pack_bf16_pairs(hb)
        logits_ref[:, rs] = _dot_nt(w_router_ref[...], hb) + b_router_ref[:, 0:1]

    st = None if per_chunk_state else jnp.where(t == 0, 0.0, st_scr[...])
    for c in range(CHUNKS_PER_TILE):
        rows = slice(c * CHUNK, (c + 1) * CHUNK)
        pair = slice((c // 2) * pair_rows, (c // 2 + 1) * pair_rows)
        if per_chunk_state:
            st = s_in_ref[c].reshape(hk, GLA_DV)
        qt = qt_all[rows]
        zero = jnp.zeros((), BF16)
        k_stack = jnp.where(same_head_k, jnp.concatenate([kt_all[rows].astype(BF16)] * GLA_HEADS, axis=0), zero)
        v_stack = jnp.where(same_head_v, jnp.concatenate(
            [proj_scr[rows, V0:R0].astype(BF16)] * GLA_HEADS, axis=0), zero)
        s_stack = jnp.where(same_head_v, jnp.concatenate([st.astype(BF16)] * GLA_HEADS, axis=1), zero)
        att = jnp.where(causal, _dot_nt(qt, k_stack), 0.0)
        o_scr[rows, :] = _dot(att.astype(BF16), v_stack) + _dot(qt, s_stack)
        v_chunk = jnp.where(pair_half == c % 2, proj_scr[pair, V0:R0], 0.0).astype(BF16)
        upd = jnp.concatenate(
            [_dot(kl_t[h * GLA_DK:(h + 1) * GLA_DK, pair], v_chunk[:, h * GLA_DV:(h + 1) * GLA_DV])
             for h in range(GLA_HEADS)], axis=0)
        st = st * decay_t[:, c * decay_cols:c * decay_cols + 1] + upd
        if per_chunk_state:
            s_out_ref[c] = st.reshape(GLA_HEADS, GLA_DK, GLA_DV)
        if (c + 1) % OUT_CHUNKS == 0:
            finish_rows(slice((c + 1 - OUT_CHUNKS) * CHUNK, (c + 1) * CHUNK))

    if not per_chunk_state:
        st_scr[...] = st
        s_out_ref[...] = st.reshape(GLA_HEADS, GLA_DK, GLA_DV)


def _const_spec(shape, single_buffer=False):
    nd = len(shape)
    if single_buffer:
        return pl.BlockSpec(shape, lambda *_: (0,) * nd, pipeline_mode=pl.Buffered(1))
    return pl.BlockSpec(shape, lambda *_: (0,) * nd)


def _mixer_weight_specs():
    return [
        _const_spec((None, D_MODEL, N_IN), single_buffer=True),
        _const_spec((D_MODEL, LANES)),
        _const_spec((LANES, GLA_HEADS * GLA_DK)),
        _const_spec((1, GLA_HEADS * GLA_DK)),
        _const_spec((len(POOL_WINDOWS) // 2, 2 * POOL_GROUP, 2 * POOL_GROUP)),
        _const_spec((1, POOL_WIDTH)),
        _const_spec((1, GLA_DV)),
        _const_spec((None, D_MODEL, D_MODEL), single_buffer=True),
        _const_spec((1, D_MODEL)),
        _const_spec((1, D_MODEL)),
        _const_spec((NUM_EXPERTS, D_MODEL)),
        _const_spec((NUM_EXPERTS, LANES)),
    ]


def _mixer_weights(w_in, w_pool, pool_scale, w_gate_up, b_gate, gla_norm_w, w_out, ln1_g, ln1_b,
                   w_router, b_router):
    w_glr = jnp.zeros((D_MODEL, LANES), BF16).at[:, :GATE_RANK].set(w_in[0, :, N_MAIN:].astype(BF16))
    w_gate = jnp.zeros((LANES, GLA_HEADS * GLA_DK), BF16).at[:GATE_RANK].set(w_gate_up[0].astype(BF16))
    wp = w_pool[0].astype(BF16)
    zero = jnp.zeros((POOL_GROUP, POOL_GROUP), BF16)
    w_pool_pairs = jnp.stack([jnp.block([[wp[2 * p], zero], [zero, wp[2 * p + 1]]])
                              for p in range(len(POOL_WINDOWS) // 2)])
    weights = (
        w_in, w_glr, w_gate, b_gate[0][None, :],
        w_pool_pairs, pool_scale[0][None, :], gla_norm_w[0][None, :],
        w_out, ln1_g[0][None, :], ln1_b[0][None, :],
        w_router[0].T.astype(BF16), jnp.broadcast_to(b_router[0][:, None], (NUM_EXPERTS, LANES)),
    )
    assert len(weights) == N_MIXER_WEIGHTS
    return weights


def _mixer_scratch(per_chunk_state):
    ext_rows = (CHUNKS_PER_TILE * (CHUNK + HIST_PAD)) if per_chunk_state else (TILE_TOKENS + HIST_PAD)
    return [
        pltpu.VMEM((TILE_TOKENS, N_MAIN), F32),
        pltpu.VMEM((TILE_TOKENS, GLA_HEADS * GLA_DK), F32),
        pltpu.VMEM((TILE_TOKENS, GLA_HEADS * GLA_DV), F32),
        pltpu.VMEM((ext_rows, POOL_WIDTH), F32),
        pltpu.VMEM((GLA_HEADS * GLA_DK, GLA_DV), F32),
        pltpu.VMEM((SCAN_ROWS, SCAN_ROWS), BF16),
        pltpu.VMEM((D_MODEL, N_MAIN), BF16),
        pltpu.VMEM((D_MODEL, D_MODEL), BF16),
    ]


def _mixer_out_shapes(n, bsz):
    return (
        jax.ShapeDtypeStruct((n, D_MODEL), F32),
        jax.ShapeDtypeStruct((n, HALF), jnp.int32),
        jax.ShapeDtypeStruct((NUM_EXPERTS, n), F32),
        jax.ShapeDtypeStruct((bsz, POOL_HIST, POOL_WIDTH), F32),
        jax.ShapeDtypeStruct((bsz, GLA_HEADS, GLA_DK, GLA_DV), F32),
    )


def _mixer_prompt(x, weights):
    bsz, seq, _ = x.shape
    tiles = seq // TILE_TOKENS
    n_total = bsz * seq
    return pl.pallas_call(
        functools.partial(_mixer_kernel, False, 0),
        grid=(bsz, tiles),
        in_specs=[pl.BlockSpec((None, TILE_TOKENS, D_MODEL), lambda b, t: (b, t, 0))] + _mixer_weight_specs(),
        out_specs=(
            pl.BlockSpec((TILE_TOKENS, D_MODEL), lambda b, t: (b * tiles + t, 0)),
            pl.BlockSpec((TILE_TOKENS, HALF), lambda b, t: (b * tiles + t, 0)),
            pl.BlockSpec((NUM_EXPERTS, TILE_TOKENS), lambda b, t: (0, b * tiles + t)),
            pl.BlockSpec((None, POOL_HIST, POOL_WIDTH), lambda b, t: (b, 0, 0)),
            pl.BlockSpec((None, GLA_HEADS, GLA_DK, GLA_DV), lambda b, t: (b, 0, 0, 0)),
        ),
        out_shape=_mixer_out_shapes(n_total, bsz),
        scratch_shapes=_mixer_scratch(False),
        compiler_params=pltpu.CompilerParams(
            dimension_semantics=("arbitrary", "arbitrary"), vmem_limit_bytes=VMEM_LIMIT),
        name="mixer_prompt",
    )(x, *weights)


def _mixer_sample(x, hist, state, weights):
    bsz = x.shape[0]
    tiles = bsz // CHUNKS_PER_TILE
    return pl.pallas_call(
        functools.partial(_mixer_kernel, True, PAST_LEN),
        grid=(tiles,),
        in_specs=[
            pl.BlockSpec((CHUNKS_PER_TILE, CHUNK, D_MODEL), lambda i: (i, 0, 0)),
            pl.BlockSpec((CHUNKS_PER_TILE, POOL_HIST, POOL_WIDTH), lambda i: (i, 0, 0)),
            pl.BlockSpec((CHUNKS_PER_TILE, GLA_HEADS, GLA_DK, GLA_DV), lambda i: (i, 0, 0, 0)),
        ] + _mixer_weight_specs(),
        out_specs=(
            pl.BlockSpec((TILE_TOKENS, D_MODEL), lambda i: (i, 0)),
            pl.BlockSpec((TILE_TOKENS, HALF), lambda i: (i, 0)),
            pl.BlockSpec((NUM_EXPERTS, TILE_TOKENS), lambda i: (0, i)),
            pl.BlockSpec((CHUNKS_PER_TILE, POOL_HIST, POOL_WIDTH), lambda i: (i, 0, 0)),
            pl.BlockSpec((CHUNKS_PER_TILE, GLA_HEADS, GLA_DK, GLA_DV), lambda i: (i, 0, 0, 0)),
        ),
        out_shape=_mixer_out_shapes(bsz * CHUNK, bsz),
        scratch_shapes=_mixer_scratch(True),
        compiler_params=pltpu.CompilerParams(
            dimension_semantics=("arbitrary",), vmem_limit_bytes=VMEM_LIMIT),
        name="mixer_sample",
    )(x, hist, state, *weights)


def _router_kernel(lt_ref, gates_ref, dest_ref, padend_ref, cnt_scr, base_scr, pstart_scr, before_scr,
                   topk_scr):
    phase = pl.program_id(0)
    i = pl.program_id(1)
    shape = (NUM_EXPERTS, ROUTE_TOKENS)
    row = lax.broadcasted_iota(jnp.int32, shape, 0)

    def tile_counts_of(chosen):
        return jnp.broadcast_to(jnp.sum(chosen, axis=1, keepdims=True), (NUM_EXPERTS, LANES))

    @pl.when(phase == 0)
    def _():
        @pl.when(i == 0)
        def _():
            cnt_scr[...] = jnp.zeros_like(cnt_scr)

        logits = lt_ref[...]
        idxs, vals = [], []
        chosen = jnp.zeros(shape, F32)
        for _ in range(TOP_K):
            m = jnp.max(logits, axis=0, keepdims=True)
            idx = jnp.min(jnp.where(logits == m, row, NUM_EXPERTS), axis=0, keepdims=True)
            hit = row == idx
            idxs.append(idx.astype(F32))
            vals.append(m)
            chosen = chosen + jnp.where(hit, 1.0, 0.0)
            logits = jnp.where(hit, -jnp.inf, logits)
        topk_scr[i] = jnp.concatenate(idxs + vals, axis=0)
        cnt_scr[...] += tile_counts_of(chosen)

    @pl.when(phase == 1)
    def _():
        @pl.when(i == 0)
        def _():
            blocks = jnp.floor((cnt_scr[...] + (MOE_ROWS - 1)) * (1.0 / MOE_ROWS))
            erow = lax.broadcasted_iota(jnp.int32, (NUM_EXPERTS, LANES), 0)
            lane = lax.broadcasted_iota(jnp.int32, (NUM_EXPERTS, LANES), 1)
            cum = blocks
            shift = 1
            while shift < NUM_EXPERTS:
                cum = cum + jnp.where(erow >= shift, pltpu.roll(cum, shift, 0), 0.0)
                shift *= 2
            padend_ref[...] = jnp.where(lane == 1, cnt_scr[...], cum * MOE_ROWS)
            pstart_scr[...] = (cum - blocks) * MOE_ROWS
            base_scr[...] = jnp.zeros_like(base_scr)
            ti = lax.broadcasted_iota(jnp.int32, (ROUTE_TOKENS, ROUTE_TOKENS), 0)
            tj = lax.broadcasted_iota(jnp.int32, (ROUTE_TOKENS, ROUTE_TOKENS), 1)
            before_scr[...] = jnp.where(ti < tj, 1.0, 0.0).astype(BF16)

        topk = topk_scr[i]
        sel = [row == topk[k:k + 1, :].astype(jnp.int32) for k in range(TOP_K)]
        vals = [topk[TOP_K + k:TOP_K + k + 1, :] for k in range(TOP_K)]
        chosen = sum(jnp.where(hit, 1.0, 0.0) for hit in sel)
        earlier = _dot(chosen.astype(BF16), before_scr[...])
        pos = pstart_scr[:, 0:1] + base_scr[:, 0:1] + earlier
        dest = [jnp.sum(jnp.where(hit, pos, 0.0), axis=0, keepdims=True) for hit in sel]
        dest_ref[...] = jnp.concatenate(dest, axis=0).astype(jnp.int32)
        ex = [jnp.exp(v - vals[0]) for v in vals]
        denom = ex[0] + ex[1] + ex[2] + ex[3]
        gates_ref[...] = jnp.concatenate([e / denom for e in ex], axis=0)
        base_scr[...] += tile_counts_of(chosen)


def _router(logits_t):
    n = logits_t.shape[1]
    assert n % ROUTE_TOKENS == 0
    tiles = n // ROUTE_TOKENS
    return pl.pallas_call(
        _router_kernel,
        grid=(2, tiles),
        in_specs=[pl.BlockSpec((NUM_EXPERTS, ROUTE_TOKENS), lambda p, i: (0, i))],
        out_specs=(
            pl.BlockSpec((TOP_K, ROUTE_TOKENS), lambda p, i: (0, i * p)),
            pl.BlockSpec((TOP_K, ROUTE_TOKENS), lambda p, i: (0, i * p)),
            pl.BlockSpec((NUM_EXPERTS, LANES), lambda p, i: (0, 0)),
        ),
        out_shape=(
            jax.ShapeDtypeStruct((TOP_K, n), F32),
            jax.ShapeDtypeStruct((TOP_K, n), jnp.int32),
            jax.ShapeDtypeStruct((NUM_EXPERTS, LANES), F32),
        ),
        scratch_shapes=[pltpu.VMEM((NUM_EXPERTS, LANES), F32)] * 3
        + [pltpu.VMEM((ROUTE_TOKENS, ROUTE_TOKENS), BF16),
           pltpu.VMEM((tiles, 2 * TOP_K, ROUTE_TOKENS), F32)],
        compiler_params=pltpu.CompilerParams(
            dimension_semantics=("arbitrary", "arbitrary"), vmem_limit_bytes=VMEM_LIMIT),
        name="router",
    )(logits_t)


SC_CORES = 2
SC_SUBCORES = 16
SC_WORKERS = SC_CORES * SC_SUBCORES
DISPATCH_ROWS = 64


def _dispatch(h_sources, dest_chunks, m_pad):
    shares, first_chunk = [], 0
    for src in h_sources:
        n_chunks = src.shape[0] // DISPATCH_ROWS
        assert src.shape[0] % DISPATCH_ROWS == 0 and n_chunks % SC_WORKERS == 0
        shares.append((first_chunk, n_chunks // SC_WORKERS))
        first_chunk += n_chunks
    plan = [(s, first, per, j) for s, (first, per) in enumerate(shares) for j in range(per)]
    per_worker = len(plan)
    mesh = plsc.VectorSubcoreMesh(core_axis_name="c", subcore_axis_name="s")

    @functools.partial(
        pl.kernel, mesh=mesh,
        out_type=jax.ShapeDtypeStruct((m_pad, HALF), jnp.int32),
        scratch_types=[
            pltpu.VMEM((2, TOP_K, DISPATCH_ROWS), jnp.int32),
            pltpu.VMEM((2, DISPATCH_ROWS, HALF), jnp.int32),
            pltpu.SemaphoreType.DMA((2,)),
            pltpu.SemaphoreType.DMA((2,)),
        ],
        compiler_params=pltpu.CompilerParams(use_tc_tiling_on_sc=True),
        name="dispatch",
    )
    def dispatch_kernel(*refs):
        h_hbms = refs[:len(h_sources)]
        dest_hbm, out_hbm, idx_v, rows_v, load_sems, scatter_sems = refs[len(h_sources):]
        wid = lax.axis_index("s") * SC_CORES + lax.axis_index("c")

        def loads(j):
            src, first, per, k = plan[j]
            local = wid * per + k
            slot = j % 2
            return (
                pltpu.make_async_copy(dest_hbm.at[first + local], idx_v.at[slot], load_sems.at[slot]),
                pltpu.make_async_copy(h_hbms[src].at[pl.ds(local * DISPATCH_ROWS, DISPATCH_ROWS)],
                                      rows_v.at[slot], load_sems.at[slot]),
            )

        def scatters(j):
            slot = j % 2
            return [pltpu.make_async_copy(rows_v.at[slot], out_hbm.at[idx_v.at[slot, k]],
                                          scatter_sems.at[slot]) for k in range(TOP_K)]

        for cp in loads(0):
            cp.start()
        for j in range(per_worker):
            for cp in loads(j):
                cp.wait()
            if j >= 1:
                for cp in scatters(j - 1):
                    cp.wait()
            if j + 1 < per_worker:
                for cp in loads(j + 1):
                    cp.start()
            for cp in scatters(j):
                cp.start()
        for cp in scatters(per_worker - 1):
            cp.wait()

    return dispatch_kernel(*h_sources, dest_chunks)


def _gather_expert_rows(y_sorted, dest_chunks, row_offset, n):
    n_chunks = n // DISPATCH_ROWS
    assert n_chunks % SC_WORKERS == 0 and row_offset % DISPATCH_ROWS == 0
    per_worker = n_chunks // SC_WORKERS
    chunk0 = row_offset // DISPATCH_ROWS
    mesh = plsc.VectorSubcoreMesh(core_axis_name="c", subcore_axis_name="s")

    @functools.partial(
        pl.kernel, mesh=mesh,
        out_type=jax.ShapeDtypeStruct((TOP_K, n, HALF), jnp.int32),
        scratch_types=[
            pltpu.VMEM((TOP_K, DISPATCH_ROWS), jnp.int32),
            pltpu.VMEM((2, DISPATCH_ROWS, HALF), jnp.int32),
            pltpu.SemaphoreType.DMA((2,)),
        ],
        compiler_params=pltpu.CompilerParams(use_tc_tiling_on_sc=True),
        name="gather_expert_rows",
    )
    def gather_kernel(y_hbm, dest_hbm, out_hbm, idx_v, rows_v, sems):
        wid = lax.axis_index("s") * SC_CORES + lax.axis_index("c")

        def gather(k):
            return pltpu.make_async_copy(y_hbm.at[idx_v.at[k]], rows_v.at[k % 2], sems.at[k % 2])

        @pl.loop(0, per_worker)
        def _(j):
            local = wid * per_worker + j
            pltpu.sync_copy(dest_hbm.at[chunk0 + local], idx_v)
            gather(0).start()
            for k in range(TOP_K):
                if k + 1 < TOP_K:
                    gather(k + 1).start()
                gather(k).wait()
                pltpu.sync_copy(rows_v.at[k % 2],
                                out_hbm.at[k, pl.ds(local * DISPATCH_ROWS, DISPATCH_ROWS)])

    return gather_kernel(y_sorted, dest_chunks)


def _moe_kernel(be_ref, wsel_ref, units_ref, nused_ref, x_ref, wgu_ref, bgu_ref, wd_ref, bd_ref, y_ref,
                wgu_bf, wd_bf):
    del wsel_ref
    i = pl.program_id(0)

    def ffn(rows):
        gu = _dot(_unpack_bf16_pairs(x_ref[rows, :]), wgu_bf[...]) + bgu_ref[...]
        gate = jnp.minimum(gu[:, :EXPERT_FF], SWIGLU_LIMIT)
        up = jnp.clip(gu[:, EXPERT_FF:], -SWIGLU_LIMIT, SWIGLU_LIMIT)
        hmid = gate * (1.0 / (1.0 + jnp.exp(-SWIGLU_ALPHA * gate))) * (up + 1.0)
        y = _dot(hmid.astype(BF16), wd_bf[...]) + bd_ref[...]
        y_ref[rows, :] = _pack_bf16_pairs(y.astype(BF16))

    @pl.when(i < nused_ref[0])
    def _():
        @pl.when((i == 0) | (be_ref[i] != be_ref[jnp.maximum(i - 1, 0)]))
        def _():
            wgu_bf[...] = wgu_ref[...].astype(BF16)
            wd_bf[...] = wd_ref[...].astype(BF16)

        for units in range(1, MOE_ROWS // MOE_TAIL_ROWS + 1):
            @pl.when(units_ref[i] == units)
            def _(units=units):
                ffn(slice(0, units * MOE_TAIL_ROWS))


def _moe_experts(block_expert, weight_expert, block_units, n_used, x_sorted, w_gu, b_gu, w_down, b_down):
    m_pad = x_sorted.shape[0]
    n_blocks = m_pad // MOE_ROWS

    def blk(i, be, ws, hb, nu):
        return jnp.minimum(i, nu[0] - 1)

    def expert(i, be, ws, hb, nu):
        return be[blk(i, be, ws, hb, nu)]

    def held(i, be, ws, hb, nu):
        return ws[blk(i, be, ws, hb, nu)]

    grid_spec = pltpu.PrefetchScalarGridSpec(
        num_scalar_prefetch=4,
        grid=(n_blocks,),
        in_specs=[
            pl.BlockSpec((MOE_ROWS, HALF), lambda *a: (blk(*a), 0)),
            pl.BlockSpec((None, D_MODEL, 2 * EXPERT_FF), lambda *a: (held(*a), 0, 0)),
            pl.BlockSpec((None, 1, 2 * EXPERT_FF), lambda *a: (expert(*a), 0, 0)),
            pl.BlockSpec((None, EXPERT_FF, D_MODEL), lambda *a: (held(*a), 0, 0)),
            pl.BlockSpec((None, 1, D_MODEL), lambda *a: (expert(*a), 0, 0)),
        ],
        out_specs=pl.BlockSpec((MOE_ROWS, HALF), lambda *a: (blk(*a), 0)),
        scratch_shapes=[
            pltpu.VMEM((D_MODEL, 2 * EXPERT_FF), BF16),
            pltpu.VMEM((EXPERT_FF, D_MODEL), BF16),
        ],
    )
    return pl.pallas_call(
        _moe_kernel,
        grid_spec=grid_spec,
        out_shape=jax.ShapeDtypeStruct((m_pad, HALF), jnp.int32),
        compiler_params=pltpu.CompilerParams(
            dimension_semantics=("arbitrary",), vmem_limit_bytes=VMEM_LIMIT),
        name="moe_experts",
    )(block_expert, weight_expert, block_units, n_used, x_sorted, w_gu, b_gu, w_down, b_down)


def _combine_kernel(yk_ref, gates_ref, h_ref, g_ref, b_ref, out_ref):
    pad = jnp.zeros((LANES - TOP_K, COMBINE_TOKENS), F32)
    gates = jnp.transpose(jnp.concatenate([gates_ref[...], pad], axis=0))
    lo = hi = None
    for k in range(TOP_K):
        u = lax.bitcast_convert_type(yk_ref[k], jnp.uint32)
        gk = gates[:, k:k + 1]
        lo_k = lax.bitcast_convert_type(u << 16, F32) * gk
        hi_k = lax.bitcast_convert_type(u & jnp.uint32(HI_MASK), F32) * gk
        lo = lo_k if lo is None else lo + lo_k
        hi = hi_k if hi is None else hi + hi_k
    acc = ALPHA * h_ref[...] + jnp.concatenate([lo, hi], axis=1)
    out_ref[...] = _layer_norm(acc, g_ref[...], b_ref[...])


def _combine_kernel_aliased(yk_ref, gates_ref, h_ref, g_ref, b_ref, prev_ref, out_ref):
    del prev_ref
    _combine_kernel(yk_ref, gates_ref, h_ref, g_ref, b_ref, out_ref)


def _combine(yk, gates, token_offset, h_src, h_offset, ln_g, ln_b, out_prev):
    n_seg = yk.shape[1]
    out_rows = h_src.shape[0]
    tile0 = token_offset // COMBINE_TOKENS
    out_tile0 = h_offset // COMBINE_TOKENS
    in_specs = [
        pl.BlockSpec((TOP_K, COMBINE_TOKENS, HALF), lambda i: (0, i, 0)),
        pl.BlockSpec((TOP_K, COMBINE_TOKENS), lambda i: (0, tile0 + i)),
        pl.BlockSpec((COMBINE_TOKENS, D_MODEL), lambda i: (out_tile0 + i, 0)),
        _const_spec((1, D_MODEL)),
        _const_spec((1, D_MODEL)),
    ]
    args = [yk, gates, h_src, ln_g, ln_b]
    aliases = {}
    kern = _combine_kernel
    if out_prev is not None:
        in_specs.append(pl.BlockSpec(memory_space=pl.ANY))
        aliases = {len(args): 0}
        args.append(out_prev)
        kern = _combine_kernel_aliased
    return pl.pallas_call(
        kern,
        grid=(n_seg // COMBINE_TOKENS,),
        in_specs=in_specs,
        out_specs=pl.BlockSpec((COMBINE_TOKENS, D_MODEL), lambda i: (out_tile0 + i, 0)),
        out_shape=jax.ShapeDtypeStruct((out_rows, D_MODEL), F32),
        input_output_aliases=aliases,
        compiler_params=pltpu.CompilerParams(
            dimension_semantics=("arbitrary",), vmem_limit_bytes=VMEM_LIMIT),
        name="moe_combine",
    )(*args)


def kernel(x_prompt, x_sample, state_pool, state_gla, w_in, w_pool, pool_scale, w_gate_up, b_gate,
           gla_norm_w, w_out, ln1_g, ln1_b, w_router, b_router, w_gu, b_gu, w_down, b_down,
           ln2_g, ln2_b):
    assert w_in.shape[0] == 1, "single-layer kernel"
    bp, seq, _ = x_prompt.shape
    bs, dec_seq, _ = x_sample.shape
    assert dec_seq == CHUNK and seq % TILE_TOKENS == 0 and bs % CHUNKS_PER_TILE == 0
    n_prompt = bp * seq
    n_sample = bs * dec_seq
    n_total = n_prompt + n_sample
    nk = n_total * TOP_K
    n_blocks = -(-nk // MOE_ROWS) + NUM_EXPERTS
    m_pad = n_blocks * MOE_ROWS

    weights = _mixer_weights(w_in, w_pool, pool_scale, w_gate_up, b_gate, gla_norm_w, w_out,
                             ln1_g, ln1_b, w_router, b_router)

    h_p, hb_p, logits_p, hist_p, s_p = _mixer_prompt(x_prompt, weights)
    h_s, hb_s, logits_s, hist_s, s_s = _mixer_sample(x_sample, state_pool[0], state_gla[0], weights)

    gates_t, dest_t, layout = _router(jnp.concatenate([logits_p, logits_s], axis=1))
    pad_end = layout[:, 0].astype(jnp.int32)
    counts = layout[:, 1].astype(jnp.int32)
    block_start = jnp.arange(n_blocks, dtype=jnp.int32) * MOE_ROWS
    block_expert = jnp.minimum(jnp.sum((block_start[:, None] >= pad_end[None, :]).astype(jnp.int32), axis=1),
                               NUM_EXPERTS - 1)
    n_used = (pad_end[-1:] // MOE_ROWS).astype(jnp.int32)
    is_first = jnp.concatenate([jnp.ones((1,), bool), block_expert[1:] != block_expert[:-1]])
    blocks = jnp.arange(n_blocks, dtype=jnp.int32)
    later_other = ((blocks[None, :] > blocks[:, None]) & (blocks[None, :] < n_used[0])
                   & (block_expert[None, :] != block_expert[:, None]))
    next_expert = jnp.min(jnp.where(later_other, block_expert[None, :], NUM_EXPERTS), axis=1)
    next_expert = jnp.where(next_expert == NUM_EXPERTS, block_expert, next_expert)
    weight_expert = jnp.where(is_first, block_expert, next_expert).astype(jnp.int32)
    of_expert = block_expert[:, None] == jnp.arange(NUM_EXPERTS, dtype=jnp.int32)[None, :]
    seg_end = jnp.sum(jnp.where(of_expert, (pad_end - (-counts % MOE_ROWS))[None, :], 0), axis=1)
    block_units = jnp.clip((seg_end - block_start + MOE_TAIL_ROWS - 1) // MOE_TAIL_ROWS,
                          1, MOE_ROWS // MOE_TAIL_ROWS).astype(jnp.int32)

    dest_chunks = dest_t.reshape(TOP_K, n_total // DISPATCH_ROWS, DISPATCH_ROWS).transpose(1, 0, 2)
    x_sorted = _dispatch((hb_p, hb_s), dest_chunks, m_pad)
    y_sorted = _moe_experts(block_expert, weight_expert, block_units, n_used, x_sorted, w_gu[0], b_gu[0][:, None, :],
                            w_down[0], b_down[0][:, None, :])
    gates = gates_t
    ln_g, ln_b = ln2_g[0][None, :], ln2_b[0][None, :]
    yk = _gather_expert_rows(y_sorted, dest_chunks, n_prompt, n_sample)
    y_sample = _combine(yk, gates, n_prompt, h_s, 0, ln_g, ln_b, None)
    unit = SC_WORKERS * DISPATCH_ROWS
    assert n_prompt % unit == 0
    sizes, left = [], n_prompt // unit
    while left > 0:
        size = min(left, 1 if len(sizes) < 2 else 2)
        sizes.append(size * unit)
        left -= size
    y_prompt, start = None, 0
    for seg in sizes:
        yk = _gather_expert_rows(y_sorted, dest_chunks, start, seg)
        y_prompt = _combine(yk, gates, start, h_p, start, ln_g, ln_b, y_prompt)
        start += seg
    y_prompt = y_prompt.reshape(bp, seq, D_MODEL)
    y_sample = y_sample.reshape(bs, dec_seq, D_MODEL)
    return (y_prompt, y_sample, hist_p[None], s_p[None], hist_s[None], s_s[None])
```

```python
import functools

import jax
import jax.numpy as jnp
from jax import lax
from jax.experimental import pallas as pl
from jax.experimental.pallas import tpu as pltpu
from jax.experimental.pallas import tpu_sc as plsc

F32 = jnp.float32
BF16 = jnp.bfloat16

D_MODEL = 1024
CHUNK = 64
PAST_LEN = 1024
POOL_WIDTH = 512
POOL_WINDOWS = (2, 4, 8, 16)
POOL_GROUP = 128
POOL_HIST = 15
GLA_HEADS = 4
GLA_DK = 64
GLA_DV = 128
GATE_RANK = 16
GATE_NORMALIZER = 16.0
NUM_EXPERTS = 32
TOP_K = 4
EXPERT_FF = 1024
SWIGLU_LIMIT = 7.0
SWIGLU_ALPHA = 1.702
LN_EPS = 1e-5
RMS_EPS = 1e-6
ALPHA = 2.0 ** 0.25

Q0 = POOL_WIDTH
K0 = Q0 + GLA_HEADS * GLA_DK
V0 = K0 + GLA_HEADS * GLA_DK
R0 = V0 + GLA_HEADS * GLA_DV
N_MAIN = R0 + GLA_HEADS * GLA_DV
N_IN = N_MAIN + GATE_RANK

LANES = 128
TILE_TOKENS = 512
CHUNKS_PER_TILE = TILE_TOKENS // CHUNK
HIST_PAD = 16
OUT_CHUNKS = 8
SCAN_ROWS = 256
MOE_ROWS = 1024
MOE_TAIL_ROWS = 256
ROUTE_TOKENS = 2048
COMBINE_TOKENS = 1024
VMEM_LIMIT = 56 * 1024 * 1024


def _dot(a, b):
    return jnp.dot(a, b, preferred_element_type=F32)


def _dot_nt(a, b):
    return lax.dot_general(a, b, (((1,), (1,)), ((), ())), preferred_element_type=F32)


def _dot_tn(a, b):
    return lax.dot_general(a, b, (((0,), (0,)), ((), ())), preferred_element_type=F32)


HALF = D_MODEL // 2
HI_MASK = 0xFFFF0000


def _pack_bf16_pairs(xb):
    lo = lax.bitcast_convert_type(xb[:, :HALF].astype(F32), jnp.uint32) >> 16
    hi = lax.bitcast_convert_type(xb[:, HALF:].astype(F32), jnp.uint32) & jnp.uint32(HI_MASK)
    return lax.bitcast_convert_type(hi | lo, jnp.int32)


def _unpack_bf16_pairs(p):
    u = lax.bitcast_convert_type(p, jnp.uint32)
    lo = lax.bitcast_convert_type(u << 16, F32)
    hi = lax.bitcast_convert_type(u & jnp.uint32(HI_MASK), F32)
    return jnp.concatenate([lo, hi], axis=1).astype(BF16)


def _layer_norm(v, g, b):
    mu = jnp.mean(v, axis=-1, keepdims=True)
    c = v - mu
    var = jnp.mean(c * c, axis=-1, keepdims=True)
    return c * lax.rsqrt(var + LN_EPS) * g + b


N_MIXER_WEIGHTS = 12


def _mixer_kernel(per_chunk_state, pos0, *refs):
    if per_chunk_state:
        (x_ref, hist_in_ref, s_in_ref, *rest) = refs
    else:
        (x_ref, *rest) = refs
        hist_in_ref = s_in_ref = None
    (w_main_ref, w_glr_ref, w_gate_ref, b_gate_ref, w_pool_ref, pscale_ref, gnorm_ref,
     w_out_ref, ln1g_ref, ln1b_ref, w_router_ref, b_router_ref, *rest) = rest
    (h_ref, hb_ref, logits_ref, hist_out_ref, s_out_ref,
     proj_scr, b_scr, o_scr, ext_scr, st_scr, tri_scr, w_main_bf, w_out_bf) = rest

    if per_chunk_state:
        t = None
        first_step = pl.program_id(0) == 0
    else:
        t = pl.program_id(1)
        first_step = (pl.program_id(0) == 0) & (t == 0)
    x = x_ref[...].reshape(TILE_TOKENS, D_MODEL)
    xb = x.astype(BF16)

    @pl.when(first_step)
    def _():
        ti = lax.broadcasted_iota(jnp.int32, (SCAN_ROWS, SCAN_ROWS), 0)
        tj = lax.broadcasted_iota(jnp.int32, (SCAN_ROWS, SCAN_ROWS), 1)
        same_chunk = (ti // CHUNK) == (tj // CHUNK)
        tri_scr[...] = jnp.where(same_chunk & (ti >= tj), 1.0, 0.0).astype(BF16)
        w_main_bf[...] = w_main_ref[:, 0:N_MAIN].astype(BF16)
        w_out_bf[...] = w_out_ref[...].astype(BF16)
        if not per_chunk_state:
            st_scr[...] = jnp.zeros_like(st_scr)
            ext_scr[0:HIST_PAD, :] = jnp.zeros((HIST_PAD, POOL_WIDTH), F32)

    glr = _dot(xb, w_glr_ref[...])
    proj_scr[:, 0:V0] = _dot(xb, w_main_bf[:, 0:V0])
    gk = _dot(glr.astype(BF16), w_gate_ref[...]) + b_gate_ref[...]
    log_sig = jnp.minimum(gk, 0.0) - jnp.log1p(jnp.exp(-jnp.abs(gk)))
    g = log_sig / GATE_NORMALIZER
    g_hi = g.astype(BF16)
    g_lo = (g - g_hi.astype(F32)).astype(BF16)
    proj_scr[:, V0:N_MAIN] = _dot(xb, w_main_bf[:, V0:N_MAIN])
    for s in range(TILE_TOKENS // SCAN_ROWS):
        rs = slice(s * SCAN_ROWS, (s + 1) * SCAN_ROWS)
        b_scr[rs, :] = _dot(tri_scr[...], g_hi[rs]) + _dot(tri_scr[...], g_lo[rs])

    if per_chunk_state:
        seg_len, seg_stride, n_seg = CHUNK, CHUNK + HIST_PAD, CHUNKS_PER_TILE
        for c in range(n_seg):
            base = c * seg_stride
            ext_scr[base:base + HIST_PAD, :] = jnp.zeros((HIST_PAD, POOL_WIDTH), F32)
            ext_scr[base + 1:base + HIST_PAD, :] = hist_in_ref[c]
            ext_scr[base + HIST_PAD:base + seg_stride, :] = proj_scr[c * CHUNK:(c + 1) * CHUNK, 0:POOL_WIDTH]
        row_pos = pos0 + lax.broadcasted_iota(jnp.int32, (seg_len, POOL_GROUP), 0)
    else:
        seg_len, seg_stride, n_seg = TILE_TOKENS, TILE_TOKENS + HIST_PAD, 1
        ext_scr[0:HIST_PAD, :] = jnp.where(t == 0, 0.0, ext_scr[0:HIST_PAD, :])
        ext_scr[HIST_PAD:seg_stride, :] = proj_scr[:, 0:POOL_WIDTH]
        row_pos = pos0 + t * TILE_TOKENS + lax.broadcasted_iota(jnp.int32, (seg_len, POOL_GROUP), 0)

    pooled_groups = []
    for gi, w in enumerate(POOL_WINDOWS):
        gs = slice(gi * POOL_GROUP, (gi + 1) * POOL_GROUP)
        cnt = jnp.minimum(row_pos + 1, w).astype(F32)
        ext = ext_scr[:, gs]
        win = ext
        shift = 1
        while shift < w:
            win = win + pltpu.roll(win, shift, 0)
            shift *= 2
        segs = []
        for s in range(n_seg):
            base = s * seg_stride + HIST_PAD
            segs.append(win[base:base + seg_len] / cnt - ext[base:base + seg_len])
        pooled = segs[0] if n_seg == 1 else jnp.concatenate(segs, axis=0)
        pooled_groups.append(pooled.astype(BF16))
    pool_cols = []
    for p in range(len(POOL_WINDOWS) // 2):
        both = jnp.concatenate(pooled_groups[2 * p:2 * p + 2], axis=1)
        pool_cols.append(_dot(both, w_pool_ref[p]))
    pool_out = jnp.concatenate(pool_cols, axis=1) * pscale_ref[...]

    if per_chunk_state:
        for c in range(n_seg):
            end = (c + 1) * seg_stride
            hist_out_ref[c] = ext_scr[end - POOL_HIST:end, :]
    else:
        hist_out_ref[...] = ext_scr[seg_stride - POOL_HIST:seg_stride, :]
        ext_scr[0:HIST_PAD, :] = ext_scr[TILE_TOKENS:seg_stride, :]

    hk = GLA_HEADS * GLA_DK
    hv = GLA_HEADS * GLA_DV
    pair_rows = 2 * CHUNK
    decay_cols = LANES // CHUNKS_PER_TILE

    def head_of(shape, dim, width):
        return lax.broadcasted_iota(jnp.int32, shape, dim) // width

    same_head_k = head_of((hk, hk), 0, CHUNK) == head_of((hk, hk), 1, GLA_DK)
    same_head_v = head_of((hk, hv), 0, CHUNK) == head_of((hk, hv), 1, GLA_DV)
    pair_half = head_of((pair_rows, hv), 0, CHUNK)
    causal = (lax.broadcasted_iota(jnp.int32, (CHUNK, hk), 0)
              >= lax.broadcasted_iota(jnp.int32, (CHUNK, hk), 1) % CHUNK)

    b_all = b_scr[...]
    b_last = [b_scr[(c + 1) * CHUNK - 1:(c + 1) * CHUNK, :] for c in range(CHUNKS_PER_TILE)]
    b_last_rows = jnp.concatenate([jnp.broadcast_to(bl, (CHUNK, hk)) for bl in b_last], axis=0)
    k_all = proj_scr[:, K0:V0]
    qt_all = (proj_scr[:, Q0:K0] * (GLA_DK ** -0.5) * jnp.exp(b_all)).astype(BF16)
    kt_all = k_all * jnp.exp(-b_all)
    kl_t = jnp.transpose(k_all * jnp.exp(b_last_rows - b_all)).astype(BF16)
    decay_t = jnp.transpose(jnp.exp(jnp.concatenate(
        [jnp.broadcast_to(bl, (decay_cols, hk)) for bl in b_last], axis=0)))

    def finish_rows(rs):
        r = proj_scr[rs, R0:N_MAIN]
        silu_r = r * (1.0 / (1.0 + jnp.exp(-r)))
        gated = []
        for h in range(GLA_HEADS):
            vs = slice(h * GLA_DV, (h + 1) * GLA_DV)
            oh = o_scr[rs, vs]
            ms = jnp.mean(oh * oh, axis=-1, keepdims=True)
            gated.append(oh * lax.rsqrt(ms + RMS_EPS) * gnorm_ref[...] * silu_r[:, vs])
        mix_in = jnp.concatenate([pool_out[rs]] + gated, axis=1).astype(BF16)
        resid = ALPHA * x[rs] + _dot(mix_in, w_out_bf[...])
        h_val = _layer_norm(resid, ln1g_ref[...], ln1b_ref[...])
        h_ref[rs, :] = h_val
        hb = h_val.astype(BF16)
        hb_ref[rs, :] = _pack_bf16_pairs(hb)
        logits_ref[:, rs] = _dot_nt(w_router_ref[...], hb) + b_router_ref[:, 0:1]

    st = None if per_chunk_state else jnp.where(t == 0, 0.0, st_scr[...])
    for c in range(CHUNKS_PER_TILE):
        rows = slice(c * CHUNK, (c + 1) * CHUNK)
        pair = slice((c // 2) * pair_rows, (c // 2 + 1) * pair_rows)
        if per_chunk_state:
            st = s_in_ref[c].reshape(hk, GLA_DV)
        qt = qt_all[rows]
        zero = jnp.zeros((), BF16)
        k_stack = jnp.where(same_head_k, jnp.concatenate([kt_all[rows].astype(BF16)] * GLA_HEADS, axis=0), zero)
        v_stack = jnp.where(same_head_v, jnp.concatenate(
            [proj_scr[rows, V0:R0].astype(BF16)] * GLA_HEADS, axis=0), zero)
        s_stack = jnp.where(same_head_v, jnp.concatenate([st.astype(BF16)] * GLA_HEADS, axis=1), zero)
        att = jnp.where(causal, _dot_nt(qt, k_stack), 0.0)
        o_scr[rows, :] = _dot(att.astype(BF16), v_stack) + _dot(qt, s_stack)
        v_chunk = jnp.where(pair_half == c % 2, proj_scr[pair, V0:R0], 0.0).astype(BF16)
        upd = jnp.concatenate(
            [_dot(kl_t[h * GLA_DK:(h + 1) * GLA_DK, pair], v_chunk[:, h * GLA_DV:(h + 1) * GLA_DV])
             for h in range(GLA_HEADS)], axis=0)
        st = st * decay_t[:, c * decay_cols:c * decay_cols + 1] + upd
        if per_chunk_state:
            s_out_ref[c] = st.reshape(GLA_HEADS, GLA_DK, GLA_DV)
        if (c + 1) % OUT_CHUNKS == 0:
            finish_rows(slice((c + 1 - OUT_CHUNKS) * CHUNK, (c + 1) * CHUNK))

    if not per_chunk_state:
        st_scr[...] = st
        s_out_ref[...] = st.reshape(GLA_HEADS, GLA_DK, GLA_DV)


def _const_spec(shape, single_buffer=False):
    nd = len(shape)
    if single_buffer:
        return pl.BlockSpec(shape, lambda *_: (0,) * nd, pipeline_mode=pl.Buffered(1))
    return pl.BlockSpec(shape, lambda *_: (0,) * nd)


def _mixer_weight_specs():
    return [
        _const_spec((None, D_MODEL, N_IN), single_buffer=True),
        _const_spec((D_MODEL, LANES)),
        _const_spec((LANES, GLA_HEADS * GLA_DK)),
        _const_spec((1, GLA_HEADS * GLA_DK)),
        _const_spec((len(POOL_WINDOWS) // 2, 2 * POOL_GROUP, 2 * POOL_GROUP)),
        _const_spec((1, POOL_WIDTH)),
        _const_spec((1, GLA_DV)),
        _const_spec((None, D_MODEL, D_MODEL), single_buffer=True),
        _const_spec((1, D_MODEL)),
        _const_spec((1, D_MODEL)),
        _const_spec((NUM_EXPERTS, D_MODEL)),
        _const_spec((NUM_EXPERTS, LANES)),
    ]


def _mixer_weights(w_in, w_pool, pool_scale, w_gate_up, b_gate, gla_norm_w, w_out, ln1_g, ln1_b,
                   w_router, b_router):
    w_glr = jnp.zeros((D_MODEL, LANES), BF16).at[:, :GATE_RANK].set(w_in[0, :, N_MAIN:].astype(BF16))
    w_gate = jnp.zeros((LANES, GLA_HEADS * GLA_DK), BF16).at[:GATE_RANK].set(w_gate_up[0].astype(BF16))
    wp = w_pool[0].astype(BF16)
    zero = jnp.zeros((POOL_GROUP, POOL_GROUP), BF16)
    w_pool_pairs = jnp.stack([jnp.block([[wp[2 * p], zero], [zero, wp[2 * p + 1]]])
                              for p in range(len(POOL_WINDOWS) // 2)])
    weights = (
        w_in, w_glr, w_gate, b_gate[0][None, :],
        w_pool_pairs, pool_scale[0][None, :], gla_norm_w[0][None, :],
        w_out, ln1_g[0][None, :], ln1_b[0][None, :],
        w_router[0].T.astype(BF16), jnp.broadcast_to(b_router[0][:, None], (NUM_EXPERTS, LANES)),
    )
    assert len(weights) == N_MIXER_WEIGHTS
    return weights


def _mixer_scratch(per_chunk_state):
    ext_rows = (CHUNKS_PER_TILE * (CHUNK + HIST_PAD)) if per_chunk_state else (TILE_TOKENS + HIST_PAD)
    return [
        pltpu.VMEM((TILE_TOKENS, N_MAIN), F32),
        pltpu.VMEM((TILE_TOKENS, GLA_HEADS * GLA_DK), F32),
        pltpu.VMEM((TILE_TOKENS, GLA_HEADS * GLA_DV), F32),
        pltpu.VMEM((ext_rows, POOL_WIDTH), F32),
        pltpu.VMEM((GLA_HEADS * GLA_DK, GLA_DV), F32),
        pltpu.VMEM((SCAN_ROWS, SCAN_ROWS), BF16),
        pltpu.VMEM((D_MODEL, N_MAIN), BF16),
        pltpu.VMEM((D_MODEL, D_MODEL), BF16),
    ]


def _mixer_out_shapes(n, bsz):
    return (
        jax.ShapeDtypeStruct((n, D_MODEL), F32),
        jax.ShapeDtypeStruct((n, HALF), jnp.int32),
        jax.ShapeDtypeStruct((NUM_EXPERTS, n), F32),
        jax.ShapeDtypeStruct((bsz, POOL_HIST, POOL_WIDTH), F32),
        jax.ShapeDtypeStruct((bsz, GLA_HEADS, GLA_DK, GLA_DV), F32),
    )


def _mixer_prompt(x, weights):
    bsz, seq, _ = x.shape
    tiles = seq // TILE_TOKENS
    n_total = bsz * seq
    return pl.pallas_call(
        functools.partial(_mixer_kernel, False, 0),
        grid=(bsz, tiles),
        in_specs=[pl.BlockSpec((None, TILE_TOKENS, D_MODEL), lambda b, t: (b, t, 0))] + _mixer_weight_specs(),
        out_specs=(
            pl.BlockSpec((TILE_TOKENS, D_MODEL), lambda b, t: (b * tiles + t, 0)),
            pl.BlockSpec((TILE_TOKENS, HALF), lambda b, t: (b * tiles + t, 0)),
            pl.BlockSpec((NUM_EXPERTS, TILE_TOKENS), lambda b, t: (0, b * tiles + t)),
            pl.BlockSpec((None, POOL_HIST, POOL_WIDTH), lambda b, t: (b, 0, 0)),
            pl.BlockSpec((None, GLA_HEADS, GLA_DK, GLA_DV), lambda b, t: (b, 0, 0, 0)),
        ),
        out_shape=_mixer_out_shapes(n_total, bsz),
        scratch_shapes=_mixer_scratch(False),
        compiler_params=pltpu.CompilerParams(
            dimension_semantics=("arbitrary", "arbitrary"), vmem_limit_bytes=VMEM_LIMIT),
        name="mixer_prompt",
    )(x, *weights)


def _mixer_sample(x, hist, state, weights):
    bsz = x.shape[0]
    tiles = bsz // CHUNKS_PER_TILE
    return pl.pallas_call(
        functools.partial(_mixer_kernel, True, PAST_LEN),
        grid=(tiles,),
        in_specs=[
            pl.BlockSpec((CHUNKS_PER_TILE, CHUNK, D_MODEL), lambda i: (i, 0, 0)),
            pl.BlockSpec((CHUNKS_PER_TILE, POOL_HIST, POOL_WIDTH), lambda i: (i, 0, 0)),
            pl.BlockSpec((CHUNKS_PER_TILE, GLA_HEADS, GLA_DK, GLA_DV), lambda i: (i, 0, 0, 0)),
        ] + _mixer_weight_specs(),
        out_specs=(
            pl.BlockSpec((TILE_TOKENS, D_MODEL), lambda i: (i, 0)),
            pl.BlockSpec((TILE_TOKENS, HALF), lambda i: (i, 0)),
            pl.BlockSpec((NUM_EXPERTS, TILE_TOKENS), lambda i: (0, i)),
            pl.BlockSpec((CHUNKS_PER_TILE, POOL_HIST, POOL_WIDTH), lambda i: (i, 0, 0)),
            pl.BlockSpec((CHUNKS_PER_TILE, GLA_HEADS, GLA_DK, GLA_DV), lambda i: (i, 0, 0, 0)),
        ),
        out_shape=_mixer_out_shapes(bsz * CHUNK, bsz),
        scratch_shapes=_mixer_scratch(True),
        compiler_params=pltpu.CompilerParams(
            dimension_semantics=("arbitrary",), vmem_limit_bytes=VMEM_LIMIT),
        name="mixer_sample",
    )(x, hist, state, *weights)


def _router_kernel(lt_ref, gates_ref, dest_ref, padend_ref, cnt_scr, base_scr, pstart_scr, before_scr,
                   topk_scr):
    phase = pl.program_id(0)
    i = pl.program_id(1)
    shape = (NUM_EXPERTS, ROUTE_TOKENS)
    row = lax.broadcasted_iota(jnp.int32, shape, 0)

    def tile_counts_of(chosen):
        return jnp.broadcast_to(jnp.sum(chosen, axis=1, keepdims=True), (NUM_EXPERTS, LANES))

    @pl.when(phase == 0)
    def _():
        @pl.when(i == 0)
        def _():
            cnt_scr[...] = jnp.zeros_like(cnt_scr)

        logits = lt_ref[...]
        idxs, vals = [], []
        chosen = jnp.zeros(shape, F32)
        for _ in range(TOP_K):
            m = jnp.max(logits, axis=0, keepdims=True)
            idx = jnp.min(jnp.where(logits == m, row, NUM_EXPERTS), axis=0, keepdims=True)
            hit = row == idx
            idxs.append(idx.astype(F32))
            vals.append(m)
            chosen = chosen + jnp.where(hit, 1.0, 0.0)
            logits = jnp.where(hit, -jnp.inf, logits)
        topk_scr[i] = jnp.concatenate(idxs + vals, axis=0)
        cnt_scr[...] += tile_counts_of(chosen)

    @pl.when(phase == 1)
    def _():
        @pl.when(i == 0)
        def _():
            blocks = jnp.floor((cnt_scr[...] + (MOE_ROWS - 1)) * (1.0 / MOE_ROWS))
            erow = lax.broadcasted_iota(jnp.int32, (NUM_EXPERTS, LANES), 0)
            lane = lax.broadcasted_iota(jnp.int32, (NUM_EXPERTS, LANES), 1)
            cum = blocks
            shift = 1
            while shift < NUM_EXPERTS:
                cum = cum + jnp.where(erow >= shift, pltpu.roll(cum, shift, 0), 0.0)
                shift *= 2
            padend_ref[...] = jnp.where(lane == 1, cnt_scr[...], cum * MOE_ROWS)
            pstart_scr[...] = (cum - blocks) * MOE_ROWS
            base_scr[...] = jnp.zeros_like(base_scr)
            ti = lax.broadcasted_iota(jnp.int32, (ROUTE_TOKENS, ROUTE_TOKENS), 0)
            tj = lax.broadcasted_iota(jnp.int32, (ROUTE_TOKENS, ROUTE_TOKENS), 1)
            before_scr[...] = jnp.where(ti < tj, 1.0, 0.0).astype(BF16)

        topk = topk_scr[i]
        sel = [row == topk[k:k + 1, :].astype(jnp.int32) for k in range(TOP_K)]
        vals = [topk[TOP_K + k:TOP_K + k + 1, :] for k in range(TOP_K)]
        chosen = sum(jnp.where(hit, 1.0, 0.0) for hit in sel)
        earlier = _dot(chosen.astype(BF16), before_scr[...])
        pos = pstart_scr[:, 0:1] + base_scr[:, 0:1] + earlier
        dest = [jnp.sum(jnp.where(hit, pos, 0.0), axis=0, keepdims=True) for hit in sel]
        dest_ref[...] = jnp.concatenate(dest, axis=0).astype(jnp.int32)
        ex = [jnp.exp(v - vals[0]) for v in vals]
        denom = ex[0] + ex[1] + ex[2] + ex[3]
        gates_ref[...] = jnp.concatenate([e / denom for e in ex], axis=0)
        base_scr[...] += tile_counts_of(chosen)


def _router(logits_t):
    n = logits_t.shape[1]
    assert n % ROUTE_TOKENS == 0
    tiles = n // ROUTE_TOKENS
    return pl.pallas_call(
        _router_kernel,
        grid=(2, tiles),
        in_specs=[pl.BlockSpec((NUM_EXPERTS, ROUTE_TOKENS), lambda p, i: (0, i))],
        out_specs=(
            pl.BlockSpec((TOP_K, ROUTE_TOKENS), lambda p, i: (0, i * p)),
            pl.BlockSpec((TOP_K, ROUTE_TOKENS), lambda p, i: (0, i * p)),
            pl.BlockSpec((NUM_EXPERTS, LANES), lambda p, i: (0, 0)),
        ),
        out_shape=(
            jax.ShapeDtypeStruct((TOP_K, n), F32),
            jax.ShapeDtypeStruct((TOP_K, n), jnp.int32),
            jax.ShapeDtypeStruct((NUM_EXPERTS, LANES), F32),
        ),
        scratch_shapes=[pltpu.VMEM((NUM_EXPERTS, LANES), F32)] * 3
        + [pltpu.VMEM((ROUTE_TOKENS, ROUTE_TOKENS), BF16),
           pltpu.VMEM((tiles, 2 * TOP_K, ROUTE_TOKENS), F32)],
        compiler_params=pltpu.CompilerParams(
            dimension_semantics=("arbitrary", "arbitrary"), vmem_limit_bytes=VMEM_LIMIT),
        name="router",
    )(logits_t)


SC_CORES = 2
SC_SUBCORES = 16
SC_WORKERS = SC_CORES * SC_SUBCORES
DISPATCH_ROWS = 64


def _dispatch(h_sources, dest_chunks, m_pad):
    shares, first_chunk = [], 0
    for src in h_sources:
        n_chunks = src.shape[0] // DISPATCH_ROWS
        assert src.shape[0] % DISPATCH_ROWS == 0 and n_chunks % SC_WORKERS == 0
        shares.append((first_chunk, n_chunks // SC_WORKERS))
        first_chunk += n_chunks
    plan = [(s, first, per, j) for s, (first, per) in enumerate(shares) for j in range(per)]
    per_worker = len(plan)
    mesh = plsc.VectorSubcoreMesh(core_axis_name="c", subcore_axis_name="s")

    @functools.partial(
        pl.kernel, mesh=mesh,
        out_type=jax.ShapeDtypeStruct((m_pad, HALF), jnp.int32),
        scratch_types=[
            pltpu.VMEM((2, TOP_K, DISPATCH_ROWS), jnp.int32),
            pltpu.VMEM((2, DISPATCH_ROWS, HALF), jnp.int32),
            pltpu.SemaphoreType.DMA((2,)),
            pltpu.SemaphoreType.DMA((2,)),
        ],
        compiler_params=pltpu.CompilerParams(use_tc_tiling_on_sc=True),
        name="dispatch",
    )
    def dispatch_kernel(*refs):
        h_hbms = refs[:len(h_sources)]
        dest_hbm, out_hbm, idx_v, rows_v, load_sems, scatter_sems = refs[len(h_sources):]
        wid = lax.axis_index("s") * SC_CORES + lax.axis_index("c")

        def loads(j):
            src, first, per, k = plan[j]
            local = wid * per + k
            slot = j % 2
            return (
                pltpu.make_async_copy(dest_hbm.at[first + local], idx_v.at[slot], load_sems.at[slot]),
                pltpu.make_async_copy(h_hbms[src].at[pl.ds(local * DISPATCH_ROWS, DISPATCH_ROWS)],
                                      rows_v.at[slot], load_sems.at[slot]),
            )

        def scatters(j):
            slot = j % 2
            return [pltpu.make_async_copy(rows_v.at[slot], out_hbm.at[idx_v.at[slot, k]],
                                          scatter_sems.at[slot]) for k in range(TOP_K)]

        for cp in loads(0):
            cp.start()
        for j in range(per_worker):
            for cp in loads(j):
                cp.wait()
            if j >= 1:
                for cp in scatters(j - 1):
                    cp.wait()
            if j + 1 < per_worker:
                for cp in loads(j + 1):
                    cp.start()
            for cp in scatters(j):
                cp.start()
        for cp in scatters(per_worker - 1):
            cp.wait()

    return dispatch_kernel(*h_sources, dest_chunks)


def _gather_expert_rows(y_sorted, dest_chunks, row_offset, n):
    n_chunks = n // DISPATCH_ROWS
    assert n_chunks % SC_WORKERS == 0 and row_offset % DISPATCH_ROWS == 0
    per_worker = n_chunks // SC_WORKERS
    chunk0 = row_offset // DISPATCH_ROWS
    mesh = plsc.VectorSubcoreMesh(core_axis_name="c", subcore_axis_name="s")

    @functools.partial(
        pl.kernel, mesh=mesh,
        out_type=jax.ShapeDtypeStruct((TOP_K, n, HALF), jnp.int32),
        scratch_types=[
            pltpu.VMEM((TOP_K, DISPATCH_ROWS), jnp.int32),
            pltpu.VMEM((2, DISPATCH_ROWS, HALF), jnp.int32),
            pltpu.SemaphoreType.DMA((2,)),
        ],
        compiler_params=pltpu.CompilerParams(use_tc_tiling_on_sc=True),
        name="gather_expert_rows",
    )
    def gather_kernel(y_hbm, dest_hbm, out_hbm, idx_v, rows_v, sems):
        wid = lax.axis_index("s") * SC_CORES + lax.axis_index("c")

        def gather(k):
            return pltpu.make_async_copy(y_hbm.at[idx_v.at[k]], rows_v.at[k % 2], sems.at[k % 2])

        @pl.loop(0, per_worker)
        def _(j):
            local = wid * per_worker + j
            pltpu.sync_copy(dest_hbm.at[chunk0 + local], idx_v)
            gather(0).start()
            for k in range(TOP_K):
                if k + 1 < TOP_K:
                    gather(k + 1).start()
                gather(k).wait()
                pltpu.sync_copy(rows_v.at[k % 2],
                                out_hbm.at[k, pl.ds(local * DISPATCH_ROWS, DISPATCH_ROWS)])

    return gather_kernel(y_sorted, dest_chunks)


def _moe_kernel(be_ref, wsel_ref, units_ref, nused_ref, x_ref, wgu_ref, bgu_ref, wd_ref, bd_ref, y_ref,
                wgu_bf, wd_bf):
    del wsel_ref
    i = pl.program_id(0)

    def ffn(rows):
        gu = _dot(_unpack_bf16_pairs(x_ref[rows, :]), wgu_bf[...]) + bgu_ref[...]
        gate = jnp.minimum(gu[:, :EXPERT_FF], SWIGLU_LIMIT)
        up = jnp.clip(gu[:, EXPERT_FF:], -SWIGLU_LIMIT, SWIGLU_LIMIT)
        hmid = gate * (1.0 / (1.0 + jnp.exp(-SWIGLU_ALPHA * gate))) * (up + 1.0)
        y = _dot(hmid.astype(BF16), wd_bf[...]) + bd_ref[...]
        y_ref[rows, :] = _pack_bf16_pairs(y.astype(BF16))

    @pl.when(i < nused_ref[0])
    def _():
        @pl.when((i == 0) | (be_ref[i] != be_ref[jnp.maximum(i - 1, 0)]))
        def _():
            wgu_bf[...] = wgu_ref[...].astype(BF16)
            wd_bf[...] = wd_ref[...].astype(BF16)

        for units in range(1, MOE_ROWS // MOE_TAIL_ROWS + 1):
            @pl.when(units_ref[i] == units)
            def _(units=units):
                ffn(slice(0, units * MOE_TAIL_ROWS))


def _moe_experts(block_expert, weight_expert, block_units, n_used, x_sorted, w_gu, b_gu, w_down, b_down):
    m_pad = x_sorted.shape[0]
    n_blocks = m_pad // MOE_ROWS

    def blk(i, be, ws, hb, nu):
        return jnp.minimum(i, nu[0] - 1)

    def expert(i, be, ws, hb, nu):
        return be[blk(i, be, ws, hb, nu)]

    def held(i, be, ws, hb, nu):
        return ws[blk(i, be, ws, hb, nu)]

    grid_spec = pltpu.PrefetchScalarGridSpec(
        num_scalar_prefetch=4,
        grid=(n_blocks,),
        in_specs=[
            pl.BlockSpec((MOE_ROWS, HALF), lambda *a: (blk(*a), 0)),
            pl.BlockSpec((None, D_MODEL, 2 * EXPERT_FF), lambda *a: (held(*a), 0, 0)),
            pl.BlockSpec((None, 1, 2 * EXPERT_FF), lambda *a: (expert(*a), 0, 0)),
            pl.BlockSpec((None, EXPERT_FF, D_MODEL), lambda *a: (held(*a), 0, 0)),
            pl.BlockSpec((None, 1, D_MODEL), lambda *a: (expert(*a), 0, 0)),
        ],
        out_specs=pl.BlockSpec((MOE_ROWS, HALF), lambda *a: (blk(*a), 0)),
        scratch_shapes=[
            pltpu.VMEM((D_MODEL, 2 * EXPERT_FF), BF16),
            pltpu.VMEM((EXPERT_FF, D_MODEL), BF16),
        ],
    )
    return pl.pallas_call(
        _moe_kernel,
        grid_spec=grid_spec,
        out_shape=jax.ShapeDtypeStruct((m_pad, HALF), jnp.int32),
        compiler_params=pltpu.CompilerParams(
            dimension_semantics=("arbitrary",), vmem_limit_bytes=VMEM_LIMIT),
        name="moe_experts",
    )(block_expert, weight_expert, block_units, n_used, x_sorted, w_gu, b_gu, w_down, b_down)


def _combine_kernel(yk_ref, gates_ref, h_ref, g_ref, b_ref, out_ref):
    pad = jnp.zeros((LANES - TOP_K, COMBINE_TOKENS), F32)
    gates = jnp.transpose(jnp.concatenate([gates_ref[...], pad], axis=0))
    lo = hi = None
    for k in range(TOP_K):
        u = lax.bitcast_convert_type(yk_ref[k], jnp.uint32)
        gk = gates[:, k:k + 1]
        lo_k = lax.bitcast_convert_type(u << 16, F32) * gk
        hi_k = lax.bitcast_convert_type(u & jnp.uint32(HI_MASK), F32) * gk
        lo = lo_k if lo is None else lo + lo_k
        hi = hi_k if hi is None else hi + hi_k
    acc = ALPHA * h_ref[...] + jnp.concatenate([lo, hi], axis=1)
    out_ref[...] = _layer_norm(acc, g_ref[...], b_ref[...])


def _combine_kernel_aliased(yk_ref, gates_ref, h_ref, g_ref, b_ref, prev_ref, out_ref):
    del prev_ref
    _combine_kernel(yk_ref, gates_ref, h_ref, g_ref, b_ref, out_ref)


def _combine(yk, gates, token_offset, h_src, h_offset, ln_g, ln_b, out_prev):
    n_seg = yk.shape[1]
    out_rows = h_src.shape[0]
    tile0 = token_offset // COMBINE_TOKENS
    out_tile0 = h_offset // COMBINE_TOKENS
    in_specs = [
        pl.BlockSpec((TOP_K, COMBINE_TOKENS, HALF), lambda i: (0, i, 0)),
        pl.BlockSpec((TOP_K, COMBINE_TOKENS), lambda i: (0, tile0 + i)),
        pl.BlockSpec((COMBINE_TOKENS, D_MODEL), lambda i: (out_tile0 + i, 0)),
        _const_spec((1, D_MODEL)),
        _const_spec((1, D_MODEL)),
    ]
    args = [yk, gates, h_src, ln_g, ln_b]
    aliases = {}
    kern = _combine_kernel
    if out_prev is not None:
        in_specs.append(pl.BlockSpec(memory_space=pl.ANY))
        aliases = {len(args): 0}
        args.append(out_prev)
        kern = _combine_kernel_aliased
    return pl.pallas_call(
        kern,
        grid=(n_seg // COMBINE_TOKENS,),
        in_specs=in_specs,
        out_specs=pl.BlockSpec((COMBINE_TOKENS, D_MODEL), lambda i: (out_tile0 + i, 0)),
        out_shape=jax.ShapeDtypeStruct((out_rows, D_MODEL), F32),
        input_output_aliases=aliases,
        compiler_params=pltpu.CompilerParams(
            dimension_semantics=("arbitrary",), vmem_limit_bytes=VMEM_LIMIT),
        name="moe_combine",
    )(*args)


def kernel(x_prompt, x_sample, state_pool, state_gla, w_in, w_pool, pool_scale, w_gate_up, b_gate,
           gla_norm_w, w_out, ln1_g, ln1_b, w_router, b_router, w_gu, b_gu, w_down, b_down,
           ln2_g, ln2_b):
    assert w_in.shape[0] == 1, "single-layer kernel"
    bp, seq, _ = x_prompt.shape
    bs, dec_seq, _ = x_sample.shape
    assert dec_seq == CHUNK and seq % TILE_TOKENS == 0 and bs % CHUNKS_PER_TILE == 0
    n_prompt = bp * seq
    n_sample = bs * dec_seq
    n_total = n_prompt + n_sample
    nk = n_total * TOP_K
    n_blocks = -(-nk // MOE_ROWS) + NUM_EXPERTS
    m_pad = n_blocks * MOE_ROWS

    weights = _mixer_weights(w_in, w_pool, pool_scale, w_gate_up, b_gate, gla_norm_w, w_out,
                             ln1_g, ln1_b, w_router, b_router)

    h_p, hb_p, logits_p, hist_p, s_p = _mixer_prompt(x_prompt, weights)
    h_s, hb_s, logits_s, hist_s, s_s = _mixer_sample(x_sample, state_pool[0], state_gla[0], weights)

    gates_t, dest_t, layout = _router(jnp.concatenate([logits_p, logits_s], axis=1))
    pad_end = layout[:, 0].astype(jnp.int32)
    counts = layout[:, 1].astype(jnp.int32)
    block_start = jnp.arange(n_blocks, dtype=jnp.int32) * MOE_ROWS
    block_expert = jnp.minimum(jnp.sum((block_start[:, None] >= pad_end[None, :]).astype(jnp.int32), axis=1),
                               NUM_EXPERTS - 1)
    n_used = (pad_end[-1:] // MOE_ROWS).astype(jnp.int32)
    is_first = jnp.concatenate([jnp.ones((1,), bool), block_expert[1:] != block_expert[:-1]])
    blocks = jnp.arange(n_blocks, dtype=jnp.int32)
    later_other = ((blocks[None, :] > blocks[:, None]) & (blocks[None, :] < n_used[0])
                   & (block_expert[None, :] != block_expert[:, None]))
    next_expert = jnp.min(jnp.where(later_other, block_expert[None, :], NUM_EXPERTS), axis=1)
    next_expert = jnp.where(next_expert == NUM_EXPERTS, block_expert, next_expert)
    weight_expert = jnp.where(is_first, block_expert, next_expert).astype(jnp.int32)
    of_expert = block_expert[:, None] == jnp.arange(NUM_EXPERTS, dtype=jnp.int32)[None, :]
    seg_end = jnp.sum(jnp.where(of_expert, (pad_end - (-counts % MOE_ROWS))[None, :], 0), axis=1)
    block_units = jnp.clip((seg_end - block_start + MOE_TAIL_ROWS - 1) // MOE_TAIL_ROWS,
                          1, MOE_ROWS // MOE_TAIL_ROWS).astype(jnp.int32)

    dest_chunks = dest_t.reshape(TOP_K, n_total // DISPATCH_ROWS, DISPATCH_ROWS).transpose(1, 0, 2)
    x_sorted = _dispatch((hb_p, hb_s), dest_chunks, m_pad)
    y_sorted = _moe_experts(block_expert, weight_expert, block_units, n_used, x_sorted, w_gu[0], b_gu[0][:, None, :],
                            w_down[0], b_down[0][:, None, :])
    gates = gates_t
    ln_g, ln_b = ln2_g[0][None, :], ln2_b[0][None, :]
    yk = _gather_expert_rows(y_sorted, dest_chunks, n_prompt, n_sample)
    y_sample = _combine(yk, gates, n_prompt, h_s, 0, ln_g, ln_b, None)
    unit = SC_WORKERS * DISPATCH_ROWS
    assert n_prompt % unit == 0
    sizes, left = [], n_prompt // unit
    while left > 0:
        size = min(left, 1 if len(sizes) < 2 else 2)
        sizes.append(size * unit)
        left -= size
    y_prompt, start = None, 0
    for seg in sizes:
        yk = _gather_expert_rows(y_sorted, dest_chunks, start, seg)
        y_prompt = _combine(yk, gates, start, h_p, start, ln_g, ln_b, y_prompt)
        start += seg
    y_prompt = y_prompt.reshape(bp, seq, D_MODEL)
    y_sample = y_sample.reshape(bs, dec_seq, D_MODEL)
    return (y_prompt, y_sample, hist_p[None], s_p[None], hist_s[None], s_s[None])
```

```python
import functools

import jax
import jax.numpy as jnp
from jax import lax
from jax.experimental import pallas as pl
from jax.experimental.pallas import tpu as pltpu
from jax.experimental.pallas import tpu_sc as plsc

F32 = jnp.float32
BF16 = jnp.bfloat16

D_MODEL = 1024
CHUNK = 64
PAST_LEN = 1024
POOL_WIDTH = 512
POOL_WINDOWS = (2, 4, 8, 16)
POOL_GROUP = 128
POOL_HIST = 15
GLA_HEADS = 4
GLA_DK = 64
GLA_DV = 128
GATE_RANK = 16
GATE_NORMALIZER = 16.0
NUM_EXPERTS = 32
TOP_K = 4
EXPERT_FF = 1024
SWIGLU_LIMIT = 7.0
SWIGLU_ALPHA = 1.702
LN_EPS = 1e-5
RMS_EPS = 1e-6
ALPHA = 2.0 ** 0.25

Q0 = POOL_WIDTH
K0 = Q0 + GLA_HEADS * GLA_DK
V0 = K0 + GLA_HEADS * GLA_DK
R0 = V0 + GLA_HEADS * GLA_DV
N_MAIN = R0 + GLA_HEADS * GLA_DV
N_IN = N_MAIN + GATE_RANK

LANES = 128
TILE_TOKENS = 512
CHUNKS_PER_TILE = TILE_TOKENS // CHUNK
HIST_PAD = 16
OUT_CHUNKS = 8
SCAN_ROWS = 256
MOE_ROWS = 1024
MOE_TAIL_ROWS = 256
ROUTE_TOKENS = 2048
COMBINE_TOKENS = 1024
VMEM_LIMIT = 56 * 1024 * 1024


def _dot(a, b):
    return jnp.dot(a, b, preferred_element_type=F32)


def _dot_nt(a, b):
    return lax.dot_general(a, b, (((1,), (1,)), ((), ())), preferred_element_type=F32)


HALF = D_MODEL // 2
HI_MASK = 0xFFFF0000


def _pack_bf16_pairs(xb):
    lo = lax.bitcast_convert_type(xb[:, :HALF].astype(F32), jnp.uint32) >> 16
    hi = lax.bitcast_convert_type(xb[:, HALF:].astype(F32), jnp.uint32) & jnp.uint32(HI_MASK)
    return lax.bitcast_convert_type(hi | lo, jnp.int32)


def _unpack_bf16_pairs(p):
    u = lax.bitcast_convert_type(p, jnp.uint32)
    lo = lax.bitcast_convert_type(u << 16, F32)
    hi = lax.bitcast_convert_type(u & jnp.uint32(HI_MASK), F32)
    return jnp.concatenate([lo, hi], axis=1).astype(BF16)


def _layer_norm(v, g, b):
    mu = jnp.mean(v, axis=-1, keepdims=True)
    c = v - mu
    var = jnp.mean(c * c, axis=-1, keepdims=True)
    return c * lax.rsqrt(var + LN_EPS) * g + b


N_MIXER_WEIGHTS = 12


def _mixer_kernel(per_chunk_state, pos0, *refs):
    if per_chunk_state:
        (x_ref, hist_in_ref, s_in_ref, *rest) = refs
    else:
        (x_ref, *rest) = refs
        hist_in_ref = s_in_ref = None
    (w_main_ref, w_glr_ref, w_gate_ref, b_gate_ref, w_pool_ref, pscale_ref, gnorm_ref,
     w_out_ref, ln1g_ref, ln1b_ref, w_router_ref, b_router_ref, *rest) = rest
    (h_ref, hb_ref, logits_ref, hist_out_ref, s_out_ref,
     proj_scr, b_scr, o_scr, ext_scr, st_scr, tri_scr, w_main_bf, w_out_bf) = rest

    if per_chunk_state:
        t = None
        first_step = pl.program_id(0) == 0
    else:
        t = pl.program_id(1)
        first_step = (pl.program_id(0) == 0) & (t == 0)
    x = x_ref[...].reshape(TILE_TOKENS, D_MODEL)
    xb = x.astype(BF16)

    @pl.when(first_step)
    def _():
        ti = lax.broadcasted_iota(jnp.int32, (SCAN_ROWS, SCAN_ROWS), 0)
        tj = lax.broadcasted_iota(jnp.int32, (SCAN_ROWS, SCAN_ROWS), 1)
        same_chunk = (ti // CHUNK) == (tj // CHUNK)
        tri_scr[...] = jnp.where(same_chunk & (ti >= tj), 1.0, 0.0).astype(BF16)
        w_main_bf[...] = w_main_ref[:, 0:N_MAIN].astype(BF16)
        w_out_bf[...] = w_out_ref[...].astype(BF16)
        if not per_chunk_state:
            st_scr[...] = jnp.zeros_like(st_scr)
            ext_scr[0:HIST_PAD, :] = jnp.zeros((HIST_PAD, POOL_WIDTH), F32)

    glr = _dot(xb, w_glr_ref[...])
    proj_scr[:, 0:V0] = _dot(xb, w_main_bf[:, 0:V0])
    gk = _dot(glr.astype(BF16), w_gate_ref[...]) + b_gate_ref[...]
    log_sig = jnp.minimum(gk, 0.0) - jnp.log1p(jnp.exp(-jnp.abs(gk)))
    g = log_sig / GATE_NORMALIZER
    g_hi = g.astype(BF16)
    g_lo = (g - g_hi.astype(F32)).astype(BF16)
    proj_scr[:, V0:N_MAIN] = _dot(xb, w_main_bf[:, V0:N_MAIN])
    for s in range(TILE_TOKENS // SCAN_ROWS):
        rs = slice(s * SCAN_ROWS, (s + 1) * SCAN_ROWS)
        b_scr[rs, :] = _dot(tri_scr[...], g_hi[rs]) + _dot(tri_scr[...], g_lo[rs])

    if per_chunk_state:
        seg_len, seg_stride, n_seg = CHUNK, CHUNK + HIST_PAD, CHUNKS_PER_TILE
        for c in range(n_seg):
            base = c * seg_stride
            ext_scr[base:base + HIST_PAD, :] = jnp.zeros((HIST_PAD, POOL_WIDTH), F32)
            ext_scr[base + 1:base + HIST_PAD, :] = hist_in_ref[c]
            ext_scr[base + HIST_PAD:base + seg_stride, :] = proj_scr[c * CHUNK:(c + 1) * CHUNK, 0:POOL_WIDTH]
        row_pos = pos0 + lax.broadcasted_iota(jnp.int32, (seg_len, POOL_GROUP), 0)
    else:
        seg_len, seg_stride, n_seg = TILE_TOKENS, TILE_TOKENS + HIST_PAD, 1
        ext_scr[0:HIST_PAD, :] = jnp.where(t == 0, 0.0, ext_scr[0:HIST_PAD, :])
        ext_scr[HIST_PAD:seg_stride, :] = proj_scr[:, 0:POOL_WIDTH]
        row_pos = pos0 + t * TILE_TOKENS + lax.broadcasted_iota(jnp.int32, (seg_len, POOL_GROUP), 0)

    pooled_groups = []
    for gi, w in enumerate(POOL_WINDOWS):
        gs = slice(gi * POOL_GROUP, (gi + 1) * POOL_GROUP)
        cnt = jnp.minimum(row_pos + 1, w).astype(F32)
        ext = ext_scr[:, gs]
        win = ext
        shift = 1
        while shift < w:
            win = win + pltpu.roll(win, shift, 0)
            shift *= 2
        segs = []
        for s in range(n_seg):
            base = s * seg_stride + HIST_PAD
            segs.append(win[base:base + seg_len] / cnt - ext[base:base + seg_len])
        pooled = segs[0] if n_seg == 1 else jnp.concatenate(segs, axis=0)
        pooled_groups.append(pooled.astype(BF16))
    pool_cols = []
    for p in range(len(POOL_WINDOWS) // 2):
        both = jnp.concatenate(pooled_groups[2 * p:2 * p + 2], axis=1)
        pool_cols.append(_dot(both, w_pool_ref[p]))
    pool_out = jnp.concatenate(pool_cols, axis=1) * pscale_ref[...]

    if per_chunk_state:
        for c in range(n_seg):
            end = (c + 1) * seg_stride
            hist_out_ref[c] = ext_scr[end - POOL_HIST:end, :]
    else:
        hist_out_ref[...] = ext_scr[seg_stride - POOL_HIST:seg_stride, :]
        ext_scr[0:HIST_PAD, :] = ext_scr[TILE_TOKENS:seg_stride, :]

    hk = GLA_HEADS * GLA_DK
    hv = GLA_HEADS * GLA_DV
    pair_rows = 2 * CHUNK
    decay_cols = LANES // CHUNKS_PER_TILE

    def head_of(shape, dim, width):
        return lax.broadcasted_iota(jnp.int32, shape, dim) // width

    same_head_k = head_of((hk, hk), 0, CHUNK) == head_of((hk, hk), 1, GLA_DK)
    same_head_v = head_of((hk, hv), 0, CHUNK) == head_of((hk, hv), 1, GLA_DV)
    pair_half = head_of((pair_rows, hv), 0, CHUNK)
    causal = (lax.broadcasted_iota(jnp.int32, (CHUNK, hk), 0)
              >= lax.broadcasted_iota(jnp.int32, (CHUNK, hk), 1) % CHUNK)

    b_all = b_scr[...]
    b_last = [b_scr[(c + 1) * CHUNK - 1:(c + 1) * CHUNK, :] for c in range(CHUNKS_PER_TILE)]
    b_last_rows = jnp.concatenate([jnp.broadcast_to(bl, (CHUNK, hk)) for bl in b_last], axis=0)
    k_all = proj_scr[:, K0:V0]
    qt_all = (proj_scr[:, Q0:K0] * (GLA_DK ** -0.5) * jnp.exp(b_all)).astype(BF16)
    kt_all = k_all * jnp.exp(-b_all)
    kl_t = jnp.transpose(k_all * jnp.exp(b_last_rows - b_all)).astype(BF16)
    decay_t = jnp.transpose(jnp.exp(jnp.concatenate(
        [jnp.broadcast_to(bl, (decay_cols, hk)) for bl in b_last], axis=0)))

    def finish_rows(rs):
        r = proj_scr[rs, R0:N_MAIN]
        silu_r = r * (1.0 / (1.0 + jnp.exp(-r)))
        gated = []
        for h in range(GLA_HEADS):
            vs = slice(h * GLA_DV, (h + 1) * GLA_DV)
            oh = o_scr[rs, vs]
            ms = jnp.mean(oh * oh, axis=-1, keepdims=True)
            gated.append(oh * lax.rsqrt(ms + RMS_EPS) * gnorm_ref[...] * silu_r[:, vs])
        mix_in = jnp.concatenate([pool_out[rs]] + gated, axis=1).astype(BF16)
        resid = ALPHA * x[rs] + _dot(mix_in, w_out_bf[...])
        h_val = _layer_norm(resid, ln1g_ref[...], ln1b_ref[...])
        h_ref[rs, :] = h_val
        hb = h_val.astype(BF16)
        hb_ref[rs, :] = _pack_bf16_pairs(hb)
        logits_ref[:, rs] = _dot_nt(w_router_ref[...], hb) + b_router_ref[:, 0:1]

    st = None if per_chunk_state else jnp.where(t == 0, 0.0, st_scr[...])
    for c in range(CHUNKS_PER_TILE):
        rows = slice(c * CHUNK, (c + 1) * CHUNK)
        pair = slice((c // 2) * pair_rows, (c // 2 + 1) * pair_rows)
        if per_chunk_state:
            st = s_in_ref[c].reshape(hk, GLA_DV)
        qt = qt_all[rows]
        zero = jnp.zeros((), BF16)
        k_stack = jnp.where(same_head_k, jnp.concatenate([kt_all[rows].astype(BF16)] * GLA_HEADS, axis=0), zero)
        v_stack = jnp.where(same_head_v, jnp.concatenate(
            [proj_scr[rows, V0:R0].astype(BF16)] * GLA_HEADS, axis=0), zero)
        s_stack = jnp.where(same_head_v, jnp.concatenate([st.astype(BF16)] * GLA_HEADS, axis=1), zero)
        att = jnp.where(causal, _dot_nt(qt, k_stack), 0.0)
        o_scr[rows, :] = _dot(att.astype(BF16), v_stack) + _dot(qt, s_stack)
        v_chunk = jnp.where(pair_half == c % 2, proj_scr[pair, V0:R0], 0.0).astype(BF16)
        upd = jnp.concatenate(
            [_dot(kl_t[h * GLA_DK:(h + 1) * GLA_DK, pair], v_chunk[:, h * GLA_DV:(h + 1) * GLA_DV])
             for h in range(GLA_HEADS)], axis=0)
        st = st * decay_t[:, c * decay_cols:c * decay_cols + 1] + upd
        if per_chunk_state:
            s_out_ref[c] = st.reshape(GLA_HEADS, GLA_DK, GLA_DV)
        if (c + 1) % OUT_CHUNKS == 0:
            finish_rows(slice((c + 1 - OUT_CHUNKS) * CHUNK, (c + 1) * CHUNK))

    if not per_chunk_state:
        st_scr[...] = st
        s_out_ref[...] = st.reshape(GLA_HEADS, GLA_DK, GLA_DV)


def _const_spec(shape, single_buffer=False):
    nd = len(shape)
    if single_buffer:
        return pl.BlockSpec(shape, lambda *_: (0,) * nd, pipeline_mode=pl.Buffered(1))
    return pl.BlockSpec(shape, lambda *_: (0,) * nd)


def _mixer_weight_specs():
    return [
        _const_spec((None, D_MODEL, N_IN), single_buffer=True),
        _const_spec((D_MODEL, LANES)),
        _const_spec((LANES, GLA_HEADS * GLA_DK)),
        _const_spec((1, GLA_HEADS * GLA_DK)),
        _const_spec((len(POOL_WINDOWS) // 2, 2 * POOL_GROUP, 2 * POOL_GROUP)),
        _const_spec((1, POOL_WIDTH)),
        _const_spec((1, GLA_DV)),
        _const_spec((None, D_MODEL, D_MODEL), single_buffer=True),
        _const_spec((1, D_MODEL)),
        _const_spec((1, D_MODEL)),
        _const_spec((NUM_EXPERTS, D_MODEL)),
        _const_spec((NUM_EXPERTS, LANES)),
    ]


def _mixer_weights(w_in, w_pool, pool_scale, w_gate_up, b_gate, gla_norm_w, w_out, ln1_g, ln1_b,
                   w_router, b_router):
    w_glr = jnp.zeros((D_MODEL, LANES), BF16).at[:, :GATE_RANK].set(w_in[0, :, N_MAIN:].astype(BF16))
    w_gate = jnp.zeros((LANES, GLA_HEADS * GLA_DK), BF16).at[:GATE_RANK].set(w_gate_up[0].astype(BF16))
    wp = w_pool[0].astype(BF16)
    zero = jnp.zeros((POOL_GROUP, POOL_GROUP), BF16)
    w_pool_pairs = jnp.stack([jnp.block([[wp[2 * p], zero], [zero, wp[2 * p + 1]]])
                              for p in range(len(POOL_WINDOWS) // 2)])
    weights = (
        w_in, w_glr, w_gate, b_gate[0][None, :],
        w_pool_pairs, pool_scale[0][None, :], gla_norm_w[0][None, :],
        w_out, ln1_g[0][None, :], ln1_b[0][None, :],
        w_router[0].T.astype(BF16), jnp.broadcast_to(b_router[0][:, None], (NUM_EXPERTS, LANES)),
    )
    assert len(weights) == N_MIXER_WEIGHTS
    return weights


def _mixer_scratch(per_chunk_state):
    ext_rows = (CHUNKS_PER_TILE * (CHUNK + HIST_PAD)) if per_chunk_state else (TILE_TOKENS + HIST_PAD)
    return [
        pltpu.VMEM((TILE_TOKENS, N_MAIN), F32),
        pltpu.VMEM((TILE_TOKENS, GLA_HEADS * GLA_DK), F32),
        pltpu.VMEM((TILE_TOKENS, GLA_HEADS * GLA_DV), F32),
        pltpu.VMEM((ext_rows, POOL_WIDTH), F32),
        pltpu.VMEM((GLA_HEADS * GLA_DK, GLA_DV), F32),
        pltpu.VMEM((SCAN_ROWS, SCAN_ROWS), BF16),
        pltpu.VMEM((D_MODEL, N_MAIN), BF16),
        pltpu.VMEM((D_MODEL, D_MODEL), BF16),
    ]


def _mixer_out_shapes(n, bsz):
    return (
        jax.ShapeDtypeStruct((n, D_MODEL), F32),
        jax.ShapeDtypeStruct((n, HALF), jnp.int32),
        jax.ShapeDtypeStruct((NUM_EXPERTS, n), F32),
        jax.ShapeDtypeStruct((bsz, POOL_HIST, POOL_WIDTH), F32),
        jax.ShapeDtypeStruct((bsz, GLA_HEADS, GLA_DK, GLA_DV), F32),
    )


def _mixer_prompt(x, weights):
    bsz, seq, _ = x.shape
    tiles = seq // TILE_TOKENS
    n_total = bsz * seq
    return pl.pallas_call(
        functools.partial(_mixer_kernel, False, 0),
        grid=(bsz, tiles),
        in_specs=[pl.BlockSpec((None, TILE_TOKENS, D_MODEL), lambda b, t: (b, t, 0))] + _mixer_weight_specs(),
        out_specs=(
            pl.BlockSpec((TILE_TOKENS, D_MODEL), lambda b, t: (b * tiles + t, 0)),
            pl.BlockSpec((TILE_TOKENS, HALF), lambda b, t: (b * tiles + t, 0)),
            pl.BlockSpec((NUM_EXPERTS, TILE_TOKENS), lambda b, t: (0, b * tiles + t)),
            pl.BlockSpec((None, POOL_HIST, POOL_WIDTH), lambda b, t: (b, 0, 0)),
            pl.BlockSpec((None, GLA_HEADS, GLA_DK, GLA_DV), lambda b, t: (b, 0, 0, 0)),
        ),
        out_shape=_mixer_out_shapes(n_total, bsz),
        scratch_shapes=_mixer_scratch(False),
        compiler_params=pltpu.CompilerParams(
            dimension_semantics=("arbitrary", "arbitrary"), vmem_limit_bytes=VMEM_LIMIT),
        name="mixer_prompt",
    )(x, *weights)


def _mixer_sample(x, hist, state, weights):
    bsz = x.shape[0]
    tiles = bsz // CHUNKS_PER_TILE
    return pl.pallas_call(
        functools.partial(_mixer_kernel, True, PAST_LEN),
        grid=(tiles,),
        in_specs=[
            pl.BlockSpec((CHUNKS_PER_TILE, CHUNK, D_MODEL), lambda i: (i, 0, 0)),
            pl.BlockSpec((CHUNKS_PER_TILE, POOL_HIST, POOL_WIDTH), lambda i: (i, 0, 0)),
            pl.BlockSpec((CHUNKS_PER_TILE, GLA_HEADS, GLA_DK, GLA_DV), lambda i: (i, 0, 0, 0)),
        ] + _mixer_weight_specs(),
        out_specs=(
            pl.BlockSpec((TILE_TOKENS, D_MODEL), lambda i: (i, 0)),
            pl.BlockSpec((TILE_TOKENS, HALF), lambda i: (i, 0)),
            pl.BlockSpec((NUM_EXPERTS, TILE_TOKENS), lambda i: (0, i)),
            pl.BlockSpec((CHUNKS_PER_TILE, POOL_HIST, POOL_WIDTH), lambda i: (i, 0, 0)),
            pl.BlockSpec((CHUNKS_PER_TILE, GLA_HEADS, GLA_DK, GLA_DV), lambda i: (i, 0, 0, 0)),
        ),
        out_shape=_mixer_out_shapes(bsz * CHUNK, bsz),
        scratch_shapes=_mixer_scratch(True),
        compiler_params=pltpu.CompilerParams(
            dimension_semantics=("arbitrary",), vmem_limit_bytes=VMEM_LIMIT),
        name="mixer_sample",
    )(x, hist, state, *weights)


def _router_kernel(lt_ref, gates_ref, dest_ref, padend_ref, cnt_scr, base_scr, pstart_scr, before_scr,
                   topk_scr):
    phase = pl.program_id(0)
    i = pl.program_id(1)
    shape = (NUM_EXPERTS, ROUTE_TOKENS)
    row = lax.broadcasted_iota(jnp.int32, shape, 0)

    def tile_counts_of(chosen):
        return jnp.broadcast_to(jnp.sum(chosen, axis=1, keepdims=True), (NUM_EXPERTS, LANES))

    @pl.when(phase == 0)
    def _():
        @pl.when(i == 0)
        def _():
            cnt_scr[...] = jnp.zeros_like(cnt_scr)

        logits = lt_ref[...]
        idxs, vals = [], []
        chosen = jnp.zeros(shape, F32)
        for _ in range(TOP_K):
            m = jnp.max(logits, axis=0, keepdims=True)
            idx = jnp.min(jnp.where(logits == m, row, NUM_EXPERTS), axis=0, keepdims=True)
            hit = row == idx
            idxs.append(idx.astype(F32))
            vals.append(m)
            chosen = chosen + jnp.where(hit, 1.0, 0.0)
            logits = jnp.where(hit, -jnp.inf, logits)
        topk_scr[i] = jnp.concatenate(idxs + vals, axis=0)
        cnt_scr[...] += tile_counts_of(chosen)

    @pl.when(phase == 1)
    def _():
        @pl.when(i == 0)
        def _():
            blocks = jnp.floor((cnt_scr[...] + (MOE_ROWS - 1)) * (1.0 / MOE_ROWS))
            erow = lax.broadcasted_iota(jnp.int32, (NUM_EXPERTS, LANES), 0)
            lane = lax.broadcasted_iota(jnp.int32, (NUM_EXPERTS, LANES), 1)
            cum = blocks
            shift = 1
            while shift < NUM_EXPERTS:
                cum = cum + jnp.where(erow >= shift, pltpu.roll(cum, shift, 0), 0.0)
                shift *= 2
            padend_ref[...] = jnp.where(lane == 1, cnt_scr[...], cum * MOE_ROWS)
            pstart_scr[...] = (cum - blocks) * MOE_ROWS
            base_scr[...] = jnp.zeros_like(base_scr)
            ti = lax.broadcasted_iota(jnp.int32, (ROUTE_TOKENS, ROUTE_TOKENS), 0)
            tj = lax.broadcasted_iota(jnp.int32, (ROUTE_TOKENS, ROUTE_TOKENS), 1)
            before_scr[...] = jnp.where(ti < tj, 1.0, 0.0).astype(BF16)

        topk = topk_scr[i]
        sel = [row == topk[k:k + 1, :].astype(jnp.int32) for k in range(TOP_K)]
        vals = [topk[TOP_K + k:TOP_K + k + 1, :] for k in range(TOP_K)]
        chosen = sum(jnp.where(hit, 1.0, 0.0) for hit in sel)
        earlier = _dot(chosen.astype(BF16), before_scr[...])
        pos = pstart_scr[:, 0:1] + base_scr[:, 0:1] + earlier
        dest = [jnp.sum(jnp.where(hit, pos, 0.0), axis=0, keepdims=True) for hit in sel]
        dest_ref[...] = jnp.concatenate(dest, axis=0).astype(jnp.int32)
        ex = [jnp.exp(v - vals[0]) for v in vals]
        denom = ex[0] + ex[1] + ex[2] + ex[3]
        gates_ref[...] = jnp.concatenate([e / denom for e in ex], axis=0)
        base_scr[...] += tile_counts_of(chosen)


def _router(logits_t):
    n = logits_t.shape[1]
    assert n % ROUTE_TOKENS == 0
    tiles = n // ROUTE_TOKENS
    return pl.pallas_call(
        _router_kernel,
        grid=(2, tiles),
        in_specs=[pl.BlockSpec((NUM_EXPERTS, ROUTE_TOKENS), lambda p, i: (0, i))],
        out_specs=(
            pl.BlockSpec((TOP_K, ROUTE_TOKENS), lambda p, i: (0, i * p)),
            pl.BlockSpec((TOP_K, ROUTE_TOKENS), lambda p, i: (0, i * p)),
            pl.BlockSpec((NUM_EXPERTS, LANES), lambda p, i: (0, 0)),
        ),
        out_shape=(
            jax.ShapeDtypeStruct((TOP_K, n), F32),
            jax.ShapeDtypeStruct((TOP_K, n), jnp.int32),
            jax.ShapeDtypeStruct((NUM_EXPERTS, LANES), F32),
        ),
        scratch_shapes=[pltpu.VMEM((NUM_EXPERTS, LANES), F32)] * 3
        + [pltpu.VMEM((ROUTE_TOKENS, ROUTE_TOKENS), BF16),
           pltpu.VMEM((tiles, 2 * TOP_K, ROUTE_TOKENS), F32)],
        compiler_params=pltpu.CompilerParams(
            dimension_semantics=("arbitrary", "arbitrary"), vmem_limit_bytes=VMEM_LIMIT),
        name="router",
    )(logits_t)


SC_CORES = 2
SC_SUBCORES = 16
SC_WORKERS = SC_CORES * SC_SUBCORES
DISPATCH_ROWS = 64


def _dispatch(h_sources, dest_chunks, m_pad):
    shares, first_chunk = [], 0
    for src in h_sources:
        n_chunks = src.shape[0] // DISPATCH_ROWS
        assert src.shape[0] % DISPATCH_ROWS == 0 and n_chunks % SC_WORKERS == 0
        shares.append((first_chunk, n_chunks // SC_WORKERS))
        first_chunk += n_chunks
    plan = [(s, first, per, j) for s, (first, per) in enumerate(shares) for j in range(per)]
    per_worker = len(plan)
    mesh = plsc.VectorSubcoreMesh(core_axis_name="c", subcore_axis_name="s")

    @functools.partial(
        pl.kernel, mesh=mesh,
        out_type=jax.ShapeDtypeStruct((m_pad, HALF), jnp.int32),
        scratch_types=[
            pltpu.VMEM((2, TOP_K, DISPATCH_ROWS), jnp.int32),
            pltpu.VMEM((2, DISPATCH_ROWS, HALF), jnp.int32),
            pltpu.SemaphoreType.DMA((2,)),
            pltpu.SemaphoreType.DMA((2,)),
        ],
        compiler_params=pltpu.CompilerParams(use_tc_tiling_on_sc=True),
        name="dispatch",
    )
    def dispatch_kernel(*refs):
        h_hbms = refs[:len(h_sources)]
        dest_hbm, out_hbm, idx_v, rows_v, load_sems, scatter_sems = refs[len(h_sources):]
        wid = lax.axis_index("s") * SC_CORES + lax.axis_index("c")

        def loads(j):
            src, first, per, k = plan[j]
            local = wid * per + k
            slot = j % 2
            return (
                pltpu.make_async_copy(dest_hbm.at[first + local], idx_v.at[slot], load_sems.at[slot]),
                pltpu.make_async_copy(h_hbms[src].at[pl.ds(local * DISPATCH_ROWS, DISPATCH_ROWS)],
                                      rows_v.at[slot], load_sems.at[slot]),
            )

        def scatters(j):
            slot = j % 2
            return [pltpu.make_async_copy(rows_v.at[slot], out_hbm.at[idx_v.at[slot, k]],
                                          scatter_sems.at[slot]) for k in range(TOP_K)]

        for cp in loads(0):
            cp.start()
        for j in range(per_worker):
            for cp in loads(j):
                cp.wait()
            if j >= 1:
                for cp in scatters(j - 1):
                    cp.wait()
            if j + 1 < per_worker:
                for cp in loads(j + 1):
                    cp.start()
            for cp in scatters(j):
                cp.start()
        for cp in scatters(per_worker - 1):
            cp.wait()

    return dispatch_kernel(*h_sources, dest_chunks)


def _gather_expert_rows(y_sorted, dest_chunks, row_offset, n):
    n_chunks = n // DISPATCH_ROWS
    assert n_chunks % SC_WORKERS == 0 and row_offset % DISPATCH_ROWS == 0
    per_worker = n_chunks // SC_WORKERS
    chunk0 = row_offset // DISPATCH_ROWS
    mesh = plsc.VectorSubcoreMesh(core_axis_name="c", subcore_axis_name="s")

    @functools.partial(
        pl.kernel, mesh=mesh,
        out_type=jax.ShapeDtypeStruct((TOP_K, n, HALF), jnp.int32),
        scratch_types=[
            pltpu.VMEM((TOP_K, DISPATCH_ROWS), jnp.int32),
            pltpu.VMEM((2, DISPATCH_ROWS, HALF), jnp.int32),
            pltpu.SemaphoreType.DMA((2,)),
        ],
        compiler_params=pltpu.CompilerParams(use_tc_tiling_on_sc=True),
        name="gather_expert_rows",
    )
    def gather_kernel(y_hbm, dest_hbm, out_hbm, idx_v, rows_v, sems):
        wid = lax.axis_index("s") * SC_CORES + lax.axis_index("c")

        def gather(k):
            return pltpu.make_async_copy(y_hbm.at[idx_v.at[k]], rows_v.at[k % 2], sems.at[k % 2])

        @pl.loop(0, per_worker)
        def _(j):
            local = wid * per_worker + j
            pltpu.sync_copy(dest_hbm.at[chunk0 + local], idx_v)
            gather(0).start()
            for k in range(TOP_K):
                if k + 1 < TOP_K:
                    gather(k + 1).start()
                gather(k).wait()
                pltpu.sync_copy(rows_v.at[k % 2],
                                out_hbm.at[k, pl.ds(local * DISPATCH_ROWS, DISPATCH_ROWS)])

    return gather_kernel(y_sorted, dest_chunks)


def _moe_kernel(be_ref, wsel_ref, units_ref, nused_ref, x_ref, wgu_ref, bgu_ref, wd_ref, bd_ref, y_ref,
                wgu_bf, wd_bf):
    del wsel_ref
    i = pl.program_id(0)

    def ffn(rows):
        gu = _dot(_unpack_bf16_pairs(x_ref[rows, :]), wgu_bf[...]) + bgu_ref[...]
        gate = jnp.minimum(gu[:, :EXPERT_FF], SWIGLU_LIMIT)
        up = jnp.clip(gu[:, EXPERT_FF:], -SWIGLU_LIMIT, SWIGLU_LIMIT)
        hmid = gate * (1.0 / (1.0 + jnp.exp(-SWIGLU_ALPHA * gate))) * (up + 1.0)
        y = _dot(hmid.astype(BF16), wd_bf[...]) + bd_ref[...]
        y_ref[rows, :] = _pack_bf16_pairs(y.astype(BF16))

    @pl.when(i < nused_ref[0])
    def _():
        @pl.when((i == 0) | (be_ref[i] != be_ref[jnp.maximum(i - 1, 0)]))
        def _():
            wgu_bf[...] = wgu_ref[...].astype(BF16)
            wd_bf[...] = wd_ref[...].astype(BF16)

        for units in range(1, MOE_ROWS // MOE_TAIL_ROWS + 1):
            @pl.when(units_ref[i] == units)
            def _(units=units):
                ffn(slice(0, units * MOE_TAIL_ROWS))


def _moe_experts(block_expert, weight_expert, block_units, n_used, x_sorted, w_gu, b_gu, w_down, b_down):
    m_pad = x_sorted.shape[0]
    n_blocks = m_pad // MOE_ROWS

    def blk(i, be, ws, hb, nu):
        return jnp.minimum(i, nu[0] - 1)

    def expert(i, be, ws, hb, nu):
        return be[blk(i, be, ws, hb, nu)]

    def held(i, be, ws, hb, nu):
        return ws[blk(i, be, ws, hb, nu)]

    grid_spec = pltpu.PrefetchScalarGridSpec(
        num_scalar_prefetch=4,
        grid=(n_blocks,),
        in_specs=[
            pl.BlockSpec((MOE_ROWS, HALF), lambda *a: (blk(*a), 0)),
            pl.BlockSpec((None, D_MODEL, 2 * EXPERT_FF), lambda *a: (held(*a), 0, 0)),
            pl.BlockSpec((None, 1, 2 * EXPERT_FF), lambda *a: (expert(*a), 0, 0)),
            pl.BlockSpec((None, EXPERT_FF, D_MODEL), lambda *a: (held(*a), 0, 0)),
            pl.BlockSpec((None, 1, D_MODEL), lambda *a: (expert(*a), 0, 0)),
        ],
        out_specs=pl.BlockSpec((MOE_ROWS, HALF), lambda *a: (blk(*a), 0)),
        scratch_shapes=[
            pltpu.VMEM((D_MODEL, 2 * EXPERT_FF), BF16),
            pltpu.VMEM((EXPERT_FF, D_MODEL), BF16),
        ],
    )
    return pl.pallas_call(
        _moe_kernel,
        grid_spec=grid_spec,
        out_shape=jax.ShapeDtypeStruct((m_pad, HALF), jnp.int32),
        compiler_params=pltpu.CompilerParams(
            dimension_semantics=("arbitrary",), vmem_limit_bytes=VMEM_LIMIT),
        name="moe_experts",
    )(block_expert, weight_expert, block_units, n_used, x_sorted, w_gu, b_gu, w_down, b_down)


def _combine_kernel(yk_ref, gates_ref, h_ref, g_ref, b_ref, out_ref):
    pad = jnp.zeros((LANES - TOP_K, COMBINE_TOKENS), F32)
    gates = jnp.transpose(jnp.concatenate([gates_ref[...], pad], axis=0))
    lo = hi = None
    for k in range(TOP_K):
        u = lax.bitcast_convert_type(yk_ref[k], jnp.uint32)
        gk = gates[:, k:k + 1]
        lo_k = lax.bitcast_convert_type(u << 16, F32) * gk
        hi_k = lax.bitcast_convert_type(u & jnp.uint32(HI_MASK), F32) * gk
        lo = lo_k if lo is None else lo + lo_k
        hi = hi_k if hi is None else hi + hi_k
    acc = ALPHA * h_ref[...] + jnp.concatenate([lo, hi], axis=1)
    out_ref[...] = _layer_norm(acc, g_ref[...], b_ref[...])


def _combine_kernel_aliased(yk_ref, gates_ref, h_ref, g_ref, b_ref, prev_ref, out_ref):
    del prev_ref
    _combine_kernel(yk_ref, gates_ref, h_ref, g_ref, b_ref, out_ref)


def _combine(yk, gates, token_offset, h_src, h_offset, ln_g, ln_b, out_prev):
    n_seg = yk.shape[1]
    out_rows = h_src.shape[0]
    tile0 = token_offset // COMBINE_TOKENS
    out_tile0 = h_offset // COMBINE_TOKENS
    in_specs = [
        pl.BlockSpec((TOP_K, COMBINE_TOKENS, HALF), lambda i: (0, i, 0)),
        pl.BlockSpec((TOP_K, COMBINE_TOKENS), lambda i: (0, tile0 + i)),
        pl.BlockSpec((COMBINE_TOKENS, D_MODEL), lambda i: (out_tile0 + i, 0)),
        _const_spec((1, D_MODEL)),
        _const_spec((1, D_MODEL)),
    ]
    args = [yk, gates, h_src, ln_g, ln_b]
    aliases = {}
    kern = _combine_kernel
    if out_prev is not None:
        in_specs.append(pl.BlockSpec(memory_space=pl.ANY))
        aliases = {len(args): 0}
        args.append(out_prev)
        kern = _combine_kernel_aliased
    return pl.pallas_call(
        kern,
        grid=(n_seg // COMBINE_TOKENS,),
        in_specs=in_specs,
        out_specs=pl.BlockSpec((COMBINE_TOKENS, D_MODEL), lambda i: (out_tile0 + i, 0)),
        out_shape=jax.ShapeDtypeStruct((out_rows, D_MODEL), F32),
        input_output_aliases=aliases,
        compiler_params=pltpu.CompilerParams(
            dimension_semantics=("arbitrary",), vmem_limit_bytes=VMEM_LIMIT),
        name="moe_combine",
    )(*args)


def kernel(x_prompt, x_sample, state_pool, state_gla, w_in, w_pool, pool_scale, w_gate_up, b_gate,
           gla_norm_w, w_out, ln1_g, ln1_b, w_router, b_router, w_gu, b_gu, w_down, b_down,
           ln2_g, ln2_b):
    assert w_in.shape[0] == 1, "single-layer kernel"
    bp, seq, _ = x_prompt.shape
    bs, dec_seq, _ = x_sample.shape
    assert dec_seq == CHUNK and seq % TILE_TOKENS == 0 and bs % CHUNKS_PER_TILE == 0
    n_prompt = bp * seq
    n_sample = bs * dec_seq
    n_total = n_prompt + n_sample
    nk = n_total * TOP_K
    n_blocks = -(-nk // MOE_ROWS) + NUM_EXPERTS
    m_pad = n_blocks * MOE_ROWS

    weights = _mixer_weights(w_in, w_pool, pool_scale, w_gate_up, b_gate, gla_norm_w, w_out,
                             ln1_g, ln1_b, w_router, b_router)

    h_p, hb_p, logits_p, hist_p, s_p = _mixer_prompt(x_prompt, weights)
    h_s, hb_s, logits_s, hist_s, s_s = _mixer_sample(x_sample, state_pool[0], state_gla[0], weights)

    gates_t, dest_t, layout = _router(jnp.concatenate([logits_p, logits_s], axis=1))
    pad_end = layout[:, 0].astype(jnp.int32)
    counts = layout[:, 1].astype(jnp.int32)
    block_start = jnp.arange(n_blocks, dtype=jnp.int32) * MOE_ROWS
    block_expert = jnp.minimum(jnp.sum((block_start[:, None] >= pad_end[None, :]).astype(jnp.int32), axis=1),
                               NUM_EXPERTS - 1)
    n_used = (pad_end[-1:] // MOE_ROWS).astype(jnp.int32)
    is_first = jnp.concatenate([jnp.ones((1,), bool), block_expert[1:] != block_expert[:-1]])
    blocks = jnp.arange(n_blocks, dtype=jnp.int32)
    later_other = ((blocks[None, :] > blocks[:, None]) & (blocks[None, :] < n_used[0])
                   & (block_expert[None, :] != block_expert[:, None]))
    next_expert = jnp.min(jnp.where(later_other, block_expert[None, :], NUM_EXPERTS), axis=1)
    next_expert = jnp.where(next_expert == NUM_EXPERTS, block_expert, next_expert)
    weight_expert = jnp.where(is_first, block_expert, next_expert).astype(jnp.int32)
    of_expert = block_expert[:, None] == jnp.arange(NUM_EXPERTS, dtype=jnp.int32)[None, :]
    seg_end = jnp.sum(jnp.where(of_expert, (pad_end - (-counts % MOE_ROWS))[None, :], 0), axis=1)
    block_units = jnp.clip((seg_end - block_start + MOE_TAIL_ROWS - 1) // MOE_TAIL_ROWS,
                          1, MOE_ROWS // MOE_TAIL_ROWS).astype(jnp.int32)

    dest_chunks = dest_t.reshape(TOP_K, n_total // DISPATCH_ROWS, DISPATCH_ROWS).transpose(1, 0, 2)
    x_sorted = _dispatch((hb_p, hb_s), dest_chunks, m_pad)
    y_sorted = _moe_experts(block_expert, weight_expert, block_units, n_used, x_sorted, w_gu[0], b_gu[0][:, None, :],
                            w_down[0], b_down[0][:, None, :])
    gates = gates_t
    ln_g, ln_b = ln2_g[0][None, :], ln2_b[0][None, :]
    yk = _gather_expert_rows(y_sorted, dest_chunks, n_prompt, n_sample)
    y_sample = _combine(yk, gates, n_prompt, h_s, 0, ln_g, ln_b, None)
    unit = SC_WORKERS * DISPATCH_ROWS
    assert n_prompt % unit == 0
    sizes, left = [], n_prompt // unit
    while left > 0:
        size = min(left, 1 if len(sizes) < 2 else 2)
        sizes.append(size * unit)
        left -= size
    y_prompt, start = None, 0
    for seg in sizes:
        yk = _gather_expert_rows(y_sorted, dest_chunks, start, seg)
        y_prompt = _combine(yk, gates, start, h_p, start, ln_g, ln_b, y_prompt)
        start += seg
    y_prompt = y_prompt.reshape(bp, seq, D_MODEL)
    y_sample = y_sample.reshape(bs, dec_seq, D_MODEL)
    return (y_prompt, y_sample, hist_p[None], s_p[None], hist_s[None], s_s[None])
```

```python
import functools

import jax
import jax.numpy as jnp
from jax import lax
from jax.experimental import pallas as pl
from jax.experimental.pallas import tpu as pltpu
from jax.experimental.pallas import tpu_sc as plsc

F32 = jnp.float32
BF16 = jnp.bfloat16

D_MODEL = 1024
CHUNK = 64
PAST_LEN = 1024
POOL_WIDTH = 512
POOL_WINDOWS = (2, 4, 8, 16)
POOL_GROUP = 128
POOL_HIST = 15
GLA_HEADS = 4
GLA_DK = 64
GLA_DV = 128
GATE_RANK = 16
GATE_NORMALIZER = 16.0
NUM_EXPERTS = 32
TOP_K = 4
EXPERT_FF = 1024
SWIGLU_LIMIT = 7.0
SWIGLU_ALPHA = 1.702
LN_EPS = 1e-5
RMS_EPS = 1e-6
ALPHA = 2.0 ** 0.25

Q0 = POOL_WIDTH
K0 = Q0 + GLA_HEADS * GLA_DK
V0 = K0 + GLA_HEADS * GLA_DK
R0 = V0 + GLA_HEADS * GLA_DV
N_MAIN = R0 + GLA_HEADS * GLA_DV
N_IN = N_MAIN + GATE_RANK

LANES = 128
TILE_TOKENS = 512
CHUNKS_PER_TILE = TILE_TOKENS // CHUNK
HIST_PAD = 16
OUT_CHUNKS = 8
SCAN_ROWS = 256
GLA_SAFE_EXPONENT = 60.0
MOE_ROWS = 1024
MOE_TAIL_ROWS = 256
ROUTE_TOKENS = 2048
COMBINE_TOKENS = 1024
VMEM_LIMIT = 56 * 1024 * 1024


def _dot(a, b):
    return jnp.dot(a, b, preferred_element_type=F32)


def _dot_nt(a, b):
    return lax.dot_general(a, b, (((1,), (1,)), ((), ())), preferred_element_type=F32)


HALF = D_MODEL // 2
HI_MASK = 0xFFFF0000


def _pack_bf16_pairs(xb):
    lo = lax.bitcast_convert_type(xb[:, :HALF].astype(F32), jnp.uint32) >> 16
    hi = lax.bitcast_convert_type(xb[:, HALF:].astype(F32), jnp.uint32) & jnp.uint32(HI_MASK)
    return lax.bitcast_convert_type(hi | lo, jnp.int32)


def _unpack_bf16_pairs(p):
    u = lax.bitcast_convert_type(p, jnp.uint32)
    lo = lax.bitcast_convert_type(u << 16, F32)
    hi = lax.bitcast_convert_type(u & jnp.uint32(HI_MASK), F32)
    return jnp.concatenate([lo, hi], axis=1).astype(BF16)


def _layer_norm(v, g, b):
    mu = jnp.mean(v, axis=-1, keepdims=True)
    c = v - mu
    var = jnp.mean(c * c, axis=-1, keepdims=True)
    return c * lax.rsqrt(var + LN_EPS) * g + b


N_MIXER_WEIGHTS = 12


def _mixer_kernel(per_chunk_state, pos0, *refs):
    if per_chunk_state:
        (x_ref, hist_in_ref, s_in_ref, *rest) = refs
    else:
        (x_ref, *rest) = refs
        hist_in_ref = s_in_ref = None
    (w_main_ref, w_glr_ref, w_gate_ref, b_gate_ref, w_pool_ref, pscale_ref, gnorm_ref,
     w_out_ref, ln1g_ref, ln1b_ref, w_router_ref, b_router_ref, *rest) = rest
    (h_ref, hb_ref, logits_ref, hist_out_ref, s_out_ref,
     proj_scr, b_scr, o_scr, ext_scr, st_scr, tri_scr, w_main_bf, w_out_bf, o_inter_scr) = rest

    if per_chunk_state:
        t = None
        first_step = pl.program_id(0) == 0
    else:
        t = pl.program_id(1)
        first_step = (pl.program_id(0) == 0) & (t == 0)
    x = x_ref[...].reshape(TILE_TOKENS, D_MODEL)
    xb = x.astype(BF16)

    @pl.when(first_step)
    def _():
        ti = lax.broadcasted_iota(jnp.int32, (SCAN_ROWS, SCAN_ROWS), 0)
        tj = lax.broadcasted_iota(jnp.int32, (SCAN_ROWS, SCAN_ROWS), 1)
        same_chunk = (ti // CHUNK) == (tj // CHUNK)
        tri_scr[...] = jnp.where(same_chunk & (ti >= tj), 1.0, 0.0).astype(BF16)
        w_main_bf[...] = w_main_ref[:, 0:N_MAIN].astype(BF16)
        w_out_bf[...] = w_out_ref[...].astype(BF16)
        if not per_chunk_state:
            st_scr[...] = jnp.zeros_like(st_scr)
            ext_scr[0:HIST_PAD, :] = jnp.zeros((HIST_PAD, POOL_WIDTH), F32)

    glr = _dot(xb, w_glr_ref[...])
    proj_scr[:, 0:V0] = _dot(xb, w_main_bf[:, 0:V0])
    gk = _dot(glr.astype(BF16), w_gate_ref[...]) + b_gate_ref[...]
    log_sig = jnp.minimum(gk, 0.0) - jnp.log1p(jnp.exp(-jnp.abs(gk)))
    g = log_sig / GATE_NORMALIZER
    g_hi = g.astype(BF16)
    g_lo = (g - g_hi.astype(F32)).astype(BF16)
    proj_scr[:, V0:N_MAIN] = _dot(xb, w_main_bf[:, V0:N_MAIN])
    for s in range(TILE_TOKENS // SCAN_ROWS):
        rs = slice(s * SCAN_ROWS, (s + 1) * SCAN_ROWS)
        b_scr[rs, :] = _dot(tri_scr[...], g_hi[rs]) + _dot(tri_scr[...], g_lo[rs])

    if per_chunk_state:
        seg_len, seg_stride, n_seg = CHUNK, CHUNK + HIST_PAD, CHUNKS_PER_TILE
        for c in range(n_seg):
            base = c * seg_stride
            ext_scr[base:base + HIST_PAD, :] = jnp.zeros((HIST_PAD, POOL_WIDTH), F32)
            ext_scr[base + 1:base + HIST_PAD, :] = hist_in_ref[c]
            ext_scr[base + HIST_PAD:base + seg_stride, :] = proj_scr[c * CHUNK:(c + 1) * CHUNK, 0:POOL_WIDTH]
        row_pos = pos0 + lax.broadcasted_iota(jnp.int32, (seg_len, POOL_GROUP), 0)
    else:
        seg_len, seg_stride, n_seg = TILE_TOKENS, TILE_TOKENS + HIST_PAD, 1
        ext_scr[0:HIST_PAD, :] = jnp.where(t == 0, 0.0, ext_scr[0:HIST_PAD, :])
        ext_scr[HIST_PAD:seg_stride, :] = proj_scr[:, 0:POOL_WIDTH]
        row_pos = pos0 + t * TILE_TOKENS + lax.broadcasted_iota(jnp.int32, (seg_len, POOL_GROUP), 0)

    pooled_groups = []
    for gi, w in enumerate(POOL_WINDOWS):
        gs = slice(gi * POOL_GROUP, (gi + 1) * POOL_GROUP)
        cnt = jnp.minimum(row_pos + 1, w).astype(F32)
        ext = ext_scr[:, gs]
        win = ext
        shift = 1
        while shift < w:
            win = win + pltpu.roll(win, shift, 0)
            shift *= 2
        segs = []
        for s in range(n_seg):
            base = s * seg_stride + HIST_PAD
            segs.append(win[base:base + seg_len] / cnt - ext[base:base + seg_len])
        pooled = segs[0] if n_seg == 1 else jnp.concatenate(segs, axis=0)
        pooled_groups.append(pooled.astype(BF16))
    pool_cols = []
    for p in range(len(POOL_WINDOWS) // 2):
        both = jnp.concatenate(pooled_groups[2 * p:2 * p + 2], axis=1)
        pool_cols.append(_dot(both, w_pool_ref[p]))
    pool_out = jnp.concatenate(pool_cols, axis=1) * pscale_ref[...]

    if per_chunk_state:
        for c in range(n_seg):
            end = (c + 1) * seg_stride
            hist_out_ref[c] = ext_scr[end - POOL_HIST:end, :]
    else:
        hist_out_ref[...] = ext_scr[seg_stride - POOL_HIST:seg_stride, :]
        ext_scr[0:HIST_PAD, :] = ext_scr[TILE_TOKENS:seg_stride, :]

    hk = GLA_HEADS * GLA_DK
    hv = GLA_HEADS * GLA_DV
    pair_rows = 2 * CHUNK
    decay_cols = LANES // CHUNKS_PER_TILE

    def head_of(shape, dim, width):
        return lax.broadcasted_iota(jnp.int32, shape, dim) // width

    same_head_k = head_of((hk, hk), 0, CHUNK) == head_of((hk, hk), 1, GLA_DK)
    same_head_v = head_of((hk, hv), 0, CHUNK) == head_of((hk, hv), 1, GLA_DV)
    pair_half = head_of((pair_rows, hv), 0, CHUNK)
    causal = (lax.broadcasted_iota(jnp.int32, (CHUNK, hk), 0)
              >= lax.broadcasted_iota(jnp.int32, (CHUNK, hk), 1) % CHUNK)

    b_all = b_scr[...]
    b_last = [b_scr[(c + 1) * CHUNK - 1:(c + 1) * CHUNK, :] for c in range(CHUNKS_PER_TILE)]
    b_last_rows = jnp.concatenate([jnp.broadcast_to(bl, (CHUNK, hk)) for bl in b_last], axis=0)
    k_all = proj_scr[:, K0:V0]
    qt_all = (proj_scr[:, Q0:K0] * (GLA_DK ** -0.5) * jnp.exp(b_all)).astype(BF16)
    kt_all = k_all * jnp.exp(-b_all)
    kl_t = jnp.transpose(k_all * jnp.exp(b_last_rows - b_all)).astype(BF16)
    decay_t = jnp.transpose(jnp.exp(jnp.concatenate(
        [jnp.broadcast_to(bl, (decay_cols, hk)) for bl in b_last], axis=0)))

    def finish_rows(rs):
        r = proj_scr[rs, R0:N_MAIN]
        silu_r = r * (1.0 / (1.0 + jnp.exp(-r)))
        gated = []
        for h in range(GLA_HEADS):
            vs = slice(h * GLA_DV, (h + 1) * GLA_DV)
            oh = o_scr[rs, vs]
            ms = jnp.mean(oh * oh, axis=-1, keepdims=True)
            gated.append(oh * lax.rsqrt(ms + RMS_EPS) * gnorm_ref[...] * silu_r[:, vs])
        mix_in = jnp.concatenate([pool_out[rs]] + gated, axis=1).astype(BF16)
        resid = ALPHA * x[rs] + _dot(mix_in, w_out_bf[...])
        h_val = _layer_norm(resid, ln1g_ref[...], ln1b_ref[...])
        h_ref[rs, :] = h_val
        hb = h_val.astype(BF16)
        hb_ref[rs, :] = _pack_bf16_pairs(hb)
        logits_ref[:, rs] = _dot_nt(w_router_ref[...], hb) + b_router_ref[:, 0:1]

    st = None if per_chunk_state else jnp.where(t == 0, 0.0, st_scr[...])
    for c in range(CHUNKS_PER_TILE):
        rows = slice(c * CHUNK, (c + 1) * CHUNK)
        pair = slice((c // 2) * pair_rows, (c // 2 + 1) * pair_rows)
        if per_chunk_state:
            st = s_in_ref[c].reshape(hk, GLA_DV)
        qt = qt_all[rows]
        zero = jnp.zeros((), BF16)
        k_stack = jnp.where(same_head_k, jnp.concatenate([kt_all[rows].astype(BF16)] * GLA_HEADS, axis=0), zero)
        v_stack = jnp.where(same_head_v, jnp.concatenate(
            [proj_scr[rows, V0:R0].astype(BF16)] * GLA_HEADS, axis=0), zero)
        s_stack = jnp.where(same_head_v, jnp.concatenate([st.astype(BF16)] * GLA_HEADS, axis=1), zero)
        att = jnp.where(causal, _dot_nt(qt, k_stack), 0.0)
        o_inter = _dot(qt, s_stack)
        o_inter_scr[rows, :] = o_inter
        o_scr[rows, :] = _dot(att.astype(BF16), v_stack) + o_inter
        v_chunk = jnp.where(pair_half == c % 2, proj_scr[pair, V0:R0], 0.0).astype(BF16)
        upd = jnp.concatenate(
            [_dot(kl_t[h * GLA_DK:(h + 1) * GLA_DK, pair], v_chunk[:, h * GLA_DV:(h + 1) * GLA_DV])
             for h in range(GLA_HEADS)], axis=0)
        st = st * decay_t[:, c * decay_cols:c * decay_cols + 1] + upd
        if per_chunk_state:
            s_out_ref[c] = st.reshape(GLA_HEADS, GLA_DK, GLA_DV)
        if (c + 1) % OUT_CHUNKS == 0:
            finish_rows(slice((c + 1 - OUT_CHUNKS) * CHUNK, (c + 1) * CHUNK))

    if not per_chunk_state:
        st_scr[...] = st
        s_out_ref[...] = st.reshape(GLA_HEADS, GLA_DK, GLA_DV)

    @pl.when(jnp.max(-b_all) > GLA_SAFE_EXPONENT)
    def _():
        head_sum = jnp.where(head_of((hk, LANES), 0, GLA_DK)
                             == lax.broadcasted_iota(jnp.int32, (hk, LANES), 1), 1.0, 0.0).astype(BF16)
        head_expand = jnp.where(lax.broadcasted_iota(jnp.int32, (LANES, hv), 0)
                                == head_of((LANES, hv), 1, GLA_DV), 1.0, 0.0).astype(BF16)
        query_row = lax.broadcasted_iota(jnp.int32, (CHUNK, hv), 0)
        for c in range(CHUNKS_PER_TILE):
            rows = slice(c * CHUNK, (c + 1) * CHUNK)
            q_c = proj_scr[rows, Q0:K0] * (GLA_DK ** -0.5)
            b_c = b_scr[rows, :]

            def add_key(j, acc, c=c, q_c=q_c, b_c=b_c):
                key = pl.ds(c * CHUNK + j, 1)
                decay = jnp.exp(jnp.minimum(b_c - b_scr[key, :], 0.0))
                w = q_c * proj_scr[key, K0:V0] * decay
                score = _dot(w.astype(BF16), head_sum)
                score = _dot(score.astype(BF16), head_expand)
                return acc + jnp.where(query_row >= j, score, 0.0) * proj_scr[key, V0:R0]

            o_intra = lax.fori_loop(0, CHUNK, add_key, jnp.zeros((CHUNK, hv), F32))
            o_scr[rows, :] = o_intra + o_inter_scr[rows, :]
        finish_rows(slice(0, TILE_TOKENS))


def _const_spec(shape, single_buffer=False):
    nd = len(shape)
    if single_buffer:
        return pl.BlockSpec(shape, lambda *_: (0,) * nd, pipeline_mode=pl.Buffered(1))
    return pl.BlockSpec(shape, lambda *_: (0,) * nd)


def _mixer_weight_specs():
    return [
        _const_spec((None, D_MODEL, N_IN), single_buffer=True),
        _const_spec((D_MODEL, LANES)),
        _const_spec((LANES, GLA_HEADS * GLA_DK)),
        _const_spec((1, GLA_HEADS * GLA_DK)),
        _const_spec((len(POOL_WINDOWS) // 2, 2 * POOL_GROUP, 2 * POOL_GROUP)),
        _const_spec((1, POOL_WIDTH)),
        _const_spec((1, GLA_DV)),
        _const_spec((None, D_MODEL, D_MODEL), single_buffer=True),
        _const_spec((1, D_MODEL)),
        _const_spec((1, D_MODEL)),
        _const_spec((NUM_EXPERTS, D_MODEL)),
        _const_spec((NUM_EXPERTS, LANES)),
    ]


def _mixer_weights(w_in, w_pool, pool_scale, w_gate_up, b_gate, gla_norm_w, w_out, ln1_g, ln1_b,
                   w_router, b_router):
    w_glr = jnp.zeros((D_MODEL, LANES), BF16).at[:, :GATE_RANK].set(w_in[0, :, N_MAIN:].astype(BF16))
    w_gate = jnp.zeros((LANES, GLA_HEADS * GLA_DK), BF16).at[:GATE_RANK].set(w_gate_up[0].astype(BF16))
    wp = w_pool[0].astype(BF16)
    zero = jnp.zeros((POOL_GROUP, POOL_GROUP), BF16)
    w_pool_pairs = jnp.stack([jnp.block([[wp[2 * p], zero], [zero, wp[2 * p + 1]]])
                              for p in range(len(POOL_WINDOWS) // 2)])
    weights = (
        w_in, w_glr, w_gate, b_gate[0][None, :],
        w_pool_pairs, pool_scale[0][None, :], gla_norm_w[0][None, :],
        w_out, ln1_g[0][None, :], ln1_b[0][None, :],
        w_router[0].T.astype(BF16), jnp.broadcast_to(b_router[0][:, None], (NUM_EXPERTS, LANES)),
    )
    assert len(weights) == N_MIXER_WEIGHTS
    return weights


def _mixer_scratch(per_chunk_state):
    ext_rows = (CHUNKS_PER_TILE * (CHUNK + HIST_PAD)) if per_chunk_state else (TILE_TOKENS + HIST_PAD)
    return [
        pltpu.VMEM((TILE_TOKENS, N_MAIN), F32),
        pltpu.VMEM((TILE_TOKENS, GLA_HEADS * GLA_DK), F32),
        pltpu.VMEM((TILE_TOKENS, GLA_HEADS * GLA_DV), F32),
        pltpu.VMEM((ext_rows, POOL_WIDTH), F32),
        pltpu.VMEM((GLA_HEADS * GLA_DK, GLA_DV), F32),
        pltpu.VMEM((SCAN_ROWS, SCAN_ROWS), BF16),
        pltpu.VMEM((D_MODEL, N_MAIN), BF16),
        pltpu.VMEM((D_MODEL, D_MODEL), BF16),
        pltpu.VMEM((TILE_TOKENS, GLA_HEADS * GLA_DV), F32),
    ]


def _mixer_out_shapes(n, bsz):
    return (
        jax.ShapeDtypeStruct((n, D_MODEL), F32),
        jax.ShapeDtypeStruct((n, HALF), jnp.int32),
        jax.ShapeDtypeStruct((NUM_EXPERTS, n), F32),
        jax.ShapeDtypeStruct((bsz, POOL_HIST, POOL_WIDTH), F32),
        jax.ShapeDtypeStruct((bsz, GLA_HEADS, GLA_DK, GLA_DV), F32),
    )


def _mixer_prompt(x, weights):
    bsz, seq, _ = x.shape
    tiles = seq // TILE_TOKENS
    n_total = bsz * seq
    return pl.pallas_call(
        functools.partial(_mixer_kernel, False, 0),
        grid=(bsz, tiles),
        in_specs=[pl.BlockSpec((None, TILE_TOKENS, D_MODEL), lambda b, t: (b, t, 0))] + _mixer_weight_specs(),
        out_specs=(
            pl.BlockSpec((TILE_TOKENS, D_MODEL), lambda b, t: (b * tiles + t, 0)),
            pl.BlockSpec((TILE_TOKENS, HALF), lambda b, t: (b * tiles + t, 0)),
            pl.BlockSpec((NUM_EXPERTS, TILE_TOKENS), lambda b, t: (0, b * tiles + t)),
            pl.BlockSpec((None, POOL_HIST, POOL_WIDTH), lambda b, t: (b, 0, 0)),
            pl.BlockSpec((None, GLA_HEADS, GLA_DK, GLA_DV), lambda b, t: (b, 0, 0, 0)),
        ),
        out_shape=_mixer_out_shapes(n_total, bsz),
        scratch_shapes=_mixer_scratch(False),
        compiler_params=pltpu.CompilerParams(
            dimension_semantics=("arbitrary", "arbitrary"), vmem_limit_bytes=VMEM_LIMIT),
        name="mixer_prompt",
    )(x, *weights)


def _mixer_sample(x, hist, state, weights):
    bsz = x.shape[0]
    tiles = bsz // CHUNKS_PER_TILE
    return pl.pallas_call(
        functools.partial(_mixer_kernel, True, PAST_LEN),
        grid=(tiles,),
        in_specs=[
            pl.BlockSpec((CHUNKS_PER_TILE, CHUNK, D_MODEL), lambda i: (i, 0, 0)),
            pl.BlockSpec((CHUNKS_PER_TILE, POOL_HIST, POOL_WIDTH), lambda i: (i, 0, 0)),
            pl.BlockSpec((CHUNKS_PER_TILE, GLA_HEADS, GLA_DK, GLA_DV), lambda i: (i, 0, 0, 0)),
        ] + _mixer_weight_specs(),
        out_specs=(
            pl.BlockSpec((TILE_TOKENS, D_MODEL), lambda i: (i, 0)),
            pl.BlockSpec((TILE_TOKENS, HALF), lambda i: (i, 0)),
            pl.BlockSpec((NUM_EXPERTS, TILE_TOKENS), lambda i: (0, i)),
            pl.BlockSpec((CHUNKS_PER_TILE, POOL_HIST, POOL_WIDTH), lambda i: (i, 0, 0)),
            pl.BlockSpec((CHUNKS_PER_TILE, GLA_HEADS, GLA_DK, GLA_DV), lambda i: (i, 0, 0, 0)),
        ),
        out_shape=_mixer_out_shapes(bsz * CHUNK, bsz),
        scratch_shapes=_mixer_scratch(True),
        compiler_params=pltpu.CompilerParams(
            dimension_semantics=("arbitrary",), vmem_limit_bytes=VMEM_LIMIT),
        name="mixer_sample",
    )(x, hist, state, *weights)


def _router_kernel(lt_ref, gates_ref, dest_ref, padend_ref, cnt_scr, base_scr, pstart_scr, before_scr,
                   topk_scr):
    phase = pl.program_id(0)
    i = pl.program_id(1)
    shape = (NUM_EXPERTS, ROUTE_TOKENS)
    row = lax.broadcasted_iota(jnp.int32, shape, 0)

    def tile_counts_of(chosen):
        return jnp.broadcast_to(jnp.sum(chosen, axis=1, keepdims=True), (NUM_EXPERTS, LANES))

    @pl.when(phase == 0)
    def _():
        @pl.when(i == 0)
        def _():
            cnt_scr[...] = jnp.zeros_like(cnt_scr)

        logits = lt_ref[...]
        idxs, vals = [], []
        chosen = jnp.zeros(shape, F32)
        for _ in range(TOP_K):
            m = jnp.max(logits, axis=0, keepdims=True)
            idx = jnp.min(jnp.where(logits == m, row, NUM_EXPERTS), axis=0, keepdims=True)
            hit = row == idx
            idxs.append(idx.astype(F32))
            vals.append(m)
            chosen = chosen + jnp.where(hit, 1.0, 0.0)
            logits = jnp.where(hit, -jnp.inf, logits)
        topk_scr[i] = jnp.concatenate(idxs + vals, axis=0)
        cnt_scr[...] += tile_counts_of(chosen)

    @pl.when(phase == 1)
    def _():
        @pl.when(i == 0)
        def _():
            blocks = jnp.floor((cnt_scr[...] + (MOE_ROWS - 1)) * (1.0 / MOE_ROWS))
            erow = lax.broadcasted_iota(jnp.int32, (NUM_EXPERTS, LANES), 0)
            lane = lax.broadcasted_iota(jnp.int32, (NUM_EXPERTS, LANES), 1)
            cum = blocks
            shift = 1
            while shift < NUM_EXPERTS:
                cum = cum + jnp.where(erow >= shift, pltpu.roll(cum, shift, 0), 0.0)
                shift *= 2
            padend_ref[...] = jnp.where(lane == 1, cnt_scr[...], cum * MOE_ROWS)
            pstart_scr[...] = (cum - blocks) * MOE_ROWS
            base_scr[...] = jnp.zeros_like(base_scr)
            ti = lax.broadcasted_iota(jnp.int32, (ROUTE_TOKENS, ROUTE_TOKENS), 0)
            tj = lax.broadcasted_iota(jnp.int32, (ROUTE_TOKENS, ROUTE_TOKENS), 1)
            before_scr[...] = jnp.where(ti < tj, 1.0, 0.0).astype(BF16)

        topk = topk_scr[i]
        sel = [row == topk[k:k + 1, :].astype(jnp.int32) for k in range(TOP_K)]
        vals = [topk[TOP_K + k:TOP_K + k + 1, :] for k in range(TOP_K)]
        chosen = sum(jnp.where(hit, 1.0, 0.0) for hit in sel)
        earlier = _dot(chosen.astype(BF16), before_scr[...])
        pos = pstart_scr[:, 0:1] + base_scr[:, 0:1] + earlier
        dest = [jnp.sum(jnp.where(hit, pos, 0.0), axis=0, keepdims=True) for hit in sel]
        dest_ref[...] = jnp.concatenate(dest, axis=0).astype(jnp.int32)
        ex = [jnp.exp(v - vals[0]) for v in vals]
        denom = ex[0] + ex[1] + ex[2] + ex[3]
        gates_ref[...] = jnp.concatenate([e / denom for e in ex], axis=0)
        base_scr[...] += tile_counts_of(chosen)


def _router(logits_t):
    n = logits_t.shape[1]
    assert n % ROUTE_TOKENS == 0
    tiles = n // ROUTE_TOKENS
    return pl.pallas_call(
        _router_kernel,
        grid=(2, tiles),
        in_specs=[pl.BlockSpec((NUM_EXPERTS, ROUTE_TOKENS), lambda p, i: (0, i))],
        out_specs=(
            pl.BlockSpec((TOP_K, ROUTE_TOKENS), lambda p, i: (0, i * p)),
            pl.BlockSpec((TOP_K, ROUTE_TOKENS), lambda p, i: (0, i * p)),
            pl.BlockSpec((NUM_EXPERTS, LANES), lambda p, i: (0, 0)),
        ),
        out_shape=(
            jax.ShapeDtypeStruct((TOP_K, n), F32),
            jax.ShapeDtypeStruct((TOP_K, n), jnp.int32),
            jax.ShapeDtypeStruct((NUM_EXPERTS, LANES), F32),
        ),
        scratch_shapes=[pltpu.VMEM((NUM_EXPERTS, LANES), F32)] * 3
        + [pltpu.VMEM((ROUTE_TOKENS, ROUTE_TOKENS), BF16),
           pltpu.VMEM((tiles, 2 * TOP_K, ROUTE_TOKENS), F32)],
        compiler_params=pltpu.CompilerParams(
            dimension_semantics=("arbitrary", "arbitrary"), vmem_limit_bytes=VMEM_LIMIT),
        name="router",
    )(logits_t)


SC_CORES = 2
SC_SUBCORES = 16
SC_WORKERS = SC_CORES * SC_SUBCORES
DISPATCH_ROWS = 64


def _dispatch(h_sources, dest_chunks, m_pad):
    shares, first_chunk = [], 0
    for src in h_sources:
        n_chunks = src.shape[0] // DISPATCH_ROWS
        assert src.shape[0] % DISPATCH_ROWS == 0 and n_chunks % SC_WORKERS == 0
        shares.append((first_chunk, n_chunks // SC_WORKERS))
        first_chunk += n_chunks
    plan = [(s, first, per, j) for s, (first, per) in enumerate(shares) for j in range(per)]
    per_worker = len(plan)
    mesh = plsc.VectorSubcoreMesh(core_axis_name="c", subcore_axis_name="s")

    @functools.partial(
        pl.kernel, mesh=mesh,
        out_type=jax.ShapeDtypeStruct((m_pad, HALF), jnp.int32),
        scratch_types=[
            pltpu.VMEM((2, TOP_K, DISPATCH_ROWS), jnp.int32),
            pltpu.VMEM((2, DISPATCH_ROWS, HALF), jnp.int32),
            pltpu.SemaphoreType.DMA((2,)),
            pltpu.SemaphoreType.DMA((2,)),
        ],
        compiler_params=pltpu.CompilerParams(use_tc_tiling_on_sc=True),
        name="dispatch",
    )
    def dispatch_kernel(*refs):
        h_hbms = refs[:len(h_sources)]
        dest_hbm, out_hbm, idx_v, rows_v, load_sems, scatter_sems = refs[len(h_sources):]
        wid = lax.axis_index("s") * SC_CORES + lax.axis_index("c")

        def loads(j):
            src, first, per, k = plan[j]
            local = wid * per + k
            slot = j % 2
            return (
                pltpu.make_async_copy(dest_hbm.at[first + local], idx_v.at[slot], load_sems.at[slot]),
                pltpu.make_async_copy(h_hbms[src].at[pl.ds(local * DISPATCH_ROWS, DISPATCH_ROWS)],
                                      rows_v.at[slot], load_sems.at[slot]),
            )

        def scatters(j):
            slot = j % 2
            return [pltpu.make_async_copy(rows_v.at[slot], out_hbm.at[idx_v.at[slot, k]],
                                          scatter_sems.at[slot]) for k in range(TOP_K)]

        for cp in loads(0):
            cp.start()
        for j in range(per_worker):
            for cp in loads(j):
                cp.wait()
            if j >= 1:
                for cp in scatters(j - 1):
                    cp.wait()
            if j + 1 < per_worker:
                for cp in loads(j + 1):
                    cp.start()
            for cp in scatters(j):
                cp.start()
        for cp in scatters(per_worker - 1):
            cp.wait()

    return dispatch_kernel(*h_sources, dest_chunks)


def _gather_expert_rows(y_sorted, dest_chunks, row_offset, n):
    n_chunks = n // DISPATCH_ROWS
    assert n_chunks % SC_WORKERS == 0 and row_offset % DISPATCH_ROWS == 0
    per_worker = n_chunks // SC_WORKERS
    chunk0 = row_offset // DISPATCH_ROWS
    mesh = plsc.VectorSubcoreMesh(core_axis_name="c", subcore_axis_name="s")

    @functools.partial(
        pl.kernel, mesh=mesh,
        out_type=jax.ShapeDtypeStruct((TOP_K, n, HALF), jnp.int32),
        scratch_types=[
            pltpu.VMEM((TOP_K, DISPATCH_ROWS), jnp.int32),
            pltpu.VMEM((2, DISPATCH_ROWS, HALF), jnp.int32),
            pltpu.SemaphoreType.DMA((2,)),
        ],
        compiler_params=pltpu.CompilerParams(use_tc_tiling_on_sc=True),
        name="gather_expert_rows",
    )
    def gather_kernel(y_hbm, dest_hbm, out_hbm, idx_v, rows_v, sems):
        wid = lax.axis_index("s") * SC_CORES + lax.axis_index("c")

        def gather(k):
            return pltpu.make_async_copy(y_hbm.at[idx_v.at[k]], rows_v.at[k % 2], sems.at[k % 2])

        @pl.loop(0, per_worker)
        def _(j):
            local = wid * per_worker + j
            pltpu.sync_copy(dest_hbm.at[chunk0 + local], idx_v)
            gather(0).start()
            for k in range(TOP_K):
                if k + 1 < TOP_K:
                    gather(k + 1).start()
                gather(k).wait()
                pltpu.sync_copy(rows_v.at[k % 2],
                                out_hbm.at[k, pl.ds(local * DISPATCH_ROWS, DISPATCH_ROWS)])

    return gather_kernel(y_sorted, dest_chunks)


def _moe_kernel(be_ref, wsel_ref, units_ref, nused_ref, x_ref, wgu_ref, bgu_ref, wd_ref, bd_ref, y_ref,
                wgu_bf, wd_bf):
    del wsel_ref
    i = pl.program_id(0)

    def ffn(rows):
        gu = _dot(_unpack_bf16_pairs(x_ref[rows, :]), wgu_bf[...]) + bgu_ref[...]
        gate = jnp.minimum(gu[:, :EXPERT_FF], SWIGLU_LIMIT)
        up = jnp.clip(gu[:, EXPERT_FF:], -SWIGLU_LIMIT, SWIGLU_LIMIT)
        hmid = gate * (1.0 / (1.0 + jnp.exp(-SWIGLU_ALPHA * gate))) * (up + 1.0)
        y = _dot(hmid.astype(BF16), wd_bf[...]) + bd_ref[...]
        y_ref[rows, :] = _pack_bf16_pairs(y.astype(BF16))

    @pl.when(i < nused_ref[0])
    def _():
        @pl.when((i == 0) | (be_ref[i] != be_ref[jnp.maximum(i - 1, 0)]))
        def _():
            wgu_bf[...] = wgu_ref[...].astype(BF16)
            wd_bf[...] = wd_ref[...].astype(BF16)

        for units in range(1, MOE_ROWS // MOE_TAIL_ROWS + 1):
            @pl.when(units_ref[i] == units)
            def _(units=units):
                ffn(slice(0, units * MOE_TAIL_ROWS))


def _moe_experts(block_expert, weight_expert, block_units, n_used, x_sorted, w_gu, b_gu, w_down, b_down):
    m_pad = x_sorted.shape[0]
    n_blocks = m_pad // MOE_ROWS

    def blk(i, be, ws, hb, nu):
        return jnp.minimum(i, nu[0] - 1)

    def expert(i, be, ws, hb, nu):
        return be[blk(i, be, ws, hb, nu)]

    def held(i, be, ws, hb, nu):
        return ws[blk(i, be, ws, hb, nu)]

    grid_spec = pltpu.PrefetchScalarGridSpec(
        num_scalar_prefetch=4,
        grid=(n_blocks,),
        in_specs=[
            pl.BlockSpec((MOE_ROWS, HALF), lambda *a: (blk(*a), 0)),
            pl.BlockSpec((None, D_MODEL, 2 * EXPERT_FF), lambda *a: (held(*a), 0, 0)),
            pl.BlockSpec((None, 1, 2 * EXPERT_FF), lambda *a: (expert(*a), 0, 0)),
            pl.BlockSpec((None, EXPERT_FF, D_MODEL), lambda *a: (held(*a), 0, 0)),
            pl.BlockSpec((None, 1, D_MODEL), lambda *a: (expert(*a), 0, 0)),
        ],
        out_specs=pl.BlockSpec((MOE_ROWS, HALF), lambda *a: (blk(*a), 0)),
        scratch_shapes=[
            pltpu.VMEM((D_MODEL, 2 * EXPERT_FF), BF16),
            pltpu.VMEM((EXPERT_FF, D_MODEL), BF16),
        ],
    )
    return pl.pallas_call(
        _moe_kernel,
        grid_spec=grid_spec,
        out_shape=jax.ShapeDtypeStruct((m_pad, HALF), jnp.int32),
        compiler_params=pltpu.CompilerParams(
            dimension_semantics=("arbitrary",), vmem_limit_bytes=VMEM_LIMIT),
        name="moe_experts",
    )(block_expert, weight_expert, block_units, n_used, x_sorted, w_gu, b_gu, w_down, b_down)


def _combine_kernel(yk_ref, gates_ref, h_ref, g_ref, b_ref, out_ref):
    pad = jnp.zeros((LANES - TOP_K, COMBINE_TOKENS), F32)
    gates = jnp.transpose(jnp.concatenate([gates_ref[...], pad], axis=0))
    lo = hi = None
    for k in range(TOP_K):
        u = lax.bitcast_convert_type(yk_ref[k], jnp.uint32)
        gk = gates[:, k:k + 1]
        lo_k = lax.bitcast_convert_type(u << 16, F32) * gk
        hi_k = lax.bitcast_convert_type(u & jnp.uint32(HI_MASK), F32) * gk
        lo = lo_k if lo is None else lo + lo_k
        hi = hi_k if hi is None else hi + hi_k
    acc = ALPHA * h_ref[...] + jnp.concatenate([lo, hi], axis=1)
    out_ref[...] = _layer_norm(acc, g_ref[...], b_ref[...])


def _combine_kernel_aliased(yk_ref, gates_ref, h_ref, g_ref, b_ref, prev_ref, out_ref):
    del prev_ref
    _combine_kernel(yk_ref, gates_ref, h_ref, g_ref, b_ref, out_ref)


def _combine(yk, gates, token_offset, h_src, h_offset, ln_g, ln_b, out_prev):
    n_seg = yk.shape[1]
    out_rows = h_src.shape[0]
    tile0 = token_offset // COMBINE_TOKENS
    out_tile0 = h_offset // COMBINE_TOKENS
    in_specs = [
        pl.BlockSpec((TOP_K, COMBINE_TOKENS, HALF), lambda i: (0, i, 0)),
        pl.BlockSpec((TOP_K, COMBINE_TOKENS), lambda i: (0, tile0 + i)),
        pl.BlockSpec((COMBINE_TOKENS, D_MODEL), lambda i: (out_tile0 + i, 0)),
        _const_spec((1, D_MODEL)),
        _const_spec((1, D_MODEL)),
    ]
    args = [yk, gates, h_src, ln_g, ln_b]
    aliases = {}
    kern = _combine_kernel
    if out_prev is not None:
        in_specs.append(pl.BlockSpec(memory_space=pl.ANY))
        aliases = {len(args): 0}
        args.append(out_prev)
        kern = _combine_kernel_aliased
    return pl.pallas_call(
        kern,
        grid=(n_seg // COMBINE_TOKENS,),
        in_specs=in_specs,
        out_specs=pl.BlockSpec((COMBINE_TOKENS, D_MODEL), lambda i: (out_tile0 + i, 0)),
        out_shape=jax.ShapeDtypeStruct((out_rows, D_MODEL), F32),
        input_output_aliases=aliases,
        compiler_params=pltpu.CompilerParams(
            dimension_semantics=("arbitrary",), vmem_limit_bytes=VMEM_LIMIT),
        name="moe_combine",
    )(*args)


def kernel(x_prompt, x_sample, state_pool, state_gla, w_in, w_pool, pool_scale, w_gate_up, b_gate,
           gla_norm_w, w_out, ln1_g, ln1_b, w_router, b_router, w_gu, b_gu, w_down, b_down,
           ln2_g, ln2_b):
    assert w_in.shape[0] == 1, "single-layer kernel"
    bp, seq, _ = x_prompt.shape
    bs, dec_seq, _ = x_sample.shape
    assert dec_seq == CHUNK and seq % TILE_TOKENS == 0 and bs % CHUNKS_PER_TILE == 0
    n_prompt = bp * seq
    n_sample = bs * dec_seq
    n_total = n_prompt + n_sample
    nk = n_total * TOP_K
    n_blocks = -(-nk // MOE_ROWS) + NUM_EXPERTS
    m_pad = n_blocks * MOE_ROWS

    weights = _mixer_weights(w_in, w_pool, pool_scale, w_gate_up, b_gate, gla_norm_w, w_out,
                             ln1_g, ln1_b, w_router, b_router)

    h_p, hb_p, logits_p, hist_p, s_p = _mixer_prompt(x_prompt, weights)
    h_s, hb_s, logits_s, hist_s, s_s = _mixer_sample(x_sample, state_pool[0], state_gla[0], weights)

    gates_t, dest_t, layout = _router(jnp.concatenate([logits_p, logits_s], axis=1))
    pad_end = layout[:, 0].astype(jnp.int32)
    counts = layout[:, 1].astype(jnp.int32)
    block_start = jnp.arange(n_blocks, dtype=jnp.int32) * MOE_ROWS
    block_expert = jnp.minimum(jnp.sum((block_start[:, None] >= pad_end[None, :]).astype(jnp.int32), axis=1),
                               NUM_EXPERTS - 1)
    n_used = (pad_end[-1:] // MOE_ROWS).astype(jnp.int32)
    is_first = jnp.concatenate([jnp.ones((1,), bool), block_expert[1:] != block_expert[:-1]])
    blocks = jnp.arange(n_blocks, dtype=jnp.int32)
    later_other = ((blocks[None, :] > blocks[:, None]) & (blocks[None, :] < n_used[0])
                   & (block_expert[None, :] != block_expert[:, None]))
    next_expert = jnp.min(jnp.where(later_other, block_expert[None, :], NUM_EXPERTS), axis=1)
    next_expert = jnp.where(next_expert == NUM_EXPERTS, block_expert, next_expert)
    weight_expert = jnp.where(is_first, block_expert, next_expert).astype(jnp.int32)
    of_expert = block_expert[:, None] == jnp.arange(NUM_EXPERTS, dtype=jnp.int32)[None, :]
    seg_end = jnp.sum(jnp.where(of_expert, (pad_end - (-counts % MOE_ROWS))[None, :], 0), axis=1)
    block_units = jnp.clip((seg_end - block_start + MOE_TAIL_ROWS - 1) // MOE_TAIL_ROWS,
                          1, MOE_ROWS // MOE_TAIL_ROWS).astype(jnp.int32)

    dest_chunks = dest_t.reshape(TOP_K, n_total // DISPATCH_ROWS, DISPATCH_ROWS).transpose(1, 0, 2)
    x_sorted = _dispatch((hb_p, hb_s), dest_chunks, m_pad)
    y_sorted = _moe_experts(block_expert, weight_expert, block_units, n_used, x_sorted, w_gu[0], b_gu[0][:, None, :],
                            w_down[0], b_down[0][:, None, :])
    gates = gates_t
    ln_g, ln_b = ln2_g[0][None, :], ln2_b[0][None, :]
    yk = _gather_expert_rows(y_sorted, dest_chunks, n_prompt, n_sample)
    y_sample = _combine(yk, gates, n_prompt, h_s, 0, ln_g, ln_b, None)
    unit = SC_WORKERS * DISPATCH_ROWS
    assert n_prompt % unit == 0
    sizes, left = [], n_prompt // unit
    while left > 0:
        size = min(left, 1 if len(sizes) < 2 else 2)
        sizes.append(size * unit)
        left -= size
    y_prompt, start = None, 0
    for seg in sizes:
        yk = _gather_expert_rows(y_sorted, dest_chunks, start, seg)
        y_prompt = _combine(yk, gates, start, h_p, start, ln_g, ln_b, y_prompt)
        start += seg
    y_prompt = y_prompt.reshape(bp, seq, D_MODEL)
    y_sample = y_sample.reshape(bs, dec_seq, D_MODEL)
    return (y_prompt, y_sample, hist_p[None], s_p[None], hist_s[None], s_s[None])
```

```python
import functools

import jax
import jax.numpy as jnp
from jax import lax
from jax.experimental import pallas as pl
from jax.experimental.pallas import tpu as pltpu
from jax.experimental.pallas import tpu_sc as plsc

F32 = jnp.float32
BF16 = jnp.bfloat16

D_MODEL = 1024
CHUNK = 64
PAST_LEN = 1024
POOL_WIDTH = 512
POOL_WINDOWS = (2, 4, 8, 16)
POOL_GROUP = 128
POOL_HIST = 15
GLA_HEADS = 4
GLA_DK = 64
GLA_DV = 128
GATE_RANK = 16
GATE_NORMALIZER = 16.0
NUM_EXPERTS = 32
TOP_K = 4
EXPERT_FF = 1024
SWIGLU_LIMIT = 7.0
SWIGLU_ALPHA = 1.702
LN_EPS = 1e-5
RMS_EPS = 1e-6
ALPHA = 2.0 ** 0.25

Q0 = POOL_WIDTH
K0 = Q0 + GLA_HEADS * GLA_DK
V0 = K0 + GLA_HEADS * GLA_DK
R0 = V0 + GLA_HEADS * GLA_DV
N_MAIN = R0 + GLA_HEADS * GLA_DV
N_IN = N_MAIN + GATE_RANK

LANES = 128
TILE_TOKENS = 512
CHUNKS_PER_TILE = TILE_TOKENS // CHUNK
HIST_PAD = 16
OUT_CHUNKS = 8
SCAN_ROWS = 256
GLA_SAFE_EXPONENT = 60.0
MOE_ROWS = 1024
MOE_TAIL_ROWS = 256
ROUTE_TOKENS = 2048
COMBINE_TOKENS = 1024
VMEM_LIMIT = 56 * 1024 * 1024


def _dot(a, b):
    return jnp.dot(a, b, preferred_element_type=F32)


def _dot_nt(a, b):
    return lax.dot_general(a, b, (((1,), (1,)), ((), ())), preferred_element_type=F32)


HALF = D_MODEL // 2
HI_MASK = 0xFFFF0000


def _pack_bf16_pairs(xb):
    lo = lax.bitcast_convert_type(xb[:, :HALF].astype(F32), jnp.uint32) >> 16
    hi = lax.bitcast_convert_type(xb[:, HALF:].astype(F32), jnp.uint32) & jnp.uint32(HI_MASK)
    return lax.bitcast_convert_type(hi | lo, jnp.int32)


def _unpack_bf16_pairs(p):
    u = lax.bitcast_convert_type(p, jnp.uint32)
    lo = lax.bitcast_convert_type(u << 16, F32)
    hi = lax.bitcast_convert_type(u & jnp.uint32(HI_MASK), F32)
    return jnp.concatenate([lo, hi], axis=1).astype(BF16)


def _layer_norm(v, g, b):
    mu = jnp.mean(v, axis=-1, keepdims=True)
    c = v - mu
    var = jnp.mean(c * c, axis=-1, keepdims=True)
    return c * lax.rsqrt(var + LN_EPS) * g + b


N_MIXER_WEIGHTS = 12


def _mixer_kernel(per_chunk_state, pos0, *refs):
    if per_chunk_state:
        (x_ref, hist_in_ref, s_in_ref, *rest) = refs
    else:
        (x_ref, *rest) = refs
        hist_in_ref = s_in_ref = None
    (w_main_ref, w_glr_ref, w_gate_ref, b_gate_ref, w_pool_ref, pscale_ref, gnorm_ref,
     w_out_ref, ln1g_ref, ln1b_ref, w_router_ref, b_router_ref, *rest) = rest
    (h_ref, hb_ref, logits_ref, hist_out_ref, s_out_ref,
     proj_scr, b_scr, o_scr, ext_scr, st_scr, tri_scr, w_main_bf, w_out_bf, o_inter_scr,
     pool_scr) = rest

    if per_chunk_state:
        t = None
        first_step = pl.program_id(0) == 0
    else:
        t = pl.program_id(1)
        first_step = (pl.program_id(0) == 0) & (t == 0)
    def load_x():
        return x_ref[...].reshape(TILE_TOKENS, D_MODEL)

    xb = load_x().astype(BF16)

    @pl.when(first_step)
    def _():
        ti = lax.broadcasted_iota(jnp.int32, (SCAN_ROWS, SCAN_ROWS), 0)
        tj = lax.broadcasted_iota(jnp.int32, (SCAN_ROWS, SCAN_ROWS), 1)
        same_chunk = (ti // CHUNK) == (tj // CHUNK)
        tri_scr[...] = jnp.where(same_chunk & (ti >= tj), 1.0, 0.0).astype(BF16)
        w_main_bf[...] = w_main_ref[:, 0:N_MAIN].astype(BF16)
        w_out_bf[...] = w_out_ref[...].astype(BF16)
        if not per_chunk_state:
            st_scr[...] = jnp.zeros_like(st_scr)
            ext_scr[0:HIST_PAD, :] = jnp.zeros((HIST_PAD, POOL_WIDTH), F32)

    glr = _dot(xb, w_glr_ref[...])
    proj_scr[:, 0:V0] = _dot(xb, w_main_bf[:, 0:V0])
    gk = _dot(glr.astype(BF16), w_gate_ref[...]) + b_gate_ref[...]
    log_sig = jnp.minimum(gk, 0.0) - jnp.log1p(jnp.exp(-jnp.abs(gk)))
    g = log_sig / GATE_NORMALIZER
    g_hi = g.astype(BF16)
    g_lo = (g - g_hi.astype(F32)).astype(BF16)
    proj_scr[:, V0:N_MAIN] = _dot(xb, w_main_bf[:, V0:N_MAIN])
    for s in range(TILE_TOKENS // SCAN_ROWS):
        rs = slice(s * SCAN_ROWS, (s + 1) * SCAN_ROWS)
        b_scr[rs, :] = _dot(tri_scr[...], g_hi[rs]) + _dot(tri_scr[...], g_lo[rs])

    if per_chunk_state:
        seg_len, seg_stride, n_seg = CHUNK, CHUNK + HIST_PAD, CHUNKS_PER_TILE
        for c in range(n_seg):
            base = c * seg_stride
            ext_scr[base:base + HIST_PAD, :] = jnp.zeros((HIST_PAD, POOL_WIDTH), F32)
            ext_scr[base + 1:base + HIST_PAD, :] = hist_in_ref[c]
            ext_scr[base + HIST_PAD:base + seg_stride, :] = proj_scr[c * CHUNK:(c + 1) * CHUNK, 0:POOL_WIDTH]
        row_pos = pos0 + lax.broadcasted_iota(jnp.int32, (seg_len, POOL_GROUP), 0)
    else:
        seg_len, seg_stride, n_seg = TILE_TOKENS, TILE_TOKENS + HIST_PAD, 1
        ext_scr[0:HIST_PAD, :] = jnp.where(t == 0, 0.0, ext_scr[0:HIST_PAD, :])
        ext_scr[HIST_PAD:seg_stride, :] = proj_scr[:, 0:POOL_WIDTH]
        row_pos = pos0 + t * TILE_TOKENS + lax.broadcasted_iota(jnp.int32, (seg_len, POOL_GROUP), 0)

    pooled_groups = []
    for gi, w in enumerate(POOL_WINDOWS):
        gs = slice(gi * POOL_GROUP, (gi + 1) * POOL_GROUP)
        cnt = jnp.minimum(row_pos + 1, w).astype(F32)
        ext = ext_scr[:, gs]
        win = ext
        shift = 1
        while shift < w:
            win = win + pltpu.roll(win, shift, 0)
            shift *= 2
        segs = []
        for s in range(n_seg):
            base = s * seg_stride + HIST_PAD
            segs.append(win[base:base + seg_len] / cnt - ext[base:base + seg_len])
        pooled = segs[0] if n_seg == 1 else jnp.concatenate(segs, axis=0)
        pooled_groups.append(pooled.astype(BF16))
    pool_cols = []
    for p in range(len(POOL_WINDOWS) // 2):
        both = jnp.concatenate(pooled_groups[2 * p:2 * p + 2], axis=1)
        pool_cols.append(_dot(both, w_pool_ref[p]))
    pool_scr[...] = jnp.concatenate(pool_cols, axis=1) * pscale_ref[...]

    if per_chunk_state:
        for c in range(n_seg):
            end = (c + 1) * seg_stride
            hist_out_ref[c] = ext_scr[end - POOL_HIST:end, :]
    else:
        hist_out_ref[...] = ext_scr[seg_stride - POOL_HIST:seg_stride, :]
        ext_scr[0:HIST_PAD, :] = ext_scr[TILE_TOKENS:seg_stride, :]

    hk = GLA_HEADS * GLA_DK
    hv = GLA_HEADS * GLA_DV
    pair_rows = 2 * CHUNK
    decay_cols = LANES // CHUNKS_PER_TILE

    def head_of(shape, dim, width):
        return lax.broadcasted_iota(jnp.int32, shape, dim) // width

    same_head_k = head_of((hk, hk), 0, CHUNK) == head_of((hk, hk), 1, GLA_DK)
    same_head_v = head_of((hk, hv), 0, CHUNK) == head_of((hk, hv), 1, GLA_DV)
    pair_half = head_of((pair_rows, hv), 0, CHUNK)
    causal = (lax.broadcasted_iota(jnp.int32, (CHUNK, hk), 0)
              >= lax.broadcasted_iota(jnp.int32, (CHUNK, hk), 1) % CHUNK)

    b_all = b_scr[...]
    b_last = [b_scr[(c + 1) * CHUNK - 1:(c + 1) * CHUNK, :] for c in range(CHUNKS_PER_TILE)]
    b_last_rows = jnp.concatenate([jnp.broadcast_to(bl, (CHUNK, hk)) for bl in b_last], axis=0)
    k_all = proj_scr[:, K0:V0]
    qt_all = (proj_scr[:, Q0:K0] * (GLA_DK ** -0.5) * jnp.exp(b_all)).astype(BF16)
    kt_all = k_all * jnp.exp(-b_all)
    kl_t = jnp.transpose(k_all * jnp.exp(b_last_rows - b_all)).astype(BF16)
    decay_t = jnp.transpose(jnp.exp(jnp.concatenate(
        [jnp.broadcast_to(bl, (decay_cols, hk)) for bl in b_last], axis=0)))

    def finish_rows(rs):
        r = proj_scr[rs, R0:N_MAIN]
        silu_r = r * (1.0 / (1.0 + jnp.exp(-r)))
        gated = []
        for h in range(GLA_HEADS):
            vs = slice(h * GLA_DV, (h + 1) * GLA_DV)
            oh = o_scr[rs, vs]
            ms = jnp.mean(oh * oh, axis=-1, keepdims=True)
            gated.append(oh * lax.rsqrt(ms + RMS_EPS) * gnorm_ref[...] * silu_r[:, vs])
        mix_in = jnp.concatenate([pool_scr[rs, :]] + gated, axis=1).astype(BF16)
        resid = ALPHA * load_x()[rs] + _dot(mix_in, w_out_bf[...])
        h_val = _layer_norm(resid, ln1g_ref[...], ln1b_ref[...])
        h_ref[rs, :] = h_val
        hb = h_val.astype(BF16)
        hb_ref[rs, :] = _pack_bf16_pairs(hb)
        logits_ref[:, rs] = _dot_nt(w_router_ref[...], hb) + b_router_ref[:, 0:1]

    st = None if per_chunk_state else jnp.where(t == 0, 0.0, st_scr[...])
    for c in range(CHUNKS_PER_TILE):
        rows = slice(c * CHUNK, (c + 1) * CHUNK)
        pair = slice((c // 2) * pair_rows, (c // 2 + 1) * pair_rows)
        if per_chunk_state:
            st = s_in_ref[c].reshape(hk, GLA_DV)
        qt = qt_all[rows]
        zero = jnp.zeros((), BF16)
        k_stack = jnp.where(same_head_k, jnp.concatenate([kt_all[rows].astype(BF16)] * GLA_HEADS, axis=0), zero)
        v_stack = jnp.where(same_head_v, jnp.concatenate(
            [proj_scr[rows, V0:R0].astype(BF16)] * GLA_HEADS, axis=0), zero)
        s_stack = jnp.where(same_head_v, jnp.concatenate([st.astype(BF16)] * GLA_HEADS, axis=1), zero)
        att = jnp.where(causal, _dot_nt(qt, k_stack), 0.0)
        o_inter = _dot(qt, s_stack)
        o_inter_scr[rows, :] = o_inter
        o_scr[rows, :] = _dot(att.astype(BF16), v_stack) + o_inter
        v_chunk = jnp.where(pair_half == c % 2, proj_scr[pair, V0:R0], 0.0).astype(BF16)
        upd = jnp.concatenate(
            [_dot(kl_t[h * GLA_DK:(h + 1) * GLA_DK, pair], v_chunk[:, h * GLA_DV:(h + 1) * GLA_DV])
             for h in range(GLA_HEADS)], axis=0)
        st = st * decay_t[:, c * decay_cols:c * decay_cols + 1] + upd
        if per_chunk_state:
            s_out_ref[c] = st.reshape(GLA_HEADS, GLA_DK, GLA_DV)
        if (c + 1) % OUT_CHUNKS == 0:
            finish_rows(slice((c + 1 - OUT_CHUNKS) * CHUNK, (c + 1) * CHUNK))

    if not per_chunk_state:
        st_scr[...] = st
        s_out_ref[...] = st.reshape(GLA_HEADS, GLA_DK, GLA_DV)

    @pl.when(jnp.max(-b_all) > GLA_SAFE_EXPONENT)
    def _():
        head_sum = jnp.where(head_of((hk, LANES), 0, GLA_DK)
                             == lax.broadcasted_iota(jnp.int32, (hk, LANES), 1), 1.0, 0.0).astype(BF16)
        head_expand = jnp.where(lax.broadcasted_iota(jnp.int32, (LANES, hv), 0)
                                == head_of((LANES, hv), 1, GLA_DV), 1.0, 0.0).astype(BF16)
        query_row = lax.broadcasted_iota(jnp.int32, (CHUNK, hv), 0)
        for c in range(CHUNKS_PER_TILE):
            rows = slice(c * CHUNK, (c + 1) * CHUNK)
            q_c = proj_scr[rows, Q0:K0] * (GLA_DK ** -0.5)
            b_c = b_scr[rows, :]

            def add_key(j, acc, c=c, q_c=q_c, b_c=b_c):
                key = pl.ds(c * CHUNK + j, 1)
                decay = jnp.exp(jnp.minimum(b_c - b_scr[key, :], 0.0))
                w = q_c * proj_scr[key, K0:V0] * decay
                score = _dot(w.astype(BF16), head_sum)
                score = _dot(score.astype(BF16), head_expand)
                return acc + jnp.where(query_row >= j, score, 0.0) * proj_scr[key, V0:R0]

            o_intra = lax.fori_loop(0, CHUNK, add_key, jnp.zeros((CHUNK, hv), F32))
            o_scr[rows, :] = o_intra + o_inter_scr[rows, :]
        finish_rows(slice(0, TILE_TOKENS))


def _const_spec(shape, single_buffer=False):
    nd = len(shape)
    if single_buffer:
        return pl.BlockSpec(shape, lambda *_: (0,) * nd, pipeline_mode=pl.Buffered(1))
    return pl.BlockSpec(shape, lambda *_: (0,) * nd)


def _mixer_weight_specs():
    return [
        _const_spec((None, D_MODEL, N_IN), single_buffer=True),
        _const_spec((D_MODEL, LANES)),
        _const_spec((LANES, GLA_HEADS * GLA_DK)),
        _const_spec((1, GLA_HEADS * GLA_DK)),
        _const_spec((len(POOL_WINDOWS) // 2, 2 * POOL_GROUP, 2 * POOL_GROUP)),
        _const_spec((1, POOL_WIDTH)),
        _const_spec((1, GLA_DV)),
        _const_spec((None, D_MODEL, D_MODEL), single_buffer=True),
        _const_spec((1, D_MODEL)),
        _const_spec((1, D_MODEL)),
        _const_spec((NUM_EXPERTS, D_MODEL)),
        _const_spec((NUM_EXPERTS, LANES)),
    ]


def _mixer_weights(w_in, w_pool, pool_scale, w_gate_up, b_gate, gla_norm_w, w_out, ln1_g, ln1_b,
                   w_router, b_router):
    w_glr = jnp.zeros((D_MODEL, LANES), BF16).at[:, :GATE_RANK].set(w_in[0, :, N_MAIN:].astype(BF16))
    w_gate = jnp.zeros((LANES, GLA_HEADS * GLA_DK), BF16).at[:GATE_RANK].set(w_gate_up[0].astype(BF16))
    wp = w_pool[0].astype(BF16)
    zero = jnp.zeros((POOL_GROUP, POOL_GROUP), BF16)
    w_pool_pairs = jnp.stack([jnp.block([[wp[2 * p], zero], [zero, wp[2 * p + 1]]])
                              for p in range(len(POOL_WINDOWS) // 2)])
    weights = (
        w_in, w_glr, w_gate, b_gate[0][None, :],
        w_pool_pairs, pool_scale[0][None, :], gla_norm_w[0][None, :],
        w_out, ln1_g[0][None, :], ln1_b[0][None, :],
        w_router[0].T.astype(BF16), jnp.broadcast_to(b_router[0][:, None], (NUM_EXPERTS, LANES)),
    )
    assert len(weights) == N_MIXER_WEIGHTS
    return weights


def _mixer_scratch(per_chunk_state):
    ext_rows = (CHUNKS_PER_TILE * (CHUNK + HIST_PAD)) if per_chunk_state else (TILE_TOKENS + HIST_PAD)
    return [
        pltpu.VMEM((TILE_TOKENS, N_MAIN), F32),
        pltpu.VMEM((TILE_TOKENS, GLA_HEADS * GLA_DK), F32),
        pltpu.VMEM((TILE_TOKENS, GLA_HEADS * GLA_DV), F32),
        pltpu.VMEM((ext_rows, POOL_WIDTH), F32),
        pltpu.VMEM((GLA_HEADS * GLA_DK, GLA_DV), F32),
        pltpu.VMEM((SCAN_ROWS, SCAN_ROWS), BF16),
        pltpu.VMEM((D_MODEL, N_MAIN), BF16),
        pltpu.VMEM((D_MODEL, D_MODEL), BF16),
        pltpu.VMEM((TILE_TOKENS, GLA_HEADS * GLA_DV), F32),
        pltpu.VMEM((TILE_TOKENS, POOL_WIDTH), F32),
    ]


def _mixer_out_shapes(n, bsz):
    return (
        jax.ShapeDtypeStruct((n, D_MODEL), F32),
        jax.ShapeDtypeStruct((n, HALF), jnp.int32),
        jax.ShapeDtypeStruct((NUM_EXPERTS, n), F32),
        jax.ShapeDtypeStruct((bsz, POOL_HIST, POOL_WIDTH), F32),
        jax.ShapeDtypeStruct((bsz, GLA_HEADS, GLA_DK, GLA_DV), F32),
    )


def _mixer_prompt(x, weights):
    bsz, seq, _ = x.shape
    tiles = seq // TILE_TOKENS
    n_total = bsz * seq
    return pl.pallas_call(
        functools.partial(_mixer_kernel, False, 0),
        grid=(bsz, tiles),
        in_specs=[pl.BlockSpec((None, TILE_TOKENS, D_MODEL), lambda b, t: (b, t, 0))] + _mixer_weight_specs(),
        out_specs=(
            pl.BlockSpec((TILE_TOKENS, D_MODEL), lambda b, t: (b * tiles + t, 0)),
            pl.BlockSpec((TILE_TOKENS, HALF), lambda b, t: (b * tiles + t, 0)),
            pl.BlockSpec((NUM_EXPERTS, TILE_TOKENS), lambda b, t: (0, b * tiles + t)),
            pl.BlockSpec((None, POOL_HIST, POOL_WIDTH), lambda b, t: (b, 0, 0)),
            pl.BlockSpec((None, GLA_HEADS, GLA_DK, GLA_DV), lambda b, t: (b, 0, 0, 0)),
        ),
        out_shape=_mixer_out_shapes(n_total, bsz),
        scratch_shapes=_mixer_scratch(False),
        compiler_params=pltpu.CompilerParams(
            dimension_semantics=("arbitrary", "arbitrary"), vmem_limit_bytes=VMEM_LIMIT),
        name="mixer_prompt",
    )(x, *weights)


def _mixer_sample(x, hist, state, weights):
    bsz = x.shape[0]
    tiles = bsz // CHUNKS_PER_TILE
    return pl.pallas_call(
        functools.partial(_mixer_kernel, True, PAST_LEN),
        grid=(tiles,),
        in_specs=[
            pl.BlockSpec((CHUNKS_PER_TILE, CHUNK, D_MODEL), lambda i: (i, 0, 0)),
            pl.BlockSpec((CHUNKS_PER_TILE, POOL_HIST, POOL_WIDTH), lambda i: (i, 0, 0)),
            pl.BlockSpec((CHUNKS_PER_TILE, GLA_HEADS, GLA_DK, GLA_DV), lambda i: (i, 0, 0, 0)),
        ] + _mixer_weight_specs(),
        out_specs=(
            pl.BlockSpec((TILE_TOKENS, D_MODEL), lambda i: (i, 0)),
            pl.BlockSpec((TILE_TOKENS, HALF), lambda i: (i, 0)),
            pl.BlockSpec((NUM_EXPERTS, TILE_TOKENS), lambda i: (0, i)),
            pl.BlockSpec((CHUNKS_PER_TILE, POOL_HIST, POOL_WIDTH), lambda i: (i, 0, 0)),
            pl.BlockSpec((CHUNKS_PER_TILE, GLA_HEADS, GLA_DK, GLA_DV), lambda i: (i, 0, 0, 0)),
        ),
        out_shape=_mixer_out_shapes(bsz * CHUNK, bsz),
        scratch_shapes=_mixer_scratch(True),
        compiler_params=pltpu.CompilerParams(
            dimension_semantics=("arbitrary",), vmem_limit_bytes=VMEM_LIMIT),
        name="mixer_sample",
    )(x, hist, state, *weights)


def _router_kernel(lt_ref, gates_ref, dest_ref, padend_ref, cnt_scr, base_scr, pstart_scr, before_scr,
                   topk_scr):
    phase = pl.program_id(0)
    i = pl.program_id(1)
    shape = (NUM_EXPERTS, ROUTE_TOKENS)
    row = lax.broadcasted_iota(jnp.int32, shape, 0)

    def tile_counts_of(chosen):
        return jnp.broadcast_to(jnp.sum(chosen, axis=1, keepdims=True), (NUM_EXPERTS, LANES))

    @pl.when(phase == 0)
    def _():
        @pl.when(i == 0)
        def _():
            cnt_scr[...] = jnp.zeros_like(cnt_scr)

        logits = lt_ref[...]
        idxs, vals = [], []
        chosen = jnp.zeros(shape, F32)
        for _ in range(TOP_K):
            m = jnp.max(logits, axis=0, keepdims=True)
            idx = jnp.min(jnp.where(logits == m, row, NUM_EXPERTS), axis=0, keepdims=True)
            hit = row == idx
            idxs.append(idx.astype(F32))
            vals.append(m)
            chosen = chosen + jnp.where(hit, 1.0, 0.0)
            logits = jnp.where(hit, -jnp.inf, logits)
        topk_scr[i] = jnp.concatenate(idxs + vals, axis=0)
        cnt_scr[...] += tile_counts_of(chosen)

    @pl.when(phase == 1)
    def _():
        @pl.when(i == 0)
        def _():
            blocks = jnp.floor((cnt_scr[...] + (MOE_ROWS - 1)) * (1.0 / MOE_ROWS))
            erow = lax.broadcasted_iota(jnp.int32, (NUM_EXPERTS, LANES), 0)
            lane = lax.broadcasted_iota(jnp.int32, (NUM_EXPERTS, LANES), 1)
            cum = blocks
            shift = 1
            while shift < NUM_EXPERTS:
                cum = cum + jnp.where(erow >= shift, pltpu.roll(cum, shift, 0), 0.0)
                shift *= 2
            padend_ref[...] = jnp.where(lane == 1, cnt_scr[...], cum * MOE_ROWS)
            pstart_scr[...] = (cum - blocks) * MOE_ROWS
            base_scr[...] = jnp.zeros_like(base_scr)
            ti = lax.broadcasted_iota(jnp.int32, (ROUTE_TOKENS, ROUTE_TOKENS), 0)
            tj = lax.broadcasted_iota(jnp.int32, (ROUTE_TOKENS, ROUTE_TOKENS), 1)
            before_scr[...] = jnp.where(ti < tj, 1.0, 0.0).astype(BF16)

        topk = topk_scr[i]
        sel = [row == topk[k:k + 1, :].astype(jnp.int32) for k in range(TOP_K)]
        vals = [topk[TOP_K + k:TOP_K + k + 1, :] for k in range(TOP_K)]
        chosen = sum(jnp.where(hit, 1.0, 0.0) for hit in sel)
        earlier = _dot(chosen.astype(BF16), before_scr[...])
        pos = pstart_scr[:, 0:1] + base_scr[:, 0:1] + earlier
        dest = [jnp.sum(jnp.where(hit, pos, 0.0), axis=0, keepdims=True) for hit in sel]
        dest_ref[...] = jnp.concatenate(dest, axis=0).astype(jnp.int32)
        ex = [jnp.exp(v - vals[0]) for v in vals]
        denom = ex[0] + ex[1] + ex[2] + ex[3]
        gates_ref[...] = jnp.concatenate([e / denom for e in ex], axis=0)
        base_scr[...] += tile_counts_of(chosen)


def _router(logits_t):
    n = logits_t.shape[1]
    assert n % ROUTE_TOKENS == 0
    tiles = n // ROUTE_TOKENS
    return pl.pallas_call(
        _router_kernel,
        grid=(2, tiles),
        in_specs=[pl.BlockSpec((NUM_EXPERTS, ROUTE_TOKENS), lambda p, i: (0, i))],
        out_specs=(
            pl.BlockSpec((TOP_K, ROUTE_TOKENS), lambda p, i: (0, i * p)),
            pl.BlockSpec((TOP_K, ROUTE_TOKENS), lambda p, i: (0, i * p)),
            pl.BlockSpec((NUM_EXPERTS, LANES), lambda p, i: (0, 0)),
        ),
        out_shape=(
            jax.ShapeDtypeStruct((TOP_K, n), F32),
            jax.ShapeDtypeStruct((TOP_K, n), jnp.int32),
            jax.ShapeDtypeStruct((NUM_EXPERTS, LANES), F32),
        ),
        scratch_shapes=[pltpu.VMEM((NUM_EXPERTS, LANES), F32)] * 3
        + [pltpu.VMEM((ROUTE_TOKENS, ROUTE_TOKENS), BF16),
           pltpu.VMEM((tiles, 2 * TOP_K, ROUTE_TOKENS), F32)],
        compiler_params=pltpu.CompilerParams(
            dimension_semantics=("arbitrary", "arbitrary"), vmem_limit_bytes=VMEM_LIMIT),
        name="router",
    )(logits_t)


SC_CORES = 2
SC_SUBCORES = 16
SC_WORKERS = SC_CORES * SC_SUBCORES
DISPATCH_ROWS = 64


def _dispatch(h_sources, dest_chunks, m_pad):
    shares, first_chunk = [], 0
    for src in h_sources:
        n_chunks = src.shape[0] // DISPATCH_ROWS
        assert src.shape[0] % DISPATCH_ROWS == 0 and n_chunks % SC_WORKERS == 0
        shares.append((first_chunk, n_chunks // SC_WORKERS))
        first_chunk += n_chunks
    plan = [(s, first, per, j) for s, (first, per) in enumerate(shares) for j in range(per)]
    per_worker = len(plan)
    mesh = plsc.VectorSubcoreMesh(core_axis_name="c", subcore_axis_name="s")

    @functools.partial(
        pl.kernel, mesh=mesh,
        out_type=jax.ShapeDtypeStruct((m_pad, HALF), jnp.int32),
        scratch_types=[
            pltpu.VMEM((2, TOP_K, DISPATCH_ROWS), jnp.int32),
            pltpu.VMEM((2, DISPATCH_ROWS, HALF), jnp.int32),
            pltpu.SemaphoreType.DMA((2,)),
            pltpu.SemaphoreType.DMA((2,)),
        ],
        compiler_params=pltpu.CompilerParams(use_tc_tiling_on_sc=True),
        name="dispatch",
    )
    def dispatch_kernel(*refs):
        h_hbms = refs[:len(h_sources)]
        dest_hbm, out_hbm, idx_v, rows_v, load_sems, scatter_sems = refs[len(h_sources):]
        wid = lax.axis_index("s") * SC_CORES + lax.axis_index("c")

        def loads(j):
            src, first, per, k = plan[j]
            local = wid * per + k
            slot = j % 2
            return (
                pltpu.make_async_copy(dest_hbm.at[first + local], idx_v.at[slot], load_sems.at[slot]),
                pltpu.make_async_copy(h_hbms[src].at[pl.ds(local * DISPATCH_ROWS, DISPATCH_ROWS)],
                                      rows_v.at[slot], load_sems.at[slot]),
            )

        def scatters(j):
            slot = j % 2
            return [pltpu.make_async_copy(rows_v.at[slot], out_hbm.at[idx_v.at[slot, k]],
                                          scatter_sems.at[slot]) for k in range(TOP_K)]

        for cp in loads(0):
            cp.start()
        for j in range(per_worker):
            for cp in loads(j):
                cp.wait()
            if j >= 1:
                for cp in scatters(j - 1):
                    cp.wait()
            if j + 1 < per_worker:
                for cp in loads(j + 1):
                    cp.start()
            for cp in scatters(j):
                cp.start()
        for cp in scatters(per_worker - 1):
            cp.wait()

    return dispatch_kernel(*h_sources, dest_chunks)


def _gather_expert_rows(y_sorted, dest_chunks, row_offset, n):
    n_chunks = n // DISPATCH_ROWS
    assert n_chunks % SC_WORKERS == 0 and row_offset % DISPATCH_ROWS == 0
    per_worker = n_chunks // SC_WORKERS
    chunk0 = row_offset // DISPATCH_ROWS
    mesh = plsc.VectorSubcoreMesh(core_axis_name="c", subcore_axis_name="s")

    @functools.partial(
        pl.kernel, mesh=mesh,
        out_type=jax.ShapeDtypeStruct((TOP_K, n, HALF), jnp.int32),
        scratch_types=[
            pltpu.VMEM((TOP_K, DISPATCH_ROWS), jnp.int32),
            pltpu.VMEM((2, DISPATCH_ROWS, HALF), jnp.int32),
            pltpu.SemaphoreType.DMA((2,)),
        ],
        compiler_params=pltpu.CompilerParams(use_tc_tiling_on_sc=True),
        name="gather_expert_rows",
    )
    def gather_kernel(y_hbm, dest_hbm, out_hbm, idx_v, rows_v, sems):
        wid = lax.axis_index("s") * SC_CORES + lax.axis_index("c")

        def gather(k):
            return pltpu.make_async_copy(y_hbm.at[idx_v.at[k]], rows_v.at[k % 2], sems.at[k % 2])

        @pl.loop(0, per_worker)
        def _(j):
            local = wid * per_worker + j
            pltpu.sync_copy(dest_hbm.at[chunk0 + local], idx_v)
            gather(0).start()
            for k in range(TOP_K):
                if k + 1 < TOP_K:
                    gather(k + 1).start()
                gather(k).wait()
                pltpu.sync_copy(rows_v.at[k % 2],
                                out_hbm.at[k, pl.ds(local * DISPATCH_ROWS, DISPATCH_ROWS)])

    return gather_kernel(y_sorted, dest_chunks)


def _moe_kernel(be_ref, wsel_ref, units_ref, nused_ref, x_ref, wgu_ref, bgu_ref, wd_ref, bd_ref, y_ref,
                wgu_bf, wd_bf):
    del wsel_ref
    i = pl.program_id(0)

    def ffn(rows):
        gu = _dot(_unpack_bf16_pairs(x_ref[rows, :]), wgu_bf[...]) + bgu_ref[...]
        gate = jnp.minimum(gu[:, :EXPERT_FF], SWIGLU_LIMIT)
        up = jnp.clip(gu[:, EXPERT_FF:], -SWIGLU_LIMIT, SWIGLU_LIMIT)
        hmid = gate * (1.0 / (1.0 + jnp.exp(-SWIGLU_ALPHA * gate))) * (up + 1.0)
        y = _dot(hmid.astype(BF16), wd_bf[...]) + bd_ref[...]
        y_ref[rows, :] = _pack_bf16_pairs(y.astype(BF16))

    @pl.when(i < nused_ref[0])
    def _():
        @pl.when((i == 0) | (be_ref[i] != be_ref[jnp.maximum(i - 1, 0)]))
        def _():
            wgu_bf[...] = wgu_ref[...].astype(BF16)
            wd_bf[...] = wd_ref[...].astype(BF16)

        for units in range(1, MOE_ROWS // MOE_TAIL_ROWS + 1):
            @pl.when(units_ref[i] == units)
            def _(units=units):
                ffn(slice(0, units * MOE_TAIL_ROWS))


def _moe_experts(block_expert, weight_expert, block_units, n_used, x_sorted, w_gu, b_gu, w_down, b_down):
    m_pad = x_sorted.shape[0]
    n_blocks = m_pad // MOE_ROWS

    def blk(i, be, ws, hb, nu):
        return jnp.minimum(i, nu[0] - 1)

    def expert(i, be, ws, hb, nu):
        return be[blk(i, be, ws, hb, nu)]

    def held(i, be, ws, hb, nu):
        return ws[blk(i, be, ws, hb, nu)]

    grid_spec = pltpu.PrefetchScalarGridSpec(
        num_scalar_prefetch=4,
        grid=(n_blocks,),
        in_specs=[
            pl.BlockSpec((MOE_ROWS, HALF), lambda *a: (blk(*a), 0)),
            pl.BlockSpec((None, D_MODEL, 2 * EXPERT_FF), lambda *a: (held(*a), 0, 0)),
            pl.BlockSpec((None, 1, 2 * EXPERT_FF), lambda *a: (expert(*a), 0, 0)),
            pl.BlockSpec((None, EXPERT_FF, D_MODEL), lambda *a: (held(*a), 0, 0)),
            pl.BlockSpec((None, 1, D_MODEL), lambda *a: (expert(*a), 0, 0)),
        ],
        out_specs=pl.BlockSpec((MOE_ROWS, HALF), lambda *a: (blk(*a), 0)),
        scratch_shapes=[
            pltpu.VMEM((D_MODEL, 2 * EXPERT_FF), BF16),
            pltpu.VMEM((EXPERT_FF, D_MODEL), BF16),
        ],
    )
    return pl.pallas_call(
        _moe_kernel,
        grid_spec=grid_spec,
        out_shape=jax.ShapeDtypeStruct((m_pad, HALF), jnp.int32),
        compiler_params=pltpu.CompilerParams(
            dimension_semantics=("arbitrary",), vmem_limit_bytes=VMEM_LIMIT),
        name="moe_experts",
    )(block_expert, weight_expert, block_units, n_used, x_sorted, w_gu, b_gu, w_down, b_down)


def _combine_kernel(yk_ref, gates_ref, h_ref, g_ref, b_ref, out_ref):
    pad = jnp.zeros((LANES - TOP_K, COMBINE_TOKENS), F32)
    gates = jnp.transpose(jnp.concatenate([gates_ref[...], pad], axis=0))
    lo = hi = None
    for k in range(TOP_K):
        u = lax.bitcast_convert_type(yk_ref[k], jnp.uint32)
        gk = gates[:, k:k + 1]
        lo_k = lax.bitcast_convert_type(u << 16, F32) * gk
        hi_k = lax.bitcast_convert_type(u & jnp.uint32(HI_MASK), F32) * gk
        lo = lo_k if lo is None else lo + lo_k
        hi = hi_k if hi is None else hi + hi_k
    acc = ALPHA * h_ref[...] + jnp.concatenate([lo, hi], axis=1)
    out_ref[...] = _layer_norm(acc, g_ref[...], b_ref[...])


def _combine_kernel_aliased(yk_ref, gates_ref, h_ref, g_ref, b_ref, prev_ref, out_ref):
    del prev_ref
    _combine_kernel(yk_ref, gates_ref, h_ref, g_ref, b_ref, out_ref)


def _combine(yk, gates, token_offset, h_src, h_offset, ln_g, ln_b, out_prev):
    n_seg = yk.shape[1]
    out_rows = h_src.shape[0]
    tile0 = token_offset // COMBINE_TOKENS
    out_tile0 = h_offset // COMBINE_TOKENS
    in_specs = [
        pl.BlockSpec((TOP_K, COMBINE_TOKENS, HALF), lambda i: (0, i, 0)),
        pl.BlockSpec((TOP_K, COMBINE_TOKENS), lambda i: (0, tile0 + i)),
        pl.BlockSpec((COMBINE_TOKENS, D_MODEL), lambda i: (out_tile0 + i, 0)),
        _const_spec((1, D_MODEL)),
        _const_spec((1, D_MODEL)),
    ]
    args = [yk, gates, h_src, ln_g, ln_b]
    aliases = {}
    kern = _combine_kernel
    if out_prev is not None:
        in_specs.append(pl.BlockSpec(memory_space=pl.ANY))
        aliases = {len(args): 0}
        args.append(out_prev)
        kern = _combine_kernel_aliased
    return pl.pallas_call(
        kern,
        grid=(n_seg // COMBINE_TOKENS,),
        in_specs=in_specs,
        out_specs=pl.BlockSpec((COMBINE_TOKENS, D_MODEL), lambda i: (out_tile0 + i, 0)),
        out_shape=jax.ShapeDtypeStruct((out_rows, D_MODEL), F32),
        input_output_aliases=aliases,
        compiler_params=pltpu.CompilerParams(
            dimension_semantics=("arbitrary",), vmem_limit_bytes=VMEM_LIMIT),
        name="moe_combine",
    )(*args)


def kernel(x_prompt, x_sample, state_pool, state_gla, w_in, w_pool, pool_scale, w_gate_up, b_gate,
           gla_norm_w, w_out, ln1_g, ln1_b, w_router, b_router, w_gu, b_gu, w_down, b_down,
           ln2_g, ln2_b):
    assert w_in.shape[0] == 1, "single-layer kernel"
    bp, seq, _ = x_prompt.shape
    bs, dec_seq, _ = x_sample.shape
    assert dec_seq == CHUNK and seq % TILE_TOKENS == 0 and bs % CHUNKS_PER_TILE == 0
    n_prompt = bp * seq
    n_sample = bs * dec_seq
    n_total = n_prompt + n_sample
    nk = n_total * TOP_K
    n_blocks = -(-nk // MOE_ROWS) + NUM_EXPERTS
    m_pad = n_blocks * MOE_ROWS

    weights = _mixer_weights(w_in, w_pool, pool_scale, w_gate_up, b_gate, gla_norm_w, w_out,
                             ln1_g, ln1_b, w_router, b_router)

    h_p, hb_p, logits_p, hist_p, s_p = _mixer_prompt(x_prompt, weights)
    h_s, hb_s, logits_s, hist_s, s_s = _mixer_sample(x_sample, state_pool[0], state_gla[0], weights)

    gates_t, dest_t, layout = _router(jnp.concatenate([logits_p, logits_s], axis=1))
    pad_end = layout[:, 0].astype(jnp.int32)
    counts = layout[:, 1].astype(jnp.int32)
    block_start = jnp.arange(n_blocks, dtype=jnp.int32) * MOE_ROWS
    block_expert = jnp.minimum(jnp.sum((block_start[:, None] >= pad_end[None, :]).astype(jnp.int32), axis=1),
                               NUM_EXPERTS - 1)
    n_used = (pad_end[-1:] // MOE_ROWS).astype(jnp.int32)
    is_first = jnp.concatenate([jnp.ones((1,), bool), block_expert[1:] != block_expert[:-1]])
    blocks = jnp.arange(n_blocks, dtype=jnp.int32)
    later_other = ((blocks[None, :] > blocks[:, None]) & (blocks[None, :] < n_used[0])
                   & (block_expert[None, :] != block_expert[:, None]))
    next_expert = jnp.min(jnp.where(later_other, block_expert[None, :], NUM_EXPERTS), axis=1)
    next_expert = jnp.where(next_expert == NUM_EXPERTS, block_expert, next_expert)
    weight_expert = jnp.where(is_first, block_expert, next_expert).astype(jnp.int32)
    of_expert = block_expert[:, None] == jnp.arange(NUM_EXPERTS, dtype=jnp.int32)[None, :]
    seg_end = jnp.sum(jnp.where(of_expert, (pad_end - (-counts % MOE_ROWS))[None, :], 0), axis=1)
    block_units = jnp.clip((seg_end - block_start + MOE_TAIL_ROWS - 1) // MOE_TAIL_ROWS,
                          1, MOE_ROWS // MOE_TAIL_ROWS).astype(jnp.int32)

    dest_chunks = dest_t.reshape(TOP_K, n_total // DISPATCH_ROWS, DISPATCH_ROWS).transpose(1, 0, 2)
    x_sorted = _dispatch((hb_p, hb_s), dest_chunks, m_pad)
    y_sorted = _moe_experts(block_expert, weight_expert, block_units, n_used, x_sorted, w_gu[0], b_gu[0][:, None, :],
                            w_down[0], b_down[0][:, None, :])
    gates = gates_t
    ln_g, ln_b = ln2_g[0][None, :], ln2_b[0][None, :]
    yk = _gather_expert_rows(y_sorted, dest_chunks, n_prompt, n_sample)
    y_sample = _combine(yk, gates, n_prompt, h_s, 0, ln_g, ln_b, None)
    unit = SC_WORKERS * DISPATCH_ROWS
    assert n_prompt % unit == 0
    sizes, left = [], n_prompt // unit
    while left > 0:
        size = min(left, 1 if len(sizes) < 2 else 2)
        sizes.append(size * unit)
        left -= size
    y_prompt, start = None, 0
    for seg in sizes:
        yk = _gather_expert_rows(y_sorted, dest_chunks, start, seg)
        y_prompt = _combine(yk, gates, start, h_p, start, ln_g, ln_b, y_prompt)
        start += seg
    y_prompt = y_prompt.reshape(bp, seq, D_MODEL)
    y_sample = y_sample.reshape(bs, dec_seq, D_MODEL)
    return (y_prompt, y_sample, hist_p[None], s_p[None], hist_s[None], s_s[None])
```

```python
import functools

import jax
import jax.numpy as jnp
from jax import lax
from jax.experimental import pallas as pl
from jax.experimental.pallas import tpu as pltpu
from jax.experimental.pallas import tpu_sc as plsc

F32 = jnp.float32
BF16 = jnp.bfloat16

D_MODEL = 1024
CHUNK = 64
PAST_LEN = 1024
POOL_WIDTH = 512
POOL_WINDOWS = (2, 4, 8, 16)
POOL_GROUP = 128
POOL_HIST = 15
GLA_HEADS = 4
GLA_DK = 64
GLA_DV = 128
GATE_RANK = 16
GATE_NORMALIZER = 16.0
NUM_EXPERTS = 32
TOP_K = 4
EXPERT_FF = 1024
SWIGLU_LIMIT = 7.0
SWIGLU_ALPHA = 1.702
LN_EPS = 1e-5
RMS_EPS = 1e-6
ALPHA = 2.0 ** 0.25

Q0 = POOL_WIDTH
K0 = Q0 + GLA_HEADS * GLA_DK
V0 = K0 + GLA_HEADS * GLA_DK
R0 = V0 + GLA_HEADS * GLA_DV
N_MAIN = R0 + GLA_HEADS * GLA_DV
N_IN = N_MAIN + GATE_RANK

LANES = 128
TILE_TOKENS = 512
CHUNKS_PER_TILE = TILE_TOKENS // CHUNK
HIST_PAD = 16
SCAN_ROWS = 256
GLA_SAFE_EXPONENT = 60.0
MOE_ROWS = 1024
MOE_TAIL_ROWS = 256
ROUTE_TOKENS = 2048
COMBINE_TOKENS = 1024
VMEM_LIMIT = 56 * 1024 * 1024


def _dot(a, b):
    return jnp.dot(a, b, preferred_element_type=F32)


def _dot_nt(a, b):
    return lax.dot_general(a, b, (((1,), (1,)), ((), ())), preferred_element_type=F32)


HALF = D_MODEL // 2
HI_MASK = 0xFFFF0000


def _pack_bf16_pairs(xb):
    lo = lax.bitcast_convert_type(xb[:, :HALF].astype(F32), jnp.uint32) >> 16
    hi = lax.bitcast_convert_type(xb[:, HALF:].astype(F32), jnp.uint32) & jnp.uint32(HI_MASK)
    return lax.bitcast_convert_type(hi | lo, jnp.int32)


def _unpack_bf16_pairs(p):
    u = lax.bitcast_convert_type(p, jnp.uint32)
    lo = lax.bitcast_convert_type(u << 16, F32)
    hi = lax.bitcast_convert_type(u & jnp.uint32(HI_MASK), F32)
    return jnp.concatenate([lo, hi], axis=1).astype(BF16)


def _layer_norm(v, g, b):
    mu = jnp.mean(v, axis=-1, keepdims=True)
    c = v - mu
    var = jnp.mean(c * c, axis=-1, keepdims=True)
    return c * lax.rsqrt(var + LN_EPS) * g + b


N_MIXER_WEIGHTS = 12


def _mixer_kernel(per_chunk_state, pos0, *refs):
    if per_chunk_state:
        (x_ref, hist_in_ref, s_in_ref, *rest) = refs
    else:
        (x_ref, *rest) = refs
        hist_in_ref = s_in_ref = None
    (w_main_ref, w_glr_ref, w_gate_ref, b_gate_ref, w_pool_ref, pscale_ref, gnorm_ref,
     w_out_ref, ln1g_ref, ln1b_ref, w_router_ref, b_router_ref, *rest) = rest
    (h_ref, hb_ref, logits_ref, hist_out_ref, s_out_ref,
     proj_scr, b_scr, o_scr, ext_scr, st_scr, tri_scr, w_main_bf, w_out_bf, o_inter_scr,
     pool_scr) = rest

    if per_chunk_state:
        t = None
        first_step = pl.program_id(0) == 0
    else:
        t = pl.program_id(1)
        first_step = (pl.program_id(0) == 0) & (t == 0)
    def load_x():
        return x_ref[...].reshape(TILE_TOKENS, D_MODEL)

    xb = load_x().astype(BF16)

    @pl.when(first_step)
    def _():
        ti = lax.broadcasted_iota(jnp.int32, (SCAN_ROWS, SCAN_ROWS), 0)
        tj = lax.broadcasted_iota(jnp.int32, (SCAN_ROWS, SCAN_ROWS), 1)
        same_chunk = (ti // CHUNK) == (tj // CHUNK)
        tri_scr[...] = jnp.where(same_chunk & (ti >= tj), 1.0, 0.0).astype(BF16)
        w_main_bf[...] = w_main_ref[:, 0:N_MAIN].astype(BF16)
        w_out_bf[...] = w_out_ref[...].astype(BF16)
        if not per_chunk_state:
            st_scr[...] = jnp.zeros_like(st_scr)
            ext_scr[0:HIST_PAD, :] = jnp.zeros((HIST_PAD, POOL_WIDTH), F32)

    glr = _dot(xb, w_glr_ref[...])
    proj_scr[:, 0:V0] = _dot(xb, w_main_bf[:, 0:V0])
    gk = _dot(glr.astype(BF16), w_gate_ref[...]) + b_gate_ref[...]
    log_sig = jnp.minimum(gk, 0.0) - jnp.log1p(jnp.exp(-jnp.abs(gk)))
    g = log_sig / GATE_NORMALIZER
    g_hi = g.astype(BF16)
    g_lo = (g - g_hi.astype(F32)).astype(BF16)
    proj_scr[:, V0:N_MAIN] = _dot(xb, w_main_bf[:, V0:N_MAIN])
    for s in range(TILE_TOKENS // SCAN_ROWS):
        rs = slice(s * SCAN_ROWS, (s + 1) * SCAN_ROWS)
        b_scr[rs, :] = _dot(tri_scr[...], g_hi[rs]) + _dot(tri_scr[...], g_lo[rs])

    if per_chunk_state:
        seg_len, seg_stride, n_seg = CHUNK, CHUNK + HIST_PAD, CHUNKS_PER_TILE
        for c in range(n_seg):
            base = c * seg_stride
            ext_scr[base:base + HIST_PAD, :] = jnp.zeros((HIST_PAD, POOL_WIDTH), F32)
            ext_scr[base + 1:base + HIST_PAD, :] = hist_in_ref[c]
            ext_scr[base + HIST_PAD:base + seg_stride, :] = proj_scr[c * CHUNK:(c + 1) * CHUNK, 0:POOL_WIDTH]
        row_pos = pos0 + lax.broadcasted_iota(jnp.int32, (seg_len, POOL_GROUP), 0)
    else:
        seg_len, seg_stride, n_seg = TILE_TOKENS, TILE_TOKENS + HIST_PAD, 1
        ext_scr[0:HIST_PAD, :] = jnp.where(t == 0, 0.0, ext_scr[0:HIST_PAD, :])
        ext_scr[HIST_PAD:seg_stride, :] = proj_scr[:, 0:POOL_WIDTH]
        row_pos = pos0 + t * TILE_TOKENS + lax.broadcasted_iota(jnp.int32, (seg_len, POOL_GROUP), 0)

    pooled_groups = []
    for gi, w in enumerate(POOL_WINDOWS):
        gs = slice(gi * POOL_GROUP, (gi + 1) * POOL_GROUP)
        cnt = jnp.minimum(row_pos + 1, w).astype(F32)
        ext = ext_scr[:, gs]
        win = ext
        shift = 1
        while shift < w:
            win = win + pltpu.roll(win, shift, 0)
            shift *= 2
        segs = []
        for s in range(n_seg):
            base = s * seg_stride + HIST_PAD
            segs.append(win[base:base + seg_len] / cnt - ext[base:base + seg_len])
        pooled = segs[0] if n_seg == 1 else jnp.concatenate(segs, axis=0)
        pooled_groups.append(pooled.astype(BF16))
    pool_cols = []
    for p in range(len(POOL_WINDOWS) // 2):
        both = jnp.concatenate(pooled_groups[2 * p:2 * p + 2], axis=1)
        pool_cols.append(_dot(both, w_pool_ref[p]))
    pool_scr[...] = jnp.concatenate(pool_cols, axis=1) * pscale_ref[...]

    if per_chunk_state:
        for c in range(n_seg):
            end = (c + 1) * seg_stride
            hist_out_ref[c] = ext_scr[end - POOL_HIST:end, :]
    else:
        hist_out_ref[...] = ext_scr[seg_stride - POOL_HIST:seg_stride, :]
        ext_scr[0:HIST_PAD, :] = ext_scr[TILE_TOKENS:seg_stride, :]

    hk = GLA_HEADS * GLA_DK
    hv = GLA_HEADS * GLA_DV
    pair_rows = 2 * CHUNK
    decay_cols = LANES // CHUNKS_PER_TILE

    def head_of(shape, dim, width):
        return lax.broadcasted_iota(jnp.int32, shape, dim) // width

    same_head_k = head_of((hk, hk), 0, CHUNK) == head_of((hk, hk), 1, GLA_DK)
    same_head_v = head_of((hk, hv), 0, CHUNK) == head_of((hk, hv), 1, GLA_DV)
    pair_half = head_of((pair_rows, hv), 0, CHUNK)
    causal = (lax.broadcasted_iota(jnp.int32, (CHUNK, hk), 0)
              >= lax.broadcasted_iota(jnp.int32, (CHUNK, hk), 1) % CHUNK)

    b_all = b_scr[...]
    b_last = [b_scr[(c + 1) * CHUNK - 1:(c + 1) * CHUNK, :] for c in range(CHUNKS_PER_TILE)]
    b_last_rows = jnp.concatenate([jnp.broadcast_to(bl, (CHUNK, hk)) for bl in b_last], axis=0)
    k_all = proj_scr[:, K0:V0]
    qt_all = (proj_scr[:, Q0:K0] * (GLA_DK ** -0.5) * jnp.exp(b_all)).astype(BF16)
    kt_all = k_all * jnp.exp(-b_all)
    kl_t = jnp.transpose(k_all * jnp.exp(b_last_rows - b_all)).astype(BF16)
    decay_t = jnp.transpose(jnp.exp(jnp.concatenate(
        [jnp.broadcast_to(bl, (decay_cols, hk)) for bl in b_last], axis=0)))

    def finish_rows(rs):
        r = proj_scr[rs, R0:N_MAIN]
        silu_r = r * (1.0 / (1.0 + jnp.exp(-r)))
        gated = []
        for h in range(GLA_HEADS):
            vs = slice(h * GLA_DV, (h + 1) * GLA_DV)
            oh = o_scr[rs, vs]
            ms = jnp.mean(oh * oh, axis=-1, keepdims=True)
            gated.append(oh * lax.rsqrt(ms + RMS_EPS) * gnorm_ref[...] * silu_r[:, vs])
        mix_in = jnp.concatenate([pool_scr[rs, :]] + gated, axis=1).astype(BF16)
        resid = ALPHA * load_x()[rs] + _dot(mix_in, w_out_bf[...])
        h_val = _layer_norm(resid, ln1g_ref[...], ln1b_ref[...])
        h_ref[rs, :] = h_val
        hb = h_val.astype(BF16)
        hb_ref[rs, :] = _pack_bf16_pairs(hb)
        logits_ref[:, rs] = _dot_nt(w_router_ref[...], hb) + b_router_ref[:, 0:1]

    st = None if per_chunk_state else jnp.where(t == 0, 0.0, st_scr[...])
    for c in range(CHUNKS_PER_TILE):
        rows = slice(c * CHUNK, (c + 1) * CHUNK)
        pair = slice((c // 2) * pair_rows, (c // 2 + 1) * pair_rows)
        if per_chunk_state:
            st = s_in_ref[c].reshape(hk, GLA_DV)
        qt = qt_all[rows]
        zero = jnp.zeros((), BF16)
        k_stack = jnp.where(same_head_k, jnp.concatenate([kt_all[rows].astype(BF16)] * GLA_HEADS, axis=0), zero)
        v_stack = jnp.where(same_head_v, jnp.concatenate(
            [proj_scr[rows, V0:R0].astype(BF16)] * GLA_HEADS, axis=0), zero)
        s_stack = jnp.where(same_head_v, jnp.concatenate([st.astype(BF16)] * GLA_HEADS, axis=1), zero)
        att = jnp.where(causal, _dot_nt(qt, k_stack), 0.0)
        o_inter = _dot(qt, s_stack)
        o_inter_scr[rows, :] = o_inter
        o_scr[rows, :] = _dot(att.astype(BF16), v_stack) + o_inter
        v_chunk = jnp.where(pair_half == c % 2, proj_scr[pair, V0:R0], 0.0).astype(BF16)
        upd = jnp.concatenate(
            [_dot(kl_t[h * GLA_DK:(h + 1) * GLA_DK, pair], v_chunk[:, h * GLA_DV:(h + 1) * GLA_DV])
             for h in range(GLA_HEADS)], axis=0)
        st = st * decay_t[:, c * decay_cols:c * decay_cols + 1] + upd
        if per_chunk_state:
            s_out_ref[c] = st.reshape(GLA_HEADS, GLA_DK, GLA_DV)

    finish_rows(slice(0, TILE_TOKENS))
    if not per_chunk_state:
        st_scr[...] = st
        s_out_ref[...] = st.reshape(GLA_HEADS, GLA_DK, GLA_DV)

    @pl.when(jnp.max(-b_all) > GLA_SAFE_EXPONENT)
    def _():
        head_sum = jnp.where(head_of((hk, LANES), 0, GLA_DK)
                             == lax.broadcasted_iota(jnp.int32, (hk, LANES), 1), 1.0, 0.0).astype(BF16)
        head_expand = jnp.where(lax.broadcasted_iota(jnp.int32, (LANES, hv), 0)
                                == head_of((LANES, hv), 1, GLA_DV), 1.0, 0.0).astype(BF16)
        query_row = lax.broadcasted_iota(jnp.int32, (CHUNK, hv), 0)
        for c in range(CHUNKS_PER_TILE):
            rows = slice(c * CHUNK, (c + 1) * CHUNK)
            q_c = proj_scr[rows, Q0:K0] * (GLA_DK ** -0.5)
            b_c = b_scr[rows, :]

            def add_key(j, acc, c=c, q_c=q_c, b_c=b_c):
                key = pl.ds(c * CHUNK + j, 1)
                decay = jnp.exp(jnp.minimum(b_c - b_scr[key, :], 0.0))
                w = q_c * proj_scr[key, K0:V0] * decay
                score = _dot(w.astype(BF16), head_sum)
                score = _dot(score.astype(BF16), head_expand)
                return acc + jnp.where(query_row >= j, score, 0.0) * proj_scr[key, V0:R0]

            o_intra = lax.fori_loop(0, CHUNK, add_key, jnp.zeros((CHUNK, hv), F32))
            o_scr[rows, :] = o_intra + o_inter_scr[rows, :]
        finish_rows(slice(0, TILE_TOKENS))


def _const_spec(shape, single_buffer=False):
    nd = len(shape)
    if single_buffer:
        return pl.BlockSpec(shape, lambda *_: (0,) * nd, pipeline_mode=pl.Buffered(1))
    return pl.BlockSpec(shape, lambda *_: (0,) * nd)


def _mixer_weight_specs():
    return [
        _const_spec((None, D_MODEL, N_IN), single_buffer=True),
        _const_spec((D_MODEL, LANES)),
        _const_spec((LANES, GLA_HEADS * GLA_DK)),
        _const_spec((1, GLA_HEADS * GLA_DK)),
        _const_spec((len(POOL_WINDOWS) // 2, 2 * POOL_GROUP, 2 * POOL_GROUP)),
        _const_spec((1, POOL_WIDTH)),
        _const_spec((1, GLA_DV)),
        _const_spec((None, D_MODEL, D_MODEL), single_buffer=True),
        _const_spec((1, D_MODEL)),
        _const_spec((1, D_MODEL)),
        _const_spec((NUM_EXPERTS, D_MODEL)),
        _const_spec((NUM_EXPERTS, LANES)),
    ]


def _mixer_weights(w_in, w_pool, pool_scale, w_gate_up, b_gate, gla_norm_w, w_out, ln1_g, ln1_b,
                   w_router, b_router):
    w_glr = jnp.zeros((D_MODEL, LANES), BF16).at[:, :GATE_RANK].set(w_in[0, :, N_MAIN:].astype(BF16))
    w_gate = jnp.zeros((LANES, GLA_HEADS * GLA_DK), BF16).at[:GATE_RANK].set(w_gate_up[0].astype(BF16))
    wp = w_pool[0].astype(BF16)
    zero = jnp.zeros((POOL_GROUP, POOL_GROUP), BF16)
    w_pool_pairs = jnp.stack([jnp.block([[wp[2 * p], zero], [zero, wp[2 * p + 1]]])
                              for p in range(len(POOL_WINDOWS) // 2)])
    weights = (
        w_in, w_glr, w_gate, b_gate[0][None, :],
        w_pool_pairs, pool_scale[0][None, :], gla_norm_w[0][None, :],
        w_out, ln1_g[0][None, :], ln1_b[0][None, :],
        w_router[0].T.astype(BF16), jnp.broadcast_to(b_router[0][:, None], (NUM_EXPERTS, LANES)),
    )
    assert len(weights) == N_MIXER_WEIGHTS
    return weights


def _mixer_scratch(per_chunk_state):
    ext_rows = (CHUNKS_PER_TILE * (CHUNK + HIST_PAD)) if per_chunk_state else (TILE_TOKENS + HIST_PAD)
    return [
        pltpu.VMEM((TILE_TOKENS, N_MAIN), F32),
        pltpu.VMEM((TILE_TOKENS, GLA_HEADS * GLA_DK), F32),
        pltpu.VMEM((TILE_TOKENS, GLA_HEADS * GLA_DV), F32),
        pltpu.VMEM((ext_rows, POOL_WIDTH), F32),
        pltpu.VMEM((GLA_HEADS * GLA_DK, GLA_DV), F32),
        pltpu.VMEM((SCAN_ROWS, SCAN_ROWS), BF16),
        pltpu.VMEM((D_MODEL, N_MAIN), BF16),
        pltpu.VMEM((D_MODEL, D_MODEL), BF16),
        pltpu.VMEM((TILE_TOKENS, GLA_HEADS * GLA_DV), F32),
        pltpu.VMEM((TILE_TOKENS, POOL_WIDTH), F32),
    ]


def _mixer_out_shapes(n, bsz):
    return (
        jax.ShapeDtypeStruct((n, D_MODEL), F32),
        jax.ShapeDtypeStruct((n, HALF), jnp.int32),
        jax.ShapeDtypeStruct((NUM_EXPERTS, n), F32),
        jax.ShapeDtypeStruct((bsz, POOL_HIST, POOL_WIDTH), F32),
        jax.ShapeDtypeStruct((bsz, GLA_HEADS, GLA_DK, GLA_DV), F32),
    )


def _mixer_prompt(x, weights):
    bsz, seq, _ = x.shape
    tiles = seq // TILE_TOKENS
    n_total = bsz * seq
    return pl.pallas_call(
        functools.partial(_mixer_kernel, False, 0),
        grid=(bsz, tiles),
        in_specs=[pl.BlockSpec((None, TILE_TOKENS, D_MODEL), lambda b, t: (b, t, 0))] + _mixer_weight_specs(),
        out_specs=(
            pl.BlockSpec((TILE_TOKENS, D_MODEL), lambda b, t: (b * tiles + t, 0)),
            pl.BlockSpec((TILE_TOKENS, HALF), lambda b, t: (b * tiles + t, 0)),
            pl.BlockSpec((NUM_EXPERTS, TILE_TOKENS), lambda b, t: (0, b * tiles + t)),
            pl.BlockSpec((None, POOL_HIST, POOL_WIDTH), lambda b, t: (b, 0, 0)),
            pl.BlockSpec((None, GLA_HEADS, GLA_DK, GLA_DV), lambda b, t: (b, 0, 0, 0)),
        ),
        out_shape=_mixer_out_shapes(n_total, bsz),
        scratch_shapes=_mixer_scratch(False),
        compiler_params=pltpu.CompilerParams(
            dimension_semantics=("arbitrary", "arbitrary"), vmem_limit_bytes=VMEM_LIMIT),
        name="mixer_prompt",
    )(x, *weights)


def _mixer_sample(x, hist, state, weights):
    bsz = x.shape[0]
    tiles = bsz // CHUNKS_PER_TILE
    return pl.pallas_call(
        functools.partial(_mixer_kernel, True, PAST_LEN),
        grid=(tiles,),
        in_specs=[
            pl.BlockSpec((CHUNKS_PER_TILE, CHUNK, D_MODEL), lambda i: (i, 0, 0)),
            pl.BlockSpec((CHUNKS_PER_TILE, POOL_HIST, POOL_WIDTH), lambda i: (i, 0, 0)),
            pl.BlockSpec((CHUNKS_PER_TILE, GLA_HEADS, GLA_DK, GLA_DV), lambda i: (i, 0, 0, 0)),
        ] + _mixer_weight_specs(),
        out_specs=(
            pl.BlockSpec((TILE_TOKENS, D_MODEL), lambda i: (i, 0)),
            pl.BlockSpec((TILE_TOKENS, HALF), lambda i: (i, 0)),
            pl.BlockSpec((NUM_EXPERTS, TILE_TOKENS), lambda i: (0, i)),
            pl.BlockSpec((CHUNKS_PER_TILE, POOL_HIST, POOL_WIDTH), lambda i: (i, 0, 0)),
            pl.BlockSpec((CHUNKS_PER_TILE, GLA_HEADS, GLA_DK, GLA_DV), lambda i: (i, 0, 0, 0)),
        ),
        out_shape=_mixer_out_shapes(bsz * CHUNK, bsz),
        scratch_shapes=_mixer_scratch(True),
        compiler_params=pltpu.CompilerParams(
            dimension_semantics=("arbitrary",), vmem_limit_bytes=VMEM_LIMIT),
        name="mixer_sample",
    )(x, hist, state, *weights)


def _router_kernel(lt_ref, gates_ref, dest_ref, padend_ref, cnt_scr, base_scr, pstart_scr, before_scr,
                   topk_scr):
    phase = pl.program_id(0)
    i = pl.program_id(1)
    shape = (NUM_EXPERTS, ROUTE_TOKENS)
    row = lax.broadcasted_iota(jnp.int32, shape, 0)

    def tile_counts_of(chosen):
        return jnp.broadcast_to(jnp.sum(chosen, axis=1, keepdims=True), (NUM_EXPERTS, LANES))

    @pl.when(phase == 0)
    def _():
        @pl.when(i == 0)
        def _():
            cnt_scr[...] = jnp.zeros_like(cnt_scr)

        logits = lt_ref[...]
        idxs, vals = [], []
        chosen = jnp.zeros(shape, F32)
        for _ in range(TOP_K):
            m = jnp.max(logits, axis=0, keepdims=True)
            idx = jnp.min(jnp.where(logits == m, row, NUM_EXPERTS), axis=0, keepdims=True)
            hit = row == idx
            idxs.append(idx.astype(F32))
            vals.append(m)
            chosen = chosen + jnp.where(hit, 1.0, 0.0)
            logits = jnp.where(hit, -jnp.inf, logits)
        topk_scr[i] = jnp.concatenate(idxs + vals, axis=0)
        cnt_scr[...] += tile_counts_of(chosen)

    @pl.when(phase == 1)
    def _():
        @pl.when(i == 0)
        def _():
            blocks = jnp.floor((cnt_scr[...] + (MOE_ROWS - 1)) * (1.0 / MOE_ROWS))
            erow = lax.broadcasted_iota(jnp.int32, (NUM_EXPERTS, LANES), 0)
            lane = lax.broadcasted_iota(jnp.int32, (NUM_EXPERTS, LANES), 1)
            cum = blocks
            shift = 1
            while shift < NUM_EXPERTS:
                cum = cum + jnp.where(erow >= shift, pltpu.roll(cum, shift, 0), 0.0)
                shift *= 2
            padend_ref[...] = jnp.where(lane == 1, cnt_scr[...], cum * MOE_ROWS)
            pstart_scr[...] = (cum - blocks) * MOE_ROWS
            base_scr[...] = jnp.zeros_like(base_scr)
            ti = lax.broadcasted_iota(jnp.int32, (ROUTE_TOKENS, ROUTE_TOKENS), 0)
            tj = lax.broadcasted_iota(jnp.int32, (ROUTE_TOKENS, ROUTE_TOKENS), 1)
            before_scr[...] = jnp.where(ti < tj, 1.0, 0.0).astype(BF16)

        topk = topk_scr[i]
        sel = [row == topk[k:k + 1, :].astype(jnp.int32) for k in range(TOP_K)]
        vals = [topk[TOP_K + k:TOP_K + k + 1, :] for k in range(TOP_K)]
        chosen = sum(jnp.where(hit, 1.0, 0.0) for hit in sel)
        earlier = _dot(chosen.astype(BF16), before_scr[...])
        pos = pstart_scr[:, 0:1] + base_scr[:, 0:1] + earlier
        dest = [jnp.sum(jnp.where(hit, pos, 0.0), axis=0, keepdims=True) for hit in sel]
        dest_ref[...] = jnp.concatenate(dest, axis=0).astype(jnp.int32)
        ex = [jnp.exp(v - vals[0]) for v in vals]
        denom = ex[0] + ex[1] + ex[2] + ex[3]
        gates_ref[...] = jnp.concatenate([e / denom for e in ex], axis=0)
        base_scr[...] += tile_counts_of(chosen)


def _router(logits_t):
    n = logits_t.shape[1]
    assert n % ROUTE_TOKENS == 0
    tiles = n // ROUTE_TOKENS
    return pl.pallas_call(
        _router_kernel,
        grid=(2, tiles),
        in_specs=[pl.BlockSpec((NUM_EXPERTS, ROUTE_TOKENS), lambda p, i: (0, i))],
        out_specs=(
            pl.BlockSpec((TOP_K, ROUTE_TOKENS), lambda p, i: (0, i * p)),
            pl.BlockSpec((TOP_K, ROUTE_TOKENS), lambda p, i: (0, i * p)),
            pl.BlockSpec((NUM_EXPERTS, LANES), lambda p, i: (0, 0)),
        ),
        out_shape=(
            jax.ShapeDtypeStruct((TOP_K, n), F32),
            jax.ShapeDtypeStruct((TOP_K, n), jnp.int32),
            jax.ShapeDtypeStruct((NUM_EXPERTS, LANES), F32),
        ),
        scratch_shapes=[pltpu.VMEM((NUM_EXPERTS, LANES), F32)] * 3
        + [pltpu.VMEM((ROUTE_TOKENS, ROUTE_TOKENS), BF16),
           pltpu.VMEM((tiles, 2 * TOP_K, ROUTE_TOKENS), F32)],
        compiler_params=pltpu.CompilerParams(
            dimension_semantics=("arbitrary", "arbitrary"), vmem_limit_bytes=VMEM_LIMIT),
        name="router",
    )(logits_t)


SC_CORES = 2
SC_SUBCORES = 16
SC_WORKERS = SC_CORES * SC_SUBCORES
DISPATCH_ROWS = 64


def _dispatch(h_sources, dest_chunks, m_pad):
    shares, first_chunk = [], 0
    for src in h_sources:
        n_chunks = src.shape[0] // DISPATCH_ROWS
        assert src.shape[0] % DISPATCH_ROWS == 0 and n_chunks % SC_WORKERS == 0
        shares.append((first_chunk, n_chunks // SC_WORKERS))
        first_chunk += n_chunks
    plan = [(s, first, per, j) for s, (first, per) in enumerate(shares) for j in range(per)]
    per_worker = len(plan)
    mesh = plsc.VectorSubcoreMesh(core_axis_name="c", subcore_axis_name="s")

    @functools.partial(
        pl.kernel, mesh=mesh,
        out_type=jax.ShapeDtypeStruct((m_pad, HALF), jnp.int32),
        scratch_types=[
            pltpu.VMEM((2, TOP_K, DISPATCH_ROWS), jnp.int32),
            pltpu.VMEM((2, DISPATCH_ROWS, HALF), jnp.int32),
            pltpu.SemaphoreType.DMA((2,)),
            pltpu.SemaphoreType.DMA((2,)),
        ],
        compiler_params=pltpu.CompilerParams(use_tc_tiling_on_sc=True),
        name="dispatch",
    )
    def dispatch_kernel(*refs):
        h_hbms = refs[:len(h_sources)]
        dest_hbm, out_hbm, idx_v, rows_v, load_sems, scatter_sems = refs[len(h_sources):]
        wid = lax.axis_index("s") * SC_CORES + lax.axis_index("c")

        def loads(j):
            src, first, per, k = plan[j]
            local = wid * per + k
            slot = j % 2
            return (
                pltpu.make_async_copy(dest_hbm.at[first + local], idx_v.at[slot], load_sems.at[slot]),
                pltpu.make_async_copy(h_hbms[src].at[pl.ds(local * DISPATCH_ROWS, DISPATCH_ROWS)],
                                      rows_v.at[slot], load_sems.at[slot]),
            )

        def scatters(j):
            slot = j % 2
            return [pltpu.make_async_copy(rows_v.at[slot], out_hbm.at[idx_v.at[slot, k]],
                                          scatter_sems.at[slot]) for k in range(TOP_K)]

        for cp in loads(0):
            cp.start()
        for j in range(per_worker):
            for cp in loads(j):
                cp.wait()
            if j >= 1:
                for cp in scatters(j - 1):
                    cp.wait()
            if j + 1 < per_worker:
                for cp in loads(j + 1):
                    cp.start()
            for cp in scatters(j):
                cp.start()
        for cp in scatters(per_worker - 1):
            cp.wait()

    return dispatch_kernel(*h_sources, dest_chunks)


def _gather_expert_rows(y_sorted, dest_chunks, row_offset, n):
    n_chunks = n // DISPATCH_ROWS
    assert n_chunks % SC_WORKERS == 0 and row_offset % DISPATCH_ROWS == 0
    per_worker = n_chunks // SC_WORKERS
    chunk0 = row_offset // DISPATCH_ROWS
    mesh = plsc.VectorSubcoreMesh(core_axis_name="c", subcore_axis_name="s")

    @functools.partial(
        pl.kernel, mesh=mesh,
        out_type=jax.ShapeDtypeStruct((TOP_K, n, HALF), jnp.int32),
        scratch_types=[
            pltpu.VMEM((TOP_K, DISPATCH_ROWS), jnp.int32),
            pltpu.VMEM((2, DISPATCH_ROWS, HALF), jnp.int32),
            pltpu.SemaphoreType.DMA((2,)),
        ],
        compiler_params=pltpu.CompilerParams(use_tc_tiling_on_sc=True),
        name="gather_expert_rows",
    )
    def gather_kernel(y_hbm, dest_hbm, out_hbm, idx_v, rows_v, sems):
        wid = lax.axis_index("s") * SC_CORES + lax.axis_index("c")

        def gather(k):
            return pltpu.make_async_copy(y_hbm.at[idx_v.at[k]], rows_v.at[k % 2], sems.at[k % 2])

        @pl.loop(0, per_worker)
        def _(j):
            local = wid * per_worker + j
            pltpu.sync_copy(dest_hbm.at[chunk0 + local], idx_v)
            gather(0).start()
            for k in range(TOP_K):
                if k + 1 < TOP_K:
                    gather(k + 1).start()
                gather(k).wait()
                pltpu.sync_copy(rows_v.at[k % 2],
                                out_hbm.at[k, pl.ds(local * DISPATCH_ROWS, DISPATCH_ROWS)])

    return gather_kernel(y_sorted, dest_chunks)


def _moe_kernel(be_ref, wsel_ref, units_ref, nused_ref, x_ref, wgu_ref, bgu_ref, wd_ref, bd_ref, y_ref,
                wgu_bf, wd_bf):
    del wsel_ref
    i = pl.program_id(0)

    def ffn(rows):
        gu = _dot(_unpack_bf16_pairs(x_ref[rows, :]), wgu_bf[...]) + bgu_ref[...]
        gate = jnp.minimum(gu[:, :EXPERT_FF], SWIGLU_LIMIT)
        up = jnp.clip(gu[:, EXPERT_FF:], -SWIGLU_LIMIT, SWIGLU_LIMIT)
        hmid = gate * (1.0 / (1.0 + jnp.exp(-SWIGLU_ALPHA * gate))) * (up + 1.0)
        y = _dot(hmid.astype(BF16), wd_bf[...]) + bd_ref[...]
        y_ref[rows, :] = _pack_bf16_pairs(y.astype(BF16))

    @pl.when(i < nused_ref[0])
    def _():
        @pl.when((i == 0) | (be_ref[i] != be_ref[jnp.maximum(i - 1, 0)]))
        def _():
            wgu_bf[...] = wgu_ref[...].astype(BF16)
            wd_bf[...] = wd_ref[...].astype(BF16)

        for units in range(1, MOE_ROWS // MOE_TAIL_ROWS + 1):
            @pl.when(units_ref[i] == units)
            def _(units=units):
                ffn(slice(0, units * MOE_TAIL_ROWS))


def _moe_experts(block_expert, weight_expert, block_units, n_used, x_sorted, w_gu, b_gu, w_down, b_down):
    m_pad = x_sorted.shape[0]
    n_blocks = m_pad // MOE_ROWS

    def blk(i, be, ws, hb, nu):
        return jnp.minimum(i, nu[0] - 1)

    def expert(i, be, ws, hb, nu):
        return be[blk(i, be, ws, hb, nu)]

    def held(i, be, ws, hb, nu):
        return ws[blk(i, be, ws, hb, nu)]

    grid_spec = pltpu.PrefetchScalarGridSpec(
        num_scalar_prefetch=4,
        grid=(n_blocks,),
        in_specs=[
            pl.BlockSpec((MOE_ROWS, HALF), lambda *a: (blk(*a), 0)),
            pl.BlockSpec((None, D_MODEL, 2 * EXPERT_FF), lambda *a: (held(*a), 0, 0)),
            pl.BlockSpec((None, 1, 2 * EXPERT_FF), lambda *a: (expert(*a), 0, 0)),
            pl.BlockSpec((None, EXPERT_FF, D_MODEL), lambda *a: (held(*a), 0, 0)),
            pl.BlockSpec((None, 1, D_MODEL), lambda *a: (expert(*a), 0, 0)),
        ],
        out_specs=pl.BlockSpec((MOE_ROWS, HALF), lambda *a: (blk(*a), 0)),
        scratch_shapes=[
            pltpu.VMEM((D_MODEL, 2 * EXPERT_FF), BF16),
            pltpu.VMEM((EXPERT_FF, D_MODEL), BF16),
        ],
    )
    return pl.pallas_call(
        _moe_kernel,
        grid_spec=grid_spec,
        out_shape=jax.ShapeDtypeStruct((m_pad, HALF), jnp.int32),
        compiler_params=pltpu.CompilerParams(
            dimension_semantics=("arbitrary",), vmem_limit_bytes=VMEM_LIMIT),
        name="moe_experts",
    )(block_expert, weight_expert, block_units, n_used, x_sorted, w_gu, b_gu, w_down, b_down)


def _combine_kernel(yk_ref, gates_ref, h_ref, g_ref, b_ref, out_ref):
    pad = jnp.zeros((LANES - TOP_K, COMBINE_TOKENS), F32)
    gates = jnp.transpose(jnp.concatenate([gates_ref[...], pad], axis=0))
    lo = hi = None
    for k in range(TOP_K):
        u = lax.bitcast_convert_type(yk_ref[k], jnp.uint32)
        gk = gates[:, k:k + 1]
        lo_k = lax.bitcast_convert_type(u << 16, F32) * gk
        hi_k = lax.bitcast_convert_type(u & jnp.uint32(HI_MASK), F32) * gk
        lo = lo_k if lo is None else lo + lo_k
        hi = hi_k if hi is None else hi + hi_k
    acc = ALPHA * h_ref[...] + jnp.concatenate([lo, hi], axis=1)
    out_ref[...] = _layer_norm(acc, g_ref[...], b_ref[...])


def _combine_kernel_aliased(yk_ref, gates_ref, h_ref, g_ref, b_ref, prev_ref, out_ref):
    del prev_ref
    _combine_kernel(yk_ref, gates_ref, h_ref, g_ref, b_ref, out_ref)


def _combine(yk, gates, token_offset, h_src, h_offset, ln_g, ln_b, out_prev):
    n_seg = yk.shape[1]
    out_rows = h_src.shape[0]
    tile0 = token_offset // COMBINE_TOKENS
    out_tile0 = h_offset // COMBINE_TOKENS
    in_specs = [
        pl.BlockSpec((TOP_K, COMBINE_TOKENS, HALF), lambda i: (0, i, 0)),
        pl.BlockSpec((TOP_K, COMBINE_TOKENS), lambda i: (0, tile0 + i)),
        pl.BlockSpec((COMBINE_TOKENS, D_MODEL), lambda i: (out_tile0 + i, 0)),
        _const_spec((1, D_MODEL)),
        _const_spec((1, D_MODEL)),
    ]
    args = [yk, gates, h_src, ln_g, ln_b]
    aliases = {}
    kern = _combine_kernel
    if out_prev is not None:
        in_specs.append(pl.BlockSpec(memory_space=pl.ANY))
        aliases = {len(args): 0}
        args.append(out_prev)
        kern = _combine_kernel_aliased
    return pl.pallas_call(
        kern,
        grid=(n_seg // COMBINE_TOKENS,),
        in_specs=in_specs,
        out_specs=pl.BlockSpec((COMBINE_TOKENS, D_MODEL), lambda i: (out_tile0 + i, 0)),
        out_shape=jax.ShapeDtypeStruct((out_rows, D_MODEL), F32),
        input_output_aliases=aliases,
        compiler_params=pltpu.CompilerParams(
            dimension_semantics=("arbitrary",), vmem_limit_bytes=VMEM_LIMIT),
        name="moe_combine",
    )(*args)


def kernel(x_prompt, x_sample, state_pool, state_gla, w_in, w_pool, pool_scale, w_gate_up, b_gate,
           gla_norm_w, w_out, ln1_g, ln1_b, w_router, b_router, w_gu, b_gu, w_down, b_down,
           ln2_g, ln2_b):
    assert w_in.shape[0] == 1, "single-layer kernel"
    bp, seq, _ = x_prompt.shape
    bs, dec_seq, _ = x_sample.shape
    assert dec_seq == CHUNK and seq % TILE_TOKENS == 0 and bs % CHUNKS_PER_TILE == 0
    n_prompt = bp * seq
    n_sample = bs * dec_seq
    n_total = n_prompt + n_sample
    nk = n_total * TOP_K
    n_blocks = -(-nk // MOE_ROWS) + NUM_EXPERTS
    m_pad = n_blocks * MOE_ROWS

    weights = _mixer_weights(w_in, w_pool, pool_scale, w_gate_up, b_gate, gla_norm_w, w_out,
                             ln1_g, ln1_b, w_router, b_router)

    h_p, hb_p, logits_p, hist_p, s_p = _mixer_prompt(x_prompt, weights)
    h_s, hb_s, logits_s, hist_s, s_s = _mixer_sample(x_sample, state_pool[0], state_gla[0], weights)

    gates_t, dest_t, layout = _router(jnp.concatenate([logits_p, logits_s], axis=1))
    pad_end = layout[:, 0].astype(jnp.int32)
    counts = layout[:, 1].astype(jnp.int32)
    block_start = jnp.arange(n_blocks, dtype=jnp.int32) * MOE_ROWS
    block_expert = jnp.minimum(jnp.sum((block_start[:, None] >= pad_end[None, :]).astype(jnp.int32), axis=1),
                               NUM_EXPERTS - 1)
    n_used = (pad_end[-1:] // MOE_ROWS).astype(jnp.int32)
    is_first = jnp.concatenate([jnp.ones((1,), bool), block_expert[1:] != block_expert[:-1]])
    blocks = jnp.arange(n_blocks, dtype=jnp.int32)
    later_other = ((blocks[None, :] > blocks[:, None]) & (blocks[None, :] < n_used[0])
                   & (block_expert[None, :] != block_expert[:, None]))
    next_expert = jnp.min(jnp.where(later_other, block_expert[None, :], NUM_EXPERTS), axis=1)
    next_expert = jnp.where(next_expert == NUM_EXPERTS, block_expert, next_expert)
    weight_expert = jnp.where(is_first, block_expert, next_expert).astype(jnp.int32)
    of_expert = block_expert[:, None] == jnp.arange(NUM_EXPERTS, dtype=jnp.int32)[None, :]
    seg_end = jnp.sum(jnp.where(of_expert, (pad_end - (-counts % MOE_ROWS))[None, :], 0), axis=1)
    block_units = jnp.clip((seg_end - block_start + MOE_TAIL_ROWS - 1) // MOE_TAIL_ROWS,
                          1, MOE_ROWS // MOE_TAIL_ROWS).astype(jnp.int32)

    dest_chunks = dest_t.reshape(TOP_K, n_total // DISPATCH_ROWS, DISPATCH_ROWS).transpose(1, 0, 2)
    x_sorted = _dispatch((hb_p, hb_s), dest_chunks, m_pad)
    y_sorted = _moe_experts(block_expert, weight_expert, block_units, n_used, x_sorted, w_gu[0], b_gu[0][:, None, :],
                            w_down[0], b_down[0][:, None, :])
    gates = gates_t
    ln_g, ln_b = ln2_g[0][None, :], ln2_b[0][None, :]
    yk = _gather_expert_rows(y_sorted, dest_chunks, n_prompt, n_sample)
    y_sample = _combine(yk, gates, n_prompt, h_s, 0, ln_g, ln_b, None)
    unit = SC_WORKERS * DISPATCH_ROWS
    assert n_prompt % unit == 0
    sizes, left = [], n_prompt // unit
    while left > 0:
        size = min(left, 1 if len(sizes) < 2 else 2)
        sizes.append(size * unit)
        left -= size
    y_prompt, start = None, 0
    for seg in sizes:
        yk = _gather_expert_rows(y_sorted, dest_chunks, start, seg)
        y_prompt = _combine(yk, gates, start, h_p, start, ln_g, ln_b, y_prompt)
        start += seg
    y_prompt = y_prompt.reshape(bp, seq, D_MODEL)
    y_sample = y_sample.reshape(bs, dec_seq, D_MODEL)
    return (y_prompt, y_sample, hist_p[None], s_p[None], hist_s[None], s_s[None])
```

```python
import functools

import jax
import jax.numpy as jnp
from jax import lax
from jax.experimental import pallas as pl
from jax.experimental.pallas import tpu as pltpu
from jax.experimental.pallas import tpu_sc as plsc

F32 = jnp.float32
BF16 = jnp.bfloat16

D_MODEL = 1024
CHUNK = 64
PAST_LEN = 1024
POOL_WIDTH = 512
POOL_WINDOWS = (2, 4, 8, 16)
POOL_GROUP = 128
POOL_HIST = 15
GLA_HEADS = 4
GLA_DK = 64
GLA_DV = 128
GATE_RANK = 16
GATE_NORMALIZER = 16.0
NUM_EXPERTS = 32
TOP_K = 4
EXPERT_FF = 1024
SWIGLU_LIMIT = 7.0
SWIGLU_ALPHA = 1.702
LN_EPS = 1e-5
RMS_EPS = 1e-6
ALPHA = 2.0 ** 0.25

Q0 = POOL_WIDTH
K0 = Q0 + GLA_HEADS * GLA_DK
V0 = K0 + GLA_HEADS * GLA_DK
R0 = V0 + GLA_HEADS * GLA_DV
N_MAIN = R0 + GLA_HEADS * GLA_DV
N_IN = N_MAIN + GATE_RANK

LANES = 128
TILE_TOKENS = 512
CHUNKS_PER_TILE = TILE_TOKENS // CHUNK
HIST_PAD = 16
SCAN_ROWS = 256
GLA_SAFE_EXPONENT = 60.0
MOE_ROWS = 1024
MOE_TAIL_ROWS = 256
ROUTE_TOKENS = 2048
COMBINE_TOKENS = 1024
VMEM_LIMIT = 56 * 1024 * 1024


def _dot(a, b):
    return jnp.dot(a, b, preferred_element_type=F32)


def _dot_nt(a, b):
    return lax.dot_general(a, b, (((1,), (1,)), ((), ())), preferred_element_type=F32)


HALF = D_MODEL // 2
HI_MASK = 0xFFFF0000


def _pack_bf16_pairs(xb):
    lo = lax.bitcast_convert_type(xb[:, :HALF].astype(F32), jnp.uint32) >> 16
    hi = lax.bitcast_convert_type(xb[:, HALF:].astype(F32), jnp.uint32) & jnp.uint32(HI_MASK)
    return lax.bitcast_convert_type(hi | lo, jnp.int32)


def _unpack_bf16_pairs(p):
    u = lax.bitcast_convert_type(p, jnp.uint32)
    lo = lax.bitcast_convert_type(u << 16, F32)
    hi = lax.bitcast_convert_type(u & jnp.uint32(HI_MASK), F32)
    return jnp.concatenate([lo, hi], axis=1).astype(BF16)


def _layer_norm(v, g, b):
    mu = jnp.mean(v, axis=-1, keepdims=True)
    c = v - mu
    var = jnp.mean(c * c, axis=-1, keepdims=True)
    return c * lax.rsqrt(var + LN_EPS) * g + b


N_MIXER_WEIGHTS = 12


def _mixer_kernel(per_chunk_state, pos0, *refs):
    if per_chunk_state:
        (x_ref, hist_in_ref, s_in_ref, *rest) = refs
    else:
        (x_ref, *rest) = refs
        hist_in_ref = s_in_ref = None
    (w_main_ref, w_glr_ref, w_gate_ref, b_gate_ref, w_pool_ref, pscale_ref, gnorm_ref,
     w_out_ref, ln1g_ref, ln1b_ref, w_router_ref, b_router_ref, *rest) = rest
    (h_ref, hb_ref, logits_ref, hist_out_ref, s_out_ref,
     proj_scr, b_scr, o_scr, ext_scr, st_scr, tri_scr, w_main_bf, w_out_bf, st_in_scr,
     pool_scr) = rest

    if per_chunk_state:
        t = None
        first_step = pl.program_id(0) == 0
    else:
        t = pl.program_id(1)
        first_step = (pl.program_id(0) == 0) & (t == 0)
    def load_x():
        return x_ref[...].reshape(TILE_TOKENS, D_MODEL)

    xb = load_x().astype(BF16)

    @pl.when(first_step)
    def _():
        ti = lax.broadcasted_iota(jnp.int32, (SCAN_ROWS, SCAN_ROWS), 0)
        tj = lax.broadcasted_iota(jnp.int32, (SCAN_ROWS, SCAN_ROWS), 1)
        same_chunk = (ti // CHUNK) == (tj // CHUNK)
        tri_scr[...] = jnp.where(same_chunk & (ti >= tj), 1.0, 0.0).astype(BF16)
        w_main_bf[...] = w_main_ref[:, 0:N_MAIN].astype(BF16)
        w_out_bf[...] = w_out_ref[...].astype(BF16)
        if not per_chunk_state:
            st_scr[...] = jnp.zeros_like(st_scr)
            ext_scr[0:HIST_PAD, :] = jnp.zeros((HIST_PAD, POOL_WIDTH), F32)

    glr = _dot(xb, w_glr_ref[...])
    proj_scr[:, 0:V0] = _dot(xb, w_main_bf[:, 0:V0])
    gk = _dot(glr.astype(BF16), w_gate_ref[...]) + b_gate_ref[...]
    log_sig = jnp.minimum(gk, 0.0) - jnp.log1p(jnp.exp(-jnp.abs(gk)))
    g = log_sig / GATE_NORMALIZER
    g_hi = g.astype(BF16)
    g_lo = (g - g_hi.astype(F32)).astype(BF16)
    proj_scr[:, V0:N_MAIN] = _dot(xb, w_main_bf[:, V0:N_MAIN])
    for s in range(TILE_TOKENS // SCAN_ROWS):
        rs = slice(s * SCAN_ROWS, (s + 1) * SCAN_ROWS)
        b_scr[rs, :] = _dot(tri_scr[...], g_hi[rs]) + _dot(tri_scr[...], g_lo[rs])

    if per_chunk_state:
        seg_len, seg_stride, n_seg = CHUNK, CHUNK + HIST_PAD, CHUNKS_PER_TILE
        for c in range(n_seg):
            base = c * seg_stride
            ext_scr[base:base + HIST_PAD, :] = jnp.zeros((HIST_PAD, POOL_WIDTH), F32)
            ext_scr[base + 1:base + HIST_PAD, :] = hist_in_ref[c]
            ext_scr[base + HIST_PAD:base + seg_stride, :] = proj_scr[c * CHUNK:(c + 1) * CHUNK, 0:POOL_WIDTH]
        row_pos = pos0 + lax.broadcasted_iota(jnp.int32, (seg_len, POOL_GROUP), 0)
    else:
        seg_len, seg_stride, n_seg = TILE_TOKENS, TILE_TOKENS + HIST_PAD, 1
        ext_scr[0:HIST_PAD, :] = jnp.where(t == 0, 0.0, ext_scr[0:HIST_PAD, :])
        ext_scr[HIST_PAD:seg_stride, :] = proj_scr[:, 0:POOL_WIDTH]
        row_pos = pos0 + t * TILE_TOKENS + lax.broadcasted_iota(jnp.int32, (seg_len, POOL_GROUP), 0)

    pooled_groups = []
    for gi, w in enumerate(POOL_WINDOWS):
        gs = slice(gi * POOL_GROUP, (gi + 1) * POOL_GROUP)
        cnt = jnp.minimum(row_pos + 1, w).astype(F32)
        ext = ext_scr[:, gs]
        win = ext
        shift = 1
        while shift < w:
            win = win + pltpu.roll(win, shift, 0)
            shift *= 2
        segs = []
        for s in range(n_seg):
            base = s * seg_stride + HIST_PAD
            segs.append(win[base:base + seg_len] / cnt - ext[base:base + seg_len])
        pooled = segs[0] if n_seg == 1 else jnp.concatenate(segs, axis=0)
        pooled_groups.append(pooled.astype(BF16))
    pool_cols = []
    for p in range(len(POOL_WINDOWS) // 2):
        both = jnp.concatenate(pooled_groups[2 * p:2 * p + 2], axis=1)
        pool_cols.append(_dot(both, w_pool_ref[p]))
    pool_scr[...] = jnp.concatenate(pool_cols, axis=1) * pscale_ref[...]

    if per_chunk_state:
        for c in range(n_seg):
            end = (c + 1) * seg_stride
            hist_out_ref[c] = ext_scr[end - POOL_HIST:end, :]
    else:
        hist_out_ref[...] = ext_scr[seg_stride - POOL_HIST:seg_stride, :]
        ext_scr[0:HIST_PAD, :] = ext_scr[TILE_TOKENS:seg_stride, :]

    hk = GLA_HEADS * GLA_DK
    hv = GLA_HEADS * GLA_DV
    pair_rows = 2 * CHUNK
    decay_cols = LANES // CHUNKS_PER_TILE

    def head_of(shape, dim, width):
        return lax.broadcasted_iota(jnp.int32, shape, dim) // width

    same_head_k = head_of((hk, hk), 0, CHUNK) == head_of((hk, hk), 1, GLA_DK)
    same_head_v = head_of((hk, hv), 0, CHUNK) == head_of((hk, hv), 1, GLA_DV)
    pair_half = head_of((pair_rows, hv), 0, CHUNK)
    causal = (lax.broadcasted_iota(jnp.int32, (CHUNK, hk), 0)
              >= lax.broadcasted_iota(jnp.int32, (CHUNK, hk), 1) % CHUNK)

    b_all = b_scr[...]
    b_last = [b_scr[(c + 1) * CHUNK - 1:(c + 1) * CHUNK, :] for c in range(CHUNKS_PER_TILE)]
    b_last_rows = jnp.concatenate([jnp.broadcast_to(bl, (CHUNK, hk)) for bl in b_last], axis=0)
    k_all = proj_scr[:, K0:V0]
    qt_all = (proj_scr[:, Q0:K0] * (GLA_DK ** -0.5) * jnp.exp(b_all)).astype(BF16)
    kt_all = k_all * jnp.exp(-b_all)
    kl_t = jnp.transpose(k_all * jnp.exp(b_last_rows - b_all)).astype(BF16)
    decay_t = jnp.transpose(jnp.exp(jnp.concatenate(
        [jnp.broadcast_to(bl, (decay_cols, hk)) for bl in b_last], axis=0)))

    def finish_rows(rs):
        r = proj_scr[rs, R0:N_MAIN]
        silu_r = r * (1.0 / (1.0 + jnp.exp(-r)))
        gated = []
        for h in range(GLA_HEADS):
            vs = slice(h * GLA_DV, (h + 1) * GLA_DV)
            oh = o_scr[rs, vs]
            ms = jnp.mean(oh * oh, axis=-1, keepdims=True)
            gated.append(oh * lax.rsqrt(ms + RMS_EPS) * gnorm_ref[...] * silu_r[:, vs])
        mix_in = jnp.concatenate([pool_scr[rs, :]] + gated, axis=1).astype(BF16)
        resid = ALPHA * load_x()[rs] + _dot(mix_in, w_out_bf[...])
        h_val = _layer_norm(resid, ln1g_ref[...], ln1b_ref[...])
        h_ref[rs, :] = h_val
        hb = h_val.astype(BF16)
        hb_ref[rs, :] = _pack_bf16_pairs(hb)
        logits_ref[:, rs] = _dot_nt(w_router_ref[...], hb) + b_router_ref[:, 0:1]

    st = None if per_chunk_state else jnp.where(t == 0, 0.0, st_scr[...])
    for c in range(CHUNKS_PER_TILE):
        rows = slice(c * CHUNK, (c + 1) * CHUNK)
        pair = slice((c // 2) * pair_rows, (c // 2 + 1) * pair_rows)
        if per_chunk_state:
            st = s_in_ref[c].reshape(hk, GLA_DV)
        qt = qt_all[rows]
        zero = jnp.zeros((), BF16)
        k_stack = jnp.where(same_head_k, jnp.concatenate([kt_all[rows].astype(BF16)] * GLA_HEADS, axis=0), zero)
        v_stack = jnp.where(same_head_v, jnp.concatenate(
            [proj_scr[rows, V0:R0].astype(BF16)] * GLA_HEADS, axis=0), zero)
        s_stack = jnp.where(same_head_v, jnp.concatenate([st.astype(BF16)] * GLA_HEADS, axis=1), zero)
        att = jnp.where(causal, _dot_nt(qt, k_stack), 0.0)
        st_in_scr[c] = st
        o_scr[rows, :] = _dot(att.astype(BF16), v_stack) + _dot(qt, s_stack)
        v_chunk = jnp.where(pair_half == c % 2, proj_scr[pair, V0:R0], 0.0).astype(BF16)
        upd = jnp.concatenate(
            [_dot(kl_t[h * GLA_DK:(h + 1) * GLA_DK, pair], v_chunk[:, h * GLA_DV:(h + 1) * GLA_DV])
             for h in range(GLA_HEADS)], axis=0)
        st = st * decay_t[:, c * decay_cols:c * decay_cols + 1] + upd
        if per_chunk_state:
            s_out_ref[c] = st.reshape(GLA_HEADS, GLA_DK, GLA_DV)

    finish_rows(slice(0, TILE_TOKENS))
    if not per_chunk_state:
        st_scr[...] = st
        s_out_ref[...] = st.reshape(GLA_HEADS, GLA_DK, GLA_DV)

    @pl.when(jnp.max(-b_all) > GLA_SAFE_EXPONENT)
    def _():
        head_sum = jnp.where(head_of((hk, LANES), 0, GLA_DK)
                             == lax.broadcasted_iota(jnp.int32, (hk, LANES), 1), 1.0, 0.0).astype(BF16)
        head_expand = jnp.where(lax.broadcasted_iota(jnp.int32, (LANES, hv), 0)
                                == head_of((LANES, hv), 1, GLA_DV), 1.0, 0.0).astype(BF16)
        query_row = lax.broadcasted_iota(jnp.int32, (CHUNK, hv), 0)
        for c in range(CHUNKS_PER_TILE):
            rows = slice(c * CHUNK, (c + 1) * CHUNK)
            q_c = proj_scr[rows, Q0:K0] * (GLA_DK ** -0.5)
            b_c = b_scr[rows, :]

            def add_key(j, acc, c=c, q_c=q_c, b_c=b_c):
                key = pl.ds(c * CHUNK + j, 1)
                decay = jnp.exp(jnp.minimum(b_c - b_scr[key, :], 0.0))
                w = q_c * proj_scr[key, K0:V0] * decay
                score = _dot(w.astype(BF16), head_sum)
                score = _dot(score.astype(BF16), head_expand)
                return acc + jnp.where(query_row >= j, score, 0.0) * proj_scr[key, V0:R0]

            o_intra = lax.fori_loop(0, CHUNK, add_key, jnp.zeros((CHUNK, hv), F32))
            s_stack = jnp.where(same_head_v, jnp.concatenate(
                [st_in_scr[c].astype(BF16)] * GLA_HEADS, axis=1), jnp.zeros((), BF16))
            o_scr[rows, :] = o_intra + _dot((q_c * jnp.exp(b_c)).astype(BF16), s_stack)
        finish_rows(slice(0, TILE_TOKENS))


def _const_spec(shape, single_buffer=False):
    nd = len(shape)
    if single_buffer:
        return pl.BlockSpec(shape, lambda *_: (0,) * nd, pipeline_mode=pl.Buffered(1))
    return pl.BlockSpec(shape, lambda *_: (0,) * nd)


def _mixer_weight_specs():
    return [
        _const_spec((None, D_MODEL, N_IN), single_buffer=True),
        _const_spec((D_MODEL, LANES)),
        _const_spec((LANES, GLA_HEADS * GLA_DK)),
        _const_spec((1, GLA_HEADS * GLA_DK)),
        _const_spec((len(POOL_WINDOWS) // 2, 2 * POOL_GROUP, 2 * POOL_GROUP)),
        _const_spec((1, POOL_WIDTH)),
        _const_spec((1, GLA_DV)),
        _const_spec((None, D_MODEL, D_MODEL), single_buffer=True),
        _const_spec((1, D_MODEL)),
        _const_spec((1, D_MODEL)),
        _const_spec((NUM_EXPERTS, D_MODEL)),
        _const_spec((NUM_EXPERTS, LANES)),
    ]


def _mixer_weights(w_in, w_pool, pool_scale, w_gate_up, b_gate, gla_norm_w, w_out, ln1_g, ln1_b,
                   w_router, b_router):
    w_glr = jnp.zeros((D_MODEL, LANES), BF16).at[:, :GATE_RANK].set(w_in[0, :, N_MAIN:].astype(BF16))
    w_gate = jnp.zeros((LANES, GLA_HEADS * GLA_DK), BF16).at[:GATE_RANK].set(w_gate_up[0].astype(BF16))
    wp = w_pool[0].astype(BF16)
    zero = jnp.zeros((POOL_GROUP, POOL_GROUP), BF16)
    w_pool_pairs = jnp.stack([jnp.block([[wp[2 * p], zero], [zero, wp[2 * p + 1]]])
                              for p in range(len(POOL_WINDOWS) // 2)])
    weights = (
        w_in, w_glr, w_gate, b_gate[0][None, :],
        w_pool_pairs, pool_scale[0][None, :], gla_norm_w[0][None, :],
        w_out, ln1_g[0][None, :], ln1_b[0][None, :],
        w_router[0].T.astype(BF16), jnp.broadcast_to(b_router[0][:, None], (NUM_EXPERTS, LANES)),
    )
    assert len(weights) == N_MIXER_WEIGHTS
    return weights


def _mixer_scratch(per_chunk_state):
    ext_rows = (CHUNKS_PER_TILE * (CHUNK + HIST_PAD)) if per_chunk_state else (TILE_TOKENS + HIST_PAD)
    return [
        pltpu.VMEM((TILE_TOKENS, N_MAIN), F32),
        pltpu.VMEM((TILE_TOKENS, GLA_HEADS * GLA_DK), F32),
        pltpu.VMEM((TILE_TOKENS, GLA_HEADS * GLA_DV), F32),
        pltpu.VMEM((ext_rows, POOL_WIDTH), F32),
        pltpu.VMEM((GLA_HEADS * GLA_DK, GLA_DV), F32),
        pltpu.VMEM((SCAN_ROWS, SCAN_ROWS), BF16),
        pltpu.VMEM((D_MODEL, N_MAIN), BF16),
        pltpu.VMEM((D_MODEL, D_MODEL), BF16),
        pltpu.VMEM((CHUNKS_PER_TILE, GLA_HEADS * GLA_DK, GLA_DV), F32),
        pltpu.VMEM((TILE_TOKENS, POOL_WIDTH), F32),
    ]


def _mixer_out_shapes(n, bsz):
    return (
        jax.ShapeDtypeStruct((n, D_MODEL), F32),
        jax.ShapeDtypeStruct((n, HALF), jnp.int32),
        jax.ShapeDtypeStruct((NUM_EXPERTS, n), F32),
        jax.ShapeDtypeStruct((bsz, POOL_HIST, POOL_WIDTH), F32),
        jax.ShapeDtypeStruct((bsz, GLA_HEADS, GLA_DK, GLA_DV), F32),
    )


def _mixer_prompt(x, weights):
    bsz, seq, _ = x.shape
    tiles = seq // TILE_TOKENS
    n_total = bsz * seq
    return pl.pallas_call(
        functools.partial(_mixer_kernel, False, 0),
        grid=(bsz, tiles),
        in_specs=[pl.BlockSpec((None, TILE_TOKENS, D_MODEL), lambda b, t: (b, t, 0))] + _mixer_weight_specs(),
        out_specs=(
            pl.BlockSpec((TILE_TOKENS, D_MODEL), lambda b, t: (b * tiles + t, 0)),
            pl.BlockSpec((TILE_TOKENS, HALF), lambda b, t: (b * tiles + t, 0)),
            pl.BlockSpec((NUM_EXPERTS, TILE_TOKENS), lambda b, t: (0, b * tiles + t)),
            pl.BlockSpec((None, POOL_HIST, POOL_WIDTH), lambda b, t: (b, 0, 0)),
            pl.BlockSpec((None, GLA_HEADS, GLA_DK, GLA_DV), lambda b, t: (b, 0, 0, 0)),
        ),
        out_shape=_mixer_out_shapes(n_total, bsz),
        scratch_shapes=_mixer_scratch(False),
        compiler_params=pltpu.CompilerParams(
            dimension_semantics=("arbitrary", "arbitrary"), vmem_limit_bytes=VMEM_LIMIT),
        name="mixer_prompt",
    )(x, *weights)


def _mixer_sample(x, hist, state, weights):
    bsz = x.shape[0]
    tiles = bsz // CHUNKS_PER_TILE
    return pl.pallas_call(
        functools.partial(_mixer_kernel, True, PAST_LEN),
        grid=(tiles,),
        in_specs=[
            pl.BlockSpec((CHUNKS_PER_TILE, CHUNK, D_MODEL), lambda i: (i, 0, 0)),
            pl.BlockSpec((CHUNKS_PER_TILE, POOL_HIST, POOL_WIDTH), lambda i: (i, 0, 0)),
            pl.BlockSpec((CHUNKS_PER_TILE, GLA_HEADS, GLA_DK, GLA_DV), lambda i: (i, 0, 0, 0)),
        ] + _mixer_weight_specs(),
        out_specs=(
            pl.BlockSpec((TILE_TOKENS, D_MODEL), lambda i: (i, 0)),
            pl.BlockSpec((TILE_TOKENS, HALF), lambda i: (i, 0)),
            pl.BlockSpec((NUM_EXPERTS, TILE_TOKENS), lambda i: (0, i)),
            pl.BlockSpec((CHUNKS_PER_TILE, POOL_HIST, POOL_WIDTH), lambda i: (i, 0, 0)),
            pl.BlockSpec((CHUNKS_PER_TILE, GLA_HEADS, GLA_DK, GLA_DV), lambda i: (i, 0, 0, 0)),
        ),
        out_shape=_mixer_out_shapes(bsz * CHUNK, bsz),
        scratch_shapes=_mixer_scratch(True),
        compiler_params=pltpu.CompilerParams(
            dimension_semantics=("arbitrary",), vmem_limit_bytes=VMEM_LIMIT),
        name="mixer_sample",
    )(x, hist, state, *weights)


def _router_kernel(lt_ref, gates_ref, dest_ref, padend_ref, cnt_scr, base_scr, pstart_scr, before_scr,
                   topk_scr):
    phase = pl.program_id(0)
    i = pl.program_id(1)
    shape = (NUM_EXPERTS, ROUTE_TOKENS)
    row = lax.broadcasted_iota(jnp.int32, shape, 0)

    def tile_counts_of(chosen):
        return jnp.broadcast_to(jnp.sum(chosen, axis=1, keepdims=True), (NUM_EXPERTS, LANES))

    @pl.when(phase == 0)
    def _():
        @pl.when(i == 0)
        def _():
            cnt_scr[...] = jnp.zeros_like(cnt_scr)

        logits = lt_ref[...]
        idxs, vals = [], []
        chosen = jnp.zeros(shape, F32)
        for _ in range(TOP_K):
            m = jnp.max(logits, axis=0, keepdims=True)
            idx = jnp.min(jnp.where(logits == m, row, NUM_EXPERTS), axis=0, keepdims=True)
            hit = row == idx
            idxs.append(idx.astype(F32))
            vals.append(m)
            chosen = chosen + jnp.where(hit, 1.0, 0.0)
            logits = jnp.where(hit, -jnp.inf, logits)
        topk_scr[i] = jnp.concatenate(idxs + vals, axis=0)
        cnt_scr[...] += tile_counts_of(chosen)

    @pl.when(phase == 1)
    def _():
        @pl.when(i == 0)
        def _():
            blocks = jnp.floor((cnt_scr[...] + (MOE_ROWS - 1)) * (1.0 / MOE_ROWS))
            erow = lax.broadcasted_iota(jnp.int32, (NUM_EXPERTS, LANES), 0)
            lane = lax.broadcasted_iota(jnp.int32, (NUM_EXPERTS, LANES), 1)
            cum = blocks
            shift = 1
            while shift < NUM_EXPERTS:
                cum = cum + jnp.where(erow >= shift, pltpu.roll(cum, shift, 0), 0.0)
                shift *= 2
            padend_ref[...] = jnp.where(lane == 1, cnt_scr[...], cum * MOE_ROWS)
            pstart_scr[...] = (cum - blocks) * MOE_ROWS
            base_scr[...] = jnp.zeros_like(base_scr)
            ti = lax.broadcasted_iota(jnp.int32, (ROUTE_TOKENS, ROUTE_TOKENS), 0)
            tj = lax.broadcasted_iota(jnp.int32, (ROUTE_TOKENS, ROUTE_TOKENS), 1)
            before_scr[...] = jnp.where(ti < tj, 1.0, 0.0).astype(BF16)

        topk = topk_scr[i]
        sel = [row == topk[k:k + 1, :].astype(jnp.int32) for k in range(TOP_K)]
        vals = [topk[TOP_K + k:TOP_K + k + 1, :] for k in range(TOP_K)]
        chosen = sum(jnp.where(hit, 1.0, 0.0) for hit in sel)
        earlier = _dot(chosen.astype(BF16), before_scr[...])
        pos = pstart_scr[:, 0:1] + base_scr[:, 0:1] + earlier
        dest = [jnp.sum(jnp.where(hit, pos, 0.0), axis=0, keepdims=True) for hit in sel]
        dest_ref[...] = jnp.concatenate(dest, axis=0).astype(jnp.int32)
        ex = [jnp.exp(v - vals[0]) for v in vals]
        denom = ex[0] + ex[1] + ex[2] + ex[3]
        gates_ref[...] = jnp.concatenate([e / denom for e in ex], axis=0)
        base_scr[...] += tile_counts_of(chosen)


def _router(logits_t):
    n = logits_t.shape[1]
    assert n % ROUTE_TOKENS == 0
    tiles = n // ROUTE_TOKENS
    return pl.pallas_call(
        _router_kernel,
        grid=(2, tiles),
        in_specs=[pl.BlockSpec((NUM_EXPERTS, ROUTE_TOKENS), lambda p, i: (0, i))],
        out_specs=(
            pl.BlockSpec((TOP_K, ROUTE_TOKENS), lambda p, i: (0, i * p)),
            pl.BlockSpec((TOP_K, ROUTE_TOKENS), lambda p, i: (0, i * p)),
            pl.BlockSpec((NUM_EXPERTS, LANES), lambda p, i: (0, 0)),
        ),
        out_shape=(
            jax.ShapeDtypeStruct((TOP_K, n), F32),
            jax.ShapeDtypeStruct((TOP_K, n), jnp.int32),
            jax.ShapeDtypeStruct((NUM_EXPERTS, LANES), F32),
        ),
        scratch_shapes=[pltpu.VMEM((NUM_EXPERTS, LANES), F32)] * 3
        + [pltpu.VMEM((ROUTE_TOKENS, ROUTE_TOKENS), BF16),
           pltpu.VMEM((tiles, 2 * TOP_K, ROUTE_TOKENS), F32)],
        compiler_params=pltpu.CompilerParams(
            dimension_semantics=("arbitrary", "arbitrary"), vmem_limit_bytes=VMEM_LIMIT),
        name="router",
    )(logits_t)


SC_CORES = 2
SC_SUBCORES = 16
SC_WORKERS = SC_CORES * SC_SUBCORES
DISPATCH_ROWS = 64


def _dispatch(h_sources, dest_chunks, m_pad):
    shares, first_chunk = [], 0
    for src in h_sources:
        n_chunks = src.shape[0] // DISPATCH_ROWS
        assert src.shape[0] % DISPATCH_ROWS == 0 and n_chunks % SC_WORKERS == 0
        shares.append((first_chunk, n_chunks // SC_WORKERS))
        first_chunk += n_chunks
    plan = [(s, first, per, j) for s, (first, per) in enumerate(shares) for j in range(per)]
    per_worker = len(plan)
    mesh = plsc.VectorSubcoreMesh(core_axis_name="c", subcore_axis_name="s")

    @functools.partial(
        pl.kernel, mesh=mesh,
        out_type=jax.ShapeDtypeStruct((m_pad, HALF), jnp.int32),
        scratch_types=[
            pltpu.VMEM((2, TOP_K, DISPATCH_ROWS), jnp.int32),
            pltpu.VMEM((2, DISPATCH_ROWS, HALF), jnp.int32),
            pltpu.SemaphoreType.DMA((2,)),
            pltpu.SemaphoreType.DMA((2,)),
        ],
        compiler_params=pltpu.CompilerParams(use_tc_tiling_on_sc=True),
        name="dispatch",
    )
    def dispatch_kernel(*refs):
        h_hbms = refs[:len(h_sources)]
        dest_hbm, out_hbm, idx_v, rows_v, load_sems, scatter_sems = refs[len(h_sources):]
        wid = lax.axis_index("s") * SC_CORES + lax.axis_index("c")

        def loads(j):
            src, first, per, k = plan[j]
            local = wid * per + k
            slot = j % 2
            return (
                pltpu.make_async_copy(dest_hbm.at[first + local], idx_v.at[slot], load_sems.at[slot]),
                pltpu.make_async_copy(h_hbms[src].at[pl.ds(local * DISPATCH_ROWS, DISPATCH_ROWS)],
                                      rows_v.at[slot], load_sems.at[slot]),
            )

        def scatters(j):
            slot = j % 2
            return [pltpu.make_async_copy(rows_v.at[slot], out_hbm.at[idx_v.at[slot, k]],
                                          scatter_sems.at[slot]) for k in range(TOP_K)]

        for cp in loads(0):
            cp.start()
        for j in range(per_worker):
            for cp in loads(j):
                cp.wait()
            if j >= 1:
                for cp in scatters(j - 1):
                    cp.wait()
            if j + 1 < per_worker:
                for cp in loads(j + 1):
                    cp.start()
            for cp in scatters(j):
                cp.start()
        for cp in scatters(per_worker - 1):
            cp.wait()

    return dispatch_kernel(*h_sources, dest_chunks)


def _gather_expert_rows(y_sorted, dest_chunks, row_offset, n):
    n_chunks = n // DISPATCH_ROWS
    assert n_chunks % SC_WORKERS == 0 and row_offset % DISPATCH_ROWS == 0
    per_worker = n_chunks // SC_WORKERS
    chunk0 = row_offset // DISPATCH_ROWS
    mesh = plsc.VectorSubcoreMesh(core_axis_name="c", subcore_axis_name="s")

    @functools.partial(
        pl.kernel, mesh=mesh,
        out_type=jax.ShapeDtypeStruct((TOP_K, n, HALF), jnp.int32),
        scratch_types=[
            pltpu.VMEM((TOP_K, DISPATCH_ROWS), jnp.int32),
            pltpu.VMEM((2, DISPATCH_ROWS, HALF), jnp.int32),
            pltpu.SemaphoreType.DMA((2,)),
        ],
        compiler_params=pltpu.CompilerParams(use_tc_tiling_on_sc=True),
        name="gather_expert_rows",
    )
    def gather_kernel(y_hbm, dest_hbm, out_hbm, idx_v, rows_v, sems):
        wid = lax.axis_index("s") * SC_CORES + lax.axis_index("c")

        def gather(k):
            return pltpu.make_async_copy(y_hbm.at[idx_v.at[k]], rows_v.at[k % 2], sems.at[k % 2])

        @pl.loop(0, per_worker)
        def _(j):
            local = wid * per_worker + j
            pltpu.sync_copy(dest_hbm.at[chunk0 + local], idx_v)
            gather(0).start()
            for k in range(TOP_K):
                if k + 1 < TOP_K:
                    gather(k + 1).start()
                gather(k).wait()
                pltpu.sync_copy(rows_v.at[k % 2],
                                out_hbm.at[k, pl.ds(local * DISPATCH_ROWS, DISPATCH_ROWS)])

    return gather_kernel(y_sorted, dest_chunks)


def _moe_kernel(be_ref, wsel_ref, units_ref, nused_ref, x_ref, wgu_ref, bgu_ref, wd_ref, bd_ref, y_ref,
                wgu_bf, wd_bf):
    del wsel_ref
    i = pl.program_id(0)

    def ffn(rows):
        gu = _dot(_unpack_bf16_pairs(x_ref[rows, :]), wgu_bf[...]) + bgu_ref[...]
        gate = jnp.minimum(gu[:, :EXPERT_FF], SWIGLU_LIMIT)
        up = jnp.clip(gu[:, EXPERT_FF:], -SWIGLU_LIMIT, SWIGLU_LIMIT)
        hmid = gate * (1.0 / (1.0 + jnp.exp(-SWIGLU_ALPHA * gate))) * (up + 1.0)
        y = _dot(hmid.astype(BF16), wd_bf[...]) + bd_ref[...]
        y_ref[rows, :] = _pack_bf16_pairs(y.astype(BF16))

    @pl.when(i < nused_ref[0])
    def _():
        @pl.when((i == 0) | (be_ref[i] != be_ref[jnp.maximum(i - 1, 0)]))
        def _():
            wgu_bf[...] = wgu_ref[...].astype(BF16)
            wd_bf[...] = wd_ref[...].astype(BF16)

        for units in range(1, MOE_ROWS // MOE_TAIL_ROWS + 1):
            @pl.when(units_ref[i] == units)
            def _(units=units):
                ffn(slice(0, units * MOE_TAIL_ROWS))


def _moe_experts(block_expert, weight_expert, block_units, n_used, x_sorted, w_gu, b_gu, w_down, b_down):
    m_pad = x_sorted.shape[0]
    n_blocks = m_pad // MOE_ROWS

    def blk(i, be, ws, hb, nu):
        return jnp.minimum(i, nu[0] - 1)

    def expert(i, be, ws, hb, nu):
        return be[blk(i, be, ws, hb, nu)]

    def held(i, be, ws, hb, nu):
        return ws[blk(i, be, ws, hb, nu)]

    grid_spec = pltpu.PrefetchScalarGridSpec(
        num_scalar_prefetch=4,
        grid=(n_blocks,),
        in_specs=[
            pl.BlockSpec((MOE_ROWS, HALF), lambda *a: (blk(*a), 0)),
            pl.BlockSpec((None, D_MODEL, 2 * EXPERT_FF), lambda *a: (held(*a), 0, 0)),
            pl.BlockSpec((None, 1, 2 * EXPERT_FF), lambda *a: (expert(*a), 0, 0)),
            pl.BlockSpec((None, EXPERT_FF, D_MODEL), lambda *a: (held(*a), 0, 0)),
            pl.BlockSpec((None, 1, D_MODEL), lambda *a: (expert(*a), 0, 0)),
        ],
        out_specs=pl.BlockSpec((MOE_ROWS, HALF), lambda *a: (blk(*a), 0)),
        scratch_shapes=[
            pltpu.VMEM((D_MODEL, 2 * EXPERT_FF), BF16),
            pltpu.VMEM((EXPERT_FF, D_MODEL), BF16),
        ],
    )
    return pl.pallas_call(
        _moe_kernel,
        grid_spec=grid_spec,
        out_shape=jax.ShapeDtypeStruct((m_pad, HALF), jnp.int32),
        compiler_params=pltpu.CompilerParams(
            dimension_semantics=("arbitrary",), vmem_limit_bytes=VMEM_LIMIT),
        name="moe_experts",
    )(block_expert, weight_expert, block_units, n_used, x_sorted, w_gu, b_gu, w_down, b_down)


def _combine_kernel(yk_ref, gates_ref, h_ref, g_ref, b_ref, out_ref):
    pad = jnp.zeros((LANES - TOP_K, COMBINE_TOKENS), F32)
    gates = jnp.transpose(jnp.concatenate([gates_ref[...], pad], axis=0))
    lo = hi = None
    for k in range(TOP_K):
        u = lax.bitcast_convert_type(yk_ref[k], jnp.uint32)
        gk = gates[:, k:k + 1]
        lo_k = lax.bitcast_convert_type(u << 16, F32) * gk
        hi_k = lax.bitcast_convert_type(u & jnp.uint32(HI_MASK), F32) * gk
        lo = lo_k if lo is None else lo + lo_k
        hi = hi_k if hi is None else hi + hi_k
    acc = ALPHA * h_ref[...] + jnp.concatenate([lo, hi], axis=1)
    out_ref[...] = _layer_norm(acc, g_ref[...], b_ref[...])


def _combine_kernel_aliased(yk_ref, gates_ref, h_ref, g_ref, b_ref, prev_ref, out_ref):
    del prev_ref
    _combine_kernel(yk_ref, gates_ref, h_ref, g_ref, b_ref, out_ref)


def _combine(yk, gates, token_offset, h_src, h_offset, ln_g, ln_b, out_prev):
    n_seg = yk.shape[1]
    out_rows = h_src.shape[0]
    tile0 = token_offset // COMBINE_TOKENS
    out_tile0 = h_offset // COMBINE_TOKENS
    in_specs = [
        pl.BlockSpec((TOP_K, COMBINE_TOKENS, HALF), lambda i: (0, i, 0)),
        pl.BlockSpec((TOP_K, COMBINE_TOKENS), lambda i: (0, tile0 + i)),
        pl.BlockSpec((COMBINE_TOKENS, D_MODEL), lambda i: (out_tile0 + i, 0)),
        _const_spec((1, D_MODEL)),
        _const_spec((1, D_MODEL)),
    ]
    args = [yk, gates, h_src, ln_g, ln_b]
    aliases = {}
    kern = _combine_kernel
    if out_prev is not None:
        in_specs.append(pl.BlockSpec(memory_space=pl.ANY))
        aliases = {len(args): 0}
        args.append(out_prev)
        kern = _combine_kernel_aliased
    return pl.pallas_call(
        kern,
        grid=(n_seg // COMBINE_TOKENS,),
        in_specs=in_specs,
        out_specs=pl.BlockSpec((COMBINE_TOKENS, D_MODEL), lambda i: (out_tile0 + i, 0)),
        out_shape=jax.ShapeDtypeStruct((out_rows, D_MODEL), F32),
        input_output_aliases=aliases,
        compiler_params=pltpu.CompilerParams(
            dimension_semantics=("arbitrary",), vmem_limit_bytes=VMEM_LIMIT),
        name="moe_combine",
    )(*args)


def kernel(x_prompt, x_sample, state_pool, state_gla, w_in, w_pool, pool_scale, w_gate_up, b_gate,
           gla_norm_w, w_out, ln1_g, ln1_b, w_router, b_router, w_gu, b_gu, w_down, b_down,
           ln2_g, ln2_b):
    assert w_in.shape[0] == 1, "single-layer kernel"
    bp, seq, _ = x_prompt.shape
    bs, dec_seq, _ = x_sample.shape
    assert dec_seq == CHUNK and seq % TILE_TOKENS == 0 and bs % CHUNKS_PER_TILE == 0
    n_prompt = bp * seq
    n_sample = bs * dec_seq
    n_total = n_prompt + n_sample
    nk = n_total * TOP_K
    n_blocks = -(-nk // MOE_ROWS) + NUM_EXPERTS
    m_pad = n_blocks * MOE_ROWS

    weights = _mixer_weights(w_in, w_pool, pool_scale, w_gate_up, b_gate, gla_norm_w, w_out,
                             ln1_g, ln1_b, w_router, b_router)

    h_p, hb_p, logits_p, hist_p, s_p = _mixer_prompt(x_prompt, weights)
    h_s, hb_s, logits_s, hist_s, s_s = _mixer_sample(x_sample, state_pool[0], state_gla[0], weights)

    gates_t, dest_t, layout = _router(jnp.concatenate([logits_p, logits_s], axis=1))
    pad_end = layout[:, 0].astype(jnp.int32)
    counts = layout[:, 1].astype(jnp.int32)
    block_start = jnp.arange(n_blocks, dtype=jnp.int32) * MOE_ROWS
    block_expert = jnp.minimum(jnp.sum((block_start[:, None] >= pad_end[None, :]).astype(jnp.int32), axis=1),
                               NUM_EXPERTS - 1)
    n_used = (pad_end[-1:] // MOE_ROWS).astype(jnp.int32)
    is_first = jnp.concatenate([jnp.ones((1,), bool), block_expert[1:] != block_expert[:-1]])
    blocks = jnp.arange(n_blocks, dtype=jnp.int32)
    later_other = ((blocks[None, :] > blocks[:, None]) & (blocks[None, :] < n_used[0])
                   & (block_expert[None, :] != block_expert[:, None]))
    next_expert = jnp.min(jnp.where(later_other, block_expert[None, :], NUM_EXPERTS), axis=1)
    next_expert = jnp.where(next_expert == NUM_EXPERTS, block_expert, next_expert)
    weight_expert = jnp.where(is_first, block_expert, next_expert).astype(jnp.int32)
    of_expert = block_expert[:, None] == jnp.arange(NUM_EXPERTS, dtype=jnp.int32)[None, :]
    seg_end = jnp.sum(jnp.where(of_expert, (pad_end - (-counts % MOE_ROWS))[None, :], 0), axis=1)
    block_units = jnp.clip((seg_end - block_start + MOE_TAIL_ROWS - 1) // MOE_TAIL_ROWS,
                          1, MOE_ROWS // MOE_TAIL_ROWS).astype(jnp.int32)

    dest_chunks = dest_t.reshape(TOP_K, n_total // DISPATCH_ROWS, DISPATCH_ROWS).transpose(1, 0, 2)
    x_sorted = _dispatch((hb_p, hb_s), dest_chunks, m_pad)
    y_sorted = _moe_experts(block_expert, weight_expert, block_units, n_used, x_sorted, w_gu[0], b_gu[0][:, None, :],
                            w_down[0], b_down[0][:, None, :])
    gates = gates_t
    ln_g, ln_b = ln2_g[0][None, :], ln2_b[0][None, :]
    yk = _gather_expert_rows(y_sorted, dest_chunks, n_prompt, n_sample)
    y_sample = _combine(yk, gates, n_prompt, h_s, 0, ln_g, ln_b, None)
    unit = SC_WORKERS * DISPATCH_ROWS
    assert n_prompt % unit == 0
    sizes, left = [], n_prompt // unit
    while left > 0:
        size = min(left, 1 if len(sizes) < 2 else 2)
        sizes.append(size * unit)
        left -= size
    y_prompt, start = None, 0
    for seg in sizes:
        yk = _gather_expert_rows(y_sorted, dest_chunks, start, seg)
        y_prompt = _combine(yk, gates, start, h_p, start, ln_g, ln_b, y_prompt)
        start += seg
    y_prompt = y_prompt.reshape(bp, seq, D_MODEL)
    y_sample = y_sample.reshape(bs, dec_seq, D_MODEL)
    return (y_prompt, y_sample, hist_p[None], s_p[None], hist_s[None], s_s[None])
```

```python
import functools

import jax
import jax.numpy as jnp
from jax import lax
from jax.experimental import pallas as pl
from jax.experimental.pallas import tpu as pltpu
from jax.experimental.pallas import tpu_sc as plsc

F32 = jnp.float32
BF16 = jnp.bfloat16

D_MODEL = 1024
CHUNK = 64
PAST_LEN = 1024
POOL_WIDTH = 512
POOL_WINDOWS = (2, 4, 8, 16)
POOL_GROUP = 128
POOL_HIST = 15
GLA_HEADS = 4
GLA_DK = 64
GLA_DV = 128
GATE_RANK = 16
GATE_NORMALIZER = 16.0
NUM_EXPERTS = 32
TOP_K = 4
EXPERT_FF = 1024
SWIGLU_LIMIT = 7.0
SWIGLU_ALPHA = 1.702
LN_EPS = 1e-5
RMS_EPS = 1e-6
ALPHA = 2.0 ** 0.25

Q0 = POOL_WIDTH
K0 = Q0 + GLA_HEADS * GLA_DK
V0 = K0 + GLA_HEADS * GLA_DK
R0 = V0 + GLA_HEADS * GLA_DV
N_MAIN = R0 + GLA_HEADS * GLA_DV
N_IN = N_MAIN + GATE_RANK

LANES = 128
TILE_TOKENS = 512
CHUNKS_PER_TILE = TILE_TOKENS // CHUNK
HIST_PAD = 16
SCAN_ROWS = 256
GLA_SAFE_EXPONENT = 60.0
MOE_ROWS = 1024
MOE_TAIL_ROWS = 256
ROUTE_TOKENS = 2048
COMBINE_TOKENS = 1024
VMEM_LIMIT = 56 * 1024 * 1024


def _dot(a, b):
    return jnp.dot(a, b, preferred_element_type=F32)


def _dot_nt(a, b):
    return lax.dot_general(a, b, (((1,), (1,)), ((), ())), preferred_element_type=F32)


HALF = D_MODEL // 2
HI_MASK = 0xFFFF0000


def _pack_bf16_pairs(xb):
    lo = lax.bitcast_convert_type(xb[:, :HALF].astype(F32), jnp.uint32) >> 16
    hi = lax.bitcast_convert_type(xb[:, HALF:].astype(F32), jnp.uint32) & jnp.uint32(HI_MASK)
    return lax.bitcast_convert_type(hi | lo, jnp.int32)


def _unpack_bf16_pairs(p):
    u = lax.bitcast_convert_type(p, jnp.uint32)
    lo = lax.bitcast_convert_type(u << 16, F32)
    hi = lax.bitcast_convert_type(u & jnp.uint32(HI_MASK), F32)
    return jnp.concatenate([lo, hi], axis=1).astype(BF16)


def _layer_norm(v, g, b):
    mu = jnp.mean(v, axis=-1, keepdims=True)
    c = v - mu
    var = jnp.mean(c * c, axis=-1, keepdims=True)
    return c * lax.rsqrt(var + LN_EPS) * g + b


N_MIXER_WEIGHTS = 12


def _mixer_kernel(per_chunk_state, pos0, *refs):
    if per_chunk_state:
        (x_ref, hist_in_ref, s_in_ref, *rest) = refs
    else:
        (x_ref, *rest) = refs
        hist_in_ref = s_in_ref = None
    (w_main_ref, w_glr_ref, w_gate_ref, b_gate_ref, w_pool_ref, pscale_ref, gnorm_ref,
     w_out_ref, ln1g_ref, ln1b_ref, w_router_ref, b_router_ref, *rest) = rest
    (h_ref, hb_ref, logits_ref, hist_out_ref, s_out_ref,
     proj_scr, b_scr, o_scr, ext_scr, st_scr, tri_scr, w_main_bf, w_out_bf, st_in_scr,
     pool_scr) = rest

    if per_chunk_state:
        t = None
        first_step = pl.program_id(0) == 0
    else:
        t = pl.program_id(1)
        first_step = (pl.program_id(0) == 0) & (t == 0)
    def load_x():
        return x_ref[...].reshape(TILE_TOKENS, D_MODEL)

    xb = load_x().astype(BF16)

    @pl.when(first_step)
    def _():
        ti = lax.broadcasted_iota(jnp.int32, (SCAN_ROWS, SCAN_ROWS), 0)
        tj = lax.broadcasted_iota(jnp.int32, (SCAN_ROWS, SCAN_ROWS), 1)
        same_chunk = (ti // CHUNK) == (tj // CHUNK)
        tri_scr[...] = jnp.where(same_chunk & (ti >= tj), 1.0, 0.0).astype(BF16)
        w_main_bf[...] = w_main_ref[:, 0:N_MAIN].astype(BF16)
        w_out_bf[...] = w_out_ref[...].astype(BF16)
        if not per_chunk_state:
            st_scr[...] = jnp.zeros_like(st_scr)
            ext_scr[0:HIST_PAD, :] = jnp.zeros((HIST_PAD, POOL_WIDTH), F32)

    glr = _dot(xb, w_glr_ref[...])
    proj_scr[:, 0:V0] = _dot(xb, w_main_bf[:, 0:V0])
    gk = _dot(glr.astype(BF16), w_gate_ref[...]) + b_gate_ref[...]
    log_sig = jnp.minimum(gk, 0.0) - jnp.log1p(jnp.exp(-jnp.abs(gk)))
    g = log_sig / GATE_NORMALIZER
    g_hi = g.astype(BF16)
    g_lo = (g - g_hi.astype(F32)).astype(BF16)
    proj_scr[:, V0:N_MAIN] = _dot(xb, w_main_bf[:, V0:N_MAIN])
    for s in range(TILE_TOKENS // SCAN_ROWS):
        rs = slice(s * SCAN_ROWS, (s + 1) * SCAN_ROWS)
        b_scr[rs, :] = _dot(tri_scr[...], g_hi[rs]) + _dot(tri_scr[...], g_lo[rs])

    if per_chunk_state:
        seg_len, seg_stride, n_seg = CHUNK, CHUNK + HIST_PAD, CHUNKS_PER_TILE
        for c in range(n_seg):
            base = c * seg_stride
            ext_scr[base:base + HIST_PAD, :] = jnp.zeros((HIST_PAD, POOL_WIDTH), F32)
            ext_scr[base + 1:base + HIST_PAD, :] = hist_in_ref[c]
            ext_scr[base + HIST_PAD:base + seg_stride, :] = proj_scr[c * CHUNK:(c + 1) * CHUNK, 0:POOL_WIDTH]
        row_pos = pos0 + lax.broadcasted_iota(jnp.int32, (seg_len, POOL_GROUP), 0)
    else:
        seg_len, seg_stride, n_seg = TILE_TOKENS, TILE_TOKENS + HIST_PAD, 1
        ext_scr[0:HIST_PAD, :] = jnp.where(t == 0, 0.0, ext_scr[0:HIST_PAD, :])
        ext_scr[HIST_PAD:seg_stride, :] = proj_scr[:, 0:POOL_WIDTH]
        row_pos = pos0 + t * TILE_TOKENS + lax.broadcasted_iota(jnp.int32, (seg_len, POOL_GROUP), 0)

    pooled_groups = []
    for gi, w in enumerate(POOL_WINDOWS):
        gs = slice(gi * POOL_GROUP, (gi + 1) * POOL_GROUP)
        cnt = jnp.minimum(row_pos + 1, w).astype(F32)
        ext = ext_scr[:, gs]
        win = ext
        shift = 1
        while shift < w:
            win = win + pltpu.roll(win, shift, 0)
            shift *= 2
        segs = []
        for s in range(n_seg):
            base = s * seg_stride + HIST_PAD
            segs.append(win[base:base + seg_len] / cnt - ext[base:base + seg_len])
        pooled = segs[0] if n_seg == 1 else jnp.concatenate(segs, axis=0)
        pooled_groups.append(pooled.astype(BF16))
    pool_cols = []
    for p in range(len(POOL_WINDOWS) // 2):
        both = jnp.concatenate(pooled_groups[2 * p:2 * p + 2], axis=1)
        pool_cols.append(_dot(both, w_pool_ref[p]))
    pool_scr[...] = jnp.concatenate(pool_cols, axis=1) * pscale_ref[...]

    if per_chunk_state:
        for c in range(n_seg):
            end = (c + 1) * seg_stride
            hist_out_ref[c] = ext_scr[end - POOL_HIST:end, :]
    else:
        hist_out_ref[...] = ext_scr[seg_stride - POOL_HIST:seg_stride, :]
        ext_scr[0:HIST_PAD, :] = ext_scr[TILE_TOKENS:seg_stride, :]

    hk = GLA_HEADS * GLA_DK
    hv = GLA_HEADS * GLA_DV
    pair_rows = 2 * CHUNK
    decay_cols = LANES // CHUNKS_PER_TILE

    def head_of(shape, dim, width):
        return lax.broadcasted_iota(jnp.int32, shape, dim) // width

    same_head_k = head_of((hk, hk), 0, CHUNK) == head_of((hk, hk), 1, GLA_DK)
    same_head_v = head_of((hk, hv), 0, CHUNK) == head_of((hk, hv), 1, GLA_DV)
    pair_half = head_of((pair_rows, hv), 0, CHUNK)
    causal = (lax.broadcasted_iota(jnp.int32, (CHUNK, hk), 0)
              >= lax.broadcasted_iota(jnp.int32, (CHUNK, hk), 1) % CHUNK)

    b_all = b_scr[...]
    b_last = [b_scr[(c + 1) * CHUNK - 1:(c + 1) * CHUNK, :] for c in range(CHUNKS_PER_TILE)]
    b_last_rows = jnp.concatenate([jnp.broadcast_to(bl, (CHUNK, hk)) for bl in b_last], axis=0)
    k_all = proj_scr[:, K0:V0]
    qt_all = (proj_scr[:, Q0:K0] * (GLA_DK ** -0.5) * jnp.exp(b_all)).astype(BF16)
    kt_all = k_all * jnp.exp(-b_all)
    kl_t = jnp.transpose(k_all * jnp.exp(b_last_rows - b_all)).astype(BF16)
    decay_t = jnp.transpose(jnp.exp(jnp.concatenate(
        [jnp.broadcast_to(bl, (decay_cols, hk)) for bl in b_last], axis=0)))

    def finish_rows(rs):
        r = proj_scr[rs, R0:N_MAIN]
        silu_r = r * (1.0 / (1.0 + jnp.exp(-r)))
        gated = []
        for h in range(GLA_HEADS):
            vs = slice(h * GLA_DV, (h + 1) * GLA_DV)
            oh = o_scr[rs, vs]
            ms = jnp.mean(oh * oh, axis=-1, keepdims=True)
            gated.append(oh * lax.rsqrt(ms + RMS_EPS) * gnorm_ref[...] * silu_r[:, vs])
        mix_in = jnp.concatenate([pool_scr[rs, :]] + gated, axis=1).astype(BF16)
        resid = ALPHA * load_x()[rs] + _dot(mix_in, w_out_bf[...])
        h_val = _layer_norm(resid, ln1g_ref[...], ln1b_ref[...])
        h_ref[rs, :] = h_val
        hb = h_val.astype(BF16)
        hb_ref[rs, :] = _pack_bf16_pairs(hb)
        logits_ref[:, rs] = _dot_nt(w_router_ref[...], hb) + b_router_ref[:, 0:1]

    st = None if per_chunk_state else jnp.where(t == 0, 0.0, st_scr[...])
    for c in range(CHUNKS_PER_TILE):
        rows = slice(c * CHUNK, (c + 1) * CHUNK)
        pair = slice((c // 2) * pair_rows, (c // 2 + 1) * pair_rows)
        if per_chunk_state:
            st = s_in_ref[c].reshape(hk, GLA_DV)
        qt = qt_all[rows]
        zero = jnp.zeros((), BF16)
        k_stack = jnp.where(same_head_k, jnp.concatenate([kt_all[rows].astype(BF16)] * GLA_HEADS, axis=0), zero)
        v_stack = jnp.where(same_head_v, jnp.concatenate(
            [proj_scr[rows, V0:R0].astype(BF16)] * GLA_HEADS, axis=0), zero)
        s_stack = jnp.where(same_head_v, jnp.concatenate([st.astype(BF16)] * GLA_HEADS, axis=1), zero)
        att = jnp.where(causal, _dot_nt(qt, k_stack), 0.0)
        st_in_scr[c] = st
        o_scr[rows, :] = _dot(att.astype(BF16), v_stack) + _dot(qt, s_stack)
        v_chunk = jnp.where(pair_half == c % 2, proj_scr[pair, V0:R0], 0.0).astype(BF16)
        upd = jnp.concatenate(
            [_dot(kl_t[h * GLA_DK:(h + 1) * GLA_DK, pair], v_chunk[:, h * GLA_DV:(h + 1) * GLA_DV])
             for h in range(GLA_HEADS)], axis=0)
        st = st * decay_t[:, c * decay_cols:c * decay_cols + 1] + upd
        if per_chunk_state:
            s_out_ref[c] = st.reshape(GLA_HEADS, GLA_DK, GLA_DV)

    finish_rows(slice(0, TILE_TOKENS))
    if not per_chunk_state:
        st_scr[...] = st
        s_out_ref[...] = st.reshape(GLA_HEADS, GLA_DK, GLA_DV)

    @pl.when(jnp.max(-b_all) > GLA_SAFE_EXPONENT)
    def _():
        head_sum = jnp.where(head_of((hk, LANES), 0, GLA_DK)
                             == lax.broadcasted_iota(jnp.int32, (hk, LANES), 1), 1.0, 0.0).astype(BF16)
        head_expand = jnp.where(lax.broadcasted_iota(jnp.int32, (LANES, hv), 0)
                                == head_of((LANES, hv), 1, GLA_DV), 1.0, 0.0).astype(BF16)
        query_row = lax.broadcasted_iota(jnp.int32, (CHUNK, hv), 0)
        for c in range(CHUNKS_PER_TILE):
            rows = slice(c * CHUNK, (c + 1) * CHUNK)
            q_c = proj_scr[rows, Q0:K0] * (GLA_DK ** -0.5)
            b_c = b_scr[rows, :]

            def add_key(j, acc, c=c, q_c=q_c, b_c=b_c):
                key = pl.ds(c * CHUNK + j, 1)
                decay = jnp.exp(jnp.minimum(b_c - b_scr[key, :], 0.0))
                w = q_c * proj_scr[key, K0:V0] * decay
                score = _dot(w.astype(BF16), head_sum)
                score = _dot(score.astype(BF16), head_expand)
                return acc + jnp.where(query_row >= j, score, 0.0) * proj_scr[key, V0:R0]

            o_intra = lax.fori_loop(0, CHUNK, add_key, jnp.zeros((CHUNK, hv), F32))
            s_stack = jnp.where(same_head_v, jnp.concatenate(
                [st_in_scr[c].astype(BF16)] * GLA_HEADS, axis=1), jnp.zeros((), BF16))
            o_scr[rows, :] = o_intra + _dot((q_c * jnp.exp(b_c)).astype(BF16), s_stack)
        finish_rows(slice(0, TILE_TOKENS))


def _const_spec(shape, single_buffer=False):
    nd = len(shape)
    if single_buffer:
        return pl.BlockSpec(shape, lambda *_: (0,) * nd, pipeline_mode=pl.Buffered(1))
    return pl.BlockSpec(shape, lambda *_: (0,) * nd)


def _mixer_weight_specs():
    return [
        _const_spec((None, D_MODEL, N_IN), single_buffer=True),
        _const_spec((D_MODEL, LANES)),
        _const_spec((LANES, GLA_HEADS * GLA_DK)),
        _const_spec((1, GLA_HEADS * GLA_DK)),
        _const_spec((len(POOL_WINDOWS) // 2, 2 * POOL_GROUP, 2 * POOL_GROUP)),
        _const_spec((1, POOL_WIDTH)),
        _const_spec((1, GLA_DV)),
        _const_spec((None, D_MODEL, D_MODEL), single_buffer=True),
        _const_spec((1, D_MODEL)),
        _const_spec((1, D_MODEL)),
        _const_spec((NUM_EXPERTS, D_MODEL)),
        _const_spec((NUM_EXPERTS, LANES)),
    ]


def _mixer_weights(w_in, w_pool, pool_scale, w_gate_up, b_gate, gla_norm_w, w_out, ln1_g, ln1_b,
                   w_router, b_router):
    w_glr = jnp.zeros((D_MODEL, LANES), BF16).at[:, :GATE_RANK].set(w_in[0, :, N_MAIN:].astype(BF16))
    w_gate = jnp.zeros((LANES, GLA_HEADS * GLA_DK), BF16).at[:GATE_RANK].set(w_gate_up[0].astype(BF16))
    wp = w_pool[0].astype(BF16)
    zero = jnp.zeros((POOL_GROUP, POOL_GROUP), BF16)
    w_pool_pairs = jnp.stack([jnp.block([[wp[2 * p], zero], [zero, wp[2 * p + 1]]])
                              for p in range(len(POOL_WINDOWS) // 2)])
    weights = (
        w_in, w_glr, w_gate, b_gate[0][None, :],
        w_pool_pairs, pool_scale[0][None, :], gla_norm_w[0][None, :],
        w_out, ln1_g[0][None, :], ln1_b[0][None, :],
        w_router[0].T.astype(BF16), jnp.broadcast_to(b_router[0][:, None], (NUM_EXPERTS, LANES)),
    )
    assert len(weights) == N_MIXER_WEIGHTS
    return weights


def _mixer_scratch(per_chunk_state):
    ext_rows = (CHUNKS_PER_TILE * (CHUNK + HIST_PAD)) if per_chunk_state else (TILE_TOKENS + HIST_PAD)
    return [
        pltpu.VMEM((TILE_TOKENS, N_MAIN), F32),
        pltpu.VMEM((TILE_TOKENS, GLA_HEADS * GLA_DK), F32),
        pltpu.VMEM((TILE_TOKENS, GLA_HEADS * GLA_DV), F32),
        pltpu.VMEM((ext_rows, POOL_WIDTH), F32),
        pltpu.VMEM((GLA_HEADS * GLA_DK, GLA_DV), F32),
        pltpu.VMEM((SCAN_ROWS, SCAN_ROWS), BF16),
        pltpu.VMEM((D_MODEL, N_MAIN), BF16),
        pltpu.VMEM((D_MODEL, D_MODEL), BF16),
        pltpu.VMEM((CHUNKS_PER_TILE, GLA_HEADS * GLA_DK, GLA_DV), F32),
        pltpu.VMEM((TILE_TOKENS, POOL_WIDTH), F32),
    ]


def _mixer_out_shapes(n, bsz):
    return (
        jax.ShapeDtypeStruct((n, D_MODEL), F32),
        jax.ShapeDtypeStruct((n, HALF), jnp.int32),
        jax.ShapeDtypeStruct((NUM_EXPERTS, n), F32),
        jax.ShapeDtypeStruct((bsz, POOL_HIST, POOL_WIDTH), F32),
        jax.ShapeDtypeStruct((bsz, GLA_HEADS, GLA_DK, GLA_DV), F32),
    )


def _mixer_prompt(x, weights):
    bsz, seq, _ = x.shape
    tiles = seq // TILE_TOKENS
    n_total = bsz * seq
    return pl.pallas_call(
        functools.partial(_mixer_kernel, False, 0),
        grid=(bsz, tiles),
        in_specs=[pl.BlockSpec((None, TILE_TOKENS, D_MODEL), lambda b, t: (b, t, 0))] + _mixer_weight_specs(),
        out_specs=(
            pl.BlockSpec((TILE_TOKENS, D_MODEL), lambda b, t: (b * tiles + t, 0)),
            pl.BlockSpec((TILE_TOKENS, HALF), lambda b, t: (b * tiles + t, 0)),
            pl.BlockSpec((NUM_EXPERTS, TILE_TOKENS), lambda b, t: (0, b * tiles + t)),
            pl.BlockSpec((None, POOL_HIST, POOL_WIDTH), lambda b, t: (b, 0, 0)),
            pl.BlockSpec((None, GLA_HEADS, GLA_DK, GLA_DV), lambda b, t: (b, 0, 0, 0)),
        ),
        out_shape=_mixer_out_shapes(n_total, bsz),
        scratch_shapes=_mixer_scratch(False),
        compiler_params=pltpu.CompilerParams(
            dimension_semantics=("arbitrary", "arbitrary"), vmem_limit_bytes=VMEM_LIMIT),
        name="mixer_prompt",
    )(x, *weights)


def _mixer_sample(x, hist, state, weights):
    bsz = x.shape[0]
    tiles = bsz // CHUNKS_PER_TILE
    return pl.pallas_call(
        functools.partial(_mixer_kernel, True, PAST_LEN),
        grid=(tiles,),
        in_specs=[
            pl.BlockSpec((CHUNKS_PER_TILE, CHUNK, D_MODEL), lambda i: (i, 0, 0)),
            pl.BlockSpec((CHUNKS_PER_TILE, POOL_HIST, POOL_WIDTH), lambda i: (i, 0, 0)),
            pl.BlockSpec((CHUNKS_PER_TILE, GLA_HEADS, GLA_DK, GLA_DV), lambda i: (i, 0, 0, 0)),
        ] + _mixer_weight_specs(),
        out_specs=(
            pl.BlockSpec((TILE_TOKENS, D_MODEL), lambda i: (i, 0)),
            pl.BlockSpec((TILE_TOKENS, HALF), lambda i: (i, 0)),
            pl.BlockSpec((NUM_EXPERTS, TILE_TOKENS), lambda i: (0, i)),
            pl.BlockSpec((CHUNKS_PER_TILE, POOL_HIST, POOL_WIDTH), lambda i: (i, 0, 0)),
            pl.BlockSpec((CHUNKS_PER_TILE, GLA_HEADS, GLA_DK, GLA_DV), lambda i: (i, 0, 0, 0)),
        ),
        out_shape=_mixer_out_shapes(bsz * CHUNK, bsz),
        scratch_shapes=_mixer_scratch(True),
        compiler_params=pltpu.CompilerParams(
            dimension_semantics=("arbitrary",), vmem_limit_bytes=VMEM_LIMIT),
        name="mixer_sample",
    )(x, hist, state, *weights)


def _router_kernel(lt_ref, gates_ref, dest_ref, padend_ref, cnt_scr, base_scr, pstart_scr, before_scr,
                   topk_scr):
    phase = pl.program_id(0)
    i = pl.program_id(1)
    shape = (NUM_EXPERTS, ROUTE_TOKENS)
    row = lax.broadcasted_iota(jnp.int32, shape, 0)

    def tile_counts_of(chosen):
        return jnp.broadcast_to(jnp.sum(chosen, axis=1, keepdims=True), (NUM_EXPERTS, LANES))

    @pl.when(phase == 0)
    def _():
        @pl.when(i == 0)
        def _():
            cnt_scr[...] = jnp.zeros_like(cnt_scr)

        logits = lt_ref[...]
        idxs, vals = [], []
        chosen = jnp.zeros(shape, F32)
        for _ in range(TOP_K):
            m = jnp.max(logits, axis=0, keepdims=True)
            idx = jnp.min(jnp.where(logits == m, row, NUM_EXPERTS), axis=0, keepdims=True)
            hit = row == idx
            idxs.append(idx.astype(F32))
            vals.append(m)
            chosen = chosen + jnp.where(hit, 1.0, 0.0)
            logits = jnp.where(hit, -jnp.inf, logits)
        topk_scr[i] = jnp.concatenate(idxs + vals, axis=0)
        cnt_scr[...] += tile_counts_of(chosen)

    @pl.when(phase == 1)
    def _():
        @pl.when(i == 0)
        def _():
            blocks = jnp.floor((cnt_scr[...] + (MOE_ROWS - 1)) * (1.0 / MOE_ROWS))
            erow = lax.broadcasted_iota(jnp.int32, (NUM_EXPERTS, LANES), 0)
            lane = lax.broadcasted_iota(jnp.int32, (NUM_EXPERTS, LANES), 1)
            cum = blocks
            shift = 1
            while shift < NUM_EXPERTS:
                cum = cum + jnp.where(erow >= shift, pltpu.roll(cum, shift, 0), 0.0)
                shift *= 2
            padend_ref[...] = jnp.where(lane == 1, cnt_scr[...], cum * MOE_ROWS)
            pstart_scr[...] = (cum - blocks) * MOE_ROWS
            base_scr[...] = jnp.zeros_like(base_scr)
            ti = lax.broadcasted_iota(jnp.int32, (ROUTE_TOKENS, ROUTE_TOKENS), 0)
            tj = lax.broadcasted_iota(jnp.int32, (ROUTE_TOKENS, ROUTE_TOKENS), 1)
            before_scr[...] = jnp.where(ti < tj, 1.0, 0.0).astype(BF16)

        topk = topk_scr[i]
        sel = [row == topk[k:k + 1, :].astype(jnp.int32) for k in range(TOP_K)]
        vals = [topk[TOP_K + k:TOP_K + k + 1, :] for k in range(TOP_K)]
        chosen = sum(jnp.where(hit, 1.0, 0.0) for hit in sel)
        earlier = _dot(chosen.astype(BF16), before_scr[...])
        pos = pstart_scr[:, 0:1] + base_scr[:, 0:1] + earlier
        dest = [jnp.sum(jnp.where(hit, pos, 0.0), axis=0, keepdims=True) for hit in sel]
        dest_all = jnp.concatenate(dest, axis=0).astype(jnp.int32)
        for c in range(ROUTE_TOKENS // DISPATCH_ROWS):
            dest_ref[c] = dest_all[:, c * DISPATCH_ROWS:(c + 1) * DISPATCH_ROWS]
        ex = [jnp.exp(v - vals[0]) for v in vals]
        denom = ex[0] + ex[1] + ex[2] + ex[3]
        gates_ref[...] = jnp.concatenate([e / denom for e in ex], axis=0)
        base_scr[...] += tile_counts_of(chosen)


def _router(logits_t):
    n = logits_t.shape[1]
    assert n % ROUTE_TOKENS == 0
    tiles = n // ROUTE_TOKENS
    return pl.pallas_call(
        _router_kernel,
        grid=(2, tiles),
        in_specs=[pl.BlockSpec((NUM_EXPERTS, ROUTE_TOKENS), lambda p, i: (0, i))],
        out_specs=(
            pl.BlockSpec((TOP_K, ROUTE_TOKENS), lambda p, i: (0, i * p)),
            pl.BlockSpec((ROUTE_TOKENS // DISPATCH_ROWS, TOP_K, DISPATCH_ROWS), lambda p, i: (i * p, 0, 0)),
            pl.BlockSpec((NUM_EXPERTS, LANES), lambda p, i: (0, 0)),
        ),
        out_shape=(
            jax.ShapeDtypeStruct((TOP_K, n), F32),
            jax.ShapeDtypeStruct((n // DISPATCH_ROWS, TOP_K, DISPATCH_ROWS), jnp.int32),
            jax.ShapeDtypeStruct((NUM_EXPERTS, LANES), F32),
        ),
        scratch_shapes=[pltpu.VMEM((NUM_EXPERTS, LANES), F32)] * 3
        + [pltpu.VMEM((ROUTE_TOKENS, ROUTE_TOKENS), BF16),
           pltpu.VMEM((tiles, 2 * TOP_K, ROUTE_TOKENS), F32)],
        compiler_params=pltpu.CompilerParams(
            dimension_semantics=("arbitrary", "arbitrary"), vmem_limit_bytes=VMEM_LIMIT),
        name="router",
    )(logits_t)


SC_CORES = 2
SC_SUBCORES = 16
SC_WORKERS = SC_CORES * SC_SUBCORES
DISPATCH_ROWS = 64


def _dispatch(h_sources, dest_chunks, m_pad):
    shares, first_chunk = [], 0
    for src in h_sources:
        n_chunks = src.shape[0] // DISPATCH_ROWS
        assert src.shape[0] % DISPATCH_ROWS == 0 and n_chunks % SC_WORKERS == 0
        shares.append((first_chunk, n_chunks // SC_WORKERS))
        first_chunk += n_chunks
    plan = [(s, first, per, j) for s, (first, per) in enumerate(shares) for j in range(per)]
    per_worker = len(plan)
    mesh = plsc.VectorSubcoreMesh(core_axis_name="c", subcore_axis_name="s")

    @functools.partial(
        pl.kernel, mesh=mesh,
        out_type=jax.ShapeDtypeStruct((m_pad, HALF), jnp.int32),
        scratch_types=[
            pltpu.VMEM((2, TOP_K, DISPATCH_ROWS), jnp.int32),
            pltpu.VMEM((2, DISPATCH_ROWS, HALF), jnp.int32),
            pltpu.SemaphoreType.DMA((2,)),
            pltpu.SemaphoreType.DMA((2,)),
        ],
        compiler_params=pltpu.CompilerParams(use_tc_tiling_on_sc=True),
        name="dispatch",
    )
    def dispatch_kernel(*refs):
        h_hbms = refs[:len(h_sources)]
        dest_hbm, out_hbm, idx_v, rows_v, load_sems, scatter_sems = refs[len(h_sources):]
        wid = lax.axis_index("s") * SC_CORES + lax.axis_index("c")

        def loads(j):
            src, first, per, k = plan[j]
            local = wid * per + k
            slot = j % 2
            return (
                pltpu.make_async_copy(dest_hbm.at[first + local], idx_v.at[slot], load_sems.at[slot]),
                pltpu.make_async_copy(h_hbms[src].at[pl.ds(local * DISPATCH_ROWS, DISPATCH_ROWS)],
                                      rows_v.at[slot], load_sems.at[slot]),
            )

        def scatters(j):
            slot = j % 2
            return [pltpu.make_async_copy(rows_v.at[slot], out_hbm.at[idx_v.at[slot, k]],
                                          scatter_sems.at[slot]) for k in range(TOP_K)]

        for cp in loads(0):
            cp.start()
        for j in range(per_worker):
            for cp in loads(j):
                cp.wait()
            if j >= 1:
                for cp in scatters(j - 1):
                    cp.wait()
            if j + 1 < per_worker:
                for cp in loads(j + 1):
                    cp.start()
            for cp in scatters(j):
                cp.start()
        for cp in scatters(per_worker - 1):
            cp.wait()

    return dispatch_kernel(*h_sources, dest_chunks)


def _gather_expert_rows(y_sorted, dest_chunks, row_offset, n):
    n_chunks = n // DISPATCH_ROWS
    assert n_chunks % SC_WORKERS == 0 and row_offset % DISPATCH_ROWS == 0
    per_worker = n_chunks // SC_WORKERS
    chunk0 = row_offset // DISPATCH_ROWS
    mesh = plsc.VectorSubcoreMesh(core_axis_name="c", subcore_axis_name="s")

    @functools.partial(
        pl.kernel, mesh=mesh,
        out_type=jax.ShapeDtypeStruct((TOP_K, n, HALF), jnp.int32),
        scratch_types=[
            pltpu.VMEM((TOP_K, DISPATCH_ROWS), jnp.int32),
            pltpu.VMEM((2, DISPATCH_ROWS, HALF), jnp.int32),
            pltpu.SemaphoreType.DMA((2,)),
        ],
        compiler_params=pltpu.CompilerParams(use_tc_tiling_on_sc=True),
        name="gather_expert_rows",
    )
    def gather_kernel(y_hbm, dest_hbm, out_hbm, idx_v, rows_v, sems):
        wid = lax.axis_index("s") * SC_CORES + lax.axis_index("c")

        def gather(k):
            return pltpu.make_async_copy(y_hbm.at[idx_v.at[k]], rows_v.at[k % 2], sems.at[k % 2])

        @pl.loop(0, per_worker)
        def _(j):
            local = wid * per_worker + j
            pltpu.sync_copy(dest_hbm.at[chunk0 + local], idx_v)
            gather(0).start()
            for k in range(TOP_K):
                if k + 1 < TOP_K:
                    gather(k + 1).start()
                gather(k).wait()
                pltpu.sync_copy(rows_v.at[k % 2],
                                out_hbm.at[k, pl.ds(local * DISPATCH_ROWS, DISPATCH_ROWS)])

    return gather_kernel(y_sorted, dest_chunks)


def _moe_kernel(be_ref, wsel_ref, units_ref, nused_ref, x_ref, wgu_ref, bgu_ref, wd_ref, bd_ref, y_ref,
                wgu_bf, wd_bf):
    del wsel_ref
    i = pl.program_id(0)

    def ffn(rows):
        gu = _dot(_unpack_bf16_pairs(x_ref[rows, :]), wgu_bf[...]) + bgu_ref[...]
        gate = jnp.minimum(gu[:, :EXPERT_FF], SWIGLU_LIMIT)
        up = jnp.clip(gu[:, EXPERT_FF:], -SWIGLU_LIMIT, SWIGLU_LIMIT)
        hmid = gate * (1.0 / (1.0 + jnp.exp(-SWIGLU_ALPHA * gate))) * (up + 1.0)
        y = _dot(hmid.astype(BF16), wd_bf[...]) + bd_ref[...]
        y_ref[rows, :] = _pack_bf16_pairs(y.astype(BF16))

    @pl.when(i < nused_ref[0])
    def _():
        @pl.when((i == 0) | (be_ref[i] != be_ref[jnp.maximum(i - 1, 0)]))
        def _():
            wgu_bf[...] = wgu_ref[...].astype(BF16)
            wd_bf[...] = wd_ref[...].astype(BF16)

        for units in range(1, MOE_ROWS // MOE_TAIL_ROWS + 1):
            @pl.when(units_ref[i] == units)
            def _(units=units):
                ffn(slice(0, units * MOE_TAIL_ROWS))


def _moe_experts(block_expert, weight_expert, block_units, n_used, x_sorted, w_gu, b_gu, w_down, b_down):
    m_pad = x_sorted.shape[0]
    n_blocks = m_pad // MOE_ROWS

    def blk(i, be, ws, hb, nu):
        return jnp.minimum(i, nu[0] - 1)

    def expert(i, be, ws, hb, nu):
        return be[blk(i, be, ws, hb, nu)]

    def held(i, be, ws, hb, nu):
        return ws[blk(i, be, ws, hb, nu)]

    grid_spec = pltpu.PrefetchScalarGridSpec(
        num_scalar_prefetch=4,
        grid=(n_blocks,),
        in_specs=[
            pl.BlockSpec((MOE_ROWS, HALF), lambda *a: (blk(*a), 0)),
            pl.BlockSpec((None, D_MODEL, 2 * EXPERT_FF), lambda *a: (held(*a), 0, 0)),
            pl.BlockSpec((None, 1, 2 * EXPERT_FF), lambda *a: (expert(*a), 0, 0)),
            pl.BlockSpec((None, EXPERT_FF, D_MODEL), lambda *a: (held(*a), 0, 0)),
            pl.BlockSpec((None, 1, D_MODEL), lambda *a: (expert(*a), 0, 0)),
        ],
        out_specs=pl.BlockSpec((MOE_ROWS, HALF), lambda *a: (blk(*a), 0)),
        scratch_shapes=[
            pltpu.VMEM((D_MODEL, 2 * EXPERT_FF), BF16),
            pltpu.VMEM((EXPERT_FF, D_MODEL), BF16),
        ],
    )
    return pl.pallas_call(
        _moe_kernel,
        grid_spec=grid_spec,
        out_shape=jax.ShapeDtypeStruct((m_pad, HALF), jnp.int32),
        compiler_params=pltpu.CompilerParams(
            dimension_semantics=("arbitrary",), vmem_limit_bytes=VMEM_LIMIT),
        name="moe_experts",
    )(block_expert, weight_expert, block_units, n_used, x_sorted, w_gu, b_gu, w_down, b_down)


def _combine_kernel(yk_ref, gates_ref, h_ref, g_ref, b_ref, out_ref):
    pad = jnp.zeros((LANES - TOP_K, COMBINE_TOKENS), F32)
    gates = jnp.transpose(jnp.concatenate([gates_ref[...], pad], axis=0))
    lo = hi = None
    for k in range(TOP_K):
        u = lax.bitcast_convert_type(yk_ref[k], jnp.uint32)
        gk = gates[:, k:k + 1]
        lo_k = lax.bitcast_convert_type(u << 16, F32) * gk
        hi_k = lax.bitcast_convert_type(u & jnp.uint32(HI_MASK), F32) * gk
        lo = lo_k if lo is None else lo + lo_k
        hi = hi_k if hi is None else hi + hi_k
    acc = ALPHA * h_ref[...] + jnp.concatenate([lo, hi], axis=1)
    out_ref[...] = _layer_norm(acc, g_ref[...], b_ref[...])


def _combine_kernel_aliased(yk_ref, gates_ref, h_ref, g_ref, b_ref, prev_ref, out_ref):
    del prev_ref
    _combine_kernel(yk_ref, gates_ref, h_ref, g_ref, b_ref, out_ref)


def _combine(yk, gates, token_offset, h_src, h_offset, ln_g, ln_b, out_prev):
    n_seg = yk.shape[1]
    out_rows = h_src.shape[0]
    tile0 = token_offset // COMBINE_TOKENS
    out_tile0 = h_offset // COMBINE_TOKENS
    in_specs = [
        pl.BlockSpec((TOP_K, COMBINE_TOKENS, HALF), lambda i: (0, i, 0)),
        pl.BlockSpec((TOP_K, COMBINE_TOKENS), lambda i: (0, tile0 + i)),
        pl.BlockSpec((COMBINE_TOKENS, D_MODEL), lambda i: (out_tile0 + i, 0)),
        _const_spec((1, D_MODEL)),
        _const_spec((1, D_MODEL)),
    ]
    args = [yk, gates, h_src, ln_g, ln_b]
    aliases = {}
    kern = _combine_kernel
    if out_prev is not None:
        in_specs.append(pl.BlockSpec(memory_space=pl.ANY))
        aliases = {len(args): 0}
        args.append(out_prev)
        kern = _combine_kernel_aliased
    return pl.pallas_call(
        kern,
        grid=(n_seg // COMBINE_TOKENS,),
        in_specs=in_specs,
        out_specs=pl.BlockSpec((COMBINE_TOKENS, D_MODEL), lambda i: (out_tile0 + i, 0)),
        out_shape=jax.ShapeDtypeStruct((out_rows, D_MODEL), F32),
        input_output_aliases=aliases,
        compiler_params=pltpu.CompilerParams(
            dimension_semantics=("arbitrary",), vmem_limit_bytes=VMEM_LIMIT),
        name="moe_combine",
    )(*args)


def kernel(x_prompt, x_sample, state_pool, state_gla, w_in, w_pool, pool_scale, w_gate_up, b_gate,
           gla_norm_w, w_out, ln1_g, ln1_b, w_router, b_router, w_gu, b_gu, w_down, b_down,
           ln2_g, ln2_b):
    assert w_in.shape[0] == 1, "single-layer kernel"
    bp, seq, _ = x_prompt.shape
    bs, dec_seq, _ = x_sample.shape
    assert dec_seq == CHUNK and seq % TILE_TOKENS == 0 and bs % CHUNKS_PER_TILE == 0
    n_prompt = bp * seq
    n_sample = bs * dec_seq
    n_total = n_prompt + n_sample
    nk = n_total * TOP_K
    n_blocks = -(-nk // MOE_ROWS) + NUM_EXPERTS
    m_pad = n_blocks * MOE_ROWS

    weights = _mixer_weights(w_in, w_pool, pool_scale, w_gate_up, b_gate, gla_norm_w, w_out,
                             ln1_g, ln1_b, w_router, b_router)

    h_p, hb_p, logits_p, hist_p, s_p = _mixer_prompt(x_prompt, weights)
    h_s, hb_s, logits_s, hist_s, s_s = _mixer_sample(x_sample, state_pool[0], state_gla[0], weights)

    gates_t, dest_t, layout = _router(jnp.concatenate([logits_p, logits_s], axis=1))
    pad_end = layout[:, 0].astype(jnp.int32)
    counts = layout[:, 1].astype(jnp.int32)
    block_start = jnp.arange(n_blocks, dtype=jnp.int32) * MOE_ROWS
    block_expert = jnp.minimum(jnp.sum((block_start[:, None] >= pad_end[None, :]).astype(jnp.int32), axis=1),
                               NUM_EXPERTS - 1)
    n_used = (pad_end[-1:] // MOE_ROWS).astype(jnp.int32)
    is_first = jnp.concatenate([jnp.ones((1,), bool), block_expert[1:] != block_expert[:-1]])
    blocks = jnp.arange(n_blocks, dtype=jnp.int32)
    later_other = ((blocks[None, :] > blocks[:, None]) & (blocks[None, :] < n_used[0])
                   & (block_expert[None, :] != block_expert[:, None]))
    next_expert = jnp.min(jnp.where(later_other, block_expert[None, :], NUM_EXPERTS), axis=1)
    next_expert = jnp.where(next_expert == NUM_EXPERTS, block_expert, next_expert)
    weight_expert = jnp.where(is_first, block_expert, next_expert).astype(jnp.int32)
    of_expert = block_expert[:, None] == jnp.arange(NUM_EXPERTS, dtype=jnp.int32)[None, :]
    seg_end = jnp.sum(jnp.where(of_expert, (pad_end - (-counts % MOE_ROWS))[None, :], 0), axis=1)
    block_units = jnp.clip((seg_end - block_start + MOE_TAIL_ROWS - 1) // MOE_TAIL_ROWS,
                          1, MOE_ROWS // MOE_TAIL_ROWS).astype(jnp.int32)

    dest_chunks = dest_t
    x_sorted = _dispatch((hb_p, hb_s), dest_chunks, m_pad)
    y_sorted = _moe_experts(block_expert, weight_expert, block_units, n_used, x_sorted, w_gu[0], b_gu[0][:, None, :],
                            w_down[0], b_down[0][:, None, :])
    gates = gates_t
    ln_g, ln_b = ln2_g[0][None, :], ln2_b[0][None, :]
    yk = _gather_expert_rows(y_sorted, dest_chunks, n_prompt, n_sample)
    y_sample = _combine(yk, gates, n_prompt, h_s, 0, ln_g, ln_b, None)
    unit = SC_WORKERS * DISPATCH_ROWS
    assert n_prompt % unit == 0
    sizes, left = [], n_prompt // unit
    while left > 0:
        size = min(left, 1 if len(sizes) < 2 else 2)
        sizes.append(size * unit)
        left -= size
    y_prompt, start = None, 0
    for seg in sizes:
        yk = _gather_expert_rows(y_sorted, dest_chunks, start, seg)
        y_prompt = _combine(yk, gates, start, h_p, start, ln_g, ln_b, y_prompt)
        start += seg
    y_prompt = y_prompt.reshape(bp, seq, D_MODEL)
    y_sample = y_sample.reshape(bs, dec_seq, D_MODEL)
    return (y_prompt, y_sample, hist_p[None], s_p[None], hist_s[None], s_s[None])
```
